```python
import math
import jax, jax.numpy as jnp
from jax import lax
import numpy as np

D_MODEL = 1024
BATCH = 32
SEQ = 2048
DEPTH = 1

C_CONV = 1024
CONV_WIDTH = 31
HEAD_DIM = 64
N_Q_HEADS = 16
N_KV_HEADS = 2
GROUP = N_Q_HEADS // N_KV_HEADS
WINDOW = 128
BLOCK = 128
D_FF = int(math.ceil((8 * D_MODEL / 3) / 256) * 256)
EPS = 1e-6
NEG = -1e30

Q_W = N_Q_HEADS * HEAD_DIM
KV_W = N_KV_HEADS * HEAD_DIM
IN_COLS = 2 * C_CONV + Q_W + 2 * KV_W + 2 * D_MODEL

kernel_name = "hybrid_conformer_conv_swa_sink_alibi_block"


def rms_norm(x, g):
    xf = x.astype(jnp.float32)
    y = xf * lax.rsqrt(jnp.mean(xf * xf, axis=-1, keepdims=True) + EPS)
    return (y * g.astype(jnp.float32)).astype(x.dtype)


def layer_norm(x, g, b):
    xf = x.astype(jnp.float32)
    mu = jnp.mean(xf, axis=-1, keepdims=True)
    var = jnp.mean(jnp.square(xf - mu), axis=-1, keepdims=True)
    y = (xf - mu) * lax.rsqrt(var + EPS)
    return (y * g.astype(jnp.float32) + b.astype(jnp.float32)).astype(x.dtype)


def conv_module(u, dw_w, dw_b, ln_g, ln_b, w_out):
    a, gate = jnp.split(u, 2, axis=-1)
    h = a * jax.nn.sigmoid(gate)
    h = lax.conv_general_dilated(
        h, dw_w, window_strides=(1,), padding=[(CONV_WIDTH - 1, 0)],
        dimension_numbers=('NWC', 'WIO', 'NWC'),
        feature_group_count=C_CONV) + dw_b
    h = jax.nn.silu(layer_norm(h, ln_g, ln_b))
    return h @ w_out


def alibi_slopes(n_heads):
    h = jnp.arange(1, n_heads + 1, dtype=jnp.float32)
    return jnp.exp2(-8.0 * h / n_heads)


def swa_attention(q, k, v, q_g, k_g, sinks):
    B, S = q.shape[0], q.shape[1]
    nb = S // BLOCK
    q = rms_norm(q, q_g)
    k = rms_norm(k, k_g)
    qb = q.reshape(B, nb, BLOCK, N_KV_HEADS, GROUP, HEAD_DIM)
    pad = ((0, 0), (BLOCK, 0), (0, 0), (0, 0))
    kb = jnp.pad(k, pad).reshape(B, nb + 1, BLOCK, N_KV_HEADS, HEAD_DIM)
    vb = jnp.pad(v, pad).reshape(B, nb + 1, BLOCK, N_KV_HEADS, HEAD_DIM)
    kwin = jnp.concatenate([kb[:, :-1], kb[:, 1:]], axis=2)
    vwin = jnp.concatenate([vb[:, :-1], vb[:, 1:]], axis=2)

    scale = 1.0 / math.sqrt(HEAD_DIM)
    s = jnp.einsum('bnqkgd,bnskd->bnkgqs', qb, kwin).astype(jnp.float32) * scale

    qi = jnp.arange(BLOCK)[:, None]
    sj = jnp.arange(2 * BLOCK)[None, :]
    dist = qi + BLOCK - sj
    s_glob = jnp.arange(nb)[:, None, None] * BLOCK - BLOCK + sj
    valid = (dist >= 0) & (dist < WINDOW) & (s_glob >= 0)
    slopes = alibi_slopes(N_Q_HEADS).reshape(N_KV_HEADS, GROUP, 1, 1)
    bias = -slopes * dist.astype(jnp.float32)
    s = jnp.where(valid[None, :, None, None], s + bias, NEG)

    sink = sinks.astype(jnp.float32).reshape(N_KV_HEADS, GROUP, 1, 1)
    m = jnp.maximum(jnp.max(s, axis=-1, keepdims=True), sink)
    p = jnp.exp(s - m)
    p = p / (jnp.sum(p, axis=-1, keepdims=True) + jnp.exp(sink - m))
    o = jnp.einsum('bnkgqs,bnskd->bnqkgd', p.astype(v.dtype), vwin)
    return o.reshape(B, S, N_Q_HEADS * HEAD_DIM)


def _fwd_setup_inputs(seed: int = 0) -> dict:
    key = jax.random.key(seed)
    ks = jax.random.split(key, 17)
    f32 = jnp.float32
    nrm = lambda k, shape, s: jax.random.normal(k, shape, f32) * s
    return {
        "x": nrm(ks[0], (BATCH, SEQ, D_MODEL), 1.0),
        "norm_mix_g": 1.0 + nrm(ks[1], (D_MODEL,), 0.02),
        "w_in": nrm(ks[2], (D_MODEL, IN_COLS), D_MODEL ** -0.5),
        "conv_dw_w": nrm(ks[3], (CONV_WIDTH, 1, C_CONV), CONV_WIDTH ** -0.5),
        "conv_dw_b": nrm(ks[4], (C_CONV,), 0.02),
        "conv_ln_g": 1.0 + nrm(ks[5], (C_CONV,), 0.02),
        "conv_ln_b": nrm(ks[6], (C_CONV,), 0.02),
        "w_conv_out": nrm(ks[7], (C_CONV, D_MODEL), C_CONV ** -0.5),
        "q_norm_g": 1.0 + nrm(ks[8], (HEAD_DIM,), 0.02),
        "k_norm_g": 1.0 + nrm(ks[9], (HEAD_DIM,), 0.02),
        "sinks": nrm(ks[10], (N_Q_HEADS,), 0.5),
        "w_attn_out": nrm(ks[11], (Q_W, D_MODEL), Q_W ** -0.5),
        "w_merge_out": nrm(ks[12], (D_MODEL, D_MODEL), D_MODEL ** -0.5),
        "norm_ffn_g": 1.0 + nrm(ks[13], (D_MODEL,), 0.02),
        "w_ffn_in": nrm(ks[14], (D_MODEL, 2 * D_FF), D_MODEL ** -0.5),
        "w_ffn_down": nrm(ks[15], (D_FF, D_MODEL), D_FF ** -0.5),
    }


def _fwd_reference(x, norm_mix_g, w_in, conv_dw_w, conv_dw_b, conv_ln_g, conv_ln_b,
              w_conv_out, q_norm_g, k_norm_g, sinks, w_attn_out, w_merge_out,
              norm_ffn_g, w_ffn_in, w_ffn_down):
    B, S, _ = x.shape
    h = x
    for _layer in range(DEPTH):
        xn = rms_norm(h, norm_mix_g)
        u = xn @ w_in
        o1 = 2 * C_CONV
        o2 = o1 + Q_W
        o3 = o2 + KV_W
        o4 = o3 + KV_W
        o5 = o4 + D_MODEL
        conv_in = u[..., :o1]
        q = u[..., o1:o2].reshape(B, S, N_Q_HEADS, HEAD_DIM)
        k = u[..., o2:o3].reshape(B, S, N_KV_HEADS, HEAD_DIM)
        v = u[..., o3:o4].reshape(B, S, N_KV_HEADS, HEAD_DIM)
        g_conv = jax.nn.sigmoid(u[..., o4:o5])
        g_attn = jax.nn.sigmoid(u[..., o5:])

        y_conv = conv_module(conv_in, conv_dw_w, conv_dw_b, conv_ln_g, conv_ln_b, w_conv_out)
        y_attn = swa_attention(q, k, v, q_norm_g, k_norm_g, sinks) @ w_attn_out
        h = h + (g_conv * y_conv + g_attn * y_attn) @ w_merge_out

        hn = rms_norm(h, norm_ffn_g)
        gate, up = jnp.split(hn @ w_ffn_in, 2, axis=-1)
        h = h + (jax.nn.silu(gate) * up) @ w_ffn_down
    return h


import jax as _jax
import jax.numpy as _jnp

TWIN_FORMAT = 'train_step'
FWD_PARAMS = ['x', 'norm_mix_g', 'w_in', 'conv_dw_w', 'conv_dw_b', 'conv_ln_g', 'conv_ln_b', 'w_conv_out', 'q_norm_g', 'k_norm_g', 'sinks', 'w_attn_out', 'w_merge_out', 'norm_ffn_g', 'w_ffn_in', 'w_ffn_down']
TWIN_WEIGHTS = ['norm_mix_g', 'w_in', 'conv_dw_w', 'conv_dw_b', 'conv_ln_g', 'conv_ln_b', 'w_conv_out', 'q_norm_g', 'k_norm_g', 'sinks', 'w_attn_out', 'w_merge_out', 'norm_ffn_g', 'w_ffn_in', 'w_ffn_down']
TWIN_DIFF_INPUT = 'x'
TWIN_INPUTS = ['x', 'norm_mix_g', 'w_in', 'conv_dw_w', 'conv_dw_b', 'conv_ln_g', 'conv_ln_b', 'w_conv_out', 'q_norm_g', 'k_norm_g', 'sinks', 'w_attn_out', 'w_merge_out', 'norm_ffn_g', 'w_ffn_in', 'w_ffn_down', 'loss_target', 'm_norm_mix_g', 'm_w_in', 'm_conv_dw_w', 'm_conv_dw_b', 'm_conv_ln_g', 'm_conv_ln_b', 'm_w_conv_out', 'm_q_norm_g', 'm_k_norm_g', 'm_sinks', 'm_w_attn_out', 'm_w_merge_out', 'm_norm_ffn_g', 'm_w_ffn_in', 'm_w_ffn_down', 'v_norm_mix_g', 'v_w_in', 'v_conv_dw_w', 'v_conv_dw_b', 'v_conv_ln_g', 'v_conv_ln_b', 'v_w_conv_out', 'v_q_norm_g', 'v_k_norm_g', 'v_sinks', 'v_w_attn_out', 'v_w_merge_out', 'v_norm_ffn_g', 'v_w_ffn_in', 'v_w_ffn_down']
TWIN_OUTPUTS = ['loss', 'grad_x', 'grad_norm_mix_g', 'grad_w_in', 'grad_conv_dw_w', 'grad_conv_dw_b', 'grad_conv_ln_g', 'grad_conv_ln_b', 'grad_w_conv_out', 'grad_q_norm_g', 'grad_k_norm_g', 'grad_sinks', 'grad_w_attn_out', 'grad_w_merge_out', 'grad_norm_ffn_g', 'grad_w_ffn_in', 'grad_w_ffn_down', 'delta_norm_mix_g', 'delta_w_in', 'delta_conv_dw_w', 'delta_conv_dw_b', 'delta_conv_ln_g', 'delta_conv_ln_b', 'delta_w_conv_out', 'delta_q_norm_g', 'delta_k_norm_g', 'delta_sinks', 'delta_w_attn_out', 'delta_w_merge_out', 'delta_norm_ffn_g', 'delta_w_ffn_in', 'delta_w_ffn_down', 'new_m_norm_mix_g', 'new_m_w_in', 'new_m_conv_dw_w', 'new_m_conv_dw_b', 'new_m_conv_ln_g', 'new_m_conv_ln_b', 'new_m_w_conv_out', 'new_m_q_norm_g', 'new_m_k_norm_g', 'new_m_sinks', 'new_m_w_attn_out', 'new_m_w_merge_out', 'new_m_norm_ffn_g', 'new_m_w_ffn_in', 'new_m_w_ffn_down', 'new_v_norm_mix_g', 'new_v_w_in', 'new_v_conv_dw_w', 'new_v_conv_dw_b', 'new_v_conv_ln_g', 'new_v_conv_ln_b', 'new_v_w_conv_out', 'new_v_q_norm_g', 'new_v_k_norm_g', 'new_v_sinks', 'new_v_w_attn_out', 'new_v_w_merge_out', 'new_v_norm_ffn_g', 'new_v_w_ffn_in', 'new_v_w_ffn_down']
TWIN_LEAF_KINDS = {'loss': 'loss', 'grad_x': 'grad_x', 'grad_norm_mix_g': 'grad_w', 'grad_w_in': 'grad_w', 'grad_conv_dw_w': 'grad_w', 'grad_conv_dw_b': 'grad_w', 'grad_conv_ln_g': 'grad_w', 'grad_conv_ln_b': 'grad_w', 'grad_w_conv_out': 'grad_w', 'grad_q_norm_g': 'grad_w', 'grad_k_norm_g': 'grad_w', 'grad_sinks': 'grad_w', 'grad_w_attn_out': 'grad_w', 'grad_w_merge_out': 'grad_w', 'grad_norm_ffn_g': 'grad_w', 'grad_w_ffn_in': 'grad_w', 'grad_w_ffn_down': 'grad_w', 'delta_norm_mix_g': 'delta_w', 'delta_w_in': 'delta_w', 'delta_conv_dw_w': 'delta_w', 'delta_conv_dw_b': 'delta_w', 'delta_conv_ln_g': 'delta_w', 'delta_conv_ln_b': 'delta_w', 'delta_w_conv_out': 'delta_w', 'delta_q_norm_g': 'delta_w', 'delta_k_norm_g': 'delta_w', 'delta_sinks': 'delta_w', 'delta_w_attn_out': 'delta_w', 'delta_w_merge_out': 'delta_w', 'delta_norm_ffn_g': 'delta_w', 'delta_w_ffn_in': 'delta_w', 'delta_w_ffn_down': 'delta_w', 'new_m_norm_mix_g': 'new_m', 'new_m_w_in': 'new_m', 'new_m_conv_dw_w': 'new_m', 'new_m_conv_dw_b': 'new_m', 'new_m_conv_ln_g': 'new_m', 'new_m_conv_ln_b': 'new_m', 'new_m_w_conv_out': 'new_m', 'new_m_q_norm_g': 'new_m', 'new_m_k_norm_g': 'new_m', 'new_m_sinks': 'new_m', 'new_m_w_attn_out': 'new_m', 'new_m_w_merge_out': 'new_m', 'new_m_norm_ffn_g': 'new_m', 'new_m_w_ffn_in': 'new_m', 'new_m_w_ffn_down': 'new_m', 'new_v_norm_mix_g': 'new_v', 'new_v_w_in': 'new_v', 'new_v_conv_dw_w': 'new_v', 'new_v_conv_dw_b': 'new_v', 'new_v_conv_ln_g': 'new_v', 'new_v_conv_ln_b': 'new_v', 'new_v_w_conv_out': 'new_v', 'new_v_q_norm_g': 'new_v', 'new_v_k_norm_g': 'new_v', 'new_v_sinks': 'new_v', 'new_v_w_attn_out': 'new_v', 'new_v_w_merge_out': 'new_v', 'new_v_norm_ffn_g': 'new_v', 'new_v_w_ffn_in': 'new_v', 'new_v_w_ffn_down': 'new_v'}


def _forward(args):
    return _fwd_reference(*[args[k] for k in FWD_PARAMS])


def _output_shape():
    out = _jax.eval_shape(lambda: _forward(_fwd_setup_inputs(0)))
    return out.shape, out.dtype

N_MICROBATCH = 1
ADAM_LR = 0.001
ADAM_B1 = 0.9
ADAM_B2 = 0.999
ADAM_EPS = 1e-08
ADAM_WD = 0.01
ADAM_STEP = 10
PER_EXAMPLE_BATCH_AXIS = {'x': 0, 'loss_target': 0}
SHARED_INPUTS = []
_WEIGHT_DTYPES = {'norm_mix_g': _jnp.float32, 'w_in': _jnp.float32, 'conv_dw_w': _jnp.float32, 'conv_dw_b': _jnp.float32, 'conv_ln_g': _jnp.float32, 'conv_ln_b': _jnp.float32, 'w_conv_out': _jnp.float32, 'q_norm_g': _jnp.float32, 'k_norm_g': _jnp.float32, 'sinks': _jnp.float32, 'w_attn_out': _jnp.float32, 'w_merge_out': _jnp.float32, 'norm_ffn_g': _jnp.float32, 'w_ffn_in': _jnp.float32, 'w_ffn_down': _jnp.float32}
MOMENT_SCALE = {'norm_mix_g': 1.814407e+00, 'w_in': 1.221506e-01, 'conv_dw_w': 2.922741e-01, 'conv_dw_b': 5.700703e+00, 'conv_ln_g': 8.356487e+00, 'conv_ln_b': 6.248615e+00, 'w_conv_out': 1.183696e+00, 'q_norm_g': 1.034311e+01, 'k_norm_g': 1.037011e+01, 'sinks': 2.288760e+01, 'w_attn_out': 1.197912e-01, 'w_merge_out': 9.302136e-01, 'norm_ffn_g': 4.919325e+01, 'w_ffn_in': 3.322081e-01, 'w_ffn_down': 3.874518e-01}


def _to_microbatches(a, axis):
    t = _jnp.moveaxis(a, axis, 0)
    t = t.reshape((N_MICROBATCH, t.shape[0] // N_MICROBATCH) + t.shape[1:])
    return _jnp.moveaxis(t, 1, axis + 1)


def setup_inputs(seed: int = 0) -> dict:
    inp = _fwd_setup_inputs(seed)
    key = _jax.random.fold_in(_jax.random.key(seed), 7919)
    shape, _ = _output_shape()
    out = dict(inp)
    out["loss_target"] = _jax.random.normal(_jax.random.fold_in(key, 0), shape, _jnp.float32)
    for i, name in enumerate(TWIN_WEIGHTS):
        w = inp[name].astype(_jnp.float32)
        if MOMENT_SCALE is None:
            s = _jnp.sqrt(_jnp.mean(_jnp.square(w)) + 1e-30)
        else:
            s = MOMENT_SCALE[name]
        km, kv = _jax.random.split(_jax.random.fold_in(key, i + 1))
        out[name] = w
        out["m_" + name] = s * _jax.random.normal(km, w.shape, _jnp.float32)
        out["v_" + name] = (s * s) * _jax.random.uniform(kv, w.shape, _jnp.float32, 0.5, 1.5)
    if N_MICROBATCH > 1:
        for name, axis in PER_EXAMPLE_BATCH_AXIS.items():
            out[name] = _to_microbatches(out[name], axis)
    return {'x': out['x'], 'norm_mix_g': out['norm_mix_g'], 'w_in': out['w_in'], 'conv_dw_w': out['conv_dw_w'], 'conv_dw_b': out['conv_dw_b'], 'conv_ln_g': out['conv_ln_g'], 'conv_ln_b': out['conv_ln_b'], 'w_conv_out': out['w_conv_out'], 'q_norm_g': out['q_norm_g'], 'k_norm_g': out['k_norm_g'], 'sinks': out['sinks'], 'w_attn_out': out['w_attn_out'], 'w_merge_out': out['w_merge_out'], 'norm_ffn_g': out['norm_ffn_g'], 'w_ffn_in': out['w_ffn_in'], 'w_ffn_down': out['w_ffn_down'], 'loss_target': out['loss_target'], 'm_norm_mix_g': out['m_norm_mix_g'], 'm_w_in': out['m_w_in'], 'm_conv_dw_w': out['m_conv_dw_w'], 'm_conv_dw_b': out['m_conv_dw_b'], 'm_conv_ln_g': out['m_conv_ln_g'], 'm_conv_ln_b': out['m_conv_ln_b'], 'm_w_conv_out': out['m_w_conv_out'], 'm_q_norm_g': out['m_q_norm_g'], 'm_k_norm_g': out['m_k_norm_g'], 'm_sinks': out['m_sinks'], 'm_w_attn_out': out['m_w_attn_out'], 'm_w_merge_out': out['m_w_merge_out'], 'm_norm_ffn_g': out['m_norm_ffn_g'], 'm_w_ffn_in': out['m_w_ffn_in'], 'm_w_ffn_down': out['m_w_ffn_down'], 'v_norm_mix_g': out['v_norm_mix_g'], 'v_w_in': out['v_w_in'], 'v_conv_dw_w': out['v_conv_dw_w'], 'v_conv_dw_b': out['v_conv_dw_b'], 'v_conv_ln_g': out['v_conv_ln_g'], 'v_conv_ln_b': out['v_conv_ln_b'], 'v_w_conv_out': out['v_w_conv_out'], 'v_q_norm_g': out['v_q_norm_g'], 'v_k_norm_g': out['v_k_norm_g'], 'v_sinks': out['v_sinks'], 'v_w_attn_out': out['v_w_attn_out'], 'v_w_merge_out': out['v_w_merge_out'], 'v_norm_ffn_g': out['v_norm_ffn_g'], 'v_w_ffn_in': out['v_w_ffn_in'], 'v_w_ffn_down': out['v_w_ffn_down']}


def _loss(weights, diff, rest, loss_target):
    with _jax.named_scope("forward"):
        args = {**rest, TWIN_DIFF_INPUT: diff, **{k: w.astype(_WEIGHT_DTYPES[k]) for k, w in weights.items()}}
        y = _forward(args)
    with _jax.named_scope("loss_head"):
        err = _jnp.square(y.astype(_jnp.float32) - loss_target)
        return 0.5 * _jnp.sum(_jnp.mean(err, axis=-1)) if err.ndim else 0.5 * err


def _adamw(w, g, m, v):
    m = ADAM_B1 * m + (1.0 - ADAM_B1) * g
    v = ADAM_B2 * v + (1.0 - ADAM_B2) * _jnp.square(g)
    m_hat = m / (1.0 - ADAM_B1 ** ADAM_STEP)
    v_hat = v / (1.0 - ADAM_B2 ** ADAM_STEP)
    delta = -ADAM_LR * (m_hat / (_jnp.sqrt(v_hat) + ADAM_EPS) + ADAM_WD * w)
    return delta, m, v


def reference(x, norm_mix_g, w_in, conv_dw_w, conv_dw_b, conv_ln_g, conv_ln_b, w_conv_out, q_norm_g, k_norm_g, sinks, w_attn_out, w_merge_out, norm_ffn_g, w_ffn_in, w_ffn_down, loss_target, m_norm_mix_g, m_w_in, m_conv_dw_w, m_conv_dw_b, m_conv_ln_g, m_conv_ln_b, m_w_conv_out, m_q_norm_g, m_k_norm_g, m_sinks, m_w_attn_out, m_w_merge_out, m_norm_ffn_g, m_w_ffn_in, m_w_ffn_down, v_norm_mix_g, v_w_in, v_conv_dw_w, v_conv_dw_b, v_conv_ln_g, v_conv_ln_b, v_w_conv_out, v_q_norm_g, v_k_norm_g, v_sinks, v_w_attn_out, v_w_merge_out, v_norm_ffn_g, v_w_ffn_in, v_w_ffn_down):
    given = dict(x=x, norm_mix_g=norm_mix_g, w_in=w_in, conv_dw_w=conv_dw_w, conv_dw_b=conv_dw_b, conv_ln_g=conv_ln_g, conv_ln_b=conv_ln_b, w_conv_out=w_conv_out, q_norm_g=q_norm_g, k_norm_g=k_norm_g, sinks=sinks, w_attn_out=w_attn_out, w_merge_out=w_merge_out, norm_ffn_g=norm_ffn_g, w_ffn_in=w_ffn_in, w_ffn_down=w_ffn_down, loss_target=loss_target, m_norm_mix_g=m_norm_mix_g, m_w_in=m_w_in, m_conv_dw_w=m_conv_dw_w, m_conv_dw_b=m_conv_dw_b, m_conv_ln_g=m_conv_ln_g, m_conv_ln_b=m_conv_ln_b, m_w_conv_out=m_w_conv_out, m_q_norm_g=m_q_norm_g, m_k_norm_g=m_k_norm_g, m_sinks=m_sinks, m_w_attn_out=m_w_attn_out, m_w_merge_out=m_w_merge_out, m_norm_ffn_g=m_norm_ffn_g, m_w_ffn_in=m_w_ffn_in, m_w_ffn_down=m_w_ffn_down, v_norm_mix_g=v_norm_mix_g, v_w_in=v_w_in, v_conv_dw_w=v_conv_dw_w, v_conv_dw_b=v_conv_dw_b, v_conv_ln_g=v_conv_ln_g, v_conv_ln_b=v_conv_ln_b, v_w_conv_out=v_w_conv_out, v_q_norm_g=v_q_norm_g, v_k_norm_g=v_k_norm_g, v_sinks=v_sinks, v_w_attn_out=v_w_attn_out, v_w_merge_out=v_w_merge_out, v_norm_ffn_g=v_norm_ffn_g, v_w_ffn_in=v_w_ffn_in, v_w_ffn_down=v_w_ffn_down)
    weights = {n: given[n] for n in TWIN_WEIGHTS}
    shared = {n: given[n] for n in SHARED_INPUTS}
    per_example = {n: given[n] for n in ['x']}
    grad_fn = _jax.value_and_grad(_loss, argnums=(0, 1))

    def one_microbatch(ex, loss_target):
        ex = dict(ex)
        diff = ex.pop(TWIN_DIFF_INPUT)
        return grad_fn(weights, diff, {**shared, **ex}, loss_target)

    if N_MICROBATCH == 1:
        loss, (grad_w, grad_x) = one_microbatch(per_example, given["loss_target"])
    else:
        def body(carry, xs):
            loss_sum, grad_sum = carry
            l_k, (gw_k, gx_k) = one_microbatch(xs[0], xs[1])
            with _jax.named_scope("update"):
                return (loss_sum + l_k, _jax.tree.map(_jnp.add, grad_sum, gw_k)), gx_k

        init = (_jnp.zeros((), _jnp.float32), _jax.tree.map(_jnp.zeros_like, weights))
        (loss, grad_w), grad_x = _jax.lax.scan(body, init, (per_example, given["loss_target"]))
    with _jax.named_scope("update"):
        delta_w, new_m, new_v = {}, {}, {}
        for n in TWIN_WEIGHTS:
            delta_w[n], new_m[n], new_v[n] = _adamw(weights[n], grad_w[n], given["m_" + n], given["v_" + n])
    return (loss, grad_x, *[grad_w[n] for n in TWIN_WEIGHTS], *[delta_w[n] for n in TWIN_WEIGHTS],
            *[new_m[n] for n in TWIN_WEIGHTS], *[new_v[n] for n in TWIN_WEIGHTS])
```

```python
import functools
import math

import jax
import jax.numpy as jnp
from jax import lax
from jax.experimental import pallas as pl
from jax.experimental.pallas import tpu as pltpu

F32 = jnp.float32
BF16 = jnp.bfloat16
MESH = pl.DeviceIdType.MESH

N_DEV = 8
D_MODEL = 1024
C_CONV = 1024
CONV_WIDTH = 31
HEAD_DIM = 64
N_Q_HEADS = 16
N_KV_HEADS = 2
GROUP = N_Q_HEADS // N_KV_HEADS
WINDOW = 128
BLOCK = 128
D_FF = 2816
EPS = 1e-6
NEG = -1e30
ATTN_SCALE = 1.0 / math.sqrt(HEAD_DIM)
Q_W = N_Q_HEADS * HEAD_DIM
KV_W = N_KV_HEADS * HEAD_DIM
IN_COLS = 2 * C_CONV + Q_W + 2 * KV_W + 2 * D_MODEL

ADAM_LR = 0.001
ADAM_B1 = 0.9
ADAM_B2 = 0.999
ADAM_EPS = 1e-08
ADAM_WD = 0.01
ADAM_STEP = 10

COL_A, COL_GATE, COL_GC, COL_GA, COL_Q, COL_K, COL_V = 0, 1024, 2048, 3072, 4096, 5120, 5248

HALO = 32
VMEM_LIMIT = 56 * 1024 * 1024


def _cp(*sem):
    return pltpu.CompilerParams(dimension_semantics=sem, vmem_limit_bytes=VMEM_LIMIT)


def _sigmoid(x):
    return jax.nn.sigmoid(x)


def _dot(a, b):
    return jnp.dot(a, b, preferred_element_type=F32)


def _dot_nt(a, b):
    return lax.dot_general(a, b, (((1,), (1,)), ((), ())), preferred_element_type=F32)


def _dot_tn(a, b):
    return lax.dot_general(a, b, (((0,), (0,)), ((), ())), preferred_element_type=F32)


def _full(shape):
    return pl.BlockSpec(shape, lambda *_: (0,) * len(shape))


def _rows(tm, ncols, colblk=0):
    return pl.BlockSpec((tm, ncols), lambda i: (i, colblk))


def _inproj_call(x, g, w, tm=256):
    T = x.shape[0]
    N = w.shape[1]

    def body(x_ref, g_ref, w_ref, u_ref, xn_ref):
        xv = x_ref[...]
        r = lax.rsqrt(jnp.mean(xv * xv, axis=-1, keepdims=True) + EPS)
        xn = (xv * r * g_ref[...]).astype(BF16)
        xn_ref[...] = xn
        u_ref[...] = _dot(xn, w_ref[...])

    return pl.pallas_call(
        body, name="inproj", grid=(T // tm,),
        in_specs=[_rows(tm, D_MODEL), _full((1, D_MODEL)), _full(w.shape)],
        out_specs=[_rows(tm, N), _rows(tm, D_MODEL)],
        out_shape=[jax.ShapeDtypeStruct((T, N), F32), jax.ShapeDtypeStruct((T, D_MODEL), BF16)],
        compiler_params=_cp("parallel"),
    )(x, g, w)


def _conv_taps(w_ref, src_ref, base, rows, cb, reverse):
    cols = pl.ds(pl.multiple_of(cb * 128, 128), 128)
    outs = []
    for rb in range(rows // 128):
        acc = jnp.zeros((128, 128), F32)
        for j in range(CONV_WIDTH):
            off = base + (CONV_WIDTH - 1 - j if reverse else j) + rb * 128
            acc = acc + w_ref[j:j + 1, cols] * src_ref[pl.ds(off, 128), cols]
        outs.append(acc)
    return outs


def _conv_fwd_call(u, conv_w, conv_b, ln_g, ln_b, B, S, ts=256):
    T = B * S
    nS = S // ts
    per32 = ts // HALO

    def body(a_ref, gate_ref, ap_ref, gp_ref, w_ref, b_ref, g_ref, bb_ref, hc_ref, c_ref, hext, acc_ref):
        s = pl.program_id(1)
        prev = ap_ref[...] * _sigmoid(gp_ref[...])
        hext[0:HALO, :] = jnp.where(s > 0, prev, 0.0)
        hext[HALO:, :] = a_ref[...] * _sigmoid(gate_ref[...])

        def colblock(cb, carry):
            cols = pl.ds(pl.multiple_of(cb * 128, 128), 128)
            outs = _conv_taps(w_ref, hext, HALO - (CONV_WIDTH - 1), ts, cb, reverse=False)
            for rb, acc in enumerate(outs):
                acc_ref[rb * 128:(rb + 1) * 128, cols] = acc
            return carry

        lax.fori_loop(0, C_CONV // 128, colblock, 0)
        hc = acc_ref[...] + b_ref[...]
        hc_ref[...] = hc
        mu = jnp.mean(hc, axis=-1, keepdims=True)
        xc = hc - mu
        var = jnp.mean(xc * xc, axis=-1, keepdims=True)
        y = xc * lax.rsqrt(var + EPS) * g_ref[...] + bb_ref[...]
        c_ref[...] = (y * _sigmoid(y)).astype(BF16)

    cur = lambda cb: pl.BlockSpec((ts, C_CONV), lambda b, s: (b * nS + s, cb))
    prv = lambda cb: pl.BlockSpec((HALO, C_CONV), lambda b, s: (jnp.maximum(b * (S // HALO) + s * per32 - 1, 0), cb))
    vec = _full((1, C_CONV))
    return pl.pallas_call(
        body, name="conv_fwd", grid=(B, nS),
        in_specs=[cur(0), cur(1), prv(0), prv(1), _full((CONV_WIDTH, C_CONV)), vec, vec, vec],
        out_specs=[pl.BlockSpec((ts, C_CONV), lambda b, s: (b * nS + s, 0))] * 2,
        out_shape=[jax.ShapeDtypeStruct((T, C_CONV), F32), jax.ShapeDtypeStruct((T, C_CONV), BF16)],
        scratch_shapes=[pltpu.VMEM((HALO + ts, C_CONV), F32), pltpu.VMEM((ts, C_CONV), F32)],
        compiler_params=_cp("parallel", "arbitrary"),
    )(u, u, u, u, conv_w, conv_b, ln_g, ln_b)


def _lane_lo():
    return lax.broadcasted_iota(jnp.int32, (1, 128), 1) < HEAD_DIM


def _half_sums(t, lo):
    s_lo = jnp.sum(jnp.where(lo, t, 0.0), axis=-1, keepdims=True)
    s_hi = jnp.sum(jnp.where(lo, 0.0, t), axis=-1, keepdims=True)
    return jnp.where(lo, s_lo, s_hi)


def _head_rstd(t, lo):
    return lax.rsqrt(_half_sums(t * t, lo) * (1.0 / HEAD_DIM) + EPS)


def _band_masks():
    qi = lax.broadcasted_iota(jnp.int32, (BLOCK, 2 * BLOCK), 0)
    sj = lax.broadcasted_iota(jnp.int32, (BLOCK, 2 * BLOCK), 1)
    dist = qi + BLOCK - sj
    valid = (dist >= 0) & (dist < WINDOW)
    return dist.astype(F32), valid, sj >= BLOCK


def _stage_kv(kvh, k_ref, v_ref, kg_ref, lo, kA, kB, vA, vB):
    kv = k_ref[...]
    rk = _head_rstd(kv, lo)
    kn = kv * rk * kg_ref[...]
    mine = lax.broadcasted_iota(jnp.int32, (1, 128), 1) // HEAD_DIM == kvh
    ksel = jnp.where(mine, kn, 0.0)
    vsel = jnp.where(mine, v_ref[...], 0.0)
    krol = pltpu.roll(ksel, HEAD_DIM, 1)
    vrol = pltpu.roll(vsel, HEAD_DIM, 1)
    first = kvh == 0
    zeros = jnp.zeros((BLOCK, 128), BF16)
    for ref, val in ((kA, jnp.where(first, ksel, krol)), (kB, jnp.where(first, krol, ksel)),
                     (vA, jnp.where(first, vsel, vrol)), (vB, jnp.where(first, vrol, vsel))):
        ref[0:BLOCK, :] = zeros
        ref[BLOCK:, :] = val.astype(BF16)


def _pair_softmax(s, kvh, p, n, sinks_ref, slopes_ref, dist, valid, cur):
    ps, psinks = [], []
    for hf in range(2):
        h = kvh * GROUP + 2 * p + hf
        sh = s[:, 2 * BLOCK * hf:2 * BLOCK * (hf + 1)]
        ok = valid & ((n > 0) | cur)
        sh = jnp.where(ok, sh - slopes_ref[h] * dist, NEG)
        sk = sinks_ref[h]
        m = jnp.maximum(jnp.max(sh, axis=-1, keepdims=True), sk)
        e = jnp.exp(sh - m)
        esk = jnp.exp(sk - m)
        rz = 1.0 / (jnp.sum(e, axis=-1, keepdims=True) + esk)
        ps.append(e * rz)
        psinks.append(esk * rz)
    return ps, psinks


def _attn_fwd_call(u, qg2, kg2, sinks, slopes, B, S):
    T = B * S
    nb = S // BLOCK
    half = Q_W // 2

    def body(sinks_ref, slopes_ref, q_ref, k_ref, v_ref, qg_ref, kg_ref, o_ref, kA, kB, vA, vB):
        kvh = pl.program_id(1)
        lo = _lane_lo()
        _stage_kv(kvh, k_ref, v_ref, kg_ref, lo, kA, kB, vA, vB)
        dist, valid, cur = _band_masks()

        def blk(n, carry):
            r0 = pl.multiple_of(n * BLOCK, BLOCK)
            win = pl.ds(r0, 2 * BLOCK)
            k2t = jnp.concatenate([kA[win, :], kB[win, :]], axis=0)
            v2 = jnp.concatenate([vA[win, :], vB[win, :]], axis=0)
            for p in range(GROUP // 2):
                qp = q_ref[pl.ds(r0, BLOCK), 128 * p:128 * (p + 1)]
                qn = (qp * _head_rstd(qp, lo) * qg_ref[...]).astype(BF16)
                s = _dot_nt(qn, k2t) * ATTN_SCALE
                ps, _ = _pair_softmax(s, kvh, p, n, sinks_ref, slopes_ref, dist, valid, cur)
                pb = jnp.concatenate(ps, axis=1).astype(BF16)
                o_ref[pl.ds(r0, BLOCK), 128 * p:128 * (p + 1)] = _dot(pb, v2).astype(BF16)
            return carry

        lax.fori_loop(0, nb, blk, 0)

    smem = pl.BlockSpec(memory_space=pltpu.SMEM)
    kvbuf = pltpu.VMEM((S + BLOCK, 128), BF16)
    return pl.pallas_call(
        body, name="attn_fwd", grid=(B, N_KV_HEADS),
        in_specs=[smem, smem,
                  pl.BlockSpec((S, half), lambda b, h: (b, COL_Q // half + h)),
                  pl.BlockSpec((S, 128), lambda b, h: (b, COL_K // 128)),
                  pl.BlockSpec((S, 128), lambda b, h: (b, COL_V // 128)),
                  _full((1, 128)), _full((1, 128))],
        out_specs=pl.BlockSpec((S, half), lambda b, h: (b, h)),
        out_shape=jax.ShapeDtypeStruct((T, Q_W), BF16),
        scratch_shapes=[kvbuf, kvbuf, kvbuf, kvbuf],
        compiler_params=_cp("parallel", "arbitrary"),
    )(sinks, slopes, u, u, u, qg2, kg2)


def _merge_call(c, o, u, x, wc, wa, wm, g_ffn, tm=256):
    T = x.shape[0]

    def body(c_ref, o_ref, gc_ref, ga_ref, x_ref, wc_ref, wa_ref, wm_ref, g_ref, yc_ref, ya_ref, z_ref, h_ref, hn_ref):
        yc = _dot(c_ref[...], wc_ref[...])
        ya = _dot(o_ref[...], wa_ref[...])
        yc_ref[...] = yc
        ya_ref[...] = ya
        z = (_sigmoid(gc_ref[...]) * yc + _sigmoid(ga_ref[...]) * ya).astype(BF16)
        z_ref[...] = z
        h = x_ref[...] + _dot(z, wm_ref[...])
        h_ref[...] = h
        r = lax.rsqrt(jnp.mean(h * h, axis=-1, keepdims=True) + EPS)
        hn_ref[...] = (h * r * g_ref[...]).astype(BF16)

    sq = _full((D_MODEL, D_MODEL))
    row = _rows(tm, D_MODEL)
    f32 = jax.ShapeDtypeStruct((T, D_MODEL), F32)
    b16 = jax.ShapeDtypeStruct((T, D_MODEL), BF16)
    return pl.pallas_call(
        body, name="merge_fwd", grid=(T // tm,),
        in_specs=[row, row, _rows(tm, D_MODEL, COL_GC // D_MODEL), _rows(tm, D_MODEL, COL_GA // D_MODEL), row,
                  sq, sq, sq, _full((1, D_MODEL))],
        out_specs=[row] * 5,
        out_shape=[f32, f32, b16, f32, b16],
        compiler_params=_cp("parallel"),
    )(c, o, u, u, x, wc, wa, wm, g_ffn)


def _ffn_in_call(hn, w, tm=256):
    T = hn.shape[0]

    def body(hn_ref, w_ref, gu_ref, act_ref):
        gu = _dot(hn_ref[...], w_ref[...])
        gu_ref[...] = gu
        g = gu[:, :D_FF]
        act_ref[...] = (g * _sigmoid(g) * gu[:, D_FF:]).astype(BF16)

    return pl.pallas_call(
        body, name="ffn_in", grid=(T // tm,),
        in_specs=[_rows(tm, D_MODEL), _full(w.shape)],
        out_specs=[_rows(tm, 2 * D_FF), _rows(tm, D_FF)],
        out_shape=[jax.ShapeDtypeStruct((T, 2 * D_FF), F32), jax.ShapeDtypeStruct((T, D_FF), BF16)],
        compiler_params=_cp("parallel"),
    )(hn, w)


def _ffn_out_call(act, wd, h, tgt, tm=512):
    T = h.shape[0]

    def body(act_ref, wd_ref, h_ref, t_ref, sse_ref, d_ref, db_ref):
        e = h_ref[...] + _dot(act_ref[...], wd_ref[...]) - t_ref[...]

        @pl.when(pl.program_id(0) == 0)
        def _():
            sse_ref[...] = jnp.zeros_like(sse_ref)

        sse_ref[...] += jnp.sum(e * e)
        d = e * (1.0 / D_MODEL)
        d_ref[...] = d
        db_ref[...] = d.astype(BF16)

    row = _rows(tm, D_MODEL)
    return pl.pallas_call(
        body, name="ffn_out", grid=(T // tm,),
        in_specs=[_rows(tm, D_FF), _full(wd.shape), row, row],
        out_specs=[_full((8, 128)), row, row],
        out_shape=[jax.ShapeDtypeStruct((8, 128), F32), jax.ShapeDtypeStruct((T, D_MODEL), F32),
                   jax.ShapeDtypeStruct((T, D_MODEL), BF16)],
        compiler_params=_cp("arbitrary"),
    )(act, wd, h, tgt)


def _dw_call(name, a, b, tn, tk=1024):
    T, K = a.shape
    N = b.shape[1]
    tk = min(tk, T)
    nk = T // tk

    def body(a_ref, b_ref, o_ref, acc_ref):
        k = pl.program_id(1)

        @pl.when(k == 0)
        def _():
            acc_ref[...] = jnp.zeros_like(acc_ref)

        acc_ref[...] += _dot_tn(a_ref[...], b_ref[...])

        @pl.when(k == nk - 1)
        def _():
            o_ref[...] = acc_ref[...].astype(BF16)

    return pl.pallas_call(
        body, name=name, grid=(N // tn, nk),
        in_specs=[pl.BlockSpec((tk, K), lambda j, k: (k, 0)), pl.BlockSpec((tk, tn), lambda j, k: (k, j))],
        out_specs=pl.BlockSpec((K, tn), lambda j, k: (0, j)),
        out_shape=jax.ShapeDtypeStruct((K, N), BF16),
        scratch_shapes=[pltpu.VMEM((K, tn), F32)],
        compiler_params=_cp("parallel", "arbitrary"),
    )(a, b)


def _rms_bwd(dy, xv, g):
    r = lax.rsqrt(jnp.mean(xv * xv, axis=-1, keepdims=True) + EPS)
    t = dy * g
    dx = r * t - xv * (r * r * r * jnp.mean(t * xv, axis=-1, keepdims=True))
    return dx, dy * xv * r


def _ffn_down_bwd_call(dout_b, wd, gu, tm=256):
    T = dout_b.shape[0]

    def body(d_ref, wd_ref, g_ref, up_ref, o_ref):
        dact = _dot_nt(d_ref[...], wd_ref[...])
        g = g_ref[...]
        sg = _sigmoid(g)
        o_ref[:, :D_FF] = (dact * up_ref[...] * (sg * (1.0 + g * (1.0 - sg)))).astype(BF16)
        o_ref[:, D_FF:] = (dact * (g * sg)).astype(BF16)

    return pl.pallas_call(
        body, name="ffn_down_bwd", grid=(T // tm,),
        in_specs=[_rows(tm, D_MODEL), _full(wd.shape), _rows(tm, D_FF, 0), _rows(tm, D_FF, 1)],
        out_specs=_rows(tm, 2 * D_FF),
        out_shape=jax.ShapeDtypeStruct((T, 2 * D_FF), BF16),
        compiler_params=_cp("parallel"),
    )(dout_b, wd, gu, gu)


def _ffn_in_bwd_call(dgu, w, h, g_ffn, dout, tm=256):
    T = h.shape[0]

    def body(dgu_ref, w_ref, h_ref, g_ref, d_ref, dh_ref, dhb_ref, dg_ref):
        dhn = _dot_nt(dgu_ref[...], w_ref[...])
        dx, dgt = _rms_bwd(dhn, h_ref[...], g_ref[...])
        dh = d_ref[...] + dx
        dh_ref[...] = dh
        dhb_ref[...] = dh.astype(BF16)

        @pl.when(pl.program_id(0) == 0)
        def _():
            dg_ref[...] = jnp.zeros_like(dg_ref)

        dg_ref[...] += jnp.sum(dgt, axis=0, keepdims=True)

    row = _rows(tm, D_MODEL)
    return pl.pallas_call(
        body, name="ffn_in_bwd", grid=(T // tm,),
        in_specs=[_rows(tm, 2 * D_FF), _full(w.shape), row, _full((1, D_MODEL)), row],
        out_specs=[row, row, _full((1, D_MODEL))],
        out_shape=[jax.ShapeDtypeStruct((T, D_MODEL), F32), jax.ShapeDtypeStruct((T, D_MODEL), BF16),
                   jax.ShapeDtypeStruct((1, D_MODEL), F32)],
        compiler_params=_cp("arbitrary"),
    )(dgu, w, h, g_ffn, dout)


def _merge_bwd_call(dh_b, wm, wc, wa, u, yc, ya, tm=256):
    T = dh_b.shape[0]

    def body(dh_ref, wm_ref, wc_ref, wa_ref, gc_ref, ga_ref, yc_ref, ya_ref, dyc_ref, dya_ref, dug_ref, dcp_ref, do_ref):
        dz = _dot_nt(dh_ref[...], wm_ref[...])
        sgc = _sigmoid(gc_ref[...])
        sga = _sigmoid(ga_ref[...])
        dyc = (dz * sgc).astype(BF16)
        dya = (dz * sga).astype(BF16)
        dyc_ref[...] = dyc
        dya_ref[...] = dya
        dug_ref[:, :D_MODEL] = (dz * yc_ref[...] * (sgc * (1.0 - sgc))).astype(BF16)
        dug_ref[:, D_MODEL:] = (dz * ya_ref[...] * (sga * (1.0 - sga))).astype(BF16)
        dcp_ref[...] = _dot_nt(dyc, wc_ref[...])
        do_ref[...] = _dot_nt(dya, wa_ref[...]).astype(BF16)

    sq = _full((D_MODEL, D_MODEL))
    row = _rows(tm, D_MODEL)
    b16 = jax.ShapeDtypeStruct((T, D_MODEL), BF16)
    return pl.pallas_call(
        body, name="merge_bwd", grid=(T // tm,),
        in_specs=[row, sq, sq, sq, _rows(tm, D_MODEL, COL_GC // D_MODEL), _rows(tm, D_MODEL, COL_GA // D_MODEL), row, row],
        out_specs=[row, row, _rows(tm, 2 * D_MODEL), row, row],
        out_shape=[b16, b16, jax.ShapeDtypeStruct((T, 2 * D_MODEL), BF16), jax.ShapeDtypeStruct((T, D_MODEL), F32), b16],
        compiler_params=_cp("parallel"),
    )(dh_b, wm, wc, wa, u, u, yc, ya)


def _conv_bwd_call(dcp, hc, u, conv_w, ln_g, ln_b, B, S, ts=256):
    T = B * S
    nS = S // ts
    per32 = ts // HALO
    last32 = T // HALO - 1

    def ln_bwd(dcp_v, hc_v, g, bb):
        mu = jnp.mean(hc_v, axis=-1, keepdims=True)
        xc = hc_v - mu
        rstd = lax.rsqrt(jnp.mean(xc * xc, axis=-1, keepdims=True) + EPS)
        xhat = xc * rstd
        y = xhat * g + bb
        sy = _sigmoid(y)
        dy = dcp_v * (sy * (1.0 + y * (1.0 - sy)))
        dxh = dy * g
        dhc = rstd * (dxh - jnp.mean(dxh, axis=-1, keepdims=True) - xhat * jnp.mean(dxh * xhat, axis=-1, keepdims=True))
        return dhc, dy * xhat, dy

    def body(dcp_ref, dcpn_ref, hc_ref, hcn_ref, a_ref, gate_ref, ap_ref, gp_ref, w_ref, g_ref, bb_ref,
             du_ref, dw_ref, db_ref, dlg_ref, dlb_ref, dhext, hgext, acc_ref):
        b = pl.program_id(0)
        s = pl.program_id(1)

        @pl.when((b == 0) & (s == 0))
        def _():
            dw_ref[...] = jnp.zeros_like(dw_ref)
            db_ref[...] = jnp.zeros_like(db_ref)
            dlg_ref[...] = jnp.zeros_like(dlg_ref)
            dlb_ref[...] = jnp.zeros_like(dlb_ref)

        dhc, dgt, dbt = ln_bwd(dcp_ref[...], hc_ref[...], g_ref[...], bb_ref[...])
        dhn, _, _ = ln_bwd(dcpn_ref[...], hcn_ref[...], g_ref[...], bb_ref[...])
        dhext[0:ts, :] = dhc
        dhext[ts:, :] = jnp.where(s < nS - 1, dhn, 0.0)
        db_ref[...] += jnp.sum(dhc, axis=0, keepdims=True)
        dlg_ref[...] += jnp.sum(dgt, axis=0, keepdims=True)
        dlb_ref[...] += jnp.sum(dbt, axis=0, keepdims=True)

        sg = _sigmoid(gate_ref[...])
        av = a_ref[...]
        hgext[0:HALO, :] = jnp.where(s > 0, ap_ref[...] * _sigmoid(gp_ref[...]), 0.0)
        hgext[HALO:, :] = av * sg

        def colblock(cb, carry):
            cols = pl.ds(pl.multiple_of(cb * 128, 128), 128)
            outs = _conv_taps(w_ref, dhext, 0, ts, cb, reverse=True)
            for rb, acc in enumerate(outs):
                acc_ref[rb * 128:(rb + 1) * 128, cols] = acc
            for j in range(CONV_WIDTH):
                part = jnp.zeros((8, 128), F32)
                for rb in range(ts // 128):
                    prod = dhext[rb * 128:(rb + 1) * 128, cols] * hgext[pl.ds(HALO - (CONV_WIDTH - 1) + j + rb * 128, 128), cols]
                    part = part + jnp.sum(prod.reshape(16, 8, 128), axis=0)
                dw_ref[8 * j:8 * (j + 1), cols] += part
            return carry

        lax.fori_loop(0, C_CONV // 128, colblock, 0)
        dglu = acc_ref[...]
        du_ref[:, :C_CONV] = (dglu * sg).astype(BF16)
        du_ref[:, C_CONV:] = (dglu * av * (sg * (1.0 - sg))).astype(BF16)

    cur = lambda cb: pl.BlockSpec((ts, C_CONV), lambda b, s: (b * nS + s, cb))
    prv = lambda cb: pl.BlockSpec((HALO, C_CONV), lambda b, s: (jnp.maximum(b * (S // HALO) + s * per32 - 1, 0), cb))
    nxt = pl.BlockSpec((HALO, C_CONV), lambda b, s: (jnp.minimum(b * (S // HALO) + (s + 1) * per32, last32), 0))
    vec = _full((1, C_CONV))
    return pl.pallas_call(
        body, name="conv_bwd", grid=(B, nS),
        in_specs=[cur(0), nxt, cur(0), nxt, cur(0), cur(1), prv(0), prv(1), _full((CONV_WIDTH, C_CONV)), vec, vec],
        out_specs=[pl.BlockSpec((ts, 2 * C_CONV), lambda b, s: (b * nS + s, 0)), _full((8 * CONV_WIDTH, C_CONV)), vec, vec, vec],
        out_shape=[jax.ShapeDtypeStruct((T, 2 * C_CONV), BF16), jax.ShapeDtypeStruct((8 * CONV_WIDTH, C_CONV), F32),
                   jax.ShapeDtypeStruct((1, C_CONV), F32), jax.ShapeDtypeStruct((1, C_CONV), F32),
                   jax.ShapeDtypeStruct((1, C_CONV), F32)],
        scratch_shapes=[pltpu.VMEM((ts + HALO, C_CONV), F32), pltpu.VMEM((HALO + ts, C_CONV), F32),
                        pltpu.VMEM((ts, C_CONV), F32)],
        compiler_params=_cp("arbitrary", "arbitrary"),
    )(dcp, dcp, hc, hc, u, u, u, u, conv_w, ln_g, ln_b)


def _attn_bwd_call(u, do, qg2, kg2, sinks, slopes, B, S):
    T = B * S
    nb = S // BLOCK
    half = Q_W // 2

    def body(sinks_ref, slopes_ref, q_ref, k_ref, v_ref, do_ref, qg_ref, kg_ref,
             dq_ref, dkv_ref, dqg_ref, dkg_ref, dsk_ref, kA, kB, vA, vB, dkn_acc, dv_acc, dkv_keep):
        b = pl.program_id(0)
        kvh = pl.program_id(1)
        lo = _lane_lo()
        lane = lax.broadcasted_iota(jnp.int32, (1, 128), 1)
        first = kvh == 0

        @pl.when((b == 0) & first)
        def _():
            dqg_ref[...] = jnp.zeros_like(dqg_ref)
            dkg_ref[...] = jnp.zeros_like(dkg_ref)
            dsk_ref[...] = jnp.zeros_like(dsk_ref)

        _stage_kv(kvh, k_ref, v_ref, kg_ref, lo, kA, kB, vA, vB)
        dkn_acc[...] = jnp.zeros_like(dkn_acc)
        dv_acc[...] = jnp.zeros_like(dv_acc)
        dist, valid, cur = _band_masks()

        def fold(acc):
            canon = jnp.where(lo, acc[0:2 * BLOCK] + pltpu.roll(acc[2 * BLOCK:], HEAD_DIM, 1), 0.0)
            return jnp.where(first, canon, pltpu.roll(canon, HEAD_DIM, 1))

        def blk(n, carry):
            r0 = pl.multiple_of(n * BLOCK, BLOCK)
            win = pl.ds(r0, 2 * BLOCK)
            rows = pl.ds(r0, BLOCK)
            k2t = jnp.concatenate([kA[win, :], kB[win, :]], axis=0)
            v2 = jnp.concatenate([vA[win, :], vB[win, :]], axis=0)
            dk2 = jnp.zeros((4 * BLOCK, 128), F32)
            dv2 = jnp.zeros((4 * BLOCK, 128), F32)
            for p in range(GROUP // 2):
                pc = slice(128 * p, 128 * (p + 1))
                qp = q_ref[rows, pc]
                rq = _head_rstd(qp, lo)
                qg = qg_ref[...]
                qn = (qp * rq * qg).astype(BF16)
                s = _dot_nt(qn, k2t) * ATTN_SCALE
                ps, psinks = _pair_softmax(s, kvh, p, n, sinks_ref, slopes_ref, dist, valid, cur)
                dop = do_ref[rows, pc]
                dp = _dot_nt(dop, v2)
                dss = []
                for hf in range(2):
                    dph = dp[:, 2 * BLOCK * hf:2 * BLOCK * (hf + 1)]
                    delta = jnp.sum(ps[hf] * dph, axis=-1, keepdims=True)
                    dss.append(ps[hf] * (dph - delta) * ATTN_SCALE)
                    h = kvh * GROUP + 2 * p + hf
                    dsk_ref[...] += jnp.where(lane == h, -jnp.sum(psinks[hf] * delta), 0.0)
                dsb = jnp.concatenate(dss, axis=1).astype(BF16)
                pb = jnp.concatenate(ps, axis=1).astype(BF16)
                dqn = _dot(dsb, k2t)
                dk2 = dk2 + _dot_tn(dsb, qn)
                dv2 = dv2 + _dot_tn(pb, dop)
                t = dqn * qg
                dq = rq * t - qp * (rq * rq * rq * (_half_sums(t * qp, lo) * (1.0 / HEAD_DIM)))
                dq_ref[rows, pc] = dq.astype(BF16)
                dqg_ref[...] += jnp.sum(dqn * qp * rq, axis=0, keepdims=True)
            dkn_acc[win, :] += fold(dk2)
            dv_acc[win, :] += fold(dv2)
            return carry

        lax.fori_loop(0, nb, blk, 0)

        dkn = dkn_acc[BLOCK:, :]
        kv = k_ref[...]
        kg = kg_ref[...]
        rk = _head_rstd(kv, lo)
        t = dkn * kg
        dk = rk * t - kv * (rk * rk * rk * (_half_sums(t * kv, lo) * (1.0 / HEAD_DIM)))
        dkg_ref[...] += jnp.sum(dkn * kv * rk, axis=0, keepdims=True)
        dv = dv_acc[BLOCK:, :]

        @pl.when(first)
        def _():
            dkv_keep[:, :128] = dk
            dkv_keep[:, 128:] = dv

        @pl.when(kvh == N_KV_HEADS - 1)
        def _():
            dkv_ref[:, :128] = (dkv_keep[:, :128] + dk).astype(BF16)
            dkv_ref[:, 128:] = (dkv_keep[:, 128:] + dv).astype(BF16)

    smem = pl.BlockSpec(memory_space=pltpu.SMEM)
    kvbuf = pltpu.VMEM((S + BLOCK, 128), BF16)
    accbuf = pltpu.VMEM((S + BLOCK, 128), F32)
    vec = _full((1, 128))
    return pl.pallas_call(
        body, name="attn_bwd", grid=(B, N_KV_HEADS),
        in_specs=[smem, smem,
                  pl.BlockSpec((S, half), lambda b, h: (b, COL_Q // half + h)),
                  pl.BlockSpec((S, 128), lambda b, h: (b, COL_K // 128)),
                  pl.BlockSpec((S, 128), lambda b, h: (b, COL_V // 128)),
                  pl.BlockSpec((S, half), lambda b, h: (b, h)),
                  vec, vec],
        out_specs=[pl.BlockSpec((S, half), lambda b, h: (b, h)), pl.BlockSpec((S, 256), lambda b, h: (b, 0)), vec, vec, vec],
        out_shape=[jax.ShapeDtypeStruct((T, Q_W), BF16), jax.ShapeDtypeStruct((T, 2 * KV_W), BF16),
                   jax.ShapeDtypeStruct((1, 128), F32), jax.ShapeDtypeStruct((1, 128), F32), jax.ShapeDtypeStruct((1, 128), F32)],
        scratch_shapes=[kvbuf, kvbuf, kvbuf, kvbuf, accbuf, accbuf, pltpu.VMEM((S, 256), F32)],
        compiler_params=_cp("arbitrary", "arbitrary"),
    )(sinks, slopes, u, u, u, do, qg2, kg2)


def _inproj_bwd_call(du_conv, du_g, du_q, du_kv, w, x, g, dh, tm=256):
    T = x.shape[0]

    def body(dc_ref, dg_ref, dq_ref, dkv_ref, w_ref, x_ref, g_ref, dh_ref, gx_ref, dgm_ref):
        dxn = _dot_nt(dc_ref[...], w_ref[:, COL_A:COL_GC])
        dxn += _dot_nt(dg_ref[...], w_ref[:, COL_GC:COL_Q])
        dxn += _dot_nt(dq_ref[...], w_ref[:, COL_Q:COL_K])
        dxn += _dot_nt(dkv_ref[...], w_ref[:, COL_K:])
        dx, dgt = _rms_bwd(dxn, x_ref[...], g_ref[...])
        gx_ref[...] = dh_ref[...] + dx

        @pl.when(pl.program_id(0) == 0)
        def _():
            dgm_ref[...] = jnp.zeros_like(dgm_ref)

        dgm_ref[...] += jnp.sum(dgt, axis=0, keepdims=True)

    row = _rows(tm, D_MODEL)
    return pl.pallas_call(
        body, name="inproj_bwd", grid=(T // tm,),
        in_specs=[_rows(tm, 2 * C_CONV), _rows(tm, 2 * D_MODEL), _rows(tm, Q_W), _rows(tm, 2 * KV_W), _full(w.shape),
                  row, _full((1, D_MODEL)), row],
        out_specs=[row, _full((1, D_MODEL))],
        out_shape=[jax.ShapeDtypeStruct((T, D_MODEL), F32), jax.ShapeDtypeStruct((1, D_MODEL), F32)],
        compiler_params=_cp("arbitrary"),
    )(du_conv, du_g, du_q, du_kv, w, x, g, dh)


def _adamw_math(w, g, m, v):
    m = ADAM_B1 * m + (1.0 - ADAM_B1) * g
    v = ADAM_B2 * v + (1.0 - ADAM_B2) * (g * g)
    m_hat = m / (1.0 - ADAM_B1 ** ADAM_STEP)
    v_hat = v / (1.0 - ADAM_B2 ** ADAM_STEP)
    delta = -ADAM_LR * (m_hat / (jnp.sqrt(v_hat) + ADAM_EPS) + ADAM_WD * w)
    return delta, m, v


def _adamw_sum_call(name, parts, w, m, v, tr):
    R, C = w.shape

    def body(p_ref, w_ref, m_ref, v_ref, g_ref, d_ref, nm_ref, nv_ref):
        g = p_ref[0].astype(F32)
        for s in range(1, N_DEV):
            g = g + p_ref[s].astype(F32)
        g_ref[...] = g
        d_ref[...], nm_ref[...], nv_ref[...] = _adamw_math(w_ref[...], g, m_ref[...], v_ref[...])

    blk = pl.BlockSpec((tr, C), lambda i: (i, 0))
    out = jax.ShapeDtypeStruct((R, C), F32)
    return pl.pallas_call(
        body, name=name, grid=(R // tr,),
        in_specs=[pl.BlockSpec((N_DEV, tr, C), lambda i: (0, i, 0)), blk, blk, blk],
        out_specs=[blk] * 4, out_shape=[out] * 4,
        compiler_params=_cp("parallel"),
    )(parts, w, m, v)


def _adamw_small_call(name, g, w, m, v):
    def body(g_ref, w_ref, m_ref, v_ref, d_ref, nm_ref, nv_ref):
        d_ref[...], nm_ref[...], nv_ref[...] = _adamw_math(w_ref[...], g_ref[...], m_ref[...], v_ref[...])

    out = jax.ShapeDtypeStruct(w.shape, F32)
    return pl.pallas_call(body, name=name, out_shape=[out] * 3)(g, w, m, v)


def _position():
    return lax.axis_index("x"), lax.axis_index("y"), lax.axis_index("c")


def _peer(x, y, c, rel):
    fx, fy, fc = (rel >> 2) & 1, (rel >> 1) & 1, rel & 1
    px = 1 - x if fx else x
    py = 1 - y if fy else y
    pc = 1 - c if fc else c
    return (px, py, pc), 4 * px + 2 * py + pc


def _gather_weights_call(shards):
    n = len(shards)

    def body(*refs):
        ins, outs = refs[:n], refs[n:2 * n]
        send_sems, recv_sems, local_sems = refs[2 * n:]
        x, y, c = _position()
        me = 4 * x + 2 * y + c
        sib, sib_id = _peer(x, y, c, 1)
        chips = [_peer(x, y, c, rel) for rel in (4, 2, 6)]

        def copy(i, k, src, slot, to):
            return pltpu.make_async_remote_copy(
                src_ref=src, dst_ref=outs[i].at[slot], send_sem=send_sems.at[i, k], recv_sem=recv_sems.at[i, k],
                device_id=to, device_id_type=MESH)

        mine = [pltpu.make_async_copy(ins[i], outs[i].at[me], local_sems.at[i]) for i in range(n)]
        for cp in mine:
            cp.start()
        first = []
        for i in range(n):
            first.append(copy(i, 0, ins[i], me, sib))
            first += [copy(i, 1 + j, ins[i], me, dev) for j, (dev, _) in enumerate(chips)]
        for cp in first:
            cp.start()
        passed = []
        for i in range(n):
            for j, (dev, slot) in enumerate(chips):
                copy(i, 1 + j, ins[i], slot, dev).wait_recv()
                fwd = copy(i, 4 + j, outs[i].at[slot], slot, sib)
                fwd.start()
                passed.append(fwd)
        for i in range(n):
            copy(i, 0, ins[i], sib_id, sib).wait_recv()
            for j, (_, slot) in enumerate(chips):
                copy(i, 4 + j, ins[i], slot ^ 1, sib).wait_recv()
        for cp in first + passed:
            cp.wait_send()
        for cp in mine:
            cp.wait()

    hbm = pl.BlockSpec(memory_space=pl.ANY)
    return pl.pallas_call(
        body, name="gather_weights",
        in_specs=[hbm] * n, out_specs=[hbm] * n,
        out_shape=[jax.ShapeDtypeStruct((N_DEV,) + s.shape, s.dtype) for s in shards],
        scratch_shapes=[pltpu.SemaphoreType.DMA((n, 7)), pltpu.SemaphoreType.DMA((n, 7)), pltpu.SemaphoreType.DMA((n,))],
    )(*shards)


def _scatter_grads_call(grads):
    n = len(grads)

    def body(*refs):
        ins, outs = refs[:n], refs[n:2 * n]
        send_sems, recv_sems, local_sems = refs[2 * n:]
        x, y, c = _position()
        me = 4 * x + 2 * y + c
        peers = [_peer(x, y, c, rel) for rel in range(1, N_DEV)]

        def copy(i, k, dev, dev_id):
            return pltpu.make_async_remote_copy(
                src_ref=ins[i].at[dev_id], dst_ref=outs[i].at[me], send_sem=send_sems.at[i, k], recv_sem=recv_sems.at[i, k],
                device_id=dev, device_id_type=MESH)

        def arrival(i, k, dev, dev_id):
            return pltpu.make_async_remote_copy(
                src_ref=ins[i].at[dev_id], dst_ref=outs[i].at[dev_id], send_sem=send_sems.at[i, k], recv_sem=recv_sems.at[i, k],
                device_id=dev, device_id_type=MESH)

        mine = [pltpu.make_async_copy(ins[i].at[me], outs[i].at[me], local_sems.at[i]) for i in range(n)]
        for cp in mine:
            cp.start()
        sends = [copy(i, k, dev, dev_id) for i in range(n) for k, (dev, dev_id) in enumerate(peers)]
        for cp in sends:
            cp.start()
        for i in range(n):
            for k, (dev, dev_id) in enumerate(peers):
                arrival(i, k, dev, dev_id).wait_recv()
        for cp in sends:
            cp.wait_send()
        for cp in mine:
            cp.wait()

    hbm = pl.BlockSpec(memory_space=pl.ANY)
    return pl.pallas_call(
        body, name="scatter_grads",
        in_specs=[hbm] * n, out_specs=[hbm] * n,
        out_shape=[jax.ShapeDtypeStruct(g.shape, g.dtype) for g in grads],
        scratch_shapes=[pltpu.SemaphoreType.DMA((n, 7)), pltpu.SemaphoreType.DMA((n, 7)), pltpu.SemaphoreType.DMA((n,))],
    )(*grads)


def _allsum_small_call(part):
    P = part.shape[0]

    def body(p_ref, o_ref, all_ref, send_sems, recv_sems):
        x, y, c = _position()
        me = 4 * x + 2 * y + c
        peers = [_peer(x, y, c, rel) for rel in range(1, N_DEV)]
        all_ref[me] = p_ref[...]

        def copy(k, slot, dev):
            return pltpu.make_async_remote_copy(
                src_ref=p_ref, dst_ref=all_ref.at[slot], send_sem=send_sems.at[k], recv_sem=recv_sems.at[k],
                device_id=dev, device_id_type=MESH)

        sends = [copy(k, me, dev) for k, (dev, _) in enumerate(peers)]
        for cp in sends:
            cp.start()
        for k, (dev, dev_id) in enumerate(peers):
            copy(k, dev_id, dev).wait_recv()
        for cp in sends:
            cp.wait_send()
        tot = all_ref[0]
        for s in range(1, N_DEV):
            tot = tot + all_ref[s]
        o_ref[...] = tot

    vm = pl.BlockSpec(memory_space=pltpu.VMEM)
    return pl.pallas_call(
        body, name="allsum_small", in_specs=[vm], out_specs=vm,
        out_shape=jax.ShapeDtypeStruct(part.shape, F32),
        scratch_shapes=[pltpu.VMEM((N_DEV, P, 128), F32), pltpu.SemaphoreType.DMA((7,)), pltpu.SemaphoreType.DMA((7,))],
    )(part)


def _pad_rows(v, rows):
    flat = v.reshape(-1)
    return jnp.pad(flat, (0, rows * 128 - flat.shape[0])).reshape(rows, 128)


def _blocks_by_cols(w):
    K, N = w.shape
    return w.reshape(K, N_DEV, N // N_DEV).transpose(1, 0, 2)


def _cols_from_blocks(wb):
    n, K, C = wb.shape
    return wb.transpose(1, 0, 2).reshape(K, n * C)


def _local_step(x, tgt, w_in_i, conv_w, conv_b, ln_g, ln_b, wc, qg, kg, sinks, wa, wm, g_mix, g_ffn, w_ffn_in, wd, B, S):
    row = lambda v: v.reshape(1, -1)
    qg2 = jnp.tile(qg, 2).reshape(1, 128)
    kg2 = jnp.tile(kg, 2).reshape(1, 128)
    heads = jnp.arange(1, N_Q_HEADS + 1, dtype=F32)
    slopes = jnp.exp2(-8.0 * heads / N_Q_HEADS)

    u, xn = _inproj_call(x, row(g_mix), w_in_i)
    hc, c = _conv_fwd_call(u, conv_w, row(conv_b), row(ln_g), row(ln_b), B, S)
    o = _attn_fwd_call(u, qg2, kg2, sinks, slopes, B, S)
    yc, ya, z, h, hn = _merge_call(c, o, u, x, wc, wa, wm, row(g_ffn))
    gu, act = _ffn_in_call(hn, w_ffn_in)
    sse, dout, dout_b = _ffn_out_call(act, wd, h, tgt)

    dgu = _ffn_down_bwd_call(dout_b, wd, gu)
    g_wd = _dw_call("dw_ffn_down", act, dout_b, tn=512)
    g_wffn = _dw_call("dw_ffn_in", hn, dgu, tn=2 * D_FF // 4)
    dh, dh_b, g_gffn = _ffn_in_bwd_call(dgu, w_ffn_in, h, row(g_ffn), dout)
    dyc, dya, du_g, dcp, do = _merge_bwd_call(dh_b, wm, wc, wa, u, yc, ya)
    g_wm = _dw_call("dw_merge", z, dh_b, tn=D_MODEL)
    g_wc = _dw_call("dw_conv_out", c, dyc, tn=D_MODEL)
    g_wa = _dw_call("dw_attn_out", o, dya, tn=D_MODEL)
    du_conv, g_cw8, g_cb, g_lg, g_lb = _conv_bwd_call(dcp, hc, u, conv_w, row(ln_g), row(ln_b), B, S)
    du_q, du_kv, g_qg2, g_kg2, g_sk = _attn_bwd_call(u, do, qg2, kg2, sinks, slopes, B, S)
    g_win_i = jnp.concatenate([
        _dw_call("dw_in_conv", xn, du_conv, tn=D_MODEL),
        _dw_call("dw_in_gates", xn, du_g, tn=D_MODEL),
        _dw_call("dw_in_q", xn, du_q, tn=D_MODEL),
        _dw_call("dw_in_kv", xn, du_kv, tn=2 * KV_W)], axis=1)
    grad_x, g_gmix = _inproj_bwd_call(du_conv, du_g, du_q, du_kv, w_in_i, x, row(g_mix), dh)

    g_cw = g_cw8.reshape(CONV_WIDTH, 8, C_CONV).sum(axis=1)
    g_qg = g_qg2[0, :HEAD_DIM] + g_qg2[0, HEAD_DIM:]
    g_kg = g_kg2[0, :HEAD_DIM] + g_kg2[0, HEAD_DIM:]
    small = dict(norm_mix_g=g_gmix[0], conv_dw_b=g_cb[0], conv_ln_g=g_lg[0], conv_ln_b=g_lb[0], q_norm_g=g_qg,
                 k_norm_g=g_kg, sinks=g_sk[0, :N_Q_HEADS], norm_ffn_g=g_gffn[0], conv_dw_w=g_cw)
    big = dict(w_in=g_win_i, w_conv_out=g_wc, w_attn_out=g_wa, w_merge_out=g_wm, w_ffn_in=g_wffn, w_ffn_down=g_wd)
    return sse[0, 0], grad_x, big, small


def _to_internal_cols(w):
    o1 = 2 * C_CONV
    o4 = o1 + Q_W + 2 * KV_W
    return jnp.concatenate([w[:, :o1], w[:, o4:], w[:, o1:o4]], axis=1)


def _from_internal_cols(w):
    return jnp.concatenate([w[:, :COL_GC], w[:, COL_Q:], w[:, COL_GC:COL_Q]], axis=1)


_SMALL = (("norm_mix_g", 8), ("conv_dw_b", 8), ("conv_ln_g", 8), ("conv_ln_b", 8), ("norm_ffn_g", 8),
          ("q_norm_g", 1), ("k_norm_g", 1), ("sinks", 1))
_SMALL_ROWS = 48
_CW_ROWS = CONV_WIDTH * C_CONV // 128


def _pack_small(d):
    rows = [_pad_rows(d[name], r) for name, r in _SMALL]
    used = sum(r for _, r in _SMALL)
    return jnp.concatenate(rows + [jnp.zeros((_SMALL_ROWS - used, 128), F32)], axis=0)


def _unpack_small(packed, like):
    out, r0 = {}, 0
    for name, r in _SMALL:
        n = like[name].size
        out[name] = packed[r0:r0 + r].reshape(-1)[:n].reshape(like[name].shape)
        r0 += r
    return out


def kernel(x, norm_mix_g, w_in, conv_dw_w, conv_dw_b, conv_ln_g, conv_ln_b, w_conv_out, q_norm_g, k_norm_g, sinks, w_attn_out, w_merge_out, norm_ffn_g, w_ffn_in, w_ffn_down, loss_target, m_norm_mix_g, m_w_in, m_conv_dw_w, m_conv_dw_b, m_conv_ln_g, m_conv_ln_b, m_w_conv_out, m_q_norm_g, m_k_norm_g, m_sinks, m_w_attn_out, m_w_merge_out, m_norm_ffn_g, m_w_ffn_in, m_w_ffn_down, v_norm_mix_g, v_w_in, v_conv_dw_w, v_conv_dw_b, v_conv_ln_g, v_conv_ln_b, v_w_conv_out, v_q_norm_g, v_k_norm_g, v_sinks, v_w_attn_out, v_w_merge_out, v_norm_ffn_g, v_w_ffn_in, v_w_ffn_down):
    names = ["norm_mix_g", "w_in", "conv_dw_w", "conv_dw_b", "conv_ln_g", "conv_ln_b", "w_conv_out", "q_norm_g",
             "k_norm_g", "sinks", "w_attn_out", "w_merge_out", "norm_ffn_g", "w_ffn_in", "w_ffn_down"]
    w = dict(zip(names, (norm_mix_g, w_in, conv_dw_w, conv_dw_b, conv_ln_g, conv_ln_b, w_conv_out, q_norm_g, k_norm_g,
                         sinks, w_attn_out, w_merge_out, norm_ffn_g, w_ffn_in, w_ffn_down)))
    m = dict(zip(names, (m_norm_mix_g, m_w_in, m_conv_dw_w, m_conv_dw_b, m_conv_ln_g, m_conv_ln_b, m_w_conv_out,
                         m_q_norm_g, m_k_norm_g, m_sinks, m_w_attn_out, m_w_merge_out, m_norm_ffn_g, m_w_ffn_in,
                         m_w_ffn_down)))
    v = dict(zip(names, (v_norm_mix_g, v_w_in, v_conv_dw_w, v_conv_dw_b, v_conv_ln_g, v_conv_ln_b, v_w_conv_out,
                         v_q_norm_g, v_k_norm_g, v_sinks, v_w_attn_out, v_w_merge_out, v_norm_ffn_g, v_w_ffn_in,
                         v_w_ffn_down)))
    B, S, _ = x.shape
    T = B * S
    me = 4 * lax.axis_index("x") + 2 * lax.axis_index("y") + lax.axis_index("c")

    big_names = ["w_in", "w_ffn_in", "w_ffn_down", "w_conv_out", "w_attn_out", "w_merge_out"]
    cw_shard = _pad_rows(conv_dw_w.reshape(CONV_WIDTH, 128), 32)
    gathered = _gather_weights_call([w[k].astype(BF16) for k in big_names] + [cw_shard])
    gw = dict(zip(big_names, gathered[:-1]))
    conv_w_full = gathered[-1].reshape(N_DEV, 32 * 128)[:, :CONV_WIDTH * 128].reshape(N_DEV, CONV_WIDTH, 128)
    conv_w_full = conv_w_full.transpose(1, 0, 2).reshape(CONV_WIDTH, C_CONV)
    w_in_i = _to_internal_cols(_cols_from_blocks(gw["w_in"]))
    w_ffn_in_f = _cols_from_blocks(gw["w_ffn_in"])
    wd_f = gw["w_ffn_down"].reshape(D_FF, D_MODEL)
    wc_f = gw["w_conv_out"].reshape(C_CONV, D_MODEL)
    wa_f = gw["w_attn_out"].reshape(Q_W, D_MODEL)
    wm_f = gw["w_merge_out"].reshape(D_MODEL, D_MODEL)

    sse, grad_x, big, small = _local_step(
        x.reshape(T, D_MODEL), loss_target.reshape(T, D_MODEL), w_in_i, conv_w_full, conv_dw_b, conv_ln_g, conv_ln_b,
        wc_f, q_norm_g, k_norm_g, sinks, wa_f, wm_f, norm_mix_g, norm_ffn_g, w_ffn_in_f, wd_f, B, S)
    loss = lax.psum(0.5 / D_MODEL * sse, ("x", "y", "c"))

    parts = {
        "w_in": _blocks_by_cols(_from_internal_cols(big["w_in"])),
        "w_ffn_in": _blocks_by_cols(big["w_ffn_in"]),
        "w_ffn_down": big["w_ffn_down"].reshape(N_DEV, D_FF // N_DEV, D_MODEL),
        "w_conv_out": big["w_conv_out"].reshape(N_DEV, C_CONV // N_DEV, D_MODEL),
        "w_attn_out": big["w_attn_out"].reshape(N_DEV, Q_W // N_DEV, D_MODEL),
        "w_merge_out": big["w_merge_out"].reshape(N_DEV, D_MODEL // N_DEV, D_MODEL),
    }
    recv = dict(zip(big_names, _scatter_grads_call([parts[k] for k in big_names])))
    grad, delta, new_m, new_v = {}, {}, {}, {}
    for k in big_names:
        tr = {1024: 256, 352: 176, 128: 128}[w[k].shape[0]]
        grad[k], delta[k], new_m[k], new_v[k] = _adamw_sum_call("adamw_" + k, recv[k], w[k], m[k], v[k], tr)

    packed = jnp.concatenate([_pack_small(small), small["conv_dw_w"].reshape(_CW_ROWS, 128)], axis=0)
    tot = _allsum_small_call(packed)
    g_small = _unpack_small(tot[:_SMALL_ROWS], w)
    d_s, m_s, v_s = _adamw_small_call("adamw_small", tot[:_SMALL_ROWS], _pack_small(w), _pack_small(m), _pack_small(v))
    d_small, m_small, v_small = _unpack_small(d_s, w), _unpack_small(m_s, w), _unpack_small(v_s, w)
    g_cw = lax.dynamic_slice_in_dim(tot[_SMALL_ROWS:].reshape(CONV_WIDTH, C_CONV), me * 128, 128, axis=1)
    cw2 = lambda t: t.reshape(CONV_WIDTH, 128)
    d_cw, m_cw, v_cw = _adamw_small_call("adamw_conv_w", g_cw, cw2(conv_dw_w), cw2(m_conv_dw_w), cw2(v_conv_dw_w))
    cw3 = lambda t: t.reshape(CONV_WIDTH, 1, 128)
    for k, _ in _SMALL:
        grad[k], delta[k], new_m[k], new_v[k] = g_small[k], d_small[k], m_small[k], v_small[k]
    grad["conv_dw_w"], delta["conv_dw_w"], new_m["conv_dw_w"], new_v["conv_dw_w"] = cw3(g_cw), cw3(d_cw), cw3(m_cw), cw3(v_cw)

    return (loss, grad_x.reshape(B, S, D_MODEL), *[grad[k] for k in names], *[delta[k] for k in names],
            *[new_m[k] for k in names], *[new_v[k] for k in names])
```

```python
import functools
import math

import jax
import jax.numpy as jnp
from jax import lax
from jax.experimental import pallas as pl
from jax.experimental.pallas import tpu as pltpu

F32 = jnp.float32
BF16 = jnp.bfloat16
MESH = pl.DeviceIdType.MESH

N_DEV = 8
D_MODEL = 1024
C_CONV = 1024
CONV_WIDTH = 31
HEAD_DIM = 64
N_Q_HEADS = 16
N_KV_HEADS = 2
GROUP = N_Q_HEADS // N_KV_HEADS
WINDOW = 128
BLOCK = 128
D_FF = 2816
EPS = 1e-6
NEG = -1e30
ATTN_SCALE = 1.0 / math.sqrt(HEAD_DIM)
Q_W = N_Q_HEADS * HEAD_DIM
KV_W = N_KV_HEADS * HEAD_DIM
IN_COLS = 2 * C_CONV + Q_W + 2 * KV_W + 2 * D_MODEL

ADAM_LR = 0.001
ADAM_B1 = 0.9
ADAM_B2 = 0.999
ADAM_EPS = 1e-08
ADAM_WD = 0.01
ADAM_STEP = 10

COL_A, COL_GATE, COL_GC, COL_GA, COL_Q, COL_K, COL_V = 0, 1024, 2048, 3072, 4096, 5120, 5248

HALO = 32
VMEM_LIMIT = 56 * 1024 * 1024


def _cp(*sem):
    return pltpu.CompilerParams(dimension_semantics=sem, vmem_limit_bytes=VMEM_LIMIT)


def _sigmoid(x):
    return jax.nn.sigmoid(x)


def _dot(a, b):
    return jnp.dot(a, b, preferred_element_type=F32)


def _dot_nt(a, b):
    return lax.dot_general(a, b, (((1,), (1,)), ((), ())), preferred_element_type=F32)


def _dot_tn(a, b):
    return lax.dot_general(a, b, (((0,), (0,)), ((), ())), preferred_element_type=F32)


def _full(shape):
    return pl.BlockSpec(shape, lambda *_: (0,) * len(shape))


def _rows(tm, ncols, colblk=0):
    return pl.BlockSpec((tm, ncols), lambda i: (i, colblk))


def _inproj_call(x, g, w, tm=256):
    T = x.shape[0]
    N = w.shape[1]

    def body(x_ref, g_ref, w_ref, u_ref, xn_ref):
        xv = x_ref[...]
        r = lax.rsqrt(jnp.mean(xv * xv, axis=-1, keepdims=True) + EPS)
        xn = (xv * r * g_ref[...]).astype(BF16)
        xn_ref[...] = xn
        u_ref[...] = _dot(xn, w_ref[...])

    return pl.pallas_call(
        body, name="inproj", grid=(T // tm,),
        in_specs=[_rows(tm, D_MODEL), _full((1, D_MODEL)), _full(w.shape)],
        out_specs=[_rows(tm, N), _rows(tm, D_MODEL)],
        out_shape=[jax.ShapeDtypeStruct((T, N), F32), jax.ShapeDtypeStruct((T, D_MODEL), BF16)],
        compiler_params=_cp("parallel"),
    )(x, g, w)


def _conv_taps(w_ref, src_ref, base, rows, cb, reverse):
    cols = pl.ds(pl.multiple_of(cb * 128, 128), 128)
    outs = []
    for rb in range(rows // 128):
        acc = jnp.zeros((128, 128), F32)
        for j in range(CONV_WIDTH):
            off = base + (CONV_WIDTH - 1 - j if reverse else j) + rb * 128
            acc = acc + w_ref[j:j + 1, cols] * src_ref[pl.ds(off, 128), cols]
        outs.append(acc)
    return outs


def _conv_fwd_call(u, conv_w, conv_b, ln_g, ln_b, B, S, ts=256):
    T = B * S
    nS = S // ts
    per32 = ts // HALO

    def body(a_ref, gate_ref, ap_ref, gp_ref, w_ref, b_ref, g_ref, bb_ref, hc_ref, c_ref, hext, acc_ref):
        s = pl.program_id(1)
        prev = ap_ref[...] * _sigmoid(gp_ref[...])
        hext[0:HALO, :] = jnp.where(s > 0, prev, 0.0)
        hext[HALO:, :] = a_ref[...] * _sigmoid(gate_ref[...])

        def colblock(cb, carry):
            cols = pl.ds(pl.multiple_of(cb * 128, 128), 128)
            outs = _conv_taps(w_ref, hext, HALO - (CONV_WIDTH - 1), ts, cb, reverse=False)
            for rb, acc in enumerate(outs):
                acc_ref[rb * 128:(rb + 1) * 128, cols] = acc
            return carry

        lax.fori_loop(0, C_CONV // 128, colblock, 0)
        hc = acc_ref[...] + b_ref[...]
        hc_ref[...] = hc
        mu = jnp.mean(hc, axis=-1, keepdims=True)
        xc = hc - mu
        var = jnp.mean(xc * xc, axis=-1, keepdims=True)
        y = xc * lax.rsqrt(var + EPS) * g_ref[...] + bb_ref[...]
        c_ref[...] = (y * _sigmoid(y)).astype(BF16)

    cur = lambda cb: pl.BlockSpec((ts, C_CONV), lambda b, s: (b * nS + s, cb))
    prv = lambda cb: pl.BlockSpec((HALO, C_CONV), lambda b, s: (jnp.maximum(b * (S // HALO) + s * per32 - 1, 0), cb))
    vec = _full((1, C_CONV))
    return pl.pallas_call(
        body, name="conv_fwd", grid=(B, nS),
        in_specs=[cur(0), cur(1), prv(0), prv(1), _full((CONV_WIDTH, C_CONV)), vec, vec, vec],
        out_specs=[pl.BlockSpec((ts, C_CONV), lambda b, s: (b * nS + s, 0))] * 2,
        out_shape=[jax.ShapeDtypeStruct((T, C_CONV), F32), jax.ShapeDtypeStruct((T, C_CONV), BF16)],
        scratch_shapes=[pltpu.VMEM((HALO + ts, C_CONV), F32), pltpu.VMEM((ts, C_CONV), F32)],
        compiler_params=_cp("parallel", "arbitrary"),
    )(u, u, u, u, conv_w, conv_b, ln_g, ln_b)


def _lane_lo():
    return lax.broadcasted_iota(jnp.int32, (1, 128), 1) < HEAD_DIM


def _half_sums(t, lo):
    s_lo = jnp.sum(jnp.where(lo, t, 0.0), axis=-1, keepdims=True)
    s_hi = jnp.sum(jnp.where(lo, 0.0, t), axis=-1, keepdims=True)
    return jnp.where(lo, s_lo, s_hi)


def _head_rstd(t, lo):
    return lax.rsqrt(_half_sums(t * t, lo) * (1.0 / HEAD_DIM) + EPS)


def _band_masks():
    qi = lax.broadcasted_iota(jnp.int32, (BLOCK, 2 * BLOCK), 0)
    sj = lax.broadcasted_iota(jnp.int32, (BLOCK, 2 * BLOCK), 1)
    dist = qi + BLOCK - sj
    valid = (dist >= 0) & (dist < WINDOW)
    return dist.astype(F32), valid, sj >= BLOCK


def _stage_kv(kvh, k_ref, v_ref, kg_ref, lo, kA, kB, vA, vB):
    kv = k_ref[...]
    rk = _head_rstd(kv, lo)
    kn = kv * rk * kg_ref[...]
    mine = lax.broadcasted_iota(jnp.int32, (1, 128), 1) // HEAD_DIM == kvh
    ksel = jnp.where(mine, kn, 0.0)
    vsel = jnp.where(mine, v_ref[...], 0.0)
    krol = pltpu.roll(ksel, HEAD_DIM, 1)
    vrol = pltpu.roll(vsel, HEAD_DIM, 1)
    first = kvh == 0
    zeros = jnp.zeros((BLOCK, 128), BF16)
    for ref, val in ((kA, jnp.where(first, ksel, krol)), (kB, jnp.where(first, krol, ksel)),
                     (vA, jnp.where(first, vsel, vrol)), (vB, jnp.where(first, vrol, vsel))):
        ref[0:BLOCK, :] = zeros
        ref[BLOCK:, :] = val.astype(BF16)


def _pair_softmax(s, kvh, p, n, sinks_ref, slopes_ref, dist, valid, cur):
    ps, psinks = [], []
    for hf in range(2):
        h = kvh * GROUP + 2 * p + hf
        sh = s[:, 2 * BLOCK * hf:2 * BLOCK * (hf + 1)]
        ok = valid & ((n > 0) | cur)
        sh = jnp.where(ok, sh - slopes_ref[h] * dist, NEG)
        sk = sinks_ref[h]
        m = jnp.maximum(jnp.max(sh, axis=-1, keepdims=True), sk)
        e = jnp.exp(sh - m)
        esk = jnp.exp(sk - m)
        rz = 1.0 / (jnp.sum(e, axis=-1, keepdims=True) + esk)
        ps.append(e * rz)
        psinks.append(esk * rz)
    return ps, psinks


def _attn_fwd_call(u, qg2, kg2, sinks, slopes, B, S):
    T = B * S
    nb = S // BLOCK
    half = Q_W // 2

    def body(sinks_ref, slopes_ref, q_ref, k_ref, v_ref, qg_ref, kg_ref, o_ref, kA, kB, vA, vB):
        kvh = pl.program_id(1)
        lo = _lane_lo()
        _stage_kv(kvh, k_ref, v_ref, kg_ref, lo, kA, kB, vA, vB)
        dist, valid, cur = _band_masks()

        def blk(n, carry):
            r0 = pl.multiple_of(n * BLOCK, BLOCK)
            win = pl.ds(r0, 2 * BLOCK)
            k2t = jnp.concatenate([kA[win, :], kB[win, :]], axis=0)
            v2 = jnp.concatenate([vA[win, :], vB[win, :]], axis=0)
            for p in range(GROUP // 2):
                qp = q_ref[pl.ds(r0, BLOCK), 128 * p:128 * (p + 1)]
                qn = (qp * _head_rstd(qp, lo) * qg_ref[...]).astype(BF16)
                s = _dot_nt(qn, k2t) * ATTN_SCALE
                ps, _ = _pair_softmax(s, kvh, p, n, sinks_ref, slopes_ref, dist, valid, cur)
                pb = jnp.concatenate(ps, axis=1).astype(BF16)
                o_ref[pl.ds(r0, BLOCK), 128 * p:128 * (p + 1)] = _dot(pb, v2).astype(BF16)
            return carry

        lax.fori_loop(0, nb, blk, 0)

    smem = pl.BlockSpec(memory_space=pltpu.SMEM)
    kvbuf = pltpu.VMEM((S + BLOCK, 128), BF16)
    return pl.pallas_call(
        body, name="attn_fwd", grid=(B, N_KV_HEADS),
        in_specs=[smem, smem,
                  pl.BlockSpec((S, half), lambda b, h: (b, COL_Q // half + h)),
                  pl.BlockSpec((S, 128), lambda b, h: (b, COL_K // 128)),
                  pl.BlockSpec((S, 128), lambda b, h: (b, COL_V // 128)),
                  _full((1, 128)), _full((1, 128))],
        out_specs=pl.BlockSpec((S, half), lambda b, h: (b, h)),
        out_shape=jax.ShapeDtypeStruct((T, Q_W), BF16),
        scratch_shapes=[kvbuf, kvbuf, kvbuf, kvbuf],
        compiler_params=_cp("parallel", "arbitrary"),
    )(sinks, slopes, u, u, u, qg2, kg2)


def _merge_call(c, o, u, x, wc, wa, wm, g_ffn, tm=256):
    T = x.shape[0]

    def body(c_ref, o_ref, gc_ref, ga_ref, x_ref, wc_ref, wa_ref, wm_ref, g_ref, yc_ref, ya_ref, z_ref, h_ref, hn_ref):
        yc = _dot(c_ref[...], wc_ref[...])
        ya = _dot(o_ref[...], wa_ref[...])
        yc_ref[...] = yc
        ya_ref[...] = ya
        z = (_sigmoid(gc_ref[...]) * yc + _sigmoid(ga_ref[...]) * ya).astype(BF16)
        z_ref[...] = z
        h = x_ref[...] + _dot(z, wm_ref[...])
        h_ref[...] = h
        r = lax.rsqrt(jnp.mean(h * h, axis=-1, keepdims=True) + EPS)
        hn_ref[...] = (h * r * g_ref[...]).astype(BF16)

    sq = _full((D_MODEL, D_MODEL))
    row = _rows(tm, D_MODEL)
    f32 = jax.ShapeDtypeStruct((T, D_MODEL), F32)
    b16 = jax.ShapeDtypeStruct((T, D_MODEL), BF16)
    return pl.pallas_call(
        body, name="merge_fwd", grid=(T // tm,),
        in_specs=[row, row, _rows(tm, D_MODEL, COL_GC // D_MODEL), _rows(tm, D_MODEL, COL_GA // D_MODEL), row,
                  sq, sq, sq, _full((1, D_MODEL))],
        out_specs=[row] * 5,
        out_shape=[f32, f32, b16, f32, b16],
        compiler_params=_cp("parallel"),
    )(c, o, u, u, x, wc, wa, wm, g_ffn)


def _ffn_in_call(hn, w, tm=256):
    T = hn.shape[0]

    def body(hn_ref, w_ref, gu_ref, act_ref):
        gu = _dot(hn_ref[...], w_ref[...])
        gu_ref[...] = gu
        g = gu[:, :D_FF]
        act_ref[...] = (g * _sigmoid(g) * gu[:, D_FF:]).astype(BF16)

    return pl.pallas_call(
        body, name="ffn_in", grid=(T // tm,),
        in_specs=[_rows(tm, D_MODEL), _full(w.shape)],
        out_specs=[_rows(tm, 2 * D_FF), _rows(tm, D_FF)],
        out_shape=[jax.ShapeDtypeStruct((T, 2 * D_FF), F32), jax.ShapeDtypeStruct((T, D_FF), BF16)],
        compiler_params=_cp("parallel"),
    )(hn, w)


def _ffn_out_call(act, wd, h, tgt, tm=512):
    T = h.shape[0]

    def body(act_ref, wd_ref, h_ref, t_ref, sse_ref, d_ref, db_ref):
        e = h_ref[...] + _dot(act_ref[...], wd_ref[...]) - t_ref[...]

        @pl.when(pl.program_id(0) == 0)
        def _():
            sse_ref[...] = jnp.zeros_like(sse_ref)

        sse_ref[...] += jnp.sum(e * e)
        d = e * (1.0 / D_MODEL)
        d_ref[...] = d
        db_ref[...] = d.astype(BF16)

    row = _rows(tm, D_MODEL)
    return pl.pallas_call(
        body, name="ffn_out", grid=(T // tm,),
        in_specs=[_rows(tm, D_FF), _full(wd.shape), row, row],
        out_specs=[_full((8, 128)), row, row],
        out_shape=[jax.ShapeDtypeStruct((8, 128), F32), jax.ShapeDtypeStruct((T, D_MODEL), F32),
                   jax.ShapeDtypeStruct((T, D_MODEL), BF16)],
        compiler_params=_cp("arbitrary"),
    )(act, wd, h, tgt)


def _dw_call(name, a, b, tn, tk=1024):
    T, K = a.shape
    N = b.shape[1]
    tk = min(tk, T)
    nk = T // tk

    def body(a_ref, b_ref, o_ref, acc_ref):
        k = pl.program_id(1)

        @pl.when(k == 0)
        def _():
            acc_ref[...] = jnp.zeros_like(acc_ref)

        acc_ref[...] += _dot_tn(a_ref[...], b_ref[...])

        @pl.when(k == nk - 1)
        def _():
            o_ref[...] = acc_ref[...].astype(BF16)

    return pl.pallas_call(
        body, name=name, grid=(N // tn, nk),
        in_specs=[pl.BlockSpec((tk, K), lambda j, k: (k, 0)), pl.BlockSpec((tk, tn), lambda j, k: (k, j))],
        out_specs=pl.BlockSpec((K, tn), lambda j, k: (0, j)),
        out_shape=jax.ShapeDtypeStruct((K, N), BF16),
        scratch_shapes=[pltpu.VMEM((K, tn), F32)],
        compiler_params=_cp("parallel", "arbitrary"),
    )(a, b)


def _rms_bwd(dy, xv, g):
    r = lax.rsqrt(jnp.mean(xv * xv, axis=-1, keepdims=True) + EPS)
    t = dy * g
    dx = r * t - xv * (r * r * r * jnp.mean(t * xv, axis=-1, keepdims=True))
    return dx, dy * xv * r


def _ffn_down_bwd_call(dout_b, wd, gu, tm=256):
    T = dout_b.shape[0]

    def body(d_ref, wd_ref, g_ref, up_ref, o_ref):
        dact = _dot_nt(d_ref[...], wd_ref[...])
        g = g_ref[...]
        sg = _sigmoid(g)
        o_ref[:, :D_FF] = (dact * up_ref[...] * (sg * (1.0 + g * (1.0 - sg)))).astype(BF16)
        o_ref[:, D_FF:] = (dact * (g * sg)).astype(BF16)

    return pl.pallas_call(
        body, name="ffn_down_bwd", grid=(T // tm,),
        in_specs=[_rows(tm, D_MODEL), _full(wd.shape), _rows(tm, D_FF, 0), _rows(tm, D_FF, 1)],
        out_specs=_rows(tm, 2 * D_FF),
        out_shape=jax.ShapeDtypeStruct((T, 2 * D_FF), BF16),
        compiler_params=_cp("parallel"),
    )(dout_b, wd, gu, gu)


def _ffn_in_bwd_call(dgu, w, h, g_ffn, dout, tm=256):
    T = h.shape[0]

    def body(dgu_ref, w_ref, h_ref, g_ref, d_ref, dh_ref, dhb_ref, dg_ref):
        dhn = _dot_nt(dgu_ref[...], w_ref[...])
        dx, dgt = _rms_bwd(dhn, h_ref[...], g_ref[...])
        dh = d_ref[...] + dx
        dh_ref[...] = dh
        dhb_ref[...] = dh.astype(BF16)

        @pl.when(pl.program_id(0) == 0)
        def _():
            dg_ref[...] = jnp.zeros_like(dg_ref)

        dg_ref[...] += jnp.sum(dgt, axis=0, keepdims=True)

    row = _rows(tm, D_MODEL)
    return pl.pallas_call(
        body, name="ffn_in_bwd", grid=(T // tm,),
        in_specs=[_rows(tm, 2 * D_FF), _full(w.shape), row, _full((1, D_MODEL)), row],
        out_specs=[row, row, _full((1, D_MODEL))],
        out_shape=[jax.ShapeDtypeStruct((T, D_MODEL), F32), jax.ShapeDtypeStruct((T, D_MODEL), BF16),
                   jax.ShapeDtypeStruct((1, D_MODEL), F32)],
        compiler_params=_cp("arbitrary"),
    )(dgu, w, h, g_ffn, dout)


def _merge_bwd_call(dh_b, wm, wc, wa, u, yc, ya, tm=256):
    T = dh_b.shape[0]

    def body(dh_ref, wm_ref, wc_ref, wa_ref, gc_ref, ga_ref, yc_ref, ya_ref, dyc_ref, dya_ref, dug_ref, dcp_ref, do_ref):
        dz = _dot_nt(dh_ref[...], wm_ref[...])
        sgc = _sigmoid(gc_ref[...])
        sga = _sigmoid(ga_ref[...])
        dyc = (dz * sgc).astype(BF16)
        dya = (dz * sga).astype(BF16)
        dyc_ref[...] = dyc
        dya_ref[...] = dya
        dug_ref[:, :D_MODEL] = (dz * yc_ref[...] * (sgc * (1.0 - sgc))).astype(BF16)
        dug_ref[:, D_MODEL:] = (dz * ya_ref[...] * (sga * (1.0 - sga))).astype(BF16)
        dcp_ref[...] = _dot_nt(dyc, wc_ref[...])
        do_ref[...] = _dot_nt(dya, wa_ref[...]).astype(BF16)

    sq = _full((D_MODEL, D_MODEL))
    row = _rows(tm, D_MODEL)
    b16 = jax.ShapeDtypeStruct((T, D_MODEL), BF16)
    return pl.pallas_call(
        body, name="merge_bwd", grid=(T // tm,),
        in_specs=[row, sq, sq, sq, _rows(tm, D_MODEL, COL_GC // D_MODEL), _rows(tm, D_MODEL, COL_GA // D_MODEL), row, row],
        out_specs=[row, row, _rows(tm, 2 * D_MODEL), row, row],
        out_shape=[b16, b16, jax.ShapeDtypeStruct((T, 2 * D_MODEL), BF16), jax.ShapeDtypeStruct((T, D_MODEL), F32), b16],
        compiler_params=_cp("parallel"),
    )(dh_b, wm, wc, wa, u, u, yc, ya)


def _conv_bwd_call(dcp, hc, u, conv_w, ln_g, ln_b, B, S, ts=256):
    T = B * S
    nS = S // ts
    per32 = ts // HALO
    last32 = T // HALO - 1

    def ln_bwd(dcp_v, hc_v, g, bb):
        mu = jnp.mean(hc_v, axis=-1, keepdims=True)
        xc = hc_v - mu
        rstd = lax.rsqrt(jnp.mean(xc * xc, axis=-1, keepdims=True) + EPS)
        xhat = xc * rstd
        y = xhat * g + bb
        sy = _sigmoid(y)
        dy = dcp_v * (sy * (1.0 + y * (1.0 - sy)))
        dxh = dy * g
        dhc = rstd * (dxh - jnp.mean(dxh, axis=-1, keepdims=True) - xhat * jnp.mean(dxh * xhat, axis=-1, keepdims=True))
        return dhc, dy * xhat, dy

    def body(dcp_ref, dcpn_ref, hc_ref, hcn_ref, a_ref, gate_ref, ap_ref, gp_ref, w_ref, g_ref, bb_ref,
             du_ref, dw_ref, db_ref, dlg_ref, dlb_ref, dhext, hgext, acc_ref):
        b = pl.program_id(0)
        s = pl.program_id(1)

        @pl.when((b == 0) & (s == 0))
        def _():
            dw_ref[...] = jnp.zeros_like(dw_ref)
            db_ref[...] = jnp.zeros_like(db_ref)
            dlg_ref[...] = jnp.zeros_like(dlg_ref)
            dlb_ref[...] = jnp.zeros_like(dlb_ref)

        dhc, dgt, dbt = ln_bwd(dcp_ref[...], hc_ref[...], g_ref[...], bb_ref[...])
        dhn, _, _ = ln_bwd(dcpn_ref[...], hcn_ref[...], g_ref[...], bb_ref[...])
        dhext[0:ts, :] = dhc
        dhext[ts:, :] = jnp.where(s < nS - 1, dhn, 0.0)
        db_ref[...] += jnp.sum(dhc, axis=0, keepdims=True)
        dlg_ref[...] += jnp.sum(dgt, axis=0, keepdims=True)
        dlb_ref[...] += jnp.sum(dbt, axis=0, keepdims=True)

        sg = _sigmoid(gate_ref[...])
        av = a_ref[...]
        hgext[0:HALO, :] = jnp.where(s > 0, ap_ref[...] * _sigmoid(gp_ref[...]), 0.0)
        hgext[HALO:, :] = av * sg

        def colblock(cb, carry):
            cols = pl.ds(pl.multiple_of(cb * 128, 128), 128)
            outs = _conv_taps(w_ref, dhext, 0, ts, cb, reverse=True)
            for rb, acc in enumerate(outs):
                acc_ref[rb * 128:(rb + 1) * 128, cols] = acc
            for j in range(CONV_WIDTH):
                part = jnp.zeros((8, 128), F32)
                for rb in range(ts // 128):
                    prod = dhext[rb * 128:(rb + 1) * 128, cols] * hgext[pl.ds(HALO - (CONV_WIDTH - 1) + j + rb * 128, 128), cols]
                    part = part + jnp.sum(prod.reshape(16, 8, 128), axis=0)
                dw_ref[8 * j:8 * (j + 1), cols] += part
            return carry

        lax.fori_loop(0, C_CONV // 128, colblock, 0)
        dglu = acc_ref[...]
        du_ref[:, :C_CONV] = (dglu * sg).astype(BF16)
        du_ref[:, C_CONV:] = (dglu * av * (sg * (1.0 - sg))).astype(BF16)

    cur = lambda cb: pl.BlockSpec((ts, C_CONV), lambda b, s: (b * nS + s, cb))
    prv = lambda cb: pl.BlockSpec((HALO, C_CONV), lambda b, s: (jnp.maximum(b * (S // HALO) + s * per32 - 1, 0), cb))
    nxt = pl.BlockSpec((HALO, C_CONV), lambda b, s: (jnp.minimum(b * (S // HALO) + (s + 1) * per32, last32), 0))
    vec = _full((1, C_CONV))
    return pl.pallas_call(
        body, name="conv_bwd", grid=(B, nS),
        in_specs=[cur(0), nxt, cur(0), nxt, cur(0), cur(1), prv(0), prv(1), _full((CONV_WIDTH, C_CONV)), vec, vec],
        out_specs=[pl.BlockSpec((ts, 2 * C_CONV), lambda b, s: (b * nS + s, 0)), _full((8 * CONV_WIDTH, C_CONV)), vec, vec, vec],
        out_shape=[jax.ShapeDtypeStruct((T, 2 * C_CONV), BF16), jax.ShapeDtypeStruct((8 * CONV_WIDTH, C_CONV), F32),
                   jax.ShapeDtypeStruct((1, C_CONV), F32), jax.ShapeDtypeStruct((1, C_CONV), F32),
                   jax.ShapeDtypeStruct((1, C_CONV), F32)],
        scratch_shapes=[pltpu.VMEM((ts + HALO, C_CONV), F32), pltpu.VMEM((HALO + ts, C_CONV), F32),
                        pltpu.VMEM((ts, C_CONV), F32)],
        compiler_params=_cp("arbitrary", "arbitrary"),
    )(dcp, dcp, hc, hc, u, u, u, u, conv_w, ln_g, ln_b)


def _attn_bwd_call(u, do, qg2, kg2, sinks, slopes, B, S):
    T = B * S
    nb = S // BLOCK
    half = Q_W // 2

    def body(sinks_ref, slopes_ref, q_ref, k_ref, v_ref, do_ref, qg_ref, kg_ref,
             dq_ref, dkv_ref, dqg_ref, dkg_ref, dsk_ref, kA, kB, vA, vB, dkn_acc, dv_acc, dkv_keep):
        b = pl.program_id(0)
        kvh = pl.program_id(1)
        lo = _lane_lo()
        lane = lax.broadcasted_iota(jnp.int32, (1, 128), 1)
        first = kvh == 0

        @pl.when((b == 0) & first)
        def _():
            dqg_ref[...] = jnp.zeros_like(dqg_ref)
            dkg_ref[...] = jnp.zeros_like(dkg_ref)
            dsk_ref[...] = jnp.zeros_like(dsk_ref)

        _stage_kv(kvh, k_ref, v_ref, kg_ref, lo, kA, kB, vA, vB)
        dkn_acc[...] = jnp.zeros_like(dkn_acc)
        dv_acc[...] = jnp.zeros_like(dv_acc)
        dist, valid, cur = _band_masks()

        def fold(acc):
            canon = jnp.where(lo, acc[0:2 * BLOCK] + pltpu.roll(acc[2 * BLOCK:], HEAD_DIM, 1), 0.0)
            return jnp.where(first, canon, pltpu.roll(canon, HEAD_DIM, 1))

        def blk(n, carry):
            r0 = pl.multiple_of(n * BLOCK, BLOCK)
            win = pl.ds(r0, 2 * BLOCK)
            rows = pl.ds(r0, BLOCK)
            k2t = jnp.concatenate([kA[win, :], kB[win, :]], axis=0)
            v2 = jnp.concatenate([vA[win, :], vB[win, :]], axis=0)
            dk2 = jnp.zeros((4 * BLOCK, 128), F32)
            dv2 = jnp.zeros((4 * BLOCK, 128), F32)
            for p in range(GROUP // 2):
                pc = slice(128 * p, 128 * (p + 1))
                qp = q_ref[rows, pc]
                rq = _head_rstd(qp, lo)
                qg = qg_ref[...]
                qn = (qp * rq * qg).astype(BF16)
                s = _dot_nt(qn, k2t) * ATTN_SCALE
                ps, psinks = _pair_softmax(s, kvh, p, n, sinks_ref, slopes_ref, dist, valid, cur)
                dop = do_ref[rows, pc]
                dp = _dot_nt(dop, v2)
                dss = []
                for hf in range(2):
                    dph = dp[:, 2 * BLOCK * hf:2 * BLOCK * (hf + 1)]
                    delta = jnp.sum(ps[hf] * dph, axis=-1, keepdims=True)
                    dss.append(ps[hf] * (dph - delta) * ATTN_SCALE)
                    h = kvh * GROUP + 2 * p + hf
                    dsk_ref[...] += jnp.where(lane == h, -jnp.sum(psinks[hf] * delta), 0.0)
                dsb = jnp.concatenate(dss, axis=1).astype(BF16)
                pb = jnp.concatenate(ps, axis=1).astype(BF16)
                dqn = _dot(dsb, k2t)
                dk2 = dk2 + _dot_tn(dsb, qn)
                dv2 = dv2 + _dot_tn(pb, dop)
                t = dqn * qg
                dq = rq * t - qp * (rq * rq * rq * (_half_sums(t * qp, lo) * (1.0 / HEAD_DIM)))
                dq_ref[rows, pc] = dq.astype(BF16)
                dqg_ref[...] += jnp.sum(dqn * qp * rq, axis=0, keepdims=True)
            dkn_acc[win, :] += fold(dk2)
            dv_acc[win, :] += fold(dv2)
            return carry

        lax.fori_loop(0, nb, blk, 0)

        dkn = dkn_acc[BLOCK:, :]
        kv = k_ref[...]
        kg = kg_ref[...]
        rk = _head_rstd(kv, lo)
        t = dkn * kg
        dk = rk * t - kv * (rk * rk * rk * (_half_sums(t * kv, lo) * (1.0 / HEAD_DIM)))
        dkg_ref[...] += jnp.sum(dkn * kv * rk, axis=0, keepdims=True)
        dv = dv_acc[BLOCK:, :]

        @pl.when(first)
        def _():
            dkv_keep[:, :128] = dk
            dkv_keep[:, 128:] = dv

        @pl.when(kvh == N_KV_HEADS - 1)
        def _():
            dkv_ref[:, :128] = (dkv_keep[:, :128] + dk).astype(BF16)
            dkv_ref[:, 128:] = (dkv_keep[:, 128:] + dv).astype(BF16)

    smem = pl.BlockSpec(memory_space=pltpu.SMEM)
    kvbuf = pltpu.VMEM((S + BLOCK, 128), BF16)
    accbuf = pltpu.VMEM((S + BLOCK, 128), F32)
    vec = _full((1, 128))
    return pl.pallas_call(
        body, name="attn_bwd", grid=(B, N_KV_HEADS),
        in_specs=[smem, smem,
                  pl.BlockSpec((S, half), lambda b, h: (b, COL_Q // half + h)),
                  pl.BlockSpec((S, 128), lambda b, h: (b, COL_K // 128)),
                  pl.BlockSpec((S, 128), lambda b, h: (b, COL_V // 128)),
                  pl.BlockSpec((S, half), lambda b, h: (b, h)),
                  vec, vec],
        out_specs=[pl.BlockSpec((S, half), lambda b, h: (b, h)), pl.BlockSpec((S, 256), lambda b, h: (b, 0)), vec, vec, vec],
        out_shape=[jax.ShapeDtypeStruct((T, Q_W), BF16), jax.ShapeDtypeStruct((T, 2 * KV_W), BF16),
                   jax.ShapeDtypeStruct((1, 128), F32), jax.ShapeDtypeStruct((1, 128), F32), jax.ShapeDtypeStruct((1, 128), F32)],
        scratch_shapes=[kvbuf, kvbuf, kvbuf, kvbuf, accbuf, accbuf, pltpu.VMEM((S, 256), F32)],
        compiler_params=_cp("arbitrary", "arbitrary"),
    )(sinks, slopes, u, u, u, do, qg2, kg2)


def _inproj_bwd_call(du_conv, du_g, du_q, du_kv, w, x, g, dh, tm=256):
    T = x.shape[0]

    def body(dc_ref, dg_ref, dq_ref, dkv_ref, w_ref, x_ref, g_ref, dh_ref, gx_ref, dgm_ref):
        dxn = _dot_nt(dc_ref[...], w_ref[:, COL_A:COL_GC])
        dxn += _dot_nt(dg_ref[...], w_ref[:, COL_GC:COL_Q])
        dxn += _dot_nt(dq_ref[...], w_ref[:, COL_Q:COL_K])
        dxn += _dot_nt(dkv_ref[...], w_ref[:, COL_K:])
        dx, dgt = _rms_bwd(dxn, x_ref[...], g_ref[...])
        gx_ref[...] = dh_ref[...] + dx

        @pl.when(pl.program_id(0) == 0)
        def _():
            dgm_ref[...] = jnp.zeros_like(dgm_ref)

        dgm_ref[...] += jnp.sum(dgt, axis=0, keepdims=True)

    row = _rows(tm, D_MODEL)
    return pl.pallas_call(
        body, name="inproj_bwd", grid=(T // tm,),
        in_specs=[_rows(tm, 2 * C_CONV), _rows(tm, 2 * D_MODEL), _rows(tm, Q_W), _rows(tm, 2 * KV_W), _full(w.shape),
                  row, _full((1, D_MODEL)), row],
        out_specs=[row, _full((1, D_MODEL))],
        out_shape=[jax.ShapeDtypeStruct((T, D_MODEL), F32), jax.ShapeDtypeStruct((1, D_MODEL), F32)],
        compiler_params=_cp("arbitrary"),
    )(du_conv, du_g, du_q, du_kv, w, x, g, dh)


def _adamw_math(w, g, m, v):
    m = ADAM_B1 * m + (1.0 - ADAM_B1) * g
    v = ADAM_B2 * v + (1.0 - ADAM_B2) * (g * g)
    m_hat = m / (1.0 - ADAM_B1 ** ADAM_STEP)
    v_hat = v / (1.0 - ADAM_B2 ** ADAM_STEP)
    delta = -ADAM_LR * (m_hat / (jnp.sqrt(v_hat) + ADAM_EPS) + ADAM_WD * w)
    return delta, m, v


def _adamw_sum_call(name, parts, own, w, m, v, tr):
    R, C = w.shape

    def body(p_ref, own_ref, w_ref, m_ref, v_ref, g_ref, d_ref, nm_ref, nv_ref):
        x, y, c = _position()
        me = 4 * x + 2 * y + c
        g = jnp.zeros((tr, C), F32)
        for s in range(N_DEV):
            g = g + jnp.where(me == s, own_ref[...], p_ref[s]).astype(F32)
        g_ref[...] = g
        d_ref[...], nm_ref[...], nv_ref[...] = _adamw_math(w_ref[...], g, m_ref[...], v_ref[...])

    blk = pl.BlockSpec((tr, C), lambda i: (i, 0))
    out = jax.ShapeDtypeStruct((R, C), F32)
    return pl.pallas_call(
        body, name=name, grid=(R // tr,),
        in_specs=[pl.BlockSpec((N_DEV, tr, C), lambda i: (0, i, 0)), blk, blk, blk, blk],
        out_specs=[blk] * 4, out_shape=[out] * 4,
        compiler_params=_cp("parallel"),
    )(parts, own, w, m, v)


def _adamw_small_call(name, g, w, m, v):
    def body(g_ref, w_ref, m_ref, v_ref, d_ref, nm_ref, nv_ref):
        d_ref[...], nm_ref[...], nv_ref[...] = _adamw_math(w_ref[...], g_ref[...], m_ref[...], v_ref[...])

    out = jax.ShapeDtypeStruct(w.shape, F32)
    return pl.pallas_call(body, name=name, out_shape=[out] * 3)(g, w, m, v)


def _position():
    return lax.axis_index("x"), lax.axis_index("y"), lax.axis_index("c")


def _peer(x, y, c, rel):
    fx, fy, fc = (rel >> 2) & 1, (rel >> 1) & 1, rel & 1
    px = 1 - x if fx else x
    py = 1 - y if fy else y
    pc = 1 - c if fc else c
    return (px, py, pc), 4 * px + 2 * py + pc


def _gather_weights_call(shards):
    n = len(shards)

    def body(*refs):
        ins, outs = refs[:n], refs[n:2 * n]
        send_sems, recv_sems, local_sems = refs[2 * n:]
        x, y, c = _position()
        me = 4 * x + 2 * y + c
        sib, sib_id = _peer(x, y, c, 1)
        chips = [_peer(x, y, c, rel) for rel in (4, 2, 6)]

        def copy(i, k, src, slot, to):
            return pltpu.make_async_remote_copy(
                src_ref=src, dst_ref=outs[i].at[slot], send_sem=send_sems.at[i, k], recv_sem=recv_sems.at[i, k],
                device_id=to, device_id_type=MESH)

        mine = [pltpu.make_async_copy(ins[i], outs[i].at[me], local_sems.at[i]) for i in range(n)]
        for cp in mine:
            cp.start()
        first = []
        for i in range(n):
            first.append(copy(i, 0, ins[i], me, sib))
            first += [copy(i, 1 + j, ins[i], me, dev) for j, (dev, _) in enumerate(chips)]
        for cp in first:
            cp.start()
        passed = []
        for i in range(n):
            for j, (dev, slot) in enumerate(chips):
                copy(i, 1 + j, ins[i], slot, dev).wait_recv()
                fwd = copy(i, 4 + j, outs[i].at[slot], slot, sib)
                fwd.start()
                passed.append(fwd)
        for i in range(n):
            copy(i, 0, ins[i], sib_id, sib).wait_recv()
            for j, (_, slot) in enumerate(chips):
                copy(i, 4 + j, ins[i], slot ^ 1, sib).wait_recv()
        for cp in first + passed:
            cp.wait_send()
        for cp in mine:
            cp.wait()

    hbm = pl.BlockSpec(memory_space=pl.ANY)
    return pl.pallas_call(
        body, name="gather_weights",
        in_specs=[hbm] * n, out_specs=[hbm] * n,
        out_shape=[jax.ShapeDtypeStruct((N_DEV,) + s.shape, s.dtype) for s in shards],
        scratch_shapes=[pltpu.SemaphoreType.DMA((n, 7)), pltpu.SemaphoreType.DMA((n, 7)), pltpu.SemaphoreType.DMA((n,))],
    )(*shards)


_HBM = pl.BlockSpec(memory_space=pltpu.HBM)
_SEM = pl.BlockSpec(memory_space=pltpu.SEMAPHORE)
_EFFECT = pltpu.SideEffectType.DATAFLOW_SIDE_EFFECTING


def _exchange_copies(scatter, incoming, src_refs, land_refs, send_sems, recv_sems):
    x, y, c = _position()
    me = 4 * x + 2 * y + c
    out = []
    for i, (src, land) in enumerate(zip(src_refs, land_refs)):
        for k in range(N_DEV - 1):
            dev, dev_id = _peer(x, y, c, k + 1)
            j = i * (N_DEV - 1) + k
            out.append(pltpu.make_async_remote_copy(
                src_ref=src.at[dev_id] if scatter else src, dst_ref=land.at[dev_id if incoming else me],
                send_sem=send_sems.at[j], recv_sem=recv_sems.at[j], device_id=dev, device_id_type=MESH))
    return out


def _exchange_start_call(name, arrays, scatter):
    n = len(arrays)
    lands = [lax.empty((N_DEV,) + (a.shape[1:] if scatter else a.shape), a.dtype) for a in arrays]

    def body(*refs):
        src_refs, land_refs = refs[:n], refs[n:2 * n]
        send_sems, recv_sems, token = refs[2 * n], refs[2 * n + 1], refs[-1]
        for copy in _exchange_copies(scatter, False, src_refs, land_refs, send_sems, recv_sems):
            copy.start()
        token[...] = jnp.zeros_like(token)

    sems = pltpu.SemaphoreType.DMA((n * (N_DEV - 1),))
    res = pl.pallas_call(
        body, name=name,
        out_shape=[sems, sems] + [pltpu.HBM(a.shape, a.dtype) for a in arrays] + [pltpu.HBM(l.shape, l.dtype) for l in lands]
        + [jax.ShapeDtypeStruct((8, 128), F32)],
        in_specs=[_HBM] * (2 * n), out_specs=[_SEM, _SEM] + [_HBM] * (2 * n) + [pl.BlockSpec(memory_space=pltpu.VMEM)],
        input_output_aliases={i: 2 + i for i in range(2 * n)},
        compiler_params=pltpu.CompilerParams(has_side_effects=_EFFECT),
    )(*[pltpu.with_memory_space_constraint(a, pltpu.HBM) for a in list(arrays) + lands])
    return res[0], res[1], res[2:2 + n], res[2 + n:2 + 2 * n], res[-1]


def _exchange_wait_call(name, started, scatter, after):
    send_sems, recv_sems, srcs, lands, _ = started
    n = len(srcs)

    def body(*refs):
        src_refs, land_refs = refs[:n], refs[n:2 * n]
        send_s, recv_s = refs[2 * n], refs[2 * n + 1]
        for copy in _exchange_copies(scatter, False, src_refs, land_refs, send_s, recv_s):
            copy.wait_send()
        for copy in _exchange_copies(scatter, True, src_refs, land_refs, send_s, recv_s):
            copy.wait_recv()

    res = pl.pallas_call(
        body, name=name,
        out_shape=[pltpu.HBM(a.shape, a.dtype) for a in list(srcs) + list(lands)],
        in_specs=[_HBM] * (2 * n) + [_SEM, _SEM, pl.BlockSpec(memory_space=pl.ANY)], out_specs=[_HBM] * (2 * n),
        input_output_aliases={i: i for i in range(2 * n)},
        compiler_params=pltpu.CompilerParams(has_side_effects=_EFFECT),
    )(*srcs, *lands, send_sems, recv_sems, after)
    return res[n:]


def _allsum_small_call(part):
    P = part.shape[0]

    def body(p_ref, o_ref, all_ref, send_sems, recv_sems):
        x, y, c = _position()
        me = 4 * x + 2 * y + c
        peers = [_peer(x, y, c, rel) for rel in range(1, N_DEV)]
        all_ref[me] = p_ref[...]

        def copy(k, slot, dev):
            return pltpu.make_async_remote_copy(
                src_ref=p_ref, dst_ref=all_ref.at[slot], send_sem=send_sems.at[k], recv_sem=recv_sems.at[k],
                device_id=dev, device_id_type=MESH)

        sends = [copy(k, me, dev) for k, (dev, _) in enumerate(peers)]
        for cp in sends:
            cp.start()
        for k, (dev, dev_id) in enumerate(peers):
            copy(k, dev_id, dev).wait_recv()
        for cp in sends:
            cp.wait_send()
        tot = all_ref[0]
        for s in range(1, N_DEV):
            tot = tot + all_ref[s]
        o_ref[...] = tot

    vm = pl.BlockSpec(memory_space=pltpu.VMEM)
    return pl.pallas_call(
        body, name="allsum_small", in_specs=[vm], out_specs=vm,
        out_shape=jax.ShapeDtypeStruct(part.shape, F32),
        scratch_shapes=[pltpu.VMEM((N_DEV, P, 128), F32), pltpu.SemaphoreType.DMA((7,)), pltpu.SemaphoreType.DMA((7,))],
    )(part)


def _pad_rows(v, rows):
    flat = v.reshape(-1)
    return jnp.pad(flat, (0, rows * 128 - flat.shape[0])).reshape(rows, 128)


def _blocks_by_cols(w):
    K, N = w.shape
    return w.reshape(K, N_DEV, N // N_DEV).transpose(1, 0, 2)


def _cols_from_blocks(wb):
    n, K, C = wb.shape
    return wb.transpose(1, 0, 2).reshape(K, n * C)


def _local_step(x, tgt, w_in_i, conv_w, conv_b, ln_g, ln_b, qg, kg, sinks, g_mix, g_ffn, token0, rest_weights, grads_ready, B, S):
    row = lambda v: v.reshape(1, -1)
    qg2 = jnp.tile(qg, 2).reshape(1, 128)
    kg2 = jnp.tile(kg, 2).reshape(1, 128)
    heads = jnp.arange(1, N_Q_HEADS + 1, dtype=F32)
    slopes = jnp.exp2(-8.0 * heads / N_Q_HEADS)

    u, xn = _inproj_call(x, row(g_mix) + token0, w_in_i)
    hc, c = _conv_fwd_call(u, conv_w, row(conv_b), row(ln_g), row(ln_b), B, S)
    o = _attn_fwd_call(u, qg2, kg2, sinks, slopes, B, S)
    wc, wa, wm, w_ffn_in, wd = rest_weights(o)
    yc, ya, z, h, hn = _merge_call(c, o, u, x, wc, wa, wm, row(g_ffn))
    gu, act = _ffn_in_call(hn, w_ffn_in)
    sse, dout, dout_b = _ffn_out_call(act, wd, h, tgt)

    dgu = _ffn_down_bwd_call(dout_b, wd, gu)
    g_wd = _dw_call("dw_ffn_down", act, dout_b, tn=512)
    g_wffn = _dw_call("dw_ffn_in", hn, dgu, tn=2 * D_FF // 4)
    tok = grads_ready(dict(w_ffn_down=g_wd, w_ffn_in=g_wffn))
    dh, dh_b, g_gffn = _ffn_in_bwd_call(dgu, w_ffn_in, h, row(g_ffn) + tok, dout)
    dyc, dya, du_g, dcp, do = _merge_bwd_call(dh_b, wm, wc, wa, u, yc, ya)
    g_wm = _dw_call("dw_merge", z, dh_b, tn=D_MODEL)
    g_wc = _dw_call("dw_conv_out", c, dyc, tn=D_MODEL)
    g_wa = _dw_call("dw_attn_out", o, dya, tn=D_MODEL)
    tok = grads_ready(dict(w_merge_out=g_wm, w_conv_out=g_wc, w_attn_out=g_wa))
    du_conv, g_cw8, g_cb, g_lg, g_lb = _conv_bwd_call(dcp, hc, u, conv_w, row(ln_g) + tok, row(ln_b), B, S)
    du_q, du_kv, g_qg2, g_kg2, g_sk = _attn_bwd_call(u, do, qg2, kg2, sinks, slopes, B, S)
    g_win = jnp.concatenate([
        _dw_call("dw_in_conv", xn, du_conv, tn=D_MODEL),
        _dw_call("dw_in_q", xn, du_q, tn=D_MODEL),
        _dw_call("dw_in_kv", xn, du_kv, tn=2 * KV_W),
        _dw_call("dw_in_gates", xn, du_g, tn=D_MODEL)], axis=1)
    tok = grads_ready(dict(w_in=g_win))
    grad_x, g_gmix = _inproj_bwd_call(du_conv, du_g, du_q, du_kv, w_in_i, x, row(g_mix) + tok, dh)

    g_cw = g_cw8.reshape(CONV_WIDTH, 8, C_CONV).sum(axis=1)
    g_qg = g_qg2[0, :HEAD_DIM] + g_qg2[0, HEAD_DIM:]
    g_kg = g_kg2[0, :HEAD_DIM] + g_kg2[0, HEAD_DIM:]
    small = dict(norm_mix_g=g_gmix[0], conv_dw_b=g_cb[0], conv_ln_g=g_lg[0], conv_ln_b=g_lb[0], q_norm_g=g_qg,
                 k_norm_g=g_kg, sinks=g_sk[0, :N_Q_HEADS], norm_ffn_g=g_gffn[0], conv_dw_w=g_cw)
    return sse[0, 0], grad_x, small


def _to_internal_cols(w):
    o1 = 2 * C_CONV
    o4 = o1 + Q_W + 2 * KV_W
    return jnp.concatenate([w[:, :o1], w[:, o4:], w[:, o1:o4]], axis=1)


_COL_SHARDED = ("w_in", "w_ffn_in")

_SMALL = (("norm_mix_g", 8), ("conv_dw_b", 8), ("conv_ln_g", 8), ("conv_ln_b", 8), ("norm_ffn_g", 8),
          ("q_norm_g", 1), ("k_norm_g", 1), ("sinks", 1))
_SMALL_ROWS = 48
_CW_ROWS = CONV_WIDTH * C_CONV // 128


def _pack_small(d):
    rows = [_pad_rows(d[name], r) for name, r in _SMALL]
    used = sum(r for _, r in _SMALL)
    return jnp.concatenate(rows + [jnp.zeros((_SMALL_ROWS - used, 128), F32)], axis=0)


def _unpack_small(packed, like):
    out, r0 = {}, 0
    for name, r in _SMALL:
        n = like[name].size
        out[name] = packed[r0:r0 + r].reshape(-1)[:n].reshape(like[name].shape)
        r0 += r
    return out


def kernel(x, norm_mix_g, w_in, conv_dw_w, conv_dw_b, conv_ln_g, conv_ln_b, w_conv_out, q_norm_g, k_norm_g, sinks, w_attn_out, w_merge_out, norm_ffn_g, w_ffn_in, w_ffn_down, loss_target, m_norm_mix_g, m_w_in, m_conv_dw_w, m_conv_dw_b, m_conv_ln_g, m_conv_ln_b, m_w_conv_out, m_q_norm_g, m_k_norm_g, m_sinks, m_w_attn_out, m_w_merge_out, m_norm_ffn_g, m_w_ffn_in, m_w_ffn_down, v_norm_mix_g, v_w_in, v_conv_dw_w, v_conv_dw_b, v_conv_ln_g, v_conv_ln_b, v_w_conv_out, v_q_norm_g, v_k_norm_g, v_sinks, v_w_attn_out, v_w_merge_out, v_norm_ffn_g, v_w_ffn_in, v_w_ffn_down):
    names = ["norm_mix_g", "w_in", "conv_dw_w", "conv_dw_b", "conv_ln_g", "conv_ln_b", "w_conv_out", "q_norm_g",
             "k_norm_g", "sinks", "w_attn_out", "w_merge_out", "norm_ffn_g", "w_ffn_in", "w_ffn_down"]
    w = dict(zip(names, (norm_mix_g, w_in, conv_dw_w, conv_dw_b, conv_ln_g, conv_ln_b, w_conv_out, q_norm_g, k_norm_g,
                         sinks, w_attn_out, w_merge_out, norm_ffn_g, w_ffn_in, w_ffn_down)))
    m = dict(zip(names, (m_norm_mix_g, m_w_in, m_conv_dw_w, m_conv_dw_b, m_conv_ln_g, m_conv_ln_b, m_w_conv_out,
                         m_q_norm_g, m_k_norm_g, m_sinks, m_w_attn_out, m_w_merge_out, m_norm_ffn_g, m_w_ffn_in,
                         m_w_ffn_down)))
    v = dict(zip(names, (v_norm_mix_g, v_w_in, v_conv_dw_w, v_conv_dw_b, v_conv_ln_g, v_conv_ln_b, v_w_conv_out,
                         v_q_norm_g, v_k_norm_g, v_sinks, v_w_attn_out, v_w_merge_out, v_norm_ffn_g, v_w_ffn_in,
                         v_w_ffn_down)))
    B, S, _ = x.shape
    T = B * S
    me = 4 * lax.axis_index("x") + 2 * lax.axis_index("y") + lax.axis_index("c")

    cw_shard = _pad_rows(conv_dw_w.reshape(CONV_WIDTH, 128), 32)
    g_win, g_cwt = _gather_weights_call([w_in.astype(BF16), cw_shard])
    conv_w_full = g_cwt.reshape(N_DEV, 32 * 128)[:, :CONV_WIDTH * 128].reshape(N_DEV, CONV_WIDTH, 128)
    conv_w_full = conv_w_full.transpose(1, 0, 2).reshape(CONV_WIDTH, C_CONV)
    w_in_i = _to_internal_cols(_cols_from_blocks(g_win))
    rest_names = ["w_conv_out", "w_attn_out", "w_merge_out", "w_ffn_in", "w_ffn_down"]
    rest_shards = [w[k].astype(BF16) for k in rest_names]
    rest_started = _exchange_start_call("gather_rest_start", rest_shards, scatter=False)

    def rest_weights(after):
        lands = _exchange_wait_call("gather_rest_wait", rest_started, False, after)
        full = [lax.dynamic_update_slice(l, s[None], (me, 0, 0)) for l, s in zip(lands, rest_shards)]
        wc_f, wa_f, wm_f = (f.reshape(D_MODEL, D_MODEL) for f in full[:3])
        return wc_f, wa_f, wm_f, _cols_from_blocks(full[3]), full[4].reshape(D_FF, D_MODEL)

    exchanges = []

    def grads_ready(group):
        ks = list(group)
        blocks = [_blocks_by_cols(group[k]) if k in _COL_SHARDED
                  else group[k].reshape(N_DEV, group[k].shape[0] // N_DEV, group[k].shape[1]) for k in ks]
        started = _exchange_start_call("scatter_" + ks[0] + "_start", blocks, scatter=True)
        exchanges.append((ks, started, [lax.dynamic_index_in_dim(b, me, 0, keepdims=False) for b in blocks]))
        return started[-1][0:1, 0:1]

    sse, grad_x, small = _local_step(
        x.reshape(T, D_MODEL), loss_target.reshape(T, D_MODEL), w_in_i, conv_w_full, conv_dw_b, conv_ln_g, conv_ln_b,
        q_norm_g, k_norm_g, sinks, norm_mix_g, norm_ffn_g, rest_started[-1][0:1, 0:1], rest_weights, grads_ready, B, S)
    loss = lax.psum(0.5 / D_MODEL * sse, ("x", "y", "c"))

    grad, delta, new_m, new_v = {}, {}, {}, {}
    packed = jnp.concatenate([_pack_small(small), small["conv_dw_w"].reshape(_CW_ROWS, 128)], axis=0)
    tot = _allsum_small_call(packed)
    g_small = _unpack_small(tot[:_SMALL_ROWS], w)
    d_s, m_s, v_s = _adamw_small_call("adamw_small", tot[:_SMALL_ROWS], _pack_small(w), _pack_small(m), _pack_small(v))
    d_small, m_small, v_small = _unpack_small(d_s, w), _unpack_small(m_s, w), _unpack_small(v_s, w)
    g_cw = lax.dynamic_slice_in_dim(tot[_SMALL_ROWS:].reshape(CONV_WIDTH, C_CONV), me * 128, 128, axis=1)
    cw2 = lambda t: t.reshape(CONV_WIDTH, 128)
    d_cw, m_cw, v_cw = _adamw_small_call("adamw_conv_w", g_cw, cw2(conv_dw_w), cw2(m_conv_dw_w), cw2(v_conv_dw_w))
    cw3 = lambda t: t.reshape(CONV_WIDTH, 1, 128)
    for k, _ in _SMALL:
        grad[k], delta[k], new_m[k], new_v[k] = g_small[k], d_small[k], m_small[k], v_small[k]
    grad["conv_dw_w"], delta["conv_dw_w"], new_m["conv_dw_w"], new_v["conv_dw_w"] = cw3(g_cw), cw3(d_cw), cw3(m_cw), cw3(v_cw)

    after = d_cw
    for ks, started, owns in exchanges:
        lands = _exchange_wait_call("scatter_" + ks[0] + "_wait", started, True, after)
        for k, land, own in zip(ks, lands, owns):
            tr = {1024: 256, 352: 176, 128: 128}[w[k].shape[0]]
            grad[k], delta[k], new_m[k], new_v[k] = _adamw_sum_call("adamw_" + k, land, own, w[k], m[k], v[k], tr)
            after = new_v[k]

    return (loss, grad_x.reshape(B, S, D_MODEL), *[grad[k] for k in names], *[delta[k] for k in names],
            *[new_m[k] for k in names], *[new_v[k] for k in names])
```

```python
import functools
import math

import jax
import jax.numpy as jnp
from jax import lax
from jax.experimental import pallas as pl
from jax.experimental.pallas import tpu as pltpu

F32 = jnp.float32
BF16 = jnp.bfloat16
MESH = pl.DeviceIdType.MESH

N_DEV = 8
D_MODEL = 1024
C_CONV = 1024
CONV_WIDTH = 31
HEAD_DIM = 64
N_Q_HEADS = 16
N_KV_HEADS = 2
GROUP = N_Q_HEADS // N_KV_HEADS
WINDOW = 128
BLOCK = 128
D_FF = 2816
EPS = 1e-6
NEG = -1e30
ATTN_SCALE = 1.0 / math.sqrt(HEAD_DIM)
Q_W = N_Q_HEADS * HEAD_DIM
KV_W = N_KV_HEADS * HEAD_DIM
IN_COLS = 2 * C_CONV + Q_W + 2 * KV_W + 2 * D_MODEL

ADAM_LR = 0.001
ADAM_B1 = 0.9
ADAM_B2 = 0.999
ADAM_EPS = 1e-08
ADAM_WD = 0.01
ADAM_STEP = 10

COL_A, COL_GATE, COL_GC, COL_GA, COL_Q, COL_K, COL_V = 0, 1024, 2048, 3072, 4096, 5120, 5248

HALO = 32
VMEM_LIMIT = 56 * 1024 * 1024


def _cp(*sem):
    return pltpu.CompilerParams(dimension_semantics=sem, vmem_limit_bytes=VMEM_LIMIT)


def _sigmoid(x):
    return jax.nn.sigmoid(x)


def _dot(a, b):
    return jnp.dot(a, b, preferred_element_type=F32)


def _dot_nt(a, b):
    return lax.dot_general(a, b, (((1,), (1,)), ((), ())), preferred_element_type=F32)


def _dot_tn(a, b):
    return lax.dot_general(a, b, (((0,), (0,)), ((), ())), preferred_element_type=F32)


def _full(shape):
    return pl.BlockSpec(shape, lambda *_: (0,) * len(shape))


def _rows(tm, ncols, colblk=0):
    return pl.BlockSpec((tm, ncols), lambda i: (i, colblk))


def _inproj_call(x, g, w, tm=256):
    T = x.shape[0]
    N = w.shape[1]

    def body(x_ref, g_ref, w_ref, u_ref, xn_ref):
        xv = x_ref[...]
        r = lax.rsqrt(jnp.mean(xv * xv, axis=-1, keepdims=True) + EPS)
        xn = (xv * r * g_ref[...]).astype(BF16)
        xn_ref[...] = xn
        u_ref[...] = _dot(xn, w_ref[...])

    return pl.pallas_call(
        body, name="inproj", grid=(T // tm,),
        in_specs=[_rows(tm, D_MODEL), _full((1, D_MODEL)), _full(w.shape)],
        out_specs=[_rows(tm, N), _rows(tm, D_MODEL)],
        out_shape=[jax.ShapeDtypeStruct((T, N), F32), jax.ShapeDtypeStruct((T, D_MODEL), BF16)],
        compiler_params=_cp("parallel"),
    )(x, g, w)


def _fill_shifted(src_ref, sh_ref):
    n = src_ref.shape[0] - 8
    for s in range(1, 8):
        sh_ref[s - 1, 0:n, :] = src_ref[s:s + n, :]


def _window(src_ref, sh_ref, off, cols):
    s = off % 8
    if s == 0:
        return src_ref[off:off + 128, cols]
    return sh_ref[s - 1, off - s:off - s + 128, cols]


def _conv_taps(w_ref, src_ref, sh_ref, base, rows, cb, reverse):
    cols = pl.ds(pl.multiple_of(cb * 128, 128), 128)
    outs = []
    for rb in range(rows // 128):
        acc = jnp.zeros((128, 128), F32)
        for j in range(CONV_WIDTH):
            off = base + (CONV_WIDTH - 1 - j if reverse else j) + rb * 128
            acc = acc + w_ref[j:j + 1, cols] * _window(src_ref, sh_ref, off, cols)
        outs.append(acc)
    return outs


def _conv_fwd_call(u, conv_w, conv_b, ln_g, ln_b, B, S, ts=256):
    T = B * S
    nS = S // ts
    per32 = ts // HALO

    def body(a_ref, gate_ref, ap_ref, gp_ref, w_ref, b_ref, g_ref, bb_ref, hc_ref, c_ref, hext, hsh, acc_ref):
        s = pl.program_id(1)
        prev = ap_ref[...] * _sigmoid(gp_ref[...])
        hext[0:HALO, :] = jnp.where(s > 0, prev, 0.0)
        hext[HALO:, :] = a_ref[...] * _sigmoid(gate_ref[...])
        _fill_shifted(hext, hsh)

        def colblock(cb, carry):
            cols = pl.ds(pl.multiple_of(cb * 128, 128), 128)
            outs = _conv_taps(w_ref, hext, hsh, HALO - (CONV_WIDTH - 1), ts, cb, reverse=False)
            for rb, acc in enumerate(outs):
                acc_ref[rb * 128:(rb + 1) * 128, cols] = acc
            return carry

        lax.fori_loop(0, C_CONV // 128, colblock, 0)
        hc = acc_ref[...] + b_ref[...]
        hc_ref[...] = hc
        mu = jnp.mean(hc, axis=-1, keepdims=True)
        xc = hc - mu
        var = jnp.mean(xc * xc, axis=-1, keepdims=True)
        y = xc * lax.rsqrt(var + EPS) * g_ref[...] + bb_ref[...]
        c_ref[...] = (y * _sigmoid(y)).astype(BF16)

    cur = lambda cb: pl.BlockSpec((ts, C_CONV), lambda b, s: (b * nS + s, cb))
    prv = lambda cb: pl.BlockSpec((HALO, C_CONV), lambda b, s: (jnp.maximum(b * (S // HALO) + s * per32 - 1, 0), cb))
    vec = _full((1, C_CONV))
    return pl.pallas_call(
        body, name="conv_fwd", grid=(B, nS),
        in_specs=[cur(0), cur(1), prv(0), prv(1), _full((CONV_WIDTH, C_CONV)), vec, vec, vec],
        out_specs=[pl.BlockSpec((ts, C_CONV), lambda b, s: (b * nS + s, 0))] * 2,
        out_shape=[jax.ShapeDtypeStruct((T, C_CONV), F32), jax.ShapeDtypeStruct((T, C_CONV), BF16)],
        scratch_shapes=[pltpu.VMEM((HALO + ts, C_CONV), F32), pltpu.VMEM((7, HALO + ts, C_CONV), F32),
                        pltpu.VMEM((ts, C_CONV), F32)],
        compiler_params=_cp("parallel", "arbitrary"),
    )(u, u, u, u, conv_w, conv_b, ln_g, ln_b)


def _lane_lo():
    return lax.broadcasted_iota(jnp.int32, (1, 128), 1) < HEAD_DIM


def _half_sums(t, lo):
    s_lo = jnp.sum(jnp.where(lo, t, 0.0), axis=-1, keepdims=True)
    s_hi = jnp.sum(jnp.where(lo, 0.0, t), axis=-1, keepdims=True)
    return jnp.where(lo, s_lo, s_hi)


def _head_rstd(t, lo):
    return lax.rsqrt(_half_sums(t * t, lo) * (1.0 / HEAD_DIM) + EPS)


SM_ROWS = 32


def _fill_bias(btab, kvh, slopes_ref):
    qi = lax.broadcasted_iota(jnp.int32, (BLOCK, 2 * BLOCK), 0)
    sj = lax.broadcasted_iota(jnp.int32, (BLOCK, 2 * BLOCK), 1)
    dist = qi + BLOCK - sj
    valid = (dist >= 0) & (dist < WINDOW)
    distf = dist.astype(F32)
    for g in range(GROUP):
        bias = jnp.where(valid, -slopes_ref[kvh * GROUP + g] * distf, NEG)
        btab[1, g] = bias
        btab[0, g] = jnp.where(sj >= BLOCK, bias, NEG)


def _softmax_rows(s_ref, btab, p, var, g, hf, rc, sk):
    rows = slice(rc * SM_ROWS, (rc + 1) * SM_ROWS)
    srows = slice(BLOCK * p + rc * SM_ROWS, BLOCK * p + (rc + 1) * SM_ROWS)
    sh = s_ref[srows, 2 * BLOCK * hf:2 * BLOCK * (hf + 1)] + btab[var, g, rows, :]
    m = jnp.maximum(jnp.max(sh, axis=-1, keepdims=True), sk)
    e = jnp.exp(sh - m)
    esk = jnp.exp(sk - m)
    rz = 1.0 / (jnp.sum(e, axis=-1, keepdims=True) + esk)
    return e * rz, esk, rz


def _stage_kv(kvh, k_ref, v_ref, kg_ref, lo, kA, kB, vA, vB):
    kv = k_ref[...]
    rk = _head_rstd(kv, lo)
    kn = kv * rk * kg_ref[...]
    mine = lax.broadcasted_iota(jnp.int32, (1, 128), 1) // HEAD_DIM == kvh
    ksel = jnp.where(mine, kn, 0.0)
    vsel = jnp.where(mine, v_ref[...], 0.0)
    krol = pltpu.roll(ksel, HEAD_DIM, 1)
    vrol = pltpu.roll(vsel, HEAD_DIM, 1)
    first = kvh == 0
    zeros = jnp.zeros((BLOCK, 128), BF16)
    for ref, val in ((kA, jnp.where(first, ksel, krol)), (kB, jnp.where(first, krol, ksel)),
                     (vA, jnp.where(first, vsel, vrol)), (vB, jnp.where(first, vrol, vsel))):
        ref[0:BLOCK, :] = zeros
        ref[BLOCK:, :] = val.astype(BF16)


def _attn_fwd_call(u, qg2, kg2, sinks, slopes, B, S):
    T = B * S
    nb = S // BLOCK
    half = Q_W // 2
    npair = GROUP // 2

    def body(sinks_ref, slopes_ref, q_ref, k_ref, v_ref, qg_ref, kg_ref, o_ref, kA, kB, vA, vB, btab, qn_scr, s_scr, p_scr):
        kvh = pl.program_id(1)
        lo = _lane_lo()
        _stage_kv(kvh, k_ref, v_ref, kg_ref, lo, kA, kB, vA, vB)
        _fill_bias(btab, kvh, slopes_ref)
        qgs = qg_ref[...] * ATTN_SCALE

        def blk(n, carry):
            r0 = pl.multiple_of(n * BLOCK, BLOCK)
            win = pl.ds(r0, 2 * BLOCK)
            k2t = jnp.concatenate([kA[win, :], kB[win, :]], axis=0)
            v2 = jnp.concatenate([vA[win, :], vB[win, :]], axis=0)
            var = jnp.minimum(n, 1)
            for p in range(npair):
                pc = slice(128 * p, 128 * (p + 1))
                qp = q_ref[pl.ds(r0, BLOCK), pc]
                qn_scr[pc, :] = (qp * _head_rstd(qp, lo) * qgs).astype(BF16)
            s_scr[...] = _dot_nt(qn_scr[...], k2t)
            for p in range(npair):
                for hf in range(2):
                    g = 2 * p + hf
                    sk = sinks_ref[kvh * GROUP + g]
                    for rc in range(BLOCK // SM_ROWS):
                        pn, _, _ = _softmax_rows(s_scr, btab, p, var, g, hf, rc, sk)
                        rr = slice(BLOCK * p + rc * SM_ROWS, BLOCK * p + (rc + 1) * SM_ROWS)
                        p_scr[rr, 2 * BLOCK * hf:2 * BLOCK * (hf + 1)] = pn.astype(BF16)
            o_all = _dot(p_scr[...], v2)
            for p in range(npair):
                pc = slice(128 * p, 128 * (p + 1))
                o_ref[pl.ds(r0, BLOCK), pc] = o_all[pc, :].astype(BF16)
            return carry

        lax.fori_loop(0, nb, blk, 0)

    smem = pl.BlockSpec(memory_space=pltpu.SMEM)
    kvbuf = pltpu.VMEM((S + BLOCK, 128), BF16)
    pair_scores = (npair * BLOCK, 4 * BLOCK)
    return pl.pallas_call(
        body, name="attn_fwd", grid=(B, N_KV_HEADS),
        in_specs=[smem, smem,
                  pl.BlockSpec((S, half), lambda b, h: (b, COL_Q // half + h)),
                  pl.BlockSpec((S, 128), lambda b, h: (b, COL_K // 128)),
                  pl.BlockSpec((S, 128), lambda b, h: (b, COL_V // 128)),
                  _full((1, 128)), _full((1, 128))],
        out_specs=pl.BlockSpec((S, half), lambda b, h: (b, h)),
        out_shape=jax.ShapeDtypeStruct((T, Q_W), BF16),
        scratch_shapes=[kvbuf, kvbuf, kvbuf, kvbuf, pltpu.VMEM((2, GROUP, BLOCK, 2 * BLOCK), F32),
                        pltpu.VMEM((npair * BLOCK, 128), BF16), pltpu.VMEM(pair_scores, F32), pltpu.VMEM(pair_scores, BF16)],
        compiler_params=_cp("parallel", "arbitrary"),
    )(sinks, slopes, u, u, u, qg2, kg2)


def _merge_call(c, o, u, x, wc, wa, wm, g_ffn, tm=256):
    T = x.shape[0]

    def body(c_ref, o_ref, gc_ref, ga_ref, x_ref, wc_ref, wa_ref, wm_ref, g_ref, yc_ref, ya_ref, z_ref, h_ref, hn_ref):
        yc = _dot(c_ref[...], wc_ref[...])
        ya = _dot(o_ref[...], wa_ref[...])
        yc_ref[...] = yc
        ya_ref[...] = ya
        z = (_sigmoid(gc_ref[...]) * yc + _sigmoid(ga_ref[...]) * ya).astype(BF16)
        z_ref[...] = z
        h = x_ref[...] + _dot(z, wm_ref[...])
        h_ref[...] = h
        r = lax.rsqrt(jnp.mean(h * h, axis=-1, keepdims=True) + EPS)
        hn_ref[...] = (h * r * g_ref[...]).astype(BF16)

    sq = _full((D_MODEL, D_MODEL))
    row = _rows(tm, D_MODEL)
    f32 = jax.ShapeDtypeStruct((T, D_MODEL), F32)
    b16 = jax.ShapeDtypeStruct((T, D_MODEL), BF16)
    return pl.pallas_call(
        body, name="merge_fwd", grid=(T // tm,),
        in_specs=[row, row, _rows(tm, D_MODEL, COL_GC // D_MODEL), _rows(tm, D_MODEL, COL_GA // D_MODEL), row,
                  sq, sq, sq, _full((1, D_MODEL))],
        out_specs=[row] * 5,
        out_shape=[f32, f32, b16, f32, b16],
        compiler_params=_cp("parallel"),
    )(c, o, u, u, x, wc, wa, wm, g_ffn)


def _ffn_in_call(hn, w, tm=256):
    T = hn.shape[0]

    def body(hn_ref, w_ref, gu_ref, act_ref):
        gu = _dot(hn_ref[...], w_ref[...])
        gu_ref[...] = gu
        g = gu[:, :D_FF]
        act_ref[...] = (g * _sigmoid(g) * gu[:, D_FF:]).astype(BF16)

    return pl.pallas_call(
        body, name="ffn_in", grid=(T // tm,),
        in_specs=[_rows(tm, D_MODEL), _full(w.shape)],
        out_specs=[_rows(tm, 2 * D_FF), _rows(tm, D_FF)],
        out_shape=[jax.ShapeDtypeStruct((T, 2 * D_FF), F32), jax.ShapeDtypeStruct((T, D_FF), BF16)],
        compiler_params=_cp("parallel"),
    )(hn, w)


def _ffn_out_call(act, wd, h, tgt, tm=512):
    T = h.shape[0]

    def body(act_ref, wd_ref, h_ref, t_ref, sse_ref, d_ref, db_ref):
        e = h_ref[...] + _dot(act_ref[...], wd_ref[...]) - t_ref[...]

        @pl.when(pl.program_id(0) == 0)
        def _():
            sse_ref[...] = jnp.zeros_like(sse_ref)

        sse_ref[...] += jnp.sum(e * e)
        d = e * (1.0 / D_MODEL)
        d_ref[...] = d
        db_ref[...] = d.astype(BF16)

    row = _rows(tm, D_MODEL)
    return pl.pallas_call(
        body, name="ffn_out", grid=(T // tm,),
        in_specs=[_rows(tm, D_FF), _full(wd.shape), row, row],
        out_specs=[_full((8, 128)), row, row],
        out_shape=[jax.ShapeDtypeStruct((8, 128), F32), jax.ShapeDtypeStruct((T, D_MODEL), F32),
                   jax.ShapeDtypeStruct((T, D_MODEL), BF16)],
        compiler_params=_cp("arbitrary"),
    )(act, wd, h, tgt)


def _dw_call(name, a, b, tn, tk=1024):
    T, K = a.shape
    N = b.shape[1]
    tk = min(tk, T)
    nk = T // tk

    def body(a_ref, b_ref, o_ref, acc_ref):
        k = pl.program_id(1)

        @pl.when(k == 0)
        def _():
            acc_ref[...] = jnp.zeros_like(acc_ref)

        acc_ref[...] += _dot_tn(a_ref[...], b_ref[...])

        @pl.when(k == nk - 1)
        def _():
            o_ref[...] = acc_ref[...].astype(BF16)

    return pl.pallas_call(
        body, name=name, grid=(N // tn, nk),
        in_specs=[pl.BlockSpec((tk, K), lambda j, k: (k, 0)), pl.BlockSpec((tk, tn), lambda j, k: (k, j))],
        out_specs=pl.BlockSpec((K, tn), lambda j, k: (0, j)),
        out_shape=jax.ShapeDtypeStruct((K, N), BF16),
        scratch_shapes=[pltpu.VMEM((K, tn), F32)],
        compiler_params=_cp("parallel", "arbitrary"),
    )(a, b)


def _rms_bwd(dy, xv, g):
    r = lax.rsqrt(jnp.mean(xv * xv, axis=-1, keepdims=True) + EPS)
    t = dy * g
    dx = r * t - xv * (r * r * r * jnp.mean(t * xv, axis=-1, keepdims=True))
    return dx, dy * xv * r


def _ffn_down_bwd_call(dout_b, wd, gu, tm=256):
    T = dout_b.shape[0]

    def body(d_ref, wd_ref, g_ref, up_ref, o_ref):
        dact = _dot_nt(d_ref[...], wd_ref[...])
        g = g_ref[...]
        sg = _sigmoid(g)
        o_ref[:, :D_FF] = (dact * up_ref[...] * (sg * (1.0 + g * (1.0 - sg)))).astype(BF16)
        o_ref[:, D_FF:] = (dact * (g * sg)).astype(BF16)

    return pl.pallas_call(
        body, name="ffn_down_bwd", grid=(T // tm,),
        in_specs=[_rows(tm, D_MODEL), _full(wd.shape), _rows(tm, D_FF, 0), _rows(tm, D_FF, 1)],
        out_specs=_rows(tm, 2 * D_FF),
        out_shape=jax.ShapeDtypeStruct((T, 2 * D_FF), BF16),
        compiler_params=_cp("parallel"),
    )(dout_b, wd, gu, gu)


def _ffn_in_bwd_call(dgu, w, h, g_ffn, dout, tm=256):
    T = h.shape[0]

    def body(dgu_ref, w_ref, h_ref, g_ref, d_ref, dh_ref, dhb_ref, dg_ref):
        dhn = _dot_nt(dgu_ref[...], w_ref[...])
        dx, dgt = _rms_bwd(dhn, h_ref[...], g_ref[...])
        dh = d_ref[...] + dx
        dh_ref[...] = dh
        dhb_ref[...] = dh.astype(BF16)

        @pl.when(pl.program_id(0) == 0)
        def _():
            dg_ref[...] = jnp.zeros_like(dg_ref)

        dg_ref[...] += jnp.sum(dgt, axis=0, keepdims=True)

    row = _rows(tm, D_MODEL)
    return pl.pallas_call(
        body, name="ffn_in_bwd", grid=(T // tm,),
        in_specs=[_rows(tm, 2 * D_FF), _full(w.shape), row, _full((1, D_MODEL)), row],
        out_specs=[row, row, _full((1, D_MODEL))],
        out_shape=[jax.ShapeDtypeStruct((T, D_MODEL), F32), jax.ShapeDtypeStruct((T, D_MODEL), BF16),
                   jax.ShapeDtypeStruct((1, D_MODEL), F32)],
        compiler_params=_cp("arbitrary"),
    )(dgu, w, h, g_ffn, dout)


def _merge_bwd_call(dh_b, wm, wc, wa, u, yc, ya, tm=256):
    T = dh_b.shape[0]

    def body(dh_ref, wm_ref, wc_ref, wa_ref, gc_ref, ga_ref, yc_ref, ya_ref, dyc_ref, dya_ref, dug_ref, dcp_ref, do_ref):
        dz = _dot_nt(dh_ref[...], wm_ref[...])
        sgc = _sigmoid(gc_ref[...])
        sga = _sigmoid(ga_ref[...])
        dyc = (dz * sgc).astype(BF16)
        dya = (dz * sga).astype(BF16)
        dyc_ref[...] = dyc
        dya_ref[...] = dya
        dug_ref[:, :D_MODEL] = (dz * yc_ref[...] * (sgc * (1.0 - sgc))).astype(BF16)
        dug_ref[:, D_MODEL:] = (dz * ya_ref[...] * (sga * (1.0 - sga))).astype(BF16)
        dcp_ref[...] = _dot_nt(dyc, wc_ref[...])
        do_ref[...] = _dot_nt(dya, wa_ref[...]).astype(BF16)

    sq = _full((D_MODEL, D_MODEL))
    row = _rows(tm, D_MODEL)
    b16 = jax.ShapeDtypeStruct((T, D_MODEL), BF16)
    return pl.pallas_call(
        body, name="merge_bwd", grid=(T // tm,),
        in_specs=[row, sq, sq, sq, _rows(tm, D_MODEL, COL_GC // D_MODEL), _rows(tm, D_MODEL, COL_GA // D_MODEL), row, row],
        out_specs=[row, row, _rows(tm, 2 * D_MODEL), row, row],
        out_shape=[b16, b16, jax.ShapeDtypeStruct((T, 2 * D_MODEL), BF16), jax.ShapeDtypeStruct((T, D_MODEL), F32), b16],
        compiler_params=_cp("parallel"),
    )(dh_b, wm, wc, wa, u, u, yc, ya)


def _conv_bwd_call(dcp, hc, u, conv_w, ln_g, ln_b, B, S, ts=256):
    T = B * S
    nS = S // ts
    per32 = ts // HALO
    last32 = T // HALO - 1

    def ln_bwd(dcp_v, hc_v, g, bb):
        mu = jnp.mean(hc_v, axis=-1, keepdims=True)
        xc = hc_v - mu
        rstd = lax.rsqrt(jnp.mean(xc * xc, axis=-1, keepdims=True) + EPS)
        xhat = xc * rstd
        y = xhat * g + bb
        sy = _sigmoid(y)
        dy = dcp_v * (sy * (1.0 + y * (1.0 - sy)))
        dxh = dy * g
        dhc = rstd * (dxh - jnp.mean(dxh, axis=-1, keepdims=True) - xhat * jnp.mean(dxh * xhat, axis=-1, keepdims=True))
        return dhc, dy * xhat, dy

    def body(dcp_ref, dcpn_ref, hc_ref, hcn_ref, a_ref, gate_ref, ap_ref, gp_ref, w_ref, g_ref, bb_ref,
             du_ref, dw_ref, db_ref, dlg_ref, dlb_ref, dhext, hgext, dhsh, hgsh, acc_ref):
        b = pl.program_id(0)
        s = pl.program_id(1)

        @pl.when((b == 0) & (s == 0))
        def _():
            dw_ref[...] = jnp.zeros_like(dw_ref)
            db_ref[...] = jnp.zeros_like(db_ref)
            dlg_ref[...] = jnp.zeros_like(dlg_ref)
            dlb_ref[...] = jnp.zeros_like(dlb_ref)

        dhc, dgt, dbt = ln_bwd(dcp_ref[...], hc_ref[...], g_ref[...], bb_ref[...])
        dhn, _, _ = ln_bwd(dcpn_ref[...], hcn_ref[...], g_ref[...], bb_ref[...])
        dhext[0:ts, :] = dhc
        dhext[ts:, :] = jnp.where(s < nS - 1, dhn, 0.0)
        db_ref[...] += jnp.sum(dhc, axis=0, keepdims=True)
        dlg_ref[...] += jnp.sum(dgt, axis=0, keepdims=True)
        dlb_ref[...] += jnp.sum(dbt, axis=0, keepdims=True)

        sg = _sigmoid(gate_ref[...])
        av = a_ref[...]
        hgext[0:HALO, :] = jnp.where(s > 0, ap_ref[...] * _sigmoid(gp_ref[...]), 0.0)
        hgext[HALO:, :] = av * sg
        _fill_shifted(dhext, dhsh)
        _fill_shifted(hgext, hgsh)

        def colblock(cb, carry):
            cols = pl.ds(pl.multiple_of(cb * 128, 128), 128)
            outs = _conv_taps(w_ref, dhext, dhsh, 0, ts, cb, reverse=True)
            for rb, acc in enumerate(outs):
                acc_ref[rb * 128:(rb + 1) * 128, cols] = acc
            dhc_rows = [dhext[rb * 128:(rb + 1) * 128, cols] for rb in range(ts // 128)]
            for j in range(CONV_WIDTH):
                part = jnp.zeros((8, 128), F32)
                for rb in range(ts // 128):
                    glu = _window(hgext, hgsh, HALO - (CONV_WIDTH - 1) + j + rb * 128, cols)
                    part = part + jnp.sum((dhc_rows[rb] * glu).reshape(16, 8, 128), axis=0)
                dw_ref[8 * j:8 * (j + 1), cols] += part
            return carry

        lax.fori_loop(0, C_CONV // 128, colblock, 0)
        dglu = acc_ref[...]
        du_ref[:, :C_CONV] = (dglu * sg).astype(BF16)
        du_ref[:, C_CONV:] = (dglu * av * (sg * (1.0 - sg))).astype(BF16)

    cur = lambda cb: pl.BlockSpec((ts, C_CONV), lambda b, s: (b * nS + s, cb))
    prv = lambda cb: pl.BlockSpec((HALO, C_CONV), lambda b, s: (jnp.maximum(b * (S // HALO) + s * per32 - 1, 0), cb))
    nxt = pl.BlockSpec((HALO, C_CONV), lambda b, s: (jnp.minimum(b * (S // HALO) + (s + 1) * per32, last32), 0))
    vec = _full((1, C_CONV))
    return pl.pallas_call(
        body, name="conv_bwd", grid=(B, nS),
        in_specs=[cur(0), nxt, cur(0), nxt, cur(0), cur(1), prv(0), prv(1), _full((CONV_WIDTH, C_CONV)), vec, vec],
        out_specs=[pl.BlockSpec((ts, 2 * C_CONV), lambda b, s: (b * nS + s, 0)), _full((8 * CONV_WIDTH, C_CONV)), vec, vec, vec],
        out_shape=[jax.ShapeDtypeStruct((T, 2 * C_CONV), BF16), jax.ShapeDtypeStruct((8 * CONV_WIDTH, C_CONV), F32),
                   jax.ShapeDtypeStruct((1, C_CONV), F32), jax.ShapeDtypeStruct((1, C_CONV), F32),
                   jax.ShapeDtypeStruct((1, C_CONV), F32)],
        scratch_shapes=[pltpu.VMEM((ts + HALO, C_CONV), F32), pltpu.VMEM((HALO + ts, C_CONV), F32),
                        pltpu.VMEM((7, ts + HALO, C_CONV), F32), pltpu.VMEM((7, HALO + ts, C_CONV), F32),
                        pltpu.VMEM((ts, C_CONV), F32)],
        compiler_params=_cp("arbitrary", "arbitrary"),
    )(dcp, dcp, hc, hc, u, u, u, u, conv_w, ln_g, ln_b)


def _attn_bwd_call(u, do, qg2, kg2, sinks, slopes, B, S):
    T = B * S
    nb = S // BLOCK
    half = Q_W // 2
    npair = GROUP // 2

    def body(sinks_ref, slopes_ref, q_ref, k_ref, v_ref, do_ref, qg_ref, kg_ref,
             dq_ref, dkv_ref, dqg_ref, dkg_ref, dsk_ref, kA, kB, vA, vB, dkn_acc, dv_acc, dkv_keep,
             btab, s_scr, dp_scr, p_scr, ds_scr, qn_scr, do_scr, rq_scr):
        b = pl.program_id(0)
        kvh = pl.program_id(1)
        lo = _lane_lo()
        lane = lax.broadcasted_iota(jnp.int32, (1, 128), 1)
        first = kvh == 0

        @pl.when((b == 0) & first)
        def _():
            dqg_ref[...] = jnp.zeros_like(dqg_ref)
            dkg_ref[...] = jnp.zeros_like(dkg_ref)
            dsk_ref[...] = jnp.zeros_like(dsk_ref)

        _stage_kv(kvh, k_ref, v_ref, kg_ref, lo, kA, kB, vA, vB)
        dkn_acc[...] = jnp.zeros_like(dkn_acc)
        dv_acc[...] = jnp.zeros_like(dv_acc)
        _fill_bias(btab, kvh, slopes_ref)
        qgs = qg_ref[...] * ATTN_SCALE

        def fold(acc):
            canon = jnp.where(lo, acc[0:2 * BLOCK] + pltpu.roll(acc[2 * BLOCK:], HEAD_DIM, 1), 0.0)
            return jnp.where(first, canon, pltpu.roll(canon, HEAD_DIM, 1))

        def blk(n, carry):
            r0 = pl.multiple_of(n * BLOCK, BLOCK)
            win = pl.ds(r0, 2 * BLOCK)
            rows = pl.ds(r0, BLOCK)
            k2t = jnp.concatenate([kA[win, :], kB[win, :]], axis=0)
            v2 = jnp.concatenate([vA[win, :], vB[win, :]], axis=0)
            var = jnp.minimum(n, 1)
            for p in range(npair):
                pc = slice(128 * p, 128 * (p + 1))
                qp = q_ref[rows, pc]
                rq = _head_rstd(qp, lo)
                rq_scr[pc, :] = rq
                qn_scr[pc, :] = (qp * rq * qgs).astype(BF16)
                do_scr[pc, :] = do_ref[rows, pc]
            s_scr[...] = _dot_nt(qn_scr[...], k2t)
            dp_scr[...] = _dot_nt(do_scr[...], v2)
            for p in range(npair):
                for hf in range(2):
                    g = 2 * p + hf
                    hc = slice(2 * BLOCK * hf, 2 * BLOCK * (hf + 1))
                    sk = sinks_ref[kvh * GROUP + g]
                    sink_rows = jnp.zeros((SM_ROWS, 1), F32)
                    for rc in range(BLOCK // SM_ROWS):
                        pn, esk, rz = _softmax_rows(s_scr, btab, p, var, g, hf, rc, sk)
                        rr = slice(BLOCK * p + rc * SM_ROWS, BLOCK * p + (rc + 1) * SM_ROWS)
                        dph = dp_scr[rr, hc]
                        delta = jnp.sum(pn * dph, axis=-1, keepdims=True)
                        ds_scr[rr, hc] = (pn * (dph - delta)).astype(BF16)
                        p_scr[rr, hc] = pn.astype(BF16)
                        sink_rows = sink_rows + esk * rz * delta
                    dsk_ref[...] += jnp.where(lane == kvh * GROUP + g, -jnp.sum(sink_rows), 0.0)
            dqs_all = _dot(ds_scr[...], k2t)
            for p in range(npair):
                pc = slice(128 * p, 128 * (p + 1))
                qp = q_ref[rows, pc]
                rq = rq_scr[pc, :]
                dqs = dqs_all[pc, :]
                t = dqs * qgs
                dq = rq * t - qp * (rq * rq * rq * (_half_sums(t * qp, lo) * (1.0 / HEAD_DIM)))
                dq_ref[rows, pc] = dq.astype(BF16)
                dqg_ref[...] += jnp.sum(dqs * qp * rq, axis=0, keepdims=True) * ATTN_SCALE
            dkn_acc[win, :] += fold(_dot_tn(ds_scr[...], qn_scr[...]))
            dv_acc[win, :] += fold(_dot_tn(p_scr[...], do_scr[...]))
            return carry

        lax.fori_loop(0, nb, blk, 0)

        dkn = dkn_acc[BLOCK:, :]
        kv = k_ref[...]
        kg = kg_ref[...]
        rk = _head_rstd(kv, lo)
        t = dkn * kg
        dk = rk * t - kv * (rk * rk * rk * (_half_sums(t * kv, lo) * (1.0 / HEAD_DIM)))
        dkg_ref[...] += jnp.sum(dkn * kv * rk, axis=0, keepdims=True)
        dv = dv_acc[BLOCK:, :]

        @pl.when(first)
        def _():
            dkv_keep[:, :128] = dk
            dkv_keep[:, 128:] = dv

        @pl.when(kvh == N_KV_HEADS - 1)
        def _():
            dkv_ref[:, :128] = (dkv_keep[:, :128] + dk).astype(BF16)
            dkv_ref[:, 128:] = (dkv_keep[:, 128:] + dv).astype(BF16)

    smem = pl.BlockSpec(memory_space=pltpu.SMEM)
    kvbuf = pltpu.VMEM((S + BLOCK, 128), BF16)
    accbuf = pltpu.VMEM((S + BLOCK, 128), F32)
    vec = _full((1, 128))
    return pl.pallas_call(
        body, name="attn_bwd", grid=(B, N_KV_HEADS),
        in_specs=[smem, smem,
                  pl.BlockSpec((S, half), lambda b, h: (b, COL_Q // half + h)),
                  pl.BlockSpec((S, 128), lambda b, h: (b, COL_K // 128)),
                  pl.BlockSpec((S, 128), lambda b, h: (b, COL_V // 128)),
                  pl.BlockSpec((S, half), lambda b, h: (b, h)),
                  vec, vec],
        out_specs=[pl.BlockSpec((S, half), lambda b, h: (b, h)), pl.BlockSpec((S, 256), lambda b, h: (b, 0)), vec, vec, vec],
        out_shape=[jax.ShapeDtypeStruct((T, Q_W), BF16), jax.ShapeDtypeStruct((T, 2 * KV_W), BF16),
                   jax.ShapeDtypeStruct((1, 128), F32), jax.ShapeDtypeStruct((1, 128), F32), jax.ShapeDtypeStruct((1, 128), F32)],
        scratch_shapes=[kvbuf, kvbuf, kvbuf, kvbuf, accbuf, accbuf, pltpu.VMEM((S, 256), F32),
                        pltpu.VMEM((2, GROUP, BLOCK, 2 * BLOCK), F32),
                        pltpu.VMEM((npair * BLOCK, 4 * BLOCK), F32), pltpu.VMEM((npair * BLOCK, 4 * BLOCK), F32),
                        pltpu.VMEM((npair * BLOCK, 4 * BLOCK), BF16), pltpu.VMEM((npair * BLOCK, 4 * BLOCK), BF16),
                        pltpu.VMEM((npair * BLOCK, 128), BF16), pltpu.VMEM((npair * BLOCK, 128), BF16),
                        pltpu.VMEM((npair * BLOCK, 128), F32)],
        compiler_params=_cp("arbitrary", "arbitrary"),
    )(sinks, slopes, u, u, u, do, qg2, kg2)


def _inproj_bwd_call(du_conv, du_g, du_q, du_kv, w, x, g, dh, tm=256):
    T = x.shape[0]

    def body(dc_ref, dg_ref, dq_ref, dkv_ref, w_ref, x_ref, g_ref, dh_ref, gx_ref, dgm_ref):
        dxn = _dot_nt(dc_ref[...], w_ref[:, COL_A:COL_GC])
        dxn += _dot_nt(dg_ref[...], w_ref[:, COL_GC:COL_Q])
        dxn += _dot_nt(dq_ref[...], w_ref[:, COL_Q:COL_K])
        dxn += _dot_nt(dkv_ref[...], w_ref[:, COL_K:])
        dx, dgt = _rms_bwd(dxn, x_ref[...], g_ref[...])
        gx_ref[...] = dh_ref[...] + dx

        @pl.when(pl.program_id(0) == 0)
        def _():
            dgm_ref[...] = jnp.zeros_like(dgm_ref)

        dgm_ref[...] += jnp.sum(dgt, axis=0, keepdims=True)

    row = _rows(tm, D_MODEL)
    return pl.pallas_call(
        body, name="inproj_bwd", grid=(T // tm,),
        in_specs=[_rows(tm, 2 * C_CONV), _rows(tm, 2 * D_MODEL), _rows(tm, Q_W), _rows(tm, 2 * KV_W), _full(w.shape),
                  row, _full((1, D_MODEL)), row],
        out_specs=[row, _full((1, D_MODEL))],
        out_shape=[jax.ShapeDtypeStruct((T, D_MODEL), F32), jax.ShapeDtypeStruct((1, D_MODEL), F32)],
        compiler_params=_cp("arbitrary"),
    )(du_conv, du_g, du_q, du_kv, w, x, g, dh)


def _adamw_math(w, g, m, v):
    m = ADAM_B1 * m + (1.0 - ADAM_B1) * g
    v = ADAM_B2 * v + (1.0 - ADAM_B2) * (g * g)
    m_hat = m / (1.0 - ADAM_B1 ** ADAM_STEP)
    v_hat = v / (1.0 - ADAM_B2 ** ADAM_STEP)
    delta = -ADAM_LR * (m_hat / (jnp.sqrt(v_hat) + ADAM_EPS) + ADAM_WD * w)
    return delta, m, v


def _adamw_sum_call(name, parts, own, w, m, v, tr):
    R, C = w.shape

    def body(p_ref, own_ref, w_ref, m_ref, v_ref, g_ref, d_ref, nm_ref, nv_ref):
        x, y, c = _position()
        me = 4 * x + 2 * y + c
        g = jnp.zeros((tr, C), F32)
        for s in range(N_DEV):
            g = g + jnp.where(me == s, own_ref[...], p_ref[s]).astype(F32)
        g_ref[...] = g
        d_ref[...], nm_ref[...], nv_ref[...] = _adamw_math(w_ref[...], g, m_ref[...], v_ref[...])

    blk = pl.BlockSpec((tr, C), lambda i: (i, 0))
    out = jax.ShapeDtypeStruct((R, C), F32)
    return pl.pallas_call(
        body, name=name, grid=(R // tr,),
        in_specs=[pl.BlockSpec((N_DEV, tr, C), lambda i: (0, i, 0)), blk, blk, blk, blk],
        out_specs=[blk] * 4, out_shape=[out] * 4,
        compiler_params=_cp("parallel"),
    )(parts, own, w, m, v)


def _adamw_small_call(name, g, w, m, v):
    def body(g_ref, w_ref, m_ref, v_ref, d_ref, nm_ref, nv_ref):
        d_ref[...], nm_ref[...], nv_ref[...] = _adamw_math(w_ref[...], g_ref[...], m_ref[...], v_ref[...])

    out = jax.ShapeDtypeStruct(w.shape, F32)
    return pl.pallas_call(body, name=name, out_shape=[out] * 3)(g, w, m, v)


def _position():
    return lax.axis_index("x"), lax.axis_index("y"), lax.axis_index("c")


def _peer(x, y, c, rel):
    fx, fy, fc = (rel >> 2) & 1, (rel >> 1) & 1, rel & 1
    px = 1 - x if fx else x
    py = 1 - y if fy else y
    pc = 1 - c if fc else c
    return (px, py, pc), 4 * px + 2 * py + pc


def _gather_weights_call(shards):
    n = len(shards)

    def body(*refs):
        ins, outs = refs[:n], refs[n:2 * n]
        send_sems, recv_sems, local_sems = refs[2 * n:]
        x, y, c = _position()
        me = 4 * x + 2 * y + c
        sib, sib_id = _peer(x, y, c, 1)
        chips = [_peer(x, y, c, rel) for rel in (4, 2, 6)]

        def copy(i, k, src, slot, to):
            return pltpu.make_async_remote_copy(
                src_ref=src, dst_ref=outs[i].at[slot], send_sem=send_sems.at[i, k], recv_sem=recv_sems.at[i, k],
                device_id=to, device_id_type=MESH)

        mine = [pltpu.make_async_copy(ins[i], outs[i].at[me], local_sems.at[i]) for i in range(n)]
        for cp in mine:
            cp.start()
        first = []
        for i in range(n):
            first.append(copy(i, 0, ins[i], me, sib))
            first += [copy(i, 1 + j, ins[i], me, dev) for j, (dev, _) in enumerate(chips)]
        for cp in first:
            cp.start()
        passed = []
        for i in range(n):
            for j, (dev, slot) in enumerate(chips):
                copy(i, 1 + j, ins[i], slot, dev).wait_recv()
                fwd = copy(i, 4 + j, outs[i].at[slot], slot, sib)
                fwd.start()
                passed.append(fwd)
        for i in range(n):
            copy(i, 0, ins[i], sib_id, sib).wait_recv()
            for j, (_, slot) in enumerate(chips):
                copy(i, 4 + j, ins[i], slot ^ 1, sib).wait_recv()
        for cp in first + passed:
            cp.wait_send()
        for cp in mine:
            cp.wait()

    hbm = pl.BlockSpec(memory_space=pl.ANY)
    return pl.pallas_call(
        body, name="gather_weights",
        in_specs=[hbm] * n, out_specs=[hbm] * n,
        out_shape=[jax.ShapeDtypeStruct((N_DEV,) + s.shape, s.dtype) for s in shards],
        scratch_shapes=[pltpu.SemaphoreType.DMA((n, 7)), pltpu.SemaphoreType.DMA((n, 7)), pltpu.SemaphoreType.DMA((n,))],
    )(*shards)


_HBM = pl.BlockSpec(memory_space=pltpu.HBM)
_SEM = pl.BlockSpec(memory_space=pltpu.SEMAPHORE)
_EFFECT = pltpu.SideEffectType.DATAFLOW_SIDE_EFFECTING


def _exchange_copies(scatter, incoming, src_refs, land_refs, send_sems, recv_sems):
    x, y, c = _position()
    me = 4 * x + 2 * y + c
    out = []
    for i, (src, land) in enumerate(zip(src_refs, land_refs)):
        for k in range(N_DEV - 1):
            dev, dev_id = _peer(x, y, c, k + 1)
            j = i * (N_DEV - 1) + k
            out.append(pltpu.make_async_remote_copy(
                src_ref=src.at[dev_id] if scatter else src, dst_ref=land.at[dev_id if incoming else me],
                send_sem=send_sems.at[j], recv_sem=recv_sems.at[j], device_id=dev, device_id_type=MESH))
    return out


def _exchange_start_call(name, arrays, scatter):
    n = len(arrays)
    lands = [lax.empty((N_DEV,) + (a.shape[1:] if scatter else a.shape), a.dtype) for a in arrays]

    def body(*refs):
        src_refs, land_refs = refs[:n], refs[n:2 * n]
        send_sems, recv_sems, token = refs[2 * n], refs[2 * n + 1], refs[-1]
        for copy in _exchange_copies(scatter, False, src_refs, land_refs, send_sems, recv_sems):
            copy.start()
        token[...] = jnp.zeros_like(token)

    sems = pltpu.SemaphoreType.DMA((n * (N_DEV - 1),))
    res = pl.pallas_call(
        body, name=name,
        out_shape=[sems, sems] + [pltpu.HBM(a.shape, a.dtype) for a in arrays] + [pltpu.HBM(l.shape, l.dtype) for l in lands]
        + [jax.ShapeDtypeStruct((8, 128), F32)],
        in_specs=[_HBM] * (2 * n), out_specs=[_SEM, _SEM] + [_HBM] * (2 * n) + [pl.BlockSpec(memory_space=pltpu.VMEM)],
        input_output_aliases={i: 2 + i for i in range(2 * n)},
        compiler_params=pltpu.CompilerParams(has_side_effects=_EFFECT),
    )(*[pltpu.with_memory_space_constraint(a, pltpu.HBM) for a in list(arrays) + lands])
    return res[0], res[1], res[2:2 + n], res[2 + n:2 + 2 * n], res[-1]


def _exchange_wait_call(name, started, scatter, after):
    send_sems, recv_sems, srcs, lands, _ = started
    n = len(srcs)

    def body(*refs):
        src_refs, land_refs = refs[:n], refs[n:2 * n]
        send_s, recv_s = refs[2 * n], refs[2 * n + 1]
        for copy in _exchange_copies(scatter, False, src_refs, land_refs, send_s, recv_s):
            copy.wait_send()
        for copy in _exchange_copies(scatter, True, src_refs, land_refs, send_s, recv_s):
            copy.wait_recv()

    res = pl.pallas_call(
        body, name=name,
        out_shape=[pltpu.HBM(a.shape, a.dtype) for a in list(srcs) + list(lands)],
        in_specs=[_HBM] * (2 * n) + [_SEM, _SEM, pl.BlockSpec(memory_space=pl.ANY)], out_specs=[_HBM] * (2 * n),
        input_output_aliases={i: i for i in range(2 * n)},
        compiler_params=pltpu.CompilerParams(has_side_effects=_EFFECT),
    )(*srcs, *lands, send_sems, recv_sems, after)
    return res[n:]


def _allsum_small_call(part):
    P = part.shape[0]

    def body(p_ref, o_ref, all_ref, send_sems, recv_sems):
        x, y, c = _position()
        me = 4 * x + 2 * y + c
        peers = [_peer(x, y, c, rel) for rel in range(1, N_DEV)]
        all_ref[me] = p_ref[...]

        def copy(k, slot, dev):
            return pltpu.make_async_remote_copy(
                src_ref=p_ref, dst_ref=all_ref.at[slot], send_sem=send_sems.at[k], recv_sem=recv_sems.at[k],
                device_id=dev, device_id_type=MESH)

        sends = [copy(k, me, dev) for k, (dev, _) in enumerate(peers)]
        for cp in sends:
            cp.start()
        for k, (dev, dev_id) in enumerate(peers):
            copy(k, dev_id, dev).wait_recv()
        for cp in sends:
            cp.wait_send()
        tot = all_ref[0]
        for s in range(1, N_DEV):
            tot = tot + all_ref[s]
        o_ref[...] = tot

    vm = pl.BlockSpec(memory_space=pltpu.VMEM)
    return pl.pallas_call(
        body, name="allsum_small", in_specs=[vm], out_specs=vm,
        out_shape=jax.ShapeDtypeStruct(part.shape, F32),
        scratch_shapes=[pltpu.VMEM((N_DEV, P, 128), F32), pltpu.SemaphoreType.DMA((7,)), pltpu.SemaphoreType.DMA((7,))],
    )(part)


def _pad_rows(v, rows):
    flat = v.reshape(-1)
    return jnp.pad(flat, (0, rows * 128 - flat.shape[0])).reshape(rows, 128)


def _blocks_by_cols(w):
    K, N = w.shape
    return w.reshape(K, N_DEV, N // N_DEV).transpose(1, 0, 2)


def _cols_from_blocks(wb):
    n, K, C = wb.shape
    return wb.transpose(1, 0, 2).reshape(K, n * C)


def _local_step(x, tgt, w_in_i, conv_w, conv_b, ln_g, ln_b, qg, kg, sinks, g_mix, g_ffn, token0, rest_weights, grads_ready, B, S):
    row = lambda v: v.reshape(1, -1)
    qg2 = jnp.tile(qg, 2).reshape(1, 128)
    kg2 = jnp.tile(kg, 2).reshape(1, 128)
    heads = jnp.arange(1, N_Q_HEADS + 1, dtype=F32)
    slopes = jnp.exp2(-8.0 * heads / N_Q_HEADS)

    u, xn = _inproj_call(x, row(g_mix) + token0, w_in_i)
    hc, c = _conv_fwd_call(u, conv_w, row(conv_b), row(ln_g), row(ln_b), B, S)
    o = _attn_fwd_call(u, qg2, kg2, sinks, slopes, B, S)
    wc, wa, wm, w_ffn_in, wd = rest_weights(o)
    yc, ya, z, h, hn = _merge_call(c, o, u, x, wc, wa, wm, row(g_ffn))
    gu, act = _ffn_in_call(hn, w_ffn_in)
    sse, dout, dout_b = _ffn_out_call(act, wd, h, tgt)

    dgu = _ffn_down_bwd_call(dout_b, wd, gu)
    g_wd = _dw_call("dw_ffn_down", act, dout_b, tn=512)
    g_wffn = _dw_call("dw_ffn_in", hn, dgu, tn=2 * D_FF // 4)
    tok = grads_ready(dict(w_ffn_down=g_wd, w_ffn_in=g_wffn))
    dh, dh_b, g_gffn = _ffn_in_bwd_call(dgu, w_ffn_in, h, row(g_ffn) + tok, dout)
    dyc, dya, du_g, dcp, do = _merge_bwd_call(dh_b, wm, wc, wa, u, yc, ya)
    g_wm = _dw_call("dw_merge", z, dh_b, tn=D_MODEL)
    g_wc = _dw_call("dw_conv_out", c, dyc, tn=D_MODEL)
    g_wa = _dw_call("dw_attn_out", o, dya, tn=D_MODEL)
    tok = grads_ready(dict(w_merge_out=g_wm, w_conv_out=g_wc, w_attn_out=g_wa))
    du_conv, g_cw8, g_cb, g_lg, g_lb = _conv_bwd_call(dcp, hc, u, conv_w, row(ln_g) + tok, row(ln_b), B, S)
    du_q, du_kv, g_qg2, g_kg2, g_sk = _attn_bwd_call(u, do, qg2, kg2, sinks, slopes, B, S)
    g_win = jnp.concatenate([
        _dw_call("dw_in_conv", xn, du_conv, tn=D_MODEL),
        _dw_call("dw_in_q", xn, du_q, tn=D_MODEL),
        _dw_call("dw_in_kv", xn, du_kv, tn=2 * KV_W),
        _dw_call("dw_in_gates", xn, du_g, tn=D_MODEL)], axis=1)
    tok = grads_ready(dict(w_in=g_win))
    grad_x, g_gmix = _inproj_bwd_call(du_conv, du_g, du_q, du_kv, w_in_i, x, row(g_mix) + tok, dh)

    g_cw = g_cw8.reshape(CONV_WIDTH, 8, C_CONV).sum(axis=1)
    g_qg = g_qg2[0, :HEAD_DIM] + g_qg2[0, HEAD_DIM:]
    g_kg = g_kg2[0, :HEAD_DIM] + g_kg2[0, HEAD_DIM:]
    small = dict(norm_mix_g=g_gmix[0], conv_dw_b=g_cb[0], conv_ln_g=g_lg[0], conv_ln_b=g_lb[0], q_norm_g=g_qg,
                 k_norm_g=g_kg, sinks=g_sk[0, :N_Q_HEADS], norm_ffn_g=g_gffn[0], conv_dw_w=g_cw)
    return sse[0, 0], grad_x, small


def _to_internal_cols(w):
    o1 = 2 * C_CONV
    o4 = o1 + Q_W + 2 * KV_W
    return jnp.concatenate([w[:, :o1], w[:, o4:], w[:, o1:o4]], axis=1)


_COL_SHARDED = ("w_in", "w_ffn_in")

_SMALL = (("norm_mix_g", 8), ("conv_dw_b", 8), ("conv_ln_g", 8), ("conv_ln_b", 8), ("norm_ffn_g", 8),
          ("q_norm_g", 1), ("k_norm_g", 1), ("sinks", 1))
_SMALL_ROWS = 48
_CW_ROWS = CONV_WIDTH * C_CONV // 128


def _pack_small(d):
    rows = [_pad_rows(d[name], r) for name, r in _SMALL]
    used = sum(r for _, r in _SMALL)
    return jnp.concatenate(rows + [jnp.zeros((_SMALL_ROWS - used, 128), F32)], axis=0)


def _unpack_small(packed, like):
    out, r0 = {}, 0
    for name, r in _SMALL:
        n = like[name].size
        out[name] = packed[r0:r0 + r].reshape(-1)[:n].reshape(like[name].shape)
        r0 += r
    return out


def kernel(x, norm_mix_g, w_in, conv_dw_w, conv_dw_b, conv_ln_g, conv_ln_b, w_conv_out, q_norm_g, k_norm_g, sinks, w_attn_out, w_merge_out, norm_ffn_g, w_ffn_in, w_ffn_down, loss_target, m_norm_mix_g, m_w_in, m_conv_dw_w, m_conv_dw_b, m_conv_ln_g, m_conv_ln_b, m_w_conv_out, m_q_norm_g, m_k_norm_g, m_sinks, m_w_attn_out, m_w_merge_out, m_norm_ffn_g, m_w_ffn_in, m_w_ffn_down, v_norm_mix_g, v_w_in, v_conv_dw_w, v_conv_dw_b, v_conv_ln_g, v_conv_ln_b, v_w_conv_out, v_q_norm_g, v_k_norm_g, v_sinks, v_w_attn_out, v_w_merge_out, v_norm_ffn_g, v_w_ffn_in, v_w_ffn_down):
    names = ["norm_mix_g", "w_in", "conv_dw_w", "conv_dw_b", "conv_ln_g", "conv_ln_b", "w_conv_out", "q_norm_g",
             "k_norm_g", "sinks", "w_attn_out", "w_merge_out", "norm_ffn_g", "w_ffn_in", "w_ffn_down"]
    w = dict(zip(names, (norm_mix_g, w_in, conv_dw_w, conv_dw_b, conv_ln_g, conv_ln_b, w_conv_out, q_norm_g, k_norm_g,
                         sinks, w_attn_out, w_merge_out, norm_ffn_g, w_ffn_in, w_ffn_down)))
    m = dict(zip(names, (m_norm_mix_g, m_w_in, m_conv_dw_w, m_conv_dw_b, m_conv_ln_g, m_conv_ln_b, m_w_conv_out,
                         m_q_norm_g, m_k_norm_g, m_sinks, m_w_attn_out, m_w_merge_out, m_norm_ffn_g, m_w_ffn_in,
                         m_w_ffn_down)))
    v = dict(zip(names, (v_norm_mix_g, v_w_in, v_conv_dw_w, v_conv_dw_b, v_conv_ln_g, v_conv_ln_b, v_w_conv_out,
                         v_q_norm_g, v_k_norm_g, v_sinks, v_w_attn_out, v_w_merge_out, v_norm_ffn_g, v_w_ffn_in,
                         v_w_ffn_down)))
    B, S, _ = x.shape
    T = B * S
    me = 4 * lax.axis_index("x") + 2 * lax.axis_index("y") + lax.axis_index("c")

    cw_shard = _pad_rows(conv_dw_w.reshape(CONV_WIDTH, 128), 32)
    g_win, g_cwt = _gather_weights_call([w_in.astype(BF16), cw_shard])
    conv_w_full = g_cwt.reshape(N_DEV, 32 * 128)[:, :CONV_WIDTH * 128].reshape(N_DEV, CONV_WIDTH, 128)
    conv_w_full = conv_w_full.transpose(1, 0, 2).reshape(CONV_WIDTH, C_CONV)
    w_in_i = _to_internal_cols(_cols_from_blocks(g_win))
    rest_names = ["w_conv_out", "w_attn_out", "w_merge_out", "w_ffn_in", "w_ffn_down"]
    rest_shards = [w[k].astype(BF16) for k in rest_names]
    rest_started = _exchange_start_call("gather_rest_start", rest_shards, scatter=False)

    def rest_weights(after):
        lands = _exchange_wait_call("gather_rest_wait", rest_started, False, after)
        full = [lax.dynamic_update_slice(l, s[None], (me, 0, 0)) for l, s in zip(lands, rest_shards)]
        wc_f, wa_f, wm_f = (f.reshape(D_MODEL, D_MODEL) for f in full[:3])
        return wc_f, wa_f, wm_f, _cols_from_blocks(full[3]), full[4].reshape(D_FF, D_MODEL)

    exchanges = []

    def grads_ready(group):
        ks = list(group)
        blocks = [_blocks_by_cols(group[k]) if k in _COL_SHARDED
                  else group[k].reshape(N_DEV, group[k].shape[0] // N_DEV, group[k].shape[1]) for k in ks]
        started = _exchange_start_call("scatter_" + ks[0] + "_start", blocks, scatter=True)
        exchanges.append((ks, started, [lax.dynamic_index_in_dim(b, me, 0, keepdims=False) for b in blocks]))
        return started[-1][0:1, 0:1]

    sse, grad_x, small = _local_step(
        x.reshape(T, D_MODEL), loss_target.reshape(T, D_MODEL), w_in_i, conv_w_full, conv_dw_b, conv_ln_g, conv_ln_b,
        q_norm_g, k_norm_g, sinks, norm_mix_g, norm_ffn_g, rest_started[-1][0:1, 0:1], rest_weights, grads_ready, B, S)
    loss = lax.psum(0.5 / D_MODEL * sse, ("x", "y", "c"))

    grad, delta, new_m, new_v = {}, {}, {}, {}
    packed = jnp.concatenate([_pack_small(small), small["conv_dw_w"].reshape(_CW_ROWS, 128)], axis=0)
    tot = _allsum_small_call(packed)
    g_small = _unpack_small(tot[:_SMALL_ROWS], w)
    d_s, m_s, v_s = _adamw_small_call("adamw_small", tot[:_SMALL_ROWS], _pack_small(w), _pack_small(m), _pack_small(v))
    d_small, m_small, v_small = _unpack_small(d_s, w), _unpack_small(m_s, w), _unpack_small(v_s, w)
    g_cw = lax.dynamic_slice_in_dim(tot[_SMALL_ROWS:].reshape(CONV_WIDTH, C_CONV), me * 128, 128, axis=1)
    cw2 = lambda t: t.reshape(CONV_WIDTH, 128)
    d_cw, m_cw, v_cw = _adamw_small_call("adamw_conv_w", g_cw, cw2(conv_dw_w), cw2(m_conv_dw_w), cw2(v_conv_dw_w))
    cw3 = lambda t: t.reshape(CONV_WIDTH, 1, 128)
    for k, _ in _SMALL:
        grad[k], delta[k], new_m[k], new_v[k] = g_small[k], d_small[k], m_small[k], v_small[k]
    grad["conv_dw_w"], delta["conv_dw_w"], new_m["conv_dw_w"], new_v["conv_dw_w"] = cw3(g_cw), cw3(d_cw), cw3(m_cw), cw3(v_cw)

    after = d_cw
    for ks, started, owns in exchanges:
        lands = _exchange_wait_call("scatter_" + ks[0] + "_wait", started, True, after)
        for k, land, own in zip(ks, lands, owns):
            tr = {1024: 256, 352: 176, 128: 128}[w[k].shape[0]]
            grad[k], delta[k], new_m[k], new_v[k] = _adamw_sum_call("adamw_" + k, land, own, w[k], m[k], v[k], tr)
            after = new_v[k]

    return (loss, grad_x.reshape(B, S, D_MODEL), *[grad[k] for k in names], *[delta[k] for k in names],
            *[new_m[k] for k in names], *[new_v[k] for k in names])
```

```python
import functools
import math

import jax
import jax.numpy as jnp
from jax import lax
from jax.experimental import pallas as pl
from jax.experimental.pallas import tpu as pltpu

F32 = jnp.float32
BF16 = jnp.bfloat16
MESH = pl.DeviceIdType.MESH

N_DEV = 8
D_MODEL = 1024
C_CONV = 1024
CONV_WIDTH = 31
HEAD_DIM = 64
N_Q_HEADS = 16
N_KV_HEADS = 2
GROUP = N_Q_HEADS // N_KV_HEADS
WINDOW = 128
BLOCK = 128
D_FF = 2816
EPS = 1e-6
NEG = -1e30
ATTN_SCALE = 1.0 / math.sqrt(HEAD_DIM)
Q_W = N_Q_HEADS * HEAD_DIM
KV_W = N_KV_HEADS * HEAD_DIM
IN_COLS = 2 * C_CONV + Q_W + 2 * KV_W + 2 * D_MODEL

ADAM_LR = 0.001
ADAM_B1 = 0.9
ADAM_B2 = 0.999
ADAM_EPS = 1e-08
ADAM_WD = 0.01
ADAM_STEP = 10

COL_A, COL_GATE, COL_GC, COL_GA, COL_Q, COL_K, COL_V = 0, 1024, 2048, 3072, 4096, 5120, 5248

HALO = 32
VMEM_LIMIT = 56 * 1024 * 1024


def _cp(*sem):
    return pltpu.CompilerParams(dimension_semantics=sem, vmem_limit_bytes=VMEM_LIMIT)


def _sigmoid(x):
    return jax.nn.sigmoid(x)


def _dot(a, b):
    return jnp.dot(a, b, preferred_element_type=F32)


def _dot_nt(a, b):
    return lax.dot_general(a, b, (((1,), (1,)), ((), ())), preferred_element_type=F32)


def _dot_tn(a, b):
    return lax.dot_general(a, b, (((0,), (0,)), ((), ())), preferred_element_type=F32)


def _full(shape):
    return pl.BlockSpec(shape, lambda *_: (0,) * len(shape))


def _rows(tm, ncols, colblk=0):
    return pl.BlockSpec((tm, ncols), lambda i: (i, colblk))


def _inproj_call(x, g, w, tm=256):
    T = x.shape[0]
    N = w.shape[1]

    def body(x_ref, g_ref, w_ref, u_ref, xn_ref):
        xv = x_ref[...]
        r = lax.rsqrt(jnp.mean(xv * xv, axis=-1, keepdims=True) + EPS)
        xn = (xv * r * g_ref[...]).astype(BF16)
        xn_ref[...] = xn
        u_ref[...] = _dot(xn, w_ref[...])

    return pl.pallas_call(
        body, name="inproj", grid=(T // tm,),
        in_specs=[_rows(tm, D_MODEL), _full((1, D_MODEL)), _full(w.shape)],
        out_specs=[_rows(tm, N), _rows(tm, D_MODEL)],
        out_shape=[jax.ShapeDtypeStruct((T, N), F32), jax.ShapeDtypeStruct((T, D_MODEL), BF16)],
        compiler_params=_cp("parallel"),
    )(x, g, w)


def _fill_shifted(src_ref, sh_ref):
    n = src_ref.shape[0] - 8
    for s in range(1, 8):
        sh_ref[s - 1, 0:n, :] = src_ref[s:s + n, :]


def _window(src_ref, sh_ref, off, cols):
    s = off % 8
    if s == 0:
        return src_ref[off:off + 128, cols]
    return sh_ref[s - 1, off - s:off - s + 128, cols]


def _conv_taps(w_ref, src_ref, sh_ref, base, rows, cb, reverse):
    cols = pl.ds(pl.multiple_of(cb * 128, 128), 128)
    outs = []
    for rb in range(rows // 128):
        acc = jnp.zeros((128, 128), F32)
        for j in range(CONV_WIDTH):
            off = base + (CONV_WIDTH - 1 - j if reverse else j) + rb * 128
            acc = acc + w_ref[j:j + 1, cols] * _window(src_ref, sh_ref, off, cols)
        outs.append(acc)
    return outs


def _conv_fwd_call(u, conv_w, conv_b, ln_g, ln_b, B, S, ts=256):
    T = B * S
    nS = S // ts
    per32 = ts // HALO

    def body(a_ref, gate_ref, ap_ref, gp_ref, w_ref, b_ref, g_ref, bb_ref, hc_ref, c_ref, hext, hsh, acc_ref):
        s = pl.program_id(1)
        prev = ap_ref[...] * _sigmoid(gp_ref[...])
        hext[0:HALO, :] = jnp.where(s > 0, prev, 0.0)
        hext[HALO:, :] = a_ref[...] * _sigmoid(gate_ref[...])
        _fill_shifted(hext, hsh)

        def colblock(cb, carry):
            cols = pl.ds(pl.multiple_of(cb * 128, 128), 128)
            outs = _conv_taps(w_ref, hext, hsh, HALO - (CONV_WIDTH - 1), ts, cb, reverse=False)
            for rb, acc in enumerate(outs):
                acc_ref[rb * 128:(rb + 1) * 128, cols] = acc
            return carry

        lax.fori_loop(0, C_CONV // 128, colblock, 0)
        hc = acc_ref[...] + b_ref[...]
        hc_ref[...] = hc
        mu = jnp.mean(hc, axis=-1, keepdims=True)
        xc = hc - mu
        var = jnp.mean(xc * xc, axis=-1, keepdims=True)
        y = xc * lax.rsqrt(var + EPS) * g_ref[...] + bb_ref[...]
        c_ref[...] = (y * _sigmoid(y)).astype(BF16)

    cur = lambda cb: pl.BlockSpec((ts, C_CONV), lambda b, s: (b * nS + s, cb))
    prv = lambda cb: pl.BlockSpec((HALO, C_CONV), lambda b, s: (jnp.maximum(b * (S // HALO) + s * per32 - 1, 0), cb))
    vec = _full((1, C_CONV))
    return pl.pallas_call(
        body, name="conv_fwd", grid=(B, nS),
        in_specs=[cur(0), cur(1), prv(0), prv(1), _full((CONV_WIDTH, C_CONV)), vec, vec, vec],
        out_specs=[pl.BlockSpec((ts, C_CONV), lambda b, s: (b * nS + s, 0))] * 2,
        out_shape=[jax.ShapeDtypeStruct((T, C_CONV), F32), jax.ShapeDtypeStruct((T, C_CONV), BF16)],
        scratch_shapes=[pltpu.VMEM((HALO + ts, C_CONV), F32), pltpu.VMEM((7, HALO + ts, C_CONV), F32),
                        pltpu.VMEM((ts, C_CONV), F32)],
        compiler_params=_cp("parallel", "arbitrary"),
    )(u, u, u, u, conv_w, conv_b, ln_g, ln_b)


def _lane_lo():
    return lax.broadcasted_iota(jnp.int32, (1, 128), 1) < HEAD_DIM


def _half_sums(t, lo):
    s_lo = jnp.sum(jnp.where(lo, t, 0.0), axis=-1, keepdims=True)
    s_hi = jnp.sum(jnp.where(lo, 0.0, t), axis=-1, keepdims=True)
    return jnp.where(lo, s_lo, s_hi)


def _head_rstd(t, lo):
    return lax.rsqrt(_half_sums(t * t, lo) * (1.0 / HEAD_DIM) + EPS)


def _fill_bias(btab, kvh, slopes_ref):
    sj = lax.broadcasted_iota(jnp.int32, (2 * BLOCK, BLOCK), 0)
    qi = lax.broadcasted_iota(jnp.int32, (2 * BLOCK, BLOCK), 1)
    dist = qi + BLOCK - sj
    valid = (dist >= 0) & (dist < WINDOW)
    distf = dist.astype(F32)
    for g in range(GROUP):
        bias = jnp.where(valid, -slopes_ref[kvh * GROUP + g] * distf, NEG)
        btab[1, g] = bias
        btab[0, g] = jnp.where(sj >= BLOCK, bias, NEG)


def _softmax_cols(st_ref, btab, p, var, g, hf, sk):
    sh = st_ref[2 * BLOCK * hf:2 * BLOCK * (hf + 1), 128 * p:128 * (p + 1)] + btab[var, g]
    m = jnp.maximum(jnp.max(sh, axis=0, keepdims=True), sk)
    e = jnp.exp(sh - m)
    esk = jnp.exp(sk - m)
    rz = 1.0 / (jnp.sum(e, axis=0, keepdims=True) + esk)
    return e * rz, esk, rz


def _stage_kv(kvh, k_ref, v_ref, kg_ref, lo, kA, kB, vA, vB):
    kv = k_ref[...]
    rk = _head_rstd(kv, lo)
    kn = kv * rk * kg_ref[...]
    mine = lax.broadcasted_iota(jnp.int32, (1, 128), 1) // HEAD_DIM == kvh
    ksel = jnp.where(mine, kn, 0.0)
    vsel = jnp.where(mine, v_ref[...], 0.0)
    krol = pltpu.roll(ksel, HEAD_DIM, 1)
    vrol = pltpu.roll(vsel, HEAD_DIM, 1)
    first = kvh == 0
    zeros = jnp.zeros((BLOCK, 128), BF16)
    for ref, val in ((kA, jnp.where(first, ksel, krol)), (kB, jnp.where(first, krol, ksel)),
                     (vA, jnp.where(first, vsel, vrol)), (vB, jnp.where(first, vrol, vsel))):
        ref[0:BLOCK, :] = zeros
        ref[BLOCK:, :] = val.astype(BF16)


def _attn_fwd_call(u, qg2, kg2, sinks, slopes, B, S):
    T = B * S
    nb = S // BLOCK
    half = Q_W // 2
    npair = GROUP // 2

    def body(sinks_ref, slopes_ref, q_ref, k_ref, v_ref, qg_ref, kg_ref, o_ref, kA, kB, vA, vB, btab, qn_buf, s_buf, p_buf):
        kvh = pl.program_id(1)
        lo = _lane_lo()
        _stage_kv(kvh, k_ref, v_ref, kg_ref, lo, kA, kB, vA, vB)
        _fill_bias(btab, kvh, slopes_ref)
        qgs = qg_ref[...] * ATTN_SCALE

        def block(n, slot):
            qn_scr, s_scr, p_scr = qn_buf.at[slot], s_buf.at[slot], p_buf.at[slot]
            r0 = pl.multiple_of(n * BLOCK, BLOCK)
            win = pl.ds(r0, 2 * BLOCK)
            k2t = jnp.concatenate([kA[win, :], kB[win, :]], axis=0)
            v2 = jnp.concatenate([vA[win, :], vB[win, :]], axis=0)
            var = jnp.minimum(n, 1)
            for p in range(npair):
                pc = slice(128 * p, 128 * (p + 1))
                qp = q_ref[pl.ds(r0, BLOCK), pc]
                qn_scr[pc, :] = (qp * _head_rstd(qp, lo) * qgs).astype(BF16)
            s_scr[...] = _dot_nt(k2t, qn_scr[...])
            for p in range(npair):
                for hf in range(2):
                    g = 2 * p + hf
                    pn, _, _ = _softmax_cols(s_scr, btab, p, var, g, hf, sinks_ref[kvh * GROUP + g])
                    p_scr[2 * BLOCK * hf:2 * BLOCK * (hf + 1), 128 * p:128 * (p + 1)] = pn.astype(BF16)
            o_all = _dot_tn(v2, p_scr[...]).T
            for p in range(npair):
                pc = slice(128 * p, 128 * (p + 1))
                o_ref[pl.ds(r0, BLOCK), pc] = o_all[pc, :].astype(BF16)

        def two_blocks(i, carry):
            block(2 * i, 0)
            block(2 * i + 1, 1)
            return carry

        lax.fori_loop(0, nb // 2, two_blocks, 0)

    smem = pl.BlockSpec(memory_space=pltpu.SMEM)
    kvbuf = pltpu.VMEM((S + BLOCK, 128), BF16)
    pair_scores = (2, npair * BLOCK, 4 * BLOCK)
    return pl.pallas_call(
        body, name="attn_fwd", grid=(B, N_KV_HEADS),
        in_specs=[smem, smem,
                  pl.BlockSpec((S, half), lambda b, h: (b, COL_Q // half + h)),
                  pl.BlockSpec((S, 128), lambda b, h: (b, COL_K // 128)),
                  pl.BlockSpec((S, 128), lambda b, h: (b, COL_V // 128)),
                  _full((1, 128)), _full((1, 128))],
        out_specs=pl.BlockSpec((S, half), lambda b, h: (b, h)),
        out_shape=jax.ShapeDtypeStruct((T, Q_W), BF16),
        scratch_shapes=[kvbuf, kvbuf, kvbuf, kvbuf, pltpu.VMEM((2, GROUP, 2 * BLOCK, BLOCK), F32),
                        pltpu.VMEM((2, npair * BLOCK, 128), BF16), pltpu.VMEM(pair_scores, F32), pltpu.VMEM(pair_scores, BF16)],
        compiler_params=_cp("parallel", "arbitrary"),
    )(sinks, slopes, u, u, u, qg2, kg2)


def _merge_call(c, o, u, x, wc, wa, wm, g_ffn, tm=256):
    T = x.shape[0]

    def body(c_ref, o_ref, gc_ref, ga_ref, x_ref, wc_ref, wa_ref, wm_ref, g_ref, yc_ref, ya_ref, z_ref, h_ref, hn_ref):
        yc = _dot(c_ref[...], wc_ref[...])
        ya = _dot(o_ref[...], wa_ref[...])
        yc_ref[...] = yc
        ya_ref[...] = ya
        z = (_sigmoid(gc_ref[...]) * yc + _sigmoid(ga_ref[...]) * ya).astype(BF16)
        z_ref[...] = z
        h = x_ref[...] + _dot(z, wm_ref[...])
        h_ref[...] = h
        r = lax.rsqrt(jnp.mean(h * h, axis=-1, keepdims=True) + EPS)
        hn_ref[...] = (h * r * g_ref[...]).astype(BF16)

    sq = _full((D_MODEL, D_MODEL))
    row = _rows(tm, D_MODEL)
    f32 = jax.ShapeDtypeStruct((T, D_MODEL), F32)
    b16 = jax.ShapeDtypeStruct((T, D_MODEL), BF16)
    return pl.pallas_call(
        body, name="merge_fwd", grid=(T // tm,),
        in_specs=[row, row, _rows(tm, D_MODEL, COL_GC // D_MODEL), _rows(tm, D_MODEL, COL_GA // D_MODEL), row,
                  sq, sq, sq, _full((1, D_MODEL))],
        out_specs=[row] * 5,
        out_shape=[f32, f32, b16, f32, b16],
        compiler_params=_cp("parallel"),
    )(c, o, u, u, x, wc, wa, wm, g_ffn)


def _ffn_in_call(hn, w, tm=256):
    T = hn.shape[0]

    def body(hn_ref, w_ref, gu_ref, act_ref):
        gu = _dot(hn_ref[...], w_ref[...])
        gu_ref[...] = gu
        g = gu[:, :D_FF]
        act_ref[...] = (g * _sigmoid(g) * gu[:, D_FF:]).astype(BF16)

    return pl.pallas_call(
        body, name="ffn_in", grid=(T // tm,),
        in_specs=[_rows(tm, D_MODEL), _full(w.shape)],
        out_specs=[_rows(tm, 2 * D_FF), _rows(tm, D_FF)],
        out_shape=[jax.ShapeDtypeStruct((T, 2 * D_FF), F32), jax.ShapeDtypeStruct((T, D_FF), BF16)],
        compiler_params=_cp("parallel"),
    )(hn, w)


def _ffn_out_call(act, wd, h, tgt, tm=512):
    T = h.shape[0]

    def body(act_ref, wd_ref, h_ref, t_ref, sse_ref, d_ref, db_ref):
        e = h_ref[...] + _dot(act_ref[...], wd_ref[...]) - t_ref[...]

        @pl.when(pl.program_id(0) == 0)
        def _():
            sse_ref[...] = jnp.zeros_like(sse_ref)

        sse_ref[...] += jnp.sum(e * e)
        d = e * (1.0 / D_MODEL)
        d_ref[...] = d
        db_ref[...] = d.astype(BF16)

    row = _rows(tm, D_MODEL)
    return pl.pallas_call(
        body, name="ffn_out", grid=(T // tm,),
        in_specs=[_rows(tm, D_FF), _full(wd.shape), row, row],
        out_specs=[_full((8, 128)), row, row],
        out_shape=[jax.ShapeDtypeStruct((8, 128), F32), jax.ShapeDtypeStruct((T, D_MODEL), F32),
                   jax.ShapeDtypeStruct((T, D_MODEL), BF16)],
        compiler_params=_cp("arbitrary"),
    )(act, wd, h, tgt)


def _dw_call(name, a, b, tn, tk=1024):
    T, K = a.shape
    N = b.shape[1]
    tk = min(tk, T)
    nk = T // tk

    def body(a_ref, b_ref, o_ref, acc_ref):
        k = pl.program_id(1)

        @pl.when(k == 0)
        def _():
            acc_ref[...] = jnp.zeros_like(acc_ref)

        acc_ref[...] += _dot_tn(a_ref[...], b_ref[...])

        @pl.when(k == nk - 1)
        def _():
            o_ref[...] = acc_ref[...].astype(BF16)

    return pl.pallas_call(
        body, name=name, grid=(N // tn, nk),
        in_specs=[pl.BlockSpec((tk, K), lambda j, k: (k, 0)), pl.BlockSpec((tk, tn), lambda j, k: (k, j))],
        out_specs=pl.BlockSpec((K, tn), lambda j, k: (0, j)),
        out_shape=jax.ShapeDtypeStruct((K, N), BF16),
        scratch_shapes=[pltpu.VMEM((K, tn), F32)],
        compiler_params=_cp("parallel", "arbitrary"),
    )(a, b)


def _rms_bwd(dy, xv, g):
    r = lax.rsqrt(jnp.mean(xv * xv, axis=-1, keepdims=True) + EPS)
    t = dy * g
    dx = r * t - xv * (r * r * r * jnp.mean(t * xv, axis=-1, keepdims=True))
    return dx, dy * xv * r


def _ffn_down_bwd_call(dout_b, wd, gu, tm=256):
    T = dout_b.shape[0]

    def body(d_ref, wd_ref, g_ref, up_ref, o_ref):
        dact = _dot_nt(d_ref[...], wd_ref[...])
        g = g_ref[...]
        sg = _sigmoid(g)
        o_ref[:, :D_FF] = (dact * up_ref[...] * (sg * (1.0 + g * (1.0 - sg)))).astype(BF16)
        o_ref[:, D_FF:] = (dact * (g * sg)).astype(BF16)

    return pl.pallas_call(
        body, name="ffn_down_bwd", grid=(T // tm,),
        in_specs=[_rows(tm, D_MODEL), _full(wd.shape), _rows(tm, D_FF, 0), _rows(tm, D_FF, 1)],
        out_specs=_rows(tm, 2 * D_FF),
        out_shape=jax.ShapeDtypeStruct((T, 2 * D_FF), BF16),
        compiler_params=_cp("parallel"),
    )(dout_b, wd, gu, gu)


def _ffn_in_bwd_call(dgu, w, h, g_ffn, dout, tm=256):
    T = h.shape[0]

    def body(dgu_ref, w_ref, h_ref, g_ref, d_ref, dh_ref, dhb_ref, dg_ref):
        dhn = _dot_nt(dgu_ref[...], w_ref[...])
        dx, dgt = _rms_bwd(dhn, h_ref[...], g_ref[...])
        dh = d_ref[...] + dx
        dh_ref[...] = dh
        dhb_ref[...] = dh.astype(BF16)

        @pl.when(pl.program_id(0) == 0)
        def _():
            dg_ref[...] = jnp.zeros_like(dg_ref)

        dg_ref[...] += jnp.sum(dgt, axis=0, keepdims=True)

    row = _rows(tm, D_MODEL)
    return pl.pallas_call(
        body, name="ffn_in_bwd", grid=(T // tm,),
        in_specs=[_rows(tm, 2 * D_FF), _full(w.shape), row, _full((1, D_MODEL)), row],
        out_specs=[row, row, _full((1, D_MODEL))],
        out_shape=[jax.ShapeDtypeStruct((T, D_MODEL), F32), jax.ShapeDtypeStruct((T, D_MODEL), BF16),
                   jax.ShapeDtypeStruct((1, D_MODEL), F32)],
        compiler_params=_cp("arbitrary"),
    )(dgu, w, h, g_ffn, dout)


def _merge_bwd_call(dh_b, wm, wc, wa, u, yc, ya, tm=256):
    T = dh_b.shape[0]

    def body(dh_ref, wm_ref, wc_ref, wa_ref, gc_ref, ga_ref, yc_ref, ya_ref, dyc_ref, dya_ref, dug_ref, dcp_ref, do_ref):
        dz = _dot_nt(dh_ref[...], wm_ref[...])
        sgc = _sigmoid(gc_ref[...])
        sga = _sigmoid(ga_ref[...])
        dyc = (dz * sgc).astype(BF16)
        dya = (dz * sga).astype(BF16)
        dyc_ref[...] = dyc
        dya_ref[...] = dya
        dug_ref[:, :D_MODEL] = (dz * yc_ref[...] * (sgc * (1.0 - sgc))).astype(BF16)
        dug_ref[:, D_MODEL:] = (dz * ya_ref[...] * (sga * (1.0 - sga))).astype(BF16)
        dcp_ref[...] = _dot_nt(dyc, wc_ref[...])
        do_ref[...] = _dot_nt(dya, wa_ref[...]).astype(BF16)

    sq = _full((D_MODEL, D_MODEL))
    row = _rows(tm, D_MODEL)
    b16 = jax.ShapeDtypeStruct((T, D_MODEL), BF16)
    return pl.pallas_call(
        body, name="merge_bwd", grid=(T // tm,),
        in_specs=[row, sq, sq, sq, _rows(tm, D_MODEL, COL_GC // D_MODEL), _rows(tm, D_MODEL, COL_GA // D_MODEL), row, row],
        out_specs=[row, row, _rows(tm, 2 * D_MODEL), row, row],
        out_shape=[b16, b16, jax.ShapeDtypeStruct((T, 2 * D_MODEL), BF16), jax.ShapeDtypeStruct((T, D_MODEL), F32), b16],
        compiler_params=_cp("parallel"),
    )(dh_b, wm, wc, wa, u, u, yc, ya)


def _conv_bwd_call(dcp, hc, u, conv_w, ln_g, ln_b, B, S, ts=256):
    T = B * S
    nS = S // ts
    per32 = ts // HALO
    last32 = T // HALO - 1

    def ln_bwd(dcp_v, hc_v, g, bb):
        mu = jnp.mean(hc_v, axis=-1, keepdims=True)
        xc = hc_v - mu
        rstd = lax.rsqrt(jnp.mean(xc * xc, axis=-1, keepdims=True) + EPS)
        xhat = xc * rstd
        y = xhat * g + bb
        sy = _sigmoid(y)
        dy = dcp_v * (sy * (1.0 + y * (1.0 - sy)))
        dxh = dy * g
        dhc = rstd * (dxh - jnp.mean(dxh, axis=-1, keepdims=True) - xhat * jnp.mean(dxh * xhat, axis=-1, keepdims=True))
        return dhc, dy * xhat, dy

    def body(dcp_ref, dcpn_ref, hc_ref, hcn_ref, a_ref, gate_ref, ap_ref, gp_ref, w_ref, g_ref, bb_ref,
             du_ref, dw_ref, db_ref, dlg_ref, dlb_ref, dhext, hgext, dhsh, hgsh, acc_ref):
        b = pl.program_id(0)
        s = pl.program_id(1)

        @pl.when((b == 0) & (s == 0))
        def _():
            dw_ref[...] = jnp.zeros_like(dw_ref)
            db_ref[...] = jnp.zeros_like(db_ref)
            dlg_ref[...] = jnp.zeros_like(dlg_ref)
            dlb_ref[...] = jnp.zeros_like(dlb_ref)

        dhc, dgt, dbt = ln_bwd(dcp_ref[...], hc_ref[...], g_ref[...], bb_ref[...])
        dhn, _, _ = ln_bwd(dcpn_ref[...], hcn_ref[...], g_ref[...], bb_ref[...])
        dhext[0:ts, :] = dhc
        dhext[ts:, :] = jnp.where(s < nS - 1, dhn, 0.0)
        db_ref[...] += jnp.sum(dhc, axis=0, keepdims=True)
        dlg_ref[...] += jnp.sum(dgt, axis=0, keepdims=True)
        dlb_ref[...] += jnp.sum(dbt, axis=0, keepdims=True)

        sg = _sigmoid(gate_ref[...])
        av = a_ref[...]
        hgext[0:HALO, :] = jnp.where(s > 0, ap_ref[...] * _sigmoid(gp_ref[...]), 0.0)
        hgext[HALO:, :] = av * sg
        _fill_shifted(dhext, dhsh)
        _fill_shifted(hgext, hgsh)

        def colblock(cb, carry):
            cols = pl.ds(pl.multiple_of(cb * 128, 128), 128)
            outs = _conv_taps(w_ref, dhext, dhsh, 0, ts, cb, reverse=True)
            for rb, acc in enumerate(outs):
                acc_ref[rb * 128:(rb + 1) * 128, cols] = acc
            dhc_rows = [dhext[rb * 128:(rb + 1) * 128, cols] for rb in range(ts // 128)]
            for j in range(CONV_WIDTH):
                part = jnp.zeros((8, 128), F32)
                for rb in range(ts // 128):
                    glu = _window(hgext, hgsh, HALO - (CONV_WIDTH - 1) + j + rb * 128, cols)
                    part = part + jnp.sum((dhc_rows[rb] * glu).reshape(16, 8, 128), axis=0)
                dw_ref[8 * j:8 * (j + 1), cols] += part
            return carry

        lax.fori_loop(0, C_CONV // 128, colblock, 0)
        dglu = acc_ref[...]
        du_ref[:, :C_CONV] = (dglu * sg).astype(BF16)
        du_ref[:, C_CONV:] = (dglu * av * (sg * (1.0 - sg))).astype(BF16)

    cur = lambda cb: pl.BlockSpec((ts, C_CONV), lambda b, s: (b * nS + s, cb))
    prv = lambda cb: pl.BlockSpec((HALO, C_CONV), lambda b, s: (jnp.maximum(b * (S // HALO) + s * per32 - 1, 0), cb))
    nxt = pl.BlockSpec((HALO, C_CONV), lambda b, s: (jnp.minimum(b * (S // HALO) + (s + 1) * per32, last32), 0))
    vec = _full((1, C_CONV))
    return pl.pallas_call(
        body, name="conv_bwd", grid=(B, nS),
        in_specs=[cur(0), nxt, cur(0), nxt, cur(0), cur(1), prv(0), prv(1), _full((CONV_WIDTH, C_CONV)), vec, vec],
        out_specs=[pl.BlockSpec((ts, 2 * C_CONV), lambda b, s: (b * nS + s, 0)), _full((8 * CONV_WIDTH, C_CONV)), vec, vec, vec],
        out_shape=[jax.ShapeDtypeStruct((T, 2 * C_CONV), BF16), jax.ShapeDtypeStruct((8 * CONV_WIDTH, C_CONV), F32),
                   jax.ShapeDtypeStruct((1, C_CONV), F32), jax.ShapeDtypeStruct((1, C_CONV), F32),
                   jax.ShapeDtypeStruct((1, C_CONV), F32)],
        scratch_shapes=[pltpu.VMEM((ts + HALO, C_CONV), F32), pltpu.VMEM((HALO + ts, C_CONV), F32),
                        pltpu.VMEM((7, ts + HALO, C_CONV), F32), pltpu.VMEM((7, HALO + ts, C_CONV), F32),
                        pltpu.VMEM((ts, C_CONV), F32)],
        compiler_params=_cp("arbitrary", "arbitrary"),
    )(dcp, dcp, hc, hc, u, u, u, u, conv_w, ln_g, ln_b)


def _attn_bwd_call(u, do, qg2, kg2, sinks, slopes, B, S):
    T = B * S
    nb = S // BLOCK
    half = Q_W // 2
    npair = GROUP // 2

    def body(sinks_ref, slopes_ref, q_ref, k_ref, v_ref, do_ref, qg_ref, kg_ref,
             dq_ref, dkv_ref, dqg_ref, dkg_ref, dsk_ref, kA, kB, vA, vB, dkn_acc, dv_acc, dkv_keep,
             btab, s_buf, dp_buf, p_buf, ds_buf, qn_buf, do_buf, rq_buf):
        b = pl.program_id(0)
        kvh = pl.program_id(1)
        lo = _lane_lo()
        lane = lax.broadcasted_iota(jnp.int32, (1, 128), 1)
        first = kvh == 0

        @pl.when((b == 0) & first)
        def _():
            dqg_ref[...] = jnp.zeros_like(dqg_ref)
            dkg_ref[...] = jnp.zeros_like(dkg_ref)
            dsk_ref[...] = jnp.zeros_like(dsk_ref)

        _stage_kv(kvh, k_ref, v_ref, kg_ref, lo, kA, kB, vA, vB)
        dkn_acc[...] = jnp.zeros_like(dkn_acc)
        dv_acc[...] = jnp.zeros_like(dv_acc)
        _fill_bias(btab, kvh, slopes_ref)
        qgs = qg_ref[...] * ATTN_SCALE

        def fold(acc):
            canon = jnp.where(lo, acc[0:2 * BLOCK] + pltpu.roll(acc[2 * BLOCK:], HEAD_DIM, 1), 0.0)
            return jnp.where(first, canon, pltpu.roll(canon, HEAD_DIM, 1))

        def block(n, slot):
            s_scr, dp_scr, p_scr, ds_scr = s_buf.at[slot], dp_buf.at[slot], p_buf.at[slot], ds_buf.at[slot]
            qn_scr, do_scr, rq_scr = qn_buf.at[slot], do_buf.at[slot], rq_buf.at[slot]
            r0 = pl.multiple_of(n * BLOCK, BLOCK)
            win = pl.ds(r0, 2 * BLOCK)
            rows = pl.ds(r0, BLOCK)
            k2t = jnp.concatenate([kA[win, :], kB[win, :]], axis=0)
            v2 = jnp.concatenate([vA[win, :], vB[win, :]], axis=0)
            var = jnp.minimum(n, 1)
            for p in range(npair):
                pc = slice(128 * p, 128 * (p + 1))
                qp = q_ref[rows, pc]
                rq = _head_rstd(qp, lo)
                rq_scr[pc, :] = rq
                qn_scr[pc, :] = (qp * rq * qgs).astype(BF16)
                do_scr[pc, :] = do_ref[rows, pc]
            s_scr[...] = _dot_nt(k2t, qn_scr[...])
            dp_scr[...] = _dot_nt(v2, do_scr[...])
            for p in range(npair):
                for hf in range(2):
                    g = 2 * p + hf
                    keys, qs = slice(2 * BLOCK * hf, 2 * BLOCK * (hf + 1)), slice(128 * p, 128 * (p + 1))
                    pn, esk, rz = _softmax_cols(s_scr, btab, p, var, g, hf, sinks_ref[kvh * GROUP + g])
                    dph = dp_scr[keys, qs]
                    delta = jnp.sum(pn * dph, axis=0, keepdims=True)
                    ds_scr[keys, qs] = (pn * (dph - delta)).astype(BF16)
                    p_scr[keys, qs] = pn.astype(BF16)
                    dsk_ref[...] += jnp.where(lane == kvh * GROUP + g, -jnp.sum(esk * rz * delta), 0.0)
            dqs_all = _dot_tn(k2t, ds_scr[...]).T
            for p in range(npair):
                pc = slice(128 * p, 128 * (p + 1))
                qp = q_ref[rows, pc]
                rq = rq_scr[pc, :]
                dqs = dqs_all[pc, :]
                t = dqs * qgs
                dq = rq * t - qp * (rq * rq * rq * (_half_sums(t * qp, lo) * (1.0 / HEAD_DIM)))
                dq_ref[rows, pc] = dq.astype(BF16)
                dqg_ref[...] += jnp.sum(dqs * qp * rq, axis=0, keepdims=True) * ATTN_SCALE
            dkn_acc[win, :] += fold(_dot(ds_scr[...], qn_scr[...]))
            dv_acc[win, :] += fold(_dot(p_scr[...], do_scr[...]))

        def two_blocks(i, carry):
            block(2 * i, 0)
            block(2 * i + 1, 1)
            return carry

        lax.fori_loop(0, nb // 2, two_blocks, 0)

        dkn = dkn_acc[BLOCK:, :]
        kv = k_ref[...]
        kg = kg_ref[...]
        rk = _head_rstd(kv, lo)
        t = dkn * kg
        dk = rk * t - kv * (rk * rk * rk * (_half_sums(t * kv, lo) * (1.0 / HEAD_DIM)))
        dkg_ref[...] += jnp.sum(dkn * kv * rk, axis=0, keepdims=True)
        dv = dv_acc[BLOCK:, :]

        @pl.when(first)
        def _():
            dkv_keep[:, :128] = dk
            dkv_keep[:, 128:] = dv

        @pl.when(kvh == N_KV_HEADS - 1)
        def _():
            dkv_ref[:, :128] = (dkv_keep[:, :128] + dk).astype(BF16)
            dkv_ref[:, 128:] = (dkv_keep[:, 128:] + dv).astype(BF16)

    smem = pl.BlockSpec(memory_space=pltpu.SMEM)
    kvbuf = pltpu.VMEM((S + BLOCK, 128), BF16)
    accbuf = pltpu.VMEM((S + BLOCK, 128), F32)
    vec = _full((1, 128))
    return pl.pallas_call(
        body, name="attn_bwd", grid=(B, N_KV_HEADS),
        in_specs=[smem, smem,
                  pl.BlockSpec((S, half), lambda b, h: (b, COL_Q // half + h)),
                  pl.BlockSpec((S, 128), lambda b, h: (b, COL_K // 128)),
                  pl.BlockSpec((S, 128), lambda b, h: (b, COL_V // 128)),
                  pl.BlockSpec((S, half), lambda b, h: (b, h)),
                  vec, vec],
        out_specs=[pl.BlockSpec((S, half), lambda b, h: (b, h)), pl.BlockSpec((S, 256), lambda b, h: (b, 0)), vec, vec, vec],
        out_shape=[jax.ShapeDtypeStruct((T, Q_W), BF16), jax.ShapeDtypeStruct((T, 2 * KV_W), BF16),
                   jax.ShapeDtypeStruct((1, 128), F32), jax.ShapeDtypeStruct((1, 128), F32), jax.ShapeDtypeStruct((1, 128), F32)],
        scratch_shapes=[kvbuf, kvbuf, kvbuf, kvbuf, accbuf, accbuf, pltpu.VMEM((S, 256), F32),
                        pltpu.VMEM((2, GROUP, 2 * BLOCK, BLOCK), F32),
                        pltpu.VMEM((2, npair * BLOCK, 4 * BLOCK), F32), pltpu.VMEM((2, npair * BLOCK, 4 * BLOCK), F32),
                        pltpu.VMEM((2, npair * BLOCK, 4 * BLOCK), BF16), pltpu.VMEM((2, npair * BLOCK, 4 * BLOCK), BF16),
                        pltpu.VMEM((2, npair * BLOCK, 128), BF16), pltpu.VMEM((2, npair * BLOCK, 128), BF16),
                        pltpu.VMEM((2, npair * BLOCK, 128), F32)],
        compiler_params=_cp("arbitrary", "arbitrary"),
    )(sinks, slopes, u, u, u, do, qg2, kg2)


def _inproj_bwd_call(du_conv, du_g, du_q, du_kv, w, x, g, dh, tm=256):
    T = x.shape[0]

    def body(dc_ref, dg_ref, dq_ref, dkv_ref, w_ref, x_ref, g_ref, dh_ref, gx_ref, dgm_ref):
        dxn = _dot_nt(dc_ref[...], w_ref[:, COL_A:COL_GC])
        dxn += _dot_nt(dg_ref[...], w_ref[:, COL_GC:COL_Q])
        dxn += _dot_nt(dq_ref[...], w_ref[:, COL_Q:COL_K])
        dxn += _dot_nt(dkv_ref[...], w_ref[:, COL_K:])
        dx, dgt = _rms_bwd(dxn, x_ref[...], g_ref[...])
        gx_ref[...] = dh_ref[...] + dx

        @pl.when(pl.program_id(0) == 0)
        def _():
            dgm_ref[...] = jnp.zeros_like(dgm_ref)

        dgm_ref[...] += jnp.sum(dgt, axis=0, keepdims=True)

    row = _rows(tm, D_MODEL)
    return pl.pallas_call(
        body, name="inproj_bwd", grid=(T // tm,),
        in_specs=[_rows(tm, 2 * C_CONV), _rows(tm, 2 * D_MODEL), _rows(tm, Q_W), _rows(tm, 2 * KV_W), _full(w.shape),
                  row, _full((1, D_MODEL)), row],
        out_specs=[row, _full((1, D_MODEL))],
        out_shape=[jax.ShapeDtypeStruct((T, D_MODEL), F32), jax.ShapeDtypeStruct((1, D_MODEL), F32)],
        compiler_params=_cp("arbitrary"),
    )(du_conv, du_g, du_q, du_kv, w, x, g, dh)


def _adamw_math(w, g, m, v):
    m = ADAM_B1 * m + (1.0 - ADAM_B1) * g
    v = ADAM_B2 * v + (1.0 - ADAM_B2) * (g * g)
    m_hat = m / (1.0 - ADAM_B1 ** ADAM_STEP)
    v_hat = v / (1.0 - ADAM_B2 ** ADAM_STEP)
    delta = -ADAM_LR * (m_hat / (jnp.sqrt(v_hat) + ADAM_EPS) + ADAM_WD * w)
    return delta, m, v


def _adamw_sum_call(name, parts, own, w, m, v, tr):
    R, C = w.shape

    def body(p_ref, own_ref, w_ref, m_ref, v_ref, g_ref, d_ref, nm_ref, nv_ref):
        x, y, c = _position()
        me = 4 * x + 2 * y + c
        g = jnp.zeros((tr, C), F32)
        for s in range(N_DEV):
            g = g + jnp.where(me == s, own_ref[...], p_ref[s]).astype(F32)
        g_ref[...] = g
        d_ref[...], nm_ref[...], nv_ref[...] = _adamw_math(w_ref[...], g, m_ref[...], v_ref[...])

    blk = pl.BlockSpec((tr, C), lambda i: (i, 0))
    out = jax.ShapeDtypeStruct((R, C), F32)
    return pl.pallas_call(
        body, name=name, grid=(R // tr,),
        in_specs=[pl.BlockSpec((N_DEV, tr, C), lambda i: (0, i, 0)), blk, blk, blk, blk],
        out_specs=[blk] * 4, out_shape=[out] * 4,
        compiler_params=_cp("parallel"),
    )(parts, own, w, m, v)


def _adamw_small_call(name, g, w, m, v):
    def body(g_ref, w_ref, m_ref, v_ref, d_ref, nm_ref, nv_ref):
        d_ref[...], nm_ref[...], nv_ref[...] = _adamw_math(w_ref[...], g_ref[...], m_ref[...], v_ref[...])

    out = jax.ShapeDtypeStruct(w.shape, F32)
    return pl.pallas_call(body, name=name, out_shape=[out] * 3)(g, w, m, v)


def _position():
    return lax.axis_index("x"), lax.axis_index("y"), lax.axis_index("c")


def _peer(x, y, c, rel):
    fx, fy, fc = (rel >> 2) & 1, (rel >> 1) & 1, rel & 1
    px = 1 - x if fx else x
    py = 1 - y if fy else y
    pc = 1 - c if fc else c
    return (px, py, pc), 4 * px + 2 * py + pc


def _gather_weights_call(shards):
    n = len(shards)

    def body(*refs):
        ins, outs = refs[:n], refs[n:2 * n]
        send_sems, recv_sems, local_sems = refs[2 * n:]
        x, y, c = _position()
        me = 4 * x + 2 * y + c
        sib, sib_id = _peer(x, y, c, 1)
        chips = [_peer(x, y, c, rel) for rel in (4, 2, 6)]

        def copy(i, k, src, slot, to):
            return pltpu.make_async_remote_copy(
                src_ref=src, dst_ref=outs[i].at[slot], send_sem=send_sems.at[i, k], recv_sem=recv_sems.at[i, k],
                device_id=to, device_id_type=MESH)

        mine = [pltpu.make_async_copy(ins[i], outs[i].at[me], local_sems.at[i]) for i in range(n)]
        for cp in mine:
            cp.start()
        first = []
        for i in range(n):
            first.append(copy(i, 0, ins[i], me, sib))
            first += [copy(i, 1 + j, ins[i], me, dev) for j, (dev, _) in enumerate(chips)]
        for cp in first:
            cp.start()
        passed = []
        for i in range(n):
            for j, (dev, slot) in enumerate(chips):
                copy(i, 1 + j, ins[i], slot, dev).wait_recv()
                fwd = copy(i, 4 + j, outs[i].at[slot], slot, sib)
                fwd.start()
                passed.append(fwd)
        for i in range(n):
            copy(i, 0, ins[i], sib_id, sib).wait_recv()
            for j, (_, slot) in enumerate(chips):
                copy(i, 4 + j, ins[i], slot ^ 1, sib).wait_recv()
        for cp in first + passed:
            cp.wait_send()
        for cp in mine:
            cp.wait()

    hbm = pl.BlockSpec(memory_space=pl.ANY)
    return pl.pallas_call(
        body, name="gather_weights",
        in_specs=[hbm] * n, out_specs=[hbm] * n,
        out_shape=[jax.ShapeDtypeStruct((N_DEV,) + s.shape, s.dtype) for s in shards],
        scratch_shapes=[pltpu.SemaphoreType.DMA((n, 7)), pltpu.SemaphoreType.DMA((n, 7)), pltpu.SemaphoreType.DMA((n,))],
    )(*shards)


_HBM = pl.BlockSpec(memory_space=pltpu.HBM)
_SEM = pl.BlockSpec(memory_space=pltpu.SEMAPHORE)
_EFFECT = pltpu.SideEffectType.DATAFLOW_SIDE_EFFECTING


def _exchange_copies(scatter, incoming, src_refs, land_refs, send_sems, recv_sems):
    x, y, c = _position()
    me = 4 * x + 2 * y + c
    out = []
    for i, (src, land) in enumerate(zip(src_refs, land_refs)):
        for k in range(N_DEV - 1):
            dev, dev_id = _peer(x, y, c, k + 1)
            j = i * (N_DEV - 1) + k
            out.append(pltpu.make_async_remote_copy(
                src_ref=src.at[dev_id] if scatter else src, dst_ref=land.at[dev_id if incoming else me],
                send_sem=send_sems.at[j], recv_sem=recv_sems.at[j], device_id=dev, device_id_type=MESH))
    return out


def _exchange_start_call(name, arrays, scatter):
    n = len(arrays)
    lands = [lax.empty((N_DEV,) + (a.shape[1:] if scatter else a.shape), a.dtype) for a in arrays]

    def body(*refs):
        src_refs, land_refs = refs[:n], refs[n:2 * n]
        send_sems, recv_sems, token = refs[2 * n], refs[2 * n + 1], refs[-1]
        for copy in _exchange_copies(scatter, False, src_refs, land_refs, send_sems, recv_sems):
            copy.start()
        token[...] = jnp.zeros_like(token)

    sems = pltpu.SemaphoreType.DMA((n * (N_DEV - 1),))
    res = pl.pallas_call(
        body, name=name,
        out_shape=[sems, sems] + [pltpu.HBM(a.shape, a.dtype) for a in arrays] + [pltpu.HBM(l.shape, l.dtype) for l in lands]
        + [jax.ShapeDtypeStruct((8, 128), F32)],
        in_specs=[_HBM] * (2 * n), out_specs=[_SEM, _SEM] + [_HBM] * (2 * n) + [pl.BlockSpec(memory_space=pltpu.VMEM)],
        input_output_aliases={i: 2 + i for i in range(2 * n)},
        compiler_params=pltpu.CompilerParams(has_side_effects=_EFFECT),
    )(*[pltpu.with_memory_space_constraint(a, pltpu.HBM) for a in list(arrays) + lands])
    return res[0], res[1], res[2:2 + n], res[2 + n:2 + 2 * n], res[-1]


def _exchange_wait_call(name, started, scatter, after):
    send_sems, recv_sems, srcs, lands, _ = started
    n = len(srcs)

    def body(*refs):
        src_refs, land_refs = refs[:n], refs[n:2 * n]
        send_s, recv_s = refs[2 * n], refs[2 * n + 1]
        for copy in _exchange_copies(scatter, False, src_refs, land_refs, send_s, recv_s):
            copy.wait_send()
        for copy in _exchange_copies(scatter, True, src_refs, land_refs, send_s, recv_s):
            copy.wait_recv()

    res = pl.pallas_call(
        body, name=name,
        out_shape=[pltpu.HBM(a.shape, a.dtype) for a in list(srcs) + list(lands)],
        in_specs=[_HBM] * (2 * n) + [_SEM, _SEM, pl.BlockSpec(memory_space=pl.ANY)], out_specs=[_HBM] * (2 * n),
        input_output_aliases={i: i for i in range(2 * n)},
        compiler_params=pltpu.CompilerParams(has_side_effects=_EFFECT),
    )(*srcs, *lands, send_sems, recv_sems, after)
    return res[n:]


def _allsum_small_call(part):
    P = part.shape[0]

    def body(p_ref, o_ref, all_ref, send_sems, recv_sems):
        x, y, c = _position()
        me = 4 * x + 2 * y + c
        peers = [_peer(x, y, c, rel) for rel in range(1, N_DEV)]
        all_ref[me] = p_ref[...]

        def copy(k, slot, dev):
            return pltpu.make_async_remote_copy(
                src_ref=p_ref, dst_ref=all_ref.at[slot], send_sem=send_sems.at[k], recv_sem=recv_sems.at[k],
                device_id=dev, device_id_type=MESH)

        sends = [copy(k, me, dev) for k, (dev, _) in enumerate(peers)]
        for cp in sends:
            cp.start()
        for k, (dev, dev_id) in enumerate(peers):
            copy(k, dev_id, dev).wait_recv()
        for cp in sends:
            cp.wait_send()
        tot = all_ref[0]
        for s in range(1, N_DEV):
            tot = tot + all_ref[s]
        o_ref[...] = tot

    vm = pl.BlockSpec(memory_space=pltpu.VMEM)
    return pl.pallas_call(
        body, name="allsum_small", in_specs=[vm], out_specs=vm,
        out_shape=jax.ShapeDtypeStruct(part.shape, F32),
        scratch_shapes=[pltpu.VMEM((N_DEV, P, 128), F32), pltpu.SemaphoreType.DMA((7,)), pltpu.SemaphoreType.DMA((7,))],
    )(part)


def _pad_rows(v, rows):
    flat = v.reshape(-1)
    return jnp.pad(flat, (0, rows * 128 - flat.shape[0])).reshape(rows, 128)


def _blocks_by_cols(w):
    K, N = w.shape
    return w.reshape(K, N_DEV, N // N_DEV).transpose(1, 0, 2)


def _cols_from_blocks(wb):
    n, K, C = wb.shape
    return wb.transpose(1, 0, 2).reshape(K, n * C)


def _local_step(x, tgt, w_in_i, conv_w, conv_b, ln_g, ln_b, qg, kg, sinks, g_mix, g_ffn, token0, rest_weights, grads_ready, B, S):
    row = lambda v: v.reshape(1, -1)
    qg2 = jnp.tile(qg, 2).reshape(1, 128)
    kg2 = jnp.tile(kg, 2).reshape(1, 128)
    heads = jnp.arange(1, N_Q_HEADS + 1, dtype=F32)
    slopes = jnp.exp2(-8.0 * heads / N_Q_HEADS)

    u, xn = _inproj_call(x, row(g_mix) + token0, w_in_i)
    hc, c = _conv_fwd_call(u, conv_w, row(conv_b), row(ln_g), row(ln_b), B, S)
    o = _attn_fwd_call(u, qg2, kg2, sinks, slopes, B, S)
    wc, wa, wm, w_ffn_in, wd = rest_weights(o)
    yc, ya, z, h, hn = _merge_call(c, o, u, x, wc, wa, wm, row(g_ffn))
    gu, act = _ffn_in_call(hn, w_ffn_in)
    sse, dout, dout_b = _ffn_out_call(act, wd, h, tgt)

    dgu = _ffn_down_bwd_call(dout_b, wd, gu)
    g_wd = _dw_call("dw_ffn_down", act, dout_b, tn=512)
    g_wffn = _dw_call("dw_ffn_in", hn, dgu, tn=2 * D_FF // 4)
    tok = grads_ready(dict(w_ffn_down=g_wd, w_ffn_in=g_wffn))
    dh, dh_b, g_gffn = _ffn_in_bwd_call(dgu, w_ffn_in, h, row(g_ffn) + tok, dout)
    dyc, dya, du_g, dcp, do = _merge_bwd_call(dh_b, wm, wc, wa, u, yc, ya)
    g_wm = _dw_call("dw_merge", z, dh_b, tn=D_MODEL)
    g_wc = _dw_call("dw_conv_out", c, dyc, tn=D_MODEL)
    g_wa = _dw_call("dw_attn_out", o, dya, tn=D_MODEL)
    tok = grads_ready(dict(w_merge_out=g_wm, w_conv_out=g_wc, w_attn_out=g_wa))
    du_conv, g_cw8, g_cb, g_lg, g_lb = _conv_bwd_call(dcp, hc, u, conv_w, row(ln_g) + tok, row(ln_b), B, S)
    du_q, du_kv, g_qg2, g_kg2, g_sk = _attn_bwd_call(u, do, qg2, kg2, sinks, slopes, B, S)
    g_win = jnp.concatenate([
        _dw_call("dw_in_conv", xn, du_conv, tn=D_MODEL),
        _dw_call("dw_in_q", xn, du_q, tn=D_MODEL),
        _dw_call("dw_in_kv", xn, du_kv, tn=2 * KV_W),
        _dw_call("dw_in_gates", xn, du_g, tn=D_MODEL)], axis=1)
    tok = grads_ready(dict(w_in=g_win))
    grad_x, g_gmix = _inproj_bwd_call(du_conv, du_g, du_q, du_kv, w_in_i, x, row(g_mix) + tok, dh)

    g_cw = g_cw8.reshape(CONV_WIDTH, 8, C_CONV).sum(axis=1)
    g_qg = g_qg2[0, :HEAD_DIM] + g_qg2[0, HEAD_DIM:]
    g_kg = g_kg2[0, :HEAD_DIM] + g_kg2[0, HEAD_DIM:]
    small = dict(norm_mix_g=g_gmix[0], conv_dw_b=g_cb[0], conv_ln_g=g_lg[0], conv_ln_b=g_lb[0], q_norm_g=g_qg,
                 k_norm_g=g_kg, sinks=g_sk[0, :N_Q_HEADS], norm_ffn_g=g_gffn[0], conv_dw_w=g_cw)
    return sse[0, 0], grad_x, small


def _to_internal_cols(w):
    o1 = 2 * C_CONV
    o4 = o1 + Q_W + 2 * KV_W
    return jnp.concatenate([w[:, :o1], w[:, o4:], w[:, o1:o4]], axis=1)


_COL_SHARDED = ("w_in", "w_ffn_in")

_SMALL = (("norm_mix_g", 8), ("conv_dw_b", 8), ("conv_ln_g", 8), ("conv_ln_b", 8), ("norm_ffn_g", 8),
          ("q_norm_g", 1), ("k_norm_g", 1), ("sinks", 1))
_SMALL_ROWS = 48
_CW_ROWS = CONV_WIDTH * C_CONV // 128


def _pack_small(d):
    rows = [_pad_rows(d[name], r) for name, r in _SMALL]
    used = sum(r for _, r in _SMALL)
    return jnp.concatenate(rows + [jnp.zeros((_SMALL_ROWS - used, 128), F32)], axis=0)


def _unpack_small(packed, like):
    out, r0 = {}, 0
    for name, r in _SMALL:
        n = like[name].size
        out[name] = packed[r0:r0 + r].reshape(-1)[:n].reshape(like[name].shape)
        r0 += r
    return out


def kernel(x, norm_mix_g, w_in, conv_dw_w, conv_dw_b, conv_ln_g, conv_ln_b, w_conv_out, q_norm_g, k_norm_g, sinks, w_attn_out, w_merge_out, norm_ffn_g, w_ffn_in, w_ffn_down, loss_target, m_norm_mix_g, m_w_in, m_conv_dw_w, m_conv_dw_b, m_conv_ln_g, m_conv_ln_b, m_w_conv_out, m_q_norm_g, m_k_norm_g, m_sinks, m_w_attn_out, m_w_merge_out, m_norm_ffn_g, m_w_ffn_in, m_w_ffn_down, v_norm_mix_g, v_w_in, v_conv_dw_w, v_conv_dw_b, v_conv_ln_g, v_conv_ln_b, v_w_conv_out, v_q_norm_g, v_k_norm_g, v_sinks, v_w_attn_out, v_w_merge_out, v_norm_ffn_g, v_w_ffn_in, v_w_ffn_down):
    names = ["norm_mix_g", "w_in", "conv_dw_w", "conv_dw_b", "conv_ln_g", "conv_ln_b", "w_conv_out", "q_norm_g",
             "k_norm_g", "sinks", "w_attn_out", "w_merge_out", "norm_ffn_g", "w_ffn_in", "w_ffn_down"]
    w = dict(zip(names, (norm_mix_g, w_in, conv_dw_w, conv_dw_b, conv_ln_g, conv_ln_b, w_conv_out, q_norm_g, k_norm_g,
                         sinks, w_attn_out, w_merge_out, norm_ffn_g, w_ffn_in, w_ffn_down)))
    m = dict(zip(names, (m_norm_mix_g, m_w_in, m_conv_dw_w, m_conv_dw_b, m_conv_ln_g, m_conv_ln_b, m_w_conv_out,
                         m_q_norm_g, m_k_norm_g, m_sinks, m_w_attn_out, m_w_merge_out, m_norm_ffn_g, m_w_ffn_in,
                         m_w_ffn_down)))
    v = dict(zip(names, (v_norm_mix_g, v_w_in, v_conv_dw_w, v_conv_dw_b, v_conv_ln_g, v_conv_ln_b, v_w_conv_out,
                         v_q_norm_g, v_k_norm_g, v_sinks, v_w_attn_out, v_w_merge_out, v_norm_ffn_g, v_w_ffn_in,
                         v_w_ffn_down)))
    B, S, _ = x.shape
    T = B * S
    me = 4 * lax.axis_index("x") + 2 * lax.axis_index("y") + lax.axis_index("c")

    cw_shard = _pad_rows(conv_dw_w.reshape(CONV_WIDTH, 128), 32)
    g_win, g_cwt = _gather_weights_call([w_in.astype(BF16), cw_shard])
    conv_w_full = g_cwt.reshape(N_DEV, 32 * 128)[:, :CONV_WIDTH * 128].reshape(N_DEV, CONV_WIDTH, 128)
    conv_w_full = conv_w_full.transpose(1, 0, 2).reshape(CONV_WIDTH, C_CONV)
    w_in_i = _to_internal_cols(_cols_from_blocks(g_win))
    rest_names = ["w_conv_out", "w_attn_out", "w_merge_out", "w_ffn_in", "w_ffn_down"]
    rest_shards = [w[k].astype(BF16) for k in rest_names]
    rest_started = _exchange_start_call("gather_rest_start", rest_shards, scatter=False)

    def rest_weights(after):
        lands = _exchange_wait_call("gather_rest_wait", rest_started, False, after)
        full = [lax.dynamic_update_slice(l, s[None], (me, 0, 0)) for l, s in zip(lands, rest_shards)]
        wc_f, wa_f, wm_f = (f.reshape(D_MODEL, D_MODEL) for f in full[:3])
        return wc_f, wa_f, wm_f, _cols_from_blocks(full[3]), full[4].reshape(D_FF, D_MODEL)

    exchanges = []

    def grads_ready(group):
        ks = list(group)
        blocks = [_blocks_by_cols(group[k]) if k in _COL_SHARDED
                  else group[k].reshape(N_DEV, group[k].shape[0] // N_DEV, group[k].shape[1]) for k in ks]
        started = _exchange_start_call("scatter_" + ks[0] + "_start", blocks, scatter=True)
        exchanges.append((ks, started, [lax.dynamic_index_in_dim(b, me, 0, keepdims=False) for b in blocks]))
        return started[-1][0:1, 0:1]

    sse, grad_x, small = _local_step(
        x.reshape(T, D_MODEL), loss_target.reshape(T, D_MODEL), w_in_i, conv_w_full, conv_dw_b, conv_ln_g, conv_ln_b,
        q_norm_g, k_norm_g, sinks, norm_mix_g, norm_ffn_g, rest_started[-1][0:1, 0:1], rest_weights, grads_ready, B, S)
    loss = lax.psum(0.5 / D_MODEL * sse, ("x", "y", "c"))

    grad, delta, new_m, new_v = {}, {}, {}, {}
    packed = jnp.concatenate([_pack_small(small), small["conv_dw_w"].reshape(_CW_ROWS, 128)], axis=0)
    tot = _allsum_small_call(packed)
    g_small = _unpack_small(tot[:_SMALL_ROWS], w)
    d_s, m_s, v_s = _adamw_small_call("adamw_small", tot[:_SMALL_ROWS], _pack_small(w), _pack_small(m), _pack_small(v))
    d_small, m_small, v_small = _unpack_small(d_s, w), _unpack_small(m_s, w), _unpack_small(v_s, w)
    g_cw = lax.dynamic_slice_in_dim(tot[_SMALL_ROWS:].reshape(CONV_WIDTH, C_CONV), me * 128, 128, axis=1)
    cw2 = lambda t: t.reshape(CONV_WIDTH, 128)
    d_cw, m_cw, v_cw = _adamw_small_call("adamw_conv_w", g_cw, cw2(conv_dw_w), cw2(m_conv_dw_w), cw2(v_conv_dw_w))
    cw3 = lambda t: t.reshape(CONV_WIDTH, 1, 128)
    for k, _ in _SMALL:
        grad[k], delta[k], new_m[k], new_v[k] = g_small[k], d_small[k], m_small[k], v_small[k]
    grad["conv_dw_w"], delta["conv_dw_w"], new_m["conv_dw_w"], new_v["conv_dw_w"] = cw3(g_cw), cw3(d_cw), cw3(m_cw), cw3(v_cw)

    after = d_cw
    for ks, started, owns in exchanges:
        lands = _exchange_wait_call("scatter_" + ks[0] + "_wait", started, True, after)
        for k, land, own in zip(ks, lands, owns):
            tr = {1024: 256, 352: 176, 128: 128}[w[k].shape[0]]
            grad[k], delta[k], new_m[k], new_v[k] = _adamw_sum_call("adamw_" + k, land, own, w[k], m[k], v[k], tr)
            after = new_v[k]

    return (loss, grad_x.reshape(B, S, D_MODEL), *[grad[k] for k in names], *[delta[k] for k in names],
            *[new_m[k] for k in names], *[new_v[k] for k in names])
```

```python
import functools
import math

import jax
import jax.numpy as jnp
from jax import lax
from jax.experimental import pallas as pl
from jax.experimental.pallas import tpu as pltpu

F32 = jnp.float32
BF16 = jnp.bfloat16
MESH = pl.DeviceIdType.MESH

N_DEV = 8
D_MODEL = 1024
C_CONV = 1024
CONV_WIDTH = 31
HEAD_DIM = 64
N_Q_HEADS = 16
N_KV_HEADS = 2
GROUP = N_Q_HEADS // N_KV_HEADS
WINDOW = 128
BLOCK = 128
D_FF = 2816
EPS = 1e-6
NEG = -1e30
ATTN_SCALE = 1.0 / math.sqrt(HEAD_DIM)
Q_W = N_Q_HEADS * HEAD_DIM
KV_W = N_KV_HEADS * HEAD_DIM
IN_COLS = 2 * C_CONV + Q_W + 2 * KV_W + 2 * D_MODEL

ADAM_LR = 0.001
ADAM_B1 = 0.9
ADAM_B2 = 0.999
ADAM_EPS = 1e-08
ADAM_WD = 0.01
ADAM_STEP = 10

COL_A, COL_GATE, COL_GC, COL_GA, COL_Q, COL_K, COL_V = 0, 1024, 2048, 3072, 4096, 5120, 5248

HALO = 32
VMEM_LIMIT = 56 * 1024 * 1024


def _cp(*sem):
    return pltpu.CompilerParams(dimension_semantics=sem, vmem_limit_bytes=VMEM_LIMIT)


def _sigmoid(x):
    return jax.nn.sigmoid(x)


def _dot(a, b):
    return jnp.dot(a, b, preferred_element_type=F32)


def _dot_nt(a, b):
    return lax.dot_general(a, b, (((1,), (1,)), ((), ())), preferred_element_type=F32)


def _dot_tn(a, b):
    return lax.dot_general(a, b, (((0,), (0,)), ((), ())), preferred_element_type=F32)


def _full(shape):
    return pl.BlockSpec(shape, lambda *_: (0,) * len(shape))


def _rows(tm, ncols, colblk=0):
    return pl.BlockSpec((tm, ncols), lambda i: (i, colblk))


def _inproj_call(x, g, w, tm=256):
    T = x.shape[0]
    N = w.shape[1]

    def body(x_ref, g_ref, w_ref, u_ref, xn_ref):
        xv = x_ref[...]
        r = lax.rsqrt(jnp.mean(xv * xv, axis=-1, keepdims=True) + EPS)
        xn = (xv * r * g_ref[...]).astype(BF16)
        xn_ref[...] = xn
        u_ref[...] = _dot(xn, w_ref[...]).astype(BF16)

    return pl.pallas_call(
        body, name="inproj", grid=(T // tm,),
        in_specs=[_rows(tm, D_MODEL), _full((1, D_MODEL)), _full(w.shape)],
        out_specs=[_rows(tm, N), _rows(tm, D_MODEL)],
        out_shape=[jax.ShapeDtypeStruct((T, N), BF16), jax.ShapeDtypeStruct((T, D_MODEL), BF16)],
        compiler_params=_cp("parallel"),
    )(x, g, w)


def _fill_shifted(src_ref, sh_ref):
    n = src_ref.shape[0] - 8
    for s in range(1, 8):
        sh_ref[s - 1, 0:n, :] = src_ref[s:s + n, :]


def _window(src_ref, sh_ref, off, cols):
    s = off % 8
    if s == 0:
        return src_ref[off:off + 128, cols]
    return sh_ref[s - 1, off - s:off - s + 128, cols]


def _conv_taps(w_ref, src_ref, sh_ref, base, rows, cb, reverse):
    cols = pl.ds(pl.multiple_of(cb * 128, 128), 128)
    outs = []
    for rb in range(rows // 128):
        acc = jnp.zeros((128, 128), F32)
        for j in range(CONV_WIDTH):
            off = base + (CONV_WIDTH - 1 - j if reverse else j) + rb * 128
            acc = acc + w_ref[j:j + 1, cols] * _window(src_ref, sh_ref, off, cols)
        outs.append(acc)
    return outs


def _conv_fwd_call(u, conv_w, conv_b, ln_g, ln_b, B, S, ts=256):
    T = B * S
    nS = S // ts
    per32 = ts // HALO

    def body(a_ref, gate_ref, ap_ref, gp_ref, w_ref, b_ref, g_ref, bb_ref, hc_ref, c_ref, hext, hsh, acc_ref):
        s = pl.program_id(1)
        prev = ap_ref[...].astype(F32) * _sigmoid(gp_ref[...].astype(F32))
        hext[0:HALO, :] = jnp.where(s > 0, prev, 0.0)
        hext[HALO:, :] = a_ref[...].astype(F32) * _sigmoid(gate_ref[...].astype(F32))
        _fill_shifted(hext, hsh)

        def colblock(cb, carry):
            cols = pl.ds(pl.multiple_of(cb * 128, 128), 128)
            outs = _conv_taps(w_ref, hext, hsh, HALO - (CONV_WIDTH - 1), ts, cb, reverse=False)
            for rb, acc in enumerate(outs):
                acc_ref[rb * 128:(rb + 1) * 128, cols] = acc
            return carry

        lax.fori_loop(0, C_CONV // 128, colblock, 0)
        hc = acc_ref[...] + b_ref[...]
        hc_ref[...] = hc
        mu = jnp.mean(hc, axis=-1, keepdims=True)
        xc = hc - mu
        var = jnp.mean(xc * xc, axis=-1, keepdims=True)
        y = xc * lax.rsqrt(var + EPS) * g_ref[...] + bb_ref[...]
        c_ref[...] = (y * _sigmoid(y)).astype(BF16)

    cur = lambda cb: pl.BlockSpec((ts, C_CONV), lambda b, s: (b * nS + s, cb))
    prv = lambda cb: pl.BlockSpec((HALO, C_CONV), lambda b, s: (jnp.maximum(b * (S // HALO) + s * per32 - 1, 0), cb))
    vec = _full((1, C_CONV))
    return pl.pallas_call(
        body, name="conv_fwd", grid=(B, nS),
        in_specs=[cur(0), cur(1), prv(0), prv(1), _full((CONV_WIDTH, C_CONV)), vec, vec, vec],
        out_specs=[pl.BlockSpec((ts, C_CONV), lambda b, s: (b * nS + s, 0))] * 2,
        out_shape=[jax.ShapeDtypeStruct((T, C_CONV), F32), jax.ShapeDtypeStruct((T, C_CONV), BF16)],
        scratch_shapes=[pltpu.VMEM((HALO + ts, C_CONV), F32), pltpu.VMEM((7, HALO + ts, C_CONV), F32),
                        pltpu.VMEM((ts, C_CONV), F32)],
        compiler_params=_cp("parallel", "arbitrary"),
    )(u, u, u, u, conv_w, conv_b, ln_g, ln_b)


def _lane_lo():
    return lax.broadcasted_iota(jnp.int32, (1, 128), 1) < HEAD_DIM


def _half_sums(t, lo):
    s_lo = jnp.sum(jnp.where(lo, t, 0.0), axis=-1, keepdims=True)
    s_hi = jnp.sum(jnp.where(lo, 0.0, t), axis=-1, keepdims=True)
    return jnp.where(lo, s_lo, s_hi)


def _head_rstd(t, lo):
    return lax.rsqrt(_half_sums(t * t, lo) * (1.0 / HEAD_DIM) + EPS)


def _fill_bias(btab, kvh, slopes_ref):
    sj = lax.broadcasted_iota(jnp.int32, (2 * BLOCK, BLOCK), 0)
    qi = lax.broadcasted_iota(jnp.int32, (2 * BLOCK, BLOCK), 1)
    dist = qi + BLOCK - sj
    valid = (dist >= 0) & (dist < WINDOW)
    distf = dist.astype(F32)
    for g in range(GROUP):
        bias = jnp.where(valid, -slopes_ref[kvh * GROUP + g] * distf, NEG)
        btab[1, g] = bias
        btab[0, g] = jnp.where(sj >= BLOCK, bias, NEG)


def _softmax_cols(st_ref, btab, p, var, g, hf, sk):
    sh = st_ref[2 * BLOCK * hf:2 * BLOCK * (hf + 1), 128 * p:128 * (p + 1)] + btab[var, g]
    m = jnp.maximum(jnp.max(sh, axis=0, keepdims=True), sk)
    e = jnp.exp(sh - m)
    esk = jnp.exp(sk - m)
    rz = 1.0 / (jnp.sum(e, axis=0, keepdims=True) + esk)
    return e * rz, esk, rz


def _stage_kv(kvh, k_ref, v_ref, kg_ref, lo, kA, kB, vA, vB):
    kv = k_ref[...].astype(F32)
    rk = _head_rstd(kv, lo)
    kn = kv * rk * kg_ref[...]
    mine = lax.broadcasted_iota(jnp.int32, (1, 128), 1) // HEAD_DIM == kvh
    ksel = jnp.where(mine, kn, 0.0)
    vsel = jnp.where(mine, v_ref[...].astype(F32), 0.0)
    krol = pltpu.roll(ksel, HEAD_DIM, 1)
    vrol = pltpu.roll(vsel, HEAD_DIM, 1)
    first = kvh == 0
    zeros = jnp.zeros((BLOCK, 128), BF16)
    for ref, val in ((kA, jnp.where(first, ksel, krol)), (kB, jnp.where(first, krol, ksel)),
                     (vA, jnp.where(first, vsel, vrol)), (vB, jnp.where(first, vrol, vsel))):
        ref[0:BLOCK, :] = zeros
        ref[BLOCK:, :] = val.astype(BF16)


def _attn_fwd_call(u, qg2, kg2, sinks, slopes, B, S):
    T = B * S
    nb = S // BLOCK
    half = Q_W // 2
    npair = GROUP // 2

    def body(sinks_ref, slopes_ref, q_ref, k_ref, v_ref, qg_ref, kg_ref, o_ref, kA, kB, vA, vB, btab, qn_buf, s_buf, p_buf):
        kvh = pl.program_id(1)
        lo = _lane_lo()
        _stage_kv(kvh, k_ref, v_ref, kg_ref, lo, kA, kB, vA, vB)
        _fill_bias(btab, kvh, slopes_ref)
        qgs = qg_ref[...] * ATTN_SCALE

        def block(n, slot):
            qn_scr, s_scr, p_scr = qn_buf.at[slot], s_buf.at[slot], p_buf.at[slot]
            r0 = pl.multiple_of(n * BLOCK, BLOCK)
            win = pl.ds(r0, 2 * BLOCK)
            k2t = jnp.concatenate([kA[win, :], kB[win, :]], axis=0)
            v2 = jnp.concatenate([vA[win, :], vB[win, :]], axis=0)
            var = jnp.minimum(n, 1)
            for p in range(npair):
                pc = slice(128 * p, 128 * (p + 1))
                qp = q_ref[pl.ds(r0, BLOCK), pc].astype(F32)
                qn_scr[pc, :] = (qp * _head_rstd(qp, lo) * qgs).astype(BF16)
            s_scr[...] = _dot_nt(k2t, qn_scr[...])
            for p in range(npair):
                for hf in range(2):
                    g = 2 * p + hf
                    pn, _, _ = _softmax_cols(s_scr, btab, p, var, g, hf, sinks_ref[kvh * GROUP + g])
                    p_scr[2 * BLOCK * hf:2 * BLOCK * (hf + 1), 128 * p:128 * (p + 1)] = pn.astype(BF16)
            o_all = _dot_tn(v2, p_scr[...]).T
            for p in range(npair):
                pc = slice(128 * p, 128 * (p + 1))
                o_ref[pl.ds(r0, BLOCK), pc] = o_all[pc, :].astype(BF16)

        def two_blocks(i, carry):
            block(2 * i, 0)
            block(2 * i + 1, 1)
            return carry

        lax.fori_loop(0, nb // 2, two_blocks, 0)

    smem = pl.BlockSpec(memory_space=pltpu.SMEM)
    kvbuf = pltpu.VMEM((S + BLOCK, 128), BF16)
    pair_scores = (2, npair * BLOCK, 4 * BLOCK)
    return pl.pallas_call(
        body, name="attn_fwd", grid=(B, N_KV_HEADS),
        in_specs=[smem, smem,
                  pl.BlockSpec((S, half), lambda b, h: (b, COL_Q // half + h)),
                  pl.BlockSpec((S, 128), lambda b, h: (b, COL_K // 128)),
                  pl.BlockSpec((S, 128), lambda b, h: (b, COL_V // 128)),
                  _full((1, 128)), _full((1, 128))],
        out_specs=pl.BlockSpec((S, half), lambda b, h: (b, h)),
        out_shape=jax.ShapeDtypeStruct((T, Q_W), BF16),
        scratch_shapes=[kvbuf, kvbuf, kvbuf, kvbuf, pltpu.VMEM((2, GROUP, 2 * BLOCK, BLOCK), F32),
                        pltpu.VMEM((2, npair * BLOCK, 128), BF16), pltpu.VMEM(pair_scores, F32), pltpu.VMEM(pair_scores, BF16)],
        compiler_params=_cp("parallel", "arbitrary"),
    )(sinks, slopes, u, u, u, qg2, kg2)


def _merge_call(c, o, u, x, wc, wa, wm, g_ffn, tm=512):
    T = x.shape[0]

    def body(c_ref, o_ref, gc_ref, ga_ref, x_ref, wc_ref, wa_ref, wm_ref, g_ref, yc_ref, ya_ref, z_ref, h_ref, hn_ref):
        yc = _dot(c_ref[...], wc_ref[...])
        ya = _dot(o_ref[...], wa_ref[...])
        yc_ref[...] = yc.astype(BF16)
        ya_ref[...] = ya.astype(BF16)
        z = (_sigmoid(gc_ref[...].astype(F32)) * yc + _sigmoid(ga_ref[...].astype(F32)) * ya).astype(BF16)
        z_ref[...] = z
        h = x_ref[...] + _dot(z, wm_ref[...])
        h_ref[...] = h
        r = lax.rsqrt(jnp.mean(h * h, axis=-1, keepdims=True) + EPS)
        hn_ref[...] = (h * r * g_ref[...]).astype(BF16)

    sq = _full((D_MODEL, D_MODEL))
    row = _rows(tm, D_MODEL)
    f32 = jax.ShapeDtypeStruct((T, D_MODEL), F32)
    b16 = jax.ShapeDtypeStruct((T, D_MODEL), BF16)
    return pl.pallas_call(
        body, name="merge_fwd", grid=(T // tm,),
        in_specs=[row, row, _rows(tm, D_MODEL, COL_GC // D_MODEL), _rows(tm, D_MODEL, COL_GA // D_MODEL), row,
                  sq, sq, sq, _full((1, D_MODEL))],
        out_specs=[row] * 5,
        out_shape=[b16, b16, b16, f32, b16],
        compiler_params=_cp("parallel"),
    )(c, o, u, u, x, wc, wa, wm, g_ffn)


def _ffn_in_call(hn, w, tm=256):
    T = hn.shape[0]

    def body(hn_ref, w_ref, gu_ref, act_ref):
        gu = _dot(hn_ref[...], w_ref[...])
        gu_ref[...] = gu.astype(BF16)
        g = gu[:, :D_FF]
        act_ref[...] = (g * _sigmoid(g) * gu[:, D_FF:]).astype(BF16)

    return pl.pallas_call(
        body, name="ffn_in", grid=(T // tm,),
        in_specs=[_rows(tm, D_MODEL), _full(w.shape)],
        out_specs=[_rows(tm, 2 * D_FF), _rows(tm, D_FF)],
        out_shape=[jax.ShapeDtypeStruct((T, 2 * D_FF), BF16), jax.ShapeDtypeStruct((T, D_FF), BF16)],
        compiler_params=_cp("parallel"),
    )(hn, w)


def _ffn_out_call(act, wd, h, tgt, tm=512):
    T = h.shape[0]

    def body(act_ref, wd_ref, h_ref, t_ref, sse_ref, d_ref, db_ref):
        e = h_ref[...] + _dot(act_ref[...], wd_ref[...]) - t_ref[...]

        @pl.when(pl.program_id(0) == 0)
        def _():
            sse_ref[...] = jnp.zeros_like(sse_ref)

        sse_ref[...] += jnp.sum(e * e)
        d = e * (1.0 / D_MODEL)
        d_ref[...] = d
        db_ref[...] = d.astype(BF16)

    row = _rows(tm, D_MODEL)
    return pl.pallas_call(
        body, name="ffn_out", grid=(T // tm,),
        in_specs=[_rows(tm, D_FF), _full(wd.shape), row, row],
        out_specs=[_full((8, 128)), row, row],
        out_shape=[jax.ShapeDtypeStruct((8, 128), F32), jax.ShapeDtypeStruct((T, D_MODEL), F32),
                   jax.ShapeDtypeStruct((T, D_MODEL), BF16)],
        compiler_params=_cp("arbitrary"),
    )(act, wd, h, tgt)


def _dw_call(name, a, b, tn, tk=1024):
    T, K = a.shape
    N = b.shape[1]
    tk = min(tk, T)
    nk = T // tk

    def body(a_ref, b_ref, o_ref, acc_ref):
        k = pl.program_id(1)

        @pl.when(k == 0)
        def _():
            acc_ref[...] = jnp.zeros_like(acc_ref)

        acc_ref[...] += _dot_tn(a_ref[...], b_ref[...])

        @pl.when(k == nk - 1)
        def _():
            o_ref[...] = acc_ref[...].astype(BF16)

    return pl.pallas_call(
        body, name=name, grid=(N // tn, nk),
        in_specs=[pl.BlockSpec((tk, K), lambda j, k: (k, 0)), pl.BlockSpec((tk, tn), lambda j, k: (k, j))],
        out_specs=pl.BlockSpec((K, tn), lambda j, k: (0, j)),
        out_shape=jax.ShapeDtypeStruct((K, N), BF16),
        scratch_shapes=[pltpu.VMEM((K, tn), F32)],
        compiler_params=_cp("parallel", "arbitrary"),
    )(a, b)


def _rms_bwd(dy, xv, g):
    r = lax.rsqrt(jnp.mean(xv * xv, axis=-1, keepdims=True) + EPS)
    t = dy * g
    dx = r * t - xv * (r * r * r * jnp.mean(t * xv, axis=-1, keepdims=True))
    return dx, dy * xv * r


def _ffn_down_bwd_call(dout_b, wd, gu, tm=512):
    T = dout_b.shape[0]

    def body(d_ref, wd_ref, g_ref, up_ref, o_ref):
        dact = _dot_nt(d_ref[...], wd_ref[...])
        g = g_ref[...].astype(F32)
        sg = _sigmoid(g)
        o_ref[:, :D_FF] = (dact * up_ref[...].astype(F32) * (sg * (1.0 + g * (1.0 - sg)))).astype(BF16)
        o_ref[:, D_FF:] = (dact * (g * sg)).astype(BF16)

    return pl.pallas_call(
        body, name="ffn_down_bwd", grid=(T // tm,),
        in_specs=[_rows(tm, D_MODEL), _full(wd.shape), _rows(tm, D_FF, 0), _rows(tm, D_FF, 1)],
        out_specs=_rows(tm, 2 * D_FF),
        out_shape=jax.ShapeDtypeStruct((T, 2 * D_FF), BF16),
        compiler_params=_cp("parallel"),
    )(dout_b, wd, gu, gu)


def _ffn_in_bwd_call(dgu, w, h, g_ffn, dout, tm=512):
    T = h.shape[0]

    def body(dgu_ref, w_ref, h_ref, g_ref, d_ref, dh_ref, dhb_ref, dg_ref):
        dhn = _dot_nt(dgu_ref[...], w_ref[...])
        dx, dgt = _rms_bwd(dhn, h_ref[...], g_ref[...])
        dh = d_ref[...] + dx
        dh_ref[...] = dh
        dhb_ref[...] = dh.astype(BF16)

        @pl.when(pl.program_id(0) == 0)
        def _():
            dg_ref[...] = jnp.zeros_like(dg_ref)

        dg_ref[...] += jnp.sum(dgt, axis=0, keepdims=True)

    row = _rows(tm, D_MODEL)
    return pl.pallas_call(
        body, name="ffn_in_bwd", grid=(T // tm,),
        in_specs=[_rows(tm, 2 * D_FF), _full(w.shape), row, _full((1, D_MODEL)), row],
        out_specs=[row, row, _full((1, D_MODEL))],
        out_shape=[jax.ShapeDtypeStruct((T, D_MODEL), F32), jax.ShapeDtypeStruct((T, D_MODEL), BF16),
                   jax.ShapeDtypeStruct((1, D_MODEL), F32)],
        compiler_params=_cp("arbitrary"),
    )(dgu, w, h, g_ffn, dout)


def _merge_bwd_call(dh_b, wm, wc, wa, u, yc, ya, tm=512):
    T = dh_b.shape[0]

    def body(dh_ref, wm_ref, wc_ref, wa_ref, gc_ref, ga_ref, yc_ref, ya_ref, dyc_ref, dya_ref, dug_ref, dcp_ref, do_ref):
        dz = _dot_nt(dh_ref[...], wm_ref[...])
        sgc = _sigmoid(gc_ref[...].astype(F32))
        sga = _sigmoid(ga_ref[...].astype(F32))
        dyc = (dz * sgc).astype(BF16)
        dya = (dz * sga).astype(BF16)
        dyc_ref[...] = dyc
        dya_ref[...] = dya
        dug_ref[:, :D_MODEL] = (dz * yc_ref[...].astype(F32) * (sgc * (1.0 - sgc))).astype(BF16)
        dug_ref[:, D_MODEL:] = (dz * ya_ref[...].astype(F32) * (sga * (1.0 - sga))).astype(BF16)
        dcp_ref[...] = _dot_nt(dyc, wc_ref[...])
        do_ref[...] = _dot_nt(dya, wa_ref[...]).astype(BF16)

    sq = _full((D_MODEL, D_MODEL))
    row = _rows(tm, D_MODEL)
    b16 = jax.ShapeDtypeStruct((T, D_MODEL), BF16)
    return pl.pallas_call(
        body, name="merge_bwd", grid=(T // tm,),
        in_specs=[row, sq, sq, sq, _rows(tm, D_MODEL, COL_GC // D_MODEL), _rows(tm, D_MODEL, COL_GA // D_MODEL), row, row],
        out_specs=[row, row, _rows(tm, 2 * D_MODEL), row, row],
        out_shape=[b16, b16, jax.ShapeDtypeStruct((T, 2 * D_MODEL), BF16), jax.ShapeDtypeStruct((T, D_MODEL), F32), b16],
        compiler_params=_cp("parallel"),
    )(dh_b, wm, wc, wa, u, u, yc, ya)


def _conv_bwd_call(dcp, hc, u, conv_w, ln_g, ln_b, B, S, ts=256):
    T = B * S
    nS = S // ts
    per32 = ts // HALO
    last32 = T // HALO - 1

    def ln_bwd(dcp_v, hc_v, g, bb):
        mu = jnp.mean(hc_v, axis=-1, keepdims=True)
        xc = hc_v - mu
        rstd = lax.rsqrt(jnp.mean(xc * xc, axis=-1, keepdims=True) + EPS)
        xhat = xc * rstd
        y = xhat * g + bb
        sy = _sigmoid(y)
        dy = dcp_v * (sy * (1.0 + y * (1.0 - sy)))
        dxh = dy * g
        dhc = rstd * (dxh - jnp.mean(dxh, axis=-1, keepdims=True) - xhat * jnp.mean(dxh * xhat, axis=-1, keepdims=True))
        return dhc, dy * xhat, dy

    def body(dcp_ref, dcpn_ref, hc_ref, hcn_ref, a_ref, gate_ref, ap_ref, gp_ref, w_ref, g_ref, bb_ref,
             du_ref, dw_ref, db_ref, dlg_ref, dlb_ref, dhext, hgext, dhsh, hgsh, acc_ref):
        b = pl.program_id(0)
        s = pl.program_id(1)

        @pl.when((b == 0) & (s == 0))
        def _():
            dw_ref[...] = jnp.zeros_like(dw_ref)
            db_ref[...] = jnp.zeros_like(db_ref)
            dlg_ref[...] = jnp.zeros_like(dlg_ref)
            dlb_ref[...] = jnp.zeros_like(dlb_ref)

        dhc, dgt, dbt = ln_bwd(dcp_ref[...], hc_ref[...], g_ref[...], bb_ref[...])
        dhn, _, _ = ln_bwd(dcpn_ref[...], hcn_ref[...], g_ref[...], bb_ref[...])
        dhext[0:ts, :] = dhc
        dhext[ts:, :] = jnp.where(s < nS - 1, dhn, 0.0)
        db_ref[...] += jnp.sum(dhc, axis=0, keepdims=True)
        dlg_ref[...] += jnp.sum(dgt, axis=0, keepdims=True)
        dlb_ref[...] += jnp.sum(dbt, axis=0, keepdims=True)

        sg = _sigmoid(gate_ref[...].astype(F32))
        av = a_ref[...].astype(F32)
        hgext[0:HALO, :] = jnp.where(s > 0, ap_ref[...].astype(F32) * _sigmoid(gp_ref[...].astype(F32)), 0.0)
        hgext[HALO:, :] = av * sg
        _fill_shifted(dhext, dhsh)
        _fill_shifted(hgext, hgsh)

        def colblock(cb, carry):
            cols = pl.ds(pl.multiple_of(cb * 128, 128), 128)
            outs = _conv_taps(w_ref, dhext, dhsh, 0, ts, cb, reverse=True)
            for rb, acc in enumerate(outs):
                acc_ref[rb * 128:(rb + 1) * 128, cols] = acc
            dhc_rows = [dhext[rb * 128:(rb + 1) * 128, cols] for rb in range(ts // 128)]
            for j in range(CONV_WIDTH):
                part = jnp.zeros((8, 128), F32)
                for rb in range(ts // 128):
                    glu = _window(hgext, hgsh, HALO - (CONV_WIDTH - 1) + j + rb * 128, cols)
                    part = part + jnp.sum((dhc_rows[rb] * glu).reshape(16, 8, 128), axis=0)
                dw_ref[8 * j:8 * (j + 1), cols] += part
            return carry

        lax.fori_loop(0, C_CONV // 128, colblock, 0)
        dglu = acc_ref[...]
        du_ref[:, :C_CONV] = (dglu * sg).astype(BF16)
        du_ref[:, C_CONV:] = (dglu * av * (sg * (1.0 - sg))).astype(BF16)

    cur = lambda cb: pl.BlockSpec((ts, C_CONV), lambda b, s: (b * nS + s, cb))
    prv = lambda cb: pl.BlockSpec((HALO, C_CONV), lambda b, s: (jnp.maximum(b * (S // HALO) + s * per32 - 1, 0), cb))
    nxt = pl.BlockSpec((HALO, C_CONV), lambda b, s: (jnp.minimum(b * (S // HALO) + (s + 1) * per32, last32), 0))
    vec = _full((1, C_CONV))
    return pl.pallas_call(
        body, name="conv_bwd", grid=(B, nS),
        in_specs=[cur(0), nxt, cur(0), nxt, cur(0), cur(1), prv(0), prv(1), _full((CONV_WIDTH, C_CONV)), vec, vec],
        out_specs=[pl.BlockSpec((ts, 2 * C_CONV), lambda b, s: (b * nS + s, 0)), _full((8 * CONV_WIDTH, C_CONV)), vec, vec, vec],
        out_shape=[jax.ShapeDtypeStruct((T, 2 * C_CONV), BF16), jax.ShapeDtypeStruct((8 * CONV_WIDTH, C_CONV), F32),
                   jax.ShapeDtypeStruct((1, C_CONV), F32), jax.ShapeDtypeStruct((1, C_CONV), F32),
                   jax.ShapeDtypeStruct((1, C_CONV), F32)],
        scratch_shapes=[pltpu.VMEM((ts + HALO, C_CONV), F32), pltpu.VMEM((HALO + ts, C_CONV), F32),
                        pltpu.VMEM((7, ts + HALO, C_CONV), F32), pltpu.VMEM((7, HALO + ts, C_CONV), F32),
                        pltpu.VMEM((ts, C_CONV), F32)],
        compiler_params=_cp("arbitrary", "arbitrary"),
    )(dcp, dcp, hc, hc, u, u, u, u, conv_w, ln_g, ln_b)


def _attn_bwd_call(u, do, qg2, kg2, sinks, slopes, B, S):
    T = B * S
    nb = S // BLOCK
    half = Q_W // 2
    npair = GROUP // 2

    def body(sinks_ref, slopes_ref, q_ref, k_ref, v_ref, do_ref, qg_ref, kg_ref,
             dq_ref, dkv_ref, dqg_ref, dkg_ref, dsk_ref, kA, kB, vA, vB, dkn_acc, dv_acc, dkv_keep,
             btab, s_buf, dp_buf, p_buf, ds_buf, qn_buf, do_buf, rq_buf):
        b = pl.program_id(0)
        kvh = pl.program_id(1)
        lo = _lane_lo()
        lane = lax.broadcasted_iota(jnp.int32, (1, 128), 1)
        first = kvh == 0

        @pl.when((b == 0) & first)
        def _():
            dqg_ref[...] = jnp.zeros_like(dqg_ref)
            dkg_ref[...] = jnp.zeros_like(dkg_ref)
            dsk_ref[...] = jnp.zeros_like(dsk_ref)

        _stage_kv(kvh, k_ref, v_ref, kg_ref, lo, kA, kB, vA, vB)
        dkn_acc[...] = jnp.zeros_like(dkn_acc)
        dv_acc[...] = jnp.zeros_like(dv_acc)
        _fill_bias(btab, kvh, slopes_ref)
        qgs = qg_ref[...] * ATTN_SCALE

        def fold(acc):
            canon = jnp.where(lo, acc[0:2 * BLOCK] + pltpu.roll(acc[2 * BLOCK:], HEAD_DIM, 1), 0.0)
            return jnp.where(first, canon, pltpu.roll(canon, HEAD_DIM, 1))

        def block(n, slot):
            s_scr, dp_scr, p_scr, ds_scr = s_buf.at[slot], dp_buf.at[slot], p_buf.at[slot], ds_buf.at[slot]
            qn_scr, do_scr, rq_scr = qn_buf.at[slot], do_buf.at[slot], rq_buf.at[slot]
            r0 = pl.multiple_of(n * BLOCK, BLOCK)
            win = pl.ds(r0, 2 * BLOCK)
            rows = pl.ds(r0, BLOCK)
            k2t = jnp.concatenate([kA[win, :], kB[win, :]], axis=0)
            v2 = jnp.concatenate([vA[win, :], vB[win, :]], axis=0)
            var = jnp.minimum(n, 1)
            for p in range(npair):
                pc = slice(128 * p, 128 * (p + 1))
                qp = q_ref[rows, pc].astype(F32)
                rq = _head_rstd(qp, lo)
                rq_scr[pc, :] = rq
                qn_scr[pc, :] = (qp * rq * qgs).astype(BF16)
                do_scr[pc, :] = do_ref[rows, pc]
            s_scr[...] = _dot_nt(k2t, qn_scr[...])
            dp_scr[...] = _dot_nt(v2, do_scr[...])
            for p in range(npair):
                for hf in range(2):
                    g = 2 * p + hf
                    keys, qs = slice(2 * BLOCK * hf, 2 * BLOCK * (hf + 1)), slice(128 * p, 128 * (p + 1))
                    pn, esk, rz = _softmax_cols(s_scr, btab, p, var, g, hf, sinks_ref[kvh * GROUP + g])
                    dph = dp_scr[keys, qs]
                    delta = jnp.sum(pn * dph, axis=0, keepdims=True)
                    ds_scr[keys, qs] = (pn * (dph - delta)).astype(BF16)
                    p_scr[keys, qs] = pn.astype(BF16)
                    dsk_ref[...] += jnp.where(lane == kvh * GROUP + g, -jnp.sum(esk * rz * delta), 0.0)
            dqs_all = _dot_tn(k2t, ds_scr[...]).T
            for p in range(npair):
                pc = slice(128 * p, 128 * (p + 1))
                qp = q_ref[rows, pc].astype(F32)
                rq = rq_scr[pc, :]
                dqs = dqs_all[pc, :]
                t = dqs * qgs
                dq = rq * t - qp * (rq * rq * rq * (_half_sums(t * qp, lo) * (1.0 / HEAD_DIM)))
                dq_ref[rows, pc] = dq.astype(BF16)
                dqg_ref[...] += jnp.sum(dqs * qp * rq, axis=0, keepdims=True) * ATTN_SCALE
            dkn_acc[win, :] += fold(_dot(ds_scr[...], qn_scr[...]))
            dv_acc[win, :] += fold(_dot(p_scr[...], do_scr[...]))

        def two_blocks(i, carry):
            block(2 * i, 0)
            block(2 * i + 1, 1)
            return carry

        lax.fori_loop(0, nb // 2, two_blocks, 0)

        dkn = dkn_acc[BLOCK:, :]
        kv = k_ref[...].astype(F32)
        kg = kg_ref[...]
        rk = _head_rstd(kv, lo)
        t = dkn * kg
        dk = rk * t - kv * (rk * rk * rk * (_half_sums(t * kv, lo) * (1.0 / HEAD_DIM)))
        dkg_ref[...] += jnp.sum(dkn * kv * rk, axis=0, keepdims=True)
        dv = dv_acc[BLOCK:, :]

        @pl.when(first)
        def _():
            dkv_keep[:, :128] = dk
            dkv_keep[:, 128:] = dv

        @pl.when(kvh == N_KV_HEADS - 1)
        def _():
            dkv_ref[:, :128] = (dkv_keep[:, :128] + dk).astype(BF16)
            dkv_ref[:, 128:] = (dkv_keep[:, 128:] + dv).astype(BF16)

    smem = pl.BlockSpec(memory_space=pltpu.SMEM)
    kvbuf = pltpu.VMEM((S + BLOCK, 128), BF16)
    accbuf = pltpu.VMEM((S + BLOCK, 128), F32)
    vec = _full((1, 128))
    return pl.pallas_call(
        body, name="attn_bwd", grid=(B, N_KV_HEADS),
        in_specs=[smem, smem,
                  pl.BlockSpec((S, half), lambda b, h: (b, COL_Q // half + h)),
                  pl.BlockSpec((S, 128), lambda b, h: (b, COL_K // 128)),
                  pl.BlockSpec((S, 128), lambda b, h: (b, COL_V // 128)),
                  pl.BlockSpec((S, half), lambda b, h: (b, h)),
                  vec, vec],
        out_specs=[pl.BlockSpec((S, half), lambda b, h: (b, h)), pl.BlockSpec((S, 256), lambda b, h: (b, 0)), vec, vec, vec],
        out_shape=[jax.ShapeDtypeStruct((T, Q_W), BF16), jax.ShapeDtypeStruct((T, 2 * KV_W), BF16),
                   jax.ShapeDtypeStruct((1, 128), F32), jax.ShapeDtypeStruct((1, 128), F32), jax.ShapeDtypeStruct((1, 128), F32)],
        scratch_shapes=[kvbuf, kvbuf, kvbuf, kvbuf, accbuf, accbuf, pltpu.VMEM((S, 256), F32),
                        pltpu.VMEM((2, GROUP, 2 * BLOCK, BLOCK), F32),
                        pltpu.VMEM((2, npair * BLOCK, 4 * BLOCK), F32), pltpu.VMEM((2, npair * BLOCK, 4 * BLOCK), F32),
                        pltpu.VMEM((2, npair * BLOCK, 4 * BLOCK), BF16), pltpu.VMEM((2, npair * BLOCK, 4 * BLOCK), BF16),
                        pltpu.VMEM((2, npair * BLOCK, 128), BF16), pltpu.VMEM((2, npair * BLOCK, 128), BF16),
                        pltpu.VMEM((2, npair * BLOCK, 128), F32)],
        compiler_params=_cp("arbitrary", "arbitrary"),
    )(sinks, slopes, u, u, u, do, qg2, kg2)


def _inproj_bwd_call(du_conv, du_g, du_q, du_kv, w, x, g, dh, tm=512):
    T = x.shape[0]

    def body(dc_ref, dg_ref, dq_ref, dkv_ref, w_ref, x_ref, g_ref, dh_ref, gx_ref, dgm_ref):
        dxn = _dot_nt(dc_ref[...], w_ref[:, COL_A:COL_GC])
        dxn += _dot_nt(dg_ref[...], w_ref[:, COL_GC:COL_Q])
        dxn += _dot_nt(dq_ref[...], w_ref[:, COL_Q:COL_K])
        dxn += _dot_nt(dkv_ref[...], w_ref[:, COL_K:])
        dx, dgt = _rms_bwd(dxn, x_ref[...], g_ref[...])
        gx_ref[...] = dh_ref[...] + dx

        @pl.when(pl.program_id(0) == 0)
        def _():
            dgm_ref[...] = jnp.zeros_like(dgm_ref)

        dgm_ref[...] += jnp.sum(dgt, axis=0, keepdims=True)

    row = _rows(tm, D_MODEL)
    return pl.pallas_call(
        body, name="inproj_bwd", grid=(T // tm,),
        in_specs=[_rows(tm, 2 * C_CONV), _rows(tm, 2 * D_MODEL), _rows(tm, Q_W), _rows(tm, 2 * KV_W), _full(w.shape),
                  row, _full((1, D_MODEL)), row],
        out_specs=[row, _full((1, D_MODEL))],
        out_shape=[jax.ShapeDtypeStruct((T, D_MODEL), F32), jax.ShapeDtypeStruct((1, D_MODEL), F32)],
        compiler_params=_cp("arbitrary"),
    )(du_conv, du_g, du_q, du_kv, w, x, g, dh)


def _adamw_math(w, g, m, v):
    m = ADAM_B1 * m + (1.0 - ADAM_B1) * g
    v = ADAM_B2 * v + (1.0 - ADAM_B2) * (g * g)
    m_hat = m / (1.0 - ADAM_B1 ** ADAM_STEP)
    v_hat = v / (1.0 - ADAM_B2 ** ADAM_STEP)
    delta = -ADAM_LR * (m_hat / (jnp.sqrt(v_hat) + ADAM_EPS) + ADAM_WD * w)
    return delta, m, v


def _adamw_sum_call(name, parts, own, w, m, v, tr):
    R, C = w.shape

    def body(p_ref, own_ref, w_ref, m_ref, v_ref, g_ref, d_ref, nm_ref, nv_ref):
        x, y, c = _position()
        me = 4 * x + 2 * y + c
        g = jnp.zeros((tr, C), F32)
        for s in range(N_DEV):
            g = g + jnp.where(me == s, own_ref[...], p_ref[s]).astype(F32)
        g_ref[...] = g
        d_ref[...], nm_ref[...], nv_ref[...] = _adamw_math(w_ref[...], g, m_ref[...], v_ref[...])

    blk = pl.BlockSpec((tr, C), lambda i: (i, 0))
    out = jax.ShapeDtypeStruct((R, C), F32)
    return pl.pallas_call(
        body, name=name, grid=(R // tr,),
        in_specs=[pl.BlockSpec((N_DEV, tr, C), lambda i: (0, i, 0)), blk, blk, blk, blk],
        out_specs=[blk] * 4, out_shape=[out] * 4,
        compiler_params=_cp("parallel"),
    )(parts, own, w, m, v)


def _adamw_small_call(name, g, w, m, v):
    def body(g_ref, w_ref, m_ref, v_ref, d_ref, nm_ref, nv_ref):
        d_ref[...], nm_ref[...], nv_ref[...] = _adamw_math(w_ref[...], g_ref[...], m_ref[...], v_ref[...])

    out = jax.ShapeDtypeStruct(w.shape, F32)
    return pl.pallas_call(body, name=name, out_shape=[out] * 3)(g, w, m, v)


def _position():
    return lax.axis_index("x"), lax.axis_index("y"), lax.axis_index("c")


def _peer(x, y, c, rel):
    fx, fy, fc = (rel >> 2) & 1, (rel >> 1) & 1, rel & 1
    px = 1 - x if fx else x
    py = 1 - y if fy else y
    pc = 1 - c if fc else c
    return (px, py, pc), 4 * px + 2 * py + pc


def _gather_weights_call(shards):
    n = len(shards)

    def body(*refs):
        ins, outs = refs[:n], refs[n:2 * n]
        send_sems, recv_sems, local_sems = refs[2 * n:]
        x, y, c = _position()
        me = 4 * x + 2 * y + c
        sib, sib_id = _peer(x, y, c, 1)
        chips = [_peer(x, y, c, rel) for rel in (4, 2, 6)]

        def copy(i, k, src, slot, to):
            return pltpu.make_async_remote_copy(
                src_ref=src, dst_ref=outs[i].at[slot], send_sem=send_sems.at[i, k], recv_sem=recv_sems.at[i, k],
                device_id=to, device_id_type=MESH)

        mine = [pltpu.make_async_copy(ins[i], outs[i].at[me], local_sems.at[i]) for i in range(n)]
        for cp in mine:
            cp.start()
        first = []
        for i in range(n):
            first.append(copy(i, 0, ins[i], me, sib))
            first += [copy(i, 1 + j, ins[i], me, dev) for j, (dev, _) in enumerate(chips)]
        for cp in first:
            cp.start()
        passed = []
        for i in range(n):
            for j, (dev, slot) in enumerate(chips):
                copy(i, 1 + j, ins[i], slot, dev).wait_recv()
                fwd = copy(i, 4 + j, outs[i].at[slot], slot, sib)
                fwd.start()
                passed.append(fwd)
        for i in range(n):
            copy(i, 0, ins[i], sib_id, sib).wait_recv()
            for j, (_, slot) in enumerate(chips):
                copy(i, 4 + j, ins[i], slot ^ 1, sib).wait_recv()
        for cp in first + passed:
            cp.wait_send()
        for cp in mine:
            cp.wait()

    hbm = pl.BlockSpec(memory_space=pl.ANY)
    return pl.pallas_call(
        body, name="gather_weights",
        in_specs=[hbm] * n, out_specs=[hbm] * n,
        out_shape=[jax.ShapeDtypeStruct((N_DEV,) + s.shape, s.dtype) for s in shards],
        scratch_shapes=[pltpu.SemaphoreType.DMA((n, 7)), pltpu.SemaphoreType.DMA((n, 7)), pltpu.SemaphoreType.DMA((n,))],
    )(*shards)


_HBM = pl.BlockSpec(memory_space=pltpu.HBM)
_SEM = pl.BlockSpec(memory_space=pltpu.SEMAPHORE)
_EFFECT = pltpu.SideEffectType.DATAFLOW_SIDE_EFFECTING


def _exchange_copies(scatter, incoming, src_refs, land_refs, send_sems, recv_sems):
    x, y, c = _position()
    me = 4 * x + 2 * y + c
    out = []
    for i, (src, land) in enumerate(zip(src_refs, land_refs)):
        for k in range(N_DEV - 1):
            dev, dev_id = _peer(x, y, c, k + 1)
            j = i * (N_DEV - 1) + k
            out.append(pltpu.make_async_remote_copy(
                src_ref=src.at[dev_id] if scatter else src, dst_ref=land.at[dev_id if incoming else me],
                send_sem=send_sems.at[j], recv_sem=recv_sems.at[j], device_id=dev, device_id_type=MESH))
    return out


def _exchange_start_call(name, arrays, scatter):
    n = len(arrays)
    lands = [lax.empty((N_DEV,) + (a.shape[1:] if scatter else a.shape), a.dtype) for a in arrays]

    def body(*refs):
        src_refs, land_refs = refs[:n], refs[n:2 * n]
        send_sems, recv_sems, token = refs[2 * n], refs[2 * n + 1], refs[-1]
        for copy in _exchange_copies(scatter, False, src_refs, land_refs, send_sems, recv_sems):
            copy.start()
        token[...] = jnp.zeros_like(token)

    sems = pltpu.SemaphoreType.DMA((n * (N_DEV - 1),))
    res = pl.pallas_call(
        body, name=name,
        out_shape=[sems, sems] + [pltpu.HBM(a.shape, a.dtype) for a in arrays] + [pltpu.HBM(l.shape, l.dtype) for l in lands]
        + [jax.ShapeDtypeStruct((8, 128), F32)],
        in_specs=[_HBM] * (2 * n), out_specs=[_SEM, _SEM] + [_HBM] * (2 * n) + [pl.BlockSpec(memory_space=pltpu.VMEM)],
        input_output_aliases={i: 2 + i for i in range(2 * n)},
        compiler_params=pltpu.CompilerParams(has_side_effects=_EFFECT),
    )(*[pltpu.with_memory_space_constraint(a, pltpu.HBM) for a in list(arrays) + lands])
    return res[0], res[1], res[2:2 + n], res[2 + n:2 + 2 * n], res[-1]


def _exchange_wait_call(name, started, scatter, after):
    send_sems, recv_sems, srcs, lands, _ = started
    n = len(srcs)

    def body(*refs):
        src_refs, land_refs = refs[:n], refs[n:2 * n]
        send_s, recv_s = refs[2 * n], refs[2 * n + 1]
        for copy in _exchange_copies(scatter, False, src_refs, land_refs, send_s, recv_s):
            copy.wait_send()
        for copy in _exchange_copies(scatter, True, src_refs, land_refs, send_s, recv_s):
            copy.wait_recv()

    res = pl.pallas_call(
        body, name=name,
        out_shape=[pltpu.HBM(a.shape, a.dtype) for a in list(srcs) + list(lands)],
        in_specs=[_HBM] * (2 * n) + [_SEM, _SEM, pl.BlockSpec(memory_space=pl.ANY)], out_specs=[_HBM] * (2 * n),
        input_output_aliases={i: i for i in range(2 * n)},
        compiler_params=pltpu.CompilerParams(has_side_effects=_EFFECT),
    )(*srcs, *lands, send_sems, recv_sems, after)
    return res[n:]


def _allsum_small_call(part):
    P = part.shape[0]

    def body(p_ref, o_ref, all_ref, send_sems, recv_sems):
        x, y, c = _position()
        me = 4 * x + 2 * y + c
        peers = [_peer(x, y, c, rel) for rel in range(1, N_DEV)]
        all_ref[me] = p_ref[...]

        def copy(k, slot, dev):
            return pltpu.make_async_remote_copy(
                src_ref=p_ref, dst_ref=all_ref.at[slot], send_sem=send_sems.at[k], recv_sem=recv_sems.at[k],
                device_id=dev, device_id_type=MESH)

        sends = [copy(k, me, dev) for k, (dev, _) in enumerate(peers)]
        for cp in sends:
            cp.start()
        for k, (dev, dev_id) in enumerate(peers):
            copy(k, dev_id, dev).wait_recv()
        for cp in sends:
            cp.wait_send()
        tot = all_ref[0]
        for s in range(1, N_DEV):
            tot = tot + all_ref[s]
        o_ref[...] = tot

    vm = pl.BlockSpec(memory_space=pltpu.VMEM)
    return pl.pallas_call(
        body, name="allsum_small", in_specs=[vm], out_specs=vm,
        out_shape=jax.ShapeDtypeStruct(part.shape, F32),
        scratch_shapes=[pltpu.VMEM((N_DEV, P, 128), F32), pltpu.SemaphoreType.DMA((7,)), pltpu.SemaphoreType.DMA((7,))],
    )(part)


def _pad_rows(v, rows):
    flat = v.reshape(-1)
    return jnp.pad(flat, (0, rows * 128 - flat.shape[0])).reshape(rows, 128)


def _blocks_by_cols(w):
    K, N = w.shape
    return w.reshape(K, N_DEV, N // N_DEV).transpose(1, 0, 2)


def _cols_from_blocks(wb):
    n, K, C = wb.shape
    return wb.transpose(1, 0, 2).reshape(K, n * C)


def _local_step(x, tgt, w_in_i, conv_w, conv_b, ln_g, ln_b, qg, kg, sinks, g_mix, g_ffn, token0, rest_weights, grads_ready, B, S):
    row = lambda v: v.reshape(1, -1)
    qg2 = jnp.tile(qg, 2).reshape(1, 128)
    kg2 = jnp.tile(kg, 2).reshape(1, 128)
    heads = jnp.arange(1, N_Q_HEADS + 1, dtype=F32)
    slopes = jnp.exp2(-8.0 * heads / N_Q_HEADS)

    u, xn = _inproj_call(x, row(g_mix) + token0, w_in_i)
    hc, c = _conv_fwd_call(u, conv_w, row(conv_b), row(ln_g), row(ln_b), B, S)
    o = _attn_fwd_call(u, qg2, kg2, sinks, slopes, B, S)
    wc, wa, wm, w_ffn_in, wd = rest_weights(o)
    yc, ya, z, h, hn = _merge_call(c, o, u, x, wc, wa, wm, row(g_ffn))
    gu, act = _ffn_in_call(hn, w_ffn_in)
    sse, dout, dout_b = _ffn_out_call(act, wd, h, tgt)

    dgu = _ffn_down_bwd_call(dout_b, wd, gu)
    g_wd = _dw_call("dw_ffn_down", act, dout_b, tn=512)
    g_wffn = _dw_call("dw_ffn_in", hn, dgu, tn=2 * D_FF // 4)
    tok = grads_ready(dict(w_ffn_down=g_wd, w_ffn_in=g_wffn))
    dh, dh_b, g_gffn = _ffn_in_bwd_call(dgu, w_ffn_in, h, row(g_ffn) + tok, dout)
    dyc, dya, du_g, dcp, do = _merge_bwd_call(dh_b, wm, wc, wa, u, yc, ya)
    g_wm = _dw_call("dw_merge", z, dh_b, tn=D_MODEL)
    g_wc = _dw_call("dw_conv_out", c, dyc, tn=D_MODEL)
    g_wa = _dw_call("dw_attn_out", o, dya, tn=D_MODEL)
    tok = grads_ready(dict(w_merge_out=g_wm, w_conv_out=g_wc, w_attn_out=g_wa))
    du_conv, g_cw8, g_cb, g_lg, g_lb = _conv_bwd_call(dcp, hc, u, conv_w, row(ln_g) + tok, row(ln_b), B, S)
    du_q, du_kv, g_qg2, g_kg2, g_sk = _attn_bwd_call(u, do, qg2, kg2, sinks, slopes, B, S)
    g_win = jnp.concatenate([
        _dw_call("dw_in_conv", xn, du_conv, tn=D_MODEL),
        _dw_call("dw_in_q", xn, du_q, tn=D_MODEL),
        _dw_call("dw_in_kv", xn, du_kv, tn=2 * KV_W),
        _dw_call("dw_in_gates", xn, du_g, tn=D_MODEL)], axis=1)
    tok = grads_ready(dict(w_in=g_win))
    grad_x, g_gmix = _inproj_bwd_call(du_conv, du_g, du_q, du_kv, w_in_i, x, row(g_mix) + tok, dh)

    g_cw = g_cw8.reshape(CONV_WIDTH, 8, C_CONV).sum(axis=1)
    g_qg = g_qg2[0, :HEAD_DIM] + g_qg2[0, HEAD_DIM:]
    g_kg = g_kg2[0, :HEAD_DIM] + g_kg2[0, HEAD_DIM:]
    small = dict(norm_mix_g=g_gmix[0], conv_dw_b=g_cb[0], conv_ln_g=g_lg[0], conv_ln_b=g_lb[0], q_norm_g=g_qg,
                 k_norm_g=g_kg, sinks=g_sk[0, :N_Q_HEADS], norm_ffn_g=g_gffn[0], conv_dw_w=g_cw)
    return sse[0, 0], grad_x, small


def _to_internal_cols(w):
    o1 = 2 * C_CONV
    o4 = o1 + Q_W + 2 * KV_W
    return jnp.concatenate([w[:, :o1], w[:, o4:], w[:, o1:o4]], axis=1)


_COL_SHARDED = ("w_in", "w_ffn_in")

_SMALL = (("norm_mix_g", 8), ("conv_dw_b", 8), ("conv_ln_g", 8), ("conv_ln_b", 8), ("norm_ffn_g", 8),
          ("q_norm_g", 1), ("k_norm_g", 1), ("sinks", 1))
_SMALL_ROWS = 48
_CW_ROWS = CONV_WIDTH * C_CONV // 128


def _pack_small(d):
    rows = [_pad_rows(d[name], r) for name, r in _SMALL]
    used = sum(r for _, r in _SMALL)
    return jnp.concatenate(rows + [jnp.zeros((_SMALL_ROWS - used, 128), F32)], axis=0)


def _unpack_small(packed, like):
    out, r0 = {}, 0
    for name, r in _SMALL:
        n = like[name].size
        out[name] = packed[r0:r0 + r].reshape(-1)[:n].reshape(like[name].shape)
        r0 += r
    return out


def kernel(x, norm_mix_g, w_in, conv_dw_w, conv_dw_b, conv_ln_g, conv_ln_b, w_conv_out, q_norm_g, k_norm_g, sinks, w_attn_out, w_merge_out, norm_ffn_g, w_ffn_in, w_ffn_down, loss_target, m_norm_mix_g, m_w_in, m_conv_dw_w, m_conv_dw_b, m_conv_ln_g, m_conv_ln_b, m_w_conv_out, m_q_norm_g, m_k_norm_g, m_sinks, m_w_attn_out, m_w_merge_out, m_norm_ffn_g, m_w_ffn_in, m_w_ffn_down, v_norm_mix_g, v_w_in, v_conv_dw_w, v_conv_dw_b, v_conv_ln_g, v_conv_ln_b, v_w_conv_out, v_q_norm_g, v_k_norm_g, v_sinks, v_w_attn_out, v_w_merge_out, v_norm_ffn_g, v_w_ffn_in, v_w_ffn_down):
    names = ["norm_mix_g", "w_in", "conv_dw_w", "conv_dw_b", "conv_ln_g", "conv_ln_b", "w_conv_out", "q_norm_g",
             "k_norm_g", "sinks", "w_attn_out", "w_merge_out", "norm_ffn_g", "w_ffn_in", "w_ffn_down"]
    w = dict(zip(names, (norm_mix_g, w_in, conv_dw_w, conv_dw_b, conv_ln_g, conv_ln_b, w_conv_out, q_norm_g, k_norm_g,
                         sinks, w_attn_out, w_merge_out, norm_ffn_g, w_ffn_in, w_ffn_down)))
    m = dict(zip(names, (m_norm_mix_g, m_w_in, m_conv_dw_w, m_conv_dw_b, m_conv_ln_g, m_conv_ln_b, m_w_conv_out,
                         m_q_norm_g, m_k_norm_g, m_sinks, m_w_attn_out, m_w_merge_out, m_norm_ffn_g, m_w_ffn_in,
                         m_w_ffn_down)))
    v = dict(zip(names, (v_norm_mix_g, v_w_in, v_conv_dw_w, v_conv_dw_b, v_conv_ln_g, v_conv_ln_b, v_w_conv_out,
                         v_q_norm_g, v_k_norm_g, v_sinks, v_w_attn_out, v_w_merge_out, v_norm_ffn_g, v_w_ffn_in,
                         v_w_ffn_down)))
    B, S, _ = x.shape
    T = B * S
    me = 4 * lax.axis_index("x") + 2 * lax.axis_index("y") + lax.axis_index("c")

    cw_shard = _pad_rows(conv_dw_w.reshape(CONV_WIDTH, 128), 32)
    g_win, g_cwt = _gather_weights_call([w_in.astype(BF16), cw_shard])
    conv_w_full = g_cwt.reshape(N_DEV, 32 * 128)[:, :CONV_WIDTH * 128].reshape(N_DEV, CONV_WIDTH, 128)
    conv_w_full = conv_w_full.transpose(1, 0, 2).reshape(CONV_WIDTH, C_CONV)
    w_in_i = _to_internal_cols(_cols_from_blocks(g_win))
    rest_names = ["w_conv_out", "w_attn_out", "w_merge_out", "w_ffn_in", "w_ffn_down"]
    rest_shards = [w[k].astype(BF16) for k in rest_names]
    rest_started = _exchange_start_call("gather_rest_start", rest_shards, scatter=False)

    def rest_weights(after):
        lands = _exchange_wait_call("gather_rest_wait", rest_started, False, after)
        mine = lax.broadcasted_iota(jnp.int32, (N_DEV, 1, 1), 0) == me
        full = [jnp.where(mine, s[None], l) for l, s in zip(lands, rest_shards)]
        wc_f, wa_f, wm_f = (f.reshape(D_MODEL, D_MODEL) for f in full[:3])
        return wc_f, wa_f, wm_f, _cols_from_blocks(full[3]), full[4].reshape(D_FF, D_MODEL)

    exchanges = []

    def grads_ready(group):
        ks = list(group)
        blocks = [_blocks_by_cols(group[k]) if k in _COL_SHARDED
                  else group[k].reshape(N_DEV, group[k].shape[0] // N_DEV, group[k].shape[1]) for k in ks]
        started = _exchange_start_call("scatter_" + ks[0] + "_start", blocks, scatter=True)
        exchanges.append((ks, started, [lax.dynamic_index_in_dim(b, me, 0, keepdims=False) for b in blocks]))
        return started[-1][0:1, 0:1]

    sse, grad_x, small = _local_step(
        x.reshape(T, D_MODEL), loss_target.reshape(T, D_MODEL), w_in_i, conv_w_full, conv_dw_b, conv_ln_g, conv_ln_b,
        q_norm_g, k_norm_g, sinks, norm_mix_g, norm_ffn_g, rest_started[-1][0:1, 0:1], rest_weights, grads_ready, B, S)

    grad, delta, new_m, new_v = {}, {}, {}, {}
    packed = jnp.concatenate([_pack_small(small), small["conv_dw_w"].reshape(_CW_ROWS, 128),
                              jnp.full((8, 128), sse, F32)], axis=0)
    tot = _allsum_small_call(packed)
    loss = 0.5 / D_MODEL * tot[_SMALL_ROWS + _CW_ROWS, 0]
    g_small = _unpack_small(tot[:_SMALL_ROWS], w)
    d_s, m_s, v_s = _adamw_small_call("adamw_small", tot[:_SMALL_ROWS], _pack_small(w), _pack_small(m), _pack_small(v))
    d_small, m_small, v_small = _unpack_small(d_s, w), _unpack_small(m_s, w), _unpack_small(v_s, w)
    g_cw = lax.dynamic_slice_in_dim(tot[_SMALL_ROWS:_SMALL_ROWS + _CW_ROWS].reshape(CONV_WIDTH, C_CONV), me * 128, 128, axis=1)
    cw2 = lambda t: t.reshape(CONV_WIDTH, 128)
    d_cw, m_cw, v_cw = _adamw_small_call("adamw_conv_w", g_cw, cw2(conv_dw_w), cw2(m_conv_dw_w), cw2(v_conv_dw_w))
    cw3 = lambda t: t.reshape(CONV_WIDTH, 1, 128)
    for k, _ in _SMALL:
        grad[k], delta[k], new_m[k], new_v[k] = g_small[k], d_small[k], m_small[k], v_small[k]
    grad["conv_dw_w"], delta["conv_dw_w"], new_m["conv_dw_w"], new_v["conv_dw_w"] = cw3(g_cw), cw3(d_cw), cw3(m_cw), cw3(v_cw)

    after = d_cw
    for ks, started, owns in exchanges:
        lands = _exchange_wait_call("scatter_" + ks[0] + "_wait", started, True, after)
        for k, land, own in zip(ks, lands, owns):
            tr = {1024: 256, 352: 176, 128: 128}[w[k].shape[0]]
            grad[k], delta[k], new_m[k], new_v[k] = _adamw_sum_call("adamw_" + k, land, own, w[k], m[k], v[k], tr)
            after = new_v[k]

    return (loss, grad_x.reshape(B, S, D_MODEL), *[grad[k] for k in names], *[delta[k] for k in names],
            *[new_m[k] for k in names], *[new_v[k] for k in names])
```

```python
import functools
import math

import jax
import jax.numpy as jnp
from jax import lax
from jax.experimental import pallas as pl
from jax.experimental.pallas import tpu as pltpu

F32 = jnp.float32
BF16 = jnp.bfloat16
MESH = pl.DeviceIdType.MESH

N_DEV = 8
D_MODEL = 1024
C_CONV = 1024
CONV_WIDTH = 31
HEAD_DIM = 64
N_Q_HEADS = 16
N_KV_HEADS = 2
GROUP = N_Q_HEADS // N_KV_HEADS
WINDOW = 128
BLOCK = 128
D_FF = 2816
EPS = 1e-6
NEG = -1e30
ATTN_SCALE = 1.0 / math.sqrt(HEAD_DIM)
Q_W = N_Q_HEADS * HEAD_DIM
KV_W = N_KV_HEADS * HEAD_DIM
IN_COLS = 2 * C_CONV + Q_W + 2 * KV_W + 2 * D_MODEL

ADAM_LR = 0.001
ADAM_B1 = 0.9
ADAM_B2 = 0.999
ADAM_EPS = 1e-08
ADAM_WD = 0.01
ADAM_STEP = 10

COL_A, COL_GATE, COL_GC, COL_GA, COL_Q, COL_K, COL_V = 0, 1024, 2048, 3072, 4096, 5120, 5248

HALO = 32
VMEM_LIMIT = 56 * 1024 * 1024


def _cp(*sem):
    return pltpu.CompilerParams(dimension_semantics=sem, vmem_limit_bytes=VMEM_LIMIT)


def _sigmoid(x):
    return jax.nn.sigmoid(x)


def _dot(a, b):
    return jnp.dot(a, b, preferred_element_type=F32)


def _dot_nt(a, b):
    return lax.dot_general(a, b, (((1,), (1,)), ((), ())), preferred_element_type=F32)


def _dot_tn(a, b):
    return lax.dot_general(a, b, (((0,), (0,)), ((), ())), preferred_element_type=F32)


def _full(shape):
    return pl.BlockSpec(shape, lambda *_: (0,) * len(shape))


def _rows(tm, ncols, colblk=0):
    return pl.BlockSpec((tm, ncols), lambda i: (i, colblk))


def _inproj_call(x, g, w, tm=512):
    T = x.shape[0]
    N = w.shape[1]

    def body(x_ref, g_ref, w_ref, u_ref, xn_ref):
        xv = x_ref[...]
        r = lax.rsqrt(jnp.mean(xv * xv, axis=-1, keepdims=True) + EPS)
        xn = (xv * r * g_ref[...]).astype(BF16)
        xn_ref[...] = xn
        u_ref[...] = _dot(xn, w_ref[...]).astype(BF16)

    return pl.pallas_call(
        body, name="inproj", grid=(T // tm,),
        in_specs=[_rows(tm, D_MODEL), _full((1, D_MODEL)), _full(w.shape)],
        out_specs=[_rows(tm, N), _rows(tm, D_MODEL)],
        out_shape=[jax.ShapeDtypeStruct((T, N), BF16), jax.ShapeDtypeStruct((T, D_MODEL), BF16)],
        compiler_params=_cp("parallel"),
    )(x, g, w)


def _fill_shifted(src_ref, sh_ref):
    n = src_ref.shape[0] - 8
    for s in range(1, 8):
        sh_ref[s - 1, 0:n, :] = src_ref[s:s + n, :]


def _window(src_ref, sh_ref, off, cols):
    s = off % 8
    if s == 0:
        return src_ref[off:off + 128, cols]
    return sh_ref[s - 1, off - s:off - s + 128, cols]


def _conv_taps(w_ref, src_ref, sh_ref, base, rows, cb, reverse):
    cols = pl.ds(pl.multiple_of(cb * 128, 128), 128)
    outs = []
    for rb in range(rows // 128):
        acc = jnp.zeros((128, 128), F32)
        for j in range(CONV_WIDTH):
            off = base + (CONV_WIDTH - 1 - j if reverse else j) + rb * 128
            acc = acc + w_ref[j:j + 1, cols] * _window(src_ref, sh_ref, off, cols)
        outs.append(acc)
    return outs


def _conv_fwd_call(u, conv_w, conv_b, ln_g, ln_b, B, S, ts=256):
    T = B * S
    nS = S // ts
    per32 = ts // HALO

    def body(a_ref, gate_ref, ap_ref, gp_ref, w_ref, b_ref, g_ref, bb_ref, hc_ref, c_ref, hext, hsh, acc_ref):
        s = pl.program_id(1)
        prev = ap_ref[...].astype(F32) * _sigmoid(gp_ref[...].astype(F32))
        hext[0:HALO, :] = jnp.where(s > 0, prev, 0.0)
        hext[HALO:, :] = a_ref[...].astype(F32) * _sigmoid(gate_ref[...].astype(F32))
        _fill_shifted(hext, hsh)

        def colblock(cb, carry):
            cols = pl.ds(pl.multiple_of(cb * 128, 128), 128)
            outs = _conv_taps(w_ref, hext, hsh, HALO - (CONV_WIDTH - 1), ts, cb, reverse=False)
            for rb, acc in enumerate(outs):
                acc_ref[rb * 128:(rb + 1) * 128, cols] = acc
            return carry

        lax.fori_loop(0, C_CONV // 128, colblock, 0)
        hc = acc_ref[...] + b_ref[...]
        hc_ref[...] = hc
        mu = jnp.mean(hc, axis=-1, keepdims=True)
        xc = hc - mu
        var = jnp.mean(xc * xc, axis=-1, keepdims=True)
        y = xc * lax.rsqrt(var + EPS) * g_ref[...] + bb_ref[...]
        c_ref[...] = (y * _sigmoid(y)).astype(BF16)

    cur = lambda cb: pl.BlockSpec((ts, C_CONV), lambda b, s: (b * nS + s, cb))
    prv = lambda cb: pl.BlockSpec((HALO, C_CONV), lambda b, s: (jnp.maximum(b * (S // HALO) + s * per32 - 1, 0), cb))
    vec = _full((1, C_CONV))
    return pl.pallas_call(
        body, name="conv_fwd", grid=(B, nS),
        in_specs=[cur(0), cur(1), prv(0), prv(1), _full((CONV_WIDTH, C_CONV)), vec, vec, vec],
        out_specs=[pl.BlockSpec((ts, C_CONV), lambda b, s: (b * nS + s, 0))] * 2,
        out_shape=[jax.ShapeDtypeStruct((T, C_CONV), F32), jax.ShapeDtypeStruct((T, C_CONV), BF16)],
        scratch_shapes=[pltpu.VMEM((HALO + ts, C_CONV), F32), pltpu.VMEM((7, HALO + ts, C_CONV), F32),
                        pltpu.VMEM((ts, C_CONV), F32)],
        compiler_params=_cp("parallel", "arbitrary"),
    )(u, u, u, u, conv_w, conv_b, ln_g, ln_b)


def _lane_lo():
    return lax.broadcasted_iota(jnp.int32, (1, 128), 1) < HEAD_DIM


def _half_sums(t, lo):
    s_lo = jnp.sum(jnp.where(lo, t, 0.0), axis=-1, keepdims=True)
    s_hi = jnp.sum(jnp.where(lo, 0.0, t), axis=-1, keepdims=True)
    return jnp.where(lo, s_lo, s_hi)


def _head_rstd(t, lo):
    return lax.rsqrt(_half_sums(t * t, lo) * (1.0 / HEAD_DIM) + EPS)


def _fill_bias(btab, kvh, slopes_ref):
    sj = lax.broadcasted_iota(jnp.int32, (2 * BLOCK, BLOCK), 0)
    qi = lax.broadcasted_iota(jnp.int32, (2 * BLOCK, BLOCK), 1)
    dist = qi + BLOCK - sj
    valid = (dist >= 0) & (dist < WINDOW)
    distf = dist.astype(F32)
    for g in range(GROUP):
        bias = jnp.where(valid, -slopes_ref[kvh * GROUP + g] * distf, NEG)
        btab[1, g] = bias
        btab[0, g] = jnp.where(sj >= BLOCK, bias, NEG)


def _softmax_cols(st_ref, btab, p, var, g, hf, sk):
    sh = st_ref[2 * BLOCK * hf:2 * BLOCK * (hf + 1), 128 * p:128 * (p + 1)] + btab[var, g]
    m = jnp.maximum(jnp.max(sh, axis=0, keepdims=True), sk)
    e = jnp.exp(sh - m)
    esk = jnp.exp(sk - m)
    rz = 1.0 / (jnp.sum(e, axis=0, keepdims=True) + esk)
    return e * rz, esk, rz


def _stage_kv(kvh, k_ref, v_ref, kg_ref, lo, kA, kB, vA, vB):
    kv = k_ref[...].astype(F32)
    rk = _head_rstd(kv, lo)
    kn = kv * rk * kg_ref[...]
    mine = lax.broadcasted_iota(jnp.int32, (1, 128), 1) // HEAD_DIM == kvh
    ksel = jnp.where(mine, kn, 0.0)
    vsel = jnp.where(mine, v_ref[...].astype(F32), 0.0)
    krol = pltpu.roll(ksel, HEAD_DIM, 1)
    vrol = pltpu.roll(vsel, HEAD_DIM, 1)
    first = kvh == 0
    zeros = jnp.zeros((BLOCK, 128), BF16)
    for ref, val in ((kA, jnp.where(first, ksel, krol)), (kB, jnp.where(first, krol, ksel)),
                     (vA, jnp.where(first, vsel, vrol)), (vB, jnp.where(first, vrol, vsel))):
        ref[0:BLOCK, :] = zeros
        ref[BLOCK:, :] = val.astype(BF16)


def _attn_fwd_call(u, qg2, kg2, sinks, slopes, B, S):
    T = B * S
    nb = S // BLOCK
    half = Q_W // 2
    npair = GROUP // 2

    def body(sinks_ref, slopes_ref, q_ref, k_ref, v_ref, qg_ref, kg_ref, o_ref, kA, kB, vA, vB, btab, qn_buf, s_buf, p_buf):
        kvh = pl.program_id(1)
        lo = _lane_lo()
        _stage_kv(kvh, k_ref, v_ref, kg_ref, lo, kA, kB, vA, vB)
        _fill_bias(btab, kvh, slopes_ref)
        qgs = qg_ref[...] * ATTN_SCALE

        def block(n, slot):
            qn_scr, s_scr, p_scr = qn_buf.at[slot], s_buf.at[slot], p_buf.at[slot]
            r0 = pl.multiple_of(n * BLOCK, BLOCK)
            win = pl.ds(r0, 2 * BLOCK)
            k2t = jnp.concatenate([kA[win, :], kB[win, :]], axis=0)
            v2 = jnp.concatenate([vA[win, :], vB[win, :]], axis=0)
            var = jnp.minimum(n, 1)
            for p in range(npair):
                pc = slice(128 * p, 128 * (p + 1))
                qp = q_ref[pl.ds(r0, BLOCK), pc].astype(F32)
                qn_scr[pc, :] = (qp * _head_rstd(qp, lo) * qgs).astype(BF16)
            s_scr[...] = _dot_nt(k2t, qn_scr[...])
            for p in range(npair):
                for hf in range(2):
                    g = 2 * p + hf
                    pn, _, _ = _softmax_cols(s_scr, btab, p, var, g, hf, sinks_ref[kvh * GROUP + g])
                    p_scr[2 * BLOCK * hf:2 * BLOCK * (hf + 1), 128 * p:128 * (p + 1)] = pn.astype(BF16)
            o_all = _dot_tn(v2, p_scr[...]).T
            for p in range(npair):
                pc = slice(128 * p, 128 * (p + 1))
                o_ref[pl.ds(r0, BLOCK), pc] = o_all[pc, :].astype(BF16)

        def two_blocks(i, carry):
            block(2 * i, 0)
            block(2 * i + 1, 1)
            return carry

        lax.fori_loop(0, nb // 2, two_blocks, 0)

    smem = pl.BlockSpec(memory_space=pltpu.SMEM)
    kvbuf = pltpu.VMEM((S + BLOCK, 128), BF16)
    pair_scores = (2, npair * BLOCK, 4 * BLOCK)
    return pl.pallas_call(
        body, name="attn_fwd", grid=(B, N_KV_HEADS),
        in_specs=[smem, smem,
                  pl.BlockSpec((S, half), lambda b, h: (b, COL_Q // half + h)),
                  pl.BlockSpec((S, 128), lambda b, h: (b, COL_K // 128)),
                  pl.BlockSpec((S, 128), lambda b, h: (b, COL_V // 128)),
                  _full((1, 128)), _full((1, 128))],
        out_specs=pl.BlockSpec((S, half), lambda b, h: (b, h)),
        out_shape=jax.ShapeDtypeStruct((T, Q_W), BF16),
        scratch_shapes=[kvbuf, kvbuf, kvbuf, kvbuf, pltpu.VMEM((2, GROUP, 2 * BLOCK, BLOCK), F32),
                        pltpu.VMEM((2, npair * BLOCK, 128), BF16), pltpu.VMEM(pair_scores, F32), pltpu.VMEM(pair_scores, BF16)],
        compiler_params=_cp("parallel", "arbitrary"),
    )(sinks, slopes, u, u, u, qg2, kg2)


def _merge_call(c, o, u, x, wc, wa, wm, g_ffn, tm=512):
    T = x.shape[0]

    def body(c_ref, o_ref, gc_ref, ga_ref, x_ref, wc_ref, wa_ref, wm_ref, g_ref, yc_ref, ya_ref, z_ref, h_ref, hn_ref):
        yc = _dot(c_ref[...], wc_ref[...])
        ya = _dot(o_ref[...], wa_ref[...])
        yc_ref[...] = yc.astype(BF16)
        ya_ref[...] = ya.astype(BF16)
        z = (_sigmoid(gc_ref[...].astype(F32)) * yc + _sigmoid(ga_ref[...].astype(F32)) * ya).astype(BF16)
        z_ref[...] = z
        h = x_ref[...] + _dot(z, wm_ref[...])
        h_ref[...] = h
        r = lax.rsqrt(jnp.mean(h * h, axis=-1, keepdims=True) + EPS)
        hn_ref[...] = (h * r * g_ref[...]).astype(BF16)

    sq = _full((D_MODEL, D_MODEL))
    row = _rows(tm, D_MODEL)
    f32 = jax.ShapeDtypeStruct((T, D_MODEL), F32)
    b16 = jax.ShapeDtypeStruct((T, D_MODEL), BF16)
    return pl.pallas_call(
        body, name="merge_fwd", grid=(T // tm,),
        in_specs=[row, row, _rows(tm, D_MODEL, COL_GC // D_MODEL), _rows(tm, D_MODEL, COL_GA // D_MODEL), row,
                  sq, sq, sq, _full((1, D_MODEL))],
        out_specs=[row] * 5,
        out_shape=[b16, b16, b16, f32, b16],
        compiler_params=_cp("parallel"),
    )(c, o, u, u, x, wc, wa, wm, g_ffn)


def _ffn_in_call(hn, w, tm=512):
    T = hn.shape[0]

    def body(hn_ref, w_ref, gu_ref, act_ref):
        gu = _dot(hn_ref[...], w_ref[...])
        gu_ref[...] = gu.astype(BF16)
        g = gu[:, :D_FF]
        act_ref[...] = (g * _sigmoid(g) * gu[:, D_FF:]).astype(BF16)

    return pl.pallas_call(
        body, name="ffn_in", grid=(T // tm,),
        in_specs=[_rows(tm, D_MODEL), _full(w.shape)],
        out_specs=[_rows(tm, 2 * D_FF), _rows(tm, D_FF)],
        out_shape=[jax.ShapeDtypeStruct((T, 2 * D_FF), BF16), jax.ShapeDtypeStruct((T, D_FF), BF16)],
        compiler_params=_cp("parallel"),
    )(hn, w)


def _ffn_out_call(act, wd, h, tgt, tm=512):
    T = h.shape[0]

    def body(act_ref, wd_ref, h_ref, t_ref, sse_ref, d_ref, db_ref):
        e = h_ref[...] + _dot(act_ref[...], wd_ref[...]) - t_ref[...]

        @pl.when(pl.program_id(0) == 0)
        def _():
            sse_ref[...] = jnp.zeros_like(sse_ref)

        sse_ref[...] += jnp.sum(e * e)
        d = e * (1.0 / D_MODEL)
        d_ref[...] = d
        db_ref[...] = d.astype(BF16)

    row = _rows(tm, D_MODEL)
    return pl.pallas_call(
        body, name="ffn_out", grid=(T // tm,),
        in_specs=[_rows(tm, D_FF), _full(wd.shape), row, row],
        out_specs=[_full((8, 128)), row, row],
        out_shape=[jax.ShapeDtypeStruct((8, 128), F32), jax.ShapeDtypeStruct((T, D_MODEL), F32),
                   jax.ShapeDtypeStruct((T, D_MODEL), BF16)],
        compiler_params=_cp("arbitrary"),
    )(act, wd, h, tgt)


def _dw_call(name, a, b, tn, tk=2048):
    T, K = a.shape
    N = b.shape[1]
    tk = min(tk, T)
    nk = T // tk

    def body(a_ref, b_ref, o_ref, acc_ref):
        k = pl.program_id(1)

        @pl.when(k == 0)
        def _():
            acc_ref[...] = jnp.zeros_like(acc_ref)

        acc_ref[...] += _dot_tn(a_ref[...], b_ref[...])

        @pl.when(k == nk - 1)
        def _():
            o_ref[...] = acc_ref[...].astype(BF16)

    return pl.pallas_call(
        body, name=name, grid=(N // tn, nk),
        in_specs=[pl.BlockSpec((tk, K), lambda j, k: (k, 0)), pl.BlockSpec((tk, tn), lambda j, k: (k, j))],
        out_specs=pl.BlockSpec((K, tn), lambda j, k: (0, j)),
        out_shape=jax.ShapeDtypeStruct((K, N), BF16),
        scratch_shapes=[pltpu.VMEM((K, tn), F32)],
        compiler_params=_cp("parallel", "arbitrary"),
    )(a, b)


def _rms_bwd(dy, xv, g):
    r = lax.rsqrt(jnp.mean(xv * xv, axis=-1, keepdims=True) + EPS)
    t = dy * g
    dx = r * t - xv * (r * r * r * jnp.mean(t * xv, axis=-1, keepdims=True))
    return dx, dy * xv * r


def _ffn_down_bwd_call(dout_b, wd, gu, tm=512):
    T = dout_b.shape[0]

    def body(d_ref, wd_ref, g_ref, up_ref, o_ref):
        dact = _dot_nt(d_ref[...], wd_ref[...])
        g = g_ref[...].astype(F32)
        sg = _sigmoid(g)
        o_ref[:, :D_FF] = (dact * up_ref[...].astype(F32) * (sg * (1.0 + g * (1.0 - sg)))).astype(BF16)
        o_ref[:, D_FF:] = (dact * (g * sg)).astype(BF16)

    return pl.pallas_call(
        body, name="ffn_down_bwd", grid=(T // tm,),
        in_specs=[_rows(tm, D_MODEL), _full(wd.shape), _rows(tm, D_FF, 0), _rows(tm, D_FF, 1)],
        out_specs=_rows(tm, 2 * D_FF),
        out_shape=jax.ShapeDtypeStruct((T, 2 * D_FF), BF16),
        compiler_params=_cp("parallel"),
    )(dout_b, wd, gu, gu)


def _ffn_in_bwd_call(dgu, w, h, g_ffn, dout, tm=512):
    T = h.shape[0]

    def body(dgu_ref, w_ref, h_ref, g_ref, d_ref, dh_ref, dhb_ref, dg_ref):
        dhn = _dot_nt(dgu_ref[...], w_ref[...])
        dx, dgt = _rms_bwd(dhn, h_ref[...], g_ref[...])
        dh = d_ref[...] + dx
        dh_ref[...] = dh
        dhb_ref[...] = dh.astype(BF16)

        @pl.when(pl.program_id(0) == 0)
        def _():
            dg_ref[...] = jnp.zeros_like(dg_ref)

        dg_ref[...] += jnp.sum(dgt, axis=0, keepdims=True)

    row = _rows(tm, D_MODEL)
    return pl.pallas_call(
        body, name="ffn_in_bwd", grid=(T // tm,),
        in_specs=[_rows(tm, 2 * D_FF), _full(w.shape), row, _full((1, D_MODEL)), row],
        out_specs=[row, row, _full((1, D_MODEL))],
        out_shape=[jax.ShapeDtypeStruct((T, D_MODEL), F32), jax.ShapeDtypeStruct((T, D_MODEL), BF16),
                   jax.ShapeDtypeStruct((1, D_MODEL), F32)],
        compiler_params=_cp("arbitrary"),
    )(dgu, w, h, g_ffn, dout)


def _merge_bwd_call(dh_b, wm, wc, wa, u, yc, ya, tm=512):
    T = dh_b.shape[0]

    def body(dh_ref, wm_ref, wc_ref, wa_ref, gc_ref, ga_ref, yc_ref, ya_ref, dyc_ref, dya_ref, dug_ref, dcp_ref, do_ref):
        dz = _dot_nt(dh_ref[...], wm_ref[...])
        sgc = _sigmoid(gc_ref[...].astype(F32))
        sga = _sigmoid(ga_ref[...].astype(F32))
        dyc = (dz * sgc).astype(BF16)
        dya = (dz * sga).astype(BF16)
        dyc_ref[...] = dyc
        dya_ref[...] = dya
        dug_ref[:, :D_MODEL] = (dz * yc_ref[...].astype(F32) * (sgc * (1.0 - sgc))).astype(BF16)
        dug_ref[:, D_MODEL:] = (dz * ya_ref[...].astype(F32) * (sga * (1.0 - sga))).astype(BF16)
        dcp_ref[...] = _dot_nt(dyc, wc_ref[...])
        do_ref[...] = _dot_nt(dya, wa_ref[...]).astype(BF16)

    sq = _full((D_MODEL, D_MODEL))
    row = _rows(tm, D_MODEL)
    b16 = jax.ShapeDtypeStruct((T, D_MODEL), BF16)
    return pl.pallas_call(
        body, name="merge_bwd", grid=(T // tm,),
        in_specs=[row, sq, sq, sq, _rows(tm, D_MODEL, COL_GC // D_MODEL), _rows(tm, D_MODEL, COL_GA // D_MODEL), row, row],
        out_specs=[row, row, _rows(tm, 2 * D_MODEL), row, row],
        out_shape=[b16, b16, jax.ShapeDtypeStruct((T, 2 * D_MODEL), BF16), jax.ShapeDtypeStruct((T, D_MODEL), F32), b16],
        compiler_params=_cp("parallel"),
    )(dh_b, wm, wc, wa, u, u, yc, ya)


def _conv_bwd_call(dcp, hc, u, conv_w, ln_g, ln_b, B, S, ts=256):
    T = B * S
    nS = S // ts
    per32 = ts // HALO
    last32 = T // HALO - 1

    def ln_bwd(dcp_v, hc_v, g, bb):
        mu = jnp.mean(hc_v, axis=-1, keepdims=True)
        xc = hc_v - mu
        rstd = lax.rsqrt(jnp.mean(xc * xc, axis=-1, keepdims=True) + EPS)
        xhat = xc * rstd
        y = xhat * g + bb
        sy = _sigmoid(y)
        dy = dcp_v * (sy * (1.0 + y * (1.0 - sy)))
        dxh = dy * g
        dhc = rstd * (dxh - jnp.mean(dxh, axis=-1, keepdims=True) - xhat * jnp.mean(dxh * xhat, axis=-1, keepdims=True))
        return dhc, dy * xhat, dy

    def body(dcp_ref, dcpn_ref, hc_ref, hcn_ref, a_ref, gate_ref, ap_ref, gp_ref, w_ref, g_ref, bb_ref,
             du_ref, dw_ref, db_ref, dlg_ref, dlb_ref, dhext, hgext, dhsh, hgsh, acc_ref):
        b = pl.program_id(0)
        s = pl.program_id(1)

        @pl.when((b == 0) & (s == 0))
        def _():
            dw_ref[...] = jnp.zeros_like(dw_ref)
            db_ref[...] = jnp.zeros_like(db_ref)
            dlg_ref[...] = jnp.zeros_like(dlg_ref)
            dlb_ref[...] = jnp.zeros_like(dlb_ref)

        dhc, dgt, dbt = ln_bwd(dcp_ref[...], hc_ref[...], g_ref[...], bb_ref[...])
        dhn, _, _ = ln_bwd(dcpn_ref[...], hcn_ref[...], g_ref[...], bb_ref[...])
        dhext[0:ts, :] = dhc
        dhext[ts:, :] = jnp.where(s < nS - 1, dhn, 0.0)
        db_ref[...] += jnp.sum(dhc, axis=0, keepdims=True)
        dlg_ref[...] += jnp.sum(dgt, axis=0, keepdims=True)
        dlb_ref[...] += jnp.sum(dbt, axis=0, keepdims=True)

        sg = _sigmoid(gate_ref[...].astype(F32))
        av = a_ref[...].astype(F32)
        hgext[0:HALO, :] = jnp.where(s > 0, ap_ref[...].astype(F32) * _sigmoid(gp_ref[...].astype(F32)), 0.0)
        hgext[HALO:, :] = av * sg
        _fill_shifted(dhext, dhsh)
        _fill_shifted(hgext, hgsh)

        def colblock(cb, carry):
            cols = pl.ds(pl.multiple_of(cb * 128, 128), 128)
            outs = _conv_taps(w_ref, dhext, dhsh, 0, ts, cb, reverse=True)
            for rb, acc in enumerate(outs):
                acc_ref[rb * 128:(rb + 1) * 128, cols] = acc
            dhc_rows = [dhext[rb * 128:(rb + 1) * 128, cols] for rb in range(ts // 128)]
            for j in range(CONV_WIDTH):
                part = jnp.zeros((8, 128), F32)
                for rb in range(ts // 128):
                    glu = _window(hgext, hgsh, HALO - (CONV_WIDTH - 1) + j + rb * 128, cols)
                    part = part + jnp.sum((dhc_rows[rb] * glu).reshape(16, 8, 128), axis=0)
                dw_ref[8 * j:8 * (j + 1), cols] += part
            return carry

        lax.fori_loop(0, C_CONV // 128, colblock, 0)
        dglu = acc_ref[...]
        du_ref[:, :C_CONV] = (dglu * sg).astype(BF16)
        du_ref[:, C_CONV:] = (dglu * av * (sg * (1.0 - sg))).astype(BF16)

    cur = lambda cb: pl.BlockSpec((ts, C_CONV), lambda b, s: (b * nS + s, cb))
    prv = lambda cb: pl.BlockSpec((HALO, C_CONV), lambda b, s: (jnp.maximum(b * (S // HALO) + s * per32 - 1, 0), cb))
    nxt = pl.BlockSpec((HALO, C_CONV), lambda b, s: (jnp.minimum(b * (S // HALO) + (s + 1) * per32, last32), 0))
    vec = _full((1, C_CONV))
    return pl.pallas_call(
        body, name="conv_bwd", grid=(B, nS),
        in_specs=[cur(0), nxt, cur(0), nxt, cur(0), cur(1), prv(0), prv(1), _full((CONV_WIDTH, C_CONV)), vec, vec],
        out_specs=[pl.BlockSpec((ts, 2 * C_CONV), lambda b, s: (b * nS + s, 0)), _full((8 * CONV_WIDTH, C_CONV)), vec, vec, vec],
        out_shape=[jax.ShapeDtypeStruct((T, 2 * C_CONV), BF16), jax.ShapeDtypeStruct((8 * CONV_WIDTH, C_CONV), F32),
                   jax.ShapeDtypeStruct((1, C_CONV), F32), jax.ShapeDtypeStruct((1, C_CONV), F32),
                   jax.ShapeDtypeStruct((1, C_CONV), F32)],
        scratch_shapes=[pltpu.VMEM((ts + HALO, C_CONV), F32), pltpu.VMEM((HALO + ts, C_CONV), F32),
                        pltpu.VMEM((7, ts + HALO, C_CONV), F32), pltpu.VMEM((7, HALO + ts, C_CONV), F32),
                        pltpu.VMEM((ts, C_CONV), F32)],
        compiler_params=_cp("arbitrary", "arbitrary"),
    )(dcp, dcp, hc, hc, u, u, u, u, conv_w, ln_g, ln_b)


def _attn_bwd_call(u, do, qg2, kg2, sinks, slopes, B, S):
    T = B * S
    nb = S // BLOCK
    half = Q_W // 2
    npair = GROUP // 2

    def body(sinks_ref, slopes_ref, q_ref, k_ref, v_ref, do_ref, qg_ref, kg_ref,
             dq_ref, dkv_ref, dqg_ref, dkg_ref, dsk_ref, kA, kB, vA, vB, dkn_acc, dv_acc, dkv_keep,
             btab, s_buf, dp_buf, p_buf, ds_buf, qn_buf, do_buf, rq_buf):
        b = pl.program_id(0)
        kvh = pl.program_id(1)
        lo = _lane_lo()
        lane = lax.broadcasted_iota(jnp.int32, (1, 128), 1)
        first = kvh == 0

        @pl.when((b == 0) & first)
        def _():
            dqg_ref[...] = jnp.zeros_like(dqg_ref)
            dkg_ref[...] = jnp.zeros_like(dkg_ref)
            dsk_ref[...] = jnp.zeros_like(dsk_ref)

        _stage_kv(kvh, k_ref, v_ref, kg_ref, lo, kA, kB, vA, vB)
        dkn_acc[...] = jnp.zeros_like(dkn_acc)
        dv_acc[...] = jnp.zeros_like(dv_acc)
        _fill_bias(btab, kvh, slopes_ref)
        qgs = qg_ref[...] * ATTN_SCALE

        def fold(acc):
            canon = jnp.where(lo, acc[0:2 * BLOCK] + pltpu.roll(acc[2 * BLOCK:], HEAD_DIM, 1), 0.0)
            return jnp.where(first, canon, pltpu.roll(canon, HEAD_DIM, 1))

        def block(n, slot):
            s_scr, dp_scr, p_scr, ds_scr = s_buf.at[slot], dp_buf.at[slot], p_buf.at[slot], ds_buf.at[slot]
            qn_scr, do_scr, rq_scr = qn_buf.at[slot], do_buf.at[slot], rq_buf.at[slot]
            r0 = pl.multiple_of(n * BLOCK, BLOCK)
            win = pl.ds(r0, 2 * BLOCK)
            rows = pl.ds(r0, BLOCK)
            k2t = jnp.concatenate([kA[win, :], kB[win, :]], axis=0)
            v2 = jnp.concatenate([vA[win, :], vB[win, :]], axis=0)
            var = jnp.minimum(n, 1)
            for p in range(npair):
                pc = slice(128 * p, 128 * (p + 1))
                qp = q_ref[rows, pc].astype(F32)
                rq = _head_rstd(qp, lo)
                rq_scr[pc, :] = rq
                qn_scr[pc, :] = (qp * rq * qgs).astype(BF16)
                do_scr[pc, :] = do_ref[rows, pc]
            s_scr[...] = _dot_nt(k2t, qn_scr[...])
            dp_scr[...] = _dot_nt(v2, do_scr[...])
            for p in range(npair):
                for hf in range(2):
                    g = 2 * p + hf
                    keys, qs = slice(2 * BLOCK * hf, 2 * BLOCK * (hf + 1)), slice(128 * p, 128 * (p + 1))
                    pn, esk, rz = _softmax_cols(s_scr, btab, p, var, g, hf, sinks_ref[kvh * GROUP + g])
                    dph = dp_scr[keys, qs]
                    delta = jnp.sum(pn * dph, axis=0, keepdims=True)
                    ds_scr[keys, qs] = (pn * (dph - delta)).astype(BF16)
                    p_scr[keys, qs] = pn.astype(BF16)
                    dsk_ref[...] += jnp.where(lane == kvh * GROUP + g, -jnp.sum(esk * rz * delta), 0.0)
            dqs_all = _dot_tn(k2t, ds_scr[...]).T
            for p in range(npair):
                pc = slice(128 * p, 128 * (p + 1))
                qp = q_ref[rows, pc].astype(F32)
                rq = rq_scr[pc, :]
                dqs = dqs_all[pc, :]
                t = dqs * qgs
                dq = rq * t - qp * (rq * rq * rq * (_half_sums(t * qp, lo) * (1.0 / HEAD_DIM)))
                dq_ref[rows, pc] = dq.astype(BF16)
                dqg_ref[...] += jnp.sum(dqs * qp * rq, axis=0, keepdims=True) * ATTN_SCALE
            dkn_acc[win, :] += fold(_dot(ds_scr[...], qn_scr[...]))
            dv_acc[win, :] += fold(_dot(p_scr[...], do_scr[...]))

        def two_blocks(i, carry):
            block(2 * i, 0)
            block(2 * i + 1, 1)
            return carry

        lax.fori_loop(0, nb // 2, two_blocks, 0)

        dkn = dkn_acc[BLOCK:, :]
        kv = k_ref[...].astype(F32)
        kg = kg_ref[...]
        rk = _head_rstd(kv, lo)
        t = dkn * kg
        dk = rk * t - kv * (rk * rk * rk * (_half_sums(t * kv, lo) * (1.0 / HEAD_DIM)))
        dkg_ref[...] += jnp.sum(dkn * kv * rk, axis=0, keepdims=True)
        dv = dv_acc[BLOCK:, :]

        @pl.when(first)
        def _():
            dkv_keep[:, :128] = dk
            dkv_keep[:, 128:] = dv

        @pl.when(kvh == N_KV_HEADS - 1)
        def _():
            dkv_ref[:, :128] = (dkv_keep[:, :128] + dk).astype(BF16)
            dkv_ref[:, 128:] = (dkv_keep[:, 128:] + dv).astype(BF16)

    smem = pl.BlockSpec(memory_space=pltpu.SMEM)
    kvbuf = pltpu.VMEM((S + BLOCK, 128), BF16)
    accbuf = pltpu.VMEM((S + BLOCK, 128), F32)
    vec = _full((1, 128))
    return pl.pallas_call(
        body, name="attn_bwd", grid=(B, N_KV_HEADS),
        in_specs=[smem, smem,
                  pl.BlockSpec((S, half), lambda b, h: (b, COL_Q // half + h)),
                  pl.BlockSpec((S, 128), lambda b, h: (b, COL_K // 128)),
                  pl.BlockSpec((S, 128), lambda b, h: (b, COL_V // 128)),
                  pl.BlockSpec((S, half), lambda b, h: (b, h)),
                  vec, vec],
        out_specs=[pl.BlockSpec((S, half), lambda b, h: (b, h)), pl.BlockSpec((S, 256), lambda b, h: (b, 0)), vec, vec, vec],
        out_shape=[jax.ShapeDtypeStruct((T, Q_W), BF16), jax.ShapeDtypeStruct((T, 2 * KV_W), BF16),
                   jax.ShapeDtypeStruct((1, 128), F32), jax.ShapeDtypeStruct((1, 128), F32), jax.ShapeDtypeStruct((1, 128), F32)],
        scratch_shapes=[kvbuf, kvbuf, kvbuf, kvbuf, accbuf, accbuf, pltpu.VMEM((S, 256), F32),
                        pltpu.VMEM((2, GROUP, 2 * BLOCK, BLOCK), F32),
                        pltpu.VMEM((2, npair * BLOCK, 4 * BLOCK), F32), pltpu.VMEM((2, npair * BLOCK, 4 * BLOCK), F32),
                        pltpu.VMEM((2, npair * BLOCK, 4 * BLOCK), BF16), pltpu.VMEM((2, npair * BLOCK, 4 * BLOCK), BF16),
                        pltpu.VMEM((2, npair * BLOCK, 128), BF16), pltpu.VMEM((2, npair * BLOCK, 128), BF16),
                        pltpu.VMEM((2, npair * BLOCK, 128), F32)],
        compiler_params=_cp("arbitrary", "arbitrary"),
    )(sinks, slopes, u, u, u, do, qg2, kg2)


def _inproj_bwd_call(du_conv, du_g, du_q, du_kv, w, x, g, dh, tm=512):
    T = x.shape[0]

    def body(dc_ref, dg_ref, dq_ref, dkv_ref, w_ref, x_ref, g_ref, dh_ref, gx_ref, dgm_ref):
        dxn = _dot_nt(dc_ref[...], w_ref[:, COL_A:COL_GC])
        dxn += _dot_nt(dg_ref[...], w_ref[:, COL_GC:COL_Q])
        dxn += _dot_nt(dq_ref[...], w_ref[:, COL_Q:COL_K])
        dxn += _dot_nt(dkv_ref[...], w_ref[:, COL_K:])
        dx, dgt = _rms_bwd(dxn, x_ref[...], g_ref[...])
        gx_ref[...] = dh_ref[...] + dx

        @pl.when(pl.program_id(0) == 0)
        def _():
            dgm_ref[...] = jnp.zeros_like(dgm_ref)

        dgm_ref[...] += jnp.sum(dgt, axis=0, keepdims=True)

    row = _rows(tm, D_MODEL)
    return pl.pallas_call(
        body, name="inproj_bwd", grid=(T // tm,),
        in_specs=[_rows(tm, 2 * C_CONV), _rows(tm, 2 * D_MODEL), _rows(tm, Q_W), _rows(tm, 2 * KV_W), _full(w.shape),
                  row, _full((1, D_MODEL)), row],
        out_specs=[row, _full((1, D_MODEL))],
        out_shape=[jax.ShapeDtypeStruct((T, D_MODEL), F32), jax.ShapeDtypeStruct((1, D_MODEL), F32)],
        compiler_params=_cp("arbitrary"),
    )(du_conv, du_g, du_q, du_kv, w, x, g, dh)


def _adamw_math(w, g, m, v):
    m = ADAM_B1 * m + (1.0 - ADAM_B1) * g
    v = ADAM_B2 * v + (1.0 - ADAM_B2) * (g * g)
    m_hat = m / (1.0 - ADAM_B1 ** ADAM_STEP)
    v_hat = v / (1.0 - ADAM_B2 ** ADAM_STEP)
    delta = -ADAM_LR * (m_hat / (jnp.sqrt(v_hat) + ADAM_EPS) + ADAM_WD * w)
    return delta, m, v


def _adamw_sum_call(name, parts, own, w, m, v, tr):
    R, C = w.shape

    def body(p_ref, own_ref, w_ref, m_ref, v_ref, g_ref, d_ref, nm_ref, nv_ref):
        x, y, c = _position()
        me = 4 * x + 2 * y + c
        g = jnp.zeros((tr, C), F32)
        for s in range(N_DEV):
            g = g + jnp.where(me == s, own_ref[...], p_ref[s]).astype(F32)
        g_ref[...] = g
        d_ref[...], nm_ref[...], nv_ref[...] = _adamw_math(w_ref[...], g, m_ref[...], v_ref[...])

    blk = pl.BlockSpec((tr, C), lambda i: (i, 0))
    out = jax.ShapeDtypeStruct((R, C), F32)
    return pl.pallas_call(
        body, name=name, grid=(R // tr,),
        in_specs=[pl.BlockSpec((N_DEV, tr, C), lambda i: (0, i, 0)), blk, blk, blk, blk],
        out_specs=[blk] * 4, out_shape=[out] * 4,
        compiler_params=_cp("parallel"),
    )(parts, own, w, m, v)


def _adamw_small_call(name, g, w, m, v):
    def body(g_ref, w_ref, m_ref, v_ref, d_ref, nm_ref, nv_ref):
        d_ref[...], nm_ref[...], nv_ref[...] = _adamw_math(w_ref[...], g_ref[...], m_ref[...], v_ref[...])

    out = jax.ShapeDtypeStruct(w.shape, F32)
    return pl.pallas_call(body, name=name, out_shape=[out] * 3)(g, w, m, v)


def _position():
    return lax.axis_index("x"), lax.axis_index("y"), lax.axis_index("c")


def _peer(x, y, c, rel):
    fx, fy, fc = (rel >> 2) & 1, (rel >> 1) & 1, rel & 1
    px = 1 - x if fx else x
    py = 1 - y if fy else y
    pc = 1 - c if fc else c
    return (px, py, pc), 4 * px + 2 * py + pc


def _gather_weights_call(shards):
    n = len(shards)

    def body(*refs):
        ins, outs = refs[:n], refs[n:2 * n]
        send_sems, recv_sems, local_sems = refs[2 * n:]
        x, y, c = _position()
        me = 4 * x + 2 * y + c
        sib, sib_id = _peer(x, y, c, 1)
        chips = [_peer(x, y, c, rel) for rel in (4, 2, 6)]

        def copy(i, k, src, slot, to):
            return pltpu.make_async_remote_copy(
                src_ref=src, dst_ref=outs[i].at[slot], send_sem=send_sems.at[i, k], recv_sem=recv_sems.at[i, k],
                device_id=to, device_id_type=MESH)

        mine = [pltpu.make_async_copy(ins[i], outs[i].at[me], local_sems.at[i]) for i in range(n)]
        for cp in mine:
            cp.start()
        first = []
        for i in range(n):
            first.append(copy(i, 0, ins[i], me, sib))
            first += [copy(i, 1 + j, ins[i], me, dev) for j, (dev, _) in enumerate(chips)]
        for cp in first:
            cp.start()
        passed = []
        for i in range(n):
            for j, (dev, slot) in enumerate(chips):
                copy(i, 1 + j, ins[i], slot, dev).wait_recv()
                fwd = copy(i, 4 + j, outs[i].at[slot], slot, sib)
                fwd.start()
                passed.append(fwd)
        for i in range(n):
            copy(i, 0, ins[i], sib_id, sib).wait_recv()
            for j, (_, slot) in enumerate(chips):
                copy(i, 4 + j, ins[i], slot ^ 1, sib).wait_recv()
        for cp in first + passed:
            cp.wait_send()
        for cp in mine:
            cp.wait()

    hbm = pl.BlockSpec(memory_space=pl.ANY)
    return pl.pallas_call(
        body, name="gather_weights",
        in_specs=[hbm] * n, out_specs=[hbm] * n,
        out_shape=[jax.ShapeDtypeStruct((N_DEV,) + s.shape, s.dtype) for s in shards],
        scratch_shapes=[pltpu.SemaphoreType.DMA((n, 7)), pltpu.SemaphoreType.DMA((n, 7)), pltpu.SemaphoreType.DMA((n,))],
    )(*shards)


_HBM = pl.BlockSpec(memory_space=pltpu.HBM)
_SEM = pl.BlockSpec(memory_space=pltpu.SEMAPHORE)
_EFFECT = pltpu.SideEffectType.DATAFLOW_SIDE_EFFECTING


def _exchange_copies(scatter, incoming, src_refs, land_refs, send_sems, recv_sems):
    x, y, c = _position()
    me = 4 * x + 2 * y + c
    out = []
    for i, (src, land) in enumerate(zip(src_refs, land_refs)):
        for k in range(N_DEV - 1):
            dev, dev_id = _peer(x, y, c, k + 1)
            j = i * (N_DEV - 1) + k
            out.append(pltpu.make_async_remote_copy(
                src_ref=src.at[dev_id] if scatter else src, dst_ref=land.at[dev_id if incoming else me],
                send_sem=send_sems.at[j], recv_sem=recv_sems.at[j], device_id=dev, device_id_type=MESH))
    return out


def _exchange_start_call(name, arrays, scatter):
    n = len(arrays)
    lands = [lax.empty((N_DEV,) + (a.shape[1:] if scatter else a.shape), a.dtype) for a in arrays]

    def body(*refs):
        src_refs, land_refs = refs[:n], refs[n:2 * n]
        send_sems, recv_sems, token = refs[2 * n], refs[2 * n + 1], refs[-1]
        for copy in _exchange_copies(scatter, False, src_refs, land_refs, send_sems, recv_sems):
            copy.start()
        token[...] = jnp.zeros_like(token)

    sems = pltpu.SemaphoreType.DMA((n * (N_DEV - 1),))
    res = pl.pallas_call(
        body, name=name,
        out_shape=[sems, sems] + [pltpu.HBM(a.shape, a.dtype) for a in arrays] + [pltpu.HBM(l.shape, l.dtype) for l in lands]
        + [jax.ShapeDtypeStruct((8, 128), F32)],
        in_specs=[_HBM] * (2 * n), out_specs=[_SEM, _SEM] + [_HBM] * (2 * n) + [pl.BlockSpec(memory_space=pltpu.VMEM)],
        input_output_aliases={i: 2 + i for i in range(2 * n)},
        compiler_params=pltpu.CompilerParams(has_side_effects=_EFFECT),
    )(*[pltpu.with_memory_space_constraint(a, pltpu.HBM) for a in list(arrays) + lands])
    return res[0], res[1], res[2:2 + n], res[2 + n:2 + 2 * n], res[-1]


def _exchange_wait_call(name, started, scatter, after):
    send_sems, recv_sems, srcs, lands, _ = started
    n = len(srcs)

    def body(*refs):
        src_refs, land_refs = refs[:n], refs[n:2 * n]
        send_s, recv_s = refs[2 * n], refs[2 * n + 1]
        for copy in _exchange_copies(scatter, False, src_refs, land_refs, send_s, recv_s):
            copy.wait_send()
        for copy in _exchange_copies(scatter, True, src_refs, land_refs, send_s, recv_s):
            copy.wait_recv()

    res = pl.pallas_call(
        body, name=name,
        out_shape=[pltpu.HBM(a.shape, a.dtype) for a in list(srcs) + list(lands)],
        in_specs=[_HBM] * (2 * n) + [_SEM, _SEM, pl.BlockSpec(memory_space=pl.ANY)], out_specs=[_HBM] * (2 * n),
        input_output_aliases={i: i for i in range(2 * n)},
        compiler_params=pltpu.CompilerParams(has_side_effects=_EFFECT),
    )(*srcs, *lands, send_sems, recv_sems, after)
    return res[n:]


def _allsum_small_call(part):
    P = part.shape[0]

    def body(p_ref, o_ref, all_ref, send_sems, recv_sems):
        x, y, c = _position()
        me = 4 * x + 2 * y + c
        peers = [_peer(x, y, c, rel) for rel in range(1, N_DEV)]
        all_ref[me] = p_ref[...]

        def copy(k, slot, dev):
            return pltpu.make_async_remote_copy(
                src_ref=p_ref, dst_ref=all_ref.at[slot], send_sem=send_sems.at[k], recv_sem=recv_sems.at[k],
                device_id=dev, device_id_type=MESH)

        sends = [copy(k, me, dev) for k, (dev, _) in enumerate(peers)]
        for cp in sends:
            cp.start()
        for k, (dev, dev_id) in enumerate(peers):
            copy(k, dev_id, dev).wait_recv()
        for cp in sends:
            cp.wait_send()
        tot = all_ref[0]
        for s in range(1, N_DEV):
            tot = tot + all_ref[s]
        o_ref[...] = tot

    vm = pl.BlockSpec(memory_space=pltpu.VMEM)
    return pl.pallas_call(
        body, name="allsum_small", in_specs=[vm], out_specs=vm,
        out_shape=jax.ShapeDtypeStruct(part.shape, F32),
        scratch_shapes=[pltpu.VMEM((N_DEV, P, 128), F32), pltpu.SemaphoreType.DMA((7,)), pltpu.SemaphoreType.DMA((7,))],
    )(part)


def _pad_rows(v, rows):
    flat = v.reshape(-1)
    return jnp.pad(flat, (0, rows * 128 - flat.shape[0])).reshape(rows, 128)


def _blocks_by_cols(w):
    K, N = w.shape
    return w.reshape(K, N_DEV, N // N_DEV).transpose(1, 0, 2)


def _cols_from_blocks(wb):
    n, K, C = wb.shape
    return wb.transpose(1, 0, 2).reshape(K, n * C)


def _local_step(x, tgt, w_in_i, conv_w, conv_b, ln_g, ln_b, qg, kg, sinks, g_mix, g_ffn, token0, rest_weights, grads_ready, B, S):
    row = lambda v: v.reshape(1, -1)
    qg2 = jnp.tile(qg, 2).reshape(1, 128)
    kg2 = jnp.tile(kg, 2).reshape(1, 128)
    heads = jnp.arange(1, N_Q_HEADS + 1, dtype=F32)
    slopes = jnp.exp2(-8.0 * heads / N_Q_HEADS)

    u, xn = _inproj_call(x, row(g_mix) + token0, w_in_i)
    hc, c = _conv_fwd_call(u, conv_w, row(conv_b), row(ln_g), row(ln_b), B, S)
    o = _attn_fwd_call(u, qg2, kg2, sinks, slopes, B, S)
    wc, wa, wm, w_ffn_in, wd = rest_weights(o)
    yc, ya, z, h, hn = _merge_call(c, o, u, x, wc, wa, wm, row(g_ffn))
    gu, act = _ffn_in_call(hn, w_ffn_in)
    sse, dout, dout_b = _ffn_out_call(act, wd, h, tgt)

    dgu = _ffn_down_bwd_call(dout_b, wd, gu)
    g_wd = _dw_call("dw_ffn_down", act, dout_b, tn=512)
    g_wffn = _dw_call("dw_ffn_in", hn, dgu, tn=2 * D_FF // 4)
    tok = grads_ready(dict(w_ffn_down=g_wd, w_ffn_in=g_wffn))
    dh, dh_b, g_gffn = _ffn_in_bwd_call(dgu, w_ffn_in, h, row(g_ffn) + tok, dout)
    dyc, dya, du_g, dcp, do = _merge_bwd_call(dh_b, wm, wc, wa, u, yc, ya)
    g_wm = _dw_call("dw_merge", z, dh_b, tn=D_MODEL)
    g_wc = _dw_call("dw_conv_out", c, dyc, tn=D_MODEL)
    g_wa = _dw_call("dw_attn_out", o, dya, tn=D_MODEL)
    tok = grads_ready(dict(w_merge_out=g_wm, w_conv_out=g_wc, w_attn_out=g_wa))
    du_conv, g_cw8, g_cb, g_lg, g_lb = _conv_bwd_call(dcp, hc, u, conv_w, row(ln_g) + tok, row(ln_b), B, S)
    du_q, du_kv, g_qg2, g_kg2, g_sk = _attn_bwd_call(u, do, qg2, kg2, sinks, slopes, B, S)
    g_win = jnp.concatenate([
        _dw_call("dw_in_conv", xn, du_conv, tn=D_MODEL),
        _dw_call("dw_in_q", xn, du_q, tn=D_MODEL),
        _dw_call("dw_in_kv", xn, du_kv, tn=2 * KV_W),
        _dw_call("dw_in_gates", xn, du_g, tn=D_MODEL)], axis=1)
    tok = grads_ready(dict(w_in=g_win))
    grad_x, g_gmix = _inproj_bwd_call(du_conv, du_g, du_q, du_kv, w_in_i, x, row(g_mix) + tok, dh)

    g_cw = g_cw8.reshape(CONV_WIDTH, 8, C_CONV).sum(axis=1)
    g_qg = g_qg2[0, :HEAD_DIM] + g_qg2[0, HEAD_DIM:]
    g_kg = g_kg2[0, :HEAD_DIM] + g_kg2[0, HEAD_DIM:]
    small = dict(norm_mix_g=g_gmix[0], conv_dw_b=g_cb[0], conv_ln_g=g_lg[0], conv_ln_b=g_lb[0], q_norm_g=g_qg,
                 k_norm_g=g_kg, sinks=g_sk[0, :N_Q_HEADS], norm_ffn_g=g_gffn[0], conv_dw_w=g_cw)
    return sse[0, 0], grad_x, small


def _to_internal_cols(w):
    o1 = 2 * C_CONV
    o4 = o1 + Q_W + 2 * KV_W
    return jnp.concatenate([w[:, :o1], w[:, o4:], w[:, o1:o4]], axis=1)


_COL_SHARDED = ("w_in", "w_ffn_in")

_SMALL = (("norm_mix_g", 8), ("conv_dw_b", 8), ("conv_ln_g", 8), ("conv_ln_b", 8), ("norm_ffn_g", 8),
          ("q_norm_g", 1), ("k_norm_g", 1), ("sinks", 1))
_SMALL_ROWS = 48
_CW_ROWS = CONV_WIDTH * C_CONV // 128


def _pack_small(d):
    rows = [_pad_rows(d[name], r) for name, r in _SMALL]
    used = sum(r for _, r in _SMALL)
    return jnp.concatenate(rows + [jnp.zeros((_SMALL_ROWS - used, 128), F32)], axis=0)


def _unpack_small(packed, like):
    out, r0 = {}, 0
    for name, r in _SMALL:
        n = like[name].size
        out[name] = packed[r0:r0 + r].reshape(-1)[:n].reshape(like[name].shape)
        r0 += r
    return out


def kernel(x, norm_mix_g, w_in, conv_dw_w, conv_dw_b, conv_ln_g, conv_ln_b, w_conv_out, q_norm_g, k_norm_g, sinks, w_attn_out, w_merge_out, norm_ffn_g, w_ffn_in, w_ffn_down, loss_target, m_norm_mix_g, m_w_in, m_conv_dw_w, m_conv_dw_b, m_conv_ln_g, m_conv_ln_b, m_w_conv_out, m_q_norm_g, m_k_norm_g, m_sinks, m_w_attn_out, m_w_merge_out, m_norm_ffn_g, m_w_ffn_in, m_w_ffn_down, v_norm_mix_g, v_w_in, v_conv_dw_w, v_conv_dw_b, v_conv_ln_g, v_conv_ln_b, v_w_conv_out, v_q_norm_g, v_k_norm_g, v_sinks, v_w_attn_out, v_w_merge_out, v_norm_ffn_g, v_w_ffn_in, v_w_ffn_down):
    names = ["norm_mix_g", "w_in", "conv_dw_w", "conv_dw_b", "conv_ln_g", "conv_ln_b", "w_conv_out", "q_norm_g",
             "k_norm_g", "sinks", "w_attn_out", "w_merge_out", "norm_ffn_g", "w_ffn_in", "w_ffn_down"]
    w = dict(zip(names, (norm_mix_g, w_in, conv_dw_w, conv_dw_b, conv_ln_g, conv_ln_b, w_conv_out, q_norm_g, k_norm_g,
                         sinks, w_attn_out, w_merge_out, norm_ffn_g, w_ffn_in, w_ffn_down)))
    m = dict(zip(names, (m_norm_mix_g, m_w_in, m_conv_dw_w, m_conv_dw_b, m_conv_ln_g, m_conv_ln_b, m_w_conv_out,
                         m_q_norm_g, m_k_norm_g, m_sinks, m_w_attn_out, m_w_merge_out, m_norm_ffn_g, m_w_ffn_in,
                         m_w_ffn_down)))
    v = dict(zip(names, (v_norm_mix_g, v_w_in, v_conv_dw_w, v_conv_dw_b, v_conv_ln_g, v_conv_ln_b, v_w_conv_out,
                         v_q_norm_g, v_k_norm_g, v_sinks, v_w_attn_out, v_w_merge_out, v_norm_ffn_g, v_w_ffn_in,
                         v_w_ffn_down)))
    B, S, _ = x.shape
    T = B * S
    me = 4 * lax.axis_index("x") + 2 * lax.axis_index("y") + lax.axis_index("c")

    cw_shard = _pad_rows(conv_dw_w.reshape(CONV_WIDTH, 128), 32)
    g_win, g_cwt = _gather_weights_call([w_in.astype(BF16), cw_shard])
    conv_w_full = g_cwt.reshape(N_DEV, 32 * 128)[:, :CONV_WIDTH * 128].reshape(N_DEV, CONV_WIDTH, 128)
    conv_w_full = conv_w_full.transpose(1, 0, 2).reshape(CONV_WIDTH, C_CONV)
    w_in_i = _to_internal_cols(_cols_from_blocks(g_win))
    rest_names = ["w_conv_out", "w_attn_out", "w_merge_out", "w_ffn_in", "w_ffn_down"]
    rest_shards = [w[k].astype(BF16) for k in rest_names]
    rest_started = _exchange_start_call("gather_rest_start", rest_shards, scatter=False)

    def rest_weights(after):
        lands = _exchange_wait_call("gather_rest_wait", rest_started, False, after)
        mine = lax.broadcasted_iota(jnp.int32, (N_DEV, 1, 1), 0) == me
        full = [jnp.where(mine, s[None], l) for l, s in zip(lands, rest_shards)]
        wc_f, wa_f, wm_f = (f.reshape(D_MODEL, D_MODEL) for f in full[:3])
        return wc_f, wa_f, wm_f, _cols_from_blocks(full[3]), full[4].reshape(D_FF, D_MODEL)

    exchanges = []

    def grads_ready(group):
        ks = list(group)
        blocks = [_blocks_by_cols(group[k]) if k in _COL_SHARDED
                  else group[k].reshape(N_DEV, group[k].shape[0] // N_DEV, group[k].shape[1]) for k in ks]
        started = _exchange_start_call("scatter_" + ks[0] + "_start", blocks, scatter=True)
        exchanges.append((ks, started, [lax.dynamic_index_in_dim(b, me, 0, keepdims=False) for b in blocks]))
        return started[-1][0:1, 0:1]

    sse, grad_x, small = _local_step(
        x.reshape(T, D_MODEL), loss_target.reshape(T, D_MODEL), w_in_i, conv_w_full, conv_dw_b, conv_ln_g, conv_ln_b,
        q_norm_g, k_norm_g, sinks, norm_mix_g, norm_ffn_g, rest_started[-1][0:1, 0:1], rest_weights, grads_ready, B, S)

    grad, delta, new_m, new_v = {}, {}, {}, {}
    packed = jnp.concatenate([_pack_small(small), small["conv_dw_w"].reshape(_CW_ROWS, 128),
                              jnp.full((8, 128), sse, F32)], axis=0)
    tot = _allsum_small_call(packed)
    loss = 0.5 / D_MODEL * tot[_SMALL_ROWS + _CW_ROWS, 0]
    g_small = _unpack_small(tot[:_SMALL_ROWS], w)
    d_s, m_s, v_s = _adamw_small_call("adamw_small", tot[:_SMALL_ROWS], _pack_small(w), _pack_small(m), _pack_small(v))
    d_small, m_small, v_small = _unpack_small(d_s, w), _unpack_small(m_s, w), _unpack_small(v_s, w)
    g_cw = lax.dynamic_slice_in_dim(tot[_SMALL_ROWS:_SMALL_ROWS + _CW_ROWS].reshape(CONV_WIDTH, C_CONV), me * 128, 128, axis=1)
    cw2 = lambda t: t.reshape(CONV_WIDTH, 128)
    d_cw, m_cw, v_cw = _adamw_small_call("adamw_conv_w", g_cw, cw2(conv_dw_w), cw2(m_conv_dw_w), cw2(v_conv_dw_w))
    cw3 = lambda t: t.reshape(CONV_WIDTH, 1, 128)
    for k, _ in _SMALL:
        grad[k], delta[k], new_m[k], new_v[k] = g_small[k], d_small[k], m_small[k], v_small[k]
    grad["conv_dw_w"], delta["conv_dw_w"], new_m["conv_dw_w"], new_v["conv_dw_w"] = cw3(g_cw), cw3(d_cw), cw3(m_cw), cw3(v_cw)

    after = d_cw
    for ks, started, owns in exchanges:
        lands = _exchange_wait_call("scatter_" + ks[0] + "_wait", started, True, after)
        for k, land, own in zip(ks, lands, owns):
            tr = {1024: 256, 352: 176, 128: 128}[w[k].shape[0]]
            grad[k], delta[k], new_m[k], new_v[k] = _adamw_sum_call("adamw_" + k, land, own, w[k], m[k], v[k], tr)
            after = new_v[k]

    return (loss, grad_x.reshape(B, S, D_MODEL), *[grad[k] for k in names], *[delta[k] for k in names],
            *[new_m[k] for k in names], *[new_v[k] for k in names])
```

```python
import functools
import math

import jax
import jax.numpy as jnp
from jax import lax
from jax.experimental import pallas as pl
from jax.experimental.pallas import tpu as pltpu

F32 = jnp.float32
BF16 = jnp.bfloat16
MESH = pl.DeviceIdType.MESH

N_DEV = 8
D_MODEL = 1024
C_CONV = 1024
CONV_WIDTH = 31
HEAD_DIM = 64
N_Q_HEADS = 16
N_KV_HEADS = 2
GROUP = N_Q_HEADS // N_KV_HEADS
WINDOW = 128
BLOCK = 128
D_FF = 2816
EPS = 1e-6
NEG = -1e30
ATTN_SCALE = 1.0 / math.sqrt(HEAD_DIM)
Q_W = N_Q_HEADS * HEAD_DIM
KV_W = N_KV_HEADS * HEAD_DIM
IN_COLS = 2 * C_CONV + Q_W + 2 * KV_W + 2 * D_MODEL

ADAM_LR = 0.001
ADAM_B1 = 0.9
ADAM_B2 = 0.999
ADAM_EPS = 1e-08
ADAM_WD = 0.01
ADAM_STEP = 10

COL_A, COL_GATE, COL_GC, COL_GA, COL_Q, COL_K, COL_V = 0, 1024, 2048, 3072, 4096, 5120, 5248

HALO = 32
VMEM_LIMIT = 56 * 1024 * 1024


def _cp(*sem):
    return pltpu.CompilerParams(dimension_semantics=sem, vmem_limit_bytes=VMEM_LIMIT)


def _sigmoid(x):
    return jax.nn.sigmoid(x)


def _dot(a, b):
    return jnp.dot(a, b, preferred_element_type=F32)


def _dot_nt(a, b):
    return lax.dot_general(a, b, (((1,), (1,)), ((), ())), preferred_element_type=F32)


def _dot_tn(a, b):
    return lax.dot_general(a, b, (((0,), (0,)), ((), ())), preferred_element_type=F32)


def _full(shape):
    return pl.BlockSpec(shape, lambda *_: (0,) * len(shape))


def _rows(tm, ncols, colblk=0):
    return pl.BlockSpec((tm, ncols), lambda i: (i, colblk))


def _inproj_call(x, g, w, tm=512):
    T = x.shape[0]
    N = w.shape[1]

    def body(x_ref, g_ref, w_ref, u_ref, xn_ref):
        xv = x_ref[...]
        r = lax.rsqrt(jnp.mean(xv * xv, axis=-1, keepdims=True) + EPS)
        xn = (xv * r * g_ref[...]).astype(BF16)
        xn_ref[...] = xn
        u_ref[...] = _dot(xn, w_ref[...]).astype(BF16)

    return pl.pallas_call(
        body, name="inproj", grid=(T // tm,),
        in_specs=[_rows(tm, D_MODEL), _full((1, D_MODEL)), _full(w.shape)],
        out_specs=[_rows(tm, N), _rows(tm, D_MODEL)],
        out_shape=[jax.ShapeDtypeStruct((T, N), BF16), jax.ShapeDtypeStruct((T, D_MODEL), BF16)],
        compiler_params=_cp("parallel"),
    )(x, g, w)


def _fill_shifted(src_ref, sh_ref):
    n = src_ref.shape[0] - 8
    for s in range(1, 8):
        sh_ref[s - 1, 0:n, :] = src_ref[s:s + n, :]


def _window(src_ref, sh_ref, off, cols):
    s = off % 8
    if s == 0:
        return src_ref[off:off + 128, cols]
    return sh_ref[s - 1, off - s:off - s + 128, cols]


def _conv_taps(w_ref, src_ref, sh_ref, base, rows, cb, reverse):
    cols = pl.ds(pl.multiple_of(cb * 128, 128), 128)
    outs = []
    for rb in range(rows // 128):
        acc = jnp.zeros((128, 128), F32)
        for j in range(CONV_WIDTH):
            off = base + (CONV_WIDTH - 1 - j if reverse else j) + rb * 128
            acc = acc + w_ref[j:j + 1, cols] * _window(src_ref, sh_ref, off, cols)
        outs.append(acc)
    return outs


def _conv_fwd_call(u, conv_w, conv_b, ln_g, ln_b, B, S, ts=256):
    T = B * S
    nS = S // ts
    per32 = ts // HALO

    def body(a_ref, gate_ref, ap_ref, gp_ref, w_ref, b_ref, g_ref, bb_ref, hc_ref, c_ref, hext, hsh, acc_ref):
        s = pl.program_id(1)
        prev = ap_ref[...].astype(F32) * _sigmoid(gp_ref[...].astype(F32))
        hext[0:HALO, :] = jnp.where(s > 0, prev, 0.0)
        hext[HALO:, :] = a_ref[...].astype(F32) * _sigmoid(gate_ref[...].astype(F32))
        _fill_shifted(hext, hsh)

        def colblock(cb, carry):
            cols = pl.ds(pl.multiple_of(cb * 128, 128), 128)
            outs = _conv_taps(w_ref, hext, hsh, HALO - (CONV_WIDTH - 1), ts, cb, reverse=False)
            for rb, acc in enumerate(outs):
                acc_ref[rb * 128:(rb + 1) * 128, cols] = acc
            return carry

        lax.fori_loop(0, C_CONV // 128, colblock, 0)
        hc = acc_ref[...] + b_ref[...]
        hc_ref[...] = hc
        mu = jnp.mean(hc, axis=-1, keepdims=True)
        xc = hc - mu
        var = jnp.mean(xc * xc, axis=-1, keepdims=True)
        y = xc * lax.rsqrt(var + EPS) * g_ref[...] + bb_ref[...]
        c_ref[...] = (y * _sigmoid(y)).astype(BF16)

    cur = lambda cb: pl.BlockSpec((ts, C_CONV), lambda b, s: (b * nS + s, cb))
    prv = lambda cb: pl.BlockSpec((HALO, C_CONV), lambda b, s: (jnp.maximum(b * (S // HALO) + s * per32 - 1, 0), cb))
    vec = _full((1, C_CONV))
    return pl.pallas_call(
        body, name="conv_fwd", grid=(B, nS),
        in_specs=[cur(0), cur(1), prv(0), prv(1), _full((CONV_WIDTH, C_CONV)), vec, vec, vec],
        out_specs=[pl.BlockSpec((ts, C_CONV), lambda b, s: (b * nS + s, 0))] * 2,
        out_shape=[jax.ShapeDtypeStruct((T, C_CONV), F32), jax.ShapeDtypeStruct((T, C_CONV), BF16)],
        scratch_shapes=[pltpu.VMEM((HALO + ts, C_CONV), F32), pltpu.VMEM((7, HALO + ts, C_CONV), F32),
                        pltpu.VMEM((ts, C_CONV), F32)],
        compiler_params=_cp("parallel", "arbitrary"),
    )(u, u, u, u, conv_w, conv_b, ln_g, ln_b)


def _lane_lo():
    return lax.broadcasted_iota(jnp.int32, (1, 128), 1) < HEAD_DIM


def _half_sums(t, lo):
    s_lo = jnp.sum(jnp.where(lo, t, 0.0), axis=-1, keepdims=True)
    s_hi = jnp.sum(jnp.where(lo, 0.0, t), axis=-1, keepdims=True)
    return jnp.where(lo, s_lo, s_hi)


def _head_rstd(t, lo):
    return lax.rsqrt(_half_sums(t * t, lo) * (1.0 / HEAD_DIM) + EPS)


def _fill_bias(btab, kvh, slopes_ref):
    sj = lax.broadcasted_iota(jnp.int32, (2 * BLOCK, BLOCK), 0)
    qi = lax.broadcasted_iota(jnp.int32, (2 * BLOCK, BLOCK), 1)
    dist = qi + BLOCK - sj
    valid = (dist >= 0) & (dist < WINDOW)
    distf = dist.astype(F32)
    for g in range(GROUP):
        bias = jnp.where(valid, -slopes_ref[kvh * GROUP + g] * distf, NEG)
        btab[1, g] = bias
        btab[0, g] = jnp.where(sj >= BLOCK, bias, NEG)


def _softmax_cols(st_ref, btab, p, var, g, hf, sk):
    sh = st_ref[2 * BLOCK * hf:2 * BLOCK * (hf + 1), 128 * p:128 * (p + 1)] + btab[var, g]
    m = jnp.maximum(jnp.max(sh, axis=0, keepdims=True), sk)
    e = jnp.exp(sh - m)
    esk = jnp.exp(sk - m)
    rz = 1.0 / (jnp.sum(e, axis=0, keepdims=True) + esk)
    return e * rz, esk, rz


def _stage_kv(kvh, k_ref, v_ref, kg_ref, lo, kA, kB, vA, vB):
    kv = k_ref[...].astype(F32)
    rk = _head_rstd(kv, lo)
    kn = kv * rk * kg_ref[...]
    mine = lax.broadcasted_iota(jnp.int32, (1, 128), 1) // HEAD_DIM == kvh
    ksel = jnp.where(mine, kn, 0.0)
    vsel = jnp.where(mine, v_ref[...].astype(F32), 0.0)
    krol = pltpu.roll(ksel, HEAD_DIM, 1)
    vrol = pltpu.roll(vsel, HEAD_DIM, 1)
    first = kvh == 0
    zeros = jnp.zeros((BLOCK, 128), BF16)
    for ref, val in ((kA, jnp.where(first, ksel, krol)), (kB, jnp.where(first, krol, ksel)),
                     (vA, jnp.where(first, vsel, vrol)), (vB, jnp.where(first, vrol, vsel))):
        ref[0:BLOCK, :] = zeros
        ref[BLOCK:, :] = val.astype(BF16)


def _attn_fwd_call(u, qg2, kg2, sinks, slopes, B, S):
    T = B * S
    nb = S // BLOCK
    half = Q_W // 2
    npair = GROUP // 2

    def body(sinks_ref, slopes_ref, q_ref, k_ref, v_ref, qg_ref, kg_ref, o_ref, kA, kB, vA, vB, btab, qn_buf, s_buf, p_buf):
        kvh = pl.program_id(1)
        lo = _lane_lo()
        _stage_kv(kvh, k_ref, v_ref, kg_ref, lo, kA, kB, vA, vB)
        _fill_bias(btab, kvh, slopes_ref)
        qgs = qg_ref[...] * ATTN_SCALE

        def block(n, slot):
            qn_scr, s_scr, p_scr = qn_buf.at[slot], s_buf.at[slot], p_buf.at[slot]
            r0 = pl.multiple_of(n * BLOCK, BLOCK)
            win = pl.ds(r0, 2 * BLOCK)
            k2t = jnp.concatenate([kA[win, :], kB[win, :]], axis=0)
            v2 = jnp.concatenate([vA[win, :], vB[win, :]], axis=0)
            var = jnp.minimum(n, 1)
            for p in range(npair):
                pc = slice(128 * p, 128 * (p + 1))
                qp = q_ref[pl.ds(r0, BLOCK), pc].astype(F32)
                qn_scr[pc, :] = (qp * _head_rstd(qp, lo) * qgs).astype(BF16)
            s_scr[...] = _dot_nt(k2t, qn_scr[...])
            for p in range(npair):
                for hf in range(2):
                    g = 2 * p + hf
                    pn, _, _ = _softmax_cols(s_scr, btab, p, var, g, hf, sinks_ref[kvh * GROUP + g])
                    p_scr[2 * BLOCK * hf:2 * BLOCK * (hf + 1), 128 * p:128 * (p + 1)] = pn.astype(BF16)
            o_all = _dot_tn(v2, p_scr[...]).T
            for p in range(npair):
                pc = slice(128 * p, 128 * (p + 1))
                o_ref[pl.ds(r0, BLOCK), pc] = o_all[pc, :].astype(BF16)

        def two_blocks(i, carry):
            block(2 * i, 0)
            block(2 * i + 1, 1)
            return carry

        lax.fori_loop(0, nb // 2, two_blocks, 0)

    smem = pl.BlockSpec(memory_space=pltpu.SMEM)
    kvbuf = pltpu.VMEM((S + BLOCK, 128), BF16)
    pair_scores = (2, npair * BLOCK, 4 * BLOCK)
    return pl.pallas_call(
        body, name="attn_fwd", grid=(B, N_KV_HEADS),
        in_specs=[smem, smem,
                  pl.BlockSpec((S, half), lambda b, h: (b, COL_Q // half + h)),
                  pl.BlockSpec((S, 128), lambda b, h: (b, COL_K // 128)),
                  pl.BlockSpec((S, 128), lambda b, h: (b, COL_V // 128)),
                  _full((1, 128)), _full((1, 128))],
        out_specs=pl.BlockSpec((S, half), lambda b, h: (b, h)),
        out_shape=jax.ShapeDtypeStruct((T, Q_W), BF16),
        scratch_shapes=[kvbuf, kvbuf, kvbuf, kvbuf, pltpu.VMEM((2, GROUP, 2 * BLOCK, BLOCK), F32),
                        pltpu.VMEM((2, npair * BLOCK, 128), BF16), pltpu.VMEM(pair_scores, F32), pltpu.VMEM(pair_scores, BF16)],
        compiler_params=_cp("parallel", "arbitrary"),
    )(sinks, slopes, u, u, u, qg2, kg2)


def _merge_call(c, o, u, x, wc, wa, wm, g_ffn, tm=512):
    T = x.shape[0]

    def body(c_ref, o_ref, gc_ref, ga_ref, x_ref, wc_ref, wa_ref, wm_ref, g_ref, yc_ref, ya_ref, z_ref, h_ref, hn_ref):
        yc = _dot(c_ref[...], wc_ref[...])
        ya = _dot(o_ref[...], wa_ref[...])
        yc_ref[...] = yc.astype(BF16)
        ya_ref[...] = ya.astype(BF16)
        z = (_sigmoid(gc_ref[...].astype(F32)) * yc + _sigmoid(ga_ref[...].astype(F32)) * ya).astype(BF16)
        z_ref[...] = z
        h = x_ref[...] + _dot(z, wm_ref[...])
        h_ref[...] = h
        r = lax.rsqrt(jnp.mean(h * h, axis=-1, keepdims=True) + EPS)
        hn_ref[...] = (h * r * g_ref[...]).astype(BF16)

    sq = _full((D_MODEL, D_MODEL))
    row = _rows(tm, D_MODEL)
    f32 = jax.ShapeDtypeStruct((T, D_MODEL), F32)
    b16 = jax.ShapeDtypeStruct((T, D_MODEL), BF16)
    return pl.pallas_call(
        body, name="merge_fwd", grid=(T // tm,),
        in_specs=[row, row, _rows(tm, D_MODEL, COL_GC // D_MODEL), _rows(tm, D_MODEL, COL_GA // D_MODEL), row,
                  sq, sq, sq, _full((1, D_MODEL))],
        out_specs=[row] * 5,
        out_shape=[b16, b16, b16, f32, b16],
        compiler_params=_cp("parallel"),
    )(c, o, u, u, x, wc, wa, wm, g_ffn)


def _ffn_in_call(hn, w, tm=512):
    T = hn.shape[0]

    def body(hn_ref, w_ref, fac_ref, act_ref):
        gu = _dot(hn_ref[...], w_ref[...])
        g = gu[:, :D_FF]
        up = gu[:, D_FF:]
        sg = _sigmoid(g)
        silu = g * sg
        act_ref[...] = (silu * up).astype(BF16)
        fac_ref[:, :D_FF] = (up * (sg * (1.0 + g * (1.0 - sg)))).astype(BF16)
        fac_ref[:, D_FF:] = silu.astype(BF16)

    return pl.pallas_call(
        body, name="ffn_in", grid=(T // tm,),
        in_specs=[_rows(tm, D_MODEL), _full(w.shape)],
        out_specs=[_rows(tm, 2 * D_FF), _rows(tm, D_FF)],
        out_shape=[jax.ShapeDtypeStruct((T, 2 * D_FF), BF16), jax.ShapeDtypeStruct((T, D_FF), BF16)],
        compiler_params=_cp("parallel"),
    )(hn, w)


def _ffn_out_call(act, wd, h, tgt, tm=512):
    T = h.shape[0]

    def body(act_ref, wd_ref, h_ref, t_ref, sse_ref, d_ref, db_ref):
        e = h_ref[...] + _dot(act_ref[...], wd_ref[...]) - t_ref[...]

        @pl.when(pl.program_id(0) == 0)
        def _():
            sse_ref[...] = jnp.zeros_like(sse_ref)

        sse_ref[...] += jnp.sum(e * e)
        d = e * (1.0 / D_MODEL)
        d_ref[...] = d
        db_ref[...] = d.astype(BF16)

    row = _rows(tm, D_MODEL)
    return pl.pallas_call(
        body, name="ffn_out", grid=(T // tm,),
        in_specs=[_rows(tm, D_FF), _full(wd.shape), row, row],
        out_specs=[_full((8, 128)), row, row],
        out_shape=[jax.ShapeDtypeStruct((8, 128), F32), jax.ShapeDtypeStruct((T, D_MODEL), F32),
                   jax.ShapeDtypeStruct((T, D_MODEL), BF16)],
        compiler_params=_cp("arbitrary"),
    )(act, wd, h, tgt)


def _dw_call(name, a, b, tn, tk=2048):
    T, K = a.shape
    N = b.shape[1]
    tk = min(tk, T)
    nk = T // tk

    def body(a_ref, b_ref, o_ref, acc_ref):
        k = pl.program_id(1)

        @pl.when(k == 0)
        def _():
            acc_ref[...] = jnp.zeros_like(acc_ref)

        acc_ref[...] += _dot_tn(a_ref[...], b_ref[...])

        @pl.when(k == nk - 1)
        def _():
            o_ref[...] = acc_ref[...].astype(BF16)

    return pl.pallas_call(
        body, name=name, grid=(N // tn, nk),
        in_specs=[pl.BlockSpec((tk, K), lambda j, k: (k, 0)), pl.BlockSpec((tk, tn), lambda j, k: (k, j))],
        out_specs=pl.BlockSpec((K, tn), lambda j, k: (0, j)),
        out_shape=jax.ShapeDtypeStruct((K, N), BF16),
        scratch_shapes=[pltpu.VMEM((K, tn), F32)],
        compiler_params=_cp("parallel", "arbitrary"),
    )(a, b)


def _rms_bwd(dy, xv, g):
    r = lax.rsqrt(jnp.mean(xv * xv, axis=-1, keepdims=True) + EPS)
    t = dy * g
    dx = r * t - xv * (r * r * r * jnp.mean(t * xv, axis=-1, keepdims=True))
    return dx, dy * xv * r


def _ffn_down_bwd_call(dout_b, wd, gu, tm=512):
    T = dout_b.shape[0]

    def body(d_ref, wd_ref, fg_ref, fu_ref, o_ref):
        dact = _dot_nt(d_ref[...], wd_ref[...])
        o_ref[:, :D_FF] = (dact * fg_ref[...].astype(F32)).astype(BF16)
        o_ref[:, D_FF:] = (dact * fu_ref[...].astype(F32)).astype(BF16)

    return pl.pallas_call(
        body, name="ffn_down_bwd", grid=(T // tm,),
        in_specs=[_rows(tm, D_MODEL), _full(wd.shape), _rows(tm, D_FF, 0), _rows(tm, D_FF, 1)],
        out_specs=_rows(tm, 2 * D_FF),
        out_shape=jax.ShapeDtypeStruct((T, 2 * D_FF), BF16),
        compiler_params=_cp("parallel"),
    )(dout_b, wd, gu, gu)


def _ffn_in_bwd_call(dgu, w, h, g_ffn, dout, tm=512):
    T = h.shape[0]

    def body(dgu_ref, w_ref, h_ref, g_ref, d_ref, dh_ref, dhb_ref, dg_ref):
        dhn = _dot_nt(dgu_ref[...], w_ref[...])
        dx, dgt = _rms_bwd(dhn, h_ref[...], g_ref[...])
        dh = d_ref[...] + dx
        dh_ref[...] = dh
        dhb_ref[...] = dh.astype(BF16)

        @pl.when(pl.program_id(0) == 0)
        def _():
            dg_ref[...] = jnp.zeros_like(dg_ref)

        dg_ref[...] += jnp.sum(dgt, axis=0, keepdims=True)

    row = _rows(tm, D_MODEL)
    return pl.pallas_call(
        body, name="ffn_in_bwd", grid=(T // tm,),
        in_specs=[_rows(tm, 2 * D_FF), _full(w.shape), row, _full((1, D_MODEL)), row],
        out_specs=[row, row, _full((1, D_MODEL))],
        out_shape=[jax.ShapeDtypeStruct((T, D_MODEL), F32), jax.ShapeDtypeStruct((T, D_MODEL), BF16),
                   jax.ShapeDtypeStruct((1, D_MODEL), F32)],
        compiler_params=_cp("arbitrary"),
    )(dgu, w, h, g_ffn, dout)


def _merge_bwd_call(dh_b, wm, wc, wa, u, yc, ya, tm=512):
    T = dh_b.shape[0]

    def body(dh_ref, wm_ref, wc_ref, wa_ref, gc_ref, ga_ref, yc_ref, ya_ref, dyc_ref, dya_ref, dug_ref, dcp_ref, do_ref):
        dz = _dot_nt(dh_ref[...], wm_ref[...])
        sgc = _sigmoid(gc_ref[...].astype(F32))
        sga = _sigmoid(ga_ref[...].astype(F32))
        dyc = (dz * sgc).astype(BF16)
        dya = (dz * sga).astype(BF16)
        dyc_ref[...] = dyc
        dya_ref[...] = dya
        dug_ref[:, :D_MODEL] = (dz * yc_ref[...].astype(F32) * (sgc * (1.0 - sgc))).astype(BF16)
        dug_ref[:, D_MODEL:] = (dz * ya_ref[...].astype(F32) * (sga * (1.0 - sga))).astype(BF16)
        dcp_ref[...] = _dot_nt(dyc, wc_ref[...])
        do_ref[...] = _dot_nt(dya, wa_ref[...]).astype(BF16)

    sq = _full((D_MODEL, D_MODEL))
    row = _rows(tm, D_MODEL)
    b16 = jax.ShapeDtypeStruct((T, D_MODEL), BF16)
    return pl.pallas_call(
        body, name="merge_bwd", grid=(T // tm,),
        in_specs=[row, sq, sq, sq, _rows(tm, D_MODEL, COL_GC // D_MODEL), _rows(tm, D_MODEL, COL_GA // D_MODEL), row, row],
        out_specs=[row, row, _rows(tm, 2 * D_MODEL), row, row],
        out_shape=[b16, b16, jax.ShapeDtypeStruct((T, 2 * D_MODEL), BF16), jax.ShapeDtypeStruct((T, D_MODEL), F32), b16],
        compiler_params=_cp("parallel"),
    )(dh_b, wm, wc, wa, u, u, yc, ya)


def _conv_bwd_call(dcp, hc, u, conv_w, ln_g, ln_b, B, S, ts=256):
    T = B * S
    nS = S // ts
    per32 = ts // HALO
    last32 = T // HALO - 1

    def ln_bwd(dcp_v, hc_v, g, bb):
        mu = jnp.mean(hc_v, axis=-1, keepdims=True)
        xc = hc_v - mu
        rstd = lax.rsqrt(jnp.mean(xc * xc, axis=-1, keepdims=True) + EPS)
        xhat = xc * rstd
        y = xhat * g + bb
        sy = _sigmoid(y)
        dy = dcp_v * (sy * (1.0 + y * (1.0 - sy)))
        dxh = dy * g
        dhc = rstd * (dxh - jnp.mean(dxh, axis=-1, keepdims=True) - xhat * jnp.mean(dxh * xhat, axis=-1, keepdims=True))
        return dhc, dy * xhat, dy

    def body(dcp_ref, dcpn_ref, hc_ref, hcn_ref, a_ref, gate_ref, ap_ref, gp_ref, w_ref, g_ref, bb_ref,
             du_ref, dw_ref, db_ref, dlg_ref, dlb_ref, dhext, hgext, dhsh, hgsh, acc_ref):
        b = pl.program_id(0)
        s = pl.program_id(1)

        @pl.when((b == 0) & (s == 0))
        def _():
            dw_ref[...] = jnp.zeros_like(dw_ref)
            db_ref[...] = jnp.zeros_like(db_ref)
            dlg_ref[...] = jnp.zeros_like(dlg_ref)
            dlb_ref[...] = jnp.zeros_like(dlb_ref)

        dhc, dgt, dbt = ln_bwd(dcp_ref[...], hc_ref[...], g_ref[...], bb_ref[...])
        dhn, _, _ = ln_bwd(dcpn_ref[...], hcn_ref[...], g_ref[...], bb_ref[...])
        dhext[0:ts, :] = dhc
        dhext[ts:, :] = jnp.where(s < nS - 1, dhn, 0.0)
        db_ref[...] += jnp.sum(dhc, axis=0, keepdims=True)
        dlg_ref[...] += jnp.sum(dgt, axis=0, keepdims=True)
        dlb_ref[...] += jnp.sum(dbt, axis=0, keepdims=True)

        sg = _sigmoid(gate_ref[...].astype(F32))
        av = a_ref[...].astype(F32)
        hgext[0:HALO, :] = jnp.where(s > 0, ap_ref[...].astype(F32) * _sigmoid(gp_ref[...].astype(F32)), 0.0)
        hgext[HALO:, :] = av * sg
        _fill_shifted(dhext, dhsh)
        _fill_shifted(hgext, hgsh)

        def colblock(cb, carry):
            cols = pl.ds(pl.multiple_of(cb * 128, 128), 128)
            outs = _conv_taps(w_ref, dhext, dhsh, 0, ts, cb, reverse=True)
            for rb, acc in enumerate(outs):
                acc_ref[rb * 128:(rb + 1) * 128, cols] = acc
            dhc_rows = [dhext[rb * 128:(rb + 1) * 128, cols] for rb in range(ts // 128)]
            for j in range(CONV_WIDTH):
                part = jnp.zeros((8, 128), F32)
                for rb in range(ts // 128):
                    glu = _window(hgext, hgsh, HALO - (CONV_WIDTH - 1) + j + rb * 128, cols)
                    part = part + jnp.sum((dhc_rows[rb] * glu).reshape(16, 8, 128), axis=0)
                dw_ref[8 * j:8 * (j + 1), cols] += part
            return carry

        lax.fori_loop(0, C_CONV // 128, colblock, 0)
        dglu = acc_ref[...]
        du_ref[:, :C_CONV] = (dglu * sg).astype(BF16)
        du_ref[:, C_CONV:] = (dglu * av * (sg * (1.0 - sg))).astype(BF16)

    cur = lambda cb: pl.BlockSpec((ts, C_CONV), lambda b, s: (b * nS + s, cb))
    prv = lambda cb: pl.BlockSpec((HALO, C_CONV), lambda b, s: (jnp.maximum(b * (S // HALO) + s * per32 - 1, 0), cb))
    nxt = pl.BlockSpec((HALO, C_CONV), lambda b, s: (jnp.minimum(b * (S // HALO) + (s + 1) * per32, last32), 0))
    vec = _full((1, C_CONV))
    return pl.pallas_call(
        body, name="conv_bwd", grid=(B, nS),
        in_specs=[cur(0), nxt, cur(0), nxt, cur(0), cur(1), prv(0), prv(1), _full((CONV_WIDTH, C_CONV)), vec, vec],
        out_specs=[pl.BlockSpec((ts, 2 * C_CONV), lambda b, s: (b * nS + s, 0)), _full((8 * CONV_WIDTH, C_CONV)), vec, vec, vec],
        out_shape=[jax.ShapeDtypeStruct((T, 2 * C_CONV), BF16), jax.ShapeDtypeStruct((8 * CONV_WIDTH, C_CONV), F32),
                   jax.ShapeDtypeStruct((1, C_CONV), F32), jax.ShapeDtypeStruct((1, C_CONV), F32),
                   jax.ShapeDtypeStruct((1, C_CONV), F32)],
        scratch_shapes=[pltpu.VMEM((ts + HALO, C_CONV), F32), pltpu.VMEM((HALO + ts, C_CONV), F32),
                        pltpu.VMEM((7, ts + HALO, C_CONV), F32), pltpu.VMEM((7, HALO + ts, C_CONV), F32),
                        pltpu.VMEM((ts, C_CONV), F32)],
        compiler_params=_cp("arbitrary", "arbitrary"),
    )(dcp, dcp, hc, hc, u, u, u, u, conv_w, ln_g, ln_b)


def _attn_bwd_call(u, do, qg2, kg2, sinks, slopes, B, S):
    T = B * S
    nb = S // BLOCK
    half = Q_W // 2
    npair = GROUP // 2

    def body(sinks_ref, slopes_ref, q_ref, k_ref, v_ref, do_ref, qg_ref, kg_ref,
             dq_ref, dkv_ref, dqg_ref, dkg_ref, dsk_ref, kA, kB, vA, vB, dkn_acc, dv_acc, dkv_keep,
             btab, s_buf, dp_buf, p_buf, ds_buf, qn_buf, do_buf, rq_buf):
        b = pl.program_id(0)
        kvh = pl.program_id(1)
        lo = _lane_lo()
        lane = lax.broadcasted_iota(jnp.int32, (1, 128), 1)
        first = kvh == 0

        @pl.when((b == 0) & first)
        def _():
            dqg_ref[...] = jnp.zeros_like(dqg_ref)
            dkg_ref[...] = jnp.zeros_like(dkg_ref)
            dsk_ref[...] = jnp.zeros_like(dsk_ref)

        _stage_kv(kvh, k_ref, v_ref, kg_ref, lo, kA, kB, vA, vB)
        dkn_acc[...] = jnp.zeros_like(dkn_acc)
        dv_acc[...] = jnp.zeros_like(dv_acc)
        _fill_bias(btab, kvh, slopes_ref)
        qgs = qg_ref[...] * ATTN_SCALE

        def fold(acc):
            canon = jnp.where(lo, acc[0:2 * BLOCK] + pltpu.roll(acc[2 * BLOCK:], HEAD_DIM, 1), 0.0)
            return jnp.where(first, canon, pltpu.roll(canon, HEAD_DIM, 1))

        def block(n, slot):
            s_scr, dp_scr, p_scr, ds_scr = s_buf.at[slot], dp_buf.at[slot], p_buf.at[slot], ds_buf.at[slot]
            qn_scr, do_scr, rq_scr = qn_buf.at[slot], do_buf.at[slot], rq_buf.at[slot]
            r0 = pl.multiple_of(n * BLOCK, BLOCK)
            win = pl.ds(r0, 2 * BLOCK)
            rows = pl.ds(r0, BLOCK)
            k2t = jnp.concatenate([kA[win, :], kB[win, :]], axis=0)
            v2 = jnp.concatenate([vA[win, :], vB[win, :]], axis=0)
            var = jnp.minimum(n, 1)
            for p in range(npair):
                pc = slice(128 * p, 128 * (p + 1))
                qp = q_ref[rows, pc].astype(F32)
                rq = _head_rstd(qp, lo)
                rq_scr[pc, :] = rq
                qn_scr[pc, :] = (qp * rq * qgs).astype(BF16)
                do_scr[pc, :] = do_ref[rows, pc]
            s_scr[...] = _dot_nt(k2t, qn_scr[...])
            dp_scr[...] = _dot_nt(v2, do_scr[...])
            for p in range(npair):
                for hf in range(2):
                    g = 2 * p + hf
                    keys, qs = slice(2 * BLOCK * hf, 2 * BLOCK * (hf + 1)), slice(128 * p, 128 * (p + 1))
                    pn, esk, rz = _softmax_cols(s_scr, btab, p, var, g, hf, sinks_ref[kvh * GROUP + g])
                    dph = dp_scr[keys, qs]
                    delta = jnp.sum(pn * dph, axis=0, keepdims=True)
                    ds_scr[keys, qs] = (pn * (dph - delta)).astype(BF16)
                    p_scr[keys, qs] = pn.astype(BF16)
                    dsk_ref[...] += jnp.where(lane == kvh * GROUP + g, -jnp.sum(esk * rz * delta), 0.0)
            dqs_all = _dot_tn(k2t, ds_scr[...]).T
            for p in range(npair):
                pc = slice(128 * p, 128 * (p + 1))
                qp = q_ref[rows, pc].astype(F32)
                rq = rq_scr[pc, :]
                dqs = dqs_all[pc, :]
                t = dqs * qgs
                dq = rq * t - qp * (rq * rq * rq * (_half_sums(t * qp, lo) * (1.0 / HEAD_DIM)))
                dq_ref[rows, pc] = dq.astype(BF16)
                dqg_ref[...] += jnp.sum(dqs * qp * rq, axis=0, keepdims=True) * ATTN_SCALE
            dkn_acc[win, :] += fold(_dot(ds_scr[...], qn_scr[...]))
            dv_acc[win, :] += fold(_dot(p_scr[...], do_scr[...]))

        def two_blocks(i, carry):
            block(2 * i, 0)
            block(2 * i + 1, 1)
            return carry

        lax.fori_loop(0, nb // 2, two_blocks, 0)

        dkn = dkn_acc[BLOCK:, :]
        kv = k_ref[...].astype(F32)
        kg = kg_ref[...]
        rk = _head_rstd(kv, lo)
        t = dkn * kg
        dk = rk * t - kv * (rk * rk * rk * (_half_sums(t * kv, lo) * (1.0 / HEAD_DIM)))
        dkg_ref[...] += jnp.sum(dkn * kv * rk, axis=0, keepdims=True)
        dv = dv_acc[BLOCK:, :]

        @pl.when(first)
        def _():
            dkv_keep[:, :128] = dk
            dkv_keep[:, 128:] = dv

        @pl.when(kvh == N_KV_HEADS - 1)
        def _():
            dkv_ref[:, :128] = (dkv_keep[:, :128] + dk).astype(BF16)
            dkv_ref[:, 128:] = (dkv_keep[:, 128:] + dv).astype(BF16)

    smem = pl.BlockSpec(memory_space=pltpu.SMEM)
    kvbuf = pltpu.VMEM((S + BLOCK, 128), BF16)
    accbuf = pltpu.VMEM((S + BLOCK, 128), F32)
    vec = _full((1, 128))
    return pl.pallas_call(
        body, name="attn_bwd", grid=(B, N_KV_HEADS),
        in_specs=[smem, smem,
                  pl.BlockSpec((S, half), lambda b, h: (b, COL_Q // half + h)),
                  pl.BlockSpec((S, 128), lambda b, h: (b, COL_K // 128)),
                  pl.BlockSpec((S, 128), lambda b, h: (b, COL_V // 128)),
                  pl.BlockSpec((S, half), lambda b, h: (b, h)),
                  vec, vec],
        out_specs=[pl.BlockSpec((S, half), lambda b, h: (b, h)), pl.BlockSpec((S, 256), lambda b, h: (b, 0)), vec, vec, vec],
        out_shape=[jax.ShapeDtypeStruct((T, Q_W), BF16), jax.ShapeDtypeStruct((T, 2 * KV_W), BF16),
                   jax.ShapeDtypeStruct((1, 128), F32), jax.ShapeDtypeStruct((1, 128), F32), jax.ShapeDtypeStruct((1, 128), F32)],
        scratch_shapes=[kvbuf, kvbuf, kvbuf, kvbuf, accbuf, accbuf, pltpu.VMEM((S, 256), F32),
                        pltpu.VMEM((2, GROUP, 2 * BLOCK, BLOCK), F32),
                        pltpu.VMEM((2, npair * BLOCK, 4 * BLOCK), F32), pltpu.VMEM((2, npair * BLOCK, 4 * BLOCK), F32),
                        pltpu.VMEM((2, npair * BLOCK, 4 * BLOCK), BF16), pltpu.VMEM((2, npair * BLOCK, 4 * BLOCK), BF16),
                        pltpu.VMEM((2, npair * BLOCK, 128), BF16), pltpu.VMEM((2, npair * BLOCK, 128), BF16),
                        pltpu.VMEM((2, npair * BLOCK, 128), F32)],
        compiler_params=_cp("arbitrary", "arbitrary"),
    )(sinks, slopes, u, u, u, do, qg2, kg2)


def _inproj_bwd_call(du_conv, du_g, du_q, du_kv, w, x, g, dh, tm=512):
    T = x.shape[0]

    def body(dc_ref, dg_ref, dq_ref, dkv_ref, w_ref, x_ref, g_ref, dh_ref, gx_ref, dgm_ref):
        dxn = _dot_nt(dc_ref[...], w_ref[:, COL_A:COL_GC])
        dxn += _dot_nt(dg_ref[...], w_ref[:, COL_GC:COL_Q])
        dxn += _dot_nt(dq_ref[...], w_ref[:, COL_Q:COL_K])
        dxn += _dot_nt(dkv_ref[...], w_ref[:, COL_K:])
        dx, dgt = _rms_bwd(dxn, x_ref[...], g_ref[...])
        gx_ref[...] = dh_ref[...] + dx

        @pl.when(pl.program_id(0) == 0)
        def _():
            dgm_ref[...] = jnp.zeros_like(dgm_ref)

        dgm_ref[...] += jnp.sum(dgt, axis=0, keepdims=True)

    row = _rows(tm, D_MODEL)
    return pl.pallas_call(
        body, name="inproj_bwd", grid=(T // tm,),
        in_specs=[_rows(tm, 2 * C_CONV), _rows(tm, 2 * D_MODEL), _rows(tm, Q_W), _rows(tm, 2 * KV_W), _full(w.shape),
                  row, _full((1, D_MODEL)), row],
        out_specs=[row, _full((1, D_MODEL))],
        out_shape=[jax.ShapeDtypeStruct((T, D_MODEL), F32), jax.ShapeDtypeStruct((1, D_MODEL), F32)],
        compiler_params=_cp("arbitrary"),
    )(du_conv, du_g, du_q, du_kv, w, x, g, dh)


def _adamw_math(w, g, m, v):
    m = ADAM_B1 * m + (1.0 - ADAM_B1) * g
    v = ADAM_B2 * v + (1.0 - ADAM_B2) * (g * g)
    m_hat = m / (1.0 - ADAM_B1 ** ADAM_STEP)
    v_hat = v / (1.0 - ADAM_B2 ** ADAM_STEP)
    delta = -ADAM_LR * (m_hat / (jnp.sqrt(v_hat) + ADAM_EPS) + ADAM_WD * w)
    return delta, m, v


def _adamw_sum_call(name, parts, own, w, m, v, tr):
    R, C = w.shape

    def body(p_ref, own_ref, w_ref, m_ref, v_ref, g_ref, d_ref, nm_ref, nv_ref):
        x, y, c = _position()
        me = 4 * x + 2 * y + c
        g = jnp.zeros((tr, C), F32)
        for s in range(N_DEV):
            g = g + jnp.where(me == s, own_ref[...], p_ref[s]).astype(F32)
        g_ref[...] = g
        d_ref[...], nm_ref[...], nv_ref[...] = _adamw_math(w_ref[...], g, m_ref[...], v_ref[...])

    blk = pl.BlockSpec((tr, C), lambda i: (i, 0))
    out = jax.ShapeDtypeStruct((R, C), F32)
    return pl.pallas_call(
        body, name=name, grid=(R // tr,),
        in_specs=[pl.BlockSpec((N_DEV, tr, C), lambda i: (0, i, 0)), blk, blk, blk, blk],
        out_specs=[blk] * 4, out_shape=[out] * 4,
        compiler_params=_cp("parallel"),
    )(parts, own, w, m, v)


def _adamw_small_call(name, g, w, m, v):
    def body(g_ref, w_ref, m_ref, v_ref, d_ref, nm_ref, nv_ref):
        d_ref[...], nm_ref[...], nv_ref[...] = _adamw_math(w_ref[...], g_ref[...], m_ref[...], v_ref[...])

    out = jax.ShapeDtypeStruct(w.shape, F32)
    return pl.pallas_call(body, name=name, out_shape=[out] * 3)(g, w, m, v)


def _position():
    return lax.axis_index("x"), lax.axis_index("y"), lax.axis_index("c")


def _peer(x, y, c, rel):
    fx, fy, fc = (rel >> 2) & 1, (rel >> 1) & 1, rel & 1
    px = 1 - x if fx else x
    py = 1 - y if fy else y
    pc = 1 - c if fc else c
    return (px, py, pc), 4 * px + 2 * py + pc


def _gather_weights_call(shards):
    n = len(shards)

    def body(*refs):
        ins, outs = refs[:n], refs[n:2 * n]
        send_sems, recv_sems, local_sems = refs[2 * n:]
        x, y, c = _position()
        me = 4 * x + 2 * y + c
        sib, sib_id = _peer(x, y, c, 1)
        chips = [_peer(x, y, c, rel) for rel in (4, 2, 6)]

        def copy(i, k, src, slot, to):
            return pltpu.make_async_remote_copy(
                src_ref=src, dst_ref=outs[i].at[slot], send_sem=send_sems.at[i, k], recv_sem=recv_sems.at[i, k],
                device_id=to, device_id_type=MESH)

        mine = [pltpu.make_async_copy(ins[i], outs[i].at[me], local_sems.at[i]) for i in range(n)]
        for cp in mine:
            cp.start()
        first = []
        for i in range(n):
            first.append(copy(i, 0, ins[i], me, sib))
            first += [copy(i, 1 + j, ins[i], me, dev) for j, (dev, _) in enumerate(chips)]
        for cp in first:
            cp.start()
        passed = []
        for i in range(n):
            for j, (dev, slot) in enumerate(chips):
                copy(i, 1 + j, ins[i], slot, dev).wait_recv()
                fwd = copy(i, 4 + j, outs[i].at[slot], slot, sib)
                fwd.start()
                passed.append(fwd)
        for i in range(n):
            copy(i, 0, ins[i], sib_id, sib).wait_recv()
            for j, (_, slot) in enumerate(chips):
                copy(i, 4 + j, ins[i], slot ^ 1, sib).wait_recv()
        for cp in first + passed:
            cp.wait_send()
        for cp in mine:
            cp.wait()

    hbm = pl.BlockSpec(memory_space=pl.ANY)
    return pl.pallas_call(
        body, name="gather_weights",
        in_specs=[hbm] * n, out_specs=[hbm] * n,
        out_shape=[jax.ShapeDtypeStruct((N_DEV,) + s.shape, s.dtype) for s in shards],
        scratch_shapes=[pltpu.SemaphoreType.DMA((n, 7)), pltpu.SemaphoreType.DMA((n, 7)), pltpu.SemaphoreType.DMA((n,))],
    )(*shards)


_HBM = pl.BlockSpec(memory_space=pltpu.HBM)
_SEM = pl.BlockSpec(memory_space=pltpu.SEMAPHORE)
_EFFECT = pltpu.SideEffectType.DATAFLOW_SIDE_EFFECTING


def _exchange_copies(scatter, incoming, src_refs, land_refs, send_sems, recv_sems):
    x, y, c = _position()
    me = 4 * x + 2 * y + c
    out = []
    for i, (src, land) in enumerate(zip(src_refs, land_refs)):
        for k in range(N_DEV - 1):
            dev, dev_id = _peer(x, y, c, k + 1)
            j = i * (N_DEV - 1) + k
            out.append(pltpu.make_async_remote_copy(
                src_ref=src.at[dev_id] if scatter else src, dst_ref=land.at[dev_id if incoming else me],
                send_sem=send_sems.at[j], recv_sem=recv_sems.at[j], device_id=dev, device_id_type=MESH))
    return out


def _exchange_start_call(name, arrays, scatter):
    n = len(arrays)
    lands = [lax.empty((N_DEV,) + (a.shape[1:] if scatter else a.shape), a.dtype) for a in arrays]

    def body(*refs):
        src_refs, land_refs = refs[:n], refs[n:2 * n]
        send_sems, recv_sems, token = refs[2 * n], refs[2 * n + 1], refs[-1]
        for copy in _exchange_copies(scatter, False, src_refs, land_refs, send_sems, recv_sems):
            copy.start()
        token[...] = jnp.zeros_like(token)

    sems = pltpu.SemaphoreType.DMA((n * (N_DEV - 1),))
    res = pl.pallas_call(
        body, name=name,
        out_shape=[sems, sems] + [pltpu.HBM(a.shape, a.dtype) for a in arrays] + [pltpu.HBM(l.shape, l.dtype) for l in lands]
        + [jax.ShapeDtypeStruct((8, 128), F32)],
        in_specs=[_HBM] * (2 * n), out_specs=[_SEM, _SEM] + [_HBM] * (2 * n) + [pl.BlockSpec(memory_space=pltpu.VMEM)],
        input_output_aliases={i: 2 + i for i in range(2 * n)},
        compiler_params=pltpu.CompilerParams(has_side_effects=_EFFECT),
    )(*[pltpu.with_memory_space_constraint(a, pltpu.HBM) for a in list(arrays) + lands])
    return res[0], res[1], res[2:2 + n], res[2 + n:2 + 2 * n], res[-1]


def _exchange_wait_call(name, started, scatter, after):
    send_sems, recv_sems, srcs, lands, _ = started
    n = len(srcs)

    def body(*refs):
        src_refs, land_refs = refs[:n], refs[n:2 * n]
        send_s, recv_s = refs[2 * n], refs[2 * n + 1]
        for copy in _exchange_copies(scatter, False, src_refs, land_refs, send_s, recv_s):
            copy.wait_send()
        for copy in _exchange_copies(scatter, True, src_refs, land_refs, send_s, recv_s):
            copy.wait_recv()

    res = pl.pallas_call(
        body, name=name,
        out_shape=[pltpu.HBM(a.shape, a.dtype) for a in list(srcs) + list(lands)],
        in_specs=[_HBM] * (2 * n) + [_SEM, _SEM, pl.BlockSpec(memory_space=pl.ANY)], out_specs=[_HBM] * (2 * n),
        input_output_aliases={i: i for i in range(2 * n)},
        compiler_params=pltpu.CompilerParams(has_side_effects=_EFFECT),
    )(*srcs, *lands, send_sems, recv_sems, after)
    return res[n:]


def _allsum_small_call(part):
    P = part.shape[0]

    def body(p_ref, o_ref, all_ref, send_sems, recv_sems):
        x, y, c = _position()
        me = 4 * x + 2 * y + c
        peers = [_peer(x, y, c, rel) for rel in range(1, N_DEV)]
        all_ref[me] = p_ref[...]

        def copy(k, slot, dev):
            return pltpu.make_async_remote_copy(
                src_ref=p_ref, dst_ref=all_ref.at[slot], send_sem=send_sems.at[k], recv_sem=recv_sems.at[k],
                device_id=dev, device_id_type=MESH)

        sends = [copy(k, me, dev) for k, (dev, _) in enumerate(peers)]
        for cp in sends:
            cp.start()
        for k, (dev, dev_id) in enumerate(peers):
            copy(k, dev_id, dev).wait_recv()
        for cp in sends:
            cp.wait_send()
        tot = all_ref[0]
        for s in range(1, N_DEV):
            tot = tot + all_ref[s]
        o_ref[...] = tot

    vm = pl.BlockSpec(memory_space=pltpu.VMEM)
    return pl.pallas_call(
        body, name="allsum_small", in_specs=[vm], out_specs=vm,
        out_shape=jax.ShapeDtypeStruct(part.shape, F32),
        scratch_shapes=[pltpu.VMEM((N_DEV, P, 128), F32), pltpu.SemaphoreType.DMA((7,)), pltpu.SemaphoreType.DMA((7,))],
    )(part)


def _pad_rows(v, rows):
    flat = v.reshape(-1)
    return jnp.pad(flat, (0, rows * 128 - flat.shape[0])).reshape(rows, 128)


def _blocks_by_cols(w):
    K, N = w.shape
    return w.reshape(K, N_DEV, N // N_DEV).transpose(1, 0, 2)


def _cols_from_blocks(wb):
    n, K, C = wb.shape
    return wb.transpose(1, 0, 2).reshape(K, n * C)


def _local_step(x, tgt, w_in_i, conv_w, conv_b, ln_g, ln_b, qg, kg, sinks, g_mix, g_ffn, token0, rest_weights, grads_ready, B, S):
    row = lambda v: v.reshape(1, -1)
    qg2 = jnp.tile(qg, 2).reshape(1, 128)
    kg2 = jnp.tile(kg, 2).reshape(1, 128)
    heads = jnp.arange(1, N_Q_HEADS + 1, dtype=F32)
    slopes = jnp.exp2(-8.0 * heads / N_Q_HEADS)

    u, xn = _inproj_call(x, row(g_mix) + token0, w_in_i)
    hc, c = _conv_fwd_call(u, conv_w, row(conv_b), row(ln_g), row(ln_b), B, S)
    o = _attn_fwd_call(u, qg2, kg2, sinks, slopes, B, S)
    wc, wa, wm, w_ffn_in, wd = rest_weights(o)
    yc, ya, z, h, hn = _merge_call(c, o, u, x, wc, wa, wm, row(g_ffn))
    gu, act = _ffn_in_call(hn, w_ffn_in)
    sse, dout, dout_b = _ffn_out_call(act, wd, h, tgt)

    dgu = _ffn_down_bwd_call(dout_b, wd, gu)
    g_wd = _dw_call("dw_ffn_down", act, dout_b, tn=512)
    g_wffn = _dw_call("dw_ffn_in", hn, dgu, tn=2 * D_FF // 4)
    tok = grads_ready(dict(w_ffn_down=g_wd, w_ffn_in=g_wffn))
    dh, dh_b, g_gffn = _ffn_in_bwd_call(dgu, w_ffn_in, h, row(g_ffn) + tok, dout)
    dyc, dya, du_g, dcp, do = _merge_bwd_call(dh_b, wm, wc, wa, u, yc, ya)
    g_wm = _dw_call("dw_merge", z, dh_b, tn=D_MODEL)
    g_wc = _dw_call("dw_conv_out", c, dyc, tn=D_MODEL)
    g_wa = _dw_call("dw_attn_out", o, dya, tn=D_MODEL)
    tok = grads_ready(dict(w_merge_out=g_wm, w_conv_out=g_wc, w_attn_out=g_wa))
    du_conv, g_cw8, g_cb, g_lg, g_lb = _conv_bwd_call(dcp, hc, u, conv_w, row(ln_g) + tok, row(ln_b), B, S)
    du_q, du_kv, g_qg2, g_kg2, g_sk = _attn_bwd_call(u, do, qg2, kg2, sinks, slopes, B, S)
    g_win = jnp.concatenate([
        _dw_call("dw_in_conv", xn, du_conv, tn=D_MODEL),
        _dw_call("dw_in_q", xn, du_q, tn=D_MODEL),
        _dw_call("dw_in_kv", xn, du_kv, tn=2 * KV_W),
        _dw_call("dw_in_gates", xn, du_g, tn=D_MODEL)], axis=1)
    tok = grads_ready(dict(w_in=g_win))
    grad_x, g_gmix = _inproj_bwd_call(du_conv, du_g, du_q, du_kv, w_in_i, x, row(g_mix) + tok, dh)

    g_cw = g_cw8.reshape(CONV_WIDTH, 8, C_CONV).sum(axis=1)
    g_qg = g_qg2[0, :HEAD_DIM] + g_qg2[0, HEAD_DIM:]
    g_kg = g_kg2[0, :HEAD_DIM] + g_kg2[0, HEAD_DIM:]
    small = dict(norm_mix_g=g_gmix[0], conv_dw_b=g_cb[0], conv_ln_g=g_lg[0], conv_ln_b=g_lb[0], q_norm_g=g_qg,
                 k_norm_g=g_kg, sinks=g_sk[0, :N_Q_HEADS], norm_ffn_g=g_gffn[0], conv_dw_w=g_cw)
    return sse[0, 0], grad_x, small


def _to_internal_cols(w):
    o1 = 2 * C_CONV
    o4 = o1 + Q_W + 2 * KV_W
    return jnp.concatenate([w[:, :o1], w[:, o4:], w[:, o1:o4]], axis=1)


_COL_SHARDED = ("w_in", "w_ffn_in")

_SMALL = (("norm_mix_g", 8), ("conv_dw_b", 8), ("conv_ln_g", 8), ("conv_ln_b", 8), ("norm_ffn_g", 8),
          ("q_norm_g", 1), ("k_norm_g", 1), ("sinks", 1))
_SMALL_ROWS = 48
_CW_ROWS = CONV_WIDTH * C_CONV // 128


def _pack_small(d):
    rows = [_pad_rows(d[name], r) for name, r in _SMALL]
    used = sum(r for _, r in _SMALL)
    return jnp.concatenate(rows + [jnp.zeros((_SMALL_ROWS - used, 128), F32)], axis=0)


def _unpack_small(packed, like):
    out, r0 = {}, 0
    for name, r in _SMALL:
        n = like[name].size
        out[name] = packed[r0:r0 + r].reshape(-1)[:n].reshape(like[name].shape)
        r0 += r
    return out


def kernel(x, norm_mix_g, w_in, conv_dw_w, conv_dw_b, conv_ln_g, conv_ln_b, w_conv_out, q_norm_g, k_norm_g, sinks, w_attn_out, w_merge_out, norm_ffn_g, w_ffn_in, w_ffn_down, loss_target, m_norm_mix_g, m_w_in, m_conv_dw_w, m_conv_dw_b, m_conv_ln_g, m_conv_ln_b, m_w_conv_out, m_q_norm_g, m_k_norm_g, m_sinks, m_w_attn_out, m_w_merge_out, m_norm_ffn_g, m_w_ffn_in, m_w_ffn_down, v_norm_mix_g, v_w_in, v_conv_dw_w, v_conv_dw_b, v_conv_ln_g, v_conv_ln_b, v_w_conv_out, v_q_norm_g, v_k_norm_g, v_sinks, v_w_attn_out, v_w_merge_out, v_norm_ffn_g, v_w_ffn_in, v_w_ffn_down):
    names = ["norm_mix_g", "w_in", "conv_dw_w", "conv_dw_b", "conv_ln_g", "conv_ln_b", "w_conv_out", "q_norm_g",
             "k_norm_g", "sinks", "w_attn_out", "w_merge_out", "norm_ffn_g", "w_ffn_in", "w_ffn_down"]
    w = dict(zip(names, (norm_mix_g, w_in, conv_dw_w, conv_dw_b, conv_ln_g, conv_ln_b, w_conv_out, q_norm_g, k_norm_g,
                         sinks, w_attn_out, w_merge_out, norm_ffn_g, w_ffn_in, w_ffn_down)))
    m = dict(zip(names, (m_norm_mix_g, m_w_in, m_conv_dw_w, m_conv_dw_b, m_conv_ln_g, m_conv_ln_b, m_w_conv_out,
                         m_q_norm_g, m_k_norm_g, m_sinks, m_w_attn_out, m_w_merge_out, m_norm_ffn_g, m_w_ffn_in,
                         m_w_ffn_down)))
    v = dict(zip(names, (v_norm_mix_g, v_w_in, v_conv_dw_w, v_conv_dw_b, v_conv_ln_g, v_conv_ln_b, v_w_conv_out,
                         v_q_norm_g, v_k_norm_g, v_sinks, v_w_attn_out, v_w_merge_out, v_norm_ffn_g, v_w_ffn_in,
                         v_w_ffn_down)))
    B, S, _ = x.shape
    T = B * S
    me = 4 * lax.axis_index("x") + 2 * lax.axis_index("y") + lax.axis_index("c")

    cw_shard = _pad_rows(conv_dw_w.reshape(CONV_WIDTH, 128), 32)
    g_win, g_cwt = _gather_weights_call([w_in.astype(BF16), cw_shard])
    conv_w_full = g_cwt.reshape(N_DEV, 32 * 128)[:, :CONV_WIDTH * 128].reshape(N_DEV, CONV_WIDTH, 128)
    conv_w_full = conv_w_full.transpose(1, 0, 2).reshape(CONV_WIDTH, C_CONV)
    w_in_i = _to_internal_cols(_cols_from_blocks(g_win))
    rest_names = ["w_conv_out", "w_attn_out", "w_merge_out", "w_ffn_in", "w_ffn_down"]
    rest_shards = [w[k].astype(BF16) for k in rest_names]
    rest_started = _exchange_start_call("gather_rest_start", rest_shards, scatter=False)

    def rest_weights(after):
        lands = _exchange_wait_call("gather_rest_wait", rest_started, False, after)
        mine = lax.broadcasted_iota(jnp.int32, (N_DEV, 1, 1), 0) == me
        full = [jnp.where(mine, s[None], l) for l, s in zip(lands, rest_shards)]
        wc_f, wa_f, wm_f = (f.reshape(D_MODEL, D_MODEL) for f in full[:3])
        return wc_f, wa_f, wm_f, _cols_from_blocks(full[3]), full[4].reshape(D_FF, D_MODEL)

    exchanges = []

    def grads_ready(group):
        ks = list(group)
        blocks = [_blocks_by_cols(group[k]) if k in _COL_SHARDED
                  else group[k].reshape(N_DEV, group[k].shape[0] // N_DEV, group[k].shape[1]) for k in ks]
        started = _exchange_start_call("scatter_" + ks[0] + "_start", blocks, scatter=True)
        exchanges.append((ks, started, [lax.dynamic_index_in_dim(b, me, 0, keepdims=False) for b in blocks]))
        return started[-1][0:1, 0:1]

    sse, grad_x, small = _local_step(
        x.reshape(T, D_MODEL), loss_target.reshape(T, D_MODEL), w_in_i, conv_w_full, conv_dw_b, conv_ln_g, conv_ln_b,
        q_norm_g, k_norm_g, sinks, norm_mix_g, norm_ffn_g, rest_started[-1][0:1, 0:1], rest_weights, grads_ready, B, S)

    grad, delta, new_m, new_v = {}, {}, {}, {}
    packed = jnp.concatenate([_pack_small(small), small["conv_dw_w"].reshape(_CW_ROWS, 128),
                              jnp.full((8, 128), sse, F32)], axis=0)
    tot = _allsum_small_call(packed)
    loss = 0.5 / D_MODEL * tot[_SMALL_ROWS + _CW_ROWS, 0]
    g_small = _unpack_small(tot[:_SMALL_ROWS], w)
    d_s, m_s, v_s = _adamw_small_call("adamw_small", tot[:_SMALL_ROWS], _pack_small(w), _pack_small(m), _pack_small(v))
    d_small, m_small, v_small = _unpack_small(d_s, w), _unpack_small(m_s, w), _unpack_small(v_s, w)
    g_cw = lax.dynamic_slice_in_dim(tot[_SMALL_ROWS:_SMALL_ROWS + _CW_ROWS].reshape(CONV_WIDTH, C_CONV), me * 128, 128, axis=1)
    cw2 = lambda t: t.reshape(CONV_WIDTH, 128)
    d_cw, m_cw, v_cw = _adamw_small_call("adamw_conv_w", g_cw, cw2(conv_dw_w), cw2(m_conv_dw_w), cw2(v_conv_dw_w))
    cw3 = lambda t: t.reshape(CONV_WIDTH, 1, 128)
    for k, _ in _SMALL:
        grad[k], delta[k], new_m[k], new_v[k] = g_small[k], d_small[k], m_small[k], v_small[k]
    grad["conv_dw_w"], delta["conv_dw_w"], new_m["conv_dw_w"], new_v["conv_dw_w"] = cw3(g_cw), cw3(d_cw), cw3(m_cw), cw3(v_cw)

    after = d_cw
    for ks, started, owns in exchanges:
        lands = _exchange_wait_call("scatter_" + ks[0] + "_wait", started, True, after)
        for k, land, own in zip(ks, lands, owns):
            tr = {1024: 256, 352: 176, 128: 128}[w[k].shape[0]]
            grad[k], delta[k], new_m[k], new_v[k] = _adamw_sum_call("adamw_" + k, land, own, w[k], m[k], v[k], tr)
            after = new_v[k]

    return (loss, grad_x.reshape(B, S, D_MODEL), *[grad[k] for k in names], *[delta[k] for k in names],
            *[new_m[k] for k in names], *[new_v[k] for k in names])
```

```python
import functools
import math

import jax
import jax.numpy as jnp
from jax import lax
from jax.experimental import pallas as pl
from jax.experimental.pallas import tpu as pltpu

F32 = jnp.float32
BF16 = jnp.bfloat16
MESH = pl.DeviceIdType.MESH

N_DEV = 8
D_MODEL = 1024
C_CONV = 1024
CONV_WIDTH = 31
HEAD_DIM = 64
N_Q_HEADS = 16
N_KV_HEADS = 2
GROUP = N_Q_HEADS // N_KV_HEADS
WINDOW = 128
BLOCK = 128
D_FF = 2816
EPS = 1e-6
NEG = -1e30
ATTN_SCALE = 1.0 / math.sqrt(HEAD_DIM)
Q_W = N_Q_HEADS * HEAD_DIM
KV_W = N_KV_HEADS * HEAD_DIM
IN_COLS = 2 * C_CONV + Q_W + 2 * KV_W + 2 * D_MODEL

ADAM_LR = 0.001
ADAM_B1 = 0.9
ADAM_B2 = 0.999
ADAM_EPS = 1e-08
ADAM_WD = 0.01
ADAM_STEP = 10

COL_A, COL_GATE, COL_GC, COL_GA, COL_Q, COL_K, COL_V = 0, 1024, 2048, 3072, 4096, 5120, 5248

HALO = 32
VMEM_LIMIT = 56 * 1024 * 1024


def _cp(*sem):
    return pltpu.CompilerParams(dimension_semantics=sem, vmem_limit_bytes=VMEM_LIMIT)


def _sigmoid(x):
    return jax.nn.sigmoid(x)


def _dot(a, b):
    return jnp.dot(a, b, preferred_element_type=F32)


def _dot_nt(a, b):
    return lax.dot_general(a, b, (((1,), (1,)), ((), ())), preferred_element_type=F32)


def _dot_tn(a, b):
    return lax.dot_general(a, b, (((0,), (0,)), ((), ())), preferred_element_type=F32)


def _full(shape):
    return pl.BlockSpec(shape, lambda *_: (0,) * len(shape))


def _rows(tm, ncols, colblk=0):
    return pl.BlockSpec((tm, ncols), lambda i: (i, colblk))


def _inproj_call(x, g, w, tm=512):
    T = x.shape[0]
    N = w.shape[1]

    def body(x_ref, g_ref, w_ref, u_ref, xn_ref):
        xv = x_ref[...]
        r = lax.rsqrt(jnp.mean(xv * xv, axis=-1, keepdims=True) + EPS)
        xn = (xv * r * g_ref[...]).astype(BF16)
        xn_ref[...] = xn
        u_ref[...] = _dot(xn, w_ref[...]).astype(BF16)

    return pl.pallas_call(
        body, name="inproj", grid=(T // tm,),
        in_specs=[_rows(tm, D_MODEL), _full((1, D_MODEL)), _full(w.shape)],
        out_specs=[_rows(tm, N), _rows(tm, D_MODEL)],
        out_shape=[jax.ShapeDtypeStruct((T, N), BF16), jax.ShapeDtypeStruct((T, D_MODEL), BF16)],
        compiler_params=_cp("parallel"),
    )(x, g, w)


def _fill_shifted(src_ref, sh_ref):
    n = src_ref.shape[0] - 8
    for s in range(1, 8):
        sh_ref[s - 1, 0:n, :] = src_ref[s:s + n, :]


def _window(src_ref, sh_ref, off, cols):
    s = off % 8
    if s == 0:
        return src_ref[off:off + 128, cols]
    return sh_ref[s - 1, off - s:off - s + 128, cols]


def _conv_taps(w_ref, src_ref, sh_ref, base, rows, cb, reverse):
    cols = pl.ds(pl.multiple_of(cb * 128, 128), 128)
    outs = []
    for rb in range(rows // 128):
        acc = jnp.zeros((128, 128), F32)
        for j in range(CONV_WIDTH):
            off = base + (CONV_WIDTH - 1 - j if reverse else j) + rb * 128
            acc = acc + w_ref[j:j + 1, cols] * _window(src_ref, sh_ref, off, cols)
        outs.append(acc)
    return outs


def _conv_fwd_call(u, conv_w, conv_b, ln_g, ln_b, B, S, ts=256):
    T = B * S
    nS = S // ts
    per32 = ts // HALO

    def body(a_ref, gate_ref, ap_ref, gp_ref, w_ref, b_ref, g_ref, bb_ref, hc_ref, c_ref, hext, hsh, acc_ref):
        s = pl.program_id(1)
        prev = ap_ref[...].astype(F32) * _sigmoid(gp_ref[...].astype(F32))
        hext[0:HALO, :] = jnp.where(s > 0, prev, 0.0)
        hext[HALO:, :] = a_ref[...].astype(F32) * _sigmoid(gate_ref[...].astype(F32))
        _fill_shifted(hext, hsh)

        def colblock(cb, carry):
            cols = pl.ds(pl.multiple_of(cb * 128, 128), 128)
            outs = _conv_taps(w_ref, hext, hsh, HALO - (CONV_WIDTH - 1), ts, cb, reverse=False)
            for rb, acc in enumerate(outs):
                acc_ref[rb * 128:(rb + 1) * 128, cols] = acc
            return carry

        lax.fori_loop(0, C_CONV // 128, colblock, 0)
        hc = acc_ref[...] + b_ref[...]
        hc_ref[...] = hc
        mu = jnp.mean(hc, axis=-1, keepdims=True)
        xc = hc - mu
        var = jnp.mean(xc * xc, axis=-1, keepdims=True)
        y = xc * lax.rsqrt(var + EPS) * g_ref[...] + bb_ref[...]
        c_ref[...] = (y * _sigmoid(y)).astype(BF16)

    cur = lambda cb: pl.BlockSpec((ts, C_CONV), lambda b, s: (b * nS + s, cb))
    prv = lambda cb: pl.BlockSpec((HALO, C_CONV), lambda b, s: (jnp.maximum(b * (S // HALO) + s * per32 - 1, 0), cb))
    vec = _full((1, C_CONV))
    return pl.pallas_call(
        body, name="conv_fwd", grid=(B, nS),
        in_specs=[cur(0), cur(1), prv(0), prv(1), _full((CONV_WIDTH, C_CONV)), vec, vec, vec],
        out_specs=[pl.BlockSpec((ts, C_CONV), lambda b, s: (b * nS + s, 0))] * 2,
        out_shape=[jax.ShapeDtypeStruct((T, C_CONV), F32), jax.ShapeDtypeStruct((T, C_CONV), BF16)],
        scratch_shapes=[pltpu.VMEM((HALO + ts, C_CONV), F32), pltpu.VMEM((7, HALO + ts, C_CONV), F32),
                        pltpu.VMEM((ts, C_CONV), F32)],
        compiler_params=_cp("parallel", "arbitrary"),
    )(u, u, u, u, conv_w, conv_b, ln_g, ln_b)


def _lane_lo():
    return lax.broadcasted_iota(jnp.int32, (1, 128), 1) < HEAD_DIM


def _half_sums(t, lo):
    s_lo = jnp.sum(jnp.where(lo, t, 0.0), axis=-1, keepdims=True)
    s_hi = jnp.sum(jnp.where(lo, 0.0, t), axis=-1, keepdims=True)
    return jnp.where(lo, s_lo, s_hi)


def _head_rstd(t, lo):
    return lax.rsqrt(_half_sums(t * t, lo) * (1.0 / HEAD_DIM) + EPS)


def _fill_bias(btab, kvh, slopes_ref):
    sj = lax.broadcasted_iota(jnp.int32, (2 * BLOCK, BLOCK), 0)
    qi = lax.broadcasted_iota(jnp.int32, (2 * BLOCK, BLOCK), 1)
    dist = qi + BLOCK - sj
    valid = (dist >= 0) & (dist < WINDOW)
    distf = dist.astype(F32)
    for g in range(GROUP):
        bias = jnp.where(valid, -slopes_ref[kvh * GROUP + g] * distf, NEG)
        btab[1, g] = bias
        btab[0, g] = jnp.where(sj >= BLOCK, bias, NEG)


def _softmax_cols(st_ref, btab, p, var, g, hf, sk):
    sh = st_ref[2 * BLOCK * hf:2 * BLOCK * (hf + 1), 128 * p:128 * (p + 1)] + btab[var, g]
    m = jnp.maximum(jnp.max(sh, axis=0, keepdims=True), sk)
    e = jnp.exp(sh - m)
    esk = jnp.exp(sk - m)
    rz = 1.0 / (jnp.sum(e, axis=0, keepdims=True) + esk)
    return e * rz, esk, rz


def _stage_kv(kvh, k_ref, v_ref, kg_ref, lo, kA, kB, vA, vB):
    kv = k_ref[...].astype(F32)
    rk = _head_rstd(kv, lo)
    kn = kv * rk * kg_ref[...]
    mine = lax.broadcasted_iota(jnp.int32, (1, 128), 1) // HEAD_DIM == kvh
    ksel = jnp.where(mine, kn, 0.0)
    vsel = jnp.where(mine, v_ref[...].astype(F32), 0.0)
    krol = pltpu.roll(ksel, HEAD_DIM, 1)
    vrol = pltpu.roll(vsel, HEAD_DIM, 1)
    first = kvh == 0
    zeros = jnp.zeros((BLOCK, 128), BF16)
    for ref, val in ((kA, jnp.where(first, ksel, krol)), (kB, jnp.where(first, krol, ksel)),
                     (vA, jnp.where(first, vsel, vrol)), (vB, jnp.where(first, vrol, vsel))):
        ref[0:BLOCK, :] = zeros
        ref[BLOCK:, :] = val.astype(BF16)


def _attn_fwd_call(u, qg2, kg2, sinks, slopes, B, S):
    T = B * S
    nb = S // BLOCK
    half = Q_W // 2
    npair = GROUP // 2

    def body(sinks_ref, slopes_ref, q_ref, k_ref, v_ref, qg_ref, kg_ref, o_ref, kA, kB, vA, vB, btab, qn_buf, s_buf, p_buf):
        kvh = pl.program_id(1)
        lo = _lane_lo()
        _stage_kv(kvh, k_ref, v_ref, kg_ref, lo, kA, kB, vA, vB)
        _fill_bias(btab, kvh, slopes_ref)
        qgs = qg_ref[...] * ATTN_SCALE

        def block(n, slot):
            qn_scr, s_scr, p_scr = qn_buf.at[slot], s_buf.at[slot], p_buf.at[slot]
            r0 = pl.multiple_of(n * BLOCK, BLOCK)
            win = pl.ds(r0, 2 * BLOCK)
            k2t = jnp.concatenate([kA[win, :], kB[win, :]], axis=0)
            v2 = jnp.concatenate([vA[win, :], vB[win, :]], axis=0)
            var = jnp.minimum(n, 1)
            for p in range(npair):
                pc = slice(128 * p, 128 * (p + 1))
                qp = q_ref[pl.ds(r0, BLOCK), pc].astype(F32)
                qn_scr[pc, :] = (qp * _head_rstd(qp, lo) * qgs).astype(BF16)
            s_scr[...] = _dot_nt(k2t, qn_scr[...])
            for p in range(npair):
                for hf in range(2):
                    g = 2 * p + hf
                    pn, _, _ = _softmax_cols(s_scr, btab, p, var, g, hf, sinks_ref[kvh * GROUP + g])
                    p_scr[2 * BLOCK * hf:2 * BLOCK * (hf + 1), 128 * p:128 * (p + 1)] = pn.astype(BF16)
            o_all = _dot_tn(v2, p_scr[...]).T
            for p in range(npair):
                pc = slice(128 * p, 128 * (p + 1))
                o_ref[pl.ds(r0, BLOCK), pc] = o_all[pc, :].astype(BF16)

        def two_blocks(i, carry):
            block(2 * i, 0)
            block(2 * i + 1, 1)
            return carry

        lax.fori_loop(0, nb // 2, two_blocks, 0)

    smem = pl.BlockSpec(memory_space=pltpu.SMEM)
    kvbuf = pltpu.VMEM((S + BLOCK, 128), BF16)
    pair_scores = (2, npair * BLOCK, 4 * BLOCK)
    return pl.pallas_call(
        body, name="attn_fwd", grid=(B, N_KV_HEADS),
        in_specs=[smem, smem,
                  pl.BlockSpec((S, half), lambda b, h: (b, COL_Q // half + h)),
                  pl.BlockSpec((S, 128), lambda b, h: (b, COL_K // 128)),
                  pl.BlockSpec((S, 128), lambda b, h: (b, COL_V // 128)),
                  _full((1, 128)), _full((1, 128))],
        out_specs=pl.BlockSpec((S, half), lambda b, h: (b, h)),
        out_shape=jax.ShapeDtypeStruct((T, Q_W), BF16),
        scratch_shapes=[kvbuf, kvbuf, kvbuf, kvbuf, pltpu.VMEM((2, GROUP, 2 * BLOCK, BLOCK), F32),
                        pltpu.VMEM((2, npair * BLOCK, 128), BF16), pltpu.VMEM(pair_scores, F32), pltpu.VMEM(pair_scores, BF16)],
        compiler_params=_cp("parallel", "arbitrary"),
    )(sinks, slopes, u, u, u, qg2, kg2)


def _merge_call(c, o, u, x, wc, wa, wm, g_ffn, tm=512):
    T = x.shape[0]

    def body(c_ref, o_ref, gc_ref, ga_ref, x_ref, wc_ref, wa_ref, wm_ref, g_ref, yc_ref, ya_ref, z_ref, h_ref, hn_ref):
        yc = _dot(c_ref[...], wc_ref[...])
        ya = _dot(o_ref[...], wa_ref[...])
        yc_ref[...] = yc.astype(BF16)
        ya_ref[...] = ya.astype(BF16)
        z = (_sigmoid(gc_ref[...].astype(F32)) * yc + _sigmoid(ga_ref[...].astype(F32)) * ya).astype(BF16)
        z_ref[...] = z
        h = x_ref[...] + _dot(z, wm_ref[...])
        h_ref[...] = h
        r = lax.rsqrt(jnp.mean(h * h, axis=-1, keepdims=True) + EPS)
        hn_ref[...] = (h * r * g_ref[...]).astype(BF16)

    sq = _full((D_MODEL, D_MODEL))
    row = _rows(tm, D_MODEL)
    f32 = jax.ShapeDtypeStruct((T, D_MODEL), F32)
    b16 = jax.ShapeDtypeStruct((T, D_MODEL), BF16)
    return pl.pallas_call(
        body, name="merge_fwd", grid=(T // tm,),
        in_specs=[row, row, _rows(tm, D_MODEL, COL_GC // D_MODEL), _rows(tm, D_MODEL, COL_GA // D_MODEL), row,
                  sq, sq, sq, _full((1, D_MODEL))],
        out_specs=[row] * 5,
        out_shape=[b16, b16, b16, f32, b16],
        compiler_params=_cp("parallel"),
    )(c, o, u, u, x, wc, wa, wm, g_ffn)


def _ffn_in_call(hn, w, tm=512):
    T = hn.shape[0]

    def body(hn_ref, w_ref, fac_ref, act_ref):
        gu = _dot(hn_ref[...], w_ref[...])
        g = gu[:, :D_FF]
        up = gu[:, D_FF:]
        sg = _sigmoid(g)
        silu = g * sg
        act_ref[...] = (silu * up).astype(BF16)
        fac_ref[:, :D_FF] = (up * (sg * (1.0 + g * (1.0 - sg)))).astype(BF16)
        fac_ref[:, D_FF:] = silu.astype(BF16)

    return pl.pallas_call(
        body, name="ffn_in", grid=(T // tm,),
        in_specs=[_rows(tm, D_MODEL), _full(w.shape)],
        out_specs=[_rows(tm, 2 * D_FF), _rows(tm, D_FF)],
        out_shape=[jax.ShapeDtypeStruct((T, 2 * D_FF), BF16), jax.ShapeDtypeStruct((T, D_FF), BF16)],
        compiler_params=_cp("parallel"),
    )(hn, w)


def _ffn_out_call(act, wd, h, tgt, tm=512):
    T = h.shape[0]

    def body(act_ref, wd_ref, h_ref, t_ref, sse_ref, d_ref, db_ref):
        e = h_ref[...] + _dot(act_ref[...], wd_ref[...]) - t_ref[...]

        @pl.when(pl.program_id(0) == 0)
        def _():
            sse_ref[...] = jnp.zeros_like(sse_ref)

        sse_ref[...] += jnp.sum(e * e)
        d = e * (1.0 / D_MODEL)
        d_ref[...] = d
        db_ref[...] = d.astype(BF16)

    row = _rows(tm, D_MODEL)
    return pl.pallas_call(
        body, name="ffn_out", grid=(T // tm,),
        in_specs=[_rows(tm, D_FF), _full(wd.shape), row, row],
        out_specs=[_full((8, 128)), row, row],
        out_shape=[jax.ShapeDtypeStruct((8, 128), F32), jax.ShapeDtypeStruct((T, D_MODEL), F32),
                   jax.ShapeDtypeStruct((T, D_MODEL), BF16)],
        compiler_params=_cp("arbitrary"),
    )(act, wd, h, tgt)


def _dw_call(name, a, b, tn, tk=2048):
    T, K = a.shape
    N = b.shape[1]
    tk = min(tk, T)
    nk = T // tk

    def body(a_ref, b_ref, o_ref, acc_ref):
        k = pl.program_id(1)

        @pl.when(k == 0)
        def _():
            acc_ref[...] = jnp.zeros_like(acc_ref)

        acc_ref[...] += _dot_tn(a_ref[...], b_ref[...])

        @pl.when(k == nk - 1)
        def _():
            o_ref[...] = acc_ref[...].astype(BF16)

    return pl.pallas_call(
        body, name=name, grid=(N // tn, nk),
        in_specs=[pl.BlockSpec((tk, K), lambda j, k: (k, 0)), pl.BlockSpec((tk, tn), lambda j, k: (k, j))],
        out_specs=pl.BlockSpec((K, tn), lambda j, k: (0, j)),
        out_shape=jax.ShapeDtypeStruct((K, N), BF16),
        scratch_shapes=[pltpu.VMEM((K, tn), F32)],
        compiler_params=_cp("parallel", "arbitrary"),
    )(a, b)


def _rms_bwd(dy, xv, g):
    r = lax.rsqrt(jnp.mean(xv * xv, axis=-1, keepdims=True) + EPS)
    t = dy * g
    dx = r * t - xv * (r * r * r * jnp.mean(t * xv, axis=-1, keepdims=True))
    return dx, dy * xv * r


def _ffn_down_bwd_call(dout_b, wd, gu, tm=512):
    T = dout_b.shape[0]

    def body(d_ref, wd_ref, fg_ref, fu_ref, o_ref):
        dact = _dot_nt(d_ref[...], wd_ref[...])
        o_ref[:, :D_FF] = (dact * fg_ref[...].astype(F32)).astype(BF16)
        o_ref[:, D_FF:] = (dact * fu_ref[...].astype(F32)).astype(BF16)

    return pl.pallas_call(
        body, name="ffn_down_bwd", grid=(T // tm,),
        in_specs=[_rows(tm, D_MODEL), _full(wd.shape), _rows(tm, D_FF, 0), _rows(tm, D_FF, 1)],
        out_specs=_rows(tm, 2 * D_FF),
        out_shape=jax.ShapeDtypeStruct((T, 2 * D_FF), BF16),
        compiler_params=_cp("parallel"),
    )(dout_b, wd, gu, gu)


def _ffn_in_bwd_call(dgu, w, h, g_ffn, dout, tm=512):
    T = h.shape[0]

    def body(dgu_ref, w_ref, h_ref, g_ref, d_ref, dh_ref, dhb_ref, dg_ref):
        dhn = _dot_nt(dgu_ref[...], w_ref[...])
        dx, dgt = _rms_bwd(dhn, h_ref[...], g_ref[...])
        dh = d_ref[...] + dx
        dh_ref[...] = dh
        dhb_ref[...] = dh.astype(BF16)

        @pl.when(pl.program_id(0) == 0)
        def _():
            dg_ref[...] = jnp.zeros_like(dg_ref)

        dg_ref[...] += jnp.sum(dgt, axis=0, keepdims=True)

    row = _rows(tm, D_MODEL)
    return pl.pallas_call(
        body, name="ffn_in_bwd", grid=(T // tm,),
        in_specs=[_rows(tm, 2 * D_FF), _full(w.shape), row, _full((1, D_MODEL)), row],
        out_specs=[row, row, _full((1, D_MODEL))],
        out_shape=[jax.ShapeDtypeStruct((T, D_MODEL), F32), jax.ShapeDtypeStruct((T, D_MODEL), BF16),
                   jax.ShapeDtypeStruct((1, D_MODEL), F32)],
        compiler_params=_cp("arbitrary"),
    )(dgu, w, h, g_ffn, dout)


def _merge_bwd_call(dh_b, wm, wc, wa, u, yc, ya, tm=512):
    T = dh_b.shape[0]

    def body(dh_ref, wm_ref, wc_ref, wa_ref, gc_ref, ga_ref, yc_ref, ya_ref, dyc_ref, dya_ref, dug_ref, dcp_ref, do_ref):
        dz = _dot_nt(dh_ref[...], wm_ref[...])
        sgc = _sigmoid(gc_ref[...].astype(F32))
        sga = _sigmoid(ga_ref[...].astype(F32))
        dyc = (dz * sgc).astype(BF16)
        dya = (dz * sga).astype(BF16)
        dyc_ref[...] = dyc
        dya_ref[...] = dya
        dug_ref[:, :D_MODEL] = (dz * yc_ref[...].astype(F32) * (sgc * (1.0 - sgc))).astype(BF16)
        dug_ref[:, D_MODEL:] = (dz * ya_ref[...].astype(F32) * (sga * (1.0 - sga))).astype(BF16)
        dcp_ref[...] = _dot_nt(dyc, wc_ref[...])
        do_ref[...] = _dot_nt(dya, wa_ref[...]).astype(BF16)

    sq = _full((D_MODEL, D_MODEL))
    row = _rows(tm, D_MODEL)
    b16 = jax.ShapeDtypeStruct((T, D_MODEL), BF16)
    return pl.pallas_call(
        body, name="merge_bwd", grid=(T // tm,),
        in_specs=[row, sq, sq, sq, _rows(tm, D_MODEL, COL_GC // D_MODEL), _rows(tm, D_MODEL, COL_GA // D_MODEL), row, row],
        out_specs=[row, row, _rows(tm, 2 * D_MODEL), row, row],
        out_shape=[b16, b16, jax.ShapeDtypeStruct((T, 2 * D_MODEL), BF16), jax.ShapeDtypeStruct((T, D_MODEL), F32), b16],
        compiler_params=_cp("parallel"),
    )(dh_b, wm, wc, wa, u, u, yc, ya)


def _conv_bwd_call(dcp, hc, u, conv_w, ln_g, ln_b, B, S, ts=256):
    T = B * S
    nS = S // ts
    per32 = ts // HALO
    last32 = T // HALO - 1

    def ln_bwd(dcp_v, hc_v, g, bb):
        mu = jnp.mean(hc_v, axis=-1, keepdims=True)
        xc = hc_v - mu
        rstd = lax.rsqrt(jnp.mean(xc * xc, axis=-1, keepdims=True) + EPS)
        xhat = xc * rstd
        y = xhat * g + bb
        sy = _sigmoid(y)
        dy = dcp_v * (sy * (1.0 + y * (1.0 - sy)))
        dxh = dy * g
        dhc = rstd * (dxh - jnp.mean(dxh, axis=-1, keepdims=True) - xhat * jnp.mean(dxh * xhat, axis=-1, keepdims=True))
        return dhc, dy * xhat, dy

    def body(dcp_ref, dcpn_ref, hc_ref, hcn_ref, a_ref, gate_ref, ap_ref, gp_ref, w_ref, g_ref, bb_ref,
             du_ref, dw_ref, db_ref, dlg_ref, dlb_ref, dhext, hgext, dhsh, hgsh, acc_ref):
        b = pl.program_id(0)
        s = pl.program_id(1)

        @pl.when((b == 0) & (s == 0))
        def _():
            dw_ref[...] = jnp.zeros_like(dw_ref)
            db_ref[...] = jnp.zeros_like(db_ref)
            dlg_ref[...] = jnp.zeros_like(dlg_ref)
            dlb_ref[...] = jnp.zeros_like(dlb_ref)

        dhc, dgt, dbt = ln_bwd(dcp_ref[...], hc_ref[...], g_ref[...], bb_ref[...])
        dhn, _, _ = ln_bwd(dcpn_ref[...], hcn_ref[...], g_ref[...], bb_ref[...])
        dhext[0:ts, :] = dhc
        dhext[ts:, :] = jnp.where(s < nS - 1, dhn, 0.0)
        db_ref[...] += jnp.sum(dhc, axis=0, keepdims=True)
        dlg_ref[...] += jnp.sum(dgt, axis=0, keepdims=True)
        dlb_ref[...] += jnp.sum(dbt, axis=0, keepdims=True)

        sg = _sigmoid(gate_ref[...].astype(F32))
        av = a_ref[...].astype(F32)
        hgext[0:HALO, :] = jnp.where(s > 0, ap_ref[...].astype(F32) * _sigmoid(gp_ref[...].astype(F32)), 0.0)
        hgext[HALO:, :] = av * sg
        _fill_shifted(dhext, dhsh)
        _fill_shifted(hgext, hgsh)

        def colblock(cb, carry):
            cols = pl.ds(pl.multiple_of(cb * 128, 128), 128)
            outs = _conv_taps(w_ref, dhext, dhsh, 0, ts, cb, reverse=True)
            for rb, acc in enumerate(outs):
                acc_ref[rb * 128:(rb + 1) * 128, cols] = acc
            taps = [jnp.zeros((8, 128), F32) for _ in range(CONV_WIDTH)]
            base = HALO - (CONV_WIDTH - 1)
            for r0 in range(0, ts, 128):
                dhc = [dhext[r0 + 8 * i:r0 + 8 * (i + 1), cols] for i in range(16)]
                for s8 in range(8):
                    users = {}
                    for j in range(CONV_WIDTH):
                        if (base + j) % 8 == s8:
                            for i in range(16):
                                users.setdefault(base + j + r0 + 8 * i - s8, []).append((j, i))
                    for row, pairs in sorted(users.items()):
                        glu = hgext[row:row + 8, cols] if s8 == 0 else hgsh[s8 - 1, row:row + 8, cols]
                        for j, i in pairs:
                            taps[j] = taps[j] + dhc[i] * glu
            for j in range(CONV_WIDTH):
                dw_ref[8 * j:8 * (j + 1), cols] += taps[j]
            return carry

        lax.fori_loop(0, C_CONV // 128, colblock, 0)
        dglu = acc_ref[...]
        du_ref[:, :C_CONV] = (dglu * sg).astype(BF16)
        du_ref[:, C_CONV:] = (dglu * av * (sg * (1.0 - sg))).astype(BF16)

    cur = lambda cb: pl.BlockSpec((ts, C_CONV), lambda b, s: (b * nS + s, cb))
    prv = lambda cb: pl.BlockSpec((HALO, C_CONV), lambda b, s: (jnp.maximum(b * (S // HALO) + s * per32 - 1, 0), cb))
    nxt = pl.BlockSpec((HALO, C_CONV), lambda b, s: (jnp.minimum(b * (S // HALO) + (s + 1) * per32, last32), 0))
    vec = _full((1, C_CONV))
    return pl.pallas_call(
        body, name="conv_bwd", grid=(B, nS),
        in_specs=[cur(0), nxt, cur(0), nxt, cur(0), cur(1), prv(0), prv(1), _full((CONV_WIDTH, C_CONV)), vec, vec],
        out_specs=[pl.BlockSpec((ts, 2 * C_CONV), lambda b, s: (b * nS + s, 0)), _full((8 * CONV_WIDTH, C_CONV)), vec, vec, vec],
        out_shape=[jax.ShapeDtypeStruct((T, 2 * C_CONV), BF16), jax.ShapeDtypeStruct((8 * CONV_WIDTH, C_CONV), F32),
                   jax.ShapeDtypeStruct((1, C_CONV), F32), jax.ShapeDtypeStruct((1, C_CONV), F32),
                   jax.ShapeDtypeStruct((1, C_CONV), F32)],
        scratch_shapes=[pltpu.VMEM((ts + HALO, C_CONV), F32), pltpu.VMEM((HALO + ts, C_CONV), F32),
                        pltpu.VMEM((7, ts + HALO, C_CONV), F32), pltpu.VMEM((7, HALO + ts, C_CONV), F32),
                        pltpu.VMEM((ts, C_CONV), F32)],
        compiler_params=_cp("arbitrary", "arbitrary"),
    )(dcp, dcp, hc, hc, u, u, u, u, conv_w, ln_g, ln_b)


def _attn_bwd_call(u, do, qg2, kg2, sinks, slopes, B, S):
    T = B * S
    nb = S // BLOCK
    half = Q_W // 2
    npair = GROUP // 2

    def body(sinks_ref, slopes_ref, q_ref, k_ref, v_ref, do_ref, qg_ref, kg_ref,
             dq_ref, dkv_ref, dqg_ref, dkg_ref, dsk_ref, kA, kB, vA, vB, dkn_acc, dv_acc, dkv_keep,
             btab, s_buf, dp_buf, p_buf, ds_buf, qn_buf, do_buf, rq_buf):
        b = pl.program_id(0)
        kvh = pl.program_id(1)
        lo = _lane_lo()
        lane = lax.broadcasted_iota(jnp.int32, (1, 128), 1)
        first = kvh == 0

        @pl.when((b == 0) & first)
        def _():
            dqg_ref[...] = jnp.zeros_like(dqg_ref)
            dkg_ref[...] = jnp.zeros_like(dkg_ref)
            dsk_ref[...] = jnp.zeros_like(dsk_ref)

        _stage_kv(kvh, k_ref, v_ref, kg_ref, lo, kA, kB, vA, vB)
        dkn_acc[...] = jnp.zeros_like(dkn_acc)
        dv_acc[...] = jnp.zeros_like(dv_acc)
        _fill_bias(btab, kvh, slopes_ref)
        qgs = qg_ref[...] * ATTN_SCALE

        def fold(acc):
            canon = jnp.where(lo, acc[0:2 * BLOCK] + pltpu.roll(acc[2 * BLOCK:], HEAD_DIM, 1), 0.0)
            return jnp.where(first, canon, pltpu.roll(canon, HEAD_DIM, 1))

        def block(n, slot):
            s_scr, dp_scr, p_scr, ds_scr = s_buf.at[slot], dp_buf.at[slot], p_buf.at[slot], ds_buf.at[slot]
            qn_scr, do_scr, rq_scr = qn_buf.at[slot], do_buf.at[slot], rq_buf.at[slot]
            r0 = pl.multiple_of(n * BLOCK, BLOCK)
            win = pl.ds(r0, 2 * BLOCK)
            rows = pl.ds(r0, BLOCK)
            k2t = jnp.concatenate([kA[win, :], kB[win, :]], axis=0)
            v2 = jnp.concatenate([vA[win, :], vB[win, :]], axis=0)
            var = jnp.minimum(n, 1)
            for p in range(npair):
                pc = slice(128 * p, 128 * (p + 1))
                qp = q_ref[rows, pc].astype(F32)
                rq = _head_rstd(qp, lo)
                rq_scr[pc, :] = rq
                qn_scr[pc, :] = (qp * rq * qgs).astype(BF16)
                do_scr[pc, :] = do_ref[rows, pc]
            s_scr[...] = _dot_nt(k2t, qn_scr[...])
            dp_scr[...] = _dot_nt(v2, do_scr[...])
            for p in range(npair):
                for hf in range(2):
                    g = 2 * p + hf
                    keys, qs = slice(2 * BLOCK * hf, 2 * BLOCK * (hf + 1)), slice(128 * p, 128 * (p + 1))
                    pn, esk, rz = _softmax_cols(s_scr, btab, p, var, g, hf, sinks_ref[kvh * GROUP + g])
                    dph = dp_scr[keys, qs]
                    delta = jnp.sum(pn * dph, axis=0, keepdims=True)
                    ds_scr[keys, qs] = (pn * (dph - delta)).astype(BF16)
                    p_scr[keys, qs] = pn.astype(BF16)
                    dsk_ref[...] += jnp.where(lane == kvh * GROUP + g, -jnp.sum(esk * rz * delta), 0.0)
            dqs_all = _dot_tn(k2t, ds_scr[...]).T
            for p in range(npair):
                pc = slice(128 * p, 128 * (p + 1))
                qp = q_ref[rows, pc].astype(F32)
                rq = rq_scr[pc, :]
                dqs = dqs_all[pc, :]
                t = dqs * qgs
                dq = rq * t - qp * (rq * rq * rq * (_half_sums(t * qp, lo) * (1.0 / HEAD_DIM)))
                dq_ref[rows, pc] = dq.astype(BF16)
                dqg_ref[...] += jnp.sum(dqs * qp * rq, axis=0, keepdims=True) * ATTN_SCALE
            dkn_acc[win, :] += fold(_dot(ds_scr[...], qn_scr[...]))
            dv_acc[win, :] += fold(_dot(p_scr[...], do_scr[...]))

        def two_blocks(i, carry):
            block(2 * i, 0)
            block(2 * i + 1, 1)
            return carry

        lax.fori_loop(0, nb // 2, two_blocks, 0)

        dkn = dkn_acc[BLOCK:, :]
        kv = k_ref[...].astype(F32)
        kg = kg_ref[...]
        rk = _head_rstd(kv, lo)
        t = dkn * kg
        dk = rk * t - kv * (rk * rk * rk * (_half_sums(t * kv, lo) * (1.0 / HEAD_DIM)))
        dkg_ref[...] += jnp.sum(dkn * kv * rk, axis=0, keepdims=True)
        dv = dv_acc[BLOCK:, :]

        @pl.when(first)
        def _():
            dkv_keep[:, :128] = dk
            dkv_keep[:, 128:] = dv

        @pl.when(kvh == N_KV_HEADS - 1)
        def _():
            dkv_ref[:, :128] = (dkv_keep[:, :128] + dk).astype(BF16)
            dkv_ref[:, 128:] = (dkv_keep[:, 128:] + dv).astype(BF16)

    smem = pl.BlockSpec(memory_space=pltpu.SMEM)
    kvbuf = pltpu.VMEM((S + BLOCK, 128), BF16)
    accbuf = pltpu.VMEM((S + BLOCK, 128), F32)
    vec = _full((1, 128))
    return pl.pallas_call(
        body, name="attn_bwd", grid=(B, N_KV_HEADS),
        in_specs=[smem, smem,
                  pl.BlockSpec((S, half), lambda b, h: (b, COL_Q // half + h)),
                  pl.BlockSpec((S, 128), lambda b, h: (b, COL_K // 128)),
                  pl.BlockSpec((S, 128), lambda b, h: (b, COL_V // 128)),
                  pl.BlockSpec((S, half), lambda b, h: (b, h)),
                  vec, vec],
        out_specs=[pl.BlockSpec((S, half), lambda b, h: (b, h)), pl.BlockSpec((S, 256), lambda b, h: (b, 0)), vec, vec, vec],
        out_shape=[jax.ShapeDtypeStruct((T, Q_W), BF16), jax.ShapeDtypeStruct((T, 2 * KV_W), BF16),
                   jax.ShapeDtypeStruct((1, 128), F32), jax.ShapeDtypeStruct((1, 128), F32), jax.ShapeDtypeStruct((1, 128), F32)],
        scratch_shapes=[kvbuf, kvbuf, kvbuf, kvbuf, accbuf, accbuf, pltpu.VMEM((S, 256), F32),
                        pltpu.VMEM((2, GROUP, 2 * BLOCK, BLOCK), F32),
                        pltpu.VMEM((2, npair * BLOCK, 4 * BLOCK), F32), pltpu.VMEM((2, npair * BLOCK, 4 * BLOCK), F32),
                        pltpu.VMEM((2, npair * BLOCK, 4 * BLOCK), BF16), pltpu.VMEM((2, npair * BLOCK, 4 * BLOCK), BF16),
                        pltpu.VMEM((2, npair * BLOCK, 128), BF16), pltpu.VMEM((2, npair * BLOCK, 128), BF16),
                        pltpu.VMEM((2, npair * BLOCK, 128), F32)],
        compiler_params=_cp("arbitrary", "arbitrary"),
    )(sinks, slopes, u, u, u, do, qg2, kg2)


def _inproj_bwd_call(du_conv, du_g, du_q, du_kv, w, x, g, dh, tm=512):
    T = x.shape[0]

    def body(dc_ref, dg_ref, dq_ref, dkv_ref, w_ref, x_ref, g_ref, dh_ref, gx_ref, dgm_ref):
        dxn = _dot_nt(dc_ref[...], w_ref[:, COL_A:COL_GC])
        dxn += _dot_nt(dg_ref[...], w_ref[:, COL_GC:COL_Q])
        dxn += _dot_nt(dq_ref[...], w_ref[:, COL_Q:COL_K])
        dxn += _dot_nt(dkv_ref[...], w_ref[:, COL_K:])
        dx, dgt = _rms_bwd(dxn, x_ref[...], g_ref[...])
        gx_ref[...] = dh_ref[...] + dx

        @pl.when(pl.program_id(0) == 0)
        def _():
            dgm_ref[...] = jnp.zeros_like(dgm_ref)

        dgm_ref[...] += jnp.sum(dgt, axis=0, keepdims=True)

    row = _rows(tm, D_MODEL)
    return pl.pallas_call(
        body, name="inproj_bwd", grid=(T // tm,),
        in_specs=[_rows(tm, 2 * C_CONV), _rows(tm, 2 * D_MODEL), _rows(tm, Q_W), _rows(tm, 2 * KV_W), _full(w.shape),
                  row, _full((1, D_MODEL)), row],
        out_specs=[row, _full((1, D_MODEL))],
        out_shape=[jax.ShapeDtypeStruct((T, D_MODEL), F32), jax.ShapeDtypeStruct((1, D_MODEL), F32)],
        compiler_params=_cp("arbitrary"),
    )(du_conv, du_g, du_q, du_kv, w, x, g, dh)


def _adamw_math(w, g, m, v):
    m = ADAM_B1 * m + (1.0 - ADAM_B1) * g
    v = ADAM_B2 * v + (1.0 - ADAM_B2) * (g * g)
    m_hat = m / (1.0 - ADAM_B1 ** ADAM_STEP)
    v_hat = v / (1.0 - ADAM_B2 ** ADAM_STEP)
    delta = -ADAM_LR * (m_hat / (jnp.sqrt(v_hat) + ADAM_EPS) + ADAM_WD * w)
    return delta, m, v


def _adamw_sum_call(name, parts, own, w, m, v, tr):
    R, C = w.shape

    def body(p_ref, own_ref, w_ref, m_ref, v_ref, g_ref, d_ref, nm_ref, nv_ref):
        x, y, c = _position()
        me = 4 * x + 2 * y + c
        g = jnp.zeros((tr, C), F32)
        for s in range(N_DEV):
            g = g + jnp.where(me == s, own_ref[...], p_ref[s]).astype(F32)
        g_ref[...] = g
        d_ref[...], nm_ref[...], nv_ref[...] = _adamw_math(w_ref[...], g, m_ref[...], v_ref[...])

    blk = pl.BlockSpec((tr, C), lambda i: (i, 0))
    out = jax.ShapeDtypeStruct((R, C), F32)
    return pl.pallas_call(
        body, name=name, grid=(R // tr,),
        in_specs=[pl.BlockSpec((N_DEV, tr, C), lambda i: (0, i, 0)), blk, blk, blk, blk],
        out_specs=[blk] * 4, out_shape=[out] * 4,
        compiler_params=_cp("parallel"),
    )(parts, own, w, m, v)


def _adamw_small_call(name, g, w, m, v):
    def body(g_ref, w_ref, m_ref, v_ref, d_ref, nm_ref, nv_ref):
        d_ref[...], nm_ref[...], nv_ref[...] = _adamw_math(w_ref[...], g_ref[...], m_ref[...], v_ref[...])

    out = jax.ShapeDtypeStruct(w.shape, F32)
    return pl.pallas_call(body, name=name, out_shape=[out] * 3)(g, w, m, v)


def _position():
    return lax.axis_index("x"), lax.axis_index("y"), lax.axis_index("c")


def _peer(x, y, c, rel):
    fx, fy, fc = (rel >> 2) & 1, (rel >> 1) & 1, rel & 1
    px = 1 - x if fx else x
    py = 1 - y if fy else y
    pc = 1 - c if fc else c
    return (px, py, pc), 4 * px + 2 * py + pc


def _gather_weights_call(shards):
    n = len(shards)

    def body(*refs):
        ins, outs = refs[:n], refs[n:2 * n]
        send_sems, recv_sems, local_sems = refs[2 * n:]
        x, y, c = _position()
        me = 4 * x + 2 * y + c
        sib, sib_id = _peer(x, y, c, 1)
        chips = [_peer(x, y, c, rel) for rel in (4, 2, 6)]

        def copy(i, k, src, slot, to):
            return pltpu.make_async_remote_copy(
                src_ref=src, dst_ref=outs[i].at[slot], send_sem=send_sems.at[i, k], recv_sem=recv_sems.at[i, k],
                device_id=to, device_id_type=MESH)

        mine = [pltpu.make_async_copy(ins[i], outs[i].at[me], local_sems.at[i]) for i in range(n)]
        for cp in mine:
            cp.start()
        first = []
        for i in range(n):
            first.append(copy(i, 0, ins[i], me, sib))
            first += [copy(i, 1 + j, ins[i], me, dev) for j, (dev, _) in enumerate(chips)]
        for cp in first:
            cp.start()
        passed = []
        for i in range(n):
            for j, (dev, slot) in enumerate(chips):
                copy(i, 1 + j, ins[i], slot, dev).wait_recv()
                fwd = copy(i, 4 + j, outs[i].at[slot], slot, sib)
                fwd.start()
                passed.append(fwd)
        for i in range(n):
            copy(i, 0, ins[i], sib_id, sib).wait_recv()
            for j, (_, slot) in enumerate(chips):
                copy(i, 4 + j, ins[i], slot ^ 1, sib).wait_recv()
        for cp in first + passed:
            cp.wait_send()
        for cp in mine:
            cp.wait()

    hbm = pl.BlockSpec(memory_space=pl.ANY)
    return pl.pallas_call(
        body, name="gather_weights",
        in_specs=[hbm] * n, out_specs=[hbm] * n,
        out_shape=[jax.ShapeDtypeStruct((N_DEV,) + s.shape, s.dtype) for s in shards],
        scratch_shapes=[pltpu.SemaphoreType.DMA((n, 7)), pltpu.SemaphoreType.DMA((n, 7)), pltpu.SemaphoreType.DMA((n,))],
    )(*shards)


_HBM = pl.BlockSpec(memory_space=pltpu.HBM)
_SEM = pl.BlockSpec(memory_space=pltpu.SEMAPHORE)
_EFFECT = pltpu.SideEffectType.DATAFLOW_SIDE_EFFECTING


def _exchange_copies(scatter, incoming, src_refs, land_refs, send_sems, recv_sems):
    x, y, c = _position()
    me = 4 * x + 2 * y + c
    out = []
    for i, (src, land) in enumerate(zip(src_refs, land_refs)):
        for k in range(N_DEV - 1):
            dev, dev_id = _peer(x, y, c, k + 1)
            j = i * (N_DEV - 1) + k
            out.append(pltpu.make_async_remote_copy(
                src_ref=src.at[dev_id] if scatter else src, dst_ref=land.at[dev_id if incoming else me],
                send_sem=send_sems.at[j], recv_sem=recv_sems.at[j], device_id=dev, device_id_type=MESH))
    return out


def _exchange_start_call(name, arrays, scatter):
    n = len(arrays)
    lands = [lax.empty((N_DEV,) + (a.shape[1:] if scatter else a.shape), a.dtype) for a in arrays]

    def body(*refs):
        src_refs, land_refs = refs[:n], refs[n:2 * n]
        send_sems, recv_sems, token = refs[2 * n], refs[2 * n + 1], refs[-1]
        for copy in _exchange_copies(scatter, False, src_refs, land_refs, send_sems, recv_sems):
            copy.start()
        token[...] = jnp.zeros_like(token)

    sems = pltpu.SemaphoreType.DMA((n * (N_DEV - 1),))
    res = pl.pallas_call(
        body, name=name,
        out_shape=[sems, sems] + [pltpu.HBM(a.shape, a.dtype) for a in arrays] + [pltpu.HBM(l.shape, l.dtype) for l in lands]
        + [jax.ShapeDtypeStruct((8, 128), F32)],
        in_specs=[_HBM] * (2 * n), out_specs=[_SEM, _SEM] + [_HBM] * (2 * n) + [pl.BlockSpec(memory_space=pltpu.VMEM)],
        input_output_aliases={i: 2 + i for i in range(2 * n)},
        compiler_params=pltpu.CompilerParams(has_side_effects=_EFFECT),
    )(*[pltpu.with_memory_space_constraint(a, pltpu.HBM) for a in list(arrays) + lands])
    return res[0], res[1], res[2:2 + n], res[2 + n:2 + 2 * n], res[-1]


def _exchange_wait_call(name, started, scatter, after):
    send_sems, recv_sems, srcs, lands, _ = started
    n = len(srcs)

    def body(*refs):
        src_refs, land_refs = refs[:n], refs[n:2 * n]
        send_s, recv_s = refs[2 * n], refs[2 * n + 1]
        for copy in _exchange_copies(scatter, False, src_refs, land_refs, send_s, recv_s):
            copy.wait_send()
        for copy in _exchange_copies(scatter, True, src_refs, land_refs, send_s, recv_s):
            copy.wait_recv()

    res = pl.pallas_call(
        body, name=name,
        out_shape=[pltpu.HBM(a.shape, a.dtype) for a in list(srcs) + list(lands)],
        in_specs=[_HBM] * (2 * n) + [_SEM, _SEM, pl.BlockSpec(memory_space=pl.ANY)], out_specs=[_HBM] * (2 * n),
        input_output_aliases={i: i for i in range(2 * n)},
        compiler_params=pltpu.CompilerParams(has_side_effects=_EFFECT),
    )(*srcs, *lands, send_sems, recv_sems, after)
    return res[n:]


def _allsum_small_call(part):
    P = part.shape[0]

    def body(p_ref, o_ref, all_ref, send_sems, recv_sems):
        x, y, c = _position()
        me = 4 * x + 2 * y + c
        peers = [_peer(x, y, c, rel) for rel in range(1, N_DEV)]
        all_ref[me] = p_ref[...]

        def copy(k, slot, dev):
            return pltpu.make_async_remote_copy(
                src_ref=p_ref, dst_ref=all_ref.at[slot], send_sem=send_sems.at[k], recv_sem=recv_sems.at[k],
                device_id=dev, device_id_type=MESH)

        sends = [copy(k, me, dev) for k, (dev, _) in enumerate(peers)]
        for cp in sends:
            cp.start()
        for k, (dev, dev_id) in enumerate(peers):
            copy(k, dev_id, dev).wait_recv()
        for cp in sends:
            cp.wait_send()
        tot = all_ref[0]
        for s in range(1, N_DEV):
            tot = tot + all_ref[s]
        o_ref[...] = tot

    vm = pl.BlockSpec(memory_space=pltpu.VMEM)
    return pl.pallas_call(
        body, name="allsum_small", in_specs=[vm], out_specs=vm,
        out_shape=jax.ShapeDtypeStruct(part.shape, F32),
        scratch_shapes=[pltpu.VMEM((N_DEV, P, 128), F32), pltpu.SemaphoreType.DMA((7,)), pltpu.SemaphoreType.DMA((7,))],
    )(part)


def _pad_rows(v, rows):
    flat = v.reshape(-1)
    return jnp.pad(flat, (0, rows * 128 - flat.shape[0])).reshape(rows, 128)


def _blocks_by_cols(w):
    K, N = w.shape
    return w.reshape(K, N_DEV, N // N_DEV).transpose(1, 0, 2)


def _cols_from_blocks(wb):
    n, K, C = wb.shape
    return wb.transpose(1, 0, 2).reshape(K, n * C)


def _local_step(x, tgt, w_in_i, conv_w, conv_b, ln_g, ln_b, qg, kg, sinks, g_mix, g_ffn, token0, rest_weights, grads_ready, B, S):
    row = lambda v: v.reshape(1, -1)
    qg2 = jnp.tile(qg, 2).reshape(1, 128)
    kg2 = jnp.tile(kg, 2).reshape(1, 128)
    heads = jnp.arange(1, N_Q_HEADS + 1, dtype=F32)
    slopes = jnp.exp2(-8.0 * heads / N_Q_HEADS)

    u, xn = _inproj_call(x, row(g_mix) + token0, w_in_i)
    hc, c = _conv_fwd_call(u, conv_w, row(conv_b), row(ln_g), row(ln_b), B, S)
    o = _attn_fwd_call(u, qg2, kg2, sinks, slopes, B, S)
    wc, wa, wm, w_ffn_in, wd = rest_weights(o)
    yc, ya, z, h, hn = _merge_call(c, o, u, x, wc, wa, wm, row(g_ffn))
    gu, act = _ffn_in_call(hn, w_ffn_in)
    sse, dout, dout_b = _ffn_out_call(act, wd, h, tgt)

    dgu = _ffn_down_bwd_call(dout_b, wd, gu)
    g_wd = _dw_call("dw_ffn_down", act, dout_b, tn=512)
    g_wffn = _dw_call("dw_ffn_in", hn, dgu, tn=2 * D_FF // 4)
    tok = grads_ready(dict(w_ffn_down=g_wd, w_ffn_in=g_wffn))
    dh, dh_b, g_gffn = _ffn_in_bwd_call(dgu, w_ffn_in, h, row(g_ffn) + tok, dout)
    dyc, dya, du_g, dcp, do = _merge_bwd_call(dh_b, wm, wc, wa, u, yc, ya)
    g_wm = _dw_call("dw_merge", z, dh_b, tn=D_MODEL)
    g_wc = _dw_call("dw_conv_out", c, dyc, tn=D_MODEL)
    g_wa = _dw_call("dw_attn_out", o, dya, tn=D_MODEL)
    tok = grads_ready(dict(w_merge_out=g_wm, w_conv_out=g_wc, w_attn_out=g_wa))
    du_conv, g_cw8, g_cb, g_lg, g_lb = _conv_bwd_call(dcp, hc, u, conv_w, row(ln_g) + tok, row(ln_b), B, S)
    du_q, du_kv, g_qg2, g_kg2, g_sk = _attn_bwd_call(u, do, qg2, kg2, sinks, slopes, B, S)
    g_win = jnp.concatenate([
        _dw_call("dw_in_conv", xn, du_conv, tn=D_MODEL),
        _dw_call("dw_in_q", xn, du_q, tn=D_MODEL),
        _dw_call("dw_in_kv", xn, du_kv, tn=2 * KV_W),
        _dw_call("dw_in_gates", xn, du_g, tn=D_MODEL)], axis=1)
    tok = grads_ready(dict(w_in=g_win))
    grad_x, g_gmix = _inproj_bwd_call(du_conv, du_g, du_q, du_kv, w_in_i, x, row(g_mix) + tok, dh)

    g_cw = g_cw8.reshape(CONV_WIDTH, 8, C_CONV).sum(axis=1)
    g_qg = g_qg2[0, :HEAD_DIM] + g_qg2[0, HEAD_DIM:]
    g_kg = g_kg2[0, :HEAD_DIM] + g_kg2[0, HEAD_DIM:]
    small = dict(norm_mix_g=g_gmix[0], conv_dw_b=g_cb[0], conv_ln_g=g_lg[0], conv_ln_b=g_lb[0], q_norm_g=g_qg,
                 k_norm_g=g_kg, sinks=g_sk[0, :N_Q_HEADS], norm_ffn_g=g_gffn[0], conv_dw_w=g_cw)
    return sse[0, 0], grad_x, small


def _to_internal_cols(w):
    o1 = 2 * C_CONV
    o4 = o1 + Q_W + 2 * KV_W
    return jnp.concatenate([w[:, :o1], w[:, o4:], w[:, o1:o4]], axis=1)


_COL_SHARDED = ("w_in", "w_ffn_in")

_SMALL = (("norm_mix_g", 8), ("conv_dw_b", 8), ("conv_ln_g", 8), ("conv_ln_b", 8), ("norm_ffn_g", 8),
          ("q_norm_g", 1), ("k_norm_g", 1), ("sinks", 1))
_SMALL_ROWS = 48
_CW_ROWS = CONV_WIDTH * C_CONV // 128


def _pack_small(d):
    rows = [_pad_rows(d[name], r) for name, r in _SMALL]
    used = sum(r for _, r in _SMALL)
    return jnp.concatenate(rows + [jnp.zeros((_SMALL_ROWS - used, 128), F32)], axis=0)


def _unpack_small(packed, like):
    out, r0 = {}, 0
    for name, r in _SMALL:
        n = like[name].size
        out[name] = packed[r0:r0 + r].reshape(-1)[:n].reshape(like[name].shape)
        r0 += r
    return out


def kernel(x, norm_mix_g, w_in, conv_dw_w, conv_dw_b, conv_ln_g, conv_ln_b, w_conv_out, q_norm_g, k_norm_g, sinks, w_attn_out, w_merge_out, norm_ffn_g, w_ffn_in, w_ffn_down, loss_target, m_norm_mix_g, m_w_in, m_conv_dw_w, m_conv_dw_b, m_conv_ln_g, m_conv_ln_b, m_w_conv_out, m_q_norm_g, m_k_norm_g, m_sinks, m_w_attn_out, m_w_merge_out, m_norm_ffn_g, m_w_ffn_in, m_w_ffn_down, v_norm_mix_g, v_w_in, v_conv_dw_w, v_conv_dw_b, v_conv_ln_g, v_conv_ln_b, v_w_conv_out, v_q_norm_g, v_k_norm_g, v_sinks, v_w_attn_out, v_w_merge_out, v_norm_ffn_g, v_w_ffn_in, v_w_ffn_down):
    names = ["norm_mix_g", "w_in", "conv_dw_w", "conv_dw_b", "conv_ln_g", "conv_ln_b", "w_conv_out", "q_norm_g",
             "k_norm_g", "sinks", "w_attn_out", "w_merge_out", "norm_ffn_g", "w_ffn_in", "w_ffn_down"]
    w = dict(zip(names, (norm_mix_g, w_in, conv_dw_w, conv_dw_b, conv_ln_g, conv_ln_b, w_conv_out, q_norm_g, k_norm_g,
                         sinks, w_attn_out, w_merge_out, norm_ffn_g, w_ffn_in, w_ffn_down)))
    m = dict(zip(names, (m_norm_mix_g, m_w_in, m_conv_dw_w, m_conv_dw_b, m_conv_ln_g, m_conv_ln_b, m_w_conv_out,
                         m_q_norm_g, m_k_norm_g, m_sinks, m_w_attn_out, m_w_merge_out, m_norm_ffn_g, m_w_ffn_in,
                         m_w_ffn_down)))
    v = dict(zip(names, (v_norm_mix_g, v_w_in, v_conv_dw_w, v_conv_dw_b, v_conv_ln_g, v_conv_ln_b, v_w_conv_out,
                         v_q_norm_g, v_k_norm_g, v_sinks, v_w_attn_out, v_w_merge_out, v_norm_ffn_g, v_w_ffn_in,
                         v_w_ffn_down)))
    B, S, _ = x.shape
    T = B * S
    me = 4 * lax.axis_index("x") + 2 * lax.axis_index("y") + lax.axis_index("c")

    cw_shard = _pad_rows(conv_dw_w.reshape(CONV_WIDTH, 128), 32)
    g_win, g_cwt = _gather_weights_call([w_in.astype(BF16), cw_shard])
    conv_w_full = g_cwt.reshape(N_DEV, 32 * 128)[:, :CONV_WIDTH * 128].reshape(N_DEV, CONV_WIDTH, 128)
    conv_w_full = conv_w_full.transpose(1, 0, 2).reshape(CONV_WIDTH, C_CONV)
    w_in_i = _to_internal_cols(_cols_from_blocks(g_win))
    rest_names = ["w_conv_out", "w_attn_out", "w_merge_out", "w_ffn_in", "w_ffn_down"]
    rest_shards = [w[k].astype(BF16) for k in rest_names]
    rest_started = _exchange_start_call("gather_rest_start", rest_shards, scatter=False)

    def rest_weights(after):
        lands = _exchange_wait_call("gather_rest_wait", rest_started, False, after)
        mine = lax.broadcasted_iota(jnp.int32, (N_DEV, 1, 1), 0) == me
        full = [jnp.where(mine, s[None], l) for l, s in zip(lands, rest_shards)]
        wc_f, wa_f, wm_f = (f.reshape(D_MODEL, D_MODEL) for f in full[:3])
        return wc_f, wa_f, wm_f, _cols_from_blocks(full[3]), full[4].reshape(D_FF, D_MODEL)

    exchanges = []

    def grads_ready(group):
        ks = list(group)
        blocks = [_blocks_by_cols(group[k]) if k in _COL_SHARDED
                  else group[k].reshape(N_DEV, group[k].shape[0] // N_DEV, group[k].shape[1]) for k in ks]
        started = _exchange_start_call("scatter_" + ks[0] + "_start", blocks, scatter=True)
        exchanges.append((ks, started, [lax.dynamic_index_in_dim(b, me, 0, keepdims=False) for b in blocks]))
        return started[-1][0:1, 0:1]

    sse, grad_x, small = _local_step(
        x.reshape(T, D_MODEL), loss_target.reshape(T, D_MODEL), w_in_i, conv_w_full, conv_dw_b, conv_ln_g, conv_ln_b,
        q_norm_g, k_norm_g, sinks, norm_mix_g, norm_ffn_g, rest_started[-1][0:1, 0:1], rest_weights, grads_ready, B, S)

    grad, delta, new_m, new_v = {}, {}, {}, {}
    packed = jnp.concatenate([_pack_small(small), small["conv_dw_w"].reshape(_CW_ROWS, 128),
                              jnp.full((8, 128), sse, F32)], axis=0)
    tot = _allsum_small_call(packed)
    loss = 0.5 / D_MODEL * tot[_SMALL_ROWS + _CW_ROWS, 0]
    g_small = _unpack_small(tot[:_SMALL_ROWS], w)
    d_s, m_s, v_s = _adamw_small_call("adamw_small", tot[:_SMALL_ROWS], _pack_small(w), _pack_small(m), _pack_small(v))
    d_small, m_small, v_small = _unpack_small(d_s, w), _unpack_small(m_s, w), _unpack_small(v_s, w)
    g_cw = lax.dynamic_slice_in_dim(tot[_SMALL_ROWS:_SMALL_ROWS + _CW_ROWS].reshape(CONV_WIDTH, C_CONV), me * 128, 128, axis=1)
    cw2 = lambda t: t.reshape(CONV_WIDTH, 128)
    d_cw, m_cw, v_cw = _adamw_small_call("adamw_conv_w", g_cw, cw2(conv_dw_w), cw2(m_conv_dw_w), cw2(v_conv_dw_w))
    cw3 = lambda t: t.reshape(CONV_WIDTH, 1, 128)
    for k, _ in _SMALL:
        grad[k], delta[k], new_m[k], new_v[k] = g_small[k], d_small[k], m_small[k], v_small[k]
    grad["conv_dw_w"], delta["conv_dw_w"], new_m["conv_dw_w"], new_v["conv_dw_w"] = cw3(g_cw), cw3(d_cw), cw3(m_cw), cw3(v_cw)

    after = d_cw
    for ks, started, owns in exchanges:
        lands = _exchange_wait_call("scatter_" + ks[0] + "_wait", started, True, after)
        for k, land, own in zip(ks, lands, owns):
            tr = {1024: 256, 352: 176, 128: 128}[w[k].shape[0]]
            grad[k], delta[k], new_m[k], new_v[k] = _adamw_sum_call("adamw_" + k, land, own, w[k], m[k], v[k], tr)
            after = new_v[k]

    return (loss, grad_x.reshape(B, S, D_MODEL), *[grad[k] for k in names], *[delta[k] for k in names],
            *[new_m[k] for k in names], *[new_v[k] for k in names])
```

```python
import functools
import math

import jax
import jax.numpy as jnp
from jax import lax
from jax.experimental import pallas as pl
from jax.experimental.pallas import tpu as pltpu

F32 = jnp.float32
BF16 = jnp.bfloat16
MESH = pl.DeviceIdType.MESH

N_DEV = 8
D_MODEL = 1024
C_CONV = 1024
CONV_WIDTH = 31
HEAD_DIM = 64
N_Q_HEADS = 16
N_KV_HEADS = 2
GROUP = N_Q_HEADS // N_KV_HEADS
WINDOW = 128
BLOCK = 128
D_FF = 2816
EPS = 1e-6
NEG = -1e30
ATTN_SCALE = 1.0 / math.sqrt(HEAD_DIM)
Q_W = N_Q_HEADS * HEAD_DIM
KV_W = N_KV_HEADS * HEAD_DIM
IN_COLS = 2 * C_CONV + Q_W + 2 * KV_W + 2 * D_MODEL

ADAM_LR = 0.001
ADAM_B1 = 0.9
ADAM_B2 = 0.999
ADAM_EPS = 1e-08
ADAM_WD = 0.01
ADAM_STEP = 10

COL_A, COL_GATE, COL_GC, COL_GA, COL_Q, COL_K, COL_V = 0, 1024, 2048, 3072, 4096, 5120, 5248

HALO = 32
VMEM_LIMIT = 56 * 1024 * 1024


def _cp(*sem):
    return pltpu.CompilerParams(dimension_semantics=sem, vmem_limit_bytes=VMEM_LIMIT)


def _sigmoid(x):
    return jax.nn.sigmoid(x)


def _dot(a, b):
    return jnp.dot(a, b, preferred_element_type=F32)


def _dot_nt(a, b):
    return lax.dot_general(a, b, (((1,), (1,)), ((), ())), preferred_element_type=F32)


def _dot_tn(a, b):
    return lax.dot_general(a, b, (((0,), (0,)), ((), ())), preferred_element_type=F32)


def _full(shape):
    return pl.BlockSpec(shape, lambda *_: (0,) * len(shape))


def _rows(tm, ncols, colblk=0):
    return pl.BlockSpec((tm, ncols), lambda i: (i, colblk))


def _inproj_call(x, g, w, tm=512):
    T = x.shape[0]
    N = w.shape[1]

    def body(x_ref, g_ref, w_ref, u_ref, xn_ref):
        xv = x_ref[...]
        r = lax.rsqrt(jnp.mean(xv * xv, axis=-1, keepdims=True) + EPS)
        xn = (xv * r * g_ref[...]).astype(BF16)
        xn_ref[...] = xn
        u_ref[...] = _dot(xn, w_ref[...]).astype(BF16)

    return pl.pallas_call(
        body, name="inproj", grid=(T // tm,),
        in_specs=[_rows(tm, D_MODEL), _full((1, D_MODEL)), _full(w.shape)],
        out_specs=[_rows(tm, N), _rows(tm, D_MODEL)],
        out_shape=[jax.ShapeDtypeStruct((T, N), BF16), jax.ShapeDtypeStruct((T, D_MODEL), BF16)],
        compiler_params=_cp("parallel"),
    )(x, g, w)


def _fill_shifted(src_ref, sh_ref):
    n = src_ref.shape[0] - 8
    for s in range(1, 8):
        sh_ref[s - 1, 0:n, :] = src_ref[s:s + n, :]


def _window(src_ref, sh_ref, off, cols):
    s = off % 8
    if s == 0:
        return src_ref[off:off + 128, cols]
    return sh_ref[s - 1, off - s:off - s + 128, cols]


def _conv_taps(w_ref, src_ref, sh_ref, base, rows, cb, reverse):
    cols = pl.ds(pl.multiple_of(cb * 128, 128), 128)
    outs = []
    for rb in range(rows // 128):
        acc = jnp.zeros((128, 128), F32)
        for j in range(CONV_WIDTH):
            off = base + (CONV_WIDTH - 1 - j if reverse else j) + rb * 128
            acc = acc + w_ref[j:j + 1, cols] * _window(src_ref, sh_ref, off, cols)
        outs.append(acc)
    return outs


def _conv_fwd_call(u, conv_w, conv_b, ln_g, ln_b, B, S, ts=256):
    T = B * S
    nS = S // ts
    per32 = ts // HALO

    def body(a_ref, gate_ref, ap_ref, gp_ref, w_ref, b_ref, g_ref, bb_ref, hc_ref, c_ref, hext, hsh, acc_ref):
        s = pl.program_id(1)
        prev = ap_ref[...].astype(F32) * _sigmoid(gp_ref[...].astype(F32))
        hext[0:HALO, :] = jnp.where(s > 0, prev, 0.0)
        hext[HALO:, :] = a_ref[...].astype(F32) * _sigmoid(gate_ref[...].astype(F32))
        _fill_shifted(hext, hsh)

        def colblock(cb, carry):
            cols = pl.ds(pl.multiple_of(cb * 128, 128), 128)
            outs = _conv_taps(w_ref, hext, hsh, HALO - (CONV_WIDTH - 1), ts, cb, reverse=False)
            for rb, acc in enumerate(outs):
                acc_ref[rb * 128:(rb + 1) * 128, cols] = acc
            return carry

        lax.fori_loop(0, C_CONV // 128, colblock, 0)
        hc = acc_ref[...] + b_ref[...]
        hc_ref[...] = hc
        mu = jnp.mean(hc, axis=-1, keepdims=True)
        xc = hc - mu
        var = jnp.mean(xc * xc, axis=-1, keepdims=True)
        y = xc * lax.rsqrt(var + EPS) * g_ref[...] + bb_ref[...]
        c_ref[...] = (y * _sigmoid(y)).astype(BF16)

    cur = lambda cb: pl.BlockSpec((ts, C_CONV), lambda b, s: (b * nS + s, cb))
    prv = lambda cb: pl.BlockSpec((HALO, C_CONV), lambda b, s: (jnp.maximum(b * (S // HALO) + s * per32 - 1, 0), cb))
    vec = _full((1, C_CONV))
    return pl.pallas_call(
        body, name="conv_fwd", grid=(B, nS),
        in_specs=[cur(0), cur(1), prv(0), prv(1), _full((CONV_WIDTH, C_CONV)), vec, vec, vec],
        out_specs=[pl.BlockSpec((ts, C_CONV), lambda b, s: (b * nS + s, 0))] * 2,
        out_shape=[jax.ShapeDtypeStruct((T, C_CONV), F32), jax.ShapeDtypeStruct((T, C_CONV), BF16)],
        scratch_shapes=[pltpu.VMEM((HALO + ts, C_CONV), F32), pltpu.VMEM((7, HALO + ts, C_CONV), F32),
                        pltpu.VMEM((ts, C_CONV), F32)],
        compiler_params=_cp("parallel", "arbitrary"),
    )(u, u, u, u, conv_w, conv_b, ln_g, ln_b)


def _lane_lo():
    return lax.broadcasted_iota(jnp.int32, (1, 128), 1) < HEAD_DIM


def _half_sums(t, lo):
    s_lo = jnp.sum(jnp.where(lo, t, 0.0), axis=-1, keepdims=True)
    s_hi = jnp.sum(jnp.where(lo, 0.0, t), axis=-1, keepdims=True)
    return jnp.where(lo, s_lo, s_hi)


def _head_rstd(t, lo):
    return lax.rsqrt(_half_sums(t * t, lo) * (1.0 / HEAD_DIM) + EPS)


def _fill_bias(btab, kvh, slopes_ref):
    sj = lax.broadcasted_iota(jnp.int32, (2 * BLOCK, BLOCK), 0)
    qi = lax.broadcasted_iota(jnp.int32, (2 * BLOCK, BLOCK), 1)
    dist = qi + BLOCK - sj
    valid = (dist >= 0) & (dist < WINDOW)
    distf = dist.astype(F32)
    for g in range(GROUP):
        bias = jnp.where(valid, -slopes_ref[kvh * GROUP + g] * distf, NEG)
        btab[1, g] = bias
        btab[0, g] = jnp.where(sj >= BLOCK, bias, NEG)


def _softmax_cols(st_ref, btab, p, var, g, hf, sk):
    sh = st_ref[2 * BLOCK * hf:2 * BLOCK * (hf + 1), 128 * p:128 * (p + 1)] + btab[var, g]
    m = jnp.maximum(jnp.max(sh, axis=0, keepdims=True), sk)
    e = jnp.exp(sh - m)
    esk = jnp.exp(sk - m)
    rz = 1.0 / (jnp.sum(e, axis=0, keepdims=True) + esk)
    return e * rz, esk, rz


def _stage_kv(kvh, k_ref, v_ref, kg_ref, lo, kA, kB, vA, vB):
    kv = k_ref[...].astype(F32)
    rk = _head_rstd(kv, lo)
    kn = kv * rk * kg_ref[...]
    mine = lax.broadcasted_iota(jnp.int32, (1, 128), 1) // HEAD_DIM == kvh
    ksel = jnp.where(mine, kn, 0.0)
    vsel = jnp.where(mine, v_ref[...].astype(F32), 0.0)
    krol = pltpu.roll(ksel, HEAD_DIM, 1)
    vrol = pltpu.roll(vsel, HEAD_DIM, 1)
    first = kvh == 0
    zeros = jnp.zeros((BLOCK, 128), BF16)
    for ref, val in ((kA, jnp.where(first, ksel, krol)), (kB, jnp.where(first, krol, ksel)),
                     (vA, jnp.where(first, vsel, vrol)), (vB, jnp.where(first, vrol, vsel))):
        ref[0:BLOCK, :] = zeros
        ref[BLOCK:, :] = val.astype(BF16)


def _attn_fwd_call(u, qg2, kg2, sinks, slopes, B, S):
    T = B * S
    nb = S // BLOCK
    half = Q_W // 2
    npair = GROUP // 2

    def body(sinks_ref, slopes_ref, q_ref, k_ref, v_ref, qg_ref, kg_ref, o_ref, kA, kB, vA, vB, btab, qn_buf, s_buf, p_buf):
        kvh = pl.program_id(1)
        lo = _lane_lo()
        _stage_kv(kvh, k_ref, v_ref, kg_ref, lo, kA, kB, vA, vB)
        _fill_bias(btab, kvh, slopes_ref)
        qgs = qg_ref[...] * ATTN_SCALE

        def block(n, slot):
            qn_scr, s_scr, p_scr = qn_buf.at[slot], s_buf.at[slot], p_buf.at[slot]
            r0 = pl.multiple_of(n * BLOCK, BLOCK)
            win = pl.ds(r0, 2 * BLOCK)
            k2t = jnp.concatenate([kA[win, :], kB[win, :]], axis=0)
            v2 = jnp.concatenate([vA[win, :], vB[win, :]], axis=0)
            var = jnp.minimum(n, 1)
            for p in range(npair):
                pc = slice(128 * p, 128 * (p + 1))
                qp = q_ref[pl.ds(r0, BLOCK), pc].astype(F32)
                qn_scr[pc, :] = (qp * _head_rstd(qp, lo) * qgs).astype(BF16)
            s_scr[...] = _dot_nt(k2t, qn_scr[...])
            for p in range(npair):
                for hf in range(2):
                    g = 2 * p + hf
                    pn, _, _ = _softmax_cols(s_scr, btab, p, var, g, hf, sinks_ref[kvh * GROUP + g])
                    p_scr[2 * BLOCK * hf:2 * BLOCK * (hf + 1), 128 * p:128 * (p + 1)] = pn.astype(BF16)
            o_all = _dot_tn(v2, p_scr[...]).T
            for p in range(npair):
                pc = slice(128 * p, 128 * (p + 1))
                o_ref[pl.ds(r0, BLOCK), pc] = o_all[pc, :].astype(BF16)

        def two_blocks(i, carry):
            block(2 * i, 0)
            block(2 * i + 1, 1)
            return carry

        lax.fori_loop(0, nb // 2, two_blocks, 0)

    smem = pl.BlockSpec(memory_space=pltpu.SMEM)
    kvbuf = pltpu.VMEM((S + BLOCK, 128), BF16)
    pair_scores = (2, npair * BLOCK, 4 * BLOCK)
    return pl.pallas_call(
        body, name="attn_fwd", grid=(B, N_KV_HEADS),
        in_specs=[smem, smem,
                  pl.BlockSpec((S, half), lambda b, h: (b, COL_Q // half + h)),
                  pl.BlockSpec((S, 128), lambda b, h: (b, COL_K // 128)),
                  pl.BlockSpec((S, 128), lambda b, h: (b, COL_V // 128)),
                  _full((1, 128)), _full((1, 128))],
        out_specs=pl.BlockSpec((S, half), lambda b, h: (b, h)),
        out_shape=jax.ShapeDtypeStruct((T, Q_W), BF16),
        scratch_shapes=[kvbuf, kvbuf, kvbuf, kvbuf, pltpu.VMEM((2, GROUP, 2 * BLOCK, BLOCK), F32),
                        pltpu.VMEM((2, npair * BLOCK, 128), BF16), pltpu.VMEM(pair_scores, F32), pltpu.VMEM(pair_scores, BF16)],
        compiler_params=_cp("parallel", "arbitrary"),
    )(sinks, slopes, u, u, u, qg2, kg2)


def _merge_call(c, o, u, x, wc, wa, wm, g_ffn, tm=512):
    T = x.shape[0]

    def body(c_ref, o_ref, gc_ref, ga_ref, x_ref, wc_ref, wa_ref, wm_ref, g_ref, yc_ref, ya_ref, z_ref, h_ref, hn_ref):
        yc = _dot(c_ref[...], wc_ref[...])
        ya = _dot(o_ref[...], wa_ref[...])
        yc_ref[...] = yc.astype(BF16)
        ya_ref[...] = ya.astype(BF16)
        z = (_sigmoid(gc_ref[...].astype(F32)) * yc + _sigmoid(ga_ref[...].astype(F32)) * ya).astype(BF16)
        z_ref[...] = z
        h = x_ref[...] + _dot(z, wm_ref[...])
        h_ref[...] = h
        r = lax.rsqrt(jnp.mean(h * h, axis=-1, keepdims=True) + EPS)
        hn_ref[...] = (h * r * g_ref[...]).astype(BF16)

    sq = _full((D_MODEL, D_MODEL))
    row = _rows(tm, D_MODEL)
    f32 = jax.ShapeDtypeStruct((T, D_MODEL), F32)
    b16 = jax.ShapeDtypeStruct((T, D_MODEL), BF16)
    return pl.pallas_call(
        body, name="merge_fwd", grid=(T // tm,),
        in_specs=[row, row, _rows(tm, D_MODEL, COL_GC // D_MODEL), _rows(tm, D_MODEL, COL_GA // D_MODEL), row,
                  sq, sq, sq, _full((1, D_MODEL))],
        out_specs=[row] * 5,
        out_shape=[b16, b16, b16, f32, b16],
        compiler_params=_cp("parallel"),
    )(c, o, u, u, x, wc, wa, wm, g_ffn)


def _ffn_in_call(hn, w, tm=512):
    T = hn.shape[0]

    def body(hn_ref, w_ref, fac_ref, act_ref):
        gu = _dot(hn_ref[...], w_ref[...])
        g = gu[:, :D_FF]
        up = gu[:, D_FF:]
        sg = _sigmoid(g)
        silu = g * sg
        act_ref[...] = (silu * up).astype(BF16)
        fac_ref[:, :D_FF] = (up * (sg * (1.0 + g * (1.0 - sg)))).astype(BF16)
        fac_ref[:, D_FF:] = silu.astype(BF16)

    return pl.pallas_call(
        body, name="ffn_in", grid=(T // tm,),
        in_specs=[_rows(tm, D_MODEL), _full(w.shape)],
        out_specs=[_rows(tm, 2 * D_FF), _rows(tm, D_FF)],
        out_shape=[jax.ShapeDtypeStruct((T, 2 * D_FF), BF16), jax.ShapeDtypeStruct((T, D_FF), BF16)],
        compiler_params=_cp("parallel"),
    )(hn, w)


def _ffn_out_call(act, wd, h, tgt, tm=512):
    T = h.shape[0]

    def body(act_ref, wd_ref, h_ref, t_ref, sse_ref, d_ref, db_ref):
        e = h_ref[...] + _dot(act_ref[...], wd_ref[...]) - t_ref[...]

        @pl.when(pl.program_id(0) == 0)
        def _():
            sse_ref[...] = jnp.zeros_like(sse_ref)

        sse_ref[...] += jnp.sum(e * e)
        d = e * (1.0 / D_MODEL)
        d_ref[...] = d
        db_ref[...] = d.astype(BF16)

    row = _rows(tm, D_MODEL)
    return pl.pallas_call(
        body, name="ffn_out", grid=(T // tm,),
        in_specs=[_rows(tm, D_FF), _full(wd.shape), row, row],
        out_specs=[_full((8, 128)), row, row],
        out_shape=[jax.ShapeDtypeStruct((8, 128), F32), jax.ShapeDtypeStruct((T, D_MODEL), F32),
                   jax.ShapeDtypeStruct((T, D_MODEL), BF16)],
        compiler_params=_cp("arbitrary"),
    )(act, wd, h, tgt)


def _dw_call(name, a, b, tn, tk=2048):
    T, K = a.shape
    N = b.shape[1]
    tk = min(tk, T)
    nk = T // tk

    def body(a_ref, b_ref, o_ref, acc_ref):
        k = pl.program_id(1)

        @pl.when(k == 0)
        def _():
            acc_ref[...] = jnp.zeros_like(acc_ref)

        acc_ref[...] += _dot_tn(a_ref[...], b_ref[...])

        @pl.when(k == nk - 1)
        def _():
            o_ref[...] = acc_ref[...].astype(BF16)

    return pl.pallas_call(
        body, name=name, grid=(N // tn, nk),
        in_specs=[pl.BlockSpec((tk, K), lambda j, k: (k, 0)), pl.BlockSpec((tk, tn), lambda j, k: (k, j))],
        out_specs=pl.BlockSpec((K, tn), lambda j, k: (0, j)),
        out_shape=jax.ShapeDtypeStruct((K, N), BF16),
        scratch_shapes=[pltpu.VMEM((K, tn), F32)],
        compiler_params=_cp("parallel", "arbitrary"),
    )(a, b)


def _rms_bwd(dy, xv, g):
    r = lax.rsqrt(jnp.mean(xv * xv, axis=-1, keepdims=True) + EPS)
    t = dy * g
    dx = r * t - xv * (r * r * r * jnp.mean(t * xv, axis=-1, keepdims=True))
    return dx, dy * xv * r


def _ffn_down_bwd_call(dout_b, wd, gu, tm=512):
    T = dout_b.shape[0]

    def body(d_ref, wd_ref, fg_ref, fu_ref, o_ref):
        dact = _dot_nt(d_ref[...], wd_ref[...])
        o_ref[:, :D_FF] = (dact * fg_ref[...].astype(F32)).astype(BF16)
        o_ref[:, D_FF:] = (dact * fu_ref[...].astype(F32)).astype(BF16)

    return pl.pallas_call(
        body, name="ffn_down_bwd", grid=(T // tm,),
        in_specs=[_rows(tm, D_MODEL), _full(wd.shape), _rows(tm, D_FF, 0), _rows(tm, D_FF, 1)],
        out_specs=_rows(tm, 2 * D_FF),
        out_shape=jax.ShapeDtypeStruct((T, 2 * D_FF), BF16),
        compiler_params=_cp("parallel"),
    )(dout_b, wd, gu, gu)


def _ffn_in_bwd_call(dgu, w, h, g_ffn, dout, tm=512):
    T = h.shape[0]

    def body(dgu_ref, w_ref, h_ref, g_ref, d_ref, dh_ref, dhb_ref, dg_ref):
        dhn = _dot_nt(dgu_ref[...], w_ref[...])
        dx, dgt = _rms_bwd(dhn, h_ref[...], g_ref[...])
        dh = d_ref[...] + dx
        dh_ref[...] = dh
        dhb_ref[...] = dh.astype(BF16)

        @pl.when(pl.program_id(0) == 0)
        def _():
            dg_ref[...] = jnp.zeros_like(dg_ref)

        dg_ref[...] += jnp.sum(dgt, axis=0, keepdims=True)

    row = _rows(tm, D_MODEL)
    return pl.pallas_call(
        body, name="ffn_in_bwd", grid=(T // tm,),
        in_specs=[_rows(tm, 2 * D_FF), _full(w.shape), row, _full((1, D_MODEL)), row],
        out_specs=[row, row, _full((1, D_MODEL))],
        out_shape=[jax.ShapeDtypeStruct((T, D_MODEL), F32), jax.ShapeDtypeStruct((T, D_MODEL), BF16),
                   jax.ShapeDtypeStruct((1, D_MODEL), F32)],
        compiler_params=_cp("arbitrary"),
    )(dgu, w, h, g_ffn, dout)


def _merge_bwd_call(dh_b, wm, wc, wa, u, yc, ya, tm=512):
    T = dh_b.shape[0]

    def body(dh_ref, wm_ref, wc_ref, wa_ref, gc_ref, ga_ref, yc_ref, ya_ref, dyc_ref, dya_ref, dug_ref, dcp_ref, do_ref):
        dz = _dot_nt(dh_ref[...], wm_ref[...])
        sgc = _sigmoid(gc_ref[...].astype(F32))
        sga = _sigmoid(ga_ref[...].astype(F32))
        dyc = (dz * sgc).astype(BF16)
        dya = (dz * sga).astype(BF16)
        dyc_ref[...] = dyc
        dya_ref[...] = dya
        dug_ref[:, :D_MODEL] = (dz * yc_ref[...].astype(F32) * (sgc * (1.0 - sgc))).astype(BF16)
        dug_ref[:, D_MODEL:] = (dz * ya_ref[...].astype(F32) * (sga * (1.0 - sga))).astype(BF16)
        dcp_ref[...] = _dot_nt(dyc, wc_ref[...])
        do_ref[...] = _dot_nt(dya, wa_ref[...]).astype(BF16)

    sq = _full((D_MODEL, D_MODEL))
    row = _rows(tm, D_MODEL)
    b16 = jax.ShapeDtypeStruct((T, D_MODEL), BF16)
    return pl.pallas_call(
        body, name="merge_bwd", grid=(T // tm,),
        in_specs=[row, sq, sq, sq, _rows(tm, D_MODEL, COL_GC // D_MODEL), _rows(tm, D_MODEL, COL_GA // D_MODEL), row, row],
        out_specs=[row, row, _rows(tm, 2 * D_MODEL), row, row],
        out_shape=[b16, b16, jax.ShapeDtypeStruct((T, 2 * D_MODEL), BF16), jax.ShapeDtypeStruct((T, D_MODEL), F32), b16],
        compiler_params=_cp("parallel"),
    )(dh_b, wm, wc, wa, u, u, yc, ya)


def _conv_bwd_call(dcp, hc, u, conv_w, ln_g, ln_b, B, S, ts=256):
    T = B * S
    nS = S // ts
    per32 = ts // HALO
    last32 = T // HALO - 1

    def ln_bwd(dcp_v, hc_v, g, bb):
        mu = jnp.mean(hc_v, axis=-1, keepdims=True)
        xc = hc_v - mu
        rstd = lax.rsqrt(jnp.mean(xc * xc, axis=-1, keepdims=True) + EPS)
        xhat = xc * rstd
        y = xhat * g + bb
        sy = _sigmoid(y)
        dy = dcp_v * (sy * (1.0 + y * (1.0 - sy)))
        dxh = dy * g
        dhc = rstd * (dxh - jnp.mean(dxh, axis=-1, keepdims=True) - xhat * jnp.mean(dxh * xhat, axis=-1, keepdims=True))
        return dhc, dy * xhat, dy

    def body(dcp_ref, dcpn_ref, hc_ref, hcn_ref, a_ref, gate_ref, w_ref, g_ref, bb_ref,
             du_ref, dw_ref, db_ref, dlg_ref, dlb_ref, dhext, glu_scr, dhsh, acc_ref):
        b = pl.program_id(0)
        s = pl.program_id(1)

        @pl.when((b == 0) & (s == 0))
        def _():
            dw_ref[...] = jnp.zeros_like(dw_ref)
            db_ref[...] = jnp.zeros_like(db_ref)
            dlg_ref[...] = jnp.zeros_like(dlg_ref)
            dlb_ref[...] = jnp.zeros_like(dlb_ref)

        dhc, dgt, dbt = ln_bwd(dcp_ref[...], hc_ref[...], g_ref[...], bb_ref[...])
        dhn, _, _ = ln_bwd(dcpn_ref[...], hcn_ref[...], g_ref[...], bb_ref[...])
        dhext[0:ts, :] = dhc
        dhext[ts:, :] = jnp.where(s < nS - 1, dhn, 0.0)
        db_ref[...] += jnp.sum(dhc, axis=0, keepdims=True)
        dlg_ref[...] += jnp.sum(dgt, axis=0, keepdims=True)
        dlb_ref[...] += jnp.sum(dbt, axis=0, keepdims=True)

        sg = _sigmoid(gate_ref[...].astype(F32))
        av = a_ref[...].astype(F32)
        glu_scr[...] = av * sg
        _fill_shifted(dhext, dhsh)

        def colblock(cb, carry):
            cols = pl.ds(pl.multiple_of(cb * 128, 128), 128)
            outs = _conv_taps(w_ref, dhext, dhsh, 0, ts, cb, reverse=True)
            for rb, acc in enumerate(outs):
                acc_ref[rb * 128:(rb + 1) * 128, cols] = acc
            taps = [jnp.zeros((8, 128), F32) for _ in range(CONV_WIDTH)]
            for r0 in range(0, ts, 128):
                glu = [glu_scr[r0 + 8 * i:r0 + 8 * (i + 1), cols] for i in range(16)]
                for s8 in range(8):
                    users = {}
                    for j in range(CONV_WIDTH):
                        if (CONV_WIDTH - 1 - j) % 8 == s8:
                            for i in range(16):
                                users.setdefault(CONV_WIDTH - 1 - j + r0 + 8 * i - s8, []).append((j, i))
                    for row, pairs in sorted(users.items()):
                        dhc8 = dhext[row:row + 8, cols] if s8 == 0 else dhsh[s8 - 1, row:row + 8, cols]
                        for j, i in pairs:
                            taps[j] = taps[j] + glu[i] * dhc8
            for j in range(CONV_WIDTH):
                dw_ref[8 * j:8 * (j + 1), cols] += taps[j]
            return carry

        lax.fori_loop(0, C_CONV // 128, colblock, 0)
        dglu = acc_ref[...]
        du_ref[:, :C_CONV] = (dglu * sg).astype(BF16)
        du_ref[:, C_CONV:] = (dglu * av * (sg * (1.0 - sg))).astype(BF16)

    cur = lambda cb: pl.BlockSpec((ts, C_CONV), lambda b, s: (b * nS + s, cb))
    nxt = pl.BlockSpec((HALO, C_CONV), lambda b, s: (jnp.minimum(b * (S // HALO) + (s + 1) * per32, last32), 0))
    vec = _full((1, C_CONV))
    return pl.pallas_call(
        body, name="conv_bwd", grid=(B, nS),
        in_specs=[cur(0), nxt, cur(0), nxt, cur(0), cur(1), _full((CONV_WIDTH, C_CONV)), vec, vec],
        out_specs=[pl.BlockSpec((ts, 2 * C_CONV), lambda b, s: (b * nS + s, 0)), _full((8 * CONV_WIDTH, C_CONV)), vec, vec, vec],
        out_shape=[jax.ShapeDtypeStruct((T, 2 * C_CONV), BF16), jax.ShapeDtypeStruct((8 * CONV_WIDTH, C_CONV), F32),
                   jax.ShapeDtypeStruct((1, C_CONV), F32), jax.ShapeDtypeStruct((1, C_CONV), F32),
                   jax.ShapeDtypeStruct((1, C_CONV), F32)],
        scratch_shapes=[pltpu.VMEM((ts + HALO, C_CONV), F32), pltpu.VMEM((ts, C_CONV), F32),
                        pltpu.VMEM((7, ts + HALO, C_CONV), F32), pltpu.VMEM((ts, C_CONV), F32)],
        compiler_params=_cp("arbitrary", "arbitrary"),
    )(dcp, dcp, hc, hc, u, u, conv_w, ln_g, ln_b)


def _attn_bwd_call(u, do, qg2, kg2, sinks, slopes, B, S):
    T = B * S
    nb = S // BLOCK
    half = Q_W // 2
    npair = GROUP // 2

    def body(sinks_ref, slopes_ref, q_ref, k_ref, v_ref, do_ref, qg_ref, kg_ref,
             dq_ref, dkv_ref, dqg_ref, dkg_ref, dsk_ref, kA, kB, vA, vB, dkn_acc, dv_acc, dkv_keep,
             btab, s_buf, dp_buf, p_buf, ds_buf, qn_buf, do_buf, rq_buf):
        b = pl.program_id(0)
        kvh = pl.program_id(1)
        lo = _lane_lo()
        lane = lax.broadcasted_iota(jnp.int32, (1, 128), 1)
        first = kvh == 0

        @pl.when((b == 0) & first)
        def _():
            dqg_ref[...] = jnp.zeros_like(dqg_ref)
            dkg_ref[...] = jnp.zeros_like(dkg_ref)
            dsk_ref[...] = jnp.zeros_like(dsk_ref)

        _stage_kv(kvh, k_ref, v_ref, kg_ref, lo, kA, kB, vA, vB)
        dkn_acc[...] = jnp.zeros_like(dkn_acc)
        dv_acc[...] = jnp.zeros_like(dv_acc)
        _fill_bias(btab, kvh, slopes_ref)
        qgs = qg_ref[...] * ATTN_SCALE

        def fold(acc):
            canon = jnp.where(lo, acc[0:2 * BLOCK] + pltpu.roll(acc[2 * BLOCK:], HEAD_DIM, 1), 0.0)
            return jnp.where(first, canon, pltpu.roll(canon, HEAD_DIM, 1))

        def block(n, slot):
            s_scr, dp_scr, p_scr, ds_scr = s_buf.at[slot], dp_buf.at[slot], p_buf.at[slot], ds_buf.at[slot]
            qn_scr, do_scr, rq_scr = qn_buf.at[slot], do_buf.at[slot], rq_buf.at[slot]
            r0 = pl.multiple_of(n * BLOCK, BLOCK)
            win = pl.ds(r0, 2 * BLOCK)
            rows = pl.ds(r0, BLOCK)
            k2t = jnp.concatenate([kA[win, :], kB[win, :]], axis=0)
            v2 = jnp.concatenate([vA[win, :], vB[win, :]], axis=0)
            var = jnp.minimum(n, 1)
            for p in range(npair):
                pc = slice(128 * p, 128 * (p + 1))
                qp = q_ref[rows, pc].astype(F32)
                rq = _head_rstd(qp, lo)
                rq_scr[pc, :] = rq
                qn_scr[pc, :] = (qp * rq * qgs).astype(BF16)
                do_scr[pc, :] = do_ref[rows, pc]
            s_scr[...] = _dot_nt(k2t, qn_scr[...])
            dp_scr[...] = _dot_nt(v2, do_scr[...])
            for p in range(npair):
                for hf in range(2):
                    g = 2 * p + hf
                    keys, qs = slice(2 * BLOCK * hf, 2 * BLOCK * (hf + 1)), slice(128 * p, 128 * (p + 1))
                    pn, esk, rz = _softmax_cols(s_scr, btab, p, var, g, hf, sinks_ref[kvh * GROUP + g])
                    dph = dp_scr[keys, qs]
                    delta = jnp.sum(pn * dph, axis=0, keepdims=True)
                    ds_scr[keys, qs] = (pn * (dph - delta)).astype(BF16)
                    p_scr[keys, qs] = pn.astype(BF16)
                    dsk_ref[...] += jnp.where(lane == kvh * GROUP + g, -jnp.sum(esk * rz * delta), 0.0)
            dqs_all = _dot_tn(k2t, ds_scr[...]).T
            for p in range(npair):
                pc = slice(128 * p, 128 * (p + 1))
                qp = q_ref[rows, pc].astype(F32)
                rq = rq_scr[pc, :]
                dqs = dqs_all[pc, :]
                t = dqs * qgs
                dq = rq * t - qp * (rq * rq * rq * (_half_sums(t * qp, lo) * (1.0 / HEAD_DIM)))
                dq_ref[rows, pc] = dq.astype(BF16)
                dqg_ref[...] += jnp.sum(dqs * qp * rq, axis=0, keepdims=True) * ATTN_SCALE
            dkn_acc[win, :] += fold(_dot(ds_scr[...], qn_scr[...]))
            dv_acc[win, :] += fold(_dot(p_scr[...], do_scr[...]))

        def two_blocks(i, carry):
            block(2 * i, 0)
            block(2 * i + 1, 1)
            return carry

        lax.fori_loop(0, nb // 2, two_blocks, 0)

        dkn = dkn_acc[BLOCK:, :]
        kv = k_ref[...].astype(F32)
        kg = kg_ref[...]
        rk = _head_rstd(kv, lo)
        t = dkn * kg
        dk = rk * t - kv * (rk * rk * rk * (_half_sums(t * kv, lo) * (1.0 / HEAD_DIM)))
        dkg_ref[...] += jnp.sum(dkn * kv * rk, axis=0, keepdims=True)
        dv = dv_acc[BLOCK:, :]

        @pl.when(first)
        def _():
            dkv_keep[:, :128] = dk
            dkv_keep[:, 128:] = dv

        @pl.when(kvh == N_KV_HEADS - 1)
        def _():
            dkv_ref[:, :128] = (dkv_keep[:, :128] + dk).astype(BF16)
            dkv_ref[:, 128:] = (dkv_keep[:, 128:] + dv).astype(BF16)

    smem = pl.BlockSpec(memory_space=pltpu.SMEM)
    kvbuf = pltpu.VMEM((S + BLOCK, 128), BF16)
    accbuf = pltpu.VMEM((S + BLOCK, 128), F32)
    vec = _full((1, 128))
    return pl.pallas_call(
        body, name="attn_bwd", grid=(B, N_KV_HEADS),
        in_specs=[smem, smem,
                  pl.BlockSpec((S, half), lambda b, h: (b, COL_Q // half + h)),
                  pl.BlockSpec((S, 128), lambda b, h: (b, COL_K // 128)),
                  pl.BlockSpec((S, 128), lambda b, h: (b, COL_V // 128)),
                  pl.BlockSpec((S, half), lambda b, h: (b, h)),
                  vec, vec],
        out_specs=[pl.BlockSpec((S, half), lambda b, h: (b, h)), pl.BlockSpec((S, 256), lambda b, h: (b, 0)), vec, vec, vec],
        out_shape=[jax.ShapeDtypeStruct((T, Q_W), BF16), jax.ShapeDtypeStruct((T, 2 * KV_W), BF16),
                   jax.ShapeDtypeStruct((1, 128), F32), jax.ShapeDtypeStruct((1, 128), F32), jax.ShapeDtypeStruct((1, 128), F32)],
        scratch_shapes=[kvbuf, kvbuf, kvbuf, kvbuf, accbuf, accbuf, pltpu.VMEM((S, 256), F32),
                        pltpu.VMEM((2, GROUP, 2 * BLOCK, BLOCK), F32),
                        pltpu.VMEM((2, npair * BLOCK, 4 * BLOCK), F32), pltpu.VMEM((2, npair * BLOCK, 4 * BLOCK), F32),
                        pltpu.VMEM((2, npair * BLOCK, 4 * BLOCK), BF16), pltpu.VMEM((2, npair * BLOCK, 4 * BLOCK), BF16),
                        pltpu.VMEM((2, npair * BLOCK, 128), BF16), pltpu.VMEM((2, npair * BLOCK, 128), BF16),
                        pltpu.VMEM((2, npair * BLOCK, 128), F32)],
        compiler_params=_cp("arbitrary", "arbitrary"),
    )(sinks, slopes, u, u, u, do, qg2, kg2)


def _inproj_bwd_call(du_conv, du_g, du_q, du_kv, w, x, g, dh, tm=512):
    T = x.shape[0]

    def body(dc_ref, dg_ref, dq_ref, dkv_ref, w_ref, x_ref, g_ref, dh_ref, gx_ref, dgm_ref):
        dxn = _dot_nt(dc_ref[...], w_ref[:, COL_A:COL_GC])
        dxn += _dot_nt(dg_ref[...], w_ref[:, COL_GC:COL_Q])
        dxn += _dot_nt(dq_ref[...], w_ref[:, COL_Q:COL_K])
        dxn += _dot_nt(dkv_ref[...], w_ref[:, COL_K:])
        dx, dgt = _rms_bwd(dxn, x_ref[...], g_ref[...])
        gx_ref[...] = dh_ref[...] + dx

        @pl.when(pl.program_id(0) == 0)
        def _():
            dgm_ref[...] = jnp.zeros_like(dgm_ref)

        dgm_ref[...] += jnp.sum(dgt, axis=0, keepdims=True)

    row = _rows(tm, D_MODEL)
    return pl.pallas_call(
        body, name="inproj_bwd", grid=(T // tm,),
        in_specs=[_rows(tm, 2 * C_CONV), _rows(tm, 2 * D_MODEL), _rows(tm, Q_W), _rows(tm, 2 * KV_W), _full(w.shape),
                  row, _full((1, D_MODEL)), row],
        out_specs=[row, _full((1, D_MODEL))],
        out_shape=[jax.ShapeDtypeStruct((T, D_MODEL), F32), jax.ShapeDtypeStruct((1, D_MODEL), F32)],
        compiler_params=_cp("arbitrary"),
    )(du_conv, du_g, du_q, du_kv, w, x, g, dh)


def _adamw_math(w, g, m, v):
    m = ADAM_B1 * m + (1.0 - ADAM_B1) * g
    v = ADAM_B2 * v + (1.0 - ADAM_B2) * (g * g)
    m_hat = m / (1.0 - ADAM_B1 ** ADAM_STEP)
    v_hat = v / (1.0 - ADAM_B2 ** ADAM_STEP)
    delta = -ADAM_LR * (m_hat / (jnp.sqrt(v_hat) + ADAM_EPS) + ADAM_WD * w)
    return delta, m, v


def _adamw_sum_call(name, parts, own, w, m, v, tr):
    R, C = w.shape

    def body(p_ref, own_ref, w_ref, m_ref, v_ref, g_ref, d_ref, nm_ref, nv_ref):
        x, y, c = _position()
        me = 4 * x + 2 * y + c
        g = jnp.zeros((tr, C), F32)
        for s in range(N_DEV):
            g = g + jnp.where(me == s, own_ref[...], p_ref[s]).astype(F32)
        g_ref[...] = g
        d_ref[...], nm_ref[...], nv_ref[...] = _adamw_math(w_ref[...], g, m_ref[...], v_ref[...])

    blk = pl.BlockSpec((tr, C), lambda i: (i, 0))
    out = jax.ShapeDtypeStruct((R, C), F32)
    return pl.pallas_call(
        body, name=name, grid=(R // tr,),
        in_specs=[pl.BlockSpec((N_DEV, tr, C), lambda i: (0, i, 0)), blk, blk, blk, blk],
        out_specs=[blk] * 4, out_shape=[out] * 4,
        compiler_params=_cp("parallel"),
    )(parts, own, w, m, v)


def _adamw_small_call(name, g, w, m, v):
    def body(g_ref, w_ref, m_ref, v_ref, d_ref, nm_ref, nv_ref):
        d_ref[...], nm_ref[...], nv_ref[...] = _adamw_math(w_ref[...], g_ref[...], m_ref[...], v_ref[...])

    out = jax.ShapeDtypeStruct(w.shape, F32)
    return pl.pallas_call(body, name=name, out_shape=[out] * 3)(g, w, m, v)


def _position():
    return lax.axis_index("x"), lax.axis_index("y"), lax.axis_index("c")


def _peer(x, y, c, rel):
    fx, fy, fc = (rel >> 2) & 1, (rel >> 1) & 1, rel & 1
    px = 1 - x if fx else x
    py = 1 - y if fy else y
    pc = 1 - c if fc else c
    return (px, py, pc), 4 * px + 2 * py + pc


def _gather_weights_call(shards):
    n = len(shards)

    def body(*refs):
        ins, outs = refs[:n], refs[n:2 * n]
        send_sems, recv_sems, local_sems = refs[2 * n:]
        x, y, c = _position()
        me = 4 * x + 2 * y + c
        sib, sib_id = _peer(x, y, c, 1)
        chips = [_peer(x, y, c, rel) for rel in (4, 2, 6)]

        def copy(i, k, src, slot, to):
            return pltpu.make_async_remote_copy(
                src_ref=src, dst_ref=outs[i].at[slot], send_sem=send_sems.at[i, k], recv_sem=recv_sems.at[i, k],
                device_id=to, device_id_type=MESH)

        mine = [pltpu.make_async_copy(ins[i], outs[i].at[me], local_sems.at[i]) for i in range(n)]
        for cp in mine:
            cp.start()
        first = []
        for i in range(n):
            first.append(copy(i, 0, ins[i], me, sib))
            first += [copy(i, 1 + j, ins[i], me, dev) for j, (dev, _) in enumerate(chips)]
        for cp in first:
            cp.start()
        passed = []
        for i in range(n):
            for j, (dev, slot) in enumerate(chips):
                copy(i, 1 + j, ins[i], slot, dev).wait_recv()
                fwd = copy(i, 4 + j, outs[i].at[slot], slot, sib)
                fwd.start()
                passed.append(fwd)
        for i in range(n):
            copy(i, 0, ins[i], sib_id, sib).wait_recv()
            for j, (_, slot) in enumerate(chips):
                copy(i, 4 + j, ins[i], slot ^ 1, sib).wait_recv()
        for cp in first + passed:
            cp.wait_send()
        for cp in mine:
            cp.wait()

    hbm = pl.BlockSpec(memory_space=pl.ANY)
    return pl.pallas_call(
        body, name="gather_weights",
        in_specs=[hbm] * n, out_specs=[hbm] * n,
        out_shape=[jax.ShapeDtypeStruct((N_DEV,) + s.shape, s.dtype) for s in shards],
        scratch_shapes=[pltpu.SemaphoreType.DMA((n, 7)), pltpu.SemaphoreType.DMA((n, 7)), pltpu.SemaphoreType.DMA((n,))],
    )(*shards)


_HBM = pl.BlockSpec(memory_space=pltpu.HBM)
_SEM = pl.BlockSpec(memory_space=pltpu.SEMAPHORE)
_EFFECT = pltpu.SideEffectType.DATAFLOW_SIDE_EFFECTING


def _exchange_copies(scatter, incoming, src_refs, land_refs, send_sems, recv_sems):
    x, y, c = _position()
    me = 4 * x + 2 * y + c
    out = []
    for i, (src, land) in enumerate(zip(src_refs, land_refs)):
        for k in range(N_DEV - 1):
            dev, dev_id = _peer(x, y, c, k + 1)
            j = i * (N_DEV - 1) + k
            out.append(pltpu.make_async_remote_copy(
                src_ref=src.at[dev_id] if scatter else src, dst_ref=land.at[dev_id if incoming else me],
                send_sem=send_sems.at[j], recv_sem=recv_sems.at[j], device_id=dev, device_id_type=MESH))
    return out


def _exchange_start_call(name, arrays, scatter):
    n = len(arrays)
    lands = [lax.empty((N_DEV,) + (a.shape[1:] if scatter else a.shape), a.dtype) for a in arrays]

    def body(*refs):
        src_refs, land_refs = refs[:n], refs[n:2 * n]
        send_sems, recv_sems, token = refs[2 * n], refs[2 * n + 1], refs[-1]
        for copy in _exchange_copies(scatter, False, src_refs, land_refs, send_sems, recv_sems):
            copy.start()
        token[...] = jnp.zeros_like(token)

    sems = pltpu.SemaphoreType.DMA((n * (N_DEV - 1),))
    res = pl.pallas_call(
        body, name=name,
        out_shape=[sems, sems] + [pltpu.HBM(a.shape, a.dtype) for a in arrays] + [pltpu.HBM(l.shape, l.dtype) for l in lands]
        + [jax.ShapeDtypeStruct((8, 128), F32)],
        in_specs=[_HBM] * (2 * n), out_specs=[_SEM, _SEM] + [_HBM] * (2 * n) + [pl.BlockSpec(memory_space=pltpu.VMEM)],
        input_output_aliases={i: 2 + i for i in range(2 * n)},
        compiler_params=pltpu.CompilerParams(has_side_effects=_EFFECT),
    )(*[pltpu.with_memory_space_constraint(a, pltpu.HBM) for a in list(arrays) + lands])
    return res[0], res[1], res[2:2 + n], res[2 + n:2 + 2 * n], res[-1]


def _exchange_wait_call(name, started, scatter, after):
    send_sems, recv_sems, srcs, lands, _ = started
    n = len(srcs)

    def body(*refs):
        src_refs, land_refs = refs[:n], refs[n:2 * n]
        send_s, recv_s = refs[2 * n], refs[2 * n + 1]
        for copy in _exchange_copies(scatter, False, src_refs, land_refs, send_s, recv_s):
            copy.wait_send()
        for copy in _exchange_copies(scatter, True, src_refs, land_refs, send_s, recv_s):
            copy.wait_recv()

    res = pl.pallas_call(
        body, name=name,
        out_shape=[pltpu.HBM(a.shape, a.dtype) for a in list(srcs) + list(lands)],
        in_specs=[_HBM] * (2 * n) + [_SEM, _SEM, pl.BlockSpec(memory_space=pl.ANY)], out_specs=[_HBM] * (2 * n),
        input_output_aliases={i: i for i in range(2 * n)},
        compiler_params=pltpu.CompilerParams(has_side_effects=_EFFECT),
    )(*srcs, *lands, send_sems, recv_sems, after)
    return res[n:]


def _allsum_small_call(part):
    P = part.shape[0]

    def body(p_ref, o_ref, all_ref, send_sems, recv_sems):
        x, y, c = _position()
        me = 4 * x + 2 * y + c
        peers = [_peer(x, y, c, rel) for rel in range(1, N_DEV)]
        all_ref[me] = p_ref[...]

        def copy(k, slot, dev):
            return pltpu.make_async_remote_copy(
                src_ref=p_ref, dst_ref=all_ref.at[slot], send_sem=send_sems.at[k], recv_sem=recv_sems.at[k],
                device_id=dev, device_id_type=MESH)

        sends = [copy(k, me, dev) for k, (dev, _) in enumerate(peers)]
        for cp in sends:
            cp.start()
        for k, (dev, dev_id) in enumerate(peers):
            copy(k, dev_id, dev).wait_recv()
        for cp in sends:
            cp.wait_send()
        tot = all_ref[0]
        for s in range(1, N_DEV):
            tot = tot + all_ref[s]
        o_ref[...] = tot

    vm = pl.BlockSpec(memory_space=pltpu.VMEM)
    return pl.pallas_call(
        body, name="allsum_small", in_specs=[vm], out_specs=vm,
        out_shape=jax.ShapeDtypeStruct(part.shape, F32),
        scratch_shapes=[pltpu.VMEM((N_DEV, P, 128), F32), pltpu.SemaphoreType.DMA((7,)), pltpu.SemaphoreType.DMA((7,))],
    )(part)


def _pad_rows(v, rows):
    flat = v.reshape(-1)
    return jnp.pad(flat, (0, rows * 128 - flat.shape[0])).reshape(rows, 128)


def _blocks_by_cols(w):
    K, N = w.shape
    return w.reshape(K, N_DEV, N // N_DEV).transpose(1, 0, 2)


def _cols_from_blocks(wb):
    n, K, C = wb.shape
    return wb.transpose(1, 0, 2).reshape(K, n * C)


def _local_step(x, tgt, w_in_i, conv_w, conv_b, ln_g, ln_b, qg, kg, sinks, g_mix, g_ffn, token0, rest_weights, grads_ready, B, S):
    row = lambda v: v.reshape(1, -1)
    qg2 = jnp.tile(qg, 2).reshape(1, 128)
    kg2 = jnp.tile(kg, 2).reshape(1, 128)
    heads = jnp.arange(1, N_Q_HEADS + 1, dtype=F32)
    slopes = jnp.exp2(-8.0 * heads / N_Q_HEADS)

    u, xn = _inproj_call(x, row(g_mix) + token0, w_in_i)
    hc, c = _conv_fwd_call(u, conv_w, row(conv_b), row(ln_g), row(ln_b), B, S)
    o = _attn_fwd_call(u, qg2, kg2, sinks, slopes, B, S)
    wc, wa, wm, w_ffn_in, wd = rest_weights(o)
    yc, ya, z, h, hn = _merge_call(c, o, u, x, wc, wa, wm, row(g_ffn))
    gu, act = _ffn_in_call(hn, w_ffn_in)
    sse, dout, dout_b = _ffn_out_call(act, wd, h, tgt)

    dgu = _ffn_down_bwd_call(dout_b, wd, gu)
    g_wd = _dw_call("dw_ffn_down", act, dout_b, tn=512)
    g_wffn = _dw_call("dw_ffn_in", hn, dgu, tn=2 * D_FF // 4)
    tok = grads_ready(dict(w_ffn_down=g_wd, w_ffn_in=g_wffn))
    dh, dh_b, g_gffn = _ffn_in_bwd_call(dgu, w_ffn_in, h, row(g_ffn) + tok, dout)
    dyc, dya, du_g, dcp, do = _merge_bwd_call(dh_b, wm, wc, wa, u, yc, ya)
    g_wm = _dw_call("dw_merge", z, dh_b, tn=D_MODEL)
    g_wc = _dw_call("dw_conv_out", c, dyc, tn=D_MODEL)
    g_wa = _dw_call("dw_attn_out", o, dya, tn=D_MODEL)
    tok = grads_ready(dict(w_merge_out=g_wm, w_conv_out=g_wc, w_attn_out=g_wa))
    du_conv, g_cw8, g_cb, g_lg, g_lb = _conv_bwd_call(dcp, hc, u, conv_w, row(ln_g) + tok, row(ln_b), B, S)
    du_q, du_kv, g_qg2, g_kg2, g_sk = _attn_bwd_call(u, do, qg2, kg2, sinks, slopes, B, S)
    g_win = jnp.concatenate([
        _dw_call("dw_in_conv", xn, du_conv, tn=D_MODEL),
        _dw_call("dw_in_q", xn, du_q, tn=D_MODEL),
        _dw_call("dw_in_kv", xn, du_kv, tn=2 * KV_W),
        _dw_call("dw_in_gates", xn, du_g, tn=D_MODEL)], axis=1)
    tok = grads_ready(dict(w_in=g_win))
    grad_x, g_gmix = _inproj_bwd_call(du_conv, du_g, du_q, du_kv, w_in_i, x, row(g_mix) + tok, dh)

    g_cw = g_cw8.reshape(CONV_WIDTH, 8, C_CONV).sum(axis=1)
    g_qg = g_qg2[0, :HEAD_DIM] + g_qg2[0, HEAD_DIM:]
    g_kg = g_kg2[0, :HEAD_DIM] + g_kg2[0, HEAD_DIM:]
    small = dict(norm_mix_g=g_gmix[0], conv_dw_b=g_cb[0], conv_ln_g=g_lg[0], conv_ln_b=g_lb[0], q_norm_g=g_qg,
                 k_norm_g=g_kg, sinks=g_sk[0, :N_Q_HEADS], norm_ffn_g=g_gffn[0], conv_dw_w=g_cw)
    return sse[0, 0], grad_x, small


def _to_internal_cols(w):
    o1 = 2 * C_CONV
    o4 = o1 + Q_W + 2 * KV_W
    return jnp.concatenate([w[:, :o1], w[:, o4:], w[:, o1:o4]], axis=1)


_COL_SHARDED = ("w_in", "w_ffn_in")

_SMALL = (("norm_mix_g", 8), ("conv_dw_b", 8), ("conv_ln_g", 8), ("conv_ln_b", 8), ("norm_ffn_g", 8),
          ("q_norm_g", 1), ("k_norm_g", 1), ("sinks", 1))
_SMALL_ROWS = 48
_CW_ROWS = CONV_WIDTH * C_CONV // 128


def _pack_small(d):
    rows = [_pad_rows(d[name], r) for name, r in _SMALL]
    used = sum(r for _, r in _SMALL)
    return jnp.concatenate(rows + [jnp.zeros((_SMALL_ROWS - used, 128), F32)], axis=0)


def _unpack_small(packed, like):
    out, r0 = {}, 0
    for name, r in _SMALL:
        n = like[name].size
        out[name] = packed[r0:r0 + r].reshape(-1)[:n].reshape(like[name].shape)
        r0 += r
    return out


def kernel(x, norm_mix_g, w_in, conv_dw_w, conv_dw_b, conv_ln_g, conv_ln_b, w_conv_out, q_norm_g, k_norm_g, sinks, w_attn_out, w_merge_out, norm_ffn_g, w_ffn_in, w_ffn_down, loss_target, m_norm_mix_g, m_w_in, m_conv_dw_w, m_conv_dw_b, m_conv_ln_g, m_conv_ln_b, m_w_conv_out, m_q_norm_g, m_k_norm_g, m_sinks, m_w_attn_out, m_w_merge_out, m_norm_ffn_g, m_w_ffn_in, m_w_ffn_down, v_norm_mix_g, v_w_in, v_conv_dw_w, v_conv_dw_b, v_conv_ln_g, v_conv_ln_b, v_w_conv_out, v_q_norm_g, v_k_norm_g, v_sinks, v_w_attn_out, v_w_merge_out, v_norm_ffn_g, v_w_ffn_in, v_w_ffn_down):
    names = ["norm_mix_g", "w_in", "conv_dw_w", "conv_dw_b", "conv_ln_g", "conv_ln_b", "w_conv_out", "q_norm_g",
             "k_norm_g", "sinks", "w_attn_out", "w_merge_out", "norm_ffn_g", "w_ffn_in", "w_ffn_down"]
    w = dict(zip(names, (norm_mix_g, w_in, conv_dw_w, conv_dw_b, conv_ln_g, conv_ln_b, w_conv_out, q_norm_g, k_norm_g,
                         sinks, w_attn_out, w_merge_out, norm_ffn_g, w_ffn_in, w_ffn_down)))
    m = dict(zip(names, (m_norm_mix_g, m_w_in, m_conv_dw_w, m_conv_dw_b, m_conv_ln_g, m_conv_ln_b, m_w_conv_out,
                         m_q_norm_g, m_k_norm_g, m_sinks, m_w_attn_out, m_w_merge_out, m_norm_ffn_g, m_w_ffn_in,
                         m_w_ffn_down)))
    v = dict(zip(names, (v_norm_mix_g, v_w_in, v_conv_dw_w, v_conv_dw_b, v_conv_ln_g, v_conv_ln_b, v_w_conv_out,
                         v_q_norm_g, v_k_norm_g, v_sinks, v_w_attn_out, v_w_merge_out, v_norm_ffn_g, v_w_ffn_in,
                         v_w_ffn_down)))
    B, S, _ = x.shape
    T = B * S
    me = 4 * lax.axis_index("x") + 2 * lax.axis_index("y") + lax.axis_index("c")

    cw_shard = _pad_rows(conv_dw_w.reshape(CONV_WIDTH, 128), 32)
    g_win, g_cwt = _gather_weights_call([w_in.astype(BF16), cw_shard])
    conv_w_full = g_cwt.reshape(N_DEV, 32 * 128)[:, :CONV_WIDTH * 128].reshape(N_DEV, CONV_WIDTH, 128)
    conv_w_full = conv_w_full.transpose(1, 0, 2).reshape(CONV_WIDTH, C_CONV)
    w_in_i = _to_internal_cols(_cols_from_blocks(g_win))
    rest_names = ["w_conv_out", "w_attn_out", "w_merge_out", "w_ffn_in", "w_ffn_down"]
    rest_shards = [w[k].astype(BF16) for k in rest_names]
    rest_started = _exchange_start_call("gather_rest_start", rest_shards, scatter=False)

    def rest_weights(after):
        lands = _exchange_wait_call("gather_rest_wait", rest_started, False, after)
        mine = lax.broadcasted_iota(jnp.int32, (N_DEV, 1, 1), 0) == me
        full = [jnp.where(mine, s[None], l) for l, s in zip(lands, rest_shards)]
        wc_f, wa_f, wm_f = (f.reshape(D_MODEL, D_MODEL) for f in full[:3])
        return wc_f, wa_f, wm_f, _cols_from_blocks(full[3]), full[4].reshape(D_FF, D_MODEL)

    exchanges = []

    def grads_ready(group):
        ks = list(group)
        blocks = [_blocks_by_cols(group[k]) if k in _COL_SHARDED
                  else group[k].reshape(N_DEV, group[k].shape[0] // N_DEV, group[k].shape[1]) for k in ks]
        started = _exchange_start_call("scatter_" + ks[0] + "_start", blocks, scatter=True)
        exchanges.append((ks, started, [lax.dynamic_index_in_dim(b, me, 0, keepdims=False) for b in blocks]))
        return started[-1][0:1, 0:1]

    sse, grad_x, small = _local_step(
        x.reshape(T, D_MODEL), loss_target.reshape(T, D_MODEL), w_in_i, conv_w_full, conv_dw_b, conv_ln_g, conv_ln_b,
        q_norm_g, k_norm_g, sinks, norm_mix_g, norm_ffn_g, rest_started[-1][0:1, 0:1], rest_weights, grads_ready, B, S)

    grad, delta, new_m, new_v = {}, {}, {}, {}
    packed = jnp.concatenate([_pack_small(small), small["conv_dw_w"].reshape(_CW_ROWS, 128),
                              jnp.full((8, 128), sse, F32)], axis=0)
    tot = _allsum_small_call(packed)
    loss = 0.5 / D_MODEL * tot[_SMALL_ROWS + _CW_ROWS, 0]
    g_small = _unpack_small(tot[:_SMALL_ROWS], w)
    d_s, m_s, v_s = _adamw_small_call("adamw_small", tot[:_SMALL_ROWS], _pack_small(w), _pack_small(m), _pack_small(v))
    d_small, m_small, v_small = _unpack_small(d_s, w), _unpack_small(m_s, w), _unpack_small(v_s, w)
    g_cw = lax.dynamic_slice_in_dim(tot[_SMALL_ROWS:_SMALL_ROWS + _CW_ROWS].reshape(CONV_WIDTH, C_CONV), me * 128, 128, axis=1)
    cw2 = lambda t: t.reshape(CONV_WIDTH, 128)
    d_cw, m_cw, v_cw = _adamw_small_call("adamw_conv_w", g_cw, cw2(conv_dw_w), cw2(m_conv_dw_w), cw2(v_conv_dw_w))
    cw3 = lambda t: t.reshape(CONV_WIDTH, 1, 128)
    for k, _ in _SMALL:
        grad[k], delta[k], new_m[k], new_v[k] = g_small[k], d_small[k], m_small[k], v_small[k]
    grad["conv_dw_w"], delta["conv_dw_w"], new_m["conv_dw_w"], new_v["conv_dw_w"] = cw3(g_cw), cw3(d_cw), cw3(m_cw), cw3(v_cw)

    after = d_cw
    for ks, started, owns in exchanges:
        lands = _exchange_wait_call("scatter_" + ks[0] + "_wait", started, True, after)
        for k, land, own in zip(ks, lands, owns):
            tr = {1024: 256, 352: 176, 128: 128}[w[k].shape[0]]
            grad[k], delta[k], new_m[k], new_v[k] = _adamw_sum_call("adamw_" + k, land, own, w[k], m[k], v[k], tr)
            after = new_v[k]

    return (loss, grad_x.reshape(B, S, D_MODEL), *[grad[k] for k in names], *[delta[k] for k in names],
            *[new_m[k] for k in names], *[new_v[k] for k in names])
```

```python
import functools
import math

import jax
import jax.numpy as jnp
from jax import lax
from jax.experimental import pallas as pl
from jax.experimental.pallas import tpu as pltpu

F32 = jnp.float32
BF16 = jnp.bfloat16
MESH = pl.DeviceIdType.MESH

N_DEV = 8
D_MODEL = 1024
C_CONV = 1024
CONV_WIDTH = 31
HEAD_DIM = 64
N_Q_HEADS = 16
N_KV_HEADS = 2
GROUP = N_Q_HEADS // N_KV_HEADS
WINDOW = 128
BLOCK = 128
D_FF = 2816
EPS = 1e-6
NEG = -1e30
ATTN_SCALE = 1.0 / math.sqrt(HEAD_DIM)
Q_W = N_Q_HEADS * HEAD_DIM
KV_W = N_KV_HEADS * HEAD_DIM
IN_COLS = 2 * C_CONV + Q_W + 2 * KV_W + 2 * D_MODEL

ADAM_LR = 0.001
ADAM_B1 = 0.9
ADAM_B2 = 0.999
ADAM_EPS = 1e-08
ADAM_WD = 0.01
ADAM_STEP = 10

COL_A, COL_GATE, COL_GC, COL_GA, COL_Q, COL_K, COL_V = 0, 1024, 2048, 3072, 4096, 5120, 5248

HALO = 32
VMEM_LIMIT = 56 * 1024 * 1024


def _cp(*sem):
    return pltpu.CompilerParams(dimension_semantics=sem, vmem_limit_bytes=VMEM_LIMIT)


def _sigmoid(x):
    return jax.nn.sigmoid(x)


def _dot(a, b):
    return jnp.dot(a, b, preferred_element_type=F32)


def _dot_nt(a, b):
    return lax.dot_general(a, b, (((1,), (1,)), ((), ())), preferred_element_type=F32)


def _dot_tn(a, b):
    return lax.dot_general(a, b, (((0,), (0,)), ((), ())), preferred_element_type=F32)


def _full(shape):
    return pl.BlockSpec(shape, lambda *_: (0,) * len(shape))


def _rows(tm, ncols, colblk=0):
    return pl.BlockSpec((tm, ncols), lambda i: (i, colblk))


def _inproj_call(x, g, w, tm=512):
    T = x.shape[0]
    N = w.shape[1]

    def body(x_ref, g_ref, w_ref, u_ref, xn_ref):
        xv = x_ref[...]
        r = lax.rsqrt(jnp.mean(xv * xv, axis=-1, keepdims=True) + EPS)
        xn = (xv * r * g_ref[...]).astype(BF16)
        xn_ref[...] = xn
        u_ref[...] = _dot(xn, w_ref[...]).astype(BF16)

    return pl.pallas_call(
        body, name="inproj", grid=(T // tm,),
        in_specs=[_rows(tm, D_MODEL), _full((1, D_MODEL)), _full(w.shape)],
        out_specs=[_rows(tm, N), _rows(tm, D_MODEL)],
        out_shape=[jax.ShapeDtypeStruct((T, N), BF16), jax.ShapeDtypeStruct((T, D_MODEL), BF16)],
        compiler_params=_cp("parallel"),
    )(x, g, w)


def _fill_shifted(src_ref, sh_ref):
    n = src_ref.shape[0] - 8
    for s in range(1, 8):
        sh_ref[s - 1, 0:n, :] = src_ref[s:s + n, :]


def _window(src_ref, sh_ref, off, cols):
    s = off % 8
    if s == 0:
        return src_ref[off:off + 128, cols]
    return sh_ref[s - 1, off - s:off - s + 128, cols]


def _conv_taps(w_ref, src_ref, sh_ref, base, rows, cb, reverse):
    cols = pl.ds(pl.multiple_of(cb * 128, 128), 128)
    outs = []
    for rb in range(rows // 128):
        acc = jnp.zeros((128, 128), F32)
        for j in range(CONV_WIDTH):
            off = base + (CONV_WIDTH - 1 - j if reverse else j) + rb * 128
            acc = acc + w_ref[j:j + 1, cols] * _window(src_ref, sh_ref, off, cols)
        outs.append(acc)
    return outs


def _conv_fwd_call(u, conv_w, conv_b, ln_g, ln_b, B, S, ts=512):
    T = B * S
    nS = S // ts
    per32 = ts // HALO

    def body(a_ref, gate_ref, ap_ref, gp_ref, w_ref, b_ref, g_ref, bb_ref, hc_ref, c_ref, hext, hsh, acc_ref):
        s = pl.program_id(1)
        prev = ap_ref[...].astype(F32) * _sigmoid(gp_ref[...].astype(F32))
        hext[0:HALO, :] = jnp.where(s > 0, prev, 0.0)
        hext[HALO:, :] = a_ref[...].astype(F32) * _sigmoid(gate_ref[...].astype(F32))
        _fill_shifted(hext, hsh)

        def colblock(cb, carry):
            cols = pl.ds(pl.multiple_of(cb * 128, 128), 128)
            outs = _conv_taps(w_ref, hext, hsh, HALO - (CONV_WIDTH - 1), ts, cb, reverse=False)
            for rb, acc in enumerate(outs):
                acc_ref[rb * 128:(rb + 1) * 128, cols] = acc
            return carry

        lax.fori_loop(0, C_CONV // 128, colblock, 0)
        hc = acc_ref[...] + b_ref[...]
        hc_ref[...] = hc
        mu = jnp.mean(hc, axis=-1, keepdims=True)
        xc = hc - mu
        var = jnp.mean(xc * xc, axis=-1, keepdims=True)
        y = xc * lax.rsqrt(var + EPS) * g_ref[...] + bb_ref[...]
        c_ref[...] = (y * _sigmoid(y)).astype(BF16)

    cur = lambda cb: pl.BlockSpec((ts, C_CONV), lambda b, s: (b * nS + s, cb))
    prv = lambda cb: pl.BlockSpec((HALO, C_CONV), lambda b, s: (jnp.maximum(b * (S // HALO) + s * per32 - 1, 0), cb))
    vec = _full((1, C_CONV))
    return pl.pallas_call(
        body, name="conv_fwd", grid=(B, nS),
        in_specs=[cur(0), cur(1), prv(0), prv(1), _full((CONV_WIDTH, C_CONV)), vec, vec, vec],
        out_specs=[pl.BlockSpec((ts, C_CONV), lambda b, s: (b * nS + s, 0))] * 2,
        out_shape=[jax.ShapeDtypeStruct((T, C_CONV), F32), jax.ShapeDtypeStruct((T, C_CONV), BF16)],
        scratch_shapes=[pltpu.VMEM((HALO + ts, C_CONV), F32), pltpu.VMEM((7, HALO + ts, C_CONV), F32),
                        pltpu.VMEM((ts, C_CONV), F32)],
        compiler_params=_cp("parallel", "arbitrary"),
    )(u, u, u, u, conv_w, conv_b, ln_g, ln_b)


def _lane_lo():
    return lax.broadcasted_iota(jnp.int32, (1, 128), 1) < HEAD_DIM


def _half_sums(t, lo):
    s_lo = jnp.sum(jnp.where(lo, t, 0.0), axis=-1, keepdims=True)
    s_hi = jnp.sum(jnp.where(lo, 0.0, t), axis=-1, keepdims=True)
    return jnp.where(lo, s_lo, s_hi)


def _head_rstd(t, lo):
    return lax.rsqrt(_half_sums(t * t, lo) * (1.0 / HEAD_DIM) + EPS)


def _fill_bias(btab, kvh, slopes_ref):
    sj = lax.broadcasted_iota(jnp.int32, (2 * BLOCK, BLOCK), 0)
    qi = lax.broadcasted_iota(jnp.int32, (2 * BLOCK, BLOCK), 1)
    dist = qi + BLOCK - sj
    valid = (dist >= 0) & (dist < WINDOW)
    distf = dist.astype(F32)
    for g in range(GROUP):
        bias = jnp.where(valid, -slopes_ref[kvh * GROUP + g] * distf, NEG)
        btab[1, g] = bias
        btab[0, g] = jnp.where(sj >= BLOCK, bias, NEG)


def _softmax_cols(st_ref, btab, p, var, g, hf, sk):
    sh = st_ref[2 * BLOCK * hf:2 * BLOCK * (hf + 1), 128 * p:128 * (p + 1)] + btab[var, g]
    m = jnp.maximum(jnp.max(sh, axis=0, keepdims=True), sk)
    e = jnp.exp(sh - m)
    esk = jnp.exp(sk - m)
    rz = 1.0 / (jnp.sum(e, axis=0, keepdims=True) + esk)
    return e * rz, esk, rz


def _stage_kv(kvh, k_ref, v_ref, kg_ref, lo, kA, kB, vA, vB):
    kv = k_ref[...].astype(F32)
    rk = _head_rstd(kv, lo)
    kn = kv * rk * kg_ref[...]
    mine = lax.broadcasted_iota(jnp.int32, (1, 128), 1) // HEAD_DIM == kvh
    ksel = jnp.where(mine, kn, 0.0)
    vsel = jnp.where(mine, v_ref[...].astype(F32), 0.0)
    krol = pltpu.roll(ksel, HEAD_DIM, 1)
    vrol = pltpu.roll(vsel, HEAD_DIM, 1)
    first = kvh == 0
    zeros = jnp.zeros((BLOCK, 128), BF16)
    for ref, val in ((kA, jnp.where(first, ksel, krol)), (kB, jnp.where(first, krol, ksel)),
                     (vA, jnp.where(first, vsel, vrol)), (vB, jnp.where(first, vrol, vsel))):
        ref[0:BLOCK, :] = zeros
        ref[BLOCK:, :] = val.astype(BF16)


def _attn_fwd_call(u, qg2, kg2, sinks, slopes, B, S):
    T = B * S
    nb = S // BLOCK
    half = Q_W // 2
    npair = GROUP // 2

    def body(sinks_ref, slopes_ref, q_ref, k_ref, v_ref, qg_ref, kg_ref, o_ref, kA, kB, vA, vB, btab, qn_buf, s_buf, p_buf):
        kvh = pl.program_id(1)
        lo = _lane_lo()
        _stage_kv(kvh, k_ref, v_ref, kg_ref, lo, kA, kB, vA, vB)
        _fill_bias(btab, kvh, slopes_ref)
        qgs = qg_ref[...] * ATTN_SCALE

        def block(n, slot):
            qn_scr, s_scr, p_scr = qn_buf.at[slot], s_buf.at[slot], p_buf.at[slot]
            r0 = pl.multiple_of(n * BLOCK, BLOCK)
            win = pl.ds(r0, 2 * BLOCK)
            k2t = jnp.concatenate([kA[win, :], kB[win, :]], axis=0)
            v2 = jnp.concatenate([vA[win, :], vB[win, :]], axis=0)
            var = jnp.minimum(n, 1)
            for p in range(npair):
                pc = slice(128 * p, 128 * (p + 1))
                qp = q_ref[pl.ds(r0, BLOCK), pc].astype(F32)
                qn_scr[pc, :] = (qp * _head_rstd(qp, lo) * qgs).astype(BF16)
            s_scr[...] = _dot_nt(k2t, qn_scr[...])
            for p in range(npair):
                for hf in range(2):
                    g = 2 * p + hf
                    pn, _, _ = _softmax_cols(s_scr, btab, p, var, g, hf, sinks_ref[kvh * GROUP + g])
                    p_scr[2 * BLOCK * hf:2 * BLOCK * (hf + 1), 128 * p:128 * (p + 1)] = pn.astype(BF16)
            o_all = _dot_tn(v2, p_scr[...]).T
            for p in range(npair):
                pc = slice(128 * p, 128 * (p + 1))
                o_ref[pl.ds(r0, BLOCK), pc] = o_all[pc, :].astype(BF16)

        def two_blocks(i, carry):
            block(2 * i, 0)
            block(2 * i + 1, 1)
            return carry

        lax.fori_loop(0, nb // 2, two_blocks, 0)

    smem = pl.BlockSpec(memory_space=pltpu.SMEM)
    kvbuf = pltpu.VMEM((S + BLOCK, 128), BF16)
    pair_scores = (2, npair * BLOCK, 4 * BLOCK)
    return pl.pallas_call(
        body, name="attn_fwd", grid=(B, N_KV_HEADS),
        in_specs=[smem, smem,
                  pl.BlockSpec((S, half), lambda b, h: (b, COL_Q // half + h)),
                  pl.BlockSpec((S, 128), lambda b, h: (b, COL_K // 128)),
                  pl.BlockSpec((S, 128), lambda b, h: (b, COL_V // 128)),
                  _full((1, 128)), _full((1, 128))],
        out_specs=pl.BlockSpec((S, half), lambda b, h: (b, h)),
        out_shape=jax.ShapeDtypeStruct((T, Q_W), BF16),
        scratch_shapes=[kvbuf, kvbuf, kvbuf, kvbuf, pltpu.VMEM((2, GROUP, 2 * BLOCK, BLOCK), F32),
                        pltpu.VMEM((2, npair * BLOCK, 128), BF16), pltpu.VMEM(pair_scores, F32), pltpu.VMEM(pair_scores, BF16)],
        compiler_params=_cp("parallel", "arbitrary"),
    )(sinks, slopes, u, u, u, qg2, kg2)


def _merge_call(c, o, u, x, wc, wa, wm, g_ffn, tm=512):
    T = x.shape[0]

    def body(c_ref, o_ref, gc_ref, ga_ref, x_ref, wc_ref, wa_ref, wm_ref, g_ref, yc_ref, ya_ref, z_ref, h_ref, hn_ref):
        yc = _dot(c_ref[...], wc_ref[...])
        ya = _dot(o_ref[...], wa_ref[...])
        yc_ref[...] = yc.astype(BF16)
        ya_ref[...] = ya.astype(BF16)
        z = (_sigmoid(gc_ref[...].astype(F32)) * yc + _sigmoid(ga_ref[...].astype(F32)) * ya).astype(BF16)
        z_ref[...] = z
        h = x_ref[...] + _dot(z, wm_ref[...])
        h_ref[...] = h
        r = lax.rsqrt(jnp.mean(h * h, axis=-1, keepdims=True) + EPS)
        hn_ref[...] = (h * r * g_ref[...]).astype(BF16)

    sq = _full((D_MODEL, D_MODEL))
    row = _rows(tm, D_MODEL)
    f32 = jax.ShapeDtypeStruct((T, D_MODEL), F32)
    b16 = jax.ShapeDtypeStruct((T, D_MODEL), BF16)
    return pl.pallas_call(
        body, name="merge_fwd", grid=(T // tm,),
        in_specs=[row, row, _rows(tm, D_MODEL, COL_GC // D_MODEL), _rows(tm, D_MODEL, COL_GA // D_MODEL), row,
                  sq, sq, sq, _full((1, D_MODEL))],
        out_specs=[row] * 5,
        out_shape=[b16, b16, b16, f32, b16],
        compiler_params=_cp("parallel"),
    )(c, o, u, u, x, wc, wa, wm, g_ffn)


def _ffn_in_call(hn, w, tm=512):
    T = hn.shape[0]

    def body(hn_ref, w_ref, fac_ref, act_ref):
        gu = _dot(hn_ref[...], w_ref[...])
        g = gu[:, :D_FF]
        up = gu[:, D_FF:]
        sg = _sigmoid(g)
        silu = g * sg
        act_ref[...] = (silu * up).astype(BF16)
        fac_ref[:, :D_FF] = (up * (sg * (1.0 + g * (1.0 - sg)))).astype(BF16)
        fac_ref[:, D_FF:] = silu.astype(BF16)

    return pl.pallas_call(
        body, name="ffn_in", grid=(T // tm,),
        in_specs=[_rows(tm, D_MODEL), _full(w.shape)],
        out_specs=[_rows(tm, 2 * D_FF), _rows(tm, D_FF)],
        out_shape=[jax.ShapeDtypeStruct((T, 2 * D_FF), BF16), jax.ShapeDtypeStruct((T, D_FF), BF16)],
        compiler_params=_cp("parallel"),
    )(hn, w)


def _ffn_out_call(act, wd, h, tgt, tm=512):
    T = h.shape[0]

    def body(act_ref, wd_ref, h_ref, t_ref, sse_ref, d_ref, db_ref):
        e = h_ref[...] + _dot(act_ref[...], wd_ref[...]) - t_ref[...]

        @pl.when(pl.program_id(0) == 0)
        def _():
            sse_ref[...] = jnp.zeros_like(sse_ref)

        sse_ref[...] += jnp.sum(e * e)
        d = e * (1.0 / D_MODEL)
        d_ref[...] = d
        db_ref[...] = d.astype(BF16)

    row = _rows(tm, D_MODEL)
    return pl.pallas_call(
        body, name="ffn_out", grid=(T // tm,),
        in_specs=[_rows(tm, D_FF), _full(wd.shape), row, row],
        out_specs=[_full((8, 128)), row, row],
        out_shape=[jax.ShapeDtypeStruct((8, 128), F32), jax.ShapeDtypeStruct((T, D_MODEL), F32),
                   jax.ShapeDtypeStruct((T, D_MODEL), BF16)],
        compiler_params=_cp("arbitrary"),
    )(act, wd, h, tgt)


def _dw_call(name, a, b, tn, tk=2048):
    T, K = a.shape
    N = b.shape[1]
    tk = min(tk, T)
    nk = T // tk

    def body(a_ref, b_ref, o_ref, acc_ref):
        k = pl.program_id(1)

        @pl.when(k == 0)
        def _():
            acc_ref[...] = jnp.zeros_like(acc_ref)

        acc_ref[...] += _dot_tn(a_ref[...], b_ref[...])

        @pl.when(k == nk - 1)
        def _():
            o_ref[...] = acc_ref[...].astype(BF16)

    return pl.pallas_call(
        body, name=name, grid=(N // tn, nk),
        in_specs=[pl.BlockSpec((tk, K), lambda j, k: (k, 0)), pl.BlockSpec((tk, tn), lambda j, k: (k, j))],
        out_specs=pl.BlockSpec((K, tn), lambda j, k: (0, j)),
        out_shape=jax.ShapeDtypeStruct((K, N), BF16),
        scratch_shapes=[pltpu.VMEM((K, tn), F32)],
        compiler_params=_cp("parallel", "arbitrary"),
    )(a, b)


def _rms_bwd(dy, xv, g):
    r = lax.rsqrt(jnp.mean(xv * xv, axis=-1, keepdims=True) + EPS)
    t = dy * g
    dx = r * t - xv * (r * r * r * jnp.mean(t * xv, axis=-1, keepdims=True))
    return dx, dy * xv * r


def _ffn_down_bwd_call(dout_b, wd, gu, tm=512):
    T = dout_b.shape[0]

    def body(d_ref, wd_ref, fg_ref, fu_ref, o_ref):
        dact = _dot_nt(d_ref[...], wd_ref[...])
        o_ref[:, :D_FF] = (dact * fg_ref[...].astype(F32)).astype(BF16)
        o_ref[:, D_FF:] = (dact * fu_ref[...].astype(F32)).astype(BF16)

    return pl.pallas_call(
        body, name="ffn_down_bwd", grid=(T // tm,),
        in_specs=[_rows(tm, D_MODEL), _full(wd.shape), _rows(tm, D_FF, 0), _rows(tm, D_FF, 1)],
        out_specs=_rows(tm, 2 * D_FF),
        out_shape=jax.ShapeDtypeStruct((T, 2 * D_FF), BF16),
        compiler_params=_cp("parallel"),
    )(dout_b, wd, gu, gu)


def _ffn_in_bwd_call(dgu, w, h, g_ffn, dout, tm=512):
    T = h.shape[0]

    def body(dgu_ref, w_ref, h_ref, g_ref, d_ref, dh_ref, dhb_ref, dg_ref):
        dhn = _dot_nt(dgu_ref[...], w_ref[...])
        dx, dgt = _rms_bwd(dhn, h_ref[...], g_ref[...])
        dh = d_ref[...] + dx
        dh_ref[...] = dh
        dhb_ref[...] = dh.astype(BF16)

        @pl.when(pl.program_id(0) == 0)
        def _():
            dg_ref[...] = jnp.zeros_like(dg_ref)

        dg_ref[...] += jnp.sum(dgt, axis=0, keepdims=True)

    row = _rows(tm, D_MODEL)
    return pl.pallas_call(
        body, name="ffn_in_bwd", grid=(T // tm,),
        in_specs=[_rows(tm, 2 * D_FF), _full(w.shape), row, _full((1, D_MODEL)), row],
        out_specs=[row, row, _full((1, D_MODEL))],
        out_shape=[jax.ShapeDtypeStruct((T, D_MODEL), F32), jax.ShapeDtypeStruct((T, D_MODEL), BF16),
                   jax.ShapeDtypeStruct((1, D_MODEL), F32)],
        compiler_params=_cp("arbitrary"),
    )(dgu, w, h, g_ffn, dout)


def _merge_bwd_call(dh_b, wm, wc, wa, u, yc, ya, tm=512):
    T = dh_b.shape[0]

    def body(dh_ref, wm_ref, wc_ref, wa_ref, gc_ref, ga_ref, yc_ref, ya_ref, dyc_ref, dya_ref, dug_ref, dcp_ref, do_ref):
        dz = _dot_nt(dh_ref[...], wm_ref[...])
        sgc = _sigmoid(gc_ref[...].astype(F32))
        sga = _sigmoid(ga_ref[...].astype(F32))
        dyc = (dz * sgc).astype(BF16)
        dya = (dz * sga).astype(BF16)
        dyc_ref[...] = dyc
        dya_ref[...] = dya
        dug_ref[:, :D_MODEL] = (dz * yc_ref[...].astype(F32) * (sgc * (1.0 - sgc))).astype(BF16)
        dug_ref[:, D_MODEL:] = (dz * ya_ref[...].astype(F32) * (sga * (1.0 - sga))).astype(BF16)
        dcp_ref[...] = _dot_nt(dyc, wc_ref[...])
        do_ref[...] = _dot_nt(dya, wa_ref[...]).astype(BF16)

    sq = _full((D_MODEL, D_MODEL))
    row = _rows(tm, D_MODEL)
    b16 = jax.ShapeDtypeStruct((T, D_MODEL), BF16)
    return pl.pallas_call(
        body, name="merge_bwd", grid=(T // tm,),
        in_specs=[row, sq, sq, sq, _rows(tm, D_MODEL, COL_GC // D_MODEL), _rows(tm, D_MODEL, COL_GA // D_MODEL), row, row],
        out_specs=[row, row, _rows(tm, 2 * D_MODEL), row, row],
        out_shape=[b16, b16, jax.ShapeDtypeStruct((T, 2 * D_MODEL), BF16), jax.ShapeDtypeStruct((T, D_MODEL), F32), b16],
        compiler_params=_cp("parallel"),
    )(dh_b, wm, wc, wa, u, u, yc, ya)


def _conv_bwd_call(dcp, hc, u, conv_w, ln_g, ln_b, B, S, ts=256):
    T = B * S
    nS = S // ts
    per32 = ts // HALO
    last32 = T // HALO - 1

    def ln_bwd(dcp_v, hc_v, g, bb):
        mu = jnp.mean(hc_v, axis=-1, keepdims=True)
        xc = hc_v - mu
        rstd = lax.rsqrt(jnp.mean(xc * xc, axis=-1, keepdims=True) + EPS)
        xhat = xc * rstd
        y = xhat * g + bb
        sy = _sigmoid(y)
        dy = dcp_v * (sy * (1.0 + y * (1.0 - sy)))
        dxh = dy * g
        dhc = rstd * (dxh - jnp.mean(dxh, axis=-1, keepdims=True) - xhat * jnp.mean(dxh * xhat, axis=-1, keepdims=True))
        return dhc, dy * xhat, dy

    def body(dcp_ref, dcpn_ref, hc_ref, hcn_ref, a_ref, gate_ref, w_ref, g_ref, bb_ref,
             du_ref, dw_ref, db_ref, dlg_ref, dlb_ref, dhext, glu_scr, dhsh, acc_ref):
        b = pl.program_id(0)
        s = pl.program_id(1)

        @pl.when((b == 0) & (s == 0))
        def _():
            dw_ref[...] = jnp.zeros_like(dw_ref)
            db_ref[...] = jnp.zeros_like(db_ref)
            dlg_ref[...] = jnp.zeros_like(dlg_ref)
            dlb_ref[...] = jnp.zeros_like(dlb_ref)

        dhc, dgt, dbt = ln_bwd(dcp_ref[...], hc_ref[...], g_ref[...], bb_ref[...])
        dhn, _, _ = ln_bwd(dcpn_ref[...], hcn_ref[...], g_ref[...], bb_ref[...])
        dhext[0:ts, :] = dhc
        dhext[ts:, :] = jnp.where(s < nS - 1, dhn, 0.0)
        db_ref[...] += jnp.sum(dhc, axis=0, keepdims=True)
        dlg_ref[...] += jnp.sum(dgt, axis=0, keepdims=True)
        dlb_ref[...] += jnp.sum(dbt, axis=0, keepdims=True)

        sg = _sigmoid(gate_ref[...].astype(F32))
        av = a_ref[...].astype(F32)
        glu_scr[...] = av * sg
        _fill_shifted(dhext, dhsh)

        def colblock(cb, carry):
            cols = pl.ds(pl.multiple_of(cb * 128, 128), 128)
            outs = _conv_taps(w_ref, dhext, dhsh, 0, ts, cb, reverse=True)
            for rb, acc in enumerate(outs):
                acc_ref[rb * 128:(rb + 1) * 128, cols] = acc
            taps = [jnp.zeros((8, 128), F32) for _ in range(CONV_WIDTH)]
            for r0 in range(0, ts, 128):
                glu = [glu_scr[r0 + 8 * i:r0 + 8 * (i + 1), cols] for i in range(16)]
                for s8 in range(8):
                    users = {}
                    for j in range(CONV_WIDTH):
                        if (CONV_WIDTH - 1 - j) % 8 == s8:
                            for i in range(16):
                                users.setdefault(CONV_WIDTH - 1 - j + r0 + 8 * i - s8, []).append((j, i))
                    for row, pairs in sorted(users.items()):
                        dhc8 = dhext[row:row + 8, cols] if s8 == 0 else dhsh[s8 - 1, row:row + 8, cols]
                        for j, i in pairs:
                            taps[j] = taps[j] + glu[i] * dhc8
            for j in range(CONV_WIDTH):
                dw_ref[8 * j:8 * (j + 1), cols] += taps[j]
            return carry

        lax.fori_loop(0, C_CONV // 128, colblock, 0)
        dglu = acc_ref[...]
        du_ref[:, :C_CONV] = (dglu * sg).astype(BF16)
        du_ref[:, C_CONV:] = (dglu * av * (sg * (1.0 - sg))).astype(BF16)

    cur = lambda cb: pl.BlockSpec((ts, C_CONV), lambda b, s: (b * nS + s, cb))
    nxt = pl.BlockSpec((HALO, C_CONV), lambda b, s: (jnp.minimum(b * (S // HALO) + (s + 1) * per32, last32), 0))
    vec = _full((1, C_CONV))
    return pl.pallas_call(
        body, name="conv_bwd", grid=(B, nS),
        in_specs=[cur(0), nxt, cur(0), nxt, cur(0), cur(1), _full((CONV_WIDTH, C_CONV)), vec, vec],
        out_specs=[pl.BlockSpec((ts, 2 * C_CONV), lambda b, s: (b * nS + s, 0)), _full((8 * CONV_WIDTH, C_CONV)), vec, vec, vec],
        out_shape=[jax.ShapeDtypeStruct((T, 2 * C_CONV), BF16), jax.ShapeDtypeStruct((8 * CONV_WIDTH, C_CONV), F32),
                   jax.ShapeDtypeStruct((1, C_CONV), F32), jax.ShapeDtypeStruct((1, C_CONV), F32),
                   jax.ShapeDtypeStruct((1, C_CONV), F32)],
        scratch_shapes=[pltpu.VMEM((ts + HALO, C_CONV), F32), pltpu.VMEM((ts, C_CONV), F32),
                        pltpu.VMEM((7, ts + HALO, C_CONV), F32), pltpu.VMEM((ts, C_CONV), F32)],
        compiler_params=_cp("arbitrary", "arbitrary"),
    )(dcp, dcp, hc, hc, u, u, conv_w, ln_g, ln_b)


def _attn_bwd_call(u, do, qg2, kg2, sinks, slopes, B, S):
    T = B * S
    nb = S // BLOCK
    half = Q_W // 2
    npair = GROUP // 2

    def body(sinks_ref, slopes_ref, q_ref, k_ref, v_ref, do_ref, qg_ref, kg_ref,
             dq_ref, dkv_ref, dqg_ref, dkg_ref, dsk_ref, kA, kB, vA, vB, dkn_acc, dv_acc, dkv_keep,
             btab, s_buf, dp_buf, p_buf, ds_buf, qn_buf, do_buf, rq_buf):
        b = pl.program_id(0)
        kvh = pl.program_id(1)
        lo = _lane_lo()
        lane = lax.broadcasted_iota(jnp.int32, (1, 128), 1)
        first = kvh == 0

        @pl.when((b == 0) & first)
        def _():
            dqg_ref[...] = jnp.zeros_like(dqg_ref)
            dkg_ref[...] = jnp.zeros_like(dkg_ref)
            dsk_ref[...] = jnp.zeros_like(dsk_ref)

        _stage_kv(kvh, k_ref, v_ref, kg_ref, lo, kA, kB, vA, vB)
        dkn_acc[...] = jnp.zeros_like(dkn_acc)
        dv_acc[...] = jnp.zeros_like(dv_acc)
        _fill_bias(btab, kvh, slopes_ref)
        qgs = qg_ref[...] * ATTN_SCALE

        def fold(acc):
            canon = jnp.where(lo, acc[0:2 * BLOCK] + pltpu.roll(acc[2 * BLOCK:], HEAD_DIM, 1), 0.0)
            return jnp.where(first, canon, pltpu.roll(canon, HEAD_DIM, 1))

        def block(n, slot):
            s_scr, dp_scr, p_scr, ds_scr = s_buf.at[slot], dp_buf.at[slot], p_buf.at[slot], ds_buf.at[slot]
            qn_scr, do_scr, rq_scr = qn_buf.at[slot], do_buf.at[slot], rq_buf.at[slot]
            r0 = pl.multiple_of(n * BLOCK, BLOCK)
            win = pl.ds(r0, 2 * BLOCK)
            rows = pl.ds(r0, BLOCK)
            k2t = jnp.concatenate([kA[win, :], kB[win, :]], axis=0)
            v2 = jnp.concatenate([vA[win, :], vB[win, :]], axis=0)
            var = jnp.minimum(n, 1)
            for p in range(npair):
                pc = slice(128 * p, 128 * (p + 1))
                qp = q_ref[rows, pc].astype(F32)
                rq = _head_rstd(qp, lo)
                rq_scr[pc, :] = rq
                qn_scr[pc, :] = (qp * rq * qgs).astype(BF16)
                do_scr[pc, :] = do_ref[rows, pc]
            s_scr[...] = _dot_nt(k2t, qn_scr[...])
            dp_scr[...] = _dot_nt(v2, do_scr[...])
            for p in range(npair):
                for hf in range(2):
                    g = 2 * p + hf
                    keys, qs = slice(2 * BLOCK * hf, 2 * BLOCK * (hf + 1)), slice(128 * p, 128 * (p + 1))
                    pn, esk, rz = _softmax_cols(s_scr, btab, p, var, g, hf, sinks_ref[kvh * GROUP + g])
                    dph = dp_scr[keys, qs]
                    delta = jnp.sum(pn * dph, axis=0, keepdims=True)
                    ds_scr[keys, qs] = (pn * (dph - delta)).astype(BF16)
                    p_scr[keys, qs] = pn.astype(BF16)
                    dsk_ref[...] += jnp.where(lane == kvh * GROUP + g, -jnp.sum(esk * rz * delta), 0.0)
            dqs_all = _dot_tn(k2t, ds_scr[...]).T
            for p in range(npair):
                pc = slice(128 * p, 128 * (p + 1))
                qp = q_ref[rows, pc].astype(F32)
                rq = rq_scr[pc, :]
                dqs = dqs_all[pc, :]
                t = dqs * qgs
                dq = rq * t - qp * (rq * rq * rq * (_half_sums(t * qp, lo) * (1.0 / HEAD_DIM)))
                dq_ref[rows, pc] = dq.astype(BF16)
                dqg_ref[...] += jnp.sum(dqs * qp * rq, axis=0, keepdims=True) * ATTN_SCALE
            dkn_acc[win, :] += fold(_dot(ds_scr[...], qn_scr[...]))
            dv_acc[win, :] += fold(_dot(p_scr[...], do_scr[...]))

        def two_blocks(i, carry):
            block(2 * i, 0)
            block(2 * i + 1, 1)
            return carry

        lax.fori_loop(0, nb // 2, two_blocks, 0)

        dkn = dkn_acc[BLOCK:, :]
        kv = k_ref[...].astype(F32)
        kg = kg_ref[...]
        rk = _head_rstd(kv, lo)
        t = dkn * kg
        dk = rk * t - kv * (rk * rk * rk * (_half_sums(t * kv, lo) * (1.0 / HEAD_DIM)))
        dkg_ref[...] += jnp.sum(dkn * kv * rk, axis=0, keepdims=True)
        dv = dv_acc[BLOCK:, :]

        @pl.when(first)
        def _():
            dkv_keep[:, :128] = dk
            dkv_keep[:, 128:] = dv

        @pl.when(kvh == N_KV_HEADS - 1)
        def _():
            dkv_ref[:, :128] = (dkv_keep[:, :128] + dk).astype(BF16)
            dkv_ref[:, 128:] = (dkv_keep[:, 128:] + dv).astype(BF16)

    smem = pl.BlockSpec(memory_space=pltpu.SMEM)
    kvbuf = pltpu.VMEM((S + BLOCK, 128), BF16)
    accbuf = pltpu.VMEM((S + BLOCK, 128), F32)
    vec = _full((1, 128))
    return pl.pallas_call(
        body, name="attn_bwd", grid=(B, N_KV_HEADS),
        in_specs=[smem, smem,
                  pl.BlockSpec((S, half), lambda b, h: (b, COL_Q // half + h)),
                  pl.BlockSpec((S, 128), lambda b, h: (b, COL_K // 128)),
                  pl.BlockSpec((S, 128), lambda b, h: (b, COL_V // 128)),
                  pl.BlockSpec((S, half), lambda b, h: (b, h)),
                  vec, vec],
        out_specs=[pl.BlockSpec((S, half), lambda b, h: (b, h)), pl.BlockSpec((S, 256), lambda b, h: (b, 0)), vec, vec, vec],
        out_shape=[jax.ShapeDtypeStruct((T, Q_W), BF16), jax.ShapeDtypeStruct((T, 2 * KV_W), BF16),
                   jax.ShapeDtypeStruct((1, 128), F32), jax.ShapeDtypeStruct((1, 128), F32), jax.ShapeDtypeStruct((1, 128), F32)],
        scratch_shapes=[kvbuf, kvbuf, kvbuf, kvbuf, accbuf, accbuf, pltpu.VMEM((S, 256), F32),
                        pltpu.VMEM((2, GROUP, 2 * BLOCK, BLOCK), F32),
                        pltpu.VMEM((2, npair * BLOCK, 4 * BLOCK), F32), pltpu.VMEM((2, npair * BLOCK, 4 * BLOCK), F32),
                        pltpu.VMEM((2, npair * BLOCK, 4 * BLOCK), BF16), pltpu.VMEM((2, npair * BLOCK, 4 * BLOCK), BF16),
                        pltpu.VMEM((2, npair * BLOCK, 128), BF16), pltpu.VMEM((2, npair * BLOCK, 128), BF16),
                        pltpu.VMEM((2, npair * BLOCK, 128), F32)],
        compiler_params=_cp("arbitrary", "arbitrary"),
    )(sinks, slopes, u, u, u, do, qg2, kg2)


def _inproj_bwd_call(du_conv, du_g, du_q, du_kv, w, x, g, dh, tm=512):
    T = x.shape[0]

    def body(dc_ref, dg_ref, dq_ref, dkv_ref, w_ref, x_ref, g_ref, dh_ref, gx_ref, dgm_ref):
        dxn = _dot_nt(dc_ref[...], w_ref[:, COL_A:COL_GC])
        dxn += _dot_nt(dg_ref[...], w_ref[:, COL_GC:COL_Q])
        dxn += _dot_nt(dq_ref[...], w_ref[:, COL_Q:COL_K])
        dxn += _dot_nt(dkv_ref[...], w_ref[:, COL_K:])
        dx, dgt = _rms_bwd(dxn, x_ref[...], g_ref[...])
        gx_ref[...] = dh_ref[...] + dx

        @pl.when(pl.program_id(0) == 0)
        def _():
            dgm_ref[...] = jnp.zeros_like(dgm_ref)

        dgm_ref[...] += jnp.sum(dgt, axis=0, keepdims=True)

    row = _rows(tm, D_MODEL)
    return pl.pallas_call(
        body, name="inproj_bwd", grid=(T // tm,),
        in_specs=[_rows(tm, 2 * C_CONV), _rows(tm, 2 * D_MODEL), _rows(tm, Q_W), _rows(tm, 2 * KV_W), _full(w.shape),
                  row, _full((1, D_MODEL)), row],
        out_specs=[row, _full((1, D_MODEL))],
        out_shape=[jax.ShapeDtypeStruct((T, D_MODEL), F32), jax.ShapeDtypeStruct((1, D_MODEL), F32)],
        compiler_params=_cp("arbitrary"),
    )(du_conv, du_g, du_q, du_kv, w, x, g, dh)


def _adamw_math(w, g, m, v):
    m = ADAM_B1 * m + (1.0 - ADAM_B1) * g
    v = ADAM_B2 * v + (1.0 - ADAM_B2) * (g * g)
    m_hat = m / (1.0 - ADAM_B1 ** ADAM_STEP)
    v_hat = v / (1.0 - ADAM_B2 ** ADAM_STEP)
    delta = -ADAM_LR * (m_hat / (jnp.sqrt(v_hat) + ADAM_EPS) + ADAM_WD * w)
    return delta, m, v


def _adamw_sum_call(name, parts, own, w, m, v, tr):
    R, C = w.shape

    def body(p_ref, own_ref, w_ref, m_ref, v_ref, g_ref, d_ref, nm_ref, nv_ref):
        x, y, c = _position()
        me = 4 * x + 2 * y + c
        g = jnp.zeros((tr, C), F32)
        for s in range(N_DEV):
            g = g + jnp.where(me == s, own_ref[...], p_ref[s]).astype(F32)
        g_ref[...] = g
        d_ref[...], nm_ref[...], nv_ref[...] = _adamw_math(w_ref[...], g, m_ref[...], v_ref[...])

    blk = pl.BlockSpec((tr, C), lambda i: (i, 0))
    out = jax.ShapeDtypeStruct((R, C), F32)
    return pl.pallas_call(
        body, name=name, grid=(R // tr,),
        in_specs=[pl.BlockSpec((N_DEV, tr, C), lambda i: (0, i, 0)), blk, blk, blk, blk],
        out_specs=[blk] * 4, out_shape=[out] * 4,
        compiler_params=_cp("parallel"),
    )(parts, own, w, m, v)


def _adamw_small_call(name, g, w, m, v):
    def body(g_ref, w_ref, m_ref, v_ref, d_ref, nm_ref, nv_ref):
        d_ref[...], nm_ref[...], nv_ref[...] = _adamw_math(w_ref[...], g_ref[...], m_ref[...], v_ref[...])

    out = jax.ShapeDtypeStruct(w.shape, F32)
    return pl.pallas_call(body, name=name, out_shape=[out] * 3)(g, w, m, v)


def _position():
    return lax.axis_index("x"), lax.axis_index("y"), lax.axis_index("c")


def _peer(x, y, c, rel):
    fx, fy, fc = (rel >> 2) & 1, (rel >> 1) & 1, rel & 1
    px = 1 - x if fx else x
    py = 1 - y if fy else y
    pc = 1 - c if fc else c
    return (px, py, pc), 4 * px + 2 * py + pc


def _gather_weights_call(shards):
    n = len(shards)

    def body(*refs):
        ins, outs = refs[:n], refs[n:2 * n]
        send_sems, recv_sems, local_sems = refs[2 * n:]
        x, y, c = _position()
        me = 4 * x + 2 * y + c
        sib, sib_id = _peer(x, y, c, 1)
        chips = [_peer(x, y, c, rel) for rel in (4, 2, 6)]

        def copy(i, k, src, slot, to):
            return pltpu.make_async_remote_copy(
                src_ref=src, dst_ref=outs[i].at[slot], send_sem=send_sems.at[i, k], recv_sem=recv_sems.at[i, k],
                device_id=to, device_id_type=MESH)

        mine = [pltpu.make_async_copy(ins[i], outs[i].at[me], local_sems.at[i]) for i in range(n)]
        for cp in mine:
            cp.start()
        first = []
        for i in range(n):
            first.append(copy(i, 0, ins[i], me, sib))
            first += [copy(i, 1 + j, ins[i], me, dev) for j, (dev, _) in enumerate(chips)]
        for cp in first:
            cp.start()
        passed = []
        for i in range(n):
            for j, (dev, slot) in enumerate(chips):
                copy(i, 1 + j, ins[i], slot, dev).wait_recv()
                fwd = copy(i, 4 + j, outs[i].at[slot], slot, sib)
                fwd.start()
                passed.append(fwd)
        for i in range(n):
            copy(i, 0, ins[i], sib_id, sib).wait_recv()
            for j, (_, slot) in enumerate(chips):
                copy(i, 4 + j, ins[i], slot ^ 1, sib).wait_recv()
        for cp in first + passed:
            cp.wait_send()
        for cp in mine:
            cp.wait()

    hbm = pl.BlockSpec(memory_space=pl.ANY)
    return pl.pallas_call(
        body, name="gather_weights",
        in_specs=[hbm] * n, out_specs=[hbm] * n,
        out_shape=[jax.ShapeDtypeStruct((N_DEV,) + s.shape, s.dtype) for s in shards],
        scratch_shapes=[pltpu.SemaphoreType.DMA((n, 7)), pltpu.SemaphoreType.DMA((n, 7)), pltpu.SemaphoreType.DMA((n,))],
    )(*shards)


_HBM = pl.BlockSpec(memory_space=pltpu.HBM)
_SEM = pl.BlockSpec(memory_space=pltpu.SEMAPHORE)
_EFFECT = pltpu.SideEffectType.DATAFLOW_SIDE_EFFECTING


def _exchange_copies(scatter, incoming, src_refs, land_refs, send_sems, recv_sems):
    x, y, c = _position()
    me = 4 * x + 2 * y + c
    out = []
    for i, (src, land) in enumerate(zip(src_refs, land_refs)):
        for k in range(N_DEV - 1):
            dev, dev_id = _peer(x, y, c, k + 1)
            j = i * (N_DEV - 1) + k
            out.append(pltpu.make_async_remote_copy(
                src_ref=src.at[dev_id] if scatter else src, dst_ref=land.at[dev_id if incoming else me],
                send_sem=send_sems.at[j], recv_sem=recv_sems.at[j], device_id=dev, device_id_type=MESH))
    return out


def _exchange_start_call(name, arrays, scatter):
    n = len(arrays)
    lands = [lax.empty((N_DEV,) + (a.shape[1:] if scatter else a.shape), a.dtype) for a in arrays]

    def body(*refs):
        src_refs, land_refs = refs[:n], refs[n:2 * n]
        send_sems, recv_sems, token = refs[2 * n], refs[2 * n + 1], refs[-1]
        for copy in _exchange_copies(scatter, False, src_refs, land_refs, send_sems, recv_sems):
            copy.start()
        token[...] = jnp.zeros_like(token)

    sems = pltpu.SemaphoreType.DMA((n * (N_DEV - 1),))
    res = pl.pallas_call(
        body, name=name,
        out_shape=[sems, sems] + [pltpu.HBM(a.shape, a.dtype) for a in arrays] + [pltpu.HBM(l.shape, l.dtype) for l in lands]
        + [jax.ShapeDtypeStruct((8, 128), F32)],
        in_specs=[_HBM] * (2 * n), out_specs=[_SEM, _SEM] + [_HBM] * (2 * n) + [pl.BlockSpec(memory_space=pltpu.VMEM)],
        input_output_aliases={i: 2 + i for i in range(2 * n)},
        compiler_params=pltpu.CompilerParams(has_side_effects=_EFFECT),
    )(*[pltpu.with_memory_space_constraint(a, pltpu.HBM) for a in list(arrays) + lands])
    return res[0], res[1], res[2:2 + n], res[2 + n:2 + 2 * n], res[-1]


def _exchange_wait_call(name, started, scatter, after):
    send_sems, recv_sems, srcs, lands, _ = started
    n = len(srcs)

    def body(*refs):
        src_refs, land_refs = refs[:n], refs[n:2 * n]
        send_s, recv_s = refs[2 * n], refs[2 * n + 1]
        for copy in _exchange_copies(scatter, False, src_refs, land_refs, send_s, recv_s):
            copy.wait_send()
        for copy in _exchange_copies(scatter, True, src_refs, land_refs, send_s, recv_s):
            copy.wait_recv()

    res = pl.pallas_call(
        body, name=name,
        out_shape=[pltpu.HBM(a.shape, a.dtype) for a in list(srcs) + list(lands)],
        in_specs=[_HBM] * (2 * n) + [_SEM, _SEM, pl.BlockSpec(memory_space=pl.ANY)], out_specs=[_HBM] * (2 * n),
        input_output_aliases={i: i for i in range(2 * n)},
        compiler_params=pltpu.CompilerParams(has_side_effects=_EFFECT),
    )(*srcs, *lands, send_sems, recv_sems, after)
    return res[n:]


def _allsum_small_call(part, blocks):
    P, Q = part.shape[0], blocks.shape[1]

    def body(p_ref, b_ref, o_ref, ob_ref, all_ref, allb_ref, send_sems, recv_sems):
        x, y, c = _position()
        me = 4 * x + 2 * y + c
        peers = [_peer(x, y, c, rel) for rel in range(1, N_DEV)]
        all_ref[me] = p_ref[...]
        allb_ref[me] = b_ref[me]

        def copies(k, slot, dev, dev_id):
            sems = lambda i: dict(send_sem=send_sems.at[i], recv_sem=recv_sems.at[i], device_id=dev, device_id_type=MESH)
            return (pltpu.make_async_remote_copy(src_ref=p_ref, dst_ref=all_ref.at[slot], **sems(k)),
                    pltpu.make_async_remote_copy(src_ref=b_ref.at[dev_id], dst_ref=allb_ref.at[slot], **sems(7 + k)))

        sends = [cp for k, (dev, dev_id) in enumerate(peers) for cp in copies(k, me, dev, dev_id)]
        for cp in sends:
            cp.start()
        for k, (dev, dev_id) in enumerate(peers):
            for cp in copies(k, dev_id, dev, dev_id):
                cp.wait_recv()
        for cp in sends:
            cp.wait_send()
        tot, totb = all_ref[0], allb_ref[0]
        for s in range(1, N_DEV):
            tot, totb = tot + all_ref[s], totb + allb_ref[s]
        o_ref[...] = tot
        ob_ref[...] = totb

    vm = pl.BlockSpec(memory_space=pltpu.VMEM)
    return pl.pallas_call(
        body, name="allsum_small", in_specs=[vm, vm], out_specs=[vm, vm],
        out_shape=[jax.ShapeDtypeStruct((P, 128), F32), jax.ShapeDtypeStruct((Q, 128), F32)],
        scratch_shapes=[pltpu.VMEM((N_DEV, P, 128), F32), pltpu.VMEM((N_DEV, Q, 128), F32),
                        pltpu.SemaphoreType.DMA((14,)), pltpu.SemaphoreType.DMA((14,))],
    )(part, blocks)


def _pad_rows(v, rows):
    flat = v.reshape(-1)
    return jnp.pad(flat, (0, rows * 128 - flat.shape[0])).reshape(rows, 128)


def _blocks_by_cols(w):
    K, N = w.shape
    return w.reshape(K, N_DEV, N // N_DEV).transpose(1, 0, 2)


def _cols_from_blocks(wb):
    n, K, C = wb.shape
    return wb.transpose(1, 0, 2).reshape(K, n * C)


def _local_step(x, tgt, w_in_i, conv_w, conv_b, ln_g, ln_b, qg, kg, sinks, g_mix, g_ffn, token0, rest_weights, grads_ready, B, S):
    row = lambda v: v.reshape(1, -1)
    qg2 = jnp.tile(qg, 2).reshape(1, 128)
    kg2 = jnp.tile(kg, 2).reshape(1, 128)
    heads = jnp.arange(1, N_Q_HEADS + 1, dtype=F32)
    slopes = jnp.exp2(-8.0 * heads / N_Q_HEADS)

    u, xn = _inproj_call(x, row(g_mix) + token0, w_in_i)
    hc, c = _conv_fwd_call(u, conv_w, row(conv_b), row(ln_g), row(ln_b), B, S)
    o = _attn_fwd_call(u, qg2, kg2, sinks, slopes, B, S)
    wc, wa, wm, w_ffn_in, wd = rest_weights(o)
    yc, ya, z, h, hn = _merge_call(c, o, u, x, wc, wa, wm, row(g_ffn))
    gu, act = _ffn_in_call(hn, w_ffn_in)
    sse, dout, dout_b = _ffn_out_call(act, wd, h, tgt)

    dgu = _ffn_down_bwd_call(dout_b, wd, gu)
    g_wd = _dw_call("dw_ffn_down", act, dout_b, tn=512)
    g_wffn = _dw_call("dw_ffn_in", hn, dgu, tn=2 * D_FF // 4)
    tok = grads_ready(dict(w_ffn_down=g_wd, w_ffn_in=g_wffn))
    dh, dh_b, g_gffn = _ffn_in_bwd_call(dgu, w_ffn_in, h, row(g_ffn) + tok, dout)
    dyc, dya, du_g, dcp, do = _merge_bwd_call(dh_b, wm, wc, wa, u, yc, ya)
    g_wm = _dw_call("dw_merge", z, dh_b, tn=D_MODEL)
    g_wc = _dw_call("dw_conv_out", c, dyc, tn=D_MODEL)
    g_wa = _dw_call("dw_attn_out", o, dya, tn=D_MODEL)
    tok = grads_ready(dict(w_merge_out=g_wm, w_conv_out=g_wc, w_attn_out=g_wa))
    du_conv, g_cw8, g_cb, g_lg, g_lb = _conv_bwd_call(dcp, hc, u, conv_w, row(ln_g) + tok, row(ln_b), B, S)
    du_q, du_kv, g_qg2, g_kg2, g_sk = _attn_bwd_call(u, do, qg2, kg2, sinks, slopes, B, S)
    g_win = jnp.concatenate([
        _dw_call("dw_in_conv", xn, du_conv, tn=D_MODEL),
        _dw_call("dw_in_q", xn, du_q, tn=D_MODEL),
        _dw_call("dw_in_kv", xn, du_kv, tn=2 * KV_W),
        _dw_call("dw_in_gates", xn, du_g, tn=D_MODEL)], axis=1)
    tok = grads_ready(dict(w_in=g_win))
    grad_x, g_gmix = _inproj_bwd_call(du_conv, du_g, du_q, du_kv, w_in_i, x, row(g_mix) + tok, dh)

    g_cw = g_cw8.reshape(CONV_WIDTH, 8, C_CONV).sum(axis=1)
    g_qg = g_qg2[0, :HEAD_DIM] + g_qg2[0, HEAD_DIM:]
    g_kg = g_kg2[0, :HEAD_DIM] + g_kg2[0, HEAD_DIM:]
    small = dict(norm_mix_g=g_gmix[0], conv_dw_b=g_cb[0], conv_ln_g=g_lg[0], conv_ln_b=g_lb[0], q_norm_g=g_qg,
                 k_norm_g=g_kg, sinks=g_sk[0, :N_Q_HEADS], norm_ffn_g=g_gffn[0], conv_dw_w=g_cw)
    return sse[0, 0], grad_x, small


def _to_internal_cols(w):
    o1 = 2 * C_CONV
    o4 = o1 + Q_W + 2 * KV_W
    return jnp.concatenate([w[:, :o1], w[:, o4:], w[:, o1:o4]], axis=1)


_COL_SHARDED = ("w_in", "w_ffn_in")

_SMALL = (("norm_mix_g", 8), ("conv_dw_b", 8), ("conv_ln_g", 8), ("conv_ln_b", 8), ("norm_ffn_g", 8),
          ("q_norm_g", 1), ("k_norm_g", 1), ("sinks", 1))
_SMALL_ROWS = 48


def _pack_small(d):
    rows = [_pad_rows(d[name], r) for name, r in _SMALL]
    used = sum(r for _, r in _SMALL)
    return jnp.concatenate(rows + [jnp.zeros((_SMALL_ROWS - used, 128), F32)], axis=0)


def _unpack_small(packed, like):
    out, r0 = {}, 0
    for name, r in _SMALL:
        n = like[name].size
        out[name] = packed[r0:r0 + r].reshape(-1)[:n].reshape(like[name].shape)
        r0 += r
    return out


def kernel(x, norm_mix_g, w_in, conv_dw_w, conv_dw_b, conv_ln_g, conv_ln_b, w_conv_out, q_norm_g, k_norm_g, sinks, w_attn_out, w_merge_out, norm_ffn_g, w_ffn_in, w_ffn_down, loss_target, m_norm_mix_g, m_w_in, m_conv_dw_w, m_conv_dw_b, m_conv_ln_g, m_conv_ln_b, m_w_conv_out, m_q_norm_g, m_k_norm_g, m_sinks, m_w_attn_out, m_w_merge_out, m_norm_ffn_g, m_w_ffn_in, m_w_ffn_down, v_norm_mix_g, v_w_in, v_conv_dw_w, v_conv_dw_b, v_conv_ln_g, v_conv_ln_b, v_w_conv_out, v_q_norm_g, v_k_norm_g, v_sinks, v_w_attn_out, v_w_merge_out, v_norm_ffn_g, v_w_ffn_in, v_w_ffn_down):
    names = ["norm_mix_g", "w_in", "conv_dw_w", "conv_dw_b", "conv_ln_g", "conv_ln_b", "w_conv_out", "q_norm_g",
             "k_norm_g", "sinks", "w_attn_out", "w_merge_out", "norm_ffn_g", "w_ffn_in", "w_ffn_down"]
    w = dict(zip(names, (norm_mix_g, w_in, conv_dw_w, conv_dw_b, conv_ln_g, conv_ln_b, w_conv_out, q_norm_g, k_norm_g,
                         sinks, w_attn_out, w_merge_out, norm_ffn_g, w_ffn_in, w_ffn_down)))
    m = dict(zip(names, (m_norm_mix_g, m_w_in, m_conv_dw_w, m_conv_dw_b, m_conv_ln_g, m_conv_ln_b, m_w_conv_out,
                         m_q_norm_g, m_k_norm_g, m_sinks, m_w_attn_out, m_w_merge_out, m_norm_ffn_g, m_w_ffn_in,
                         m_w_ffn_down)))
    v = dict(zip(names, (v_norm_mix_g, v_w_in, v_conv_dw_w, v_conv_dw_b, v_conv_ln_g, v_conv_ln_b, v_w_conv_out,
                         v_q_norm_g, v_k_norm_g, v_sinks, v_w_attn_out, v_w_merge_out, v_norm_ffn_g, v_w_ffn_in,
                         v_w_ffn_down)))
    B, S, _ = x.shape
    T = B * S
    me = 4 * lax.axis_index("x") + 2 * lax.axis_index("y") + lax.axis_index("c")

    cw_shard = _pad_rows(conv_dw_w.reshape(CONV_WIDTH, 128), 32)
    g_win, g_cwt = _gather_weights_call([w_in.astype(BF16), cw_shard])
    conv_w_full = g_cwt.reshape(N_DEV, 32 * 128)[:, :CONV_WIDTH * 128].reshape(N_DEV, CONV_WIDTH, 128)
    conv_w_full = conv_w_full.transpose(1, 0, 2).reshape(CONV_WIDTH, C_CONV)
    w_in_i = _to_internal_cols(_cols_from_blocks(g_win))
    rest_names = ["w_conv_out", "w_attn_out", "w_merge_out", "w_ffn_in", "w_ffn_down"]
    rest_shards = [w[k].astype(BF16) for k in rest_names]
    rest_started = _exchange_start_call("gather_rest_start", rest_shards, scatter=False)

    def rest_weights(after):
        lands = _exchange_wait_call("gather_rest_wait", rest_started, False, after)
        mine = lax.broadcasted_iota(jnp.int32, (N_DEV, 1, 1), 0) == me
        full = [jnp.where(mine, s[None], l) for l, s in zip(lands, rest_shards)]
        wc_f, wa_f, wm_f = (f.reshape(D_MODEL, D_MODEL) for f in full[:3])
        return wc_f, wa_f, wm_f, _cols_from_blocks(full[3]), full[4].reshape(D_FF, D_MODEL)

    exchanges = []

    def grads_ready(group):
        ks = list(group)
        blocks = [_blocks_by_cols(group[k]) if k in _COL_SHARDED
                  else group[k].reshape(N_DEV, group[k].shape[0] // N_DEV, group[k].shape[1]) for k in ks]
        started = _exchange_start_call("scatter_" + ks[0] + "_start", blocks, scatter=True)
        exchanges.append((ks, started, [lax.dynamic_index_in_dim(b, me, 0, keepdims=False) for b in blocks]))
        return started[-1][0:1, 0:1]

    sse, grad_x, small = _local_step(
        x.reshape(T, D_MODEL), loss_target.reshape(T, D_MODEL), w_in_i, conv_w_full, conv_dw_b, conv_ln_g, conv_ln_b,
        q_norm_g, k_norm_g, sinks, norm_mix_g, norm_ffn_g, rest_started[-1][0:1, 0:1], rest_weights, grads_ready, B, S)

    grad, delta, new_m, new_v = {}, {}, {}, {}
    packed = jnp.concatenate([_pack_small(small), jnp.full((8, 128), sse, F32)], axis=0)
    cw_blocks = small["conv_dw_w"].reshape(CONV_WIDTH, N_DEV, 128).transpose(1, 0, 2)
    tot, cw_tot = _allsum_small_call(packed, jnp.pad(cw_blocks, ((0, 0), (0, 32 - CONV_WIDTH), (0, 0))))
    loss = 0.5 / D_MODEL * tot[_SMALL_ROWS, 0]
    g_small = _unpack_small(tot[:_SMALL_ROWS], w)
    d_s, m_s, v_s = _adamw_small_call("adamw_small", tot[:_SMALL_ROWS], _pack_small(w), _pack_small(m), _pack_small(v))
    d_small, m_small, v_small = _unpack_small(d_s, w), _unpack_small(m_s, w), _unpack_small(v_s, w)
    g_cw = cw_tot[:CONV_WIDTH]
    cw2 = lambda t: t.reshape(CONV_WIDTH, 128)
    d_cw, m_cw, v_cw = _adamw_small_call("adamw_conv_w", g_cw, cw2(conv_dw_w), cw2(m_conv_dw_w), cw2(v_conv_dw_w))
    cw3 = lambda t: t.reshape(CONV_WIDTH, 1, 128)
    for k, _ in _SMALL:
        grad[k], delta[k], new_m[k], new_v[k] = g_small[k], d_small[k], m_small[k], v_small[k]
    grad["conv_dw_w"], delta["conv_dw_w"], new_m["conv_dw_w"], new_v["conv_dw_w"] = cw3(g_cw), cw3(d_cw), cw3(m_cw), cw3(v_cw)

    after = d_cw
    for ks, started, owns in exchanges:
        lands = _exchange_wait_call("scatter_" + ks[0] + "_wait", started, True, after)
        for k, land, own in zip(ks, lands, owns):
            tr = {1024: 256, 352: 176, 128: 128}[w[k].shape[0]]
            grad[k], delta[k], new_m[k], new_v[k] = _adamw_sum_call("adamw_" + k, land, own, w[k], m[k], v[k], tr)
            after = new_v[k]

    return (loss, grad_x.reshape(B, S, D_MODEL), *[grad[k] for k in names], *[delta[k] for k in names],
            *[new_m[k] for k in names], *[new_v[k] for k in names])
```

```python
import functools
import math

import jax
import jax.numpy as jnp
from jax import lax
from jax.experimental import pallas as pl
from jax.experimental.pallas import tpu as pltpu

F32 = jnp.float32
BF16 = jnp.bfloat16
MESH = pl.DeviceIdType.MESH

N_DEV = 8
D_MODEL = 1024
C_CONV = 1024
CONV_WIDTH = 31
HEAD_DIM = 64
N_Q_HEADS = 16
N_KV_HEADS = 2
GROUP = N_Q_HEADS // N_KV_HEADS
WINDOW = 128
BLOCK = 128
D_FF = 2816
EPS = 1e-6
NEG = -1e30
ATTN_SCALE = 1.0 / math.sqrt(HEAD_DIM)
Q_W = N_Q_HEADS * HEAD_DIM
KV_W = N_KV_HEADS * HEAD_DIM
IN_COLS = 2 * C_CONV + Q_W + 2 * KV_W + 2 * D_MODEL

ADAM_LR = 0.001
ADAM_B1 = 0.9
ADAM_B2 = 0.999
ADAM_EPS = 1e-08
ADAM_WD = 0.01
ADAM_STEP = 10

COL_A, COL_GATE, COL_GC, COL_GA, COL_Q, COL_K, COL_V = 0, 1024, 2048, 3072, 4096, 5120, 5248

HALO = 32
VMEM_LIMIT = 56 * 1024 * 1024


def _cp(*sem):
    return pltpu.CompilerParams(dimension_semantics=sem, vmem_limit_bytes=VMEM_LIMIT)


def _sigmoid(x):
    return jax.nn.sigmoid(x)


def _dot(a, b):
    return jnp.dot(a, b, preferred_element_type=F32)


def _dot_nt(a, b):
    return lax.dot_general(a, b, (((1,), (1,)), ((), ())), preferred_element_type=F32)


def _dot_tn(a, b):
    return lax.dot_general(a, b, (((0,), (0,)), ((), ())), preferred_element_type=F32)


def _full(shape):
    return pl.BlockSpec(shape, lambda *_: (0,) * len(shape))


def _rows(tm, ncols, colblk=0):
    return pl.BlockSpec((tm, ncols), lambda i: (i, colblk))


def _inproj_call(x, g, w, tm=512):
    T = x.shape[0]
    N = w.shape[0]

    def body(x_ref, g_ref, w_ref, u_ref, xn_ref):
        xv = x_ref[...]
        r = lax.rsqrt(jnp.mean(xv * xv, axis=-1, keepdims=True) + EPS)
        xn = (xv * r * g_ref[...]).astype(BF16)
        xn_ref[...] = xn
        u_ref[...] = _dot_nt(xn, w_ref[...]).astype(BF16)

    return pl.pallas_call(
        body, name="inproj", grid=(T // tm,),
        in_specs=[_rows(tm, D_MODEL), _full((1, D_MODEL)), _full(w.shape)],
        out_specs=[_rows(tm, N), _rows(tm, D_MODEL)],
        out_shape=[jax.ShapeDtypeStruct((T, N), BF16), jax.ShapeDtypeStruct((T, D_MODEL), BF16)],
        compiler_params=_cp("parallel"),
    )(x, g, w)


def _fill_shifted(src_ref, sh_ref):
    n = src_ref.shape[0] - 8
    for s in range(1, 8):
        sh_ref[s - 1, 0:n, :] = src_ref[s:s + n, :]


def _window(src_ref, sh_ref, off, cols):
    s = off % 8
    if s == 0:
        return src_ref[off:off + 128, cols]
    return sh_ref[s - 1, off - s:off - s + 128, cols]


def _conv_taps(w_ref, src_ref, sh_ref, base, rows, cb, reverse):
    cols = pl.ds(pl.multiple_of(cb * 128, 128), 128)
    outs = []
    for rb in range(rows // 128):
        acc = jnp.zeros((128, 128), F32)
        for j in range(CONV_WIDTH):
            off = base + (CONV_WIDTH - 1 - j if reverse else j) + rb * 128
            acc = acc + w_ref[j:j + 1, cols] * _window(src_ref, sh_ref, off, cols)
        outs.append(acc)
    return outs


def _conv_fwd_call(u, conv_w, conv_b, ln_g, ln_b, B, S, ts=512):
    T = B * S
    nS = S // ts
    per32 = ts // HALO

    def body(a_ref, gate_ref, ap_ref, gp_ref, w_ref, b_ref, g_ref, bb_ref, hc_ref, c_ref, hext, hsh, acc_ref):
        s = pl.program_id(1)
        prev = ap_ref[...].astype(F32) * _sigmoid(gp_ref[...].astype(F32))
        hext[0:HALO, :] = jnp.where(s > 0, prev, 0.0)
        hext[HALO:, :] = a_ref[...].astype(F32) * _sigmoid(gate_ref[...].astype(F32))
        _fill_shifted(hext, hsh)

        def colblock(cb, carry):
            cols = pl.ds(pl.multiple_of(cb * 128, 128), 128)
            outs = _conv_taps(w_ref, hext, hsh, HALO - (CONV_WIDTH - 1), ts, cb, reverse=False)
            for rb, acc in enumerate(outs):
                acc_ref[rb * 128:(rb + 1) * 128, cols] = acc
            return carry

        lax.fori_loop(0, C_CONV // 128, colblock, 0)
        hc = acc_ref[...] + b_ref[...]
        hc_ref[...] = hc
        mu = jnp.mean(hc, axis=-1, keepdims=True)
        xc = hc - mu
        var = jnp.mean(xc * xc, axis=-1, keepdims=True)
        y = xc * lax.rsqrt(var + EPS) * g_ref[...] + bb_ref[...]
        c_ref[...] = (y * _sigmoid(y)).astype(BF16)

    cur = lambda cb: pl.BlockSpec((ts, C_CONV), lambda b, s: (b * nS + s, cb))
    prv = lambda cb: pl.BlockSpec((HALO, C_CONV), lambda b, s: (jnp.maximum(b * (S // HALO) + s * per32 - 1, 0), cb))
    vec = _full((1, C_CONV))
    return pl.pallas_call(
        body, name="conv_fwd", grid=(B, nS),
        in_specs=[cur(0), cur(1), prv(0), prv(1), _full((CONV_WIDTH, C_CONV)), vec, vec, vec],
        out_specs=[pl.BlockSpec((ts, C_CONV), lambda b, s: (b * nS + s, 0))] * 2,
        out_shape=[jax.ShapeDtypeStruct((T, C_CONV), F32), jax.ShapeDtypeStruct((T, C_CONV), BF16)],
        scratch_shapes=[pltpu.VMEM((HALO + ts, C_CONV), F32), pltpu.VMEM((7, HALO + ts, C_CONV), F32),
                        pltpu.VMEM((ts, C_CONV), F32)],
        compiler_params=_cp("parallel", "arbitrary"),
    )(u, u, u, u, conv_w, conv_b, ln_g, ln_b)


def _lane_lo():
    return lax.broadcasted_iota(jnp.int32, (1, 128), 1) < HEAD_DIM


def _half_sums(t, lo):
    s_lo = jnp.sum(jnp.where(lo, t, 0.0), axis=-1, keepdims=True)
    s_hi = jnp.sum(jnp.where(lo, 0.0, t), axis=-1, keepdims=True)
    return jnp.where(lo, s_lo, s_hi)


def _head_rstd(t, lo):
    return lax.rsqrt(_half_sums(t * t, lo) * (1.0 / HEAD_DIM) + EPS)


def _fill_bias(btab, kvh, slopes_ref):
    sj = lax.broadcasted_iota(jnp.int32, (2 * BLOCK, BLOCK), 0)
    qi = lax.broadcasted_iota(jnp.int32, (2 * BLOCK, BLOCK), 1)
    dist = qi + BLOCK - sj
    valid = (dist >= 0) & (dist < WINDOW)
    distf = dist.astype(F32)
    for g in range(GROUP):
        bias = jnp.where(valid, -slopes_ref[kvh * GROUP + g] * distf, NEG)
        btab[1, g] = bias
        btab[0, g] = jnp.where(sj >= BLOCK, bias, NEG)


def _softmax_cols(st_ref, btab, p, var, g, hf, sk):
    sh = st_ref[2 * BLOCK * hf:2 * BLOCK * (hf + 1), 128 * p:128 * (p + 1)] + btab[var, g]
    m = jnp.maximum(jnp.max(sh, axis=0, keepdims=True), sk)
    e = jnp.exp(sh - m)
    esk = jnp.exp(sk - m)
    rz = 1.0 / (jnp.sum(e, axis=0, keepdims=True) + esk)
    return e * rz, esk, rz


def _stage_kv(kvh, k_ref, v_ref, kg_ref, lo, kA, kB, vA, vB):
    kv = k_ref[...].astype(F32)
    rk = _head_rstd(kv, lo)
    kn = kv * rk * kg_ref[...]
    mine = lax.broadcasted_iota(jnp.int32, (1, 128), 1) // HEAD_DIM == kvh
    ksel = jnp.where(mine, kn, 0.0)
    vsel = jnp.where(mine, v_ref[...].astype(F32), 0.0)
    krol = pltpu.roll(ksel, HEAD_DIM, 1)
    vrol = pltpu.roll(vsel, HEAD_DIM, 1)
    first = kvh == 0
    zeros = jnp.zeros((BLOCK, 128), BF16)
    for ref, val in ((kA, jnp.where(first, ksel, krol)), (kB, jnp.where(first, krol, ksel)),
                     (vA, jnp.where(first, vsel, vrol)), (vB, jnp.where(first, vrol, vsel))):
        ref[0:BLOCK, :] = zeros
        ref[BLOCK:, :] = val.astype(BF16)


def _attn_fwd_call(u, qg2, kg2, sinks, slopes, B, S):
    T = B * S
    nb = S // BLOCK
    half = Q_W // 2
    npair = GROUP // 2

    def body(sinks_ref, slopes_ref, q_ref, k_ref, v_ref, qg_ref, kg_ref, o_ref, kA, kB, vA, vB, btab, qn_buf, s_buf, p_buf):
        kvh = pl.program_id(1)
        lo = _lane_lo()
        _stage_kv(kvh, k_ref, v_ref, kg_ref, lo, kA, kB, vA, vB)
        _fill_bias(btab, kvh, slopes_ref)
        qgs = qg_ref[...] * ATTN_SCALE

        def block(n, slot):
            qn_scr, s_scr, p_scr = qn_buf.at[slot], s_buf.at[slot], p_buf.at[slot]
            r0 = pl.multiple_of(n * BLOCK, BLOCK)
            win = pl.ds(r0, 2 * BLOCK)
            k2t = jnp.concatenate([kA[win, :], kB[win, :]], axis=0)
            v2 = jnp.concatenate([vA[win, :], vB[win, :]], axis=0)
            var = jnp.minimum(n, 1)
            for p in range(npair):
                pc = slice(128 * p, 128 * (p + 1))
                qp = q_ref[pl.ds(r0, BLOCK), pc].astype(F32)
                qn_scr[pc, :] = (qp * _head_rstd(qp, lo) * qgs).astype(BF16)
            s_scr[...] = _dot_nt(k2t, qn_scr[...])
            for p in range(npair):
                for hf in range(2):
                    g = 2 * p + hf
                    pn, _, _ = _softmax_cols(s_scr, btab, p, var, g, hf, sinks_ref[kvh * GROUP + g])
                    p_scr[2 * BLOCK * hf:2 * BLOCK * (hf + 1), 128 * p:128 * (p + 1)] = pn.astype(BF16)
            o_all = _dot_tn(v2, p_scr[...]).T
            for p in range(npair):
                pc = slice(128 * p, 128 * (p + 1))
                o_ref[pl.ds(r0, BLOCK), pc] = o_all[pc, :].astype(BF16)

        def two_blocks(i, carry):
            block(2 * i, 0)
            block(2 * i + 1, 1)
            return carry

        lax.fori_loop(0, nb // 2, two_blocks, 0)

    smem = pl.BlockSpec(memory_space=pltpu.SMEM)
    kvbuf = pltpu.VMEM((S + BLOCK, 128), BF16)
    pair_scores = (2, npair * BLOCK, 4 * BLOCK)
    return pl.pallas_call(
        body, name="attn_fwd", grid=(B, N_KV_HEADS),
        in_specs=[smem, smem,
                  pl.BlockSpec((S, half), lambda b, h: (b, COL_Q // half + h)),
                  pl.BlockSpec((S, 128), lambda b, h: (b, COL_K // 128)),
                  pl.BlockSpec((S, 128), lambda b, h: (b, COL_V // 128)),
                  _full((1, 128)), _full((1, 128))],
        out_specs=pl.BlockSpec((S, half), lambda b, h: (b, h)),
        out_shape=jax.ShapeDtypeStruct((T, Q_W), BF16),
        scratch_shapes=[kvbuf, kvbuf, kvbuf, kvbuf, pltpu.VMEM((2, GROUP, 2 * BLOCK, BLOCK), F32),
                        pltpu.VMEM((2, npair * BLOCK, 128), BF16), pltpu.VMEM(pair_scores, F32), pltpu.VMEM(pair_scores, BF16)],
        compiler_params=_cp("parallel", "arbitrary"),
    )(sinks, slopes, u, u, u, qg2, kg2)


def _merge_call(c, o, u, x, wc, wa, wm, g_ffn, tm=512):
    T = x.shape[0]

    def body(c_ref, o_ref, gc_ref, ga_ref, x_ref, wc_ref, wa_ref, wm_ref, g_ref, yc_ref, ya_ref, z_ref, h_ref, hn_ref):
        yc = _dot(c_ref[...], wc_ref[...])
        ya = _dot(o_ref[...], wa_ref[...])
        yc_ref[...] = yc.astype(BF16)
        ya_ref[...] = ya.astype(BF16)
        z = (_sigmoid(gc_ref[...].astype(F32)) * yc + _sigmoid(ga_ref[...].astype(F32)) * ya).astype(BF16)
        z_ref[...] = z
        h = x_ref[...] + _dot(z, wm_ref[...])
        h_ref[...] = h
        r = lax.rsqrt(jnp.mean(h * h, axis=-1, keepdims=True) + EPS)
        hn_ref[...] = (h * r * g_ref[...]).astype(BF16)

    sq = _full((D_MODEL, D_MODEL))
    row = _rows(tm, D_MODEL)
    f32 = jax.ShapeDtypeStruct((T, D_MODEL), F32)
    b16 = jax.ShapeDtypeStruct((T, D_MODEL), BF16)
    return pl.pallas_call(
        body, name="merge_fwd", grid=(T // tm,),
        in_specs=[row, row, _rows(tm, D_MODEL, COL_GC // D_MODEL), _rows(tm, D_MODEL, COL_GA // D_MODEL), row,
                  sq, sq, sq, _full((1, D_MODEL))],
        out_specs=[row] * 5,
        out_shape=[b16, b16, b16, f32, b16],
        compiler_params=_cp("parallel"),
    )(c, o, u, u, x, wc, wa, wm, g_ffn)


def _ffn_in_call(hn, w, tm=512):
    T = hn.shape[0]

    def body(hn_ref, w_ref, fac_ref, act_ref):
        gu = _dot_nt(hn_ref[...], w_ref[...])
        g = gu[:, :D_FF]
        up = gu[:, D_FF:]
        sg = _sigmoid(g)
        silu = g * sg
        act_ref[...] = (silu * up).astype(BF16)
        fac_ref[:, :D_FF] = (up * (sg * (1.0 + g * (1.0 - sg)))).astype(BF16)
        fac_ref[:, D_FF:] = silu.astype(BF16)

    return pl.pallas_call(
        body, name="ffn_in", grid=(T // tm,),
        in_specs=[_rows(tm, D_MODEL), _full(w.shape)],
        out_specs=[_rows(tm, 2 * D_FF), _rows(tm, D_FF)],
        out_shape=[jax.ShapeDtypeStruct((T, 2 * D_FF), BF16), jax.ShapeDtypeStruct((T, D_FF), BF16)],
        compiler_params=_cp("parallel"),
    )(hn, w)


def _ffn_out_call(act, wd, h, tgt, tm=512):
    T = h.shape[0]

    def body(act_ref, wd_ref, h_ref, t_ref, sse_ref, d_ref, db_ref):
        e = h_ref[...] + _dot(act_ref[...], wd_ref[...]) - t_ref[...]

        @pl.when(pl.program_id(0) == 0)
        def _():
            sse_ref[...] = jnp.zeros_like(sse_ref)

        sse_ref[...] += jnp.sum(e * e)
        d = e * (1.0 / D_MODEL)
        d_ref[...] = d
        db_ref[...] = d.astype(BF16)

    row = _rows(tm, D_MODEL)
    return pl.pallas_call(
        body, name="ffn_out", grid=(T // tm,),
        in_specs=[_rows(tm, D_FF), _full(wd.shape), row, row],
        out_specs=[_full((8, 128)), row, row],
        out_shape=[jax.ShapeDtypeStruct((8, 128), F32), jax.ShapeDtypeStruct((T, D_MODEL), F32),
                   jax.ShapeDtypeStruct((T, D_MODEL), BF16)],
        compiler_params=_cp("arbitrary"),
    )(act, wd, h, tgt)


def _dw_call(name, a, b, tr, tk=2048):
    T, K = a.shape
    N = b.shape[1]
    tk = min(tk, T)
    nk = T // tk

    def body(a_ref, b_ref, o_ref, acc_ref):
        k = pl.program_id(1)

        @pl.when(k == 0)
        def _():
            acc_ref[...] = jnp.zeros_like(acc_ref)

        acc_ref[...] += _dot_tn(a_ref[...], b_ref[...])

        @pl.when(k == nk - 1)
        def _():
            o_ref[...] = acc_ref[...].astype(BF16)

    return pl.pallas_call(
        body, name=name, grid=(K // tr, nk),
        in_specs=[pl.BlockSpec((tk, tr), lambda j, k: (k, j)), pl.BlockSpec((tk, N), lambda j, k: (k, 0))],
        out_specs=pl.BlockSpec((tr, N), lambda j, k: (j, 0)),
        out_shape=jax.ShapeDtypeStruct((K, N), BF16),
        scratch_shapes=[pltpu.VMEM((tr, N), F32)],
        compiler_params=_cp("parallel", "arbitrary"),
    )(a, b)


def _rms_bwd(dy, xv, g):
    r = lax.rsqrt(jnp.mean(xv * xv, axis=-1, keepdims=True) + EPS)
    t = dy * g
    dx = r * t - xv * (r * r * r * jnp.mean(t * xv, axis=-1, keepdims=True))
    return dx, dy * xv * r


def _ffn_down_bwd_call(dout_b, wd, gu, tm=512):
    T = dout_b.shape[0]

    def body(d_ref, wd_ref, fg_ref, fu_ref, o_ref):
        dact = _dot_nt(d_ref[...], wd_ref[...])
        o_ref[:, :D_FF] = (dact * fg_ref[...].astype(F32)).astype(BF16)
        o_ref[:, D_FF:] = (dact * fu_ref[...].astype(F32)).astype(BF16)

    return pl.pallas_call(
        body, name="ffn_down_bwd", grid=(T // tm,),
        in_specs=[_rows(tm, D_MODEL), _full(wd.shape), _rows(tm, D_FF, 0), _rows(tm, D_FF, 1)],
        out_specs=_rows(tm, 2 * D_FF),
        out_shape=jax.ShapeDtypeStruct((T, 2 * D_FF), BF16),
        compiler_params=_cp("parallel"),
    )(dout_b, wd, gu, gu)


def _ffn_in_bwd_call(dgu, w, h, g_ffn, dout, tm=512):
    T = h.shape[0]

    def body(dgu_ref, w_ref, h_ref, g_ref, d_ref, dh_ref, dhb_ref, dg_ref):
        dhn = _dot(dgu_ref[...], w_ref[...])
        dx, dgt = _rms_bwd(dhn, h_ref[...], g_ref[...])
        dh = d_ref[...] + dx
        dh_ref[...] = dh
        dhb_ref[...] = dh.astype(BF16)

        @pl.when(pl.program_id(0) == 0)
        def _():
            dg_ref[...] = jnp.zeros_like(dg_ref)

        dg_ref[...] += jnp.sum(dgt, axis=0, keepdims=True)

    row = _rows(tm, D_MODEL)
    return pl.pallas_call(
        body, name="ffn_in_bwd", grid=(T // tm,),
        in_specs=[_rows(tm, 2 * D_FF), _full(w.shape), row, _full((1, D_MODEL)), row],
        out_specs=[row, row, _full((1, D_MODEL))],
        out_shape=[jax.ShapeDtypeStruct((T, D_MODEL), F32), jax.ShapeDtypeStruct((T, D_MODEL), BF16),
                   jax.ShapeDtypeStruct((1, D_MODEL), F32)],
        compiler_params=_cp("arbitrary"),
    )(dgu, w, h, g_ffn, dout)


def _merge_bwd_call(dh_b, wm, wc, wa, u, yc, ya, tm=512):
    T = dh_b.shape[0]

    def body(dh_ref, wm_ref, wc_ref, wa_ref, gc_ref, ga_ref, yc_ref, ya_ref, dyc_ref, dya_ref, dug_ref, dcp_ref, do_ref):
        dz = _dot_nt(dh_ref[...], wm_ref[...])
        sgc = _sigmoid(gc_ref[...].astype(F32))
        sga = _sigmoid(ga_ref[...].astype(F32))
        dyc = (dz * sgc).astype(BF16)
        dya = (dz * sga).astype(BF16)
        dyc_ref[...] = dyc
        dya_ref[...] = dya
        dug_ref[:, :D_MODEL] = (dz * yc_ref[...].astype(F32) * (sgc * (1.0 - sgc))).astype(BF16)
        dug_ref[:, D_MODEL:] = (dz * ya_ref[...].astype(F32) * (sga * (1.0 - sga))).astype(BF16)
        dcp_ref[...] = _dot_nt(dyc, wc_ref[...])
        do_ref[...] = _dot_nt(dya, wa_ref[...]).astype(BF16)

    sq = _full((D_MODEL, D_MODEL))
    row = _rows(tm, D_MODEL)
    b16 = jax.ShapeDtypeStruct((T, D_MODEL), BF16)
    return pl.pallas_call(
        body, name="merge_bwd", grid=(T // tm,),
        in_specs=[row, sq, sq, sq, _rows(tm, D_MODEL, COL_GC // D_MODEL), _rows(tm, D_MODEL, COL_GA // D_MODEL), row, row],
        out_specs=[row, row, _rows(tm, 2 * D_MODEL), row, row],
        out_shape=[b16, b16, jax.ShapeDtypeStruct((T, 2 * D_MODEL), BF16), jax.ShapeDtypeStruct((T, D_MODEL), F32), b16],
        compiler_params=_cp("parallel"),
    )(dh_b, wm, wc, wa, u, u, yc, ya)


def _conv_bwd_call(dcp, hc, u, conv_w, ln_g, ln_b, B, S, ts=256):
    T = B * S
    nS = S // ts
    per32 = ts // HALO
    last32 = T // HALO - 1

    def ln_bwd(dcp_v, hc_v, g, bb):
        mu = jnp.mean(hc_v, axis=-1, keepdims=True)
        xc = hc_v - mu
        rstd = lax.rsqrt(jnp.mean(xc * xc, axis=-1, keepdims=True) + EPS)
        xhat = xc * rstd
        y = xhat * g + bb
        sy = _sigmoid(y)
        dy = dcp_v * (sy * (1.0 + y * (1.0 - sy)))
        dxh = dy * g
        dhc = rstd * (dxh - jnp.mean(dxh, axis=-1, keepdims=True) - xhat * jnp.mean(dxh * xhat, axis=-1, keepdims=True))
        return dhc, dy * xhat, dy

    def body(dcp_ref, dcpn_ref, hc_ref, hcn_ref, a_ref, gate_ref, w_ref, g_ref, bb_ref,
             du_ref, dw_ref, db_ref, dlg_ref, dlb_ref, dhext, glu_scr, dhsh, acc_ref):
        b = pl.program_id(0)
        s = pl.program_id(1)

        @pl.when((b == 0) & (s == 0))
        def _():
            dw_ref[...] = jnp.zeros_like(dw_ref)
            db_ref[...] = jnp.zeros_like(db_ref)
            dlg_ref[...] = jnp.zeros_like(dlg_ref)
            dlb_ref[...] = jnp.zeros_like(dlb_ref)

        dhc, dgt, dbt = ln_bwd(dcp_ref[...], hc_ref[...], g_ref[...], bb_ref[...])
        dhn, _, _ = ln_bwd(dcpn_ref[...], hcn_ref[...], g_ref[...], bb_ref[...])
        dhext[0:ts, :] = dhc
        dhext[ts:, :] = jnp.where(s < nS - 1, dhn, 0.0)
        db_ref[...] += jnp.sum(dhc, axis=0, keepdims=True)
        dlg_ref[...] += jnp.sum(dgt, axis=0, keepdims=True)
        dlb_ref[...] += jnp.sum(dbt, axis=0, keepdims=True)

        sg = _sigmoid(gate_ref[...].astype(F32))
        av = a_ref[...].astype(F32)
        glu_scr[...] = av * sg
        _fill_shifted(dhext, dhsh)

        def colblock(cb, carry):
            cols = pl.ds(pl.multiple_of(cb * 128, 128), 128)
            outs = _conv_taps(w_ref, dhext, dhsh, 0, ts, cb, reverse=True)
            for rb, acc in enumerate(outs):
                acc_ref[rb * 128:(rb + 1) * 128, cols] = acc
            taps = [jnp.zeros((8, 128), F32) for _ in range(CONV_WIDTH)]
            for r0 in range(0, ts, 128):
                glu = [glu_scr[r0 + 8 * i:r0 + 8 * (i + 1), cols] for i in range(16)]
                for s8 in range(8):
                    users = {}
                    for j in range(CONV_WIDTH):
                        if (CONV_WIDTH - 1 - j) % 8 == s8:
                            for i in range(16):
                                users.setdefault(CONV_WIDTH - 1 - j + r0 + 8 * i - s8, []).append((j, i))
                    for row, pairs in sorted(users.items()):
                        dhc8 = dhext[row:row + 8, cols] if s8 == 0 else dhsh[s8 - 1, row:row + 8, cols]
                        for j, i in pairs:
                            taps[j] = taps[j] + glu[i] * dhc8
            for j in range(CONV_WIDTH):
                dw_ref[8 * j:8 * (j + 1), cols] += taps[j]
            return carry

        lax.fori_loop(0, C_CONV // 128, colblock, 0)
        dglu = acc_ref[...]
        du_ref[:, :C_CONV] = (dglu * sg).astype(BF16)
        du_ref[:, C_CONV:] = (dglu * av * (sg * (1.0 - sg))).astype(BF16)

    cur = lambda cb: pl.BlockSpec((ts, C_CONV), lambda b, s: (b * nS + s, cb))
    nxt = pl.BlockSpec((HALO, C_CONV), lambda b, s: (jnp.minimum(b * (S // HALO) + (s + 1) * per32, last32), 0))
    vec = _full((1, C_CONV))
    return pl.pallas_call(
        body, name="conv_bwd", grid=(B, nS),
        in_specs=[cur(0), nxt, cur(0), nxt, cur(0), cur(1), _full((CONV_WIDTH, C_CONV)), vec, vec],
        out_specs=[pl.BlockSpec((ts, 2 * C_CONV), lambda b, s: (b * nS + s, 0)), _full((8 * CONV_WIDTH, C_CONV)), vec, vec, vec],
        out_shape=[jax.ShapeDtypeStruct((T, 2 * C_CONV), BF16), jax.ShapeDtypeStruct((8 * CONV_WIDTH, C_CONV), F32),
                   jax.ShapeDtypeStruct((1, C_CONV), F32), jax.ShapeDtypeStruct((1, C_CONV), F32),
                   jax.ShapeDtypeStruct((1, C_CONV), F32)],
        scratch_shapes=[pltpu.VMEM((ts + HALO, C_CONV), F32), pltpu.VMEM((ts, C_CONV), F32),
                        pltpu.VMEM((7, ts + HALO, C_CONV), F32), pltpu.VMEM((ts, C_CONV), F32)],
        compiler_params=_cp("arbitrary", "arbitrary"),
    )(dcp, dcp, hc, hc, u, u, conv_w, ln_g, ln_b)


def _attn_bwd_call(u, do, qg2, kg2, sinks, slopes, B, S):
    T = B * S
    nb = S // BLOCK
    half = Q_W // 2
    npair = GROUP // 2

    def body(sinks_ref, slopes_ref, q_ref, k_ref, v_ref, do_ref, qg_ref, kg_ref,
             dq_ref, dkv_ref, dqg_ref, dkg_ref, dsk_ref, kA, kB, vA, vB, dkn_acc, dv_acc, dkv_keep,
             btab, s_buf, dp_buf, p_buf, ds_buf, qn_buf, do_buf, rq_buf):
        b = pl.program_id(0)
        kvh = pl.program_id(1)
        lo = _lane_lo()
        lane = lax.broadcasted_iota(jnp.int32, (1, 128), 1)
        first = kvh == 0

        @pl.when((b == 0) & first)
        def _():
            dqg_ref[...] = jnp.zeros_like(dqg_ref)
            dkg_ref[...] = jnp.zeros_like(dkg_ref)
            dsk_ref[...] = jnp.zeros_like(dsk_ref)

        _stage_kv(kvh, k_ref, v_ref, kg_ref, lo, kA, kB, vA, vB)
        dkn_acc[...] = jnp.zeros_like(dkn_acc)
        dv_acc[...] = jnp.zeros_like(dv_acc)
        _fill_bias(btab, kvh, slopes_ref)
        qgs = qg_ref[...] * ATTN_SCALE

        def fold(acc):
            canon = jnp.where(lo, acc[0:2 * BLOCK] + pltpu.roll(acc[2 * BLOCK:], HEAD_DIM, 1), 0.0)
            return jnp.where(first, canon, pltpu.roll(canon, HEAD_DIM, 1))

        def block(n, slot):
            s_scr, dp_scr, p_scr, ds_scr = s_buf.at[slot], dp_buf.at[slot], p_buf.at[slot], ds_buf.at[slot]
            qn_scr, do_scr, rq_scr = qn_buf.at[slot], do_buf.at[slot], rq_buf.at[slot]
            r0 = pl.multiple_of(n * BLOCK, BLOCK)
            win = pl.ds(r0, 2 * BLOCK)
            rows = pl.ds(r0, BLOCK)
            k2t = jnp.concatenate([kA[win, :], kB[win, :]], axis=0)
            v2 = jnp.concatenate([vA[win, :], vB[win, :]], axis=0)
            var = jnp.minimum(n, 1)
            for p in range(npair):
                pc = slice(128 * p, 128 * (p + 1))
                qp = q_ref[rows, pc].astype(F32)
                rq = _head_rstd(qp, lo)
                rq_scr[pc, :] = rq
                qn_scr[pc, :] = (qp * rq * qgs).astype(BF16)
                do_scr[pc, :] = do_ref[rows, pc]
            s_scr[...] = _dot_nt(k2t, qn_scr[...])
            dp_scr[...] = _dot_nt(v2, do_scr[...])
            for p in range(npair):
                for hf in range(2):
                    g = 2 * p + hf
                    keys, qs = slice(2 * BLOCK * hf, 2 * BLOCK * (hf + 1)), slice(128 * p, 128 * (p + 1))
                    pn, esk, rz = _softmax_cols(s_scr, btab, p, var, g, hf, sinks_ref[kvh * GROUP + g])
                    dph = dp_scr[keys, qs]
                    delta = jnp.sum(pn * dph, axis=0, keepdims=True)
                    ds_scr[keys, qs] = (pn * (dph - delta)).astype(BF16)
                    p_scr[keys, qs] = pn.astype(BF16)
                    dsk_ref[...] += jnp.where(lane == kvh * GROUP + g, -jnp.sum(esk * rz * delta), 0.0)
            dqs_all = _dot_tn(k2t, ds_scr[...]).T
            for p in range(npair):
                pc = slice(128 * p, 128 * (p + 1))
                qp = q_ref[rows, pc].astype(F32)
                rq = rq_scr[pc, :]
                dqs = dqs_all[pc, :]
                t = dqs * qgs
                dq = rq * t - qp * (rq * rq * rq * (_half_sums(t * qp, lo) * (1.0 / HEAD_DIM)))
                dq_ref[rows, pc] = dq.astype(BF16)
                dqg_ref[...] += jnp.sum(dqs * qp * rq, axis=0, keepdims=True) * ATTN_SCALE
            dkn_acc[win, :] += fold(_dot(ds_scr[...], qn_scr[...]))
            dv_acc[win, :] += fold(_dot(p_scr[...], do_scr[...]))

        def two_blocks(i, carry):
            block(2 * i, 0)
            block(2 * i + 1, 1)
            return carry

        lax.fori_loop(0, nb // 2, two_blocks, 0)

        dkn = dkn_acc[BLOCK:, :]
        kv = k_ref[...].astype(F32)
        kg = kg_ref[...]
        rk = _head_rstd(kv, lo)
        t = dkn * kg
        dk = rk * t - kv * (rk * rk * rk * (_half_sums(t * kv, lo) * (1.0 / HEAD_DIM)))
        dkg_ref[...] += jnp.sum(dkn * kv * rk, axis=0, keepdims=True)
        dv = dv_acc[BLOCK:, :]

        @pl.when(first)
        def _():
            dkv_keep[:, :128] = dk
            dkv_keep[:, 128:] = dv

        @pl.when(kvh == N_KV_HEADS - 1)
        def _():
            dkv_ref[:, :128] = (dkv_keep[:, :128] + dk).astype(BF16)
            dkv_ref[:, 128:] = (dkv_keep[:, 128:] + dv).astype(BF16)

    smem = pl.BlockSpec(memory_space=pltpu.SMEM)
    kvbuf = pltpu.VMEM((S + BLOCK, 128), BF16)
    accbuf = pltpu.VMEM((S + BLOCK, 128), F32)
    vec = _full((1, 128))
    return pl.pallas_call(
        body, name="attn_bwd", grid=(B, N_KV_HEADS),
        in_specs=[smem, smem,
                  pl.BlockSpec((S, half), lambda b, h: (b, COL_Q // half + h)),
                  pl.BlockSpec((S, 128), lambda b, h: (b, COL_K // 128)),
                  pl.BlockSpec((S, 128), lambda b, h: (b, COL_V // 128)),
                  pl.BlockSpec((S, half), lambda b, h: (b, h)),
                  vec, vec],
        out_specs=[pl.BlockSpec((S, half), lambda b, h: (b, h)), pl.BlockSpec((S, 256), lambda b, h: (b, 0)), vec, vec, vec],
        out_shape=[jax.ShapeDtypeStruct((T, Q_W), BF16), jax.ShapeDtypeStruct((T, 2 * KV_W), BF16),
                   jax.ShapeDtypeStruct((1, 128), F32), jax.ShapeDtypeStruct((1, 128), F32), jax.ShapeDtypeStruct((1, 128), F32)],
        scratch_shapes=[kvbuf, kvbuf, kvbuf, kvbuf, accbuf, accbuf, pltpu.VMEM((S, 256), F32),
                        pltpu.VMEM((2, GROUP, 2 * BLOCK, BLOCK), F32),
                        pltpu.VMEM((2, npair * BLOCK, 4 * BLOCK), F32), pltpu.VMEM((2, npair * BLOCK, 4 * BLOCK), F32),
                        pltpu.VMEM((2, npair * BLOCK, 4 * BLOCK), BF16), pltpu.VMEM((2, npair * BLOCK, 4 * BLOCK), BF16),
                        pltpu.VMEM((2, npair * BLOCK, 128), BF16), pltpu.VMEM((2, npair * BLOCK, 128), BF16),
                        pltpu.VMEM((2, npair * BLOCK, 128), F32)],
        compiler_params=_cp("arbitrary", "arbitrary"),
    )(sinks, slopes, u, u, u, do, qg2, kg2)


def _inproj_bwd_call(du_conv, du_g, du_q, du_kv, w, x, g, dh, tm=512):
    T = x.shape[0]

    def body(dc_ref, dg_ref, dq_ref, dkv_ref, w_ref, x_ref, g_ref, dh_ref, gx_ref, dgm_ref):
        dxn = _dot(dc_ref[...], w_ref[COL_A:COL_GC, :])
        dxn += _dot(dg_ref[...], w_ref[COL_GC:COL_Q, :])
        dxn += _dot(dq_ref[...], w_ref[COL_Q:COL_K, :])
        dxn += _dot(dkv_ref[...], w_ref[COL_K:, :])
        dx, dgt = _rms_bwd(dxn, x_ref[...], g_ref[...])
        gx_ref[...] = dh_ref[...] + dx

        @pl.when(pl.program_id(0) == 0)
        def _():
            dgm_ref[...] = jnp.zeros_like(dgm_ref)

        dgm_ref[...] += jnp.sum(dgt, axis=0, keepdims=True)

    row = _rows(tm, D_MODEL)
    return pl.pallas_call(
        body, name="inproj_bwd", grid=(T // tm,),
        in_specs=[_rows(tm, 2 * C_CONV), _rows(tm, 2 * D_MODEL), _rows(tm, Q_W), _rows(tm, 2 * KV_W), _full(w.shape),
                  row, _full((1, D_MODEL)), row],
        out_specs=[row, _full((1, D_MODEL))],
        out_shape=[jax.ShapeDtypeStruct((T, D_MODEL), F32), jax.ShapeDtypeStruct((1, D_MODEL), F32)],
        compiler_params=_cp("arbitrary"),
    )(du_conv, du_g, du_q, du_kv, w, x, g, dh)


def _adamw_math(w, g, m, v):
    m = ADAM_B1 * m + (1.0 - ADAM_B1) * g
    v = ADAM_B2 * v + (1.0 - ADAM_B2) * (g * g)
    m_hat = m / (1.0 - ADAM_B1 ** ADAM_STEP)
    v_hat = v / (1.0 - ADAM_B2 ** ADAM_STEP)
    delta = -ADAM_LR * (m_hat / (jnp.sqrt(v_hat) + ADAM_EPS) + ADAM_WD * w)
    return delta, m, v


def _adamw_sum_call(name, parts, own, w, m, v, tr):
    R, C = w.shape

    def body(p_ref, own_ref, w_ref, m_ref, v_ref, g_ref, d_ref, nm_ref, nv_ref):
        x, y, c = _position()
        me = 4 * x + 2 * y + c
        g = jnp.zeros((tr, C), F32)
        for s in range(N_DEV):
            g = g + jnp.where(me == s, own_ref[...], p_ref[s]).astype(F32)
        g_ref[...] = g
        d_ref[...], nm_ref[...], nv_ref[...] = _adamw_math(w_ref[...], g, m_ref[...], v_ref[...])

    blk = pl.BlockSpec((tr, C), lambda i: (i, 0))
    out = jax.ShapeDtypeStruct((R, C), F32)
    return pl.pallas_call(
        body, name=name, grid=(R // tr,),
        in_specs=[pl.BlockSpec((N_DEV, tr, C), lambda i: (0, i, 0)), blk, blk, blk, blk],
        out_specs=[blk] * 4, out_shape=[out] * 4,
        compiler_params=_cp("parallel"),
    )(parts, own, w, m, v)


def _adamw_small_call(name, g, w, m, v):
    def body(g_ref, w_ref, m_ref, v_ref, d_ref, nm_ref, nv_ref):
        d_ref[...], nm_ref[...], nv_ref[...] = _adamw_math(w_ref[...], g_ref[...], m_ref[...], v_ref[...])

    out = jax.ShapeDtypeStruct(w.shape, F32)
    return pl.pallas_call(body, name=name, out_shape=[out] * 3)(g, w, m, v)


def _position():
    return lax.axis_index("x"), lax.axis_index("y"), lax.axis_index("c")


def _peer(x, y, c, rel):
    fx, fy, fc = (rel >> 2) & 1, (rel >> 1) & 1, rel & 1
    px = 1 - x if fx else x
    py = 1 - y if fy else y
    pc = 1 - c if fc else c
    return (px, py, pc), 4 * px + 2 * py + pc


def _gather_weights_call(shards):
    n = len(shards)

    def body(*refs):
        ins, outs = refs[:n], refs[n:2 * n]
        send_sems, recv_sems, local_sems = refs[2 * n:]
        x, y, c = _position()
        me = 4 * x + 2 * y + c
        sib, sib_id = _peer(x, y, c, 1)
        chips = [_peer(x, y, c, rel) for rel in (4, 2, 6)]

        def copy(i, k, src, slot, to):
            return pltpu.make_async_remote_copy(
                src_ref=src, dst_ref=outs[i].at[slot], send_sem=send_sems.at[i, k], recv_sem=recv_sems.at[i, k],
                device_id=to, device_id_type=MESH)

        mine = [pltpu.make_async_copy(ins[i], outs[i].at[me], local_sems.at[i]) for i in range(n)]
        for cp in mine:
            cp.start()
        first = []
        for i in range(n):
            first.append(copy(i, 0, ins[i], me, sib))
            first += [copy(i, 1 + j, ins[i], me, dev) for j, (dev, _) in enumerate(chips)]
        for cp in first:
            cp.start()
        passed = []
        for i in range(n):
            for j, (dev, slot) in enumerate(chips):
                copy(i, 1 + j, ins[i], slot, dev).wait_recv()
                fwd = copy(i, 4 + j, outs[i].at[slot], slot, sib)
                fwd.start()
                passed.append(fwd)
        for i in range(n):
            copy(i, 0, ins[i], sib_id, sib).wait_recv()
            for j, (_, slot) in enumerate(chips):
                copy(i, 4 + j, ins[i], slot ^ 1, sib).wait_recv()
        for cp in first + passed:
            cp.wait_send()
        for cp in mine:
            cp.wait()

    hbm = pl.BlockSpec(memory_space=pl.ANY)
    return pl.pallas_call(
        body, name="gather_weights",
        in_specs=[hbm] * n, out_specs=[hbm] * n,
        out_shape=[jax.ShapeDtypeStruct((N_DEV,) + s.shape, s.dtype) for s in shards],
        scratch_shapes=[pltpu.SemaphoreType.DMA((n, 7)), pltpu.SemaphoreType.DMA((n, 7)), pltpu.SemaphoreType.DMA((n,))],
    )(*shards)


_HBM = pl.BlockSpec(memory_space=pltpu.HBM)
_SEM = pl.BlockSpec(memory_space=pltpu.SEMAPHORE)
_EFFECT = pltpu.SideEffectType.DATAFLOW_SIDE_EFFECTING


def _exchange_copies(scatter, incoming, src_refs, land_refs, send_sems, recv_sems):
    x, y, c = _position()
    me = 4 * x + 2 * y + c
    out = []
    for i, (src, land) in enumerate(zip(src_refs, land_refs)):
        for k in range(N_DEV - 1):
            dev, dev_id = _peer(x, y, c, k + 1)
            j = i * (N_DEV - 1) + k
            out.append(pltpu.make_async_remote_copy(
                src_ref=src.at[dev_id] if scatter else src, dst_ref=land.at[dev_id if incoming else me],
                send_sem=send_sems.at[j], recv_sem=recv_sems.at[j], device_id=dev, device_id_type=MESH))
    return out


def _exchange_start_call(name, arrays, scatter):
    n = len(arrays)
    lands = [lax.empty((N_DEV,) + (a.shape[1:] if scatter else a.shape), a.dtype) for a in arrays]

    def body(*refs):
        src_refs, land_refs = refs[:n], refs[n:2 * n]
        send_sems, recv_sems, token = refs[2 * n], refs[2 * n + 1], refs[-1]
        for copy in _exchange_copies(scatter, False, src_refs, land_refs, send_sems, recv_sems):
            copy.start()
        token[...] = jnp.zeros_like(token)

    sems = pltpu.SemaphoreType.DMA((n * (N_DEV - 1),))
    res = pl.pallas_call(
        body, name=name,
        out_shape=[sems, sems] + [pltpu.HBM(a.shape, a.dtype) for a in arrays] + [pltpu.HBM(l.shape, l.dtype) for l in lands]
        + [jax.ShapeDtypeStruct((8, 128), F32)],
        in_specs=[_HBM] * (2 * n), out_specs=[_SEM, _SEM] + [_HBM] * (2 * n) + [pl.BlockSpec(memory_space=pltpu.VMEM)],
        input_output_aliases={i: 2 + i for i in range(2 * n)},
        compiler_params=pltpu.CompilerParams(has_side_effects=_EFFECT),
    )(*[pltpu.with_memory_space_constraint(a, pltpu.HBM) for a in list(arrays) + lands])
    return res[0], res[1], res[2:2 + n], res[2 + n:2 + 2 * n], res[-1]


def _exchange_wait_call(name, started, scatter, after):
    send_sems, recv_sems, srcs, lands, _ = started
    n = len(srcs)

    def body(*refs):
        src_refs, land_refs = refs[:n], refs[n:2 * n]
        send_s, recv_s = refs[2 * n], refs[2 * n + 1]
        for copy in _exchange_copies(scatter, False, src_refs, land_refs, send_s, recv_s):
            copy.wait_send()
        for copy in _exchange_copies(scatter, True, src_refs, land_refs, send_s, recv_s):
            copy.wait_recv()

    res = pl.pallas_call(
        body, name=name,
        out_shape=[pltpu.HBM(a.shape, a.dtype) for a in list(srcs) + list(lands)],
        in_specs=[_HBM] * (2 * n) + [_SEM, _SEM, pl.BlockSpec(memory_space=pl.ANY)], out_specs=[_HBM] * (2 * n),
        input_output_aliases={i: i for i in range(2 * n)},
        compiler_params=pltpu.CompilerParams(has_side_effects=_EFFECT),
    )(*srcs, *lands, send_sems, recv_sems, after)
    return res[n:]


def _allsum_small_call(part, blocks):
    P, Q = part.shape[0], blocks.shape[1]

    def body(p_ref, b_ref, o_ref, ob_ref, all_ref, allb_ref, send_sems, recv_sems):
        x, y, c = _position()
        me = 4 * x + 2 * y + c
        peers = [_peer(x, y, c, rel) for rel in range(1, N_DEV)]
        all_ref[me] = p_ref[...]
        allb_ref[me] = b_ref[me]

        def copies(k, slot, dev, dev_id):
            sems = lambda i: dict(send_sem=send_sems.at[i], recv_sem=recv_sems.at[i], device_id=dev, device_id_type=MESH)
            return (pltpu.make_async_remote_copy(src_ref=p_ref, dst_ref=all_ref.at[slot], **sems(k)),
                    pltpu.make_async_remote_copy(src_ref=b_ref.at[dev_id], dst_ref=allb_ref.at[slot], **sems(7 + k)))

        sends = [cp for k, (dev, dev_id) in enumerate(peers) for cp in copies(k, me, dev, dev_id)]
        for cp in sends:
            cp.start()
        for k, (dev, dev_id) in enumerate(peers):
            for cp in copies(k, dev_id, dev, dev_id):
                cp.wait_recv()
        for cp in sends:
            cp.wait_send()
        tot, totb = all_ref[0], allb_ref[0]
        for s in range(1, N_DEV):
            tot, totb = tot + all_ref[s], totb + allb_ref[s]
        o_ref[...] = tot
        ob_ref[...] = totb

    vm = pl.BlockSpec(memory_space=pltpu.VMEM)
    return pl.pallas_call(
        body, name="allsum_small", in_specs=[vm, vm], out_specs=[vm, vm],
        out_shape=[jax.ShapeDtypeStruct((P, 128), F32), jax.ShapeDtypeStruct((Q, 128), F32)],
        scratch_shapes=[pltpu.VMEM((N_DEV, P, 128), F32), pltpu.VMEM((N_DEV, Q, 128), F32),
                        pltpu.SemaphoreType.DMA((14,)), pltpu.SemaphoreType.DMA((14,))],
    )(part, blocks)


def _pad_rows(v, rows):
    flat = v.reshape(-1)
    return jnp.pad(flat, (0, rows * 128 - flat.shape[0])).reshape(rows, 128)


def _local_step(x, tgt, w_in_i, conv_w, conv_b, ln_g, ln_b, qg, kg, sinks, g_mix, g_ffn, token0, rest_weights, grads_ready, B, S):
    row = lambda v: v.reshape(1, -1)
    qg2 = jnp.tile(qg, 2).reshape(1, 128)
    kg2 = jnp.tile(kg, 2).reshape(1, 128)
    heads = jnp.arange(1, N_Q_HEADS + 1, dtype=F32)
    slopes = jnp.exp2(-8.0 * heads / N_Q_HEADS)

    u, xn = _inproj_call(x, row(g_mix) + token0, w_in_i)
    hc, c = _conv_fwd_call(u, conv_w, row(conv_b), row(ln_g), row(ln_b), B, S)
    o = _attn_fwd_call(u, qg2, kg2, sinks, slopes, B, S)
    wc, wa, wm, w_ffn_in, wd = rest_weights(o)
    yc, ya, z, h, hn = _merge_call(c, o, u, x, wc, wa, wm, row(g_ffn))
    gu, act = _ffn_in_call(hn, w_ffn_in)
    sse, dout, dout_b = _ffn_out_call(act, wd, h, tgt)

    dgu = _ffn_down_bwd_call(dout_b, wd, gu)
    g_wd = _dw_call("dw_ffn_down", act, dout_b, tr=D_FF // 2)
    g_wffn_t = _dw_call("dw_ffn_in", dgu, hn, tr=2 * D_FF // 4)
    tok = grads_ready(dict(w_ffn_down=g_wd, w_ffn_in=g_wffn_t))
    dh, dh_b, g_gffn = _ffn_in_bwd_call(dgu, w_ffn_in, h, row(g_ffn) + tok, dout)
    dyc, dya, du_g, dcp, do = _merge_bwd_call(dh_b, wm, wc, wa, u, yc, ya)
    g_wm = _dw_call("dw_merge", z, dh_b, tr=D_MODEL)
    g_wc = _dw_call("dw_conv_out", c, dyc, tr=D_MODEL)
    g_wa = _dw_call("dw_attn_out", o, dya, tr=D_MODEL)
    tok = grads_ready(dict(w_merge_out=g_wm, w_conv_out=g_wc, w_attn_out=g_wa))
    du_conv, g_cw8, g_cb, g_lg, g_lb = _conv_bwd_call(dcp, hc, u, conv_w, row(ln_g) + tok, row(ln_b), B, S)
    du_q, du_kv, g_qg2, g_kg2, g_sk = _attn_bwd_call(u, do, qg2, kg2, sinks, slopes, B, S)
    g_win_t = jnp.concatenate([
        _dw_call("dw_in_conv", du_conv, xn, tr=D_MODEL),
        _dw_call("dw_in_q", du_q, xn, tr=D_MODEL),
        _dw_call("dw_in_kv", du_kv, xn, tr=2 * KV_W),
        _dw_call("dw_in_gates", du_g, xn, tr=D_MODEL)], axis=0)
    tok = grads_ready(dict(w_in=g_win_t))
    grad_x, g_gmix = _inproj_bwd_call(du_conv, du_g, du_q, du_kv, w_in_i, x, row(g_mix) + tok, dh)

    g_cw = g_cw8.reshape(CONV_WIDTH, 8, C_CONV).sum(axis=1)
    g_qg = g_qg2[0, :HEAD_DIM] + g_qg2[0, HEAD_DIM:]
    g_kg = g_kg2[0, :HEAD_DIM] + g_kg2[0, HEAD_DIM:]
    small = dict(norm_mix_g=g_gmix[0], conv_dw_b=g_cb[0], conv_ln_g=g_lg[0], conv_ln_b=g_lb[0], q_norm_g=g_qg,
                 k_norm_g=g_kg, sinks=g_sk[0, :N_Q_HEADS], norm_ffn_g=g_gffn[0], conv_dw_w=g_cw)
    return sse[0, 0], grad_x, small


def _to_internal_rows(wt):
    o1 = 2 * C_CONV
    o4 = o1 + Q_W + 2 * KV_W
    return jnp.concatenate([wt[:o1], wt[o4:], wt[o1:o4]], axis=0)


_COL_SHARDED = ("w_in", "w_ffn_in")

_SMALL = (("norm_mix_g", 8), ("conv_dw_b", 8), ("conv_ln_g", 8), ("conv_ln_b", 8), ("norm_ffn_g", 8),
          ("q_norm_g", 1), ("k_norm_g", 1), ("sinks", 1))
_SMALL_ROWS = 48


def _pack_small(d):
    rows = [_pad_rows(d[name], r) for name, r in _SMALL]
    used = sum(r for _, r in _SMALL)
    return jnp.concatenate(rows + [jnp.zeros((_SMALL_ROWS - used, 128), F32)], axis=0)


def _unpack_small(packed, like):
    out, r0 = {}, 0
    for name, r in _SMALL:
        n = like[name].size
        out[name] = packed[r0:r0 + r].reshape(-1)[:n].reshape(like[name].shape)
        r0 += r
    return out


def kernel(x, norm_mix_g, w_in, conv_dw_w, conv_dw_b, conv_ln_g, conv_ln_b, w_conv_out, q_norm_g, k_norm_g, sinks, w_attn_out, w_merge_out, norm_ffn_g, w_ffn_in, w_ffn_down, loss_target, m_norm_mix_g, m_w_in, m_conv_dw_w, m_conv_dw_b, m_conv_ln_g, m_conv_ln_b, m_w_conv_out, m_q_norm_g, m_k_norm_g, m_sinks, m_w_attn_out, m_w_merge_out, m_norm_ffn_g, m_w_ffn_in, m_w_ffn_down, v_norm_mix_g, v_w_in, v_conv_dw_w, v_conv_dw_b, v_conv_ln_g, v_conv_ln_b, v_w_conv_out, v_q_norm_g, v_k_norm_g, v_sinks, v_w_attn_out, v_w_merge_out, v_norm_ffn_g, v_w_ffn_in, v_w_ffn_down):
    names = ["norm_mix_g", "w_in", "conv_dw_w", "conv_dw_b", "conv_ln_g", "conv_ln_b", "w_conv_out", "q_norm_g",
             "k_norm_g", "sinks", "w_attn_out", "w_merge_out", "norm_ffn_g", "w_ffn_in", "w_ffn_down"]
    w = dict(zip(names, (norm_mix_g, w_in, conv_dw_w, conv_dw_b, conv_ln_g, conv_ln_b, w_conv_out, q_norm_g, k_norm_g,
                         sinks, w_attn_out, w_merge_out, norm_ffn_g, w_ffn_in, w_ffn_down)))
    m = dict(zip(names, (m_norm_mix_g, m_w_in, m_conv_dw_w, m_conv_dw_b, m_conv_ln_g, m_conv_ln_b, m_w_conv_out,
                         m_q_norm_g, m_k_norm_g, m_sinks, m_w_attn_out, m_w_merge_out, m_norm_ffn_g, m_w_ffn_in,
                         m_w_ffn_down)))
    v = dict(zip(names, (v_norm_mix_g, v_w_in, v_conv_dw_w, v_conv_dw_b, v_conv_ln_g, v_conv_ln_b, v_w_conv_out,
                         v_q_norm_g, v_k_norm_g, v_sinks, v_w_attn_out, v_w_merge_out, v_norm_ffn_g, v_w_ffn_in,
                         v_w_ffn_down)))
    B, S, _ = x.shape
    T = B * S
    me = 4 * lax.axis_index("x") + 2 * lax.axis_index("y") + lax.axis_index("c")

    as_rows = lambda k, t: t.T if k in _COL_SHARDED else t
    cw_shard = _pad_rows(conv_dw_w.reshape(CONV_WIDTH, 128), 32)
    g_win_t, g_cwt = _gather_weights_call([w_in.T.astype(BF16), cw_shard])
    conv_w_full = g_cwt.reshape(N_DEV, 32 * 128)[:, :CONV_WIDTH * 128].reshape(N_DEV, CONV_WIDTH, 128)
    conv_w_full = conv_w_full.transpose(1, 0, 2).reshape(CONV_WIDTH, C_CONV)
    w_in_i = _to_internal_rows(g_win_t.reshape(IN_COLS, D_MODEL))
    rest_names = ["w_conv_out", "w_attn_out", "w_merge_out", "w_ffn_in", "w_ffn_down"]
    rest_shards = [as_rows(k, w[k]).astype(BF16) for k in rest_names]
    rest_started = _exchange_start_call("gather_rest_start", rest_shards, scatter=False)

    def rest_weights(after):
        lands = _exchange_wait_call("gather_rest_wait", rest_started, False, after)
        mine = lax.broadcasted_iota(jnp.int32, (N_DEV, 1, 1), 0) == me
        full = [jnp.where(mine, s[None], l) for l, s in zip(lands, rest_shards)]
        return tuple(f.reshape(N_DEV * f.shape[1], D_MODEL) for f in full)

    exchanges = []

    def grads_ready(group):
        ks = list(group)
        blocks = [group[k].reshape(N_DEV, group[k].shape[0] // N_DEV, D_MODEL) for k in ks]
        started = _exchange_start_call("scatter_" + ks[0] + "_start", blocks, scatter=True)
        exchanges.append((ks, started, [lax.dynamic_index_in_dim(b, me, 0, keepdims=False) for b in blocks]))
        return started[-1][0:1, 0:1]

    sse, grad_x, small = _local_step(
        x.reshape(T, D_MODEL), loss_target.reshape(T, D_MODEL), w_in_i, conv_w_full, conv_dw_b, conv_ln_g, conv_ln_b,
        q_norm_g, k_norm_g, sinks, norm_mix_g, norm_ffn_g, rest_started[-1][0:1, 0:1], rest_weights, grads_ready, B, S)

    grad, delta, new_m, new_v = {}, {}, {}, {}
    packed = jnp.concatenate([_pack_small(small), jnp.full((8, 128), sse, F32)], axis=0)
    cw_blocks = small["conv_dw_w"].reshape(CONV_WIDTH, N_DEV, 128).transpose(1, 0, 2)
    tot, cw_tot = _allsum_small_call(packed, jnp.pad(cw_blocks, ((0, 0), (0, 32 - CONV_WIDTH), (0, 0))))
    loss = 0.5 / D_MODEL * tot[_SMALL_ROWS, 0]
    g_small = _unpack_small(tot[:_SMALL_ROWS], w)
    d_s, m_s, v_s = _adamw_small_call("adamw_small", tot[:_SMALL_ROWS], _pack_small(w), _pack_small(m), _pack_small(v))
    d_small, m_small, v_small = _unpack_small(d_s, w), _unpack_small(m_s, w), _unpack_small(v_s, w)
    g_cw = cw_tot[:CONV_WIDTH]
    cw2 = lambda t: t.reshape(CONV_WIDTH, 128)
    d_cw, m_cw, v_cw = _adamw_small_call("adamw_conv_w", g_cw, cw2(conv_dw_w), cw2(m_conv_dw_w), cw2(v_conv_dw_w))
    cw3 = lambda t: t.reshape(CONV_WIDTH, 1, 128)
    for k, _ in _SMALL:
        grad[k], delta[k], new_m[k], new_v[k] = g_small[k], d_small[k], m_small[k], v_small[k]
    grad["conv_dw_w"], delta["conv_dw_w"], new_m["conv_dw_w"], new_v["conv_dw_w"] = cw3(g_cw), cw3(d_cw), cw3(m_cw), cw3(v_cw)

    after = d_cw
    for ks, started, owns in exchanges:
        lands = _exchange_wait_call("scatter_" + ks[0] + "_wait", started, True, after)
        for k, land, own in zip(ks, lands, owns):
            tr = {672: 336, 704: 352, 352: 176, 128: 128}[own.shape[0]]
            outs = _adamw_sum_call("adamw_" + k, land, own, as_rows(k, w[k]), as_rows(k, m[k]), as_rows(k, v[k]), tr)
            grad[k], delta[k], new_m[k], new_v[k] = (as_rows(k, t) for t in outs)
            after = outs[3]

    return (loss, grad_x.reshape(B, S, D_MODEL), *[grad[k] for k in names], *[delta[k] for k in names],
            *[new_m[k] for k in names], *[new_v[k] for k in names])
```

```python
import functools
import math

import jax
import jax.numpy as jnp
from jax import lax
from jax.experimental import pallas as pl
from jax.experimental.pallas import tpu as pltpu

F32 = jnp.float32
BF16 = jnp.bfloat16
MESH = pl.DeviceIdType.MESH

N_DEV = 8
D_MODEL = 1024
C_CONV = 1024
CONV_WIDTH = 31
HEAD_DIM = 64
N_Q_HEADS = 16
N_KV_HEADS = 2
GROUP = N_Q_HEADS // N_KV_HEADS
WINDOW = 128
BLOCK = 128
D_FF = 2816
EPS = 1e-6
NEG = -1e30
ATTN_SCALE = 1.0 / math.sqrt(HEAD_DIM)
Q_W = N_Q_HEADS * HEAD_DIM
KV_W = N_KV_HEADS * HEAD_DIM
IN_COLS = 2 * C_CONV + Q_W + 2 * KV_W + 2 * D_MODEL

ADAM_LR = 0.001
ADAM_B1 = 0.9
ADAM_B2 = 0.999
ADAM_EPS = 1e-08
ADAM_WD = 0.01
ADAM_STEP = 10

COL_A, COL_GATE, COL_GC, COL_GA, COL_Q, COL_K, COL_V = 0, 1024, 2048, 3072, 4096, 5120, 5248

HALO = 32
VMEM_LIMIT = 56 * 1024 * 1024


def _cp(*sem):
    return pltpu.CompilerParams(dimension_semantics=sem, vmem_limit_bytes=VMEM_LIMIT)


def _sigmoid(x):
    return jax.nn.sigmoid(x)


def _dot(a, b):
    return jnp.dot(a, b, preferred_element_type=F32)


def _dot_nt(a, b):
    return lax.dot_general(a, b, (((1,), (1,)), ((), ())), preferred_element_type=F32)


def _dot_tn(a, b):
    return lax.dot_general(a, b, (((0,), (0,)), ((), ())), preferred_element_type=F32)


def _full(shape):
    return pl.BlockSpec(shape, lambda *_: (0,) * len(shape))


def _rows(tm, ncols, colblk=0):
    return pl.BlockSpec((tm, ncols), lambda i: (i, colblk))


def _inproj_call(x, g, w, tm=512):
    T = x.shape[0]
    N = w.shape[0]

    def body(x_ref, g_ref, w_ref, u_ref, xn_ref):
        xv = x_ref[...]
        r = lax.rsqrt(jnp.mean(xv * xv, axis=-1, keepdims=True) + EPS)
        xn = (xv * r * g_ref[...]).astype(BF16)
        xn_ref[...] = xn
        u_ref[...] = _dot_nt(xn, w_ref[...]).astype(BF16)

    return pl.pallas_call(
        body, name="inproj", grid=(T // tm,),
        in_specs=[_rows(tm, D_MODEL), _full((1, D_MODEL)), _full(w.shape)],
        out_specs=[_rows(tm, N), _rows(tm, D_MODEL)],
        out_shape=[jax.ShapeDtypeStruct((T, N), BF16), jax.ShapeDtypeStruct((T, D_MODEL), BF16)],
        compiler_params=_cp("parallel"),
    )(x, g, w)


def _fill_shifted(src_ref, sh_ref):
    n = src_ref.shape[0] - 8
    for s in range(1, 8):
        sh_ref[s - 1, 0:n, :] = src_ref[s:s + n, :]


def _window(src_ref, sh_ref, off, cols):
    s = off % 8
    if s == 0:
        return src_ref[off:off + 128, cols]
    return sh_ref[s - 1, off - s:off - s + 128, cols]


def _conv_taps(w_ref, src_ref, sh_ref, base, rows, cb, reverse):
    cols = pl.ds(pl.multiple_of(cb * 128, 128), 128)
    outs = []
    for rb in range(rows // 128):
        acc = jnp.zeros((128, 128), F32)
        for j in range(CONV_WIDTH):
            off = base + (CONV_WIDTH - 1 - j if reverse else j) + rb * 128
            acc = acc + w_ref[j:j + 1, cols] * _window(src_ref, sh_ref, off, cols)
        outs.append(acc)
    return outs


def _conv_fwd_call(u, conv_w, conv_b, ln_g, ln_b, B, S, ts=512):
    T = B * S
    nS = S // ts
    per32 = ts // HALO

    def body(a_ref, gate_ref, ap_ref, gp_ref, w_ref, b_ref, g_ref, bb_ref, hc_ref, c_ref, hext, hsh, acc_ref):
        s = pl.program_id(1)
        prev = ap_ref[...].astype(F32) * _sigmoid(gp_ref[...].astype(F32))
        hext[0:HALO, :] = jnp.where(s > 0, prev, 0.0)
        hext[HALO:, :] = a_ref[...].astype(F32) * _sigmoid(gate_ref[...].astype(F32))
        _fill_shifted(hext, hsh)

        def colblock(cb, carry):
            cols = pl.ds(pl.multiple_of(cb * 128, 128), 128)
            outs = _conv_taps(w_ref, hext, hsh, HALO - (CONV_WIDTH - 1), ts, cb, reverse=False)
            for rb, acc in enumerate(outs):
                acc_ref[rb * 128:(rb + 1) * 128, cols] = acc
            return carry

        lax.fori_loop(0, C_CONV // 128, colblock, 0)
        hc = acc_ref[...] + b_ref[...]
        hc_ref[...] = hc
        mu = jnp.mean(hc, axis=-1, keepdims=True)
        xc = hc - mu
        var = jnp.mean(xc * xc, axis=-1, keepdims=True)
        y = xc * lax.rsqrt(var + EPS) * g_ref[...] + bb_ref[...]
        c_ref[...] = (y * _sigmoid(y)).astype(BF16)

    cur = lambda cb: pl.BlockSpec((ts, C_CONV), lambda b, s: (b * nS + s, cb))
    prv = lambda cb: pl.BlockSpec((HALO, C_CONV), lambda b, s: (jnp.maximum(b * (S // HALO) + s * per32 - 1, 0), cb))
    vec = _full((1, C_CONV))
    return pl.pallas_call(
        body, name="conv_fwd", grid=(B, nS),
        in_specs=[cur(0), cur(1), prv(0), prv(1), _full((CONV_WIDTH, C_CONV)), vec, vec, vec],
        out_specs=[pl.BlockSpec((ts, C_CONV), lambda b, s: (b * nS + s, 0))] * 2,
        out_shape=[jax.ShapeDtypeStruct((T, C_CONV), F32), jax.ShapeDtypeStruct((T, C_CONV), BF16)],
        scratch_shapes=[pltpu.VMEM((HALO + ts, C_CONV), F32), pltpu.VMEM((7, HALO + ts, C_CONV), F32),
                        pltpu.VMEM((ts, C_CONV), F32)],
        compiler_params=_cp("parallel", "arbitrary"),
    )(u, u, u, u, conv_w, conv_b, ln_g, ln_b)


def _lane_lo():
    return lax.broadcasted_iota(jnp.int32, (1, 128), 1) < HEAD_DIM


def _half_sums(t, lo):
    s_lo = jnp.sum(jnp.where(lo, t, 0.0), axis=-1, keepdims=True)
    s_hi = jnp.sum(jnp.where(lo, 0.0, t), axis=-1, keepdims=True)
    return jnp.where(lo, s_lo, s_hi)


def _head_rstd(t, lo):
    return lax.rsqrt(_half_sums(t * t, lo) * (1.0 / HEAD_DIM) + EPS)


def _fill_bias(btab, kvh, slopes_ref):
    sj = lax.broadcasted_iota(jnp.int32, (2 * BLOCK, BLOCK), 0)
    qi = lax.broadcasted_iota(jnp.int32, (2 * BLOCK, BLOCK), 1)
    dist = qi + BLOCK - sj
    valid = (dist >= 0) & (dist < WINDOW)
    distf = dist.astype(F32)
    for g in range(GROUP):
        bias = jnp.where(valid, -slopes_ref[kvh * GROUP + g] * distf, NEG)
        btab[1, g] = bias
        btab[0, g] = jnp.where(sj >= BLOCK, bias, NEG)


def _softmax_cols(st_ref, btab, p, var, g, hf, sk):
    sh = st_ref[2 * BLOCK * hf:2 * BLOCK * (hf + 1), 128 * p:128 * (p + 1)] + btab[var, g]
    m = jnp.maximum(jnp.max(sh, axis=0, keepdims=True), sk)
    e = jnp.exp(sh - m)
    esk = jnp.exp(sk - m)
    rz = 1.0 / (jnp.sum(e, axis=0, keepdims=True) + esk)
    return e * rz, esk, rz


def _stage_kv(kvh, k_ref, v_ref, kg_ref, lo, kA, kB, vA, vB):
    kv = k_ref[...].astype(F32)
    rk = _head_rstd(kv, lo)
    kn = kv * rk * kg_ref[...]
    mine = lax.broadcasted_iota(jnp.int32, (1, 128), 1) // HEAD_DIM == kvh
    ksel = jnp.where(mine, kn, 0.0)
    vsel = jnp.where(mine, v_ref[...].astype(F32), 0.0)
    krol = pltpu.roll(ksel, HEAD_DIM, 1)
    vrol = pltpu.roll(vsel, HEAD_DIM, 1)
    first = kvh == 0
    zeros = jnp.zeros((BLOCK, 128), BF16)
    for ref, val in ((kA, jnp.where(first, ksel, krol)), (kB, jnp.where(first, krol, ksel)),
                     (vA, jnp.where(first, vsel, vrol)), (vB, jnp.where(first, vrol, vsel))):
        ref[0:BLOCK, :] = zeros
        ref[BLOCK:, :] = val.astype(BF16)


def _attn_fwd_call(u, qg2, kg2, sinks, slopes, B, S):
    T = B * S
    nb = S // BLOCK
    half = Q_W // 2
    npair = GROUP // 2

    def body(sinks_ref, slopes_ref, q_ref, k_ref, v_ref, qg_ref, kg_ref, o_ref, kA, kB, vA, vB, btab, qn_buf, s_buf, p_buf):
        kvh = pl.program_id(1)
        lo = _lane_lo()
        _stage_kv(kvh, k_ref, v_ref, kg_ref, lo, kA, kB, vA, vB)
        _fill_bias(btab, kvh, slopes_ref)
        qgs = qg_ref[...] * ATTN_SCALE

        def block(n, slot):
            qn_scr, s_scr, p_scr = qn_buf.at[slot], s_buf.at[slot], p_buf.at[slot]
            r0 = pl.multiple_of(n * BLOCK, BLOCK)
            win = pl.ds(r0, 2 * BLOCK)
            k2t = jnp.concatenate([kA[win, :], kB[win, :]], axis=0)
            v2 = jnp.concatenate([vA[win, :], vB[win, :]], axis=0)
            var = jnp.minimum(n, 1)
            for p in range(npair):
                pc = slice(128 * p, 128 * (p + 1))
                qp = q_ref[pl.ds(r0, BLOCK), pc].astype(F32)
                qn_scr[pc, :] = (qp * _head_rstd(qp, lo) * qgs).astype(BF16)
            s_scr[...] = _dot_nt(k2t, qn_scr[...])
            for p in range(npair):
                for hf in range(2):
                    g = 2 * p + hf
                    pn, _, _ = _softmax_cols(s_scr, btab, p, var, g, hf, sinks_ref[kvh * GROUP + g])
                    p_scr[2 * BLOCK * hf:2 * BLOCK * (hf + 1), 128 * p:128 * (p + 1)] = pn.astype(BF16)
            o_all = _dot_tn(v2, p_scr[...]).T
            for p in range(npair):
                pc = slice(128 * p, 128 * (p + 1))
                o_ref[pl.ds(r0, BLOCK), pc] = o_all[pc, :].astype(BF16)

        def two_blocks(i, carry):
            block(2 * i, 0)
            block(2 * i + 1, 1)
            return carry

        lax.fori_loop(0, nb // 2, two_blocks, 0)

    smem = pl.BlockSpec(memory_space=pltpu.SMEM)
    kvbuf = pltpu.VMEM((S + BLOCK, 128), BF16)
    pair_scores = (2, npair * BLOCK, 4 * BLOCK)
    return pl.pallas_call(
        body, name="attn_fwd", grid=(B, N_KV_HEADS),
        in_specs=[smem, smem,
                  pl.BlockSpec((S, half), lambda b, h: (b, COL_Q // half + h)),
                  pl.BlockSpec((S, 128), lambda b, h: (b, COL_K // 128)),
                  pl.BlockSpec((S, 128), lambda b, h: (b, COL_V // 128)),
                  _full((1, 128)), _full((1, 128))],
        out_specs=pl.BlockSpec((S, half), lambda b, h: (b, h)),
        out_shape=jax.ShapeDtypeStruct((T, Q_W), BF16),
        scratch_shapes=[kvbuf, kvbuf, kvbuf, kvbuf, pltpu.VMEM((2, GROUP, 2 * BLOCK, BLOCK), F32),
                        pltpu.VMEM((2, npair * BLOCK, 128), BF16), pltpu.VMEM(pair_scores, F32), pltpu.VMEM(pair_scores, BF16)],
        compiler_params=_cp("parallel", "arbitrary"),
    )(sinks, slopes, u, u, u, qg2, kg2)


def _merge_call(c, o, u, x, wc, wa, wm, g_ffn, tm=512):
    T = x.shape[0]

    def body(c_ref, o_ref, gc_ref, ga_ref, x_ref, wc_ref, wa_ref, wm_ref, g_ref, yc_ref, ya_ref, z_ref, h_ref, hn_ref):
        yc = _dot(c_ref[...], wc_ref[...])
        ya = _dot(o_ref[...], wa_ref[...])
        yc_ref[...] = yc.astype(BF16)
        ya_ref[...] = ya.astype(BF16)
        z = (_sigmoid(gc_ref[...].astype(F32)) * yc + _sigmoid(ga_ref[...].astype(F32)) * ya).astype(BF16)
        z_ref[...] = z
        h = x_ref[...] + _dot(z, wm_ref[...])
        h_ref[...] = h
        r = lax.rsqrt(jnp.mean(h * h, axis=-1, keepdims=True) + EPS)
        hn_ref[...] = (h * r * g_ref[...]).astype(BF16)

    sq = _full((D_MODEL, D_MODEL))
    row = _rows(tm, D_MODEL)
    f32 = jax.ShapeDtypeStruct((T, D_MODEL), F32)
    b16 = jax.ShapeDtypeStruct((T, D_MODEL), BF16)
    return pl.pallas_call(
        body, name="merge_fwd", grid=(T // tm,),
        in_specs=[row, row, _rows(tm, D_MODEL, COL_GC // D_MODEL), _rows(tm, D_MODEL, COL_GA // D_MODEL), row,
                  sq, sq, sq, _full((1, D_MODEL))],
        out_specs=[row] * 5,
        out_shape=[b16, b16, b16, f32, b16],
        compiler_params=_cp("parallel"),
    )(c, o, u, u, x, wc, wa, wm, g_ffn)


def _ffn_in_call(hn, w, tm=512):
    T = hn.shape[0]

    def body(hn_ref, w_ref, fac_ref, act_ref):
        gu = _dot_nt(hn_ref[...], w_ref[...])
        g = gu[:, :D_FF]
        up = gu[:, D_FF:]
        sg = _sigmoid(g)
        silu = g * sg
        act_ref[...] = (silu * up).astype(BF16)
        fac_ref[:, :D_FF] = (up * (sg * (1.0 + g * (1.0 - sg)))).astype(BF16)
        fac_ref[:, D_FF:] = silu.astype(BF16)

    return pl.pallas_call(
        body, name="ffn_in", grid=(T // tm,),
        in_specs=[_rows(tm, D_MODEL), _full(w.shape)],
        out_specs=[_rows(tm, 2 * D_FF), _rows(tm, D_FF)],
        out_shape=[jax.ShapeDtypeStruct((T, 2 * D_FF), BF16), jax.ShapeDtypeStruct((T, D_FF), BF16)],
        compiler_params=_cp("parallel"),
    )(hn, w)


def _ffn_out_call(act, wd, h, tgt, tm=512):
    T = h.shape[0]

    def body(act_ref, wd_ref, h_ref, t_ref, sse_ref, d_ref, db_ref):
        e = h_ref[...] + _dot(act_ref[...], wd_ref[...]) - t_ref[...]

        @pl.when(pl.program_id(0) == 0)
        def _():
            sse_ref[...] = jnp.zeros_like(sse_ref)

        sse_ref[...] += jnp.sum(e * e)
        d = e * (1.0 / D_MODEL)
        d_ref[...] = d
        db_ref[...] = d.astype(BF16)

    row = _rows(tm, D_MODEL)
    return pl.pallas_call(
        body, name="ffn_out", grid=(T // tm,),
        in_specs=[_rows(tm, D_FF), _full(wd.shape), row, row],
        out_specs=[_full((8, 128)), row, row],
        out_shape=[jax.ShapeDtypeStruct((8, 128), F32), jax.ShapeDtypeStruct((T, D_MODEL), F32),
                   jax.ShapeDtypeStruct((T, D_MODEL), BF16)],
        compiler_params=_cp("arbitrary"),
    )(act, wd, h, tgt)


def _dw_call(name, a, b, tr, tk=2048):
    T, K = a.shape
    N = b.shape[1]
    tk = min(tk, T)
    nk = T // tk

    def body(a_ref, b_ref, o_ref, acc_ref):
        k = pl.program_id(1)

        @pl.when(k == 0)
        def _():
            acc_ref[...] = jnp.zeros_like(acc_ref)

        acc_ref[...] += _dot_tn(a_ref[...], b_ref[...])

        @pl.when(k == nk - 1)
        def _():
            o_ref[...] = acc_ref[...].astype(BF16)

    return pl.pallas_call(
        body, name=name, grid=(K // tr, nk),
        in_specs=[pl.BlockSpec((tk, tr), lambda j, k: (k, j)), pl.BlockSpec((tk, N), lambda j, k: (k, 0))],
        out_specs=pl.BlockSpec((tr, N), lambda j, k: (j, 0)),
        out_shape=jax.ShapeDtypeStruct((K, N), BF16),
        scratch_shapes=[pltpu.VMEM((tr, N), F32)],
        compiler_params=_cp("parallel", "arbitrary"),
    )(a, b)


def _rms_bwd(dy, xv, g):
    r = lax.rsqrt(jnp.mean(xv * xv, axis=-1, keepdims=True) + EPS)
    t = dy * g
    dx = r * t - xv * (r * r * r * jnp.mean(t * xv, axis=-1, keepdims=True))
    return dx, dy * xv * r


def _ffn_down_bwd_call(dout_b, wd, gu, tm=512):
    T = dout_b.shape[0]

    def body(d_ref, wd_ref, fg_ref, fu_ref, o_ref):
        dact = _dot_nt(d_ref[...], wd_ref[...])
        o_ref[:, :D_FF] = (dact * fg_ref[...].astype(F32)).astype(BF16)
        o_ref[:, D_FF:] = (dact * fu_ref[...].astype(F32)).astype(BF16)

    return pl.pallas_call(
        body, name="ffn_down_bwd", grid=(T // tm,),
        in_specs=[_rows(tm, D_MODEL), _full(wd.shape), _rows(tm, D_FF, 0), _rows(tm, D_FF, 1)],
        out_specs=_rows(tm, 2 * D_FF),
        out_shape=jax.ShapeDtypeStruct((T, 2 * D_FF), BF16),
        compiler_params=_cp("parallel"),
    )(dout_b, wd, gu, gu)


def _ffn_in_bwd_call(dgu, w, h, g_ffn, dout, tm=512):
    T = h.shape[0]

    def body(dgu_ref, w_ref, h_ref, g_ref, d_ref, dh_ref, dhb_ref, dg_ref):
        dhn = _dot(dgu_ref[...], w_ref[...])
        dx, dgt = _rms_bwd(dhn, h_ref[...], g_ref[...])
        dh = d_ref[...] + dx
        dh_ref[...] = dh
        dhb_ref[...] = dh.astype(BF16)

        @pl.when(pl.program_id(0) == 0)
        def _():
            dg_ref[...] = jnp.zeros_like(dg_ref)

        dg_ref[...] += jnp.sum(dgt, axis=0, keepdims=True)

    row = _rows(tm, D_MODEL)
    return pl.pallas_call(
        body, name="ffn_in_bwd", grid=(T // tm,),
        in_specs=[_rows(tm, 2 * D_FF), _full(w.shape), row, _full((1, D_MODEL)), row],
        out_specs=[row, row, _full((1, D_MODEL))],
        out_shape=[jax.ShapeDtypeStruct((T, D_MODEL), F32), jax.ShapeDtypeStruct((T, D_MODEL), BF16),
                   jax.ShapeDtypeStruct((1, D_MODEL), F32)],
        compiler_params=_cp("arbitrary"),
    )(dgu, w, h, g_ffn, dout)


def _merge_bwd_call(dh_b, wm, wc, wa, u, yc, ya, tm=512):
    T = dh_b.shape[0]

    def body(dh_ref, wm_ref, wc_ref, wa_ref, gc_ref, ga_ref, yc_ref, ya_ref, dyc_ref, dya_ref, dug_ref, dcp_ref, do_ref):
        dz = _dot_nt(dh_ref[...], wm_ref[...])
        sgc = _sigmoid(gc_ref[...].astype(F32))
        sga = _sigmoid(ga_ref[...].astype(F32))
        dyc = (dz * sgc).astype(BF16)
        dya = (dz * sga).astype(BF16)
        dyc_ref[...] = dyc
        dya_ref[...] = dya
        dug_ref[:, :D_MODEL] = (dz * yc_ref[...].astype(F32) * (sgc * (1.0 - sgc))).astype(BF16)
        dug_ref[:, D_MODEL:] = (dz * ya_ref[...].astype(F32) * (sga * (1.0 - sga))).astype(BF16)
        dcp_ref[...] = _dot_nt(dyc, wc_ref[...])
        do_ref[...] = _dot_nt(dya, wa_ref[...]).astype(BF16)

    sq = _full((D_MODEL, D_MODEL))
    row = _rows(tm, D_MODEL)
    b16 = jax.ShapeDtypeStruct((T, D_MODEL), BF16)
    return pl.pallas_call(
        body, name="merge_bwd", grid=(T // tm,),
        in_specs=[row, sq, sq, sq, _rows(tm, D_MODEL, COL_GC // D_MODEL), _rows(tm, D_MODEL, COL_GA // D_MODEL), row, row],
        out_specs=[row, row, _rows(tm, 2 * D_MODEL), row, row],
        out_shape=[b16, b16, jax.ShapeDtypeStruct((T, 2 * D_MODEL), BF16), jax.ShapeDtypeStruct((T, D_MODEL), F32), b16],
        compiler_params=_cp("parallel"),
    )(dh_b, wm, wc, wa, u, u, yc, ya)


def _conv_bwd_call(dcp, hc, u, conv_w, ln_g, ln_b, B, S, ts=256):
    T = B * S
    nS = S // ts
    per32 = ts // HALO
    last32 = T // HALO - 1

    def ln_bwd(dcp_v, hc_v, g, bb):
        mu = jnp.mean(hc_v, axis=-1, keepdims=True)
        xc = hc_v - mu
        rstd = lax.rsqrt(jnp.mean(xc * xc, axis=-1, keepdims=True) + EPS)
        xhat = xc * rstd
        y = xhat * g + bb
        sy = _sigmoid(y)
        dy = dcp_v * (sy * (1.0 + y * (1.0 - sy)))
        dxh = dy * g
        dhc = rstd * (dxh - jnp.mean(dxh, axis=-1, keepdims=True) - xhat * jnp.mean(dxh * xhat, axis=-1, keepdims=True))
        return dhc, dy * xhat, dy

    def body(dcp_ref, dcpn_ref, hc_ref, hcn_ref, a_ref, gate_ref, w_ref, g_ref, bb_ref,
             du_ref, dw_ref, db_ref, dlg_ref, dlb_ref, dhext, glu_scr, dhsh, acc_ref):
        b = pl.program_id(0)
        s = pl.program_id(1)

        @pl.when((b == 0) & (s == 0))
        def _():
            dw_ref[...] = jnp.zeros_like(dw_ref)
            db_ref[...] = jnp.zeros_like(db_ref)
            dlg_ref[...] = jnp.zeros_like(dlg_ref)
            dlb_ref[...] = jnp.zeros_like(dlb_ref)

        dhc, dgt, dbt = ln_bwd(dcp_ref[...], hc_ref[...], g_ref[...], bb_ref[...])
        dhn, _, _ = ln_bwd(dcpn_ref[...], hcn_ref[...], g_ref[...], bb_ref[...])
        dhext[0:ts, :] = dhc
        dhext[ts:, :] = jnp.where(s < nS - 1, dhn, 0.0)
        db_ref[...] += jnp.sum(dhc, axis=0, keepdims=True)
        dlg_ref[...] += jnp.sum(dgt, axis=0, keepdims=True)
        dlb_ref[...] += jnp.sum(dbt, axis=0, keepdims=True)

        sg = _sigmoid(gate_ref[...].astype(F32))
        av = a_ref[...].astype(F32)
        glu_scr[...] = av * sg
        _fill_shifted(dhext, dhsh)

        def colblock(cb, carry):
            cols = pl.ds(pl.multiple_of(cb * 128, 128), 128)
            outs = _conv_taps(w_ref, dhext, dhsh, 0, ts, cb, reverse=True)
            for rb, acc in enumerate(outs):
                acc_ref[rb * 128:(rb + 1) * 128, cols] = acc
            taps = [jnp.zeros((8, 128), F32) for _ in range(CONV_WIDTH)]
            for r0 in range(0, ts, 128):
                glu = [glu_scr[r0 + 8 * i:r0 + 8 * (i + 1), cols] for i in range(16)]
                for s8 in range(8):
                    users = {}
                    for j in range(CONV_WIDTH):
                        if (CONV_WIDTH - 1 - j) % 8 == s8:
                            for i in range(16):
                                users.setdefault(CONV_WIDTH - 1 - j + r0 + 8 * i - s8, []).append((j, i))
                    for row, pairs in sorted(users.items()):
                        dhc8 = dhext[row:row + 8, cols] if s8 == 0 else dhsh[s8 - 1, row:row + 8, cols]
                        for j, i in pairs:
                            taps[j] = taps[j] + glu[i] * dhc8
            for j in range(CONV_WIDTH):
                dw_ref[8 * j:8 * (j + 1), cols] += taps[j]
            return carry

        lax.fori_loop(0, C_CONV // 128, colblock, 0)
        dglu = acc_ref[...]
        du_ref[:, :C_CONV] = (dglu * sg).astype(BF16)
        du_ref[:, C_CONV:] = (dglu * av * (sg * (1.0 - sg))).astype(BF16)

    cur = lambda cb: pl.BlockSpec((ts, C_CONV), lambda b, s: (b * nS + s, cb))
    nxt = pl.BlockSpec((HALO, C_CONV), lambda b, s: (jnp.minimum(b * (S // HALO) + (s + 1) * per32, last32), 0))
    vec = _full((1, C_CONV))
    return pl.pallas_call(
        body, name="conv_bwd", grid=(B, nS),
        in_specs=[cur(0), nxt, cur(0), nxt, cur(0), cur(1), _full((CONV_WIDTH, C_CONV)), vec, vec],
        out_specs=[pl.BlockSpec((ts, 2 * C_CONV), lambda b, s: (b * nS + s, 0)), _full((8 * CONV_WIDTH, C_CONV)), vec, vec, vec],
        out_shape=[jax.ShapeDtypeStruct((T, 2 * C_CONV), BF16), jax.ShapeDtypeStruct((8 * CONV_WIDTH, C_CONV), F32),
                   jax.ShapeDtypeStruct((1, C_CONV), F32), jax.ShapeDtypeStruct((1, C_CONV), F32),
                   jax.ShapeDtypeStruct((1, C_CONV), F32)],
        scratch_shapes=[pltpu.VMEM((ts + HALO, C_CONV), F32), pltpu.VMEM((ts, C_CONV), F32),
                        pltpu.VMEM((7, ts + HALO, C_CONV), F32), pltpu.VMEM((ts, C_CONV), F32)],
        compiler_params=_cp("arbitrary", "arbitrary"),
    )(dcp, dcp, hc, hc, u, u, conv_w, ln_g, ln_b)


def _attn_bwd_call(u, do, qg2, kg2, sinks, slopes, B, S):
    T = B * S
    nb = S // BLOCK
    half = Q_W // 2
    npair = GROUP // 2

    def body(sinks_ref, slopes_ref, q_ref, k_ref, v_ref, do_ref, qg_ref, kg_ref,
             dq_ref, dkv_ref, dqg_ref, dkg_ref, dsk_ref, kA, kB, vA, vB, dkn_acc, dv_acc, dkv_keep,
             btab, s_buf, dp_buf, p_buf, ds_buf, qn_buf, do_buf, rq_buf):
        b = pl.program_id(0)
        kvh = pl.program_id(1)
        lo = _lane_lo()
        lane = lax.broadcasted_iota(jnp.int32, (1, 128), 1)
        first = kvh == 0

        @pl.when((b == 0) & first)
        def _():
            dqg_ref[...] = jnp.zeros_like(dqg_ref)
            dkg_ref[...] = jnp.zeros_like(dkg_ref)
            dsk_ref[...] = jnp.zeros_like(dsk_ref)

        _stage_kv(kvh, k_ref, v_ref, kg_ref, lo, kA, kB, vA, vB)
        dkn_acc[...] = jnp.zeros_like(dkn_acc)
        dv_acc[...] = jnp.zeros_like(dv_acc)
        _fill_bias(btab, kvh, slopes_ref)
        qgs = qg_ref[...] * ATTN_SCALE

        def fold(acc):
            canon = jnp.where(lo, acc[0:2 * BLOCK] + pltpu.roll(acc[2 * BLOCK:], HEAD_DIM, 1), 0.0)
            return jnp.where(first, canon, pltpu.roll(canon, HEAD_DIM, 1))

        def block(n, slot):
            s_scr, dp_scr, p_scr, ds_scr = s_buf.at[slot], dp_buf.at[slot], p_buf.at[slot], ds_buf.at[slot]
            qn_scr, do_scr, rq_scr = qn_buf.at[slot], do_buf.at[slot], rq_buf.at[slot]
            r0 = pl.multiple_of(n * BLOCK, BLOCK)
            win = pl.ds(r0, 2 * BLOCK)
            rows = pl.ds(r0, BLOCK)
            k2t = jnp.concatenate([kA[win, :], kB[win, :]], axis=0)
            v2 = jnp.concatenate([vA[win, :], vB[win, :]], axis=0)
            var = jnp.minimum(n, 1)
            for p in range(npair):
                pc = slice(128 * p, 128 * (p + 1))
                qp = q_ref[rows, pc].astype(F32)
                rq = _head_rstd(qp, lo)
                rq_scr[pc, :] = rq
                qn_scr[pc, :] = (qp * rq * qgs).astype(BF16)
                do_scr[pc, :] = do_ref[rows, pc]
            s_scr[...] = _dot_nt(k2t, qn_scr[...])
            dp_scr[...] = _dot_nt(v2, do_scr[...])
            for p in range(npair):
                for hf in range(2):
                    g = 2 * p + hf
                    keys, qs = slice(2 * BLOCK * hf, 2 * BLOCK * (hf + 1)), slice(128 * p, 128 * (p + 1))
                    pn, esk, rz = _softmax_cols(s_scr, btab, p, var, g, hf, sinks_ref[kvh * GROUP + g])
                    dph = dp_scr[keys, qs]
                    delta = jnp.sum(pn * dph, axis=0, keepdims=True)
                    ds_scr[keys, qs] = (pn * (dph - delta)).astype(BF16)
                    p_scr[keys, qs] = pn.astype(BF16)
                    dsk_ref[...] += jnp.where(lane == kvh * GROUP + g, -jnp.sum(esk * rz * delta), 0.0)
            dqs_all = _dot_tn(k2t, ds_scr[...]).T
            for p in range(npair):
                pc = slice(128 * p, 128 * (p + 1))
                qp = q_ref[rows, pc].astype(F32)
                rq = rq_scr[pc, :]
                dqs = dqs_all[pc, :]
                t = dqs * qgs
                dq = rq * t - qp * (rq * rq * rq * (_half_sums(t * qp, lo) * (1.0 / HEAD_DIM)))
                dq_ref[rows, pc] = dq.astype(BF16)
                dqg_ref[...] += jnp.sum(dqs * qp * rq, axis=0, keepdims=True) * ATTN_SCALE
            dkn_acc[win, :] += fold(_dot(ds_scr[...], qn_scr[...]))
            dv_acc[win, :] += fold(_dot(p_scr[...], do_scr[...]))

        def two_blocks(i, carry):
            block(2 * i, 0)
            block(2 * i + 1, 1)
            return carry

        lax.fori_loop(0, nb // 2, two_blocks, 0)

        dkn = dkn_acc[BLOCK:, :]
        kv = k_ref[...].astype(F32)
        kg = kg_ref[...]
        rk = _head_rstd(kv, lo)
        t = dkn * kg
        dk = rk * t - kv * (rk * rk * rk * (_half_sums(t * kv, lo) * (1.0 / HEAD_DIM)))
        dkg_ref[...] += jnp.sum(dkn * kv * rk, axis=0, keepdims=True)
        dv = dv_acc[BLOCK:, :]

        @pl.when(first)
        def _():
            dkv_keep[:, :128] = dk
            dkv_keep[:, 128:] = dv

        @pl.when(kvh == N_KV_HEADS - 1)
        def _():
            dkv_ref[:, :128] = (dkv_keep[:, :128] + dk).astype(BF16)
            dkv_ref[:, 128:] = (dkv_keep[:, 128:] + dv).astype(BF16)

    smem = pl.BlockSpec(memory_space=pltpu.SMEM)
    kvbuf = pltpu.VMEM((S + BLOCK, 128), BF16)
    accbuf = pltpu.VMEM((S + BLOCK, 128), F32)
    vec = _full((1, 128))
    return pl.pallas_call(
        body, name="attn_bwd", grid=(B, N_KV_HEADS),
        in_specs=[smem, smem,
                  pl.BlockSpec((S, half), lambda b, h: (b, COL_Q // half + h)),
                  pl.BlockSpec((S, 128), lambda b, h: (b, COL_K // 128)),
                  pl.BlockSpec((S, 128), lambda b, h: (b, COL_V // 128)),
                  pl.BlockSpec((S, half), lambda b, h: (b, h)),
                  vec, vec],
        out_specs=[pl.BlockSpec((S, half), lambda b, h: (b, h)), pl.BlockSpec((S, 256), lambda b, h: (b, 0)), vec, vec, vec],
        out_shape=[jax.ShapeDtypeStruct((T, Q_W), BF16), jax.ShapeDtypeStruct((T, 2 * KV_W), BF16),
                   jax.ShapeDtypeStruct((1, 128), F32), jax.ShapeDtypeStruct((1, 128), F32), jax.ShapeDtypeStruct((1, 128), F32)],
        scratch_shapes=[kvbuf, kvbuf, kvbuf, kvbuf, accbuf, accbuf, pltpu.VMEM((S, 256), F32),
                        pltpu.VMEM((2, GROUP, 2 * BLOCK, BLOCK), F32),
                        pltpu.VMEM((2, npair * BLOCK, 4 * BLOCK), F32), pltpu.VMEM((2, npair * BLOCK, 4 * BLOCK), F32),
                        pltpu.VMEM((2, npair * BLOCK, 4 * BLOCK), BF16), pltpu.VMEM((2, npair * BLOCK, 4 * BLOCK), BF16),
                        pltpu.VMEM((2, npair * BLOCK, 128), BF16), pltpu.VMEM((2, npair * BLOCK, 128), BF16),
                        pltpu.VMEM((2, npair * BLOCK, 128), F32)],
        compiler_params=_cp("arbitrary", "arbitrary"),
    )(sinks, slopes, u, u, u, do, qg2, kg2)


def _inproj_bwd_call(du_conv, du_g, du_q, du_kv, w, x, g, dh, tm=512):
    T = x.shape[0]

    def body(dc_ref, dg_ref, dq_ref, dkv_ref, w_ref, x_ref, g_ref, dh_ref, gx_ref, dgm_ref):
        dxn = _dot(dc_ref[...], w_ref[COL_A:COL_GC, :])
        dxn += _dot(dg_ref[...], w_ref[COL_GC:COL_Q, :])
        dxn += _dot(dq_ref[...], w_ref[COL_Q:COL_K, :])
        dxn += _dot(dkv_ref[...], w_ref[COL_K:, :])
        dx, dgt = _rms_bwd(dxn, x_ref[...], g_ref[...])
        gx_ref[...] = dh_ref[...] + dx

        @pl.when(pl.program_id(0) == 0)
        def _():
            dgm_ref[...] = jnp.zeros_like(dgm_ref)

        dgm_ref[...] += jnp.sum(dgt, axis=0, keepdims=True)

    row = _rows(tm, D_MODEL)
    return pl.pallas_call(
        body, name="inproj_bwd", grid=(T // tm,),
        in_specs=[_rows(tm, 2 * C_CONV), _rows(tm, 2 * D_MODEL), _rows(tm, Q_W), _rows(tm, 2 * KV_W), _full(w.shape),
                  row, _full((1, D_MODEL)), row],
        out_specs=[row, _full((1, D_MODEL))],
        out_shape=[jax.ShapeDtypeStruct((T, D_MODEL), F32), jax.ShapeDtypeStruct((1, D_MODEL), F32)],
        compiler_params=_cp("arbitrary"),
    )(du_conv, du_g, du_q, du_kv, w, x, g, dh)


def _adamw_math(w, g, m, v):
    m = ADAM_B1 * m + (1.0 - ADAM_B1) * g
    v = ADAM_B2 * v + (1.0 - ADAM_B2) * (g * g)
    m_hat = m / (1.0 - ADAM_B1 ** ADAM_STEP)
    v_hat = v / (1.0 - ADAM_B2 ** ADAM_STEP)
    delta = -ADAM_LR * (m_hat / (jnp.sqrt(v_hat) + ADAM_EPS) + ADAM_WD * w)
    return delta, m, v


def _adamw_sum_call(name, parts, own, w, m, v, tr):
    R, C = w.shape

    def body(p_ref, own_ref, w_ref, m_ref, v_ref, g_ref, d_ref, nm_ref, nv_ref):
        x, y, c = _position()
        me = 4 * x + 2 * y + c
        g = jnp.zeros((tr, C), F32)
        for s in range(N_DEV):
            g = g + jnp.where(me == s, own_ref[...], p_ref[s]).astype(F32)
        g_ref[...] = g
        d_ref[...], nm_ref[...], nv_ref[...] = _adamw_math(w_ref[...], g, m_ref[...], v_ref[...])

    blk = pl.BlockSpec((tr, C), lambda i: (i, 0))
    out = jax.ShapeDtypeStruct((R, C), F32)
    return pl.pallas_call(
        body, name=name, grid=(R // tr,),
        in_specs=[pl.BlockSpec((N_DEV, tr, C), lambda i: (0, i, 0)), blk, blk, blk, blk],
        out_specs=[blk] * 4, out_shape=[out] * 4,
        compiler_params=_cp("parallel"),
    )(parts, own, w, m, v)


def _adamw_small_call(name, g, w, m, v):
    def body(g_ref, w_ref, m_ref, v_ref, d_ref, nm_ref, nv_ref):
        d_ref[...], nm_ref[...], nv_ref[...] = _adamw_math(w_ref[...], g_ref[...], m_ref[...], v_ref[...])

    out = jax.ShapeDtypeStruct(w.shape, F32)
    return pl.pallas_call(body, name=name, out_shape=[out] * 3)(g, w, m, v)


def _position():
    return lax.axis_index("x"), lax.axis_index("y"), lax.axis_index("c")


def _peer(x, y, c, rel):
    fx, fy, fc = (rel >> 2) & 1, (rel >> 1) & 1, rel & 1
    px = 1 - x if fx else x
    py = 1 - y if fy else y
    pc = 1 - c if fc else c
    return (px, py, pc), 4 * px + 2 * py + pc


def _gather_weights_call(shards):
    n = len(shards)

    def body(*refs):
        ins, outs = refs[:n], refs[n:2 * n]
        send_sems, recv_sems, local_sems = refs[2 * n:]
        x, y, c = _position()
        me = 4 * x + 2 * y + c
        sib, sib_id = _peer(x, y, c, 1)
        chips = [_peer(x, y, c, rel) for rel in (4, 2, 6)]

        def copy(i, k, src, slot, to):
            return pltpu.make_async_remote_copy(
                src_ref=src, dst_ref=outs[i].at[slot], send_sem=send_sems.at[i, k], recv_sem=recv_sems.at[i, k],
                device_id=to, device_id_type=MESH)

        mine = [pltpu.make_async_copy(ins[i], outs[i].at[me], local_sems.at[i]) for i in range(n)]
        for cp in mine:
            cp.start()
        first = []
        for i in range(n):
            first.append(copy(i, 0, ins[i], me, sib))
            first += [copy(i, 1 + j, ins[i], me, dev) for j, (dev, _) in enumerate(chips)]
        for cp in first:
            cp.start()
        passed = []
        for i in range(n):
            for j, (dev, slot) in enumerate(chips):
                copy(i, 1 + j, ins[i], slot, dev).wait_recv()
                fwd = copy(i, 4 + j, outs[i].at[slot], slot, sib)
                fwd.start()
                passed.append(fwd)
        for i in range(n):
            copy(i, 0, ins[i], sib_id, sib).wait_recv()
            for j, (_, slot) in enumerate(chips):
                copy(i, 4 + j, ins[i], slot ^ 1, sib).wait_recv()
        for cp in first + passed:
            cp.wait_send()
        for cp in mine:
            cp.wait()

    hbm = pl.BlockSpec(memory_space=pl.ANY)
    return pl.pallas_call(
        body, name="gather_weights",
        in_specs=[hbm] * n, out_specs=[hbm] * n,
        out_shape=[jax.ShapeDtypeStruct((N_DEV,) + s.shape, s.dtype) for s in shards],
        scratch_shapes=[pltpu.SemaphoreType.DMA((n, 7)), pltpu.SemaphoreType.DMA((n, 7)), pltpu.SemaphoreType.DMA((n,))],
    )(*shards)


_HBM = pl.BlockSpec(memory_space=pltpu.HBM)
_SEM = pl.BlockSpec(memory_space=pltpu.SEMAPHORE)
_EFFECT = pltpu.SideEffectType.DATAFLOW_SIDE_EFFECTING


def _exchange_copies(scatter, incoming, src_refs, land_refs, send_sems, recv_sems):
    x, y, c = _position()
    me = 4 * x + 2 * y + c
    out = []
    for i, (src, land) in enumerate(zip(src_refs, land_refs)):
        for k in range(N_DEV - 1):
            dev, dev_id = _peer(x, y, c, k + 1)
            j = i * (N_DEV - 1) + k
            out.append(pltpu.make_async_remote_copy(
                src_ref=src.at[dev_id] if scatter else src, dst_ref=land.at[dev_id if incoming else me],
                send_sem=send_sems.at[j], recv_sem=recv_sems.at[j], device_id=dev, device_id_type=MESH))
    return out


def _exchange_start_call(name, arrays, scatter):
    n = len(arrays)
    lands = [lax.empty((N_DEV,) + (a.shape[1:] if scatter else a.shape), a.dtype) for a in arrays]

    def body(*refs):
        src_refs, land_refs = refs[:n], refs[n:2 * n]
        send_sems, recv_sems, token = refs[2 * n], refs[2 * n + 1], refs[-1]
        for copy in _exchange_copies(scatter, False, src_refs, land_refs, send_sems, recv_sems):
            copy.start()
        token[...] = jnp.zeros_like(token)

    sems = pltpu.SemaphoreType.DMA((n * (N_DEV - 1),))
    res = pl.pallas_call(
        body, name=name,
        out_shape=[sems, sems] + [pltpu.HBM(a.shape, a.dtype) for a in arrays] + [pltpu.HBM(l.shape, l.dtype) for l in lands]
        + [jax.ShapeDtypeStruct((8, 128), F32)],
        in_specs=[_HBM] * (2 * n), out_specs=[_SEM, _SEM] + [_HBM] * (2 * n) + [pl.BlockSpec(memory_space=pltpu.VMEM)],
        input_output_aliases={i: 2 + i for i in range(2 * n)},
        compiler_params=pltpu.CompilerParams(has_side_effects=_EFFECT),
    )(*[pltpu.with_memory_space_constraint(a, pltpu.HBM) for a in list(arrays) + lands])
    return res[0], res[1], res[2:2 + n], res[2 + n:2 + 2 * n], res[-1]


def _exchange_wait_call(name, started, scatter, after):
    send_sems, recv_sems, srcs, lands, _ = started
    n = len(srcs)

    def body(*refs):
        src_refs, land_refs = refs[:n], refs[n:2 * n]
        send_s, recv_s = refs[2 * n], refs[2 * n + 1]
        for copy in _exchange_copies(scatter, False, src_refs, land_refs, send_s, recv_s):
            copy.wait_send()
        for copy in _exchange_copies(scatter, True, src_refs, land_refs, send_s, recv_s):
            copy.wait_recv()

    res = pl.pallas_call(
        body, name=name,
        out_shape=[pltpu.HBM(a.shape, a.dtype) for a in list(srcs) + list(lands)],
        in_specs=[_HBM] * (2 * n) + [_SEM, _SEM, pl.BlockSpec(memory_space=pl.ANY)], out_specs=[_HBM] * (2 * n),
        input_output_aliases={i: i for i in range(2 * n)},
        compiler_params=pltpu.CompilerParams(has_side_effects=_EFFECT),
    )(*srcs, *lands, send_sems, recv_sems, after)
    return res[n:]


def _allsum_small_call(part, blocks):
    P, Q = part.shape[0], blocks.shape[1]

    def body(p_ref, b_ref, o_ref, ob_ref, all_ref, allb_ref, send_sems, recv_sems):
        x, y, c = _position()
        me = 4 * x + 2 * y + c
        peers = [_peer(x, y, c, rel) for rel in range(1, N_DEV)]
        all_ref[me] = p_ref[...]
        allb_ref[me] = b_ref[me]

        def copies(k, slot, dev, dev_id):
            sems = lambda i: dict(send_sem=send_sems.at[i], recv_sem=recv_sems.at[i], device_id=dev, device_id_type=MESH)
            return (pltpu.make_async_remote_copy(src_ref=p_ref, dst_ref=all_ref.at[slot], **sems(k)),
                    pltpu.make_async_remote_copy(src_ref=b_ref.at[dev_id], dst_ref=allb_ref.at[slot], **sems(7 + k)))

        sends = [cp for k, (dev, dev_id) in enumerate(peers) for cp in copies(k, me, dev, dev_id)]
        for cp in sends:
            cp.start()
        for k, (dev, dev_id) in enumerate(peers):
            for cp in copies(k, dev_id, dev, dev_id):
                cp.wait_recv()
        for cp in sends:
            cp.wait_send()
        tot, totb = all_ref[0], allb_ref[0]
        for s in range(1, N_DEV):
            tot, totb = tot + all_ref[s], totb + allb_ref[s]
        o_ref[...] = tot
        ob_ref[...] = totb

    vm = pl.BlockSpec(memory_space=pltpu.VMEM)
    return pl.pallas_call(
        body, name="allsum_small", in_specs=[vm, vm], out_specs=[vm, vm],
        out_shape=[jax.ShapeDtypeStruct((P, 128), F32), jax.ShapeDtypeStruct((Q, 128), F32)],
        scratch_shapes=[pltpu.VMEM((N_DEV, P, 128), F32), pltpu.VMEM((N_DEV, Q, 128), F32),
                        pltpu.SemaphoreType.DMA((14,)), pltpu.SemaphoreType.DMA((14,))],
    )(part, blocks)


def _pad_rows(v, rows):
    flat = v.reshape(-1)
    return jnp.pad(flat, (0, rows * 128 - flat.shape[0])).reshape(rows, 128)


def _local_step(x, tgt, w_in_i, conv_w, conv_b, ln_g, ln_b, qg, kg, sinks, g_mix, g_ffn, token0, rest_weights, grads_ready, B, S):
    row = lambda v: v.reshape(1, -1)
    qg2 = jnp.tile(qg, 2).reshape(1, 128)
    kg2 = jnp.tile(kg, 2).reshape(1, 128)
    heads = jnp.arange(1, N_Q_HEADS + 1, dtype=F32)
    slopes = jnp.exp2(-8.0 * heads / N_Q_HEADS)

    u, xn = _inproj_call(x, row(g_mix) + token0, w_in_i)
    hc, c = _conv_fwd_call(u, conv_w, row(conv_b), row(ln_g), row(ln_b), B, S)
    o = _attn_fwd_call(u, qg2, kg2, sinks, slopes, B, S)
    wc, wa, wm, w_ffn_in, wd = rest_weights(o)
    yc, ya, z, h, hn = _merge_call(c, o, u, x, wc, wa, wm, row(g_ffn))
    gu, act = _ffn_in_call(hn, w_ffn_in)
    sse, dout, dout_b = _ffn_out_call(act, wd, h, tgt)

    dgu = _ffn_down_bwd_call(dout_b, wd, gu)
    g_wd = _dw_call("dw_ffn_down", act, dout_b, tr=D_FF // 2)
    g_wffn_t = _dw_call("dw_ffn_in", dgu, hn, tr=2 * D_FF // 4)
    tok = grads_ready(dict(w_ffn_down=g_wd, w_ffn_in=g_wffn_t))
    dh, dh_b, g_gffn = _ffn_in_bwd_call(dgu, w_ffn_in, h, row(g_ffn) + tok, dout)
    dyc, dya, du_g, dcp, do = _merge_bwd_call(dh_b, wm, wc, wa, u, yc, ya)
    g_wm = _dw_call("dw_merge", z, dh_b, tr=D_MODEL)
    g_wc = _dw_call("dw_conv_out", c, dyc, tr=D_MODEL)
    g_wa = _dw_call("dw_attn_out", o, dya, tr=D_MODEL)
    tok = grads_ready(dict(w_merge_out=g_wm, w_conv_out=g_wc, w_attn_out=g_wa))
    du_conv, g_cw8, g_cb, g_lg, g_lb = _conv_bwd_call(dcp, hc, u, conv_w, row(ln_g) + tok, row(ln_b), B, S)
    du_q, du_kv, g_qg2, g_kg2, g_sk = _attn_bwd_call(u, do, qg2, kg2, sinks, slopes, B, S)
    g_win_t = jnp.concatenate([
        _dw_call("dw_in_conv", du_conv, xn, tr=D_MODEL),
        _dw_call("dw_in_q", du_q, xn, tr=D_MODEL),
        _dw_call("dw_in_kv", du_kv, xn, tr=2 * KV_W),
        _dw_call("dw_in_gates", du_g, xn, tr=D_MODEL)], axis=0)
    tok = grads_ready(dict(w_in=g_win_t))
    grad_x, g_gmix = _inproj_bwd_call(du_conv, du_g, du_q, du_kv, w_in_i, x, row(g_mix) + tok, dh)

    g_cw = g_cw8.reshape(CONV_WIDTH, 8, C_CONV).sum(axis=1)
    g_qg = g_qg2[0, :HEAD_DIM] + g_qg2[0, HEAD_DIM:]
    g_kg = g_kg2[0, :HEAD_DIM] + g_kg2[0, HEAD_DIM:]
    small = dict(norm_mix_g=g_gmix[0], conv_dw_b=g_cb[0], conv_ln_g=g_lg[0], conv_ln_b=g_lb[0], q_norm_g=g_qg,
                 k_norm_g=g_kg, sinks=g_sk[0, :N_Q_HEADS], norm_ffn_g=g_gffn[0], conv_dw_w=g_cw)
    return sse[0, 0], grad_x, small


def _to_internal_rows(wt):
    o1 = 2 * C_CONV
    o4 = o1 + Q_W + 2 * KV_W
    return jnp.concatenate([wt[:o1], wt[o4:], wt[o1:o4]], axis=0)


_COL_SHARDED = ("w_in", "w_ffn_in")

_SMALL = (("norm_mix_g", 8), ("conv_dw_b", 8), ("conv_ln_g", 8), ("conv_ln_b", 8), ("norm_ffn_g", 8),
          ("q_norm_g", 1), ("k_norm_g", 1), ("sinks", 1))
_SMALL_ROWS = 48


def _pack_small(d):
    rows = [_pad_rows(d[name], r) for name, r in _SMALL]
    used = sum(r for _, r in _SMALL)
    return jnp.concatenate(rows + [jnp.zeros((_SMALL_ROWS - used, 128), F32)], axis=0)


def _unpack_small(packed, like):
    out, r0 = {}, 0
    for name, r in _SMALL:
        n = like[name].size
        out[name] = packed[r0:r0 + r].reshape(-1)[:n].reshape(like[name].shape)
        r0 += r
    return out


def kernel(x, norm_mix_g, w_in, conv_dw_w, conv_dw_b, conv_ln_g, conv_ln_b, w_conv_out, q_norm_g, k_norm_g, sinks, w_attn_out, w_merge_out, norm_ffn_g, w_ffn_in, w_ffn_down, loss_target, m_norm_mix_g, m_w_in, m_conv_dw_w, m_conv_dw_b, m_conv_ln_g, m_conv_ln_b, m_w_conv_out, m_q_norm_g, m_k_norm_g, m_sinks, m_w_attn_out, m_w_merge_out, m_norm_ffn_g, m_w_ffn_in, m_w_ffn_down, v_norm_mix_g, v_w_in, v_conv_dw_w, v_conv_dw_b, v_conv_ln_g, v_conv_ln_b, v_w_conv_out, v_q_norm_g, v_k_norm_g, v_sinks, v_w_attn_out, v_w_merge_out, v_norm_ffn_g, v_w_ffn_in, v_w_ffn_down):
    names = ["norm_mix_g", "w_in", "conv_dw_w", "conv_dw_b", "conv_ln_g", "conv_ln_b", "w_conv_out", "q_norm_g",
             "k_norm_g", "sinks", "w_attn_out", "w_merge_out", "norm_ffn_g", "w_ffn_in", "w_ffn_down"]
    w = dict(zip(names, (norm_mix_g, w_in, conv_dw_w, conv_dw_b, conv_ln_g, conv_ln_b, w_conv_out, q_norm_g, k_norm_g,
                         sinks, w_attn_out, w_merge_out, norm_ffn_g, w_ffn_in, w_ffn_down)))
    m = dict(zip(names, (m_norm_mix_g, m_w_in, m_conv_dw_w, m_conv_dw_b, m_conv_ln_g, m_conv_ln_b, m_w_conv_out,
                         m_q_norm_g, m_k_norm_g, m_sinks, m_w_attn_out, m_w_merge_out, m_norm_ffn_g, m_w_ffn_in,
                         m_w_ffn_down)))
    v = dict(zip(names, (v_norm_mix_g, v_w_in, v_conv_dw_w, v_conv_dw_b, v_conv_ln_g, v_conv_ln_b, v_w_conv_out,
                         v_q_norm_g, v_k_norm_g, v_sinks, v_w_attn_out, v_w_merge_out, v_norm_ffn_g, v_w_ffn_in,
                         v_w_ffn_down)))
    B, S, _ = x.shape
    T = B * S
    me = 4 * lax.axis_index("x") + 2 * lax.axis_index("y") + lax.axis_index("c")

    as_rows = lambda k, t: t.T if k in _COL_SHARDED else t
    cw_shard = _pad_rows(conv_dw_w.reshape(CONV_WIDTH, 128), 32)
    g_win, g_cwt = _gather_weights_call([w_in.astype(BF16), cw_shard])
    conv_w_full = g_cwt.reshape(N_DEV, 32 * 128)[:, :CONV_WIDTH * 128].reshape(N_DEV, CONV_WIDTH, 128)
    conv_w_full = conv_w_full.transpose(1, 0, 2).reshape(CONV_WIDTH, C_CONV)
    w_in_i = _to_internal_rows(g_win.transpose(0, 2, 1).reshape(IN_COLS, D_MODEL))
    rest_names = ["w_conv_out", "w_attn_out", "w_merge_out", "w_ffn_in", "w_ffn_down"]
    rest_shards = [as_rows(k, w[k]).astype(BF16) for k in rest_names]
    rest_started = _exchange_start_call("gather_rest_start", rest_shards, scatter=False)

    def rest_weights(after):
        lands = _exchange_wait_call("gather_rest_wait", rest_started, False, after)
        mine = lax.broadcasted_iota(jnp.int32, (N_DEV, 1, 1), 0) == me
        full = [jnp.where(mine, s[None], l) for l, s in zip(lands, rest_shards)]
        return tuple(f.reshape(N_DEV * f.shape[1], D_MODEL) for f in full)

    exchanges = []

    def grads_ready(group):
        ks = list(group)
        blocks = [group[k].reshape(N_DEV, group[k].shape[0] // N_DEV, D_MODEL) for k in ks]
        started = _exchange_start_call("scatter_" + ks[0] + "_start", blocks, scatter=True)
        exchanges.append((ks, started, [lax.dynamic_index_in_dim(b, me, 0, keepdims=False) for b in blocks]))
        return started[-1][0:1, 0:1]

    sse, grad_x, small = _local_step(
        x.reshape(T, D_MODEL), loss_target.reshape(T, D_MODEL), w_in_i, conv_w_full, conv_dw_b, conv_ln_g, conv_ln_b,
        q_norm_g, k_norm_g, sinks, norm_mix_g, norm_ffn_g, rest_started[-1][0:1, 0:1], rest_weights, grads_ready, B, S)

    grad, delta, new_m, new_v = {}, {}, {}, {}
    packed = jnp.concatenate([_pack_small(small), jnp.full((8, 128), sse, F32)], axis=0)
    cw_blocks = small["conv_dw_w"].reshape(CONV_WIDTH, N_DEV, 128).transpose(1, 0, 2)
    tot, cw_tot = _allsum_small_call(packed, jnp.pad(cw_blocks, ((0, 0), (0, 32 - CONV_WIDTH), (0, 0))))
    loss = 0.5 / D_MODEL * tot[_SMALL_ROWS, 0]
    g_small = _unpack_small(tot[:_SMALL_ROWS], w)
    d_s, m_s, v_s = _adamw_small_call("adamw_small", tot[:_SMALL_ROWS], _pack_small(w), _pack_small(m), _pack_small(v))
    d_small, m_small, v_small = _unpack_small(d_s, w), _unpack_small(m_s, w), _unpack_small(v_s, w)
    g_cw = cw_tot[:CONV_WIDTH]
    cw2 = lambda t: t.reshape(CONV_WIDTH, 128)
    d_cw, m_cw, v_cw = _adamw_small_call("adamw_conv_w", g_cw, cw2(conv_dw_w), cw2(m_conv_dw_w), cw2(v_conv_dw_w))
    cw3 = lambda t: t.reshape(CONV_WIDTH, 1, 128)
    for k, _ in _SMALL:
        grad[k], delta[k], new_m[k], new_v[k] = g_small[k], d_small[k], m_small[k], v_small[k]
    grad["conv_dw_w"], delta["conv_dw_w"], new_m["conv_dw_w"], new_v["conv_dw_w"] = cw3(g_cw), cw3(d_cw), cw3(m_cw), cw3(v_cw)

    after = d_cw
    for ks, started, owns in exchanges:
        lands = _exchange_wait_call("scatter_" + ks[0] + "_wait", started, True, after)
        for k, land, own in zip(ks, lands, owns):
            tr = {672: 336, 704: 352, 352: 176, 128: 128}[own.shape[0]]
            outs = _adamw_sum_call("adamw_" + k, land, own, as_rows(k, w[k]), as_rows(k, m[k]), as_rows(k, v[k]), tr)
            grad[k], delta[k], new_m[k], new_v[k] = (as_rows(k, t) for t in outs)
            after = outs[3]

    return (loss, grad_x.reshape(B, S, D_MODEL), *[grad[k] for k in names], *[delta[k] for k in names],
            *[new_m[k] for k in names], *[new_v[k] for k in names])
```

```python
import functools
import math

import jax
import jax.numpy as jnp
from jax import lax
from jax.experimental import pallas as pl
from jax.experimental.pallas import tpu as pltpu

F32 = jnp.float32
BF16 = jnp.bfloat16
MESH = pl.DeviceIdType.MESH

N_DEV = 8
D_MODEL = 1024
C_CONV = 1024
CONV_WIDTH = 31
HEAD_DIM = 64
N_Q_HEADS = 16
N_KV_HEADS = 2
GROUP = N_Q_HEADS // N_KV_HEADS
WINDOW = 128
BLOCK = 128
D_FF = 2816
EPS = 1e-6
NEG = -1e30
ATTN_SCALE = 1.0 / math.sqrt(HEAD_DIM)
Q_W = N_Q_HEADS * HEAD_DIM
KV_W = N_KV_HEADS * HEAD_DIM
IN_COLS = 2 * C_CONV + Q_W + 2 * KV_W + 2 * D_MODEL

ADAM_LR = 0.001
ADAM_B1 = 0.9
ADAM_B2 = 0.999
ADAM_EPS = 1e-08
ADAM_WD = 0.01
ADAM_STEP = 10

COL_A, COL_GATE, COL_GC, COL_GA, COL_Q, COL_K, COL_V = 0, 1024, 2048, 3072, 4096, 5120, 5248

HALO = 32
VMEM_LIMIT = 56 * 1024 * 1024


def _cp(*sem):
    return pltpu.CompilerParams(dimension_semantics=sem, vmem_limit_bytes=VMEM_LIMIT)


def _sigmoid(x):
    return jax.nn.sigmoid(x)


def _dot(a, b):
    return jnp.dot(a, b, preferred_element_type=F32)


def _dot_nt(a, b):
    return lax.dot_general(a, b, (((1,), (1,)), ((), ())), preferred_element_type=F32)


def _dot_tn(a, b):
    return lax.dot_general(a, b, (((0,), (0,)), ((), ())), preferred_element_type=F32)


def _full(shape):
    return pl.BlockSpec(shape, lambda *_: (0,) * len(shape))


def _rows(tm, ncols, colblk=0):
    return pl.BlockSpec((tm, ncols), lambda i: (i, colblk))


def _inproj_call(x, g, w, tm=512):
    T = x.shape[0]
    N = w.shape[0]

    def body(x_ref, g_ref, w_ref, u_ref, xn_ref):
        xv = x_ref[...]
        r = lax.rsqrt(jnp.mean(xv * xv, axis=-1, keepdims=True) + EPS)
        xn = (xv * r * g_ref[...]).astype(BF16)
        xn_ref[...] = xn
        u_ref[...] = _dot_nt(xn, w_ref[...]).astype(BF16)

    return pl.pallas_call(
        body, name="inproj", grid=(T // tm,),
        in_specs=[_rows(tm, D_MODEL), _full((1, D_MODEL)), _full(w.shape)],
        out_specs=[_rows(tm, N), _rows(tm, D_MODEL)],
        out_shape=[jax.ShapeDtypeStruct((T, N), BF16), jax.ShapeDtypeStruct((T, D_MODEL), BF16)],
        compiler_params=_cp("parallel"),
    )(x, g, w)


def _fill_shifted(src_ref, sh_ref):
    n = src_ref.shape[0] - 8
    for s in range(1, 8):
        sh_ref[s - 1, 0:n, :] = src_ref[s:s + n, :]


def _window(src_ref, sh_ref, off, cols):
    s = off % 8
    if s == 0:
        return src_ref[off:off + 128, cols]
    return sh_ref[s - 1, off - s:off - s + 128, cols]


def _conv_taps(w_ref, src_ref, sh_ref, base, rows, cb, reverse):
    cols = pl.ds(pl.multiple_of(cb * 128, 128), 128)
    outs = []
    for rb in range(rows // 128):
        acc = jnp.zeros((128, 128), F32)
        for j in range(CONV_WIDTH):
            off = base + (CONV_WIDTH - 1 - j if reverse else j) + rb * 128
            acc = acc + w_ref[j:j + 1, cols] * _window(src_ref, sh_ref, off, cols)
        outs.append(acc)
    return outs


def _conv_fwd_call(u, conv_w, conv_b, ln_g, ln_b, B, S, ts=512):
    T = B * S
    nS = S // ts
    per32 = ts // HALO

    def body(a_ref, gate_ref, ap_ref, gp_ref, w_ref, b_ref, g_ref, bb_ref, hc_ref, c_ref, hext, hsh, acc_ref):
        s = pl.program_id(1)
        prev = ap_ref[...].astype(F32) * _sigmoid(gp_ref[...].astype(F32))
        hext[0:HALO, :] = jnp.where(s > 0, prev, 0.0)
        hext[HALO:, :] = a_ref[...].astype(F32) * _sigmoid(gate_ref[...].astype(F32))
        _fill_shifted(hext, hsh)

        def colblock(cb, carry):
            cols = pl.ds(pl.multiple_of(cb * 128, 128), 128)
            outs = _conv_taps(w_ref, hext, hsh, HALO - (CONV_WIDTH - 1), ts, cb, reverse=False)
            for rb, acc in enumerate(outs):
                acc_ref[rb * 128:(rb + 1) * 128, cols] = acc
            return carry

        lax.fori_loop(0, C_CONV // 128, colblock, 0)
        hc = acc_ref[...] + b_ref[...]
        hc_ref[...] = hc
        mu = jnp.mean(hc, axis=-1, keepdims=True)
        xc = hc - mu
        var = jnp.mean(xc * xc, axis=-1, keepdims=True)
        y = xc * lax.rsqrt(var + EPS) * g_ref[...] + bb_ref[...]
        c_ref[...] = (y * _sigmoid(y)).astype(BF16)

    cur = lambda cb: pl.BlockSpec((ts, C_CONV), lambda b, s: (b * nS + s, cb))
    prv = lambda cb: pl.BlockSpec((HALO, C_CONV), lambda b, s: (jnp.maximum(b * (S // HALO) + s * per32 - 1, 0), cb))
    vec = _full((1, C_CONV))
    return pl.pallas_call(
        body, name="conv_fwd", grid=(B, nS),
        in_specs=[cur(0), cur(1), prv(0), prv(1), _full((CONV_WIDTH, C_CONV)), vec, vec, vec],
        out_specs=[pl.BlockSpec((ts, C_CONV), lambda b, s: (b * nS + s, 0))] * 2,
        out_shape=[jax.ShapeDtypeStruct((T, C_CONV), F32), jax.ShapeDtypeStruct((T, C_CONV), BF16)],
        scratch_shapes=[pltpu.VMEM((HALO + ts, C_CONV), F32), pltpu.VMEM((7, HALO + ts, C_CONV), F32),
                        pltpu.VMEM((ts, C_CONV), F32)],
        compiler_params=_cp("parallel", "arbitrary"),
    )(u, u, u, u, conv_w, conv_b, ln_g, ln_b)


def _lane_lo():
    return lax.broadcasted_iota(jnp.int32, (1, 128), 1) < HEAD_DIM


def _half_sums(t, lo):
    s_lo = jnp.sum(jnp.where(lo, t, 0.0), axis=-1, keepdims=True)
    s_hi = jnp.sum(jnp.where(lo, 0.0, t), axis=-1, keepdims=True)
    return jnp.where(lo, s_lo, s_hi)


def _head_rstd(t, lo):
    return lax.rsqrt(_half_sums(t * t, lo) * (1.0 / HEAD_DIM) + EPS)


def _fill_bias(btab, kvh, slopes_ref):
    sj = lax.broadcasted_iota(jnp.int32, (2 * BLOCK, BLOCK), 0)
    qi = lax.broadcasted_iota(jnp.int32, (2 * BLOCK, BLOCK), 1)
    dist = qi + BLOCK - sj
    valid = (dist >= 0) & (dist < WINDOW)
    distf = dist.astype(F32)
    for g in range(GROUP):
        bias = jnp.where(valid, -slopes_ref[kvh * GROUP + g] * distf, NEG)
        btab[1, g] = bias
        btab[0, g] = jnp.where(sj >= BLOCK, bias, NEG)


def _softmax_cols(st_ref, btab, p, var, g, hf, sk):
    sh = st_ref[2 * BLOCK * hf:2 * BLOCK * (hf + 1), 128 * p:128 * (p + 1)] + btab[var, g]
    m = jnp.maximum(jnp.max(sh, axis=0, keepdims=True), sk)
    e = jnp.exp(sh - m)
    esk = jnp.exp(sk - m)
    rz = 1.0 / (jnp.sum(e, axis=0, keepdims=True) + esk)
    return e * rz, esk, rz


def _stage_kv(kvh, k_ref, v_ref, kg_ref, lo, kA, kB, vA, vB):
    kv = k_ref[...].astype(F32)
    rk = _head_rstd(kv, lo)
    kn = kv * rk * kg_ref[...]
    mine = lax.broadcasted_iota(jnp.int32, (1, 128), 1) // HEAD_DIM == kvh
    ksel = jnp.where(mine, kn, 0.0)
    vsel = jnp.where(mine, v_ref[...].astype(F32), 0.0)
    krol = pltpu.roll(ksel, HEAD_DIM, 1)
    vrol = pltpu.roll(vsel, HEAD_DIM, 1)
    first = kvh == 0
    zeros = jnp.zeros((BLOCK, 128), BF16)
    for ref, val in ((kA, jnp.where(first, ksel, krol)), (kB, jnp.where(first, krol, ksel)),
                     (vA, jnp.where(first, vsel, vrol)), (vB, jnp.where(first, vrol, vsel))):
        ref[0:BLOCK, :] = zeros
        ref[BLOCK:, :] = val.astype(BF16)


def _attn_fwd_call(u, qg2, kg2, sinks, slopes, B, S):
    T = B * S
    nb = S // BLOCK
    half = Q_W // 2
    npair = GROUP // 2

    def body(sinks_ref, slopes_ref, q_ref, k_ref, v_ref, qg_ref, kg_ref, o_ref, kA, kB, vA, vB, btab, qn_buf, s_buf, p_buf):
        kvh = pl.program_id(1)
        lo = _lane_lo()
        _stage_kv(kvh, k_ref, v_ref, kg_ref, lo, kA, kB, vA, vB)
        _fill_bias(btab, kvh, slopes_ref)
        qgs = qg_ref[...] * ATTN_SCALE

        def block(n, slot):
            qn_scr, s_scr, p_scr = qn_buf.at[slot], s_buf.at[slot], p_buf.at[slot]
            r0 = pl.multiple_of(n * BLOCK, BLOCK)
            win = pl.ds(r0, 2 * BLOCK)
            k2t = jnp.concatenate([kA[win, :], kB[win, :]], axis=0)
            v2 = jnp.concatenate([vA[win, :], vB[win, :]], axis=0)
            var = jnp.minimum(n, 1)
            for p in range(npair):
                pc = slice(128 * p, 128 * (p + 1))
                qp = q_ref[pl.ds(r0, BLOCK), pc].astype(F32)
                qn_scr[pc, :] = (qp * _head_rstd(qp, lo) * qgs).astype(BF16)
            s_scr[...] = _dot_nt(k2t, qn_scr[...])
            for p in range(npair):
                for hf in range(2):
                    g = 2 * p + hf
                    pn, _, _ = _softmax_cols(s_scr, btab, p, var, g, hf, sinks_ref[kvh * GROUP + g])
                    p_scr[2 * BLOCK * hf:2 * BLOCK * (hf + 1), 128 * p:128 * (p + 1)] = pn.astype(BF16)
            o_all = _dot_tn(v2, p_scr[...]).T
            for p in range(npair):
                pc = slice(128 * p, 128 * (p + 1))
                o_ref[pl.ds(r0, BLOCK), pc] = o_all[pc, :].astype(BF16)

        def two_blocks(i, carry):
            block(2 * i, 0)
            block(2 * i + 1, 1)
            return carry

        lax.fori_loop(0, nb // 2, two_blocks, 0)

    smem = pl.BlockSpec(memory_space=pltpu.SMEM)
    kvbuf = pltpu.VMEM((S + BLOCK, 128), BF16)
    pair_scores = (2, npair * BLOCK, 4 * BLOCK)
    return pl.pallas_call(
        body, name="attn_fwd", grid=(B, N_KV_HEADS),
        in_specs=[smem, smem,
                  pl.BlockSpec((S, half), lambda b, h: (b, COL_Q // half + h)),
                  pl.BlockSpec((S, 128), lambda b, h: (b, COL_K // 128)),
                  pl.BlockSpec((S, 128), lambda b, h: (b, COL_V // 128)),
                  _full((1, 128)), _full((1, 128))],
        out_specs=pl.BlockSpec((S, half), lambda b, h: (b, h)),
        out_shape=jax.ShapeDtypeStruct((T, Q_W), BF16),
        scratch_shapes=[kvbuf, kvbuf, kvbuf, kvbuf, pltpu.VMEM((2, GROUP, 2 * BLOCK, BLOCK), F32),
                        pltpu.VMEM((2, npair * BLOCK, 128), BF16), pltpu.VMEM(pair_scores, F32), pltpu.VMEM(pair_scores, BF16)],
        compiler_params=_cp("parallel", "arbitrary"),
    )(sinks, slopes, u, u, u, qg2, kg2)


def _merge_call(c, o, u, x, wc, wa, wm, g_ffn, tm=512):
    T = x.shape[0]

    def body(c_ref, o_ref, gc_ref, ga_ref, x_ref, wc_ref, wa_ref, wm_ref, g_ref, yc_ref, ya_ref, z_ref, h_ref, hn_ref):
        yc = _dot(c_ref[...], wc_ref[...])
        ya = _dot(o_ref[...], wa_ref[...])
        yc_ref[...] = yc.astype(BF16)
        ya_ref[...] = ya.astype(BF16)
        z = (_sigmoid(gc_ref[...].astype(F32)) * yc + _sigmoid(ga_ref[...].astype(F32)) * ya).astype(BF16)
        z_ref[...] = z
        h = x_ref[...] + _dot(z, wm_ref[...])
        h_ref[...] = h
        r = lax.rsqrt(jnp.mean(h * h, axis=-1, keepdims=True) + EPS)
        hn_ref[...] = (h * r * g_ref[...]).astype(BF16)

    sq = _full((D_MODEL, D_MODEL))
    row = _rows(tm, D_MODEL)
    f32 = jax.ShapeDtypeStruct((T, D_MODEL), F32)
    b16 = jax.ShapeDtypeStruct((T, D_MODEL), BF16)
    return pl.pallas_call(
        body, name="merge_fwd", grid=(T // tm,),
        in_specs=[row, row, _rows(tm, D_MODEL, COL_GC // D_MODEL), _rows(tm, D_MODEL, COL_GA // D_MODEL), row,
                  sq, sq, sq, _full((1, D_MODEL))],
        out_specs=[row] * 5,
        out_shape=[b16, b16, b16, f32, b16],
        compiler_params=_cp("parallel"),
    )(c, o, u, u, x, wc, wa, wm, g_ffn)


def _ffn_in_call(hn, w, tm=512):
    T = hn.shape[0]

    def body(hn_ref, w_ref, fac_ref, act_ref):
        gu = _dot_nt(hn_ref[...], w_ref[...])
        g = gu[:, :D_FF]
        up = gu[:, D_FF:]
        sg = _sigmoid(g)
        silu = g * sg
        act_ref[...] = (silu * up).astype(BF16)
        fac_ref[:, :D_FF] = (up * (sg * (1.0 + g * (1.0 - sg)))).astype(BF16)
        fac_ref[:, D_FF:] = silu.astype(BF16)

    return pl.pallas_call(
        body, name="ffn_in", grid=(T // tm,),
        in_specs=[_rows(tm, D_MODEL), _full(w.shape)],
        out_specs=[_rows(tm, 2 * D_FF), _rows(tm, D_FF)],
        out_shape=[jax.ShapeDtypeStruct((T, 2 * D_FF), BF16), jax.ShapeDtypeStruct((T, D_FF), BF16)],
        compiler_params=_cp("parallel"),
    )(hn, w)


def _ffn_out_call(act, wd, h, tgt, tm=512):
    T = h.shape[0]

    def body(act_ref, wd_ref, h_ref, t_ref, sse_ref, d_ref, db_ref):
        e = h_ref[...] + _dot(act_ref[...], wd_ref[...]) - t_ref[...]

        @pl.when(pl.program_id(0) == 0)
        def _():
            sse_ref[...] = jnp.zeros_like(sse_ref)

        sse_ref[...] += jnp.sum(e * e)
        d = e * (1.0 / D_MODEL)
        d_ref[...] = d
        db_ref[...] = d.astype(BF16)

    row = _rows(tm, D_MODEL)
    return pl.pallas_call(
        body, name="ffn_out", grid=(T // tm,),
        in_specs=[_rows(tm, D_FF), _full(wd.shape), row, row],
        out_specs=[_full((8, 128)), row, row],
        out_shape=[jax.ShapeDtypeStruct((8, 128), F32), jax.ShapeDtypeStruct((T, D_MODEL), F32),
                   jax.ShapeDtypeStruct((T, D_MODEL), BF16)],
        compiler_params=_cp("arbitrary"),
    )(act, wd, h, tgt)


def _dw_call(name, a, b, tr, tk=2048):
    T, K = a.shape
    N = b.shape[1]
    tk = min(tk, T)
    nk = T // tk

    def body(a_ref, b_ref, o_ref, acc_ref):
        k = pl.program_id(1)

        @pl.when(k == 0)
        def _():
            acc_ref[...] = jnp.zeros_like(acc_ref)

        acc_ref[...] += _dot_tn(a_ref[...], b_ref[...])

        @pl.when(k == nk - 1)
        def _():
            o_ref[...] = acc_ref[...].astype(BF16)

    return pl.pallas_call(
        body, name=name, grid=(K // tr, nk),
        in_specs=[pl.BlockSpec((tk, tr), lambda j, k: (k, j)), pl.BlockSpec((tk, N), lambda j, k: (k, 0))],
        out_specs=pl.BlockSpec((tr, N), lambda j, k: (j, 0)),
        out_shape=jax.ShapeDtypeStruct((K, N), BF16),
        scratch_shapes=[pltpu.VMEM((tr, N), F32)],
        compiler_params=_cp("parallel", "arbitrary"),
    )(a, b)


def _concat_rows_call(name, pieces, tr=256):
    N = pieces[0].shape[1]
    starts = [0]
    for p in pieces:
        starts.append(starts[-1] + p.shape[0] // tr)

    def body(*refs):
        o_ref = refs[-1]
        i = pl.program_id(0)
        for p, ref in enumerate(refs[:-1]):
            @pl.when((i >= starts[p]) & (i < starts[p + 1]))
            def _(ref=ref):
                o_ref[...] = ref[...]

    def spec(p):
        lo, n = starts[p], pieces[p].shape[0] // tr
        return pl.BlockSpec((tr, N), lambda i: (jnp.clip(i - lo, 0, n - 1), 0))

    return pl.pallas_call(
        body, name=name, grid=(starts[-1],),
        in_specs=[spec(p) for p in range(len(pieces))], out_specs=pl.BlockSpec((tr, N), lambda i: (i, 0)),
        out_shape=jax.ShapeDtypeStruct((starts[-1] * tr, N), pieces[0].dtype),
        compiler_params=_cp("arbitrary"),
    )(*pieces)


def _rms_bwd(dy, xv, g):
    r = lax.rsqrt(jnp.mean(xv * xv, axis=-1, keepdims=True) + EPS)
    t = dy * g
    dx = r * t - xv * (r * r * r * jnp.mean(t * xv, axis=-1, keepdims=True))
    return dx, dy * xv * r


def _ffn_down_bwd_call(dout_b, wd, gu, tm=512):
    T = dout_b.shape[0]

    def body(d_ref, wd_ref, fg_ref, fu_ref, o_ref):
        dact = _dot_nt(d_ref[...], wd_ref[...])
        o_ref[:, :D_FF] = (dact * fg_ref[...].astype(F32)).astype(BF16)
        o_ref[:, D_FF:] = (dact * fu_ref[...].astype(F32)).astype(BF16)

    return pl.pallas_call(
        body, name="ffn_down_bwd", grid=(T // tm,),
        in_specs=[_rows(tm, D_MODEL), _full(wd.shape), _rows(tm, D_FF, 0), _rows(tm, D_FF, 1)],
        out_specs=_rows(tm, 2 * D_FF),
        out_shape=jax.ShapeDtypeStruct((T, 2 * D_FF), BF16),
        compiler_params=_cp("parallel"),
    )(dout_b, wd, gu, gu)


def _ffn_in_bwd_call(dgu, w, h, g_ffn, dout, tm=512):
    T = h.shape[0]

    def body(dgu_ref, w_ref, h_ref, g_ref, d_ref, dh_ref, dhb_ref, dg_ref):
        dhn = _dot(dgu_ref[...], w_ref[...])
        dx, dgt = _rms_bwd(dhn, h_ref[...], g_ref[...])
        dh = d_ref[...] + dx
        dh_ref[...] = dh
        dhb_ref[...] = dh.astype(BF16)

        @pl.when(pl.program_id(0) == 0)
        def _():
            dg_ref[...] = jnp.zeros_like(dg_ref)

        dg_ref[...] += jnp.sum(dgt, axis=0, keepdims=True)

    row = _rows(tm, D_MODEL)
    return pl.pallas_call(
        body, name="ffn_in_bwd", grid=(T // tm,),
        in_specs=[_rows(tm, 2 * D_FF), _full(w.shape), row, _full((1, D_MODEL)), row],
        out_specs=[row, row, _full((1, D_MODEL))],
        out_shape=[jax.ShapeDtypeStruct((T, D_MODEL), F32), jax.ShapeDtypeStruct((T, D_MODEL), BF16),
                   jax.ShapeDtypeStruct((1, D_MODEL), F32)],
        compiler_params=_cp("arbitrary"),
    )(dgu, w, h, g_ffn, dout)


def _merge_bwd_call(dh_b, wm, wc, wa, u, yc, ya, tm=512):
    T = dh_b.shape[0]

    def body(dh_ref, wm_ref, wc_ref, wa_ref, gc_ref, ga_ref, yc_ref, ya_ref, dyc_ref, dya_ref, dug_ref, dcp_ref, do_ref):
        dz = _dot_nt(dh_ref[...], wm_ref[...])
        sgc = _sigmoid(gc_ref[...].astype(F32))
        sga = _sigmoid(ga_ref[...].astype(F32))
        dyc = (dz * sgc).astype(BF16)
        dya = (dz * sga).astype(BF16)
        dyc_ref[...] = dyc
        dya_ref[...] = dya
        dug_ref[:, :D_MODEL] = (dz * yc_ref[...].astype(F32) * (sgc * (1.0 - sgc))).astype(BF16)
        dug_ref[:, D_MODEL:] = (dz * ya_ref[...].astype(F32) * (sga * (1.0 - sga))).astype(BF16)
        dcp_ref[...] = _dot_nt(dyc, wc_ref[...])
        do_ref[...] = _dot_nt(dya, wa_ref[...]).astype(BF16)

    sq = _full((D_MODEL, D_MODEL))
    row = _rows(tm, D_MODEL)
    b16 = jax.ShapeDtypeStruct((T, D_MODEL), BF16)
    return pl.pallas_call(
        body, name="merge_bwd", grid=(T // tm,),
        in_specs=[row, sq, sq, sq, _rows(tm, D_MODEL, COL_GC // D_MODEL), _rows(tm, D_MODEL, COL_GA // D_MODEL), row, row],
        out_specs=[row, row, _rows(tm, 2 * D_MODEL), row, row],
        out_shape=[b16, b16, jax.ShapeDtypeStruct((T, 2 * D_MODEL), BF16), jax.ShapeDtypeStruct((T, D_MODEL), F32), b16],
        compiler_params=_cp("parallel"),
    )(dh_b, wm, wc, wa, u, u, yc, ya)


def _conv_bwd_call(dcp, hc, u, conv_w, ln_g, ln_b, B, S, ts=256):
    T = B * S
    nS = S // ts
    per32 = ts // HALO
    last32 = T // HALO - 1

    def ln_bwd(dcp_v, hc_v, g, bb):
        mu = jnp.mean(hc_v, axis=-1, keepdims=True)
        xc = hc_v - mu
        rstd = lax.rsqrt(jnp.mean(xc * xc, axis=-1, keepdims=True) + EPS)
        xhat = xc * rstd
        y = xhat * g + bb
        sy = _sigmoid(y)
        dy = dcp_v * (sy * (1.0 + y * (1.0 - sy)))
        dxh = dy * g
        dhc = rstd * (dxh - jnp.mean(dxh, axis=-1, keepdims=True) - xhat * jnp.mean(dxh * xhat, axis=-1, keepdims=True))
        return dhc, dy * xhat, dy

    def body(dcp_ref, dcpn_ref, hc_ref, hcn_ref, a_ref, gate_ref, w_ref, g_ref, bb_ref,
             du_ref, dw_ref, db_ref, dlg_ref, dlb_ref, dhext, glu_scr, dhsh, acc_ref):
        b = pl.program_id(0)
        s = pl.program_id(1)

        @pl.when((b == 0) & (s == 0))
        def _():
            dw_ref[...] = jnp.zeros_like(dw_ref)
            db_ref[...] = jnp.zeros_like(db_ref)
            dlg_ref[...] = jnp.zeros_like(dlg_ref)
            dlb_ref[...] = jnp.zeros_like(dlb_ref)

        dhc, dgt, dbt = ln_bwd(dcp_ref[...], hc_ref[...], g_ref[...], bb_ref[...])
        dhn, _, _ = ln_bwd(dcpn_ref[...], hcn_ref[...], g_ref[...], bb_ref[...])
        dhext[0:ts, :] = dhc
        dhext[ts:, :] = jnp.where(s < nS - 1, dhn, 0.0)
        db_ref[...] += jnp.sum(dhc, axis=0, keepdims=True)
        dlg_ref[...] += jnp.sum(dgt, axis=0, keepdims=True)
        dlb_ref[...] += jnp.sum(dbt, axis=0, keepdims=True)

        sg = _sigmoid(gate_ref[...].astype(F32))
        av = a_ref[...].astype(F32)
        glu_scr[...] = av * sg
        _fill_shifted(dhext, dhsh)

        def colblock(cb, carry):
            cols = pl.ds(pl.multiple_of(cb * 128, 128), 128)
            outs = _conv_taps(w_ref, dhext, dhsh, 0, ts, cb, reverse=True)
            for rb, acc in enumerate(outs):
                acc_ref[rb * 128:(rb + 1) * 128, cols] = acc
            taps = [jnp.zeros((8, 128), F32) for _ in range(CONV_WIDTH)]
            for r0 in range(0, ts, 128):
                glu = [glu_scr[r0 + 8 * i:r0 + 8 * (i + 1), cols] for i in range(16)]
                for s8 in range(8):
                    users = {}
                    for j in range(CONV_WIDTH):
                        if (CONV_WIDTH - 1 - j) % 8 == s8:
                            for i in range(16):
                                users.setdefault(CONV_WIDTH - 1 - j + r0 + 8 * i - s8, []).append((j, i))
                    for row, pairs in sorted(users.items()):
                        dhc8 = dhext[row:row + 8, cols] if s8 == 0 else dhsh[s8 - 1, row:row + 8, cols]
                        for j, i in pairs:
                            taps[j] = taps[j] + glu[i] * dhc8
            for j in range(CONV_WIDTH):
                dw_ref[8 * j:8 * (j + 1), cols] += taps[j]
            return carry

        lax.fori_loop(0, C_CONV // 128, colblock, 0)
        dglu = acc_ref[...]
        du_ref[:, :C_CONV] = (dglu * sg).astype(BF16)
        du_ref[:, C_CONV:] = (dglu * av * (sg * (1.0 - sg))).astype(BF16)

    cur = lambda cb: pl.BlockSpec((ts, C_CONV), lambda b, s: (b * nS + s, cb))
    nxt = pl.BlockSpec((HALO, C_CONV), lambda b, s: (jnp.minimum(b * (S // HALO) + (s + 1) * per32, last32), 0))
    vec = _full((1, C_CONV))
    return pl.pallas_call(
        body, name="conv_bwd", grid=(B, nS),
        in_specs=[cur(0), nxt, cur(0), nxt, cur(0), cur(1), _full((CONV_WIDTH, C_CONV)), vec, vec],
        out_specs=[pl.BlockSpec((ts, 2 * C_CONV), lambda b, s: (b * nS + s, 0)), _full((8 * CONV_WIDTH, C_CONV)), vec, vec, vec],
        out_shape=[jax.ShapeDtypeStruct((T, 2 * C_CONV), BF16), jax.ShapeDtypeStruct((8 * CONV_WIDTH, C_CONV), F32),
                   jax.ShapeDtypeStruct((1, C_CONV), F32), jax.ShapeDtypeStruct((1, C_CONV), F32),
                   jax.ShapeDtypeStruct((1, C_CONV), F32)],
        scratch_shapes=[pltpu.VMEM((ts + HALO, C_CONV), F32), pltpu.VMEM((ts, C_CONV), F32),
                        pltpu.VMEM((7, ts + HALO, C_CONV), F32), pltpu.VMEM((ts, C_CONV), F32)],
        compiler_params=_cp("arbitrary", "arbitrary"),
    )(dcp, dcp, hc, hc, u, u, conv_w, ln_g, ln_b)


def _attn_bwd_call(u, do, qg2, kg2, sinks, slopes, B, S):
    T = B * S
    nb = S // BLOCK
    half = Q_W // 2
    npair = GROUP // 2

    def body(sinks_ref, slopes_ref, q_ref, k_ref, v_ref, do_ref, qg_ref, kg_ref,
             dq_ref, dkv_ref, dqg_ref, dkg_ref, dsk_ref, kA, kB, vA, vB, dkn_acc, dv_acc, dkv_keep,
             btab, s_buf, dp_buf, p_buf, ds_buf, qn_buf, do_buf, rq_buf):
        b = pl.program_id(0)
        kvh = pl.program_id(1)
        lo = _lane_lo()
        lane = lax.broadcasted_iota(jnp.int32, (1, 128), 1)
        first = kvh == 0

        @pl.when((b == 0) & first)
        def _():
            dqg_ref[...] = jnp.zeros_like(dqg_ref)
            dkg_ref[...] = jnp.zeros_like(dkg_ref)
            dsk_ref[...] = jnp.zeros_like(dsk_ref)

        _stage_kv(kvh, k_ref, v_ref, kg_ref, lo, kA, kB, vA, vB)
        dkn_acc[...] = jnp.zeros_like(dkn_acc)
        dv_acc[...] = jnp.zeros_like(dv_acc)
        _fill_bias(btab, kvh, slopes_ref)
        qgs = qg_ref[...] * ATTN_SCALE

        def fold(acc):
            canon = jnp.where(lo, acc[0:2 * BLOCK] + pltpu.roll(acc[2 * BLOCK:], HEAD_DIM, 1), 0.0)
            return jnp.where(first, canon, pltpu.roll(canon, HEAD_DIM, 1))

        def block(n, slot):
            s_scr, dp_scr, p_scr, ds_scr = s_buf.at[slot], dp_buf.at[slot], p_buf.at[slot], ds_buf.at[slot]
            qn_scr, do_scr, rq_scr = qn_buf.at[slot], do_buf.at[slot], rq_buf.at[slot]
            r0 = pl.multiple_of(n * BLOCK, BLOCK)
            win = pl.ds(r0, 2 * BLOCK)
            rows = pl.ds(r0, BLOCK)
            k2t = jnp.concatenate([kA[win, :], kB[win, :]], axis=0)
            v2 = jnp.concatenate([vA[win, :], vB[win, :]], axis=0)
            var = jnp.minimum(n, 1)
            for p in range(npair):
                pc = slice(128 * p, 128 * (p + 1))
                qp = q_ref[rows, pc].astype(F32)
                rq = _head_rstd(qp, lo)
                rq_scr[pc, :] = rq
                qn_scr[pc, :] = (qp * rq * qgs).astype(BF16)
                do_scr[pc, :] = do_ref[rows, pc]
            s_scr[...] = _dot_nt(k2t, qn_scr[...])
            dp_scr[...] = _dot_nt(v2, do_scr[...])
            for p in range(npair):
                for hf in range(2):
                    g = 2 * p + hf
                    keys, qs = slice(2 * BLOCK * hf, 2 * BLOCK * (hf + 1)), slice(128 * p, 128 * (p + 1))
                    pn, esk, rz = _softmax_cols(s_scr, btab, p, var, g, hf, sinks_ref[kvh * GROUP + g])
                    dph = dp_scr[keys, qs]
                    delta = jnp.sum(pn * dph, axis=0, keepdims=True)
                    ds_scr[keys, qs] = (pn * (dph - delta)).astype(BF16)
                    p_scr[keys, qs] = pn.astype(BF16)
                    dsk_ref[...] += jnp.where(lane == kvh * GROUP + g, -jnp.sum(esk * rz * delta), 0.0)
            dqs_all = _dot_tn(k2t, ds_scr[...]).T
            for p in range(npair):
                pc = slice(128 * p, 128 * (p + 1))
                qp = q_ref[rows, pc].astype(F32)
                rq = rq_scr[pc, :]
                dqs = dqs_all[pc, :]
                t = dqs * qgs
                dq = rq * t - qp * (rq * rq * rq * (_half_sums(t * qp, lo) * (1.0 / HEAD_DIM)))
                dq_ref[rows, pc] = dq.astype(BF16)
                dqg_ref[...] += jnp.sum(dqs * qp * rq, axis=0, keepdims=True) * ATTN_SCALE
            dkn_acc[win, :] += fold(_dot(ds_scr[...], qn_scr[...]))
            dv_acc[win, :] += fold(_dot(p_scr[...], do_scr[...]))

        def two_blocks(i, carry):
            block(2 * i, 0)
            block(2 * i + 1, 1)
            return carry

        lax.fori_loop(0, nb // 2, two_blocks, 0)

        dkn = dkn_acc[BLOCK:, :]
        kv = k_ref[...].astype(F32)
        kg = kg_ref[...]
        rk = _head_rstd(kv, lo)
        t = dkn * kg
        dk = rk * t - kv * (rk * rk * rk * (_half_sums(t * kv, lo) * (1.0 / HEAD_DIM)))
        dkg_ref[...] += jnp.sum(dkn * kv * rk, axis=0, keepdims=True)
        dv = dv_acc[BLOCK:, :]

        @pl.when(first)
        def _():
            dkv_keep[:, :128] = dk
            dkv_keep[:, 128:] = dv

        @pl.when(kvh == N_KV_HEADS - 1)
        def _():
            dkv_ref[:, :128] = (dkv_keep[:, :128] + dk).astype(BF16)
            dkv_ref[:, 128:] = (dkv_keep[:, 128:] + dv).astype(BF16)

    smem = pl.BlockSpec(memory_space=pltpu.SMEM)
    kvbuf = pltpu.VMEM((S + BLOCK, 128), BF16)
    accbuf = pltpu.VMEM((S + BLOCK, 128), F32)
    vec = _full((1, 128))
    return pl.pallas_call(
        body, name="attn_bwd", grid=(B, N_KV_HEADS),
        in_specs=[smem, smem,
                  pl.BlockSpec((S, half), lambda b, h: (b, COL_Q // half + h)),
                  pl.BlockSpec((S, 128), lambda b, h: (b, COL_K // 128)),
                  pl.BlockSpec((S, 128), lambda b, h: (b, COL_V // 128)),
                  pl.BlockSpec((S, half), lambda b, h: (b, h)),
                  vec, vec],
        out_specs=[pl.BlockSpec((S, half), lambda b, h: (b, h)), pl.BlockSpec((S, 256), lambda b, h: (b, 0)), vec, vec, vec],
        out_shape=[jax.ShapeDtypeStruct((T, Q_W), BF16), jax.ShapeDtypeStruct((T, 2 * KV_W), BF16),
                   jax.ShapeDtypeStruct((1, 128), F32), jax.ShapeDtypeStruct((1, 128), F32), jax.ShapeDtypeStruct((1, 128), F32)],
        scratch_shapes=[kvbuf, kvbuf, kvbuf, kvbuf, accbuf, accbuf, pltpu.VMEM((S, 256), F32),
                        pltpu.VMEM((2, GROUP, 2 * BLOCK, BLOCK), F32),
                        pltpu.VMEM((2, npair * BLOCK, 4 * BLOCK), F32), pltpu.VMEM((2, npair * BLOCK, 4 * BLOCK), F32),
                        pltpu.VMEM((2, npair * BLOCK, 4 * BLOCK), BF16), pltpu.VMEM((2, npair * BLOCK, 4 * BLOCK), BF16),
                        pltpu.VMEM((2, npair * BLOCK, 128), BF16), pltpu.VMEM((2, npair * BLOCK, 128), BF16),
                        pltpu.VMEM((2, npair * BLOCK, 128), F32)],
        compiler_params=_cp("arbitrary", "arbitrary"),
    )(sinks, slopes, u, u, u, do, qg2, kg2)


def _inproj_bwd_call(du_conv, du_g, du_q, du_kv, w, x, g, dh, tm=512):
    T = x.shape[0]

    def body(dc_ref, dg_ref, dq_ref, dkv_ref, w_ref, x_ref, g_ref, dh_ref, gx_ref, dgm_ref):
        dxn = _dot(dc_ref[...], w_ref[COL_A:COL_GC, :])
        dxn += _dot(dg_ref[...], w_ref[COL_GC:COL_Q, :])
        dxn += _dot(dq_ref[...], w_ref[COL_Q:COL_K, :])
        dxn += _dot(dkv_ref[...], w_ref[COL_K:, :])
        dx, dgt = _rms_bwd(dxn, x_ref[...], g_ref[...])
        gx_ref[...] = dh_ref[...] + dx

        @pl.when(pl.program_id(0) == 0)
        def _():
            dgm_ref[...] = jnp.zeros_like(dgm_ref)

        dgm_ref[...] += jnp.sum(dgt, axis=0, keepdims=True)

    row = _rows(tm, D_MODEL)
    return pl.pallas_call(
        body, name="inproj_bwd", grid=(T // tm,),
        in_specs=[_rows(tm, 2 * C_CONV), _rows(tm, 2 * D_MODEL), _rows(tm, Q_W), _rows(tm, 2 * KV_W), _full(w.shape),
                  row, _full((1, D_MODEL)), row],
        out_specs=[row, _full((1, D_MODEL))],
        out_shape=[jax.ShapeDtypeStruct((T, D_MODEL), F32), jax.ShapeDtypeStruct((1, D_MODEL), F32)],
        compiler_params=_cp("arbitrary"),
    )(du_conv, du_g, du_q, du_kv, w, x, g, dh)


def _adamw_math(w, g, m, v):
    m = ADAM_B1 * m + (1.0 - ADAM_B1) * g
    v = ADAM_B2 * v + (1.0 - ADAM_B2) * (g * g)
    m_hat = m / (1.0 - ADAM_B1 ** ADAM_STEP)
    v_hat = v / (1.0 - ADAM_B2 ** ADAM_STEP)
    delta = -ADAM_LR * (m_hat / (jnp.sqrt(v_hat) + ADAM_EPS) + ADAM_WD * w)
    return delta, m, v


def _adamw_sum_call(name, parts, own, w, m, v, tr):
    R, C = w.shape

    def body(p_ref, own_ref, w_ref, m_ref, v_ref, g_ref, d_ref, nm_ref, nv_ref):
        x, y, c = _position()
        me = 4 * x + 2 * y + c
        g = jnp.zeros((tr, C), F32)
        for s in range(N_DEV):
            g = g + jnp.where(me == s, own_ref[...], p_ref[s]).astype(F32)
        g_ref[...] = g
        d_ref[...], nm_ref[...], nv_ref[...] = _adamw_math(w_ref[...], g, m_ref[...], v_ref[...])

    blk = pl.BlockSpec((tr, C), lambda i: (i, 0))
    out = jax.ShapeDtypeStruct((R, C), F32)
    return pl.pallas_call(
        body, name=name, grid=(R // tr,),
        in_specs=[pl.BlockSpec((N_DEV, tr, C), lambda i: (0, i, 0)), blk, blk, blk, blk],
        out_specs=[blk] * 4, out_shape=[out] * 4,
        compiler_params=_cp("parallel"),
    )(parts, own, w, m, v)


def _adamw_small_call(name, g, w, m, v):
    def body(g_ref, w_ref, m_ref, v_ref, d_ref, nm_ref, nv_ref):
        d_ref[...], nm_ref[...], nv_ref[...] = _adamw_math(w_ref[...], g_ref[...], m_ref[...], v_ref[...])

    out = jax.ShapeDtypeStruct(w.shape, F32)
    return pl.pallas_call(body, name=name, out_shape=[out] * 3)(g, w, m, v)


def _position():
    return lax.axis_index("x"), lax.axis_index("y"), lax.axis_index("c")


def _peer(x, y, c, rel):
    fx, fy, fc = (rel >> 2) & 1, (rel >> 1) & 1, rel & 1
    px = 1 - x if fx else x
    py = 1 - y if fy else y
    pc = 1 - c if fc else c
    return (px, py, pc), 4 * px + 2 * py + pc


def _gather_weights_call(shards):
    n = len(shards)

    def body(*refs):
        ins, outs = refs[:n], refs[n:2 * n]
        send_sems, recv_sems, local_sems = refs[2 * n:]
        x, y, c = _position()
        me = 4 * x + 2 * y + c
        sib, sib_id = _peer(x, y, c, 1)
        chips = [_peer(x, y, c, rel) for rel in (4, 2, 6)]

        def copy(i, k, src, slot, to):
            return pltpu.make_async_remote_copy(
                src_ref=src, dst_ref=outs[i].at[slot], send_sem=send_sems.at[i, k], recv_sem=recv_sems.at[i, k],
                device_id=to, device_id_type=MESH)

        mine = [pltpu.make_async_copy(ins[i], outs[i].at[me], local_sems.at[i]) for i in range(n)]
        for cp in mine:
            cp.start()
        first = []
        for i in range(n):
            first.append(copy(i, 0, ins[i], me, sib))
            first += [copy(i, 1 + j, ins[i], me, dev) for j, (dev, _) in enumerate(chips)]
        for cp in first:
            cp.start()
        passed = []
        for i in range(n):
            for j, (dev, slot) in enumerate(chips):
                copy(i, 1 + j, ins[i], slot, dev).wait_recv()
                fwd = copy(i, 4 + j, outs[i].at[slot], slot, sib)
                fwd.start()
                passed.append(fwd)
        for i in range(n):
            copy(i, 0, ins[i], sib_id, sib).wait_recv()
            for j, (_, slot) in enumerate(chips):
                copy(i, 4 + j, ins[i], slot ^ 1, sib).wait_recv()
        for cp in first + passed:
            cp.wait_send()
        for cp in mine:
            cp.wait()

    hbm = pl.BlockSpec(memory_space=pl.ANY)
    return pl.pallas_call(
        body, name="gather_weights",
        in_specs=[hbm] * n, out_specs=[hbm] * n,
        out_shape=[jax.ShapeDtypeStruct((N_DEV,) + s.shape, s.dtype) for s in shards],
        scratch_shapes=[pltpu.SemaphoreType.DMA((n, 7)), pltpu.SemaphoreType.DMA((n, 7)), pltpu.SemaphoreType.DMA((n,))],
    )(*shards)


_HBM = pl.BlockSpec(memory_space=pltpu.HBM)
_SEM = pl.BlockSpec(memory_space=pltpu.SEMAPHORE)
_EFFECT = pltpu.SideEffectType.DATAFLOW_SIDE_EFFECTING


def _exchange_copies(scatter, incoming, src_refs, land_refs, send_sems, recv_sems):
    x, y, c = _position()
    me = 4 * x + 2 * y + c
    out = []
    for i, (src, land) in enumerate(zip(src_refs, land_refs)):
        for k in range(N_DEV - 1):
            dev, dev_id = _peer(x, y, c, k + 1)
            j = i * (N_DEV - 1) + k
            out.append(pltpu.make_async_remote_copy(
                src_ref=src.at[dev_id] if scatter else src, dst_ref=land.at[dev_id if incoming else me],
                send_sem=send_sems.at[j], recv_sem=recv_sems.at[j], device_id=dev, device_id_type=MESH))
    return out


def _exchange_start_call(name, arrays, scatter):
    n = len(arrays)
    lands = [lax.empty((N_DEV,) + (a.shape[1:] if scatter else a.shape), a.dtype) for a in arrays]

    def body(*refs):
        src_refs, land_refs = refs[:n], refs[n:2 * n]
        send_sems, recv_sems, token = refs[2 * n], refs[2 * n + 1], refs[-1]
        for copy in _exchange_copies(scatter, False, src_refs, land_refs, send_sems, recv_sems):
            copy.start()
        token[...] = jnp.zeros_like(token)

    sems = pltpu.SemaphoreType.DMA((n * (N_DEV - 1),))
    res = pl.pallas_call(
        body, name=name,
        out_shape=[sems, sems] + [pltpu.HBM(a.shape, a.dtype) for a in arrays] + [pltpu.HBM(l.shape, l.dtype) for l in lands]
        + [jax.ShapeDtypeStruct((8, 128), F32)],
        in_specs=[_HBM] * (2 * n), out_specs=[_SEM, _SEM] + [_HBM] * (2 * n) + [pl.BlockSpec(memory_space=pltpu.VMEM)],
        input_output_aliases={i: 2 + i for i in range(2 * n)},
        compiler_params=pltpu.CompilerParams(has_side_effects=_EFFECT),
    )(*[pltpu.with_memory_space_constraint(a, pltpu.HBM) for a in list(arrays) + lands])
    return res[0], res[1], res[2:2 + n], res[2 + n:2 + 2 * n], res[-1]


def _exchange_wait_call(name, started, scatter, after):
    send_sems, recv_sems, srcs, lands, _ = started
    n = len(srcs)

    def body(*refs):
        src_refs, land_refs = refs[:n], refs[n:2 * n]
        send_s, recv_s = refs[2 * n], refs[2 * n + 1]
        for copy in _exchange_copies(scatter, False, src_refs, land_refs, send_s, recv_s):
            copy.wait_send()
        for copy in _exchange_copies(scatter, True, src_refs, land_refs, send_s, recv_s):
            copy.wait_recv()

    res = pl.pallas_call(
        body, name=name,
        out_shape=[pltpu.HBM(a.shape, a.dtype) for a in list(srcs) + list(lands)],
        in_specs=[_HBM] * (2 * n) + [_SEM, _SEM, pl.BlockSpec(memory_space=pl.ANY)], out_specs=[_HBM] * (2 * n),
        input_output_aliases={i: i for i in range(2 * n)},
        compiler_params=pltpu.CompilerParams(has_side_effects=_EFFECT),
    )(*srcs, *lands, send_sems, recv_sems, after)
    return res[n:]


def _allsum_small_call(part, blocks):
    P, Q = part.shape[0], blocks.shape[1]

    def body(p_ref, b_ref, o_ref, ob_ref, all_ref, allb_ref, send_sems, recv_sems):
        x, y, c = _position()
        me = 4 * x + 2 * y + c
        peers = [_peer(x, y, c, rel) for rel in range(1, N_DEV)]
        all_ref[me] = p_ref[...]
        allb_ref[me] = b_ref[me]

        def copies(k, slot, dev, dev_id):
            sems = lambda i: dict(send_sem=send_sems.at[i], recv_sem=recv_sems.at[i], device_id=dev, device_id_type=MESH)
            return (pltpu.make_async_remote_copy(src_ref=p_ref, dst_ref=all_ref.at[slot], **sems(k)),
                    pltpu.make_async_remote_copy(src_ref=b_ref.at[dev_id], dst_ref=allb_ref.at[slot], **sems(7 + k)))

        sends = [cp for k, (dev, dev_id) in enumerate(peers) for cp in copies(k, me, dev, dev_id)]
        for cp in sends:
            cp.start()
        for k, (dev, dev_id) in enumerate(peers):
            for cp in copies(k, dev_id, dev, dev_id):
                cp.wait_recv()
        for cp in sends:
            cp.wait_send()
        tot, totb = all_ref[0], allb_ref[0]
        for s in range(1, N_DEV):
            tot, totb = tot + all_ref[s], totb + allb_ref[s]
        o_ref[...] = tot
        ob_ref[...] = totb

    vm = pl.BlockSpec(memory_space=pltpu.VMEM)
    return pl.pallas_call(
        body, name="allsum_small", in_specs=[vm, vm], out_specs=[vm, vm],
        out_shape=[jax.ShapeDtypeStruct((P, 128), F32), jax.ShapeDtypeStruct((Q, 128), F32)],
        scratch_shapes=[pltpu.VMEM((N_DEV, P, 128), F32), pltpu.VMEM((N_DEV, Q, 128), F32),
                        pltpu.SemaphoreType.DMA((14,)), pltpu.SemaphoreType.DMA((14,))],
    )(part, blocks)


def _pad_rows(v, rows):
    flat = v.reshape(-1)
    return jnp.pad(flat, (0, rows * 128 - flat.shape[0])).reshape(rows, 128)


def _local_step(x, tgt, w_in_i, conv_w, conv_b, ln_g, ln_b, qg, kg, sinks, g_mix, g_ffn, token0, rest_weights, grads_ready, B, S):
    row = lambda v: v.reshape(1, -1)
    qg2 = jnp.tile(qg, 2).reshape(1, 128)
    kg2 = jnp.tile(kg, 2).reshape(1, 128)
    heads = jnp.arange(1, N_Q_HEADS + 1, dtype=F32)
    slopes = jnp.exp2(-8.0 * heads / N_Q_HEADS)

    u, xn = _inproj_call(x, row(g_mix) + token0, w_in_i)
    hc, c = _conv_fwd_call(u, conv_w, row(conv_b), row(ln_g), row(ln_b), B, S)
    o = _attn_fwd_call(u, qg2, kg2, sinks, slopes, B, S)
    wc, wa, wm, w_ffn_in, wd = rest_weights(o)
    yc, ya, z, h, hn = _merge_call(c, o, u, x, wc, wa, wm, row(g_ffn))
    gu, act = _ffn_in_call(hn, w_ffn_in)
    sse, dout, dout_b = _ffn_out_call(act, wd, h, tgt)

    dgu = _ffn_down_bwd_call(dout_b, wd, gu)
    g_wd = _dw_call("dw_ffn_down", act, dout_b, tr=D_FF // 2)
    g_wffn_t = _dw_call("dw_ffn_in", dgu, hn, tr=2 * D_FF // 4)
    tok = grads_ready(dict(w_ffn_down=g_wd, w_ffn_in=g_wffn_t))
    dh, dh_b, g_gffn = _ffn_in_bwd_call(dgu, w_ffn_in, h, row(g_ffn) + tok, dout)
    dyc, dya, du_g, dcp, do = _merge_bwd_call(dh_b, wm, wc, wa, u, yc, ya)
    g_wm = _dw_call("dw_merge", z, dh_b, tr=D_MODEL)
    g_wc = _dw_call("dw_conv_out", c, dyc, tr=D_MODEL)
    g_wa = _dw_call("dw_attn_out", o, dya, tr=D_MODEL)
    tok = grads_ready(dict(w_merge_out=g_wm, w_conv_out=g_wc, w_attn_out=g_wa))
    du_conv, g_cw8, g_cb, g_lg, g_lb = _conv_bwd_call(dcp, hc, u, conv_w, row(ln_g) + tok, row(ln_b), B, S)
    du_q, du_kv, g_qg2, g_kg2, g_sk = _attn_bwd_call(u, do, qg2, kg2, sinks, slopes, B, S)
    g_win_t = _concat_rows_call("dw_in_rows", [
        _dw_call("dw_in_conv", du_conv, xn, tr=D_MODEL),
        _dw_call("dw_in_q", du_q, xn, tr=D_MODEL),
        _dw_call("dw_in_kv", du_kv, xn, tr=2 * KV_W),
        _dw_call("dw_in_gates", du_g, xn, tr=D_MODEL)])
    tok = grads_ready(dict(w_in=g_win_t))
    grad_x, g_gmix = _inproj_bwd_call(du_conv, du_g, du_q, du_kv, w_in_i, x, row(g_mix) + tok, dh)

    g_cw = g_cw8.reshape(CONV_WIDTH, 8, C_CONV).sum(axis=1)
    g_qg = g_qg2[0, :HEAD_DIM] + g_qg2[0, HEAD_DIM:]
    g_kg = g_kg2[0, :HEAD_DIM] + g_kg2[0, HEAD_DIM:]
    small = dict(norm_mix_g=g_gmix[0], conv_dw_b=g_cb[0], conv_ln_g=g_lg[0], conv_ln_b=g_lb[0], q_norm_g=g_qg,
                 k_norm_g=g_kg, sinks=g_sk[0, :N_Q_HEADS], norm_ffn_g=g_gffn[0], conv_dw_w=g_cw)
    return sse[0, 0], grad_x, small


def _to_internal_rows(wt):
    o1 = 2 * C_CONV
    o4 = o1 + Q_W + 2 * KV_W
    return jnp.concatenate([wt[:o1], wt[o4:], wt[o1:o4]], axis=0)


_COL_SHARDED = ("w_in", "w_ffn_in")

_SMALL = (("norm_mix_g", 8), ("conv_dw_b", 8), ("conv_ln_g", 8), ("conv_ln_b", 8), ("norm_ffn_g", 8),
          ("q_norm_g", 1), ("k_norm_g", 1), ("sinks", 1))
_SMALL_ROWS = 48


def _pack_small(d):
    rows = [_pad_rows(d[name], r) for name, r in _SMALL]
    used = sum(r for _, r in _SMALL)
    return jnp.concatenate(rows + [jnp.zeros((_SMALL_ROWS - used, 128), F32)], axis=0)


def _unpack_small(packed, like):
    out, r0 = {}, 0
    for name, r in _SMALL:
        n = like[name].size
        out[name] = packed[r0:r0 + r].reshape(-1)[:n].reshape(like[name].shape)
        r0 += r
    return out


def kernel(x, norm_mix_g, w_in, conv_dw_w, conv_dw_b, conv_ln_g, conv_ln_b, w_conv_out, q_norm_g, k_norm_g, sinks, w_attn_out, w_merge_out, norm_ffn_g, w_ffn_in, w_ffn_down, loss_target, m_norm_mix_g, m_w_in, m_conv_dw_w, m_conv_dw_b, m_conv_ln_g, m_conv_ln_b, m_w_conv_out, m_q_norm_g, m_k_norm_g, m_sinks, m_w_attn_out, m_w_merge_out, m_norm_ffn_g, m_w_ffn_in, m_w_ffn_down, v_norm_mix_g, v_w_in, v_conv_dw_w, v_conv_dw_b, v_conv_ln_g, v_conv_ln_b, v_w_conv_out, v_q_norm_g, v_k_norm_g, v_sinks, v_w_attn_out, v_w_merge_out, v_norm_ffn_g, v_w_ffn_in, v_w_ffn_down):
    names = ["norm_mix_g", "w_in", "conv_dw_w", "conv_dw_b", "conv_ln_g", "conv_ln_b", "w_conv_out", "q_norm_g",
             "k_norm_g", "sinks", "w_attn_out", "w_merge_out", "norm_ffn_g", "w_ffn_in", "w_ffn_down"]
    w = dict(zip(names, (norm_mix_g, w_in, conv_dw_w, conv_dw_b, conv_ln_g, conv_ln_b, w_conv_out, q_norm_g, k_norm_g,
                         sinks, w_attn_out, w_merge_out, norm_ffn_g, w_ffn_in, w_ffn_down)))
    m = dict(zip(names, (m_norm_mix_g, m_w_in, m_conv_dw_w, m_conv_dw_b, m_conv_ln_g, m_conv_ln_b, m_w_conv_out,
                         m_q_norm_g, m_k_norm_g, m_sinks, m_w_attn_out, m_w_merge_out, m_norm_ffn_g, m_w_ffn_in,
                         m_w_ffn_down)))
    v = dict(zip(names, (v_norm_mix_g, v_w_in, v_conv_dw_w, v_conv_dw_b, v_conv_ln_g, v_conv_ln_b, v_w_conv_out,
                         v_q_norm_g, v_k_norm_g, v_sinks, v_w_attn_out, v_w_merge_out, v_norm_ffn_g, v_w_ffn_in,
                         v_w_ffn_down)))
    B, S, _ = x.shape
    T = B * S
    me = 4 * lax.axis_index("x") + 2 * lax.axis_index("y") + lax.axis_index("c")

    as_rows = lambda k, t: t.T if k in _COL_SHARDED else t
    cw_shard = _pad_rows(conv_dw_w.reshape(CONV_WIDTH, 128), 32)
    g_win, g_cwt = _gather_weights_call([w_in.astype(BF16), cw_shard])
    conv_w_full = g_cwt.reshape(N_DEV, 32 * 128)[:, :CONV_WIDTH * 128].reshape(N_DEV, CONV_WIDTH, 128)
    conv_w_full = conv_w_full.transpose(1, 0, 2).reshape(CONV_WIDTH, C_CONV)
    w_in_i = _to_internal_rows(g_win.transpose(0, 2, 1).reshape(IN_COLS, D_MODEL))
    rest_names = ["w_conv_out", "w_attn_out", "w_merge_out", "w_ffn_in", "w_ffn_down"]
    rest_shards = [as_rows(k, w[k]).astype(BF16) for k in rest_names]
    rest_started = _exchange_start_call("gather_rest_start", rest_shards, scatter=False)

    def rest_weights(after):
        lands = _exchange_wait_call("gather_rest_wait", rest_started, False, after)
        mine = lax.broadcasted_iota(jnp.int32, (N_DEV, 1, 1), 0) == me
        full = [jnp.where(mine, s[None], l) for l, s in zip(lands, rest_shards)]
        return tuple(f.reshape(N_DEV * f.shape[1], D_MODEL) for f in full)

    exchanges = []

    def grads_ready(group):
        ks = list(group)
        blocks = [group[k].reshape(N_DEV, group[k].shape[0] // N_DEV, D_MODEL) for k in ks]
        started = _exchange_start_call("scatter_" + ks[0] + "_start", blocks, scatter=True)
        exchanges.append((ks, started, [lax.dynamic_index_in_dim(b, me, 0, keepdims=False) for b in blocks]))
        return started[-1][0:1, 0:1]

    sse, grad_x, small = _local_step(
        x.reshape(T, D_MODEL), loss_target.reshape(T, D_MODEL), w_in_i, conv_w_full, conv_dw_b, conv_ln_g, conv_ln_b,
        q_norm_g, k_norm_g, sinks, norm_mix_g, norm_ffn_g, rest_started[-1][0:1, 0:1], rest_weights, grads_ready, B, S)

    grad, delta, new_m, new_v = {}, {}, {}, {}
    packed = jnp.concatenate([_pack_small(small), jnp.full((8, 128), sse, F32)], axis=0)
    cw_blocks = small["conv_dw_w"].reshape(CONV_WIDTH, N_DEV, 128).transpose(1, 0, 2)
    tot, cw_tot = _allsum_small_call(packed, jnp.pad(cw_blocks, ((0, 0), (0, 32 - CONV_WIDTH), (0, 0))))
    loss = 0.5 / D_MODEL * tot[_SMALL_ROWS, 0]
    g_small = _unpack_small(tot[:_SMALL_ROWS], w)
    d_s, m_s, v_s = _adamw_small_call("adamw_small", tot[:_SMALL_ROWS], _pack_small(w), _pack_small(m), _pack_small(v))
    d_small, m_small, v_small = _unpack_small(d_s, w), _unpack_small(m_s, w), _unpack_small(v_s, w)
    g_cw = cw_tot[:CONV_WIDTH]
    cw2 = lambda t: t.reshape(CONV_WIDTH, 128)
    d_cw, m_cw, v_cw = _adamw_small_call("adamw_conv_w", g_cw, cw2(conv_dw_w), cw2(m_conv_dw_w), cw2(v_conv_dw_w))
    cw3 = lambda t: t.reshape(CONV_WIDTH, 1, 128)
    for k, _ in _SMALL:
        grad[k], delta[k], new_m[k], new_v[k] = g_small[k], d_small[k], m_small[k], v_small[k]
    grad["conv_dw_w"], delta["conv_dw_w"], new_m["conv_dw_w"], new_v["conv_dw_w"] = cw3(g_cw), cw3(d_cw), cw3(m_cw), cw3(v_cw)

    after = d_cw
    for ks, started, owns in exchanges:
        lands = _exchange_wait_call("scatter_" + ks[0] + "_wait", started, True, after)
        for k, land, own in zip(ks, lands, owns):
            tr = {672: 336, 704: 352, 352: 176, 128: 128}[own.shape[0]]
            outs = _adamw_sum_call("adamw_" + k, land, own, as_rows(k, w[k]), as_rows(k, m[k]), as_rows(k, v[k]), tr)
            grad[k], delta[k], new_m[k], new_v[k] = (as_rows(k, t) for t in outs)
            after = outs[3]

    return (loss, grad_x.reshape(B, S, D_MODEL), *[grad[k] for k in names], *[delta[k] for k in names],
            *[new_m[k] for k in names], *[new_v[k] for k in names])
```

```python
import functools
import math

import jax
import jax.numpy as jnp
from jax import lax
from jax.experimental import pallas as pl
from jax.experimental.pallas import tpu as pltpu

F32 = jnp.float32
BF16 = jnp.bfloat16
MESH = pl.DeviceIdType.MESH

N_DEV = 8
D_MODEL = 1024
C_CONV = 1024
CONV_WIDTH = 31
HEAD_DIM = 64
N_Q_HEADS = 16
N_KV_HEADS = 2
GROUP = N_Q_HEADS // N_KV_HEADS
WINDOW = 128
BLOCK = 128
D_FF = 2816
EPS = 1e-6
NEG = -1e30
ATTN_SCALE = 1.0 / math.sqrt(HEAD_DIM)
Q_W = N_Q_HEADS * HEAD_DIM
KV_W = N_KV_HEADS * HEAD_DIM
IN_COLS = 2 * C_CONV + Q_W + 2 * KV_W + 2 * D_MODEL

ADAM_LR = 0.001
ADAM_B1 = 0.9
ADAM_B2 = 0.999
ADAM_EPS = 1e-08
ADAM_WD = 0.01
ADAM_STEP = 10

COL_A, COL_GATE, COL_GC, COL_GA, COL_Q, COL_K, COL_V = 0, 1024, 2048, 3072, 4096, 5120, 5248

HALO = 32
VMEM_LIMIT = 56 * 1024 * 1024


def _cp(*sem):
    return pltpu.CompilerParams(dimension_semantics=sem, vmem_limit_bytes=VMEM_LIMIT)


def _sigmoid(x):
    return jax.nn.sigmoid(x)


def _dot(a, b):
    return jnp.dot(a, b, preferred_element_type=F32)


def _dot_nt(a, b):
    return lax.dot_general(a, b, (((1,), (1,)), ((), ())), preferred_element_type=F32)


def _dot_tn(a, b):
    return lax.dot_general(a, b, (((0,), (0,)), ((), ())), preferred_element_type=F32)


def _full(shape):
    return pl.BlockSpec(shape, lambda *_: (0,) * len(shape))


def _rows(tm, ncols, colblk=0):
    return pl.BlockSpec((tm, ncols), lambda i: (i, colblk))


def _inproj_call(x, g, w, tm=512):
    T = x.shape[0]
    N = w.shape[0]

    def body(x_ref, g_ref, w_ref, u_ref, xn_ref):
        xv = x_ref[...]
        r = lax.rsqrt(jnp.mean(xv * xv, axis=-1, keepdims=True) + EPS)
        xn = (xv * r * g_ref[...]).astype(BF16)
        xn_ref[...] = xn
        u_ref[...] = _dot_nt(xn, w_ref[...]).astype(BF16)

    return pl.pallas_call(
        body, name="inproj", grid=(T // tm,),
        in_specs=[_rows(tm, D_MODEL), _full((1, D_MODEL)), _full(w.shape)],
        out_specs=[_rows(tm, N), _rows(tm, D_MODEL)],
        out_shape=[jax.ShapeDtypeStruct((T, N), BF16), jax.ShapeDtypeStruct((T, D_MODEL), BF16)],
        compiler_params=_cp("parallel"),
    )(x, g, w)


def _fill_shifted(src_ref, sh_ref):
    n = src_ref.shape[0] - 8
    for s in range(1, 8):
        sh_ref[s - 1, 0:n, :] = src_ref[s:s + n, :]


def _window(src_ref, sh_ref, off, cols):
    s = off % 8
    if s == 0:
        return src_ref[off:off + 128, cols]
    return sh_ref[s - 1, off - s:off - s + 128, cols]


def _conv_taps(w_ref, src_ref, sh_ref, base, rows, cb, reverse):
    cols = pl.ds(pl.multiple_of(cb * 128, 128), 128)
    outs = []
    for rb in range(rows // 128):
        acc = jnp.zeros((128, 128), F32)
        for j in range(CONV_WIDTH):
            off = base + (CONV_WIDTH - 1 - j if reverse else j) + rb * 128
            acc = acc + w_ref[j:j + 1, cols] * _window(src_ref, sh_ref, off, cols)
        outs.append(acc)
    return outs


def _conv_fwd_call(u, conv_w, conv_b, ln_g, ln_b, B, S, ts=512):
    T = B * S
    nS = S // ts
    per32 = ts // HALO

    def body(a_ref, gate_ref, ap_ref, gp_ref, w_ref, b_ref, g_ref, bb_ref, hc_ref, c_ref, hext, hsh, acc_ref):
        s = pl.program_id(1)
        prev = ap_ref[...].astype(F32) * _sigmoid(gp_ref[...].astype(F32))
        hext[0:HALO, :] = jnp.where(s > 0, prev, 0.0)
        hext[HALO:, :] = a_ref[...].astype(F32) * _sigmoid(gate_ref[...].astype(F32))
        _fill_shifted(hext, hsh)

        def colblock(cb, carry):
            cols = pl.ds(pl.multiple_of(cb * 128, 128), 128)
            outs = _conv_taps(w_ref, hext, hsh, HALO - (CONV_WIDTH - 1), ts, cb, reverse=False)
            for rb, acc in enumerate(outs):
                acc_ref[rb * 128:(rb + 1) * 128, cols] = acc
            return carry

        lax.fori_loop(0, C_CONV // 128, colblock, 0)
        hc = acc_ref[...] + b_ref[...]
        hc_ref[...] = hc
        mu = jnp.mean(hc, axis=-1, keepdims=True)
        xc = hc - mu
        var = jnp.mean(xc * xc, axis=-1, keepdims=True)
        y = xc * lax.rsqrt(var + EPS) * g_ref[...] + bb_ref[...]
        c_ref[...] = (y * _sigmoid(y)).astype(BF16)

    cur = lambda cb: pl.BlockSpec((ts, C_CONV), lambda b, s: (b * nS + s, cb))
    prv = lambda cb: pl.BlockSpec((HALO, C_CONV), lambda b, s: (jnp.maximum(b * (S // HALO) + s * per32 - 1, 0), cb))
    vec = _full((1, C_CONV))
    return pl.pallas_call(
        body, name="conv_fwd", grid=(B, nS),
        in_specs=[cur(0), cur(1), prv(0), prv(1), _full((CONV_WIDTH, C_CONV)), vec, vec, vec],
        out_specs=[pl.BlockSpec((ts, C_CONV), lambda b, s: (b * nS + s, 0))] * 2,
        out_shape=[jax.ShapeDtypeStruct((T, C_CONV), F32), jax.ShapeDtypeStruct((T, C_CONV), BF16)],
        scratch_shapes=[pltpu.VMEM((HALO + ts, C_CONV), F32), pltpu.VMEM((7, HALO + ts, C_CONV), F32),
                        pltpu.VMEM((ts, C_CONV), F32)],
        compiler_params=_cp("parallel", "arbitrary"),
    )(u, u, u, u, conv_w, conv_b, ln_g, ln_b)


def _lane_lo():
    return lax.broadcasted_iota(jnp.int32, (1, 128), 1) < HEAD_DIM


def _half_sums(t, lo):
    s_lo = jnp.sum(jnp.where(lo, t, 0.0), axis=-1, keepdims=True)
    s_hi = jnp.sum(jnp.where(lo, 0.0, t), axis=-1, keepdims=True)
    return jnp.where(lo, s_lo, s_hi)


def _head_rstd(t, lo):
    return lax.rsqrt(_half_sums(t * t, lo) * (1.0 / HEAD_DIM) + EPS)


def _fill_bias(btab, kvh, slopes_ref):
    sj = lax.broadcasted_iota(jnp.int32, (2 * BLOCK, BLOCK), 0)
    qi = lax.broadcasted_iota(jnp.int32, (2 * BLOCK, BLOCK), 1)
    dist = qi + BLOCK - sj
    valid = (dist >= 0) & (dist < WINDOW)
    distf = dist.astype(F32)
    for g in range(GROUP):
        bias = jnp.where(valid, -slopes_ref[kvh * GROUP + g] * distf, NEG)
        btab[1, g] = bias
        btab[0, g] = jnp.where(sj >= BLOCK, bias, NEG)


def _softmax_cols(st_ref, btab, p, var, g, hf, sk):
    sh = st_ref[2 * BLOCK * hf:2 * BLOCK * (hf + 1), 128 * p:128 * (p + 1)] + btab[var, g]
    m = jnp.maximum(jnp.max(sh, axis=0, keepdims=True), sk)
    e = jnp.exp(sh - m)
    esk = jnp.exp(sk - m)
    rz = 1.0 / (jnp.sum(e, axis=0, keepdims=True) + esk)
    return e * rz, esk, rz


def _stage_kv(kvh, k_ref, v_ref, kg_ref, lo, kA, kB, vA, vB):
    kv = k_ref[...].astype(F32)
    rk = _head_rstd(kv, lo)
    kn = kv * rk * kg_ref[...]
    mine = lax.broadcasted_iota(jnp.int32, (1, 128), 1) // HEAD_DIM == kvh
    ksel = jnp.where(mine, kn, 0.0)
    vsel = jnp.where(mine, v_ref[...].astype(F32), 0.0)
    krol = pltpu.roll(ksel, HEAD_DIM, 1)
    vrol = pltpu.roll(vsel, HEAD_DIM, 1)
    first = kvh == 0
    zeros = jnp.zeros((BLOCK, 128), BF16)
    for ref, val in ((kA, jnp.where(first, ksel, krol)), (kB, jnp.where(first, krol, ksel)),
                     (vA, jnp.where(first, vsel, vrol)), (vB, jnp.where(first, vrol, vsel))):
        ref[0:BLOCK, :] = zeros
        ref[BLOCK:, :] = val.astype(BF16)


def _attn_fwd_call(u, qg2, kg2, sinks, slopes, B, S):
    T = B * S
    nb = S // BLOCK
    half = Q_W // 2
    npair = GROUP // 2

    def body(sinks_ref, slopes_ref, q_ref, k_ref, v_ref, qg_ref, kg_ref, o_ref, kA, kB, vA, vB, btab, qn_buf, s_buf, p_buf):
        kvh = pl.program_id(1)
        lo = _lane_lo()
        _stage_kv(kvh, k_ref, v_ref, kg_ref, lo, kA, kB, vA, vB)
        _fill_bias(btab, kvh, slopes_ref)
        qgs = qg_ref[...] * ATTN_SCALE

        def block(n, slot):
            qn_scr, s_scr, p_scr = qn_buf.at[slot], s_buf.at[slot], p_buf.at[slot]
            r0 = pl.multiple_of(n * BLOCK, BLOCK)
            win = pl.ds(r0, 2 * BLOCK)
            k2t = jnp.concatenate([kA[win, :], kB[win, :]], axis=0)
            v2 = jnp.concatenate([vA[win, :], vB[win, :]], axis=0)
            var = jnp.minimum(n, 1)
            for p in range(npair):
                pc = slice(128 * p, 128 * (p + 1))
                qp = q_ref[pl.ds(r0, BLOCK), pc].astype(F32)
                qn_scr[pc, :] = (qp * _head_rstd(qp, lo) * qgs).astype(BF16)
            s_scr[...] = _dot_nt(k2t, qn_scr[...])
            for p in range(npair):
                for hf in range(2):
                    g = 2 * p + hf
                    pn, _, _ = _softmax_cols(s_scr, btab, p, var, g, hf, sinks_ref[kvh * GROUP + g])
                    p_scr[2 * BLOCK * hf:2 * BLOCK * (hf + 1), 128 * p:128 * (p + 1)] = pn.astype(BF16)
            o_all = _dot_tn(v2, p_scr[...]).T
            for p in range(npair):
                pc = slice(128 * p, 128 * (p + 1))
                o_ref[pl.ds(r0, BLOCK), pc] = o_all[pc, :].astype(BF16)

        def four_blocks(i, carry):
            for slot in range(4):
                block(4 * i + slot, slot)
            return carry

        lax.fori_loop(0, nb // 4, four_blocks, 0)

    smem = pl.BlockSpec(memory_space=pltpu.SMEM)
    kvbuf = pltpu.VMEM((S + BLOCK, 128), BF16)
    pair_scores = (4, npair * BLOCK, 4 * BLOCK)
    return pl.pallas_call(
        body, name="attn_fwd", grid=(B, N_KV_HEADS),
        in_specs=[smem, smem,
                  pl.BlockSpec((S, half), lambda b, h: (b, COL_Q // half + h)),
                  pl.BlockSpec((S, 128), lambda b, h: (b, COL_K // 128)),
                  pl.BlockSpec((S, 128), lambda b, h: (b, COL_V // 128)),
                  _full((1, 128)), _full((1, 128))],
        out_specs=pl.BlockSpec((S, half), lambda b, h: (b, h)),
        out_shape=jax.ShapeDtypeStruct((T, Q_W), BF16),
        scratch_shapes=[kvbuf, kvbuf, kvbuf, kvbuf, pltpu.VMEM((2, GROUP, 2 * BLOCK, BLOCK), F32),
                        pltpu.VMEM((4, npair * BLOCK, 128), BF16), pltpu.VMEM(pair_scores, F32), pltpu.VMEM(pair_scores, BF16)],
        compiler_params=_cp("parallel", "arbitrary"),
    )(sinks, slopes, u, u, u, qg2, kg2)


def _merge_call(c, o, u, x, wc, wa, wm, g_ffn, tm=512):
    T = x.shape[0]

    def body(c_ref, o_ref, gc_ref, ga_ref, x_ref, wc_ref, wa_ref, wm_ref, g_ref, yc_ref, ya_ref, z_ref, h_ref, hn_ref):
        yc = _dot(c_ref[...], wc_ref[...])
        ya = _dot(o_ref[...], wa_ref[...])
        yc_ref[...] = yc.astype(BF16)
        ya_ref[...] = ya.astype(BF16)
        z = (_sigmoid(gc_ref[...].astype(F32)) * yc + _sigmoid(ga_ref[...].astype(F32)) * ya).astype(BF16)
        z_ref[...] = z
        h = x_ref[...] + _dot(z, wm_ref[...])
        h_ref[...] = h
        r = lax.rsqrt(jnp.mean(h * h, axis=-1, keepdims=True) + EPS)
        hn_ref[...] = (h * r * g_ref[...]).astype(BF16)

    sq = _full((D_MODEL, D_MODEL))
    row = _rows(tm, D_MODEL)
    f32 = jax.ShapeDtypeStruct((T, D_MODEL), F32)
    b16 = jax.ShapeDtypeStruct((T, D_MODEL), BF16)
    return pl.pallas_call(
        body, name="merge_fwd", grid=(T // tm,),
        in_specs=[row, row, _rows(tm, D_MODEL, COL_GC // D_MODEL), _rows(tm, D_MODEL, COL_GA // D_MODEL), row,
                  sq, sq, sq, _full((1, D_MODEL))],
        out_specs=[row] * 5,
        out_shape=[b16, b16, b16, f32, b16],
        compiler_params=_cp("parallel"),
    )(c, o, u, u, x, wc, wa, wm, g_ffn)


def _ffn_in_call(hn, w, tm=512):
    T = hn.shape[0]

    def body(hn_ref, w_ref, fac_ref, act_ref):
        gu = _dot_nt(hn_ref[...], w_ref[...])
        g = gu[:, :D_FF]
        up = gu[:, D_FF:]
        sg = _sigmoid(g)
        silu = g * sg
        act_ref[...] = (silu * up).astype(BF16)
        fac_ref[:, :D_FF] = (up * (sg * (1.0 + g * (1.0 - sg)))).astype(BF16)
        fac_ref[:, D_FF:] = silu.astype(BF16)

    return pl.pallas_call(
        body, name="ffn_in", grid=(T // tm,),
        in_specs=[_rows(tm, D_MODEL), _full(w.shape)],
        out_specs=[_rows(tm, 2 * D_FF), _rows(tm, D_FF)],
        out_shape=[jax.ShapeDtypeStruct((T, 2 * D_FF), BF16), jax.ShapeDtypeStruct((T, D_FF), BF16)],
        compiler_params=_cp("parallel"),
    )(hn, w)


def _ffn_out_call(act, wd, h, tgt, tm=512):
    T = h.shape[0]

    def body(act_ref, wd_ref, h_ref, t_ref, sse_ref, d_ref, db_ref):
        e = h_ref[...] + _dot(act_ref[...], wd_ref[...]) - t_ref[...]

        @pl.when(pl.program_id(0) == 0)
        def _():
            sse_ref[...] = jnp.zeros_like(sse_ref)

        sse_ref[...] += jnp.sum(e * e)
        d = e * (1.0 / D_MODEL)
        d_ref[...] = d
        db_ref[...] = d.astype(BF16)

    row = _rows(tm, D_MODEL)
    return pl.pallas_call(
        body, name="ffn_out", grid=(T // tm,),
        in_specs=[_rows(tm, D_FF), _full(wd.shape), row, row],
        out_specs=[_full((8, 128)), row, row],
        out_shape=[jax.ShapeDtypeStruct((8, 128), F32), jax.ShapeDtypeStruct((T, D_MODEL), F32),
                   jax.ShapeDtypeStruct((T, D_MODEL), BF16)],
        compiler_params=_cp("arbitrary"),
    )(act, wd, h, tgt)


def _dw_call(name, a, b, tr, tk=2048):
    T, K = a.shape
    N = b.shape[1]
    tk = min(tk, T)
    nk = T // tk

    def body(a_ref, b_ref, o_ref, acc_ref):
        k = pl.program_id(1)

        @pl.when(k == 0)
        def _():
            acc_ref[...] = jnp.zeros_like(acc_ref)

        acc_ref[...] += _dot_tn(a_ref[...], b_ref[...])

        @pl.when(k == nk - 1)
        def _():
            o_ref[...] = acc_ref[...].astype(BF16)

    return pl.pallas_call(
        body, name=name, grid=(K // tr, nk),
        in_specs=[pl.BlockSpec((tk, tr), lambda j, k: (k, j)), pl.BlockSpec((tk, N), lambda j, k: (k, 0))],
        out_specs=pl.BlockSpec((tr, N), lambda j, k: (j, 0)),
        out_shape=jax.ShapeDtypeStruct((K, N), BF16),
        scratch_shapes=[pltpu.VMEM((tr, N), F32)],
        compiler_params=_cp("parallel", "arbitrary"),
    )(a, b)


def _concat_rows_call(name, pieces, tr=256):
    N = pieces[0].shape[1]
    starts = [0]
    for p in pieces:
        starts.append(starts[-1] + p.shape[0] // tr)

    def body(*refs):
        o_ref = refs[-1]
        i = pl.program_id(0)
        for p, ref in enumerate(refs[:-1]):
            @pl.when((i >= starts[p]) & (i < starts[p + 1]))
            def _(ref=ref):
                o_ref[...] = ref[...]

    def spec(p):
        lo, n = starts[p], pieces[p].shape[0] // tr
        return pl.BlockSpec((tr, N), lambda i: (jnp.clip(i - lo, 0, n - 1), 0))

    return pl.pallas_call(
        body, name=name, grid=(starts[-1],),
        in_specs=[spec(p) for p in range(len(pieces))], out_specs=pl.BlockSpec((tr, N), lambda i: (i, 0)),
        out_shape=jax.ShapeDtypeStruct((starts[-1] * tr, N), pieces[0].dtype),
        compiler_params=_cp("arbitrary"),
    )(*pieces)


def _rms_bwd(dy, xv, g):
    r = lax.rsqrt(jnp.mean(xv * xv, axis=-1, keepdims=True) + EPS)
    t = dy * g
    dx = r * t - xv * (r * r * r * jnp.mean(t * xv, axis=-1, keepdims=True))
    return dx, dy * xv * r


def _ffn_down_bwd_call(dout_b, wd, gu, tm=512):
    T = dout_b.shape[0]

    def body(d_ref, wd_ref, fg_ref, fu_ref, o_ref):
        dact = _dot_nt(d_ref[...], wd_ref[...])
        o_ref[:, :D_FF] = (dact * fg_ref[...].astype(F32)).astype(BF16)
        o_ref[:, D_FF:] = (dact * fu_ref[...].astype(F32)).astype(BF16)

    return pl.pallas_call(
        body, name="ffn_down_bwd", grid=(T // tm,),
        in_specs=[_rows(tm, D_MODEL), _full(wd.shape), _rows(tm, D_FF, 0), _rows(tm, D_FF, 1)],
        out_specs=_rows(tm, 2 * D_FF),
        out_shape=jax.ShapeDtypeStruct((T, 2 * D_FF), BF16),
        compiler_params=_cp("parallel"),
    )(dout_b, wd, gu, gu)


def _ffn_in_bwd_call(dgu, w, h, g_ffn, dout, tm=512):
    T = h.shape[0]

    def body(dgu_ref, w_ref, h_ref, g_ref, d_ref, dh_ref, dhb_ref, dg_ref):
        dhn = _dot(dgu_ref[...], w_ref[...])
        dx, dgt = _rms_bwd(dhn, h_ref[...], g_ref[...])
        dh = d_ref[...] + dx
        dh_ref[...] = dh
        dhb_ref[...] = dh.astype(BF16)

        @pl.when(pl.program_id(0) == 0)
        def _():
            dg_ref[...] = jnp.zeros_like(dg_ref)

        dg_ref[...] += jnp.sum(dgt, axis=0, keepdims=True)

    row = _rows(tm, D_MODEL)
    return pl.pallas_call(
        body, name="ffn_in_bwd", grid=(T // tm,),
        in_specs=[_rows(tm, 2 * D_FF), _full(w.shape), row, _full((1, D_MODEL)), row],
        out_specs=[row, row, _full((1, D_MODEL))],
        out_shape=[jax.ShapeDtypeStruct((T, D_MODEL), F32), jax.ShapeDtypeStruct((T, D_MODEL), BF16),
                   jax.ShapeDtypeStruct((1, D_MODEL), F32)],
        compiler_params=_cp("arbitrary"),
    )(dgu, w, h, g_ffn, dout)


def _merge_bwd_call(dh_b, wm, wc, wa, u, yc, ya, tm=512):
    T = dh_b.shape[0]

    def body(dh_ref, wm_ref, wc_ref, wa_ref, gc_ref, ga_ref, yc_ref, ya_ref, dyc_ref, dya_ref, dug_ref, dcp_ref, do_ref):
        dz = _dot_nt(dh_ref[...], wm_ref[...])
        sgc = _sigmoid(gc_ref[...].astype(F32))
        sga = _sigmoid(ga_ref[...].astype(F32))
        dyc = (dz * sgc).astype(BF16)
        dya = (dz * sga).astype(BF16)
        dyc_ref[...] = dyc
        dya_ref[...] = dya
        dug_ref[:, :D_MODEL] = (dz * yc_ref[...].astype(F32) * (sgc * (1.0 - sgc))).astype(BF16)
        dug_ref[:, D_MODEL:] = (dz * ya_ref[...].astype(F32) * (sga * (1.0 - sga))).astype(BF16)
        dcp_ref[...] = _dot_nt(dyc, wc_ref[...])
        do_ref[...] = _dot_nt(dya, wa_ref[...]).astype(BF16)

    sq = _full((D_MODEL, D_MODEL))
    row = _rows(tm, D_MODEL)
    b16 = jax.ShapeDtypeStruct((T, D_MODEL), BF16)
    return pl.pallas_call(
        body, name="merge_bwd", grid=(T // tm,),
        in_specs=[row, sq, sq, sq, _rows(tm, D_MODEL, COL_GC // D_MODEL), _rows(tm, D_MODEL, COL_GA // D_MODEL), row, row],
        out_specs=[row, row, _rows(tm, 2 * D_MODEL), row, row],
        out_shape=[b16, b16, jax.ShapeDtypeStruct((T, 2 * D_MODEL), BF16), jax.ShapeDtypeStruct((T, D_MODEL), F32), b16],
        compiler_params=_cp("parallel"),
    )(dh_b, wm, wc, wa, u, u, yc, ya)


def _conv_bwd_call(dcp, hc, u, conv_w, ln_g, ln_b, B, S, ts=256):
    T = B * S
    nS = S // ts
    per32 = ts // HALO
    last32 = T // HALO - 1

    def ln_bwd(dcp_v, hc_v, g, bb):
        mu = jnp.mean(hc_v, axis=-1, keepdims=True)
        xc = hc_v - mu
        rstd = lax.rsqrt(jnp.mean(xc * xc, axis=-1, keepdims=True) + EPS)
        xhat = xc * rstd
        y = xhat * g + bb
        sy = _sigmoid(y)
        dy = dcp_v * (sy * (1.0 + y * (1.0 - sy)))
        dxh = dy * g
        dhc = rstd * (dxh - jnp.mean(dxh, axis=-1, keepdims=True) - xhat * jnp.mean(dxh * xhat, axis=-1, keepdims=True))
        return dhc, dy * xhat, dy

    def body(dcp_ref, dcpn_ref, hc_ref, hcn_ref, a_ref, gate_ref, w_ref, g_ref, bb_ref,
             du_ref, dw_ref, db_ref, dlg_ref, dlb_ref, dhext, glu_scr, dhsh, acc_ref):
        b = pl.program_id(0)
        s = pl.program_id(1)

        @pl.when((b == 0) & (s == 0))
        def _():
            dw_ref[...] = jnp.zeros_like(dw_ref)
            db_ref[...] = jnp.zeros_like(db_ref)
            dlg_ref[...] = jnp.zeros_like(dlg_ref)
            dlb_ref[...] = jnp.zeros_like(dlb_ref)

        dhc, dgt, dbt = ln_bwd(dcp_ref[...], hc_ref[...], g_ref[...], bb_ref[...])
        dhn, _, _ = ln_bwd(dcpn_ref[...], hcn_ref[...], g_ref[...], bb_ref[...])
        dhext[0:ts, :] = dhc
        dhext[ts:, :] = jnp.where(s < nS - 1, dhn, 0.0)
        db_ref[...] += jnp.sum(dhc, axis=0, keepdims=True)
        dlg_ref[...] += jnp.sum(dgt, axis=0, keepdims=True)
        dlb_ref[...] += jnp.sum(dbt, axis=0, keepdims=True)

        sg = _sigmoid(gate_ref[...].astype(F32))
        av = a_ref[...].astype(F32)
        glu_scr[...] = av * sg
        _fill_shifted(dhext, dhsh)

        def colblock(cb, carry):
            cols = pl.ds(pl.multiple_of(cb * 128, 128), 128)
            outs = _conv_taps(w_ref, dhext, dhsh, 0, ts, cb, reverse=True)
            for rb, acc in enumerate(outs):
                acc_ref[rb * 128:(rb + 1) * 128, cols] = acc
            taps = [jnp.zeros((8, 128), F32) for _ in range(CONV_WIDTH)]
            for r0 in range(0, ts, 128):
                glu = [glu_scr[r0 + 8 * i:r0 + 8 * (i + 1), cols] for i in range(16)]
                for s8 in range(8):
                    users = {}
                    for j in range(CONV_WIDTH):
                        if (CONV_WIDTH - 1 - j) % 8 == s8:
                            for i in range(16):
                                users.setdefault(CONV_WIDTH - 1 - j + r0 + 8 * i - s8, []).append((j, i))
                    for row, pairs in sorted(users.items()):
                        dhc8 = dhext[row:row + 8, cols] if s8 == 0 else dhsh[s8 - 1, row:row + 8, cols]
                        for j, i in pairs:
                            taps[j] = taps[j] + glu[i] * dhc8
            for j in range(CONV_WIDTH):
                dw_ref[8 * j:8 * (j + 1), cols] += taps[j]
            return carry

        lax.fori_loop(0, C_CONV // 128, colblock, 0)
        dglu = acc_ref[...]
        du_ref[:, :C_CONV] = (dglu * sg).astype(BF16)
        du_ref[:, C_CONV:] = (dglu * av * (sg * (1.0 - sg))).astype(BF16)

    cur = lambda cb: pl.BlockSpec((ts, C_CONV), lambda b, s: (b * nS + s, cb))
    nxt = pl.BlockSpec((HALO, C_CONV), lambda b, s: (jnp.minimum(b * (S // HALO) + (s + 1) * per32, last32), 0))
    vec = _full((1, C_CONV))
    return pl.pallas_call(
        body, name="conv_bwd", grid=(B, nS),
        in_specs=[cur(0), nxt, cur(0), nxt, cur(0), cur(1), _full((CONV_WIDTH, C_CONV)), vec, vec],
        out_specs=[pl.BlockSpec((ts, 2 * C_CONV), lambda b, s: (b * nS + s, 0)), _full((8 * CONV_WIDTH, C_CONV)), vec, vec, vec],
        out_shape=[jax.ShapeDtypeStruct((T, 2 * C_CONV), BF16), jax.ShapeDtypeStruct((8 * CONV_WIDTH, C_CONV), F32),
                   jax.ShapeDtypeStruct((1, C_CONV), F32), jax.ShapeDtypeStruct((1, C_CONV), F32),
                   jax.ShapeDtypeStruct((1, C_CONV), F32)],
        scratch_shapes=[pltpu.VMEM((ts + HALO, C_CONV), F32), pltpu.VMEM((ts, C_CONV), F32),
                        pltpu.VMEM((7, ts + HALO, C_CONV), F32), pltpu.VMEM((ts, C_CONV), F32)],
        compiler_params=_cp("arbitrary", "arbitrary"),
    )(dcp, dcp, hc, hc, u, u, conv_w, ln_g, ln_b)


def _attn_bwd_call(u, do, qg2, kg2, sinks, slopes, B, S):
    T = B * S
    nb = S // BLOCK
    half = Q_W // 2
    npair = GROUP // 2

    def body(sinks_ref, slopes_ref, q_ref, k_ref, v_ref, do_ref, qg_ref, kg_ref,
             dq_ref, dkv_ref, dqg_ref, dkg_ref, dsk_ref, kA, kB, vA, vB, dkn_acc, dv_acc, dkv_keep,
             btab, s_buf, dp_buf, p_buf, ds_buf, qn_buf, do_buf, rq_buf):
        b = pl.program_id(0)
        kvh = pl.program_id(1)
        lo = _lane_lo()
        lane = lax.broadcasted_iota(jnp.int32, (1, 128), 1)
        first = kvh == 0

        @pl.when((b == 0) & first)
        def _():
            dqg_ref[...] = jnp.zeros_like(dqg_ref)
            dkg_ref[...] = jnp.zeros_like(dkg_ref)
            dsk_ref[...] = jnp.zeros_like(dsk_ref)

        _stage_kv(kvh, k_ref, v_ref, kg_ref, lo, kA, kB, vA, vB)
        dkn_acc[...] = jnp.zeros_like(dkn_acc)
        dv_acc[...] = jnp.zeros_like(dv_acc)
        _fill_bias(btab, kvh, slopes_ref)
        qgs = qg_ref[...] * ATTN_SCALE

        def fold(acc):
            canon = jnp.where(lo, acc[0:2 * BLOCK] + pltpu.roll(acc[2 * BLOCK:], HEAD_DIM, 1), 0.0)
            return jnp.where(first, canon, pltpu.roll(canon, HEAD_DIM, 1))

        def block(n, slot):
            s_scr, dp_scr, p_scr, ds_scr = s_buf.at[slot], dp_buf.at[slot], p_buf.at[slot], ds_buf.at[slot]
            qn_scr, do_scr, rq_scr = qn_buf.at[slot], do_buf.at[slot], rq_buf.at[slot]
            r0 = pl.multiple_of(n * BLOCK, BLOCK)
            win = pl.ds(r0, 2 * BLOCK)
            rows = pl.ds(r0, BLOCK)
            k2t = jnp.concatenate([kA[win, :], kB[win, :]], axis=0)
            v2 = jnp.concatenate([vA[win, :], vB[win, :]], axis=0)
            var = jnp.minimum(n, 1)
            for p in range(npair):
                pc = slice(128 * p, 128 * (p + 1))
                qp = q_ref[rows, pc].astype(F32)
                rq = _head_rstd(qp, lo)
                rq_scr[pc, :] = rq
                qn_scr[pc, :] = (qp * rq * qgs).astype(BF16)
                do_scr[pc, :] = do_ref[rows, pc]
            s_scr[...] = _dot_nt(k2t, qn_scr[...])
            dp_scr[...] = _dot_nt(v2, do_scr[...])
            for p in range(npair):
                for hf in range(2):
                    g = 2 * p + hf
                    keys, qs = slice(2 * BLOCK * hf, 2 * BLOCK * (hf + 1)), slice(128 * p, 128 * (p + 1))
                    pn, esk, rz = _softmax_cols(s_scr, btab, p, var, g, hf, sinks_ref[kvh * GROUP + g])
                    dph = dp_scr[keys, qs]
                    delta = jnp.sum(pn * dph, axis=0, keepdims=True)
                    ds_scr[keys, qs] = (pn * (dph - delta)).astype(BF16)
                    p_scr[keys, qs] = pn.astype(BF16)
                    dsk_ref[...] += jnp.where(lane == kvh * GROUP + g, -jnp.sum(esk * rz * delta), 0.0)
            dqs_all = _dot_tn(k2t, ds_scr[...]).T
            for p in range(npair):
                pc = slice(128 * p, 128 * (p + 1))
                qp = q_ref[rows, pc].astype(F32)
                rq = rq_scr[pc, :]
                dqs = dqs_all[pc, :]
                t = dqs * qgs
                dq = rq * t - qp * (rq * rq * rq * (_half_sums(t * qp, lo) * (1.0 / HEAD_DIM)))
                dq_ref[rows, pc] = dq.astype(BF16)
                dqg_ref[...] += jnp.sum(dqs * qp * rq, axis=0, keepdims=True) * ATTN_SCALE
            dkn_acc[win, :] += fold(_dot(ds_scr[...], qn_scr[...]))
            dv_acc[win, :] += fold(_dot(p_scr[...], do_scr[...]))

        def four_blocks(i, carry):
            for slot in range(4):
                block(4 * i + slot, slot)
            return carry

        lax.fori_loop(0, nb // 4, four_blocks, 0)

        dkn = dkn_acc[BLOCK:, :]
        kv = k_ref[...].astype(F32)
        kg = kg_ref[...]
        rk = _head_rstd(kv, lo)
        t = dkn * kg
        dk = rk * t - kv * (rk * rk * rk * (_half_sums(t * kv, lo) * (1.0 / HEAD_DIM)))
        dkg_ref[...] += jnp.sum(dkn * kv * rk, axis=0, keepdims=True)
        dv = dv_acc[BLOCK:, :]

        @pl.when(first)
        def _():
            dkv_keep[:, :128] = dk
            dkv_keep[:, 128:] = dv

        @pl.when(kvh == N_KV_HEADS - 1)
        def _():
            dkv_ref[:, :128] = (dkv_keep[:, :128] + dk).astype(BF16)
            dkv_ref[:, 128:] = (dkv_keep[:, 128:] + dv).astype(BF16)

    smem = pl.BlockSpec(memory_space=pltpu.SMEM)
    kvbuf = pltpu.VMEM((S + BLOCK, 128), BF16)
    accbuf = pltpu.VMEM((S + BLOCK, 128), F32)
    vec = _full((1, 128))
    return pl.pallas_call(
        body, name="attn_bwd", grid=(B, N_KV_HEADS),
        in_specs=[smem, smem,
                  pl.BlockSpec((S, half), lambda b, h: (b, COL_Q // half + h)),
                  pl.BlockSpec((S, 128), lambda b, h: (b, COL_K // 128)),
                  pl.BlockSpec((S, 128), lambda b, h: (b, COL_V // 128)),
                  pl.BlockSpec((S, half), lambda b, h: (b, h)),
                  vec, vec],
        out_specs=[pl.BlockSpec((S, half), lambda b, h: (b, h)), pl.BlockSpec((S, 256), lambda b, h: (b, 0)), vec, vec, vec],
        out_shape=[jax.ShapeDtypeStruct((T, Q_W), BF16), jax.ShapeDtypeStruct((T, 2 * KV_W), BF16),
                   jax.ShapeDtypeStruct((1, 128), F32), jax.ShapeDtypeStruct((1, 128), F32), jax.ShapeDtypeStruct((1, 128), F32)],
        scratch_shapes=[kvbuf, kvbuf, kvbuf, kvbuf, accbuf, accbuf, pltpu.VMEM((S, 256), F32),
                        pltpu.VMEM((2, GROUP, 2 * BLOCK, BLOCK), F32),
                        pltpu.VMEM((4, npair * BLOCK, 4 * BLOCK), F32), pltpu.VMEM((4, npair * BLOCK, 4 * BLOCK), F32),
                        pltpu.VMEM((4, npair * BLOCK, 4 * BLOCK), BF16), pltpu.VMEM((4, npair * BLOCK, 4 * BLOCK), BF16),
                        pltpu.VMEM((4, npair * BLOCK, 128), BF16), pltpu.VMEM((4, npair * BLOCK, 128), BF16),
                        pltpu.VMEM((4, npair * BLOCK, 128), F32)],
        compiler_params=_cp("arbitrary", "arbitrary"),
    )(sinks, slopes, u, u, u, do, qg2, kg2)


def _inproj_bwd_call(du_conv, du_g, du_q, du_kv, w, x, g, dh, tm=512):
    T = x.shape[0]

    def body(dc_ref, dg_ref, dq_ref, dkv_ref, w_ref, x_ref, g_ref, dh_ref, gx_ref, dgm_ref):
        dxn = _dot(dc_ref[...], w_ref[COL_A:COL_GC, :])
        dxn += _dot(dg_ref[...], w_ref[COL_GC:COL_Q, :])
        dxn += _dot(dq_ref[...], w_ref[COL_Q:COL_K, :])
        dxn += _dot(dkv_ref[...], w_ref[COL_K:, :])
        dx, dgt = _rms_bwd(dxn, x_ref[...], g_ref[...])
        gx_ref[...] = dh_ref[...] + dx

        @pl.when(pl.program_id(0) == 0)
        def _():
            dgm_ref[...] = jnp.zeros_like(dgm_ref)

        dgm_ref[...] += jnp.sum(dgt, axis=0, keepdims=True)

    row = _rows(tm, D_MODEL)
    return pl.pallas_call(
        body, name="inproj_bwd", grid=(T // tm,),
        in_specs=[_rows(tm, 2 * C_CONV), _rows(tm, 2 * D_MODEL), _rows(tm, Q_W), _rows(tm, 2 * KV_W), _full(w.shape),
                  row, _full((1, D_MODEL)), row],
        out_specs=[row, _full((1, D_MODEL))],
        out_shape=[jax.ShapeDtypeStruct((T, D_MODEL), F32), jax.ShapeDtypeStruct((1, D_MODEL), F32)],
        compiler_params=_cp("arbitrary"),
    )(du_conv, du_g, du_q, du_kv, w, x, g, dh)


def _adamw_math(w, g, m, v):
    m = ADAM_B1 * m + (1.0 - ADAM_B1) * g
    v = ADAM_B2 * v + (1.0 - ADAM_B2) * (g * g)
    m_hat = m / (1.0 - ADAM_B1 ** ADAM_STEP)
    v_hat = v / (1.0 - ADAM_B2 ** ADAM_STEP)
    delta = -ADAM_LR * (m_hat / (jnp.sqrt(v_hat) + ADAM_EPS) + ADAM_WD * w)
    return delta, m, v


def _adamw_sum_call(name, parts, own, w, m, v, tr):
    R, C = w.shape

    def body(p_ref, own_ref, w_ref, m_ref, v_ref, g_ref, d_ref, nm_ref, nv_ref):
        x, y, c = _position()
        me = 4 * x + 2 * y + c
        g = jnp.zeros((tr, C), F32)
        for s in range(N_DEV):
            g = g + jnp.where(me == s, own_ref[...], p_ref[s]).astype(F32)
        g_ref[...] = g
        d_ref[...], nm_ref[...], nv_ref[...] = _adamw_math(w_ref[...], g, m_ref[...], v_ref[...])

    blk = pl.BlockSpec((tr, C), lambda i: (i, 0))
    out = jax.ShapeDtypeStruct((R, C), F32)
    return pl.pallas_call(
        body, name=name, grid=(R // tr,),
        in_specs=[pl.BlockSpec((N_DEV, tr, C), lambda i: (0, i, 0)), blk, blk, blk, blk],
        out_specs=[blk] * 4, out_shape=[out] * 4,
        compiler_params=_cp("parallel"),
    )(parts, own, w, m, v)


def _adamw_small_call(name, g, w, m, v):
    def body(g_ref, w_ref, m_ref, v_ref, d_ref, nm_ref, nv_ref):
        d_ref[...], nm_ref[...], nv_ref[...] = _adamw_math(w_ref[...], g_ref[...], m_ref[...], v_ref[...])

    out = jax.ShapeDtypeStruct(w.shape, F32)
    return pl.pallas_call(body, name=name, out_shape=[out] * 3)(g, w, m, v)


def _position():
    return lax.axis_index("x"), lax.axis_index("y"), lax.axis_index("c")


def _peer(x, y, c, rel):
    fx, fy, fc = (rel >> 2) & 1, (rel >> 1) & 1, rel & 1
    px = 1 - x if fx else x
    py = 1 - y if fy else y
    pc = 1 - c if fc else c
    return (px, py, pc), 4 * px + 2 * py + pc


def _gather_weights_call(shards):
    n = len(shards)

    def body(*refs):
        ins, outs = refs[:n], refs[n:2 * n]
        send_sems, recv_sems, local_sems = refs[2 * n:]
        x, y, c = _position()
        me = 4 * x + 2 * y + c
        sib, sib_id = _peer(x, y, c, 1)
        chips = [_peer(x, y, c, rel) for rel in (4, 2, 6)]

        def copy(i, k, src, slot, to):
            return pltpu.make_async_remote_copy(
                src_ref=src, dst_ref=outs[i].at[slot], send_sem=send_sems.at[i, k], recv_sem=recv_sems.at[i, k],
                device_id=to, device_id_type=MESH)

        mine = [pltpu.make_async_copy(ins[i], outs[i].at[me], local_sems.at[i]) for i in range(n)]
        for cp in mine:
            cp.start()
        first = []
        for i in range(n):
            first.append(copy(i, 0, ins[i], me, sib))
            first += [copy(i, 1 + j, ins[i], me, dev) for j, (dev, _) in enumerate(chips)]
        for cp in first:
            cp.start()
        passed = []
        for i in range(n):
            for j, (dev, slot) in enumerate(chips):
                copy(i, 1 + j, ins[i], slot, dev).wait_recv()
                fwd = copy(i, 4 + j, outs[i].at[slot], slot, sib)
                fwd.start()
                passed.append(fwd)
        for i in range(n):
            copy(i, 0, ins[i], sib_id, sib).wait_recv()
            for j, (_, slot) in enumerate(chips):
                copy(i, 4 + j, ins[i], slot ^ 1, sib).wait_recv()
        for cp in first + passed:
            cp.wait_send()
        for cp in mine:
            cp.wait()

    hbm = pl.BlockSpec(memory_space=pl.ANY)
    return pl.pallas_call(
        body, name="gather_weights",
        in_specs=[hbm] * n, out_specs=[hbm] * n,
        out_shape=[jax.ShapeDtypeStruct((N_DEV,) + s.shape, s.dtype) for s in shards],
        scratch_shapes=[pltpu.SemaphoreType.DMA((n, 7)), pltpu.SemaphoreType.DMA((n, 7)), pltpu.SemaphoreType.DMA((n,))],
    )(*shards)


_HBM = pl.BlockSpec(memory_space=pltpu.HBM)
_SEM = pl.BlockSpec(memory_space=pltpu.SEMAPHORE)
_EFFECT = pltpu.SideEffectType.DATAFLOW_SIDE_EFFECTING


def _exchange_copies(scatter, incoming, src_refs, land_refs, send_sems, recv_sems):
    x, y, c = _position()
    me = 4 * x + 2 * y + c
    out = []
    for i, (src, land) in enumerate(zip(src_refs, land_refs)):
        for k in range(N_DEV - 1):
            dev, dev_id = _peer(x, y, c, k + 1)
            j = i * (N_DEV - 1) + k
            out.append(pltpu.make_async_remote_copy(
                src_ref=src.at[dev_id] if scatter else src, dst_ref=land.at[dev_id if incoming else me],
                send_sem=send_sems.at[j], recv_sem=recv_sems.at[j], device_id=dev, device_id_type=MESH))
    return out


def _exchange_start_call(name, arrays, scatter):
    n = len(arrays)
    lands = [lax.empty((N_DEV,) + (a.shape[1:] if scatter else a.shape), a.dtype) for a in arrays]

    def body(*refs):
        src_refs, land_refs = refs[:n], refs[n:2 * n]
        send_sems, recv_sems, token = refs[2 * n], refs[2 * n + 1], refs[-1]
        for copy in _exchange_copies(scatter, False, src_refs, land_refs, send_sems, recv_sems):
            copy.start()
        token[...] = jnp.zeros_like(token)

    sems = pltpu.SemaphoreType.DMA((n * (N_DEV - 1),))
    res = pl.pallas_call(
        body, name=name,
        out_shape=[sems, sems] + [pltpu.HBM(a.shape, a.dtype) for a in arrays] + [pltpu.HBM(l.shape, l.dtype) for l in lands]
        + [jax.ShapeDtypeStruct((8, 128), F32)],
        in_specs=[_HBM] * (2 * n), out_specs=[_SEM, _SEM] + [_HBM] * (2 * n) + [pl.BlockSpec(memory_space=pltpu.VMEM)],
        input_output_aliases={i: 2 + i for i in range(2 * n)},
        compiler_params=pltpu.CompilerParams(has_side_effects=_EFFECT),
    )(*[pltpu.with_memory_space_constraint(a, pltpu.HBM) for a in list(arrays) + lands])
    return res[0], res[1], res[2:2 + n], res[2 + n:2 + 2 * n], res[-1]


def _exchange_wait_call(name, started, scatter, after):
    send_sems, recv_sems, srcs, lands, _ = started
    n = len(srcs)

    def body(*refs):
        src_refs, land_refs = refs[:n], refs[n:2 * n]
        send_s, recv_s = refs[2 * n], refs[2 * n + 1]
        for copy in _exchange_copies(scatter, False, src_refs, land_refs, send_s, recv_s):
            copy.wait_send()
        for copy in _exchange_copies(scatter, True, src_refs, land_refs, send_s, recv_s):
            copy.wait_recv()

    res = pl.pallas_call(
        body, name=name,
        out_shape=[pltpu.HBM(a.shape, a.dtype) for a in list(srcs) + list(lands)],
        in_specs=[_HBM] * (2 * n) + [_SEM, _SEM, pl.BlockSpec(memory_space=pl.ANY)], out_specs=[_HBM] * (2 * n),
        input_output_aliases={i: i for i in range(2 * n)},
        compiler_params=pltpu.CompilerParams(has_side_effects=_EFFECT),
    )(*srcs, *lands, send_sems, recv_sems, after)
    return res[n:]


def _allsum_small_call(part, blocks):
    P, Q = part.shape[0], blocks.shape[1]

    def body(p_ref, b_ref, o_ref, ob_ref, all_ref, allb_ref, send_sems, recv_sems):
        x, y, c = _position()
        me = 4 * x + 2 * y + c
        peers = [_peer(x, y, c, rel) for rel in range(1, N_DEV)]
        all_ref[me] = p_ref[...]
        allb_ref[me] = b_ref[me]

        def copies(k, slot, dev, dev_id):
            sems = lambda i: dict(send_sem=send_sems.at[i], recv_sem=recv_sems.at[i], device_id=dev, device_id_type=MESH)
            return (pltpu.make_async_remote_copy(src_ref=p_ref, dst_ref=all_ref.at[slot], **sems(k)),
                    pltpu.make_async_remote_copy(src_ref=b_ref.at[dev_id], dst_ref=allb_ref.at[slot], **sems(7 + k)))

        sends = [cp for k, (dev, dev_id) in enumerate(peers) for cp in copies(k, me, dev, dev_id)]
        for cp in sends:
            cp.start()
        for k, (dev, dev_id) in enumerate(peers):
            for cp in copies(k, dev_id, dev, dev_id):
                cp.wait_recv()
        for cp in sends:
            cp.wait_send()
        tot, totb = all_ref[0], allb_ref[0]
        for s in range(1, N_DEV):
            tot, totb = tot + all_ref[s], totb + allb_ref[s]
        o_ref[...] = tot
        ob_ref[...] = totb

    vm = pl.BlockSpec(memory_space=pltpu.VMEM)
    return pl.pallas_call(
        body, name="allsum_small", in_specs=[vm, vm], out_specs=[vm, vm],
        out_shape=[jax.ShapeDtypeStruct((P, 128), F32), jax.ShapeDtypeStruct((Q, 128), F32)],
        scratch_shapes=[pltpu.VMEM((N_DEV, P, 128), F32), pltpu.VMEM((N_DEV, Q, 128), F32),
                        pltpu.SemaphoreType.DMA((14,)), pltpu.SemaphoreType.DMA((14,))],
    )(part, blocks)


def _pad_rows(v, rows):
    flat = v.reshape(-1)
    return jnp.pad(flat, (0, rows * 128 - flat.shape[0])).reshape(rows, 128)


def _local_step(x, tgt, w_in_i, conv_w, conv_b, ln_g, ln_b, qg, kg, sinks, g_mix, g_ffn, token0, rest_weights, grads_ready, B, S):
    row = lambda v: v.reshape(1, -1)
    qg2 = jnp.tile(qg, 2).reshape(1, 128)
    kg2 = jnp.tile(kg, 2).reshape(1, 128)
    heads = jnp.arange(1, N_Q_HEADS + 1, dtype=F32)
    slopes = jnp.exp2(-8.0 * heads / N_Q_HEADS)

    u, xn = _inproj_call(x, row(g_mix) + token0, w_in_i)
    hc, c = _conv_fwd_call(u, conv_w, row(conv_b), row(ln_g), row(ln_b), B, S)
    o = _attn_fwd_call(u, qg2, kg2, sinks, slopes, B, S)
    wc, wa, wm, w_ffn_in, wd = rest_weights(o)
    yc, ya, z, h, hn = _merge_call(c, o, u, x, wc, wa, wm, row(g_ffn))
    gu, act = _ffn_in_call(hn, w_ffn_in)
    sse, dout, dout_b = _ffn_out_call(act, wd, h, tgt)

    dgu = _ffn_down_bwd_call(dout_b, wd, gu)
    g_wd = _dw_call("dw_ffn_down", act, dout_b, tr=D_FF // 2)
    g_wffn_t = _dw_call("dw_ffn_in", dgu, hn, tr=2 * D_FF // 4)
    tok = grads_ready(dict(w_ffn_down=g_wd, w_ffn_in=g_wffn_t))
    dh, dh_b, g_gffn = _ffn_in_bwd_call(dgu, w_ffn_in, h, row(g_ffn) + tok, dout)
    dyc, dya, du_g, dcp, do = _merge_bwd_call(dh_b, wm, wc, wa, u, yc, ya)
    g_wm = _dw_call("dw_merge", z, dh_b, tr=D_MODEL)
    g_wc = _dw_call("dw_conv_out", c, dyc, tr=D_MODEL)
    g_wa = _dw_call("dw_attn_out", o, dya, tr=D_MODEL)
    tok = grads_ready(dict(w_merge_out=g_wm, w_conv_out=g_wc, w_attn_out=g_wa))
    du_conv, g_cw8, g_cb, g_lg, g_lb = _conv_bwd_call(dcp, hc, u, conv_w, row(ln_g) + tok, row(ln_b), B, S)
    du_q, du_kv, g_qg2, g_kg2, g_sk = _attn_bwd_call(u, do, qg2, kg2, sinks, slopes, B, S)
    g_win_t = _concat_rows_call("dw_in_rows", [
        _dw_call("dw_in_conv", du_conv, xn, tr=D_MODEL),
        _dw_call("dw_in_q", du_q, xn, tr=D_MODEL),
        _dw_call("dw_in_kv", du_kv, xn, tr=2 * KV_W),
        _dw_call("dw_in_gates", du_g, xn, tr=D_MODEL)])
    tok = grads_ready(dict(w_in=g_win_t))
    grad_x, g_gmix = _inproj_bwd_call(du_conv, du_g, du_q, du_kv, w_in_i, x, row(g_mix) + tok, dh)

    g_cw = g_cw8.reshape(CONV_WIDTH, 8, C_CONV).sum(axis=1)
    g_qg = g_qg2[0, :HEAD_DIM] + g_qg2[0, HEAD_DIM:]
    g_kg = g_kg2[0, :HEAD_DIM] + g_kg2[0, HEAD_DIM:]
    small = dict(norm_mix_g=g_gmix[0], conv_dw_b=g_cb[0], conv_ln_g=g_lg[0], conv_ln_b=g_lb[0], q_norm_g=g_qg,
                 k_norm_g=g_kg, sinks=g_sk[0, :N_Q_HEADS], norm_ffn_g=g_gffn[0], conv_dw_w=g_cw)
    return sse[0, 0], grad_x, small


def _to_internal_rows(wt):
    o1 = 2 * C_CONV
    o4 = o1 + Q_W + 2 * KV_W
    return jnp.concatenate([wt[:o1], wt[o4:], wt[o1:o4]], axis=0)


_COL_SHARDED = ("w_in", "w_ffn_in")

_SMALL = (("norm_mix_g", 8), ("conv_dw_b", 8), ("conv_ln_g", 8), ("conv_ln_b", 8), ("norm_ffn_g", 8),
          ("q_norm_g", 1), ("k_norm_g", 1), ("sinks", 1))
_SMALL_ROWS = 48


def _pack_small(d):
    rows = [_pad_rows(d[name], r) for name, r in _SMALL]
    used = sum(r for _, r in _SMALL)
    return jnp.concatenate(rows + [jnp.zeros((_SMALL_ROWS - used, 128), F32)], axis=0)


def _unpack_small(packed, like):
    out, r0 = {}, 0
    for name, r in _SMALL:
        n = like[name].size
        out[name] = packed[r0:r0 + r].reshape(-1)[:n].reshape(like[name].shape)
        r0 += r
    return out


def kernel(x, norm_mix_g, w_in, conv_dw_w, conv_dw_b, conv_ln_g, conv_ln_b, w_conv_out, q_norm_g, k_norm_g, sinks, w_attn_out, w_merge_out, norm_ffn_g, w_ffn_in, w_ffn_down, loss_target, m_norm_mix_g, m_w_in, m_conv_dw_w, m_conv_dw_b, m_conv_ln_g, m_conv_ln_b, m_w_conv_out, m_q_norm_g, m_k_norm_g, m_sinks, m_w_attn_out, m_w_merge_out, m_norm_ffn_g, m_w_ffn_in, m_w_ffn_down, v_norm_mix_g, v_w_in, v_conv_dw_w, v_conv_dw_b, v_conv_ln_g, v_conv_ln_b, v_w_conv_out, v_q_norm_g, v_k_norm_g, v_sinks, v_w_attn_out, v_w_merge_out, v_norm_ffn_g, v_w_ffn_in, v_w_ffn_down):
    names = ["norm_mix_g", "w_in", "conv_dw_w", "conv_dw_b", "conv_ln_g", "conv_ln_b", "w_conv_out", "q_norm_g",
             "k_norm_g", "sinks", "w_attn_out", "w_merge_out", "norm_ffn_g", "w_ffn_in", "w_ffn_down"]
    w = dict(zip(names, (norm_mix_g, w_in, conv_dw_w, conv_dw_b, conv_ln_g, conv_ln_b, w_conv_out, q_norm_g, k_norm_g,
                         sinks, w_attn_out, w_merge_out, norm_ffn_g, w_ffn_in, w_ffn_down)))
    m = dict(zip(names, (m_norm_mix_g, m_w_in, m_conv_dw_w, m_conv_dw_b, m_conv_ln_g, m_conv_ln_b, m_w_conv_out,
                         m_q_norm_g, m_k_norm_g, m_sinks, m_w_attn_out, m_w_merge_out, m_norm_ffn_g, m_w_ffn_in,
                         m_w_ffn_down)))
    v = dict(zip(names, (v_norm_mix_g, v_w_in, v_conv_dw_w, v_conv_dw_b, v_conv_ln_g, v_conv_ln_b, v_w_conv_out,
                         v_q_norm_g, v_k_norm_g, v_sinks, v_w_attn_out, v_w_merge_out, v_norm_ffn_g, v_w_ffn_in,
                         v_w_ffn_down)))
    B, S, _ = x.shape
    T = B * S
    me = 4 * lax.axis_index("x") + 2 * lax.axis_index("y") + lax.axis_index("c")

    as_rows = lambda k, t: t.T if k in _COL_SHARDED else t
    cw_shard = _pad_rows(conv_dw_w.reshape(CONV_WIDTH, 128), 32)
    g_win, g_cwt = _gather_weights_call([w_in.astype(BF16), cw_shard])
    conv_w_full = g_cwt.reshape(N_DEV, 32 * 128)[:, :CONV_WIDTH * 128].reshape(N_DEV, CONV_WIDTH, 128)
    conv_w_full = conv_w_full.transpose(1, 0, 2).reshape(CONV_WIDTH, C_CONV)
    w_in_i = _to_internal_rows(g_win.transpose(0, 2, 1).reshape(IN_COLS, D_MODEL))
    rest_names = ["w_conv_out", "w_attn_out", "w_merge_out", "w_ffn_in", "w_ffn_down"]
    rest_shards = [as_rows(k, w[k]).astype(BF16) for k in rest_names]
    rest_started = _exchange_start_call("gather_rest_start", rest_shards, scatter=False)

    def rest_weights(after):
        lands = _exchange_wait_call("gather_rest_wait", rest_started, False, after)
        mine = lax.broadcasted_iota(jnp.int32, (N_DEV, 1, 1), 0) == me
        full = [jnp.where(mine, s[None], l) for l, s in zip(lands, rest_shards)]
        return tuple(f.reshape(N_DEV * f.shape[1], D_MODEL) for f in full)

    exchanges = []

    def grads_ready(group):
        ks = list(group)
        blocks = [group[k].reshape(N_DEV, group[k].shape[0] // N_DEV, D_MODEL) for k in ks]
        started = _exchange_start_call("scatter_" + ks[0] + "_start", blocks, scatter=True)
        exchanges.append((ks, started, [lax.dynamic_index_in_dim(b, me, 0, keepdims=False) for b in blocks]))
        return started[-1][0:1, 0:1]

    sse, grad_x, small = _local_step(
        x.reshape(T, D_MODEL), loss_target.reshape(T, D_MODEL), w_in_i, conv_w_full, conv_dw_b, conv_ln_g, conv_ln_b,
        q_norm_g, k_norm_g, sinks, norm_mix_g, norm_ffn_g, rest_started[-1][0:1, 0:1], rest_weights, grads_ready, B, S)

    grad, delta, new_m, new_v = {}, {}, {}, {}
    packed = jnp.concatenate([_pack_small(small), jnp.full((8, 128), sse, F32)], axis=0)
    cw_blocks = small["conv_dw_w"].reshape(CONV_WIDTH, N_DEV, 128).transpose(1, 0, 2)
    tot, cw_tot = _allsum_small_call(packed, jnp.pad(cw_blocks, ((0, 0), (0, 32 - CONV_WIDTH), (0, 0))))
    loss = 0.5 / D_MODEL * tot[_SMALL_ROWS, 0]
    g_small = _unpack_small(tot[:_SMALL_ROWS], w)
    d_s, m_s, v_s = _adamw_small_call("adamw_small", tot[:_SMALL_ROWS], _pack_small(w), _pack_small(m), _pack_small(v))
    d_small, m_small, v_small = _unpack_small(d_s, w), _unpack_small(m_s, w), _unpack_small(v_s, w)
    g_cw = cw_tot[:CONV_WIDTH]
    cw2 = lambda t: t.reshape(CONV_WIDTH, 128)
    d_cw, m_cw, v_cw = _adamw_small_call("adamw_conv_w", g_cw, cw2(conv_dw_w), cw2(m_conv_dw_w), cw2(v_conv_dw_w))
    cw3 = lambda t: t.reshape(CONV_WIDTH, 1, 128)
    for k, _ in _SMALL:
        grad[k], delta[k], new_m[k], new_v[k] = g_small[k], d_small[k], m_small[k], v_small[k]
    grad["conv_dw_w"], delta["conv_dw_w"], new_m["conv_dw_w"], new_v["conv_dw_w"] = cw3(g_cw), cw3(d_cw), cw3(m_cw), cw3(v_cw)

    after = d_cw
    for ks, started, owns in exchanges:
        lands = _exchange_wait_call("scatter_" + ks[0] + "_wait", started, True, after)
        for k, land, own in zip(ks, lands, owns):
            tr = {672: 336, 704: 352, 352: 176, 128: 128}[own.shape[0]]
            outs = _adamw_sum_call("adamw_" + k, land, own, as_rows(k, w[k]), as_rows(k, m[k]), as_rows(k, v[k]), tr)
            grad[k], delta[k], new_m[k], new_v[k] = (as_rows(k, t) for t in outs)
            after = outs[3]

    return (loss, grad_x.reshape(B, S, D_MODEL), *[grad[k] for k in names], *[delta[k] for k in names],
            *[new_m[k] for k in names], *[new_v[k] for k in names])
```

```python
import functools
import math

import jax
import jax.numpy as jnp
from jax import lax
from jax.experimental import pallas as pl
from jax.experimental.pallas import tpu as pltpu

F32 = jnp.float32
BF16 = jnp.bfloat16
MESH = pl.DeviceIdType.MESH

N_DEV = 8
D_MODEL = 1024
C_CONV = 1024
CONV_WIDTH = 31
HEAD_DIM = 64
N_Q_HEADS = 16
N_KV_HEADS = 2
GROUP = N_Q_HEADS // N_KV_HEADS
WINDOW = 128
BLOCK = 128
D_FF = 2816
EPS = 1e-6
NEG = -1e30
ATTN_SCALE = 1.0 / math.sqrt(HEAD_DIM)
Q_W = N_Q_HEADS * HEAD_DIM
KV_W = N_KV_HEADS * HEAD_DIM
IN_COLS = 2 * C_CONV + Q_W + 2 * KV_W + 2 * D_MODEL

ADAM_LR = 0.001
ADAM_B1 = 0.9
ADAM_B2 = 0.999
ADAM_EPS = 1e-08
ADAM_WD = 0.01
ADAM_STEP = 10

COL_A, COL_GATE, COL_GC, COL_GA, COL_Q, COL_K, COL_V = 0, 1024, 2048, 3072, 4096, 5120, 5248

HALO = 32
VMEM_LIMIT = 56 * 1024 * 1024


def _cp(*sem):
    return pltpu.CompilerParams(dimension_semantics=sem, vmem_limit_bytes=VMEM_LIMIT)


def _sigmoid(x):
    return jax.nn.sigmoid(x)


def _dot(a, b):
    return jnp.dot(a, b, preferred_element_type=F32)


def _dot_nt(a, b):
    return lax.dot_general(a, b, (((1,), (1,)), ((), ())), preferred_element_type=F32)


def _dot_tn(a, b):
    return lax.dot_general(a, b, (((0,), (0,)), ((), ())), preferred_element_type=F32)


def _full(shape):
    return pl.BlockSpec(shape, lambda *_: (0,) * len(shape))


def _rows(tm, ncols, colblk=0):
    return pl.BlockSpec((tm, ncols), lambda i: (i, colblk))


def _inproj_call(x, g, w, tm=512):
    T = x.shape[0]
    N = w.shape[0]

    def body(x_ref, g_ref, w_ref, u_ref, xn_ref):
        xv = x_ref[...]
        r = lax.rsqrt(jnp.mean(xv * xv, axis=-1, keepdims=True) + EPS)
        xn = (xv * r * g_ref[...]).astype(BF16)
        xn_ref[...] = xn
        u_ref[...] = _dot_nt(xn, w_ref[...]).astype(BF16)

    return pl.pallas_call(
        body, name="inproj", grid=(T // tm,),
        in_specs=[_rows(tm, D_MODEL), _full((1, D_MODEL)), _full(w.shape)],
        out_specs=[_rows(tm, N), _rows(tm, D_MODEL)],
        out_shape=[jax.ShapeDtypeStruct((T, N), BF16), jax.ShapeDtypeStruct((T, D_MODEL), BF16)],
        compiler_params=_cp("parallel"),
    )(x, g, w)


def _fill_shifted(src_ref, sh_ref):
    n = src_ref.shape[0] - 8
    for s in range(1, 8):
        sh_ref[s - 1, 0:n, :] = src_ref[s:s + n, :]


def _window(src_ref, sh_ref, off, cols):
    s = off % 8
    if s == 0:
        return src_ref[off:off + 128, cols]
    return sh_ref[s - 1, off - s:off - s + 128, cols]


def _conv_taps(w_ref, src_ref, sh_ref, base, rows, cb, reverse):
    cols = pl.ds(pl.multiple_of(cb * 128, 128), 128)
    outs = []
    for rb in range(rows // 128):
        acc = jnp.zeros((128, 128), F32)
        for j in range(CONV_WIDTH):
            off = base + (CONV_WIDTH - 1 - j if reverse else j) + rb * 128
            acc = acc + w_ref[j:j + 1, cols] * _window(src_ref, sh_ref, off, cols)
        outs.append(acc)
    return outs


def _conv_fwd_call(u, conv_w, conv_b, ln_g, ln_b, B, S, ts=512):
    T = B * S
    nS = S // ts
    per32 = ts // HALO

    def body(a_ref, gate_ref, ap_ref, gp_ref, w_ref, b_ref, g_ref, bb_ref, hc_ref, c_ref, hext, hsh, acc_ref):
        s = pl.program_id(1)
        prev = ap_ref[...].astype(F32) * _sigmoid(gp_ref[...].astype(F32))
        hext[0:HALO, :] = jnp.where(s > 0, prev, 0.0)
        hext[HALO:, :] = a_ref[...].astype(F32) * _sigmoid(gate_ref[...].astype(F32))
        _fill_shifted(hext, hsh)

        def colblock(cb, carry):
            cols = pl.ds(pl.multiple_of(cb * 128, 128), 128)
            outs = _conv_taps(w_ref, hext, hsh, HALO - (CONV_WIDTH - 1), ts, cb, reverse=False)
            for rb, acc in enumerate(outs):
                acc_ref[rb * 128:(rb + 1) * 128, cols] = acc
            return carry

        lax.fori_loop(0, C_CONV // 128, colblock, 0)
        hc = acc_ref[...] + b_ref[...]
        hc_ref[...] = hc
        mu = jnp.mean(hc, axis=-1, keepdims=True)
        xc = hc - mu
        var = jnp.mean(xc * xc, axis=-1, keepdims=True)
        y = xc * lax.rsqrt(var + EPS) * g_ref[...] + bb_ref[...]
        c_ref[...] = (y * _sigmoid(y)).astype(BF16)

    cur = lambda cb: pl.BlockSpec((ts, C_CONV), lambda b, s: (b * nS + s, cb))
    prv = lambda cb: pl.BlockSpec((HALO, C_CONV), lambda b, s: (jnp.maximum(b * (S // HALO) + s * per32 - 1, 0), cb))
    vec = _full((1, C_CONV))
    return pl.pallas_call(
        body, name="conv_fwd", grid=(B, nS),
        in_specs=[cur(0), cur(1), prv(0), prv(1), _full((CONV_WIDTH, C_CONV)), vec, vec, vec],
        out_specs=[pl.BlockSpec((ts, C_CONV), lambda b, s: (b * nS + s, 0))] * 2,
        out_shape=[jax.ShapeDtypeStruct((T, C_CONV), F32), jax.ShapeDtypeStruct((T, C_CONV), BF16)],
        scratch_shapes=[pltpu.VMEM((HALO + ts, C_CONV), F32), pltpu.VMEM((7, HALO + ts, C_CONV), F32),
                        pltpu.VMEM((ts, C_CONV), F32)],
        compiler_params=_cp("parallel", "arbitrary"),
    )(u, u, u, u, conv_w, conv_b, ln_g, ln_b)


def _lane_lo():
    return lax.broadcasted_iota(jnp.int32, (1, 128), 1) < HEAD_DIM


def _half_sums(t, lo):
    s_lo = jnp.sum(jnp.where(lo, t, 0.0), axis=-1, keepdims=True)
    s_hi = jnp.sum(jnp.where(lo, 0.0, t), axis=-1, keepdims=True)
    return jnp.where(lo, s_lo, s_hi)


def _head_rstd(t, lo):
    return lax.rsqrt(_half_sums(t * t, lo) * (1.0 / HEAD_DIM) + EPS)


def _fill_bias(btab, kvh, slopes_ref):
    sj = lax.broadcasted_iota(jnp.int32, (2 * BLOCK, BLOCK), 0)
    qi = lax.broadcasted_iota(jnp.int32, (2 * BLOCK, BLOCK), 1)
    dist = qi + BLOCK - sj
    valid = (dist >= 0) & (dist < WINDOW)
    distf = dist.astype(F32)
    for g in range(GROUP):
        bias = jnp.where(valid, -slopes_ref[kvh * GROUP + g] * distf, NEG)
        btab[1, g] = bias
        btab[0, g] = jnp.where(sj >= BLOCK, bias, NEG)


def _softmax_cols(st_ref, btab, p, var, g, hf, sk):
    sh = st_ref[2 * BLOCK * hf:2 * BLOCK * (hf + 1), 128 * p:128 * (p + 1)] + btab[var, g]
    m = jnp.maximum(jnp.max(sh, axis=0, keepdims=True), sk)
    e = jnp.exp(sh - m)
    esk = jnp.exp(sk - m)
    rz = 1.0 / (jnp.sum(e, axis=0, keepdims=True) + esk)
    return e * rz, esk, rz


def _stage_kv(kvh, k_ref, v_ref, kg_ref, lo, kA, kB, vA, vB):
    kv = k_ref[...].astype(F32)
    rk = _head_rstd(kv, lo)
    kn = kv * rk * kg_ref[...]
    mine = lax.broadcasted_iota(jnp.int32, (1, 128), 1) // HEAD_DIM == kvh
    ksel = jnp.where(mine, kn, 0.0)
    vsel = jnp.where(mine, v_ref[...].astype(F32), 0.0)
    krol = pltpu.roll(ksel, HEAD_DIM, 1)
    vrol = pltpu.roll(vsel, HEAD_DIM, 1)
    first = kvh == 0
    zeros = jnp.zeros((BLOCK, 128), BF16)
    for ref, val in ((kA, jnp.where(first, ksel, krol)), (kB, jnp.where(first, krol, ksel)),
                     (vA, jnp.where(first, vsel, vrol)), (vB, jnp.where(first, vrol, vsel))):
        ref[0:BLOCK, :] = zeros
        ref[BLOCK:, :] = val.astype(BF16)


def _attn_fwd_call(u, qg2, kg2, sinks, slopes, B, S):
    T = B * S
    nb = S // BLOCK
    half = Q_W // 2
    npair = GROUP // 2
    per_trip = 8 if nb % 8 == 0 else 4

    def body(sinks_ref, slopes_ref, q_ref, k_ref, v_ref, qg_ref, kg_ref, o_ref, kA, kB, vA, vB, btab, qn_buf, s_buf, p_buf):
        kvh = pl.program_id(1)
        lo = _lane_lo()
        _stage_kv(kvh, k_ref, v_ref, kg_ref, lo, kA, kB, vA, vB)
        _fill_bias(btab, kvh, slopes_ref)
        qgs = qg_ref[...] * ATTN_SCALE

        def block(n, slot):
            qn_scr, s_scr, p_scr = qn_buf.at[slot], s_buf.at[slot], p_buf.at[slot]
            r0 = pl.multiple_of(n * BLOCK, BLOCK)
            win = pl.ds(r0, 2 * BLOCK)
            k2t = jnp.concatenate([kA[win, :], kB[win, :]], axis=0)
            v2 = jnp.concatenate([vA[win, :], vB[win, :]], axis=0)
            var = jnp.minimum(n, 1)
            for p in range(npair):
                pc = slice(128 * p, 128 * (p + 1))
                qp = q_ref[pl.ds(r0, BLOCK), pc].astype(F32)
                qn_scr[pc, :] = (qp * _head_rstd(qp, lo) * qgs).astype(BF16)
            s_scr[...] = _dot_nt(k2t, qn_scr[...])
            for p in range(npair):
                for hf in range(2):
                    g = 2 * p + hf
                    pn, _, _ = _softmax_cols(s_scr, btab, p, var, g, hf, sinks_ref[kvh * GROUP + g])
                    p_scr[2 * BLOCK * hf:2 * BLOCK * (hf + 1), 128 * p:128 * (p + 1)] = pn.astype(BF16)
            o_all = _dot_tn(v2, p_scr[...]).T
            for p in range(npair):
                pc = slice(128 * p, 128 * (p + 1))
                o_ref[pl.ds(r0, BLOCK), pc] = o_all[pc, :].astype(BF16)

        def trip(i, carry):
            for slot in range(per_trip):
                block(per_trip * i + slot, slot)
            return carry

        lax.fori_loop(0, nb // per_trip, trip, 0)

    smem = pl.BlockSpec(memory_space=pltpu.SMEM)
    kvbuf = pltpu.VMEM((S + BLOCK, 128), BF16)
    pair_scores = (per_trip, npair * BLOCK, 4 * BLOCK)
    return pl.pallas_call(
        body, name="attn_fwd", grid=(B, N_KV_HEADS),
        in_specs=[smem, smem,
                  pl.BlockSpec((S, half), lambda b, h: (b, COL_Q // half + h)),
                  pl.BlockSpec((S, 128), lambda b, h: (b, COL_K // 128)),
                  pl.BlockSpec((S, 128), lambda b, h: (b, COL_V // 128)),
                  _full((1, 128)), _full((1, 128))],
        out_specs=pl.BlockSpec((S, half), lambda b, h: (b, h)),
        out_shape=jax.ShapeDtypeStruct((T, Q_W), BF16),
        scratch_shapes=[kvbuf, kvbuf, kvbuf, kvbuf, pltpu.VMEM((2, GROUP, 2 * BLOCK, BLOCK), F32),
                        pltpu.VMEM((per_trip, npair * BLOCK, 128), BF16), pltpu.VMEM(pair_scores, F32),
                        pltpu.VMEM(pair_scores, BF16)],
        compiler_params=_cp("parallel", "arbitrary"),
    )(sinks, slopes, u, u, u, qg2, kg2)


def _merge_call(c, o, u, x, wc, wa, wm, g_ffn, tm=512):
    T = x.shape[0]

    def body(c_ref, o_ref, gc_ref, ga_ref, x_ref, wc_ref, wa_ref, wm_ref, g_ref, yc_ref, ya_ref, z_ref, h_ref, hn_ref):
        yc = _dot(c_ref[...], wc_ref[...])
        ya = _dot(o_ref[...], wa_ref[...])
        yc_ref[...] = yc.astype(BF16)
        ya_ref[...] = ya.astype(BF16)
        z = (_sigmoid(gc_ref[...].astype(F32)) * yc + _sigmoid(ga_ref[...].astype(F32)) * ya).astype(BF16)
        z_ref[...] = z
        h = x_ref[...] + _dot(z, wm_ref[...])
        h_ref[...] = h
        r = lax.rsqrt(jnp.mean(h * h, axis=-1, keepdims=True) + EPS)
        hn_ref[...] = (h * r * g_ref[...]).astype(BF16)

    sq = _full((D_MODEL, D_MODEL))
    row = _rows(tm, D_MODEL)
    f32 = jax.ShapeDtypeStruct((T, D_MODEL), F32)
    b16 = jax.ShapeDtypeStruct((T, D_MODEL), BF16)
    return pl.pallas_call(
        body, name="merge_fwd", grid=(T // tm,),
        in_specs=[row, row, _rows(tm, D_MODEL, COL_GC // D_MODEL), _rows(tm, D_MODEL, COL_GA // D_MODEL), row,
                  sq, sq, sq, _full((1, D_MODEL))],
        out_specs=[row] * 5,
        out_shape=[b16, b16, b16, f32, b16],
        compiler_params=_cp("parallel"),
    )(c, o, u, u, x, wc, wa, wm, g_ffn)


def _ffn_in_call(hn, w, tm=512):
    T = hn.shape[0]

    def body(hn_ref, w_ref, fac_ref, act_ref):
        gu = _dot_nt(hn_ref[...], w_ref[...])
        g = gu[:, :D_FF]
        up = gu[:, D_FF:]
        sg = _sigmoid(g)
        silu = g * sg
        act_ref[...] = (silu * up).astype(BF16)
        fac_ref[:, :D_FF] = (up * (sg * (1.0 + g * (1.0 - sg)))).astype(BF16)
        fac_ref[:, D_FF:] = silu.astype(BF16)

    return pl.pallas_call(
        body, name="ffn_in", grid=(T // tm,),
        in_specs=[_rows(tm, D_MODEL), _full(w.shape)],
        out_specs=[_rows(tm, 2 * D_FF), _rows(tm, D_FF)],
        out_shape=[jax.ShapeDtypeStruct((T, 2 * D_FF), BF16), jax.ShapeDtypeStruct((T, D_FF), BF16)],
        compiler_params=_cp("parallel"),
    )(hn, w)


def _ffn_out_call(act, wd, h, tgt, tm=512):
    T = h.shape[0]

    def body(act_ref, wd_ref, h_ref, t_ref, sse_ref, d_ref, db_ref):
        e = h_ref[...] + _dot(act_ref[...], wd_ref[...]) - t_ref[...]

        @pl.when(pl.program_id(0) == 0)
        def _():
            sse_ref[...] = jnp.zeros_like(sse_ref)

        sse_ref[...] += jnp.sum(e * e)
        d = e * (1.0 / D_MODEL)
        d_ref[...] = d
        db_ref[...] = d.astype(BF16)

    row = _rows(tm, D_MODEL)
    return pl.pallas_call(
        body, name="ffn_out", grid=(T // tm,),
        in_specs=[_rows(tm, D_FF), _full(wd.shape), row, row],
        out_specs=[_full((8, 128)), row, row],
        out_shape=[jax.ShapeDtypeStruct((8, 128), F32), jax.ShapeDtypeStruct((T, D_MODEL), F32),
                   jax.ShapeDtypeStruct((T, D_MODEL), BF16)],
        compiler_params=_cp("arbitrary"),
    )(act, wd, h, tgt)


def _dw_call(name, a, b, tr, tk=2048):
    T, K = a.shape
    N = b.shape[1]
    tk = min(tk, T)
    nk = T // tk

    def body(a_ref, b_ref, o_ref, acc_ref):
        k = pl.program_id(1)

        @pl.when(k == 0)
        def _():
            acc_ref[...] = jnp.zeros_like(acc_ref)

        acc_ref[...] += _dot_tn(a_ref[...], b_ref[...])

        @pl.when(k == nk - 1)
        def _():
            o_ref[...] = acc_ref[...].astype(BF16)

    return pl.pallas_call(
        body, name=name, grid=(K // tr, nk),
        in_specs=[pl.BlockSpec((tk, tr), lambda j, k: (k, j)), pl.BlockSpec((tk, N), lambda j, k: (k, 0))],
        out_specs=pl.BlockSpec((tr, N), lambda j, k: (j, 0)),
        out_shape=jax.ShapeDtypeStruct((K, N), BF16),
        scratch_shapes=[pltpu.VMEM((tr, N), F32)],
        compiler_params=_cp("parallel", "arbitrary"),
    )(a, b)


def _concat_rows_call(name, pieces, tr=256):
    N = pieces[0].shape[1]
    starts = [0]
    for p in pieces:
        starts.append(starts[-1] + p.shape[0] // tr)

    def body(*refs):
        o_ref = refs[-1]
        i = pl.program_id(0)
        for p, ref in enumerate(refs[:-1]):
            @pl.when((i >= starts[p]) & (i < starts[p + 1]))
            def _(ref=ref):
                o_ref[...] = ref[...]

    def spec(p):
        lo, n = starts[p], pieces[p].shape[0] // tr
        return pl.BlockSpec((tr, N), lambda i: (jnp.clip(i - lo, 0, n - 1), 0))

    return pl.pallas_call(
        body, name=name, grid=(starts[-1],),
        in_specs=[spec(p) for p in range(len(pieces))], out_specs=pl.BlockSpec((tr, N), lambda i: (i, 0)),
        out_shape=jax.ShapeDtypeStruct((starts[-1] * tr, N), pieces[0].dtype),
        compiler_params=_cp("arbitrary"),
    )(*pieces)


def _rms_bwd(dy, xv, g):
    r = lax.rsqrt(jnp.mean(xv * xv, axis=-1, keepdims=True) + EPS)
    t = dy * g
    dx = r * t - xv * (r * r * r * jnp.mean(t * xv, axis=-1, keepdims=True))
    return dx, dy * xv * r


def _ffn_down_bwd_call(dout_b, wd, gu, tm=512):
    T = dout_b.shape[0]

    def body(d_ref, wd_ref, fg_ref, fu_ref, o_ref):
        dact = _dot_nt(d_ref[...], wd_ref[...])
        o_ref[:, :D_FF] = (dact * fg_ref[...].astype(F32)).astype(BF16)
        o_ref[:, D_FF:] = (dact * fu_ref[...].astype(F32)).astype(BF16)

    return pl.pallas_call(
        body, name="ffn_down_bwd", grid=(T // tm,),
        in_specs=[_rows(tm, D_MODEL), _full(wd.shape), _rows(tm, D_FF, 0), _rows(tm, D_FF, 1)],
        out_specs=_rows(tm, 2 * D_FF),
        out_shape=jax.ShapeDtypeStruct((T, 2 * D_FF), BF16),
        compiler_params=_cp("parallel"),
    )(dout_b, wd, gu, gu)


def _ffn_in_bwd_call(dgu, w, h, g_ffn, dout, tm=512):
    T = h.shape[0]

    def body(dgu_ref, w_ref, h_ref, g_ref, d_ref, dh_ref, dhb_ref, dg_ref):
        dhn = _dot(dgu_ref[...], w_ref[...])
        dx, dgt = _rms_bwd(dhn, h_ref[...], g_ref[...])
        dh = d_ref[...] + dx
        dh_ref[...] = dh
        dhb_ref[...] = dh.astype(BF16)

        @pl.when(pl.program_id(0) == 0)
        def _():
            dg_ref[...] = jnp.zeros_like(dg_ref)

        dg_ref[...] += jnp.sum(dgt, axis=0, keepdims=True)

    row = _rows(tm, D_MODEL)
    return pl.pallas_call(
        body, name="ffn_in_bwd", grid=(T // tm,),
        in_specs=[_rows(tm, 2 * D_FF), _full(w.shape), row, _full((1, D_MODEL)), row],
        out_specs=[row, row, _full((1, D_MODEL))],
        out_shape=[jax.ShapeDtypeStruct((T, D_MODEL), F32), jax.ShapeDtypeStruct((T, D_MODEL), BF16),
                   jax.ShapeDtypeStruct((1, D_MODEL), F32)],
        compiler_params=_cp("arbitrary"),
    )(dgu, w, h, g_ffn, dout)


def _merge_bwd_call(dh_b, wm, wc, wa, u, yc, ya, tm=512):
    T = dh_b.shape[0]

    def body(dh_ref, wm_ref, wc_ref, wa_ref, gc_ref, ga_ref, yc_ref, ya_ref, dyc_ref, dya_ref, dug_ref, dcp_ref, do_ref):
        dz = _dot_nt(dh_ref[...], wm_ref[...])
        sgc = _sigmoid(gc_ref[...].astype(F32))
        sga = _sigmoid(ga_ref[...].astype(F32))
        dyc = (dz * sgc).astype(BF16)
        dya = (dz * sga).astype(BF16)
        dyc_ref[...] = dyc
        dya_ref[...] = dya
        dug_ref[:, :D_MODEL] = (dz * yc_ref[...].astype(F32) * (sgc * (1.0 - sgc))).astype(BF16)
        dug_ref[:, D_MODEL:] = (dz * ya_ref[...].astype(F32) * (sga * (1.0 - sga))).astype(BF16)
        dcp_ref[...] = _dot_nt(dyc, wc_ref[...])
        do_ref[...] = _dot_nt(dya, wa_ref[...]).astype(BF16)

    sq = _full((D_MODEL, D_MODEL))
    row = _rows(tm, D_MODEL)
    b16 = jax.ShapeDtypeStruct((T, D_MODEL), BF16)
    return pl.pallas_call(
        body, name="merge_bwd", grid=(T // tm,),
        in_specs=[row, sq, sq, sq, _rows(tm, D_MODEL, COL_GC // D_MODEL), _rows(tm, D_MODEL, COL_GA // D_MODEL), row, row],
        out_specs=[row, row, _rows(tm, 2 * D_MODEL), row, row],
        out_shape=[b16, b16, jax.ShapeDtypeStruct((T, 2 * D_MODEL), BF16), jax.ShapeDtypeStruct((T, D_MODEL), F32), b16],
        compiler_params=_cp("parallel"),
    )(dh_b, wm, wc, wa, u, u, yc, ya)


def _conv_bwd_call(dcp, hc, u, conv_w, ln_g, ln_b, B, S, ts=256):
    T = B * S
    nS = S // ts
    per32 = ts // HALO
    last32 = T // HALO - 1

    def ln_bwd(dcp_v, hc_v, g, bb):
        mu = jnp.mean(hc_v, axis=-1, keepdims=True)
        xc = hc_v - mu
        rstd = lax.rsqrt(jnp.mean(xc * xc, axis=-1, keepdims=True) + EPS)
        xhat = xc * rstd
        y = xhat * g + bb
        sy = _sigmoid(y)
        dy = dcp_v * (sy * (1.0 + y * (1.0 - sy)))
        dxh = dy * g
        dhc = rstd * (dxh - jnp.mean(dxh, axis=-1, keepdims=True) - xhat * jnp.mean(dxh * xhat, axis=-1, keepdims=True))
        return dhc, dy * xhat, dy

    def body(dcp_ref, dcpn_ref, hc_ref, hcn_ref, a_ref, gate_ref, w_ref, g_ref, bb_ref,
             du_ref, dw_ref, db_ref, dlg_ref, dlb_ref, dhext, glu_scr, dhsh, acc_ref):
        b = pl.program_id(0)
        s = pl.program_id(1)

        @pl.when((b == 0) & (s == 0))
        def _():
            dw_ref[...] = jnp.zeros_like(dw_ref)
            db_ref[...] = jnp.zeros_like(db_ref)
            dlg_ref[...] = jnp.zeros_like(dlg_ref)
            dlb_ref[...] = jnp.zeros_like(dlb_ref)

        dhc, dgt, dbt = ln_bwd(dcp_ref[...], hc_ref[...], g_ref[...], bb_ref[...])
        dhn, _, _ = ln_bwd(dcpn_ref[...], hcn_ref[...], g_ref[...], bb_ref[...])
        dhext[0:ts, :] = dhc
        dhext[ts:, :] = jnp.where(s < nS - 1, dhn, 0.0)
        db_ref[...] += jnp.sum(dhc, axis=0, keepdims=True)
        dlg_ref[...] += jnp.sum(dgt, axis=0, keepdims=True)
        dlb_ref[...] += jnp.sum(dbt, axis=0, keepdims=True)

        sg = _sigmoid(gate_ref[...].astype(F32))
        av = a_ref[...].astype(F32)
        glu_scr[...] = av * sg
        _fill_shifted(dhext, dhsh)

        def colblock(cb, carry):
            cols = pl.ds(pl.multiple_of(cb * 128, 128), 128)
            outs = _conv_taps(w_ref, dhext, dhsh, 0, ts, cb, reverse=True)
            for rb, acc in enumerate(outs):
                acc_ref[rb * 128:(rb + 1) * 128, cols] = acc
            taps = [jnp.zeros((8, 128), F32) for _ in range(CONV_WIDTH)]
            for r0 in range(0, ts, 128):
                glu = [glu_scr[r0 + 8 * i:r0 + 8 * (i + 1), cols] for i in range(16)]
                for s8 in range(8):
                    users = {}
                    for j in range(CONV_WIDTH):
                        if (CONV_WIDTH - 1 - j) % 8 == s8:
                            for i in range(16):
                                users.setdefault(CONV_WIDTH - 1 - j + r0 + 8 * i - s8, []).append((j, i))
                    for row, pairs in sorted(users.items()):
                        dhc8 = dhext[row:row + 8, cols] if s8 == 0 else dhsh[s8 - 1, row:row + 8, cols]
                        for j, i in pairs:
                            taps[j] = taps[j] + glu[i] * dhc8
            for j in range(CONV_WIDTH):
                dw_ref[8 * j:8 * (j + 1), cols] += taps[j]
            return carry

        lax.fori_loop(0, C_CONV // 128, colblock, 0)
        dglu = acc_ref[...]
        du_ref[:, :C_CONV] = (dglu * sg).astype(BF16)
        du_ref[:, C_CONV:] = (dglu * av * (sg * (1.0 - sg))).astype(BF16)

    cur = lambda cb: pl.BlockSpec((ts, C_CONV), lambda b, s: (b * nS + s, cb))
    nxt = pl.BlockSpec((HALO, C_CONV), lambda b, s: (jnp.minimum(b * (S // HALO) + (s + 1) * per32, last32), 0))
    vec = _full((1, C_CONV))
    return pl.pallas_call(
        body, name="conv_bwd", grid=(B, nS),
        in_specs=[cur(0), nxt, cur(0), nxt, cur(0), cur(1), _full((CONV_WIDTH, C_CONV)), vec, vec],
        out_specs=[pl.BlockSpec((ts, 2 * C_CONV), lambda b, s: (b * nS + s, 0)), _full((8 * CONV_WIDTH, C_CONV)), vec, vec, vec],
        out_shape=[jax.ShapeDtypeStruct((T, 2 * C_CONV), BF16), jax.ShapeDtypeStruct((8 * CONV_WIDTH, C_CONV), F32),
                   jax.ShapeDtypeStruct((1, C_CONV), F32), jax.ShapeDtypeStruct((1, C_CONV), F32),
                   jax.ShapeDtypeStruct((1, C_CONV), F32)],
        scratch_shapes=[pltpu.VMEM((ts + HALO, C_CONV), F32), pltpu.VMEM((ts, C_CONV), F32),
                        pltpu.VMEM((7, ts + HALO, C_CONV), F32), pltpu.VMEM((ts, C_CONV), F32)],
        compiler_params=_cp("arbitrary", "arbitrary"),
    )(dcp, dcp, hc, hc, u, u, conv_w, ln_g, ln_b)


def _attn_bwd_call(u, do, qg2, kg2, sinks, slopes, B, S):
    T = B * S
    nb = S // BLOCK
    half = Q_W // 2
    npair = GROUP // 2

    def body(sinks_ref, slopes_ref, q_ref, k_ref, v_ref, do_ref, qg_ref, kg_ref,
             dq_ref, dkv_ref, dqg_ref, dkg_ref, dsk_ref, kA, kB, vA, vB, dkn_acc, dv_acc, dkv_keep,
             btab, s_buf, dp_buf, p_buf, ds_buf, qn_buf, do_buf, rq_buf):
        b = pl.program_id(0)
        kvh = pl.program_id(1)
        lo = _lane_lo()
        lane = lax.broadcasted_iota(jnp.int32, (1, 128), 1)
        first = kvh == 0

        @pl.when((b == 0) & first)
        def _():
            dqg_ref[...] = jnp.zeros_like(dqg_ref)
            dkg_ref[...] = jnp.zeros_like(dkg_ref)
            dsk_ref[...] = jnp.zeros_like(dsk_ref)

        _stage_kv(kvh, k_ref, v_ref, kg_ref, lo, kA, kB, vA, vB)
        dkn_acc[...] = jnp.zeros_like(dkn_acc)
        dv_acc[...] = jnp.zeros_like(dv_acc)
        _fill_bias(btab, kvh, slopes_ref)
        qgs = qg_ref[...] * ATTN_SCALE

        def fold(acc):
            canon = jnp.where(lo, acc[0:2 * BLOCK] + pltpu.roll(acc[2 * BLOCK:], HEAD_DIM, 1), 0.0)
            return jnp.where(first, canon, pltpu.roll(canon, HEAD_DIM, 1))

        def block(n, slot):
            s_scr, dp_scr, p_scr, ds_scr = s_buf.at[slot], dp_buf.at[slot], p_buf.at[slot], ds_buf.at[slot]
            qn_scr, do_scr, rq_scr = qn_buf.at[slot], do_buf.at[slot], rq_buf.at[slot]
            r0 = pl.multiple_of(n * BLOCK, BLOCK)
            win = pl.ds(r0, 2 * BLOCK)
            rows = pl.ds(r0, BLOCK)
            k2t = jnp.concatenate([kA[win, :], kB[win, :]], axis=0)
            v2 = jnp.concatenate([vA[win, :], vB[win, :]], axis=0)
            var = jnp.minimum(n, 1)
            for p in range(npair):
                pc = slice(128 * p, 128 * (p + 1))
                qp = q_ref[rows, pc].astype(F32)
                rq = _head_rstd(qp, lo)
                rq_scr[pc, :] = rq
                qn_scr[pc, :] = (qp * rq * qgs).astype(BF16)
                do_scr[pc, :] = do_ref[rows, pc]
            s_scr[...] = _dot_nt(k2t, qn_scr[...])
            dp_scr[...] = _dot_nt(v2, do_scr[...])
            for p in range(npair):
                for hf in range(2):
                    g = 2 * p + hf
                    keys, qs = slice(2 * BLOCK * hf, 2 * BLOCK * (hf + 1)), slice(128 * p, 128 * (p + 1))
                    pn, esk, rz = _softmax_cols(s_scr, btab, p, var, g, hf, sinks_ref[kvh * GROUP + g])
                    dph = dp_scr[keys, qs]
                    delta = jnp.sum(pn * dph, axis=0, keepdims=True)
                    ds_scr[keys, qs] = (pn * (dph - delta)).astype(BF16)
                    p_scr[keys, qs] = pn.astype(BF16)
                    dsk_ref[...] += jnp.where(lane == kvh * GROUP + g, -jnp.sum(esk * rz * delta), 0.0)
            dqs_all = _dot_tn(k2t, ds_scr[...]).T
            for p in range(npair):
                pc = slice(128 * p, 128 * (p + 1))
                qp = q_ref[rows, pc].astype(F32)
                rq = rq_scr[pc, :]
                dqs = dqs_all[pc, :]
                t = dqs * qgs
                dq = rq * t - qp * (rq * rq * rq * (_half_sums(t * qp, lo) * (1.0 / HEAD_DIM)))
                dq_ref[rows, pc] = dq.astype(BF16)
                dqg_ref[...] += jnp.sum(dqs * qp * rq, axis=0, keepdims=True) * ATTN_SCALE
            dkn_acc[win, :] += fold(_dot(ds_scr[...], qn_scr[...]))
            dv_acc[win, :] += fold(_dot(p_scr[...], do_scr[...]))

        def four_blocks(i, carry):
            for slot in range(4):
                block(4 * i + slot, slot)
            return carry

        lax.fori_loop(0, nb // 4, four_blocks, 0)

        dkn = dkn_acc[BLOCK:, :]
        kv = k_ref[...].astype(F32)
        kg = kg_ref[...]
        rk = _head_rstd(kv, lo)
        t = dkn * kg
        dk = rk * t - kv * (rk * rk * rk * (_half_sums(t * kv, lo) * (1.0 / HEAD_DIM)))
        dkg_ref[...] += jnp.sum(dkn * kv * rk, axis=0, keepdims=True)
        dv = dv_acc[BLOCK:, :]

        @pl.when(first)
        def _():
            dkv_keep[:, :128] = dk
            dkv_keep[:, 128:] = dv

        @pl.when(kvh == N_KV_HEADS - 1)
        def _():
            dkv_ref[:, :128] = (dkv_keep[:, :128] + dk).astype(BF16)
            dkv_ref[:, 128:] = (dkv_keep[:, 128:] + dv).astype(BF16)

    smem = pl.BlockSpec(memory_space=pltpu.SMEM)
    kvbuf = pltpu.VMEM((S + BLOCK, 128), BF16)
    accbuf = pltpu.VMEM((S + BLOCK, 128), F32)
    vec = _full((1, 128))
    return pl.pallas_call(
        body, name="attn_bwd", grid=(B, N_KV_HEADS),
        in_specs=[smem, smem,
                  pl.BlockSpec((S, half), lambda b, h: (b, COL_Q // half + h)),
                  pl.BlockSpec((S, 128), lambda b, h: (b, COL_K // 128)),
                  pl.BlockSpec((S, 128), lambda b, h: (b, COL_V // 128)),
                  pl.BlockSpec((S, half), lambda b, h: (b, h)),
                  vec, vec],
        out_specs=[pl.BlockSpec((S, half), lambda b, h: (b, h)), pl.BlockSpec((S, 256), lambda b, h: (b, 0)), vec, vec, vec],
        out_shape=[jax.ShapeDtypeStruct((T, Q_W), BF16), jax.ShapeDtypeStruct((T, 2 * KV_W), BF16),
                   jax.ShapeDtypeStruct((1, 128), F32), jax.ShapeDtypeStruct((1, 128), F32), jax.ShapeDtypeStruct((1, 128), F32)],
        scratch_shapes=[kvbuf, kvbuf, kvbuf, kvbuf, accbuf, accbuf, pltpu.VMEM((S, 256), F32),
                        pltpu.VMEM((2, GROUP, 2 * BLOCK, BLOCK), F32),
                        pltpu.VMEM((4, npair * BLOCK, 4 * BLOCK), F32), pltpu.VMEM((4, npair * BLOCK, 4 * BLOCK), F32),
                        pltpu.VMEM((4, npair * BLOCK, 4 * BLOCK), BF16), pltpu.VMEM((4, npair * BLOCK, 4 * BLOCK), BF16),
                        pltpu.VMEM((4, npair * BLOCK, 128), BF16), pltpu.VMEM((4, npair * BLOCK, 128), BF16),
                        pltpu.VMEM((4, npair * BLOCK, 128), F32)],
        compiler_params=_cp("arbitrary", "arbitrary"),
    )(sinks, slopes, u, u, u, do, qg2, kg2)


def _inproj_bwd_call(du_conv, du_g, du_q, du_kv, w, x, g, dh, tm=512):
    T = x.shape[0]

    def body(dc_ref, dg_ref, dq_ref, dkv_ref, w_ref, x_ref, g_ref, dh_ref, gx_ref, dgm_ref):
        dxn = _dot(dc_ref[...], w_ref[COL_A:COL_GC, :])
        dxn += _dot(dg_ref[...], w_ref[COL_GC:COL_Q, :])
        dxn += _dot(dq_ref[...], w_ref[COL_Q:COL_K, :])
        dxn += _dot(dkv_ref[...], w_ref[COL_K:, :])
        dx, dgt = _rms_bwd(dxn, x_ref[...], g_ref[...])
        gx_ref[...] = dh_ref[...] + dx

        @pl.when(pl.program_id(0) == 0)
        def _():
            dgm_ref[...] = jnp.zeros_like(dgm_ref)

        dgm_ref[...] += jnp.sum(dgt, axis=0, keepdims=True)

    row = _rows(tm, D_MODEL)
    return pl.pallas_call(
        body, name="inproj_bwd", grid=(T // tm,),
        in_specs=[_rows(tm, 2 * C_CONV), _rows(tm, 2 * D_MODEL), _rows(tm, Q_W), _rows(tm, 2 * KV_W), _full(w.shape),
                  row, _full((1, D_MODEL)), row],
        out_specs=[row, _full((1, D_MODEL))],
        out_shape=[jax.ShapeDtypeStruct((T, D_MODEL), F32), jax.ShapeDtypeStruct((1, D_MODEL), F32)],
        compiler_params=_cp("arbitrary"),
    )(du_conv, du_g, du_q, du_kv, w, x, g, dh)


def _adamw_math(w, g, m, v):
    m = ADAM_B1 * m + (1.0 - ADAM_B1) * g
    v = ADAM_B2 * v + (1.0 - ADAM_B2) * (g * g)
    m_hat = m / (1.0 - ADAM_B1 ** ADAM_STEP)
    v_hat = v / (1.0 - ADAM_B2 ** ADAM_STEP)
    delta = -ADAM_LR * (m_hat / (jnp.sqrt(v_hat) + ADAM_EPS) + ADAM_WD * w)
    return delta, m, v


def _adamw_sum_call(name, parts, own, w, m, v, tr):
    R, C = w.shape

    def body(p_ref, own_ref, w_ref, m_ref, v_ref, g_ref, d_ref, nm_ref, nv_ref):
        x, y, c = _position()
        me = 4 * x + 2 * y + c
        g = jnp.zeros((tr, C), F32)
        for s in range(N_DEV):
            g = g + jnp.where(me == s, own_ref[...], p_ref[s]).astype(F32)
        g_ref[...] = g
        d_ref[...], nm_ref[...], nv_ref[...] = _adamw_math(w_ref[...], g, m_ref[...], v_ref[...])

    blk = pl.BlockSpec((tr, C), lambda i: (i, 0))
    out = jax.ShapeDtypeStruct((R, C), F32)
    return pl.pallas_call(
        body, name=name, grid=(R // tr,),
        in_specs=[pl.BlockSpec((N_DEV, tr, C), lambda i: (0, i, 0)), blk, blk, blk, blk],
        out_specs=[blk] * 4, out_shape=[out] * 4,
        compiler_params=_cp("parallel"),
    )(parts, own, w, m, v)


def _adamw_small_call(name, g, w, m, v):
    def body(g_ref, w_ref, m_ref, v_ref, d_ref, nm_ref, nv_ref):
        d_ref[...], nm_ref[...], nv_ref[...] = _adamw_math(w_ref[...], g_ref[...], m_ref[...], v_ref[...])

    out = jax.ShapeDtypeStruct(w.shape, F32)
    return pl.pallas_call(body, name=name, out_shape=[out] * 3)(g, w, m, v)


def _position():
    return lax.axis_index("x"), lax.axis_index("y"), lax.axis_index("c")


def _peer(x, y, c, rel):
    fx, fy, fc = (rel >> 2) & 1, (rel >> 1) & 1, rel & 1
    px = 1 - x if fx else x
    py = 1 - y if fy else y
    pc = 1 - c if fc else c
    return (px, py, pc), 4 * px + 2 * py + pc


def _gather_weights_call(shards):
    n = len(shards)

    def body(*refs):
        ins, outs = refs[:n], refs[n:2 * n]
        send_sems, recv_sems, local_sems = refs[2 * n:]
        x, y, c = _position()
        me = 4 * x + 2 * y + c
        sib, sib_id = _peer(x, y, c, 1)
        chips = [_peer(x, y, c, rel) for rel in (4, 2, 6)]

        def copy(i, k, src, slot, to):
            return pltpu.make_async_remote_copy(
                src_ref=src, dst_ref=outs[i].at[slot], send_sem=send_sems.at[i, k], recv_sem=recv_sems.at[i, k],
                device_id=to, device_id_type=MESH)

        mine = [pltpu.make_async_copy(ins[i], outs[i].at[me], local_sems.at[i]) for i in range(n)]
        for cp in mine:
            cp.start()
        first = []
        for i in range(n):
            first.append(copy(i, 0, ins[i], me, sib))
            first += [copy(i, 1 + j, ins[i], me, dev) for j, (dev, _) in enumerate(chips)]
        for cp in first:
            cp.start()
        passed = []
        for i in range(n):
            for j, (dev, slot) in enumerate(chips):
                copy(i, 1 + j, ins[i], slot, dev).wait_recv()
                fwd = copy(i, 4 + j, outs[i].at[slot], slot, sib)
                fwd.start()
                passed.append(fwd)
        for i in range(n):
            copy(i, 0, ins[i], sib_id, sib).wait_recv()
            for j, (_, slot) in enumerate(chips):
                copy(i, 4 + j, ins[i], slot ^ 1, sib).wait_recv()
        for cp in first + passed:
            cp.wait_send()
        for cp in mine:
            cp.wait()

    hbm = pl.BlockSpec(memory_space=pl.ANY)
    return pl.pallas_call(
        body, name="gather_weights",
        in_specs=[hbm] * n, out_specs=[hbm] * n,
        out_shape=[jax.ShapeDtypeStruct((N_DEV,) + s.shape, s.dtype) for s in shards],
        scratch_shapes=[pltpu.SemaphoreType.DMA((n, 7)), pltpu.SemaphoreType.DMA((n, 7)), pltpu.SemaphoreType.DMA((n,))],
    )(*shards)


_HBM = pl.BlockSpec(memory_space=pltpu.HBM)
_SEM = pl.BlockSpec(memory_space=pltpu.SEMAPHORE)
_EFFECT = pltpu.SideEffectType.DATAFLOW_SIDE_EFFECTING


def _exchange_copies(scatter, incoming, src_refs, land_refs, send_sems, recv_sems):
    x, y, c = _position()
    me = 4 * x + 2 * y + c
    out = []
    for i, (src, land) in enumerate(zip(src_refs, land_refs)):
        for k in range(N_DEV - 1):
            dev, dev_id = _peer(x, y, c, k + 1)
            j = i * (N_DEV - 1) + k
            out.append(pltpu.make_async_remote_copy(
                src_ref=src.at[dev_id] if scatter else src, dst_ref=land.at[dev_id if incoming else me],
                send_sem=send_sems.at[j], recv_sem=recv_sems.at[j], device_id=dev, device_id_type=MESH))
    return out


def _exchange_start_call(name, arrays, scatter):
    n = len(arrays)
    lands = [lax.empty((N_DEV,) + (a.shape[1:] if scatter else a.shape), a.dtype) for a in arrays]

    def body(*refs):
        src_refs, land_refs = refs[:n], refs[n:2 * n]
        send_sems, recv_sems, token = refs[2 * n], refs[2 * n + 1], refs[-1]
        for copy in _exchange_copies(scatter, False, src_refs, land_refs, send_sems, recv_sems):
            copy.start()
        token[...] = jnp.zeros_like(token)

    sems = pltpu.SemaphoreType.DMA((n * (N_DEV - 1),))
    res = pl.pallas_call(
        body, name=name,
        out_shape=[sems, sems] + [pltpu.HBM(a.shape, a.dtype) for a in arrays] + [pltpu.HBM(l.shape, l.dtype) for l in lands]
        + [jax.ShapeDtypeStruct((8, 128), F32)],
        in_specs=[_HBM] * (2 * n), out_specs=[_SEM, _SEM] + [_HBM] * (2 * n) + [pl.BlockSpec(memory_space=pltpu.VMEM)],
        input_output_aliases={i: 2 + i for i in range(2 * n)},
        compiler_params=pltpu.CompilerParams(has_side_effects=_EFFECT),
    )(*[pltpu.with_memory_space_constraint(a, pltpu.HBM) for a in list(arrays) + lands])
    return res[0], res[1], res[2:2 + n], res[2 + n:2 + 2 * n], res[-1]


def _exchange_wait_call(name, started, scatter, after):
    send_sems, recv_sems, srcs, lands, _ = started
    n = len(srcs)

    def body(*refs):
        src_refs, land_refs = refs[:n], refs[n:2 * n]
        send_s, recv_s = refs[2 * n], refs[2 * n + 1]
        for copy in _exchange_copies(scatter, False, src_refs, land_refs, send_s, recv_s):
            copy.wait_send()
        for copy in _exchange_copies(scatter, True, src_refs, land_refs, send_s, recv_s):
            copy.wait_recv()

    res = pl.pallas_call(
        body, name=name,
        out_shape=[pltpu.HBM(a.shape, a.dtype) for a in list(srcs) + list(lands)],
        in_specs=[_HBM] * (2 * n) + [_SEM, _SEM, pl.BlockSpec(memory_space=pl.ANY)], out_specs=[_HBM] * (2 * n),
        input_output_aliases={i: i for i in range(2 * n)},
        compiler_params=pltpu.CompilerParams(has_side_effects=_EFFECT),
    )(*srcs, *lands, send_sems, recv_sems, after)
    return res[n:]


def _allsum_small_call(part, blocks):
    P, Q = part.shape[0], blocks.shape[1]

    def body(p_ref, b_ref, o_ref, ob_ref, all_ref, allb_ref, send_sems, recv_sems):
        x, y, c = _position()
        me = 4 * x + 2 * y + c
        peers = [_peer(x, y, c, rel) for rel in range(1, N_DEV)]
        all_ref[me] = p_ref[...]
        allb_ref[me] = b_ref[me]

        def copies(k, slot, dev, dev_id):
            sems = lambda i: dict(send_sem=send_sems.at[i], recv_sem=recv_sems.at[i], device_id=dev, device_id_type=MESH)
            return (pltpu.make_async_remote_copy(src_ref=p_ref, dst_ref=all_ref.at[slot], **sems(k)),
                    pltpu.make_async_remote_copy(src_ref=b_ref.at[dev_id], dst_ref=allb_ref.at[slot], **sems(7 + k)))

        sends = [cp for k, (dev, dev_id) in enumerate(peers) for cp in copies(k, me, dev, dev_id)]
        for cp in sends:
            cp.start()
        for k, (dev, dev_id) in enumerate(peers):
            for cp in copies(k, dev_id, dev, dev_id):
                cp.wait_recv()
        for cp in sends:
            cp.wait_send()
        tot, totb = all_ref[0], allb_ref[0]
        for s in range(1, N_DEV):
            tot, totb = tot + all_ref[s], totb + allb_ref[s]
        o_ref[...] = tot
        ob_ref[...] = totb

    vm = pl.BlockSpec(memory_space=pltpu.VMEM)
    return pl.pallas_call(
        body, name="allsum_small", in_specs=[vm, vm], out_specs=[vm, vm],
        out_shape=[jax.ShapeDtypeStruct((P, 128), F32), jax.ShapeDtypeStruct((Q, 128), F32)],
        scratch_shapes=[pltpu.VMEM((N_DEV, P, 128), F32), pltpu.VMEM((N_DEV, Q, 128), F32),
                        pltpu.SemaphoreType.DMA((14,)), pltpu.SemaphoreType.DMA((14,))],
    )(part, blocks)


def _pad_rows(v, rows):
    flat = v.reshape(-1)
    return jnp.pad(flat, (0, rows * 128 - flat.shape[0])).reshape(rows, 128)


def _local_step(x, tgt, w_in_i, conv_w, conv_b, ln_g, ln_b, qg, kg, sinks, g_mix, g_ffn, token0, rest_weights, grads_ready, B, S):
    row = lambda v: v.reshape(1, -1)
    qg2 = jnp.tile(qg, 2).reshape(1, 128)
    kg2 = jnp.tile(kg, 2).reshape(1, 128)
    heads = jnp.arange(1, N_Q_HEADS + 1, dtype=F32)
    slopes = jnp.exp2(-8.0 * heads / N_Q_HEADS)

    u, xn = _inproj_call(x, row(g_mix) + token0, w_in_i)
    hc, c = _conv_fwd_call(u, conv_w, row(conv_b), row(ln_g), row(ln_b), B, S)
    o = _attn_fwd_call(u, qg2, kg2, sinks, slopes, B, S)
    wc, wa, wm, w_ffn_in, wd = rest_weights(o)
    yc, ya, z, h, hn = _merge_call(c, o, u, x, wc, wa, wm, row(g_ffn))
    gu, act = _ffn_in_call(hn, w_ffn_in)
    sse, dout, dout_b = _ffn_out_call(act, wd, h, tgt)

    dgu = _ffn_down_bwd_call(dout_b, wd, gu)
    g_wd = _dw_call("dw_ffn_down", act, dout_b, tr=D_FF // 2)
    g_wffn_t = _dw_call("dw_ffn_in", dgu, hn, tr=2 * D_FF // 4)
    tok = grads_ready(dict(w_ffn_down=g_wd, w_ffn_in=g_wffn_t))
    dh, dh_b, g_gffn = _ffn_in_bwd_call(dgu, w_ffn_in, h, row(g_ffn) + tok, dout)
    dyc, dya, du_g, dcp, do = _merge_bwd_call(dh_b, wm, wc, wa, u, yc, ya)
    g_wm = _dw_call("dw_merge", z, dh_b, tr=D_MODEL, tk=4096)
    g_wc = _dw_call("dw_conv_out", c, dyc, tr=D_MODEL, tk=4096)
    g_wa = _dw_call("dw_attn_out", o, dya, tr=D_MODEL, tk=4096)
    tok = grads_ready(dict(w_merge_out=g_wm, w_conv_out=g_wc, w_attn_out=g_wa))
    du_conv, g_cw8, g_cb, g_lg, g_lb = _conv_bwd_call(dcp, hc, u, conv_w, row(ln_g) + tok, row(ln_b), B, S)
    du_q, du_kv, g_qg2, g_kg2, g_sk = _attn_bwd_call(u, do, qg2, kg2, sinks, slopes, B, S)
    g_win_t = _concat_rows_call("dw_in_rows", [
        _dw_call("dw_in_conv", du_conv, xn, tr=D_MODEL, tk=4096),
        _dw_call("dw_in_q", du_q, xn, tr=D_MODEL, tk=4096),
        _dw_call("dw_in_kv", du_kv, xn, tr=2 * KV_W),
        _dw_call("dw_in_gates", du_g, xn, tr=D_MODEL, tk=4096)])
    tok = grads_ready(dict(w_in=g_win_t))
    grad_x, g_gmix = _inproj_bwd_call(du_conv, du_g, du_q, du_kv, w_in_i, x, row(g_mix) + tok, dh)

    g_cw = g_cw8.reshape(CONV_WIDTH, 8, C_CONV).sum(axis=1)
    g_qg = g_qg2[0, :HEAD_DIM] + g_qg2[0, HEAD_DIM:]
    g_kg = g_kg2[0, :HEAD_DIM] + g_kg2[0, HEAD_DIM:]
    small = dict(norm_mix_g=g_gmix[0], conv_dw_b=g_cb[0], conv_ln_g=g_lg[0], conv_ln_b=g_lb[0], q_norm_g=g_qg,
                 k_norm_g=g_kg, sinks=g_sk[0, :N_Q_HEADS], norm_ffn_g=g_gffn[0], conv_dw_w=g_cw)
    return sse[0, 0], grad_x, small


def _to_internal_rows(wt):
    o1 = 2 * C_CONV
    o4 = o1 + Q_W + 2 * KV_W
    return jnp.concatenate([wt[:o1], wt[o4:], wt[o1:o4]], axis=0)


_COL_SHARDED = ("w_in", "w_ffn_in")

_SMALL = (("norm_mix_g", 8), ("conv_dw_b", 8), ("conv_ln_g", 8), ("conv_ln_b", 8), ("norm_ffn_g", 8),
          ("q_norm_g", 1), ("k_norm_g", 1), ("sinks", 1))
_SMALL_ROWS = 48


def _pack_small(d):
    rows = [_pad_rows(d[name], r) for name, r in _SMALL]
    used = sum(r for _, r in _SMALL)
    return jnp.concatenate(rows + [jnp.zeros((_SMALL_ROWS - used, 128), F32)], axis=0)


def _unpack_small(packed, like):
    out, r0 = {}, 0
    for name, r in _SMALL:
        n = like[name].size
        out[name] = packed[r0:r0 + r].reshape(-1)[:n].reshape(like[name].shape)
        r0 += r
    return out


def kernel(x, norm_mix_g, w_in, conv_dw_w, conv_dw_b, conv_ln_g, conv_ln_b, w_conv_out, q_norm_g, k_norm_g, sinks, w_attn_out, w_merge_out, norm_ffn_g, w_ffn_in, w_ffn_down, loss_target, m_norm_mix_g, m_w_in, m_conv_dw_w, m_conv_dw_b, m_conv_ln_g, m_conv_ln_b, m_w_conv_out, m_q_norm_g, m_k_norm_g, m_sinks, m_w_attn_out, m_w_merge_out, m_norm_ffn_g, m_w_ffn_in, m_w_ffn_down, v_norm_mix_g, v_w_in, v_conv_dw_w, v_conv_dw_b, v_conv_ln_g, v_conv_ln_b, v_w_conv_out, v_q_norm_g, v_k_norm_g, v_sinks, v_w_attn_out, v_w_merge_out, v_norm_ffn_g, v_w_ffn_in, v_w_ffn_down):
    names = ["norm_mix_g", "w_in", "conv_dw_w", "conv_dw_b", "conv_ln_g", "conv_ln_b", "w_conv_out", "q_norm_g",
             "k_norm_g", "sinks", "w_attn_out", "w_merge_out", "norm_ffn_g", "w_ffn_in", "w_ffn_down"]
    w = dict(zip(names, (norm_mix_g, w_in, conv_dw_w, conv_dw_b, conv_ln_g, conv_ln_b, w_conv_out, q_norm_g, k_norm_g,
                         sinks, w_attn_out, w_merge_out, norm_ffn_g, w_ffn_in, w_ffn_down)))
    m = dict(zip(names, (m_norm_mix_g, m_w_in, m_conv_dw_w, m_conv_dw_b, m_conv_ln_g, m_conv_ln_b, m_w_conv_out,
                         m_q_norm_g, m_k_norm_g, m_sinks, m_w_attn_out, m_w_merge_out, m_norm_ffn_g, m_w_ffn_in,
                         m_w_ffn_down)))
    v = dict(zip(names, (v_norm_mix_g, v_w_in, v_conv_dw_w, v_conv_dw_b, v_conv_ln_g, v_conv_ln_b, v_w_conv_out,
                         v_q_norm_g, v_k_norm_g, v_sinks, v_w_attn_out, v_w_merge_out, v_norm_ffn_g, v_w_ffn_in,
                         v_w_ffn_down)))
    B, S, _ = x.shape
    T = B * S
    me = 4 * lax.axis_index("x") + 2 * lax.axis_index("y") + lax.axis_index("c")

    as_rows = lambda k, t: t.T if k in _COL_SHARDED else t
    cw_shard = _pad_rows(conv_dw_w.reshape(CONV_WIDTH, 128), 32)
    g_win, g_cwt = _gather_weights_call([w_in.astype(BF16), cw_shard])
    conv_w_full = g_cwt.reshape(N_DEV, 32 * 128)[:, :CONV_WIDTH * 128].reshape(N_DEV, CONV_WIDTH, 128)
    conv_w_full = conv_w_full.transpose(1, 0, 2).reshape(CONV_WIDTH, C_CONV)
    w_in_i = _to_internal_rows(g_win.transpose(0, 2, 1).reshape(IN_COLS, D_MODEL))
    rest_names = ["w_conv_out", "w_attn_out", "w_merge_out", "w_ffn_in", "w_ffn_down"]
    rest_shards = [as_rows(k, w[k]).astype(BF16) for k in rest_names]
    rest_started = _exchange_start_call("gather_rest_start", rest_shards, scatter=False)

    def rest_weights(after):
        lands = _exchange_wait_call("gather_rest_wait", rest_started, False, after)
        mine = lax.broadcasted_iota(jnp.int32, (N_DEV, 1, 1), 0) == me
        full = [jnp.where(mine, s[None], l) for l, s in zip(lands, rest_shards)]
        return tuple(f.reshape(N_DEV * f.shape[1], D_MODEL) for f in full)

    exchanges = []

    def grads_ready(group):
        ks = list(group)
        blocks = [group[k].reshape(N_DEV, group[k].shape[0] // N_DEV, D_MODEL) for k in ks]
        started = _exchange_start_call("scatter_" + ks[0] + "_start", blocks, scatter=True)
        exchanges.append((ks, started, [lax.dynamic_index_in_dim(b, me, 0, keepdims=False) for b in blocks]))
        return started[-1][0:1, 0:1]

    sse, grad_x, small = _local_step(
        x.reshape(T, D_MODEL), loss_target.reshape(T, D_MODEL), w_in_i, conv_w_full, conv_dw_b, conv_ln_g, conv_ln_b,
        q_norm_g, k_norm_g, sinks, norm_mix_g, norm_ffn_g, rest_started[-1][0:1, 0:1], rest_weights, grads_ready, B, S)

    grad, delta, new_m, new_v = {}, {}, {}, {}
    packed = jnp.concatenate([_pack_small(small), jnp.full((8, 128), sse, F32)], axis=0)
    cw_blocks = small["conv_dw_w"].reshape(CONV_WIDTH, N_DEV, 128).transpose(1, 0, 2)
    tot, cw_tot = _allsum_small_call(packed, jnp.pad(cw_blocks, ((0, 0), (0, 32 - CONV_WIDTH), (0, 0))))
    loss = 0.5 / D_MODEL * tot[_SMALL_ROWS, 0]
    g_small = _unpack_small(tot[:_SMALL_ROWS], w)
    d_s, m_s, v_s = _adamw_small_call("adamw_small", tot[:_SMALL_ROWS], _pack_small(w), _pack_small(m), _pack_small(v))
    d_small, m_small, v_small = _unpack_small(d_s, w), _unpack_small(m_s, w), _unpack_small(v_s, w)
    g_cw = cw_tot[:CONV_WIDTH]
    cw2 = lambda t: t.reshape(CONV_WIDTH, 128)
    d_cw, m_cw, v_cw = _adamw_small_call("adamw_conv_w", g_cw, cw2(conv_dw_w), cw2(m_conv_dw_w), cw2(v_conv_dw_w))
    cw3 = lambda t: t.reshape(CONV_WIDTH, 1, 128)
    for k, _ in _SMALL:
        grad[k], delta[k], new_m[k], new_v[k] = g_small[k], d_small[k], m_small[k], v_small[k]
    grad["conv_dw_w"], delta["conv_dw_w"], new_m["conv_dw_w"], new_v["conv_dw_w"] = cw3(g_cw), cw3(d_cw), cw3(m_cw), cw3(v_cw)

    after = d_cw
    for ks, started, owns in exchanges:
        lands = _exchange_wait_call("scatter_" + ks[0] + "_wait", started, True, after)
        for k, land, own in zip(ks, lands, owns):
            tr = {672: 336, 704: 352, 352: 176, 128: 128}[own.shape[0]]
            outs = _adamw_sum_call("adamw_" + k, land, own, as_rows(k, w[k]), as_rows(k, m[k]), as_rows(k, v[k]), tr)
            grad[k], delta[k], new_m[k], new_v[k] = (as_rows(k, t) for t in outs)
            after = outs[3]

    return (loss, grad_x.reshape(B, S, D_MODEL), *[grad[k] for k in names], *[delta[k] for k in names],
            *[new_m[k] for k in names], *[new_v[k] for k in names])
```

```python
import functools
import math

import jax
import jax.numpy as jnp
from jax import lax
from jax.experimental import pallas as pl
from jax.experimental.pallas import tpu as pltpu

F32 = jnp.float32
BF16 = jnp.bfloat16
MESH = pl.DeviceIdType.MESH

N_DEV = 8
D_MODEL = 1024
C_CONV = 1024
CONV_WIDTH = 31
HEAD_DIM = 64
N_Q_HEADS = 16
N_KV_HEADS = 2
GROUP = N_Q_HEADS // N_KV_HEADS
WINDOW = 128
BLOCK = 128
D_FF = 2816
EPS = 1e-6
NEG = -1e30
ATTN_SCALE = 1.0 / math.sqrt(HEAD_DIM)
Q_W = N_Q_HEADS * HEAD_DIM
KV_W = N_KV_HEADS * HEAD_DIM
IN_COLS = 2 * C_CONV + Q_W + 2 * KV_W + 2 * D_MODEL

ADAM_LR = 0.001
ADAM_B1 = 0.9
ADAM_B2 = 0.999
ADAM_EPS = 1e-08
ADAM_WD = 0.01
ADAM_STEP = 10

COL_A, COL_GATE, COL_GC, COL_GA, COL_Q, COL_K, COL_V = 0, 1024, 2048, 3072, 4096, 5120, 5248

HALO = 32
VMEM_LIMIT = 56 * 1024 * 1024


def _cp(*sem):
    return pltpu.CompilerParams(dimension_semantics=sem, vmem_limit_bytes=VMEM_LIMIT)


def _sigmoid(x):
    return jax.nn.sigmoid(x)


def _dot(a, b):
    return jnp.dot(a, b, preferred_element_type=F32)


def _dot_nt(a, b):
    return lax.dot_general(a, b, (((1,), (1,)), ((), ())), preferred_element_type=F32)


def _dot_tn(a, b):
    return lax.dot_general(a, b, (((0,), (0,)), ((), ())), preferred_element_type=F32)


def _full(shape):
    return pl.BlockSpec(shape, lambda *_: (0,) * len(shape))


def _rows(tm, ncols, colblk=0):
    return pl.BlockSpec((tm, ncols), lambda i: (i, colblk))


def _inproj_call(x, g, w, tm=512):
    T = x.shape[0]
    N = w.shape[0]

    def body(x_ref, g_ref, w_ref, u_ref, xn_ref):
        xv = x_ref[...]
        r = lax.rsqrt(jnp.mean(xv * xv, axis=-1, keepdims=True) + EPS)
        xn = (xv * r * g_ref[...]).astype(BF16)
        xn_ref[...] = xn
        u_ref[...] = _dot_nt(xn, w_ref[...]).astype(BF16)

    return pl.pallas_call(
        body, name="inproj", grid=(T // tm,),
        in_specs=[_rows(tm, D_MODEL), _full((1, D_MODEL)), _full(w.shape)],
        out_specs=[_rows(tm, N), _rows(tm, D_MODEL)],
        out_shape=[jax.ShapeDtypeStruct((T, N), BF16), jax.ShapeDtypeStruct((T, D_MODEL), BF16)],
        compiler_params=_cp("parallel"),
    )(x, g, w)


def _fill_shifted(src_ref, sh_ref):
    n = src_ref.shape[0] - 8
    for s in range(1, 8):
        sh_ref[s - 1, 0:n, :] = src_ref[s:s + n, :]


def _window(src_ref, sh_ref, off, cols):
    s = off % 8
    if s == 0:
        return src_ref[off:off + 128, cols]
    return sh_ref[s - 1, off - s:off - s + 128, cols]


def _conv_taps(w_ref, src_ref, sh_ref, base, rows, cb, reverse):
    cols = pl.ds(pl.multiple_of(cb * 128, 128), 128)
    outs = []
    for rb in range(rows // 128):
        acc = jnp.zeros((128, 128), F32)
        for j in range(CONV_WIDTH):
            off = base + (CONV_WIDTH - 1 - j if reverse else j) + rb * 128
            acc = acc + w_ref[j:j + 1, cols] * _window(src_ref, sh_ref, off, cols)
        outs.append(acc)
    return outs


def _conv_fwd_call(u, conv_w, conv_b, ln_g, ln_b, B, S, ts=512):
    T = B * S
    nS = S // ts
    per32 = ts // HALO

    def body(a_ref, gate_ref, ap_ref, gp_ref, w_ref, b_ref, g_ref, bb_ref, hc_ref, c_ref, hext, hsh, acc_ref):
        s = pl.program_id(1)
        prev = ap_ref[...].astype(F32) * _sigmoid(gp_ref[...].astype(F32))
        hext[0:HALO, :] = jnp.where(s > 0, prev, 0.0)
        hext[HALO:, :] = a_ref[...].astype(F32) * _sigmoid(gate_ref[...].astype(F32))
        _fill_shifted(hext, hsh)

        def colblock(cb, carry):
            cols = pl.ds(pl.multiple_of(cb * 128, 128), 128)
            outs = _conv_taps(w_ref, hext, hsh, HALO - (CONV_WIDTH - 1), ts, cb, reverse=False)
            for rb, acc in enumerate(outs):
                acc_ref[rb * 128:(rb + 1) * 128, cols] = acc
            return carry

        lax.fori_loop(0, C_CONV // 128, colblock, 0)
        hc = acc_ref[...] + b_ref[...]
        hc_ref[...] = hc
        mu = jnp.mean(hc, axis=-1, keepdims=True)
        xc = hc - mu
        var = jnp.mean(xc * xc, axis=-1, keepdims=True)
        y = xc * lax.rsqrt(var + EPS) * g_ref[...] + bb_ref[...]
        c_ref[...] = (y * _sigmoid(y)).astype(BF16)

    cur = lambda cb: pl.BlockSpec((ts, C_CONV), lambda b, s: (b * nS + s, cb))
    prv = lambda cb: pl.BlockSpec((HALO, C_CONV), lambda b, s: (jnp.maximum(b * (S // HALO) + s * per32 - 1, 0), cb))
    vec = _full((1, C_CONV))
    return pl.pallas_call(
        body, name="conv_fwd", grid=(B, nS),
        in_specs=[cur(0), cur(1), prv(0), prv(1), _full((CONV_WIDTH, C_CONV)), vec, vec, vec],
        out_specs=[pl.BlockSpec((ts, C_CONV), lambda b, s: (b * nS + s, 0))] * 2,
        out_shape=[jax.ShapeDtypeStruct((T, C_CONV), F32), jax.ShapeDtypeStruct((T, C_CONV), BF16)],
        scratch_shapes=[pltpu.VMEM((HALO + ts, C_CONV), F32), pltpu.VMEM((7, HALO + ts, C_CONV), F32),
                        pltpu.VMEM((ts, C_CONV), F32)],
        compiler_params=_cp("parallel", "arbitrary"),
    )(u, u, u, u, conv_w, conv_b, ln_g, ln_b)


def _lane_lo():
    return lax.broadcasted_iota(jnp.int32, (1, 128), 1) < HEAD_DIM


def _half_sums(t, lo):
    s_lo = jnp.sum(jnp.where(lo, t, 0.0), axis=-1, keepdims=True)
    s_hi = jnp.sum(jnp.where(lo, 0.0, t), axis=-1, keepdims=True)
    return jnp.where(lo, s_lo, s_hi)


def _head_rstd(t, lo):
    return lax.rsqrt(_half_sums(t * t, lo) * (1.0 / HEAD_DIM) + EPS)


def _fill_bias(btab, kvh, slopes_ref):
    sj = lax.broadcasted_iota(jnp.int32, (2 * BLOCK, BLOCK), 0)
    qi = lax.broadcasted_iota(jnp.int32, (2 * BLOCK, BLOCK), 1)
    dist = qi + BLOCK - sj
    valid = (dist >= 0) & (dist < WINDOW)
    distf = dist.astype(F32)
    for g in range(GROUP):
        bias = jnp.where(valid, -slopes_ref[kvh * GROUP + g] * distf, NEG)
        btab[1, g] = bias
        btab[0, g] = jnp.where(sj >= BLOCK, bias, NEG)


def _softmax_cols(st_ref, btab, p, var, g, hf, sk):
    sh = st_ref[2 * BLOCK * hf:2 * BLOCK * (hf + 1), 128 * p:128 * (p + 1)] + btab[var, g]
    m = jnp.maximum(jnp.max(sh, axis=0, keepdims=True), sk)
    e = jnp.exp(sh - m)
    esk = jnp.exp(sk - m)
    rz = 1.0 / (jnp.sum(e, axis=0, keepdims=True) + esk)
    return e * rz, esk, rz


def _stage_kv(kvh, k_ref, v_ref, kg_ref, lo, kA, kB, vA, vB):
    kv = k_ref[...].astype(F32)
    rk = _head_rstd(kv, lo)
    kn = kv * rk * kg_ref[...]
    mine = lax.broadcasted_iota(jnp.int32, (1, 128), 1) // HEAD_DIM == kvh
    ksel = jnp.where(mine, kn, 0.0)
    vsel = jnp.where(mine, v_ref[...].astype(F32), 0.0)
    krol = pltpu.roll(ksel, HEAD_DIM, 1)
    vrol = pltpu.roll(vsel, HEAD_DIM, 1)
    first = kvh == 0
    zeros = jnp.zeros((BLOCK, 128), BF16)
    for ref, val in ((kA, jnp.where(first, ksel, krol)), (kB, jnp.where(first, krol, ksel)),
                     (vA, jnp.where(first, vsel, vrol)), (vB, jnp.where(first, vrol, vsel))):
        ref[0:BLOCK, :] = zeros
        ref[BLOCK:, :] = val.astype(BF16)


def _attn_fwd_call(u, qg2, kg2, sinks, slopes, B, S):
    T = B * S
    nb = S // BLOCK
    half = Q_W // 2
    npair = GROUP // 2
    per_trip = 8 if nb % 8 == 0 else 4

    def body(sinks_ref, slopes_ref, q_ref, k_ref, v_ref, qg_ref, kg_ref, o_ref, kA, kB, vA, vB, btab, qn_buf, s_buf, p_buf):
        kvh = pl.program_id(1)
        lo = _lane_lo()
        _stage_kv(kvh, k_ref, v_ref, kg_ref, lo, kA, kB, vA, vB)
        _fill_bias(btab, kvh, slopes_ref)
        qgs = qg_ref[...] * ATTN_SCALE

        def block(n, slot):
            qn_scr, s_scr, p_scr = qn_buf.at[slot], s_buf.at[slot], p_buf.at[slot]
            r0 = pl.multiple_of(n * BLOCK, BLOCK)
            win = pl.ds(r0, 2 * BLOCK)
            k2t = jnp.concatenate([kA[win, :], kB[win, :]], axis=0)
            v2 = jnp.concatenate([vA[win, :], vB[win, :]], axis=0)
            var = jnp.minimum(n, 1)
            for p in range(npair):
                pc = slice(128 * p, 128 * (p + 1))
                qp = q_ref[pl.ds(r0, BLOCK), pc].astype(F32)
                qn_scr[pc, :] = (qp * _head_rstd(qp, lo) * qgs).astype(BF16)
            s_scr[...] = _dot_nt(k2t, qn_scr[...])
            for p in range(npair):
                for hf in range(2):
                    g = 2 * p + hf
                    pn, _, _ = _softmax_cols(s_scr, btab, p, var, g, hf, sinks_ref[kvh * GROUP + g])
                    p_scr[2 * BLOCK * hf:2 * BLOCK * (hf + 1), 128 * p:128 * (p + 1)] = pn.astype(BF16)
            o_all = _dot_tn(v2, p_scr[...]).T
            for p in range(npair):
                pc = slice(128 * p, 128 * (p + 1))
                o_ref[pl.ds(r0, BLOCK), pc] = o_all[pc, :].astype(BF16)

        def trip(i, carry):
            for slot in range(per_trip):
                block(per_trip * i + slot, slot)
            return carry

        lax.fori_loop(0, nb // per_trip, trip, 0)

    smem = pl.BlockSpec(memory_space=pltpu.SMEM)
    kvbuf = pltpu.VMEM((S + BLOCK, 128), BF16)
    pair_scores = (per_trip, npair * BLOCK, 4 * BLOCK)
    return pl.pallas_call(
        body, name="attn_fwd", grid=(B, N_KV_HEADS),
        in_specs=[smem, smem,
                  pl.BlockSpec((S, half), lambda b, h: (b, COL_Q // half + h)),
                  pl.BlockSpec((S, 128), lambda b, h: (b, COL_K // 128)),
                  pl.BlockSpec((S, 128), lambda b, h: (b, COL_V // 128)),
                  _full((1, 128)), _full((1, 128))],
        out_specs=pl.BlockSpec((S, half), lambda b, h: (b, h)),
        out_shape=jax.ShapeDtypeStruct((T, Q_W), BF16),
        scratch_shapes=[kvbuf, kvbuf, kvbuf, kvbuf, pltpu.VMEM((2, GROUP, 2 * BLOCK, BLOCK), F32),
                        pltpu.VMEM((per_trip, npair * BLOCK, 128), BF16), pltpu.VMEM(pair_scores, F32),
                        pltpu.VMEM(pair_scores, BF16)],
        compiler_params=_cp("parallel", "arbitrary"),
    )(sinks, slopes, u, u, u, qg2, kg2)


def _merge_call(c, o, u, x, wc, wa, wm, g_ffn, tm=512):
    T = x.shape[0]

    def body(c_ref, o_ref, gc_ref, ga_ref, x_ref, wc_ref, wa_ref, wm_ref, g_ref, yc_ref, ya_ref, z_ref, h_ref, hn_ref):
        yc = _dot(c_ref[...], wc_ref[...])
        ya = _dot(o_ref[...], wa_ref[...])
        yc_ref[...] = yc.astype(BF16)
        ya_ref[...] = ya.astype(BF16)
        z = (_sigmoid(gc_ref[...].astype(F32)) * yc + _sigmoid(ga_ref[...].astype(F32)) * ya).astype(BF16)
        z_ref[...] = z
        h = x_ref[...] + _dot(z, wm_ref[...])
        h_ref[...] = h
        r = lax.rsqrt(jnp.mean(h * h, axis=-1, keepdims=True) + EPS)
        hn_ref[...] = (h * r * g_ref[...]).astype(BF16)

    sq = _full((D_MODEL, D_MODEL))
    row = _rows(tm, D_MODEL)
    f32 = jax.ShapeDtypeStruct((T, D_MODEL), F32)
    b16 = jax.ShapeDtypeStruct((T, D_MODEL), BF16)
    return pl.pallas_call(
        body, name="merge_fwd", grid=(T // tm,),
        in_specs=[row, row, _rows(tm, D_MODEL, COL_GC // D_MODEL), _rows(tm, D_MODEL, COL_GA // D_MODEL), row,
                  sq, sq, sq, _full((1, D_MODEL))],
        out_specs=[row] * 5,
        out_shape=[b16, b16, b16, f32, b16],
        compiler_params=_cp("parallel"),
    )(c, o, u, u, x, wc, wa, wm, g_ffn)


def _ffn_in_call(hn, w, tm=512):
    T = hn.shape[0]

    def body(hn_ref, w_ref, fac_ref, act_ref):
        gu = _dot_nt(hn_ref[...], w_ref[...])
        g = gu[:, :D_FF]
        up = gu[:, D_FF:]
        sg = _sigmoid(g)
        silu = g * sg
        act_ref[...] = (silu * up).astype(BF16)
        fac_ref[:, :D_FF] = (up * (sg * (1.0 + g * (1.0 - sg)))).astype(BF16)
        fac_ref[:, D_FF:] = silu.astype(BF16)

    return pl.pallas_call(
        body, name="ffn_in", grid=(T // tm,),
        in_specs=[_rows(tm, D_MODEL), _full(w.shape)],
        out_specs=[_rows(tm, 2 * D_FF), _rows(tm, D_FF)],
        out_shape=[jax.ShapeDtypeStruct((T, 2 * D_FF), BF16), jax.ShapeDtypeStruct((T, D_FF), BF16)],
        compiler_params=_cp("parallel"),
    )(hn, w)


def _ffn_out_call(act, wd, h, tgt, tm=512):
    T = h.shape[0]

    def body(act_ref, wd_ref, h_ref, t_ref, sse_ref, d_ref, db_ref):
        e = h_ref[...] + _dot(act_ref[...], wd_ref[...]) - t_ref[...]

        @pl.when(pl.program_id(0) == 0)
        def _():
            sse_ref[...] = jnp.zeros_like(sse_ref)

        sse_ref[...] += jnp.sum(e * e)
        d = e * (1.0 / D_MODEL)
        d_ref[...] = d
        db_ref[...] = d.astype(BF16)

    row = _rows(tm, D_MODEL)
    return pl.pallas_call(
        body, name="ffn_out", grid=(T // tm,),
        in_specs=[_rows(tm, D_FF), _full(wd.shape), row, row],
        out_specs=[_full((8, 128)), row, row],
        out_shape=[jax.ShapeDtypeStruct((8, 128), F32), jax.ShapeDtypeStruct((T, D_MODEL), F32),
                   jax.ShapeDtypeStruct((T, D_MODEL), BF16)],
        compiler_params=_cp("arbitrary"),
    )(act, wd, h, tgt)


def _dw_call(name, a, b, tr, tk=2048):
    T, K = a.shape
    N = b.shape[1]
    tk = min(tk, T)
    nk = T // tk

    def body(a_ref, b_ref, o_ref, acc_ref):
        k = pl.program_id(1)

        @pl.when(k == 0)
        def _():
            acc_ref[...] = jnp.zeros_like(acc_ref)

        acc_ref[...] += _dot_tn(a_ref[...], b_ref[...])

        @pl.when(k == nk - 1)
        def _():
            o_ref[...] = acc_ref[...].astype(BF16)

    return pl.pallas_call(
        body, name=name, grid=(K // tr, nk),
        in_specs=[pl.BlockSpec((tk, tr), lambda j, k: (k, j)), pl.BlockSpec((tk, N), lambda j, k: (k, 0))],
        out_specs=pl.BlockSpec((tr, N), lambda j, k: (j, 0)),
        out_shape=jax.ShapeDtypeStruct((K, N), BF16),
        scratch_shapes=[pltpu.VMEM((tr, N), F32)],
        compiler_params=_cp("parallel", "arbitrary"),
    )(a, b)


def _concat_rows_call(name, pieces, tr=256):
    N = pieces[0].shape[1]
    starts = [0]
    for p in pieces:
        starts.append(starts[-1] + p.shape[0] // tr)

    def body(*refs):
        o_ref = refs[-1]
        i = pl.program_id(0)
        for p, ref in enumerate(refs[:-1]):
            @pl.when((i >= starts[p]) & (i < starts[p + 1]))
            def _(ref=ref):
                o_ref[...] = ref[...]

    def spec(p):
        lo, n = starts[p], pieces[p].shape[0] // tr
        return pl.BlockSpec((tr, N), lambda i: (jnp.clip(i - lo, 0, n - 1), 0))

    return pl.pallas_call(
        body, name=name, grid=(starts[-1],),
        in_specs=[spec(p) for p in range(len(pieces))], out_specs=pl.BlockSpec((tr, N), lambda i: (i, 0)),
        out_shape=jax.ShapeDtypeStruct((starts[-1] * tr, N), pieces[0].dtype),
        compiler_params=_cp("arbitrary"),
    )(*pieces)


def _rms_bwd(dy, xv, g):
    r = lax.rsqrt(jnp.mean(xv * xv, axis=-1, keepdims=True) + EPS)
    t = dy * g
    dx = r * t - xv * (r * r * r * jnp.mean(t * xv, axis=-1, keepdims=True))
    return dx, dy * xv * r


def _ffn_down_bwd_call(dout_b, wd, gu, tm=512):
    T = dout_b.shape[0]

    def body(d_ref, wd_ref, fg_ref, fu_ref, o_ref):
        dact = _dot_nt(d_ref[...], wd_ref[...])
        o_ref[:, :D_FF] = (dact * fg_ref[...].astype(F32)).astype(BF16)
        o_ref[:, D_FF:] = (dact * fu_ref[...].astype(F32)).astype(BF16)

    return pl.pallas_call(
        body, name="ffn_down_bwd", grid=(T // tm,),
        in_specs=[_rows(tm, D_MODEL), _full(wd.shape), _rows(tm, D_FF, 0), _rows(tm, D_FF, 1)],
        out_specs=_rows(tm, 2 * D_FF),
        out_shape=jax.ShapeDtypeStruct((T, 2 * D_FF), BF16),
        compiler_params=_cp("parallel"),
    )(dout_b, wd, gu, gu)


def _ffn_in_bwd_call(dgu, w, h, g_ffn, dout, tm=512):
    T = h.shape[0]

    def body(dgu_ref, w_ref, h_ref, g_ref, d_ref, dh_ref, dhb_ref, dg_ref):
        dhn = _dot(dgu_ref[...], w_ref[...])
        dx, dgt = _rms_bwd(dhn, h_ref[...], g_ref[...])
        dh = d_ref[...] + dx
        dh_ref[...] = dh
        dhb_ref[...] = dh.astype(BF16)

        @pl.when(pl.program_id(0) == 0)
        def _():
            dg_ref[...] = jnp.zeros_like(dg_ref)

        dg_ref[...] += jnp.sum(dgt, axis=0, keepdims=True)

    row = _rows(tm, D_MODEL)
    return pl.pallas_call(
        body, name="ffn_in_bwd", grid=(T // tm,),
        in_specs=[_rows(tm, 2 * D_FF), _full(w.shape), row, _full((1, D_MODEL)), row],
        out_specs=[row, row, _full((1, D_MODEL))],
        out_shape=[jax.ShapeDtypeStruct((T, D_MODEL), F32), jax.ShapeDtypeStruct((T, D_MODEL), BF16),
                   jax.ShapeDtypeStruct((1, D_MODEL), F32)],
        compiler_params=_cp("arbitrary"),
    )(dgu, w, h, g_ffn, dout)


def _merge_bwd_call(dh_b, wm, wc, wa, u, yc, ya, tm=512):
    T = dh_b.shape[0]

    def body(dh_ref, wm_ref, wc_ref, wa_ref, gc_ref, ga_ref, yc_ref, ya_ref, dyc_ref, dya_ref, dug_ref, dcp_ref, do_ref):
        dz = _dot_nt(dh_ref[...], wm_ref[...])
        sgc = _sigmoid(gc_ref[...].astype(F32))
        sga = _sigmoid(ga_ref[...].astype(F32))
        dyc = (dz * sgc).astype(BF16)
        dya = (dz * sga).astype(BF16)
        dyc_ref[...] = dyc
        dya_ref[...] = dya
        dug_ref[:, :D_MODEL] = (dz * yc_ref[...].astype(F32) * (sgc * (1.0 - sgc))).astype(BF16)
        dug_ref[:, D_MODEL:] = (dz * ya_ref[...].astype(F32) * (sga * (1.0 - sga))).astype(BF16)
        dcp_ref[...] = _dot_nt(dyc, wc_ref[...])
        do_ref[...] = _dot_nt(dya, wa_ref[...]).astype(BF16)

    sq = _full((D_MODEL, D_MODEL))
    row = _rows(tm, D_MODEL)
    b16 = jax.ShapeDtypeStruct((T, D_MODEL), BF16)
    return pl.pallas_call(
        body, name="merge_bwd", grid=(T // tm,),
        in_specs=[row, sq, sq, sq, _rows(tm, D_MODEL, COL_GC // D_MODEL), _rows(tm, D_MODEL, COL_GA // D_MODEL), row, row],
        out_specs=[row, row, _rows(tm, 2 * D_MODEL), row, row],
        out_shape=[b16, b16, jax.ShapeDtypeStruct((T, 2 * D_MODEL), BF16), jax.ShapeDtypeStruct((T, D_MODEL), F32), b16],
        compiler_params=_cp("parallel"),
    )(dh_b, wm, wc, wa, u, u, yc, ya)


def _conv_bwd_call(dcp, hc, u, conv_w, ln_g, ln_b, B, S, ts=256):
    T = B * S
    nS = S // ts
    per32 = ts // HALO
    last32 = T // HALO - 1

    def ln_bwd(dcp_v, hc_v, g, bb):
        mu = jnp.mean(hc_v, axis=-1, keepdims=True)
        xc = hc_v - mu
        rstd = lax.rsqrt(jnp.mean(xc * xc, axis=-1, keepdims=True) + EPS)
        xhat = xc * rstd
        y = xhat * g + bb
        sy = _sigmoid(y)
        dy = dcp_v * (sy * (1.0 + y * (1.0 - sy)))
        dxh = dy * g
        dhc = rstd * (dxh - jnp.mean(dxh, axis=-1, keepdims=True) - xhat * jnp.mean(dxh * xhat, axis=-1, keepdims=True))
        return dhc, dy * xhat, dy

    def body(dcp_ref, dcpn_ref, hc_ref, hcn_ref, a_ref, gate_ref, w_ref, g_ref, bb_ref,
             du_ref, dw_ref, db_ref, dlg_ref, dlb_ref, dhext, glu_scr, dhsh, acc_ref):
        b = pl.program_id(0)
        s = pl.program_id(1)

        @pl.when((b == 0) & (s == 0))
        def _():
            dw_ref[...] = jnp.zeros_like(dw_ref)
            db_ref[...] = jnp.zeros_like(db_ref)
            dlg_ref[...] = jnp.zeros_like(dlg_ref)
            dlb_ref[...] = jnp.zeros_like(dlb_ref)

        dhc, dgt, dbt = ln_bwd(dcp_ref[...], hc_ref[...], g_ref[...], bb_ref[...])
        dhn, _, _ = ln_bwd(dcpn_ref[...], hcn_ref[...], g_ref[...], bb_ref[...])
        dhext[0:ts, :] = dhc
        dhext[ts:, :] = jnp.where(s < nS - 1, dhn, 0.0)
        db_ref[...] += jnp.sum(dhc, axis=0, keepdims=True)
        dlg_ref[...] += jnp.sum(dgt, axis=0, keepdims=True)
        dlb_ref[...] += jnp.sum(dbt, axis=0, keepdims=True)

        sg = _sigmoid(gate_ref[...].astype(F32))
        av = a_ref[...].astype(F32)
        glu_scr[...] = av * sg
        _fill_shifted(dhext, dhsh)

        def colblock(cb, carry):
            cols = pl.ds(pl.multiple_of(cb * 128, 128), 128)
            outs = _conv_taps(w_ref, dhext, dhsh, 0, ts, cb, reverse=True)
            for rb, acc in enumerate(outs):
                acc_ref[rb * 128:(rb + 1) * 128, cols] = acc
            taps = [jnp.zeros((8, 128), F32) for _ in range(CONV_WIDTH)]
            for r0 in range(0, ts, 128):
                glu = [glu_scr[r0 + 8 * i:r0 + 8 * (i + 1), cols] for i in range(16)]
                for s8 in range(8):
                    users = {}
                    for j in range(CONV_WIDTH):
                        if (CONV_WIDTH - 1 - j) % 8 == s8:
                            for i in range(16):
                                users.setdefault(CONV_WIDTH - 1 - j + r0 + 8 * i - s8, []).append((j, i))
                    for row, pairs in sorted(users.items()):
                        dhc8 = dhext[row:row + 8, cols] if s8 == 0 else dhsh[s8 - 1, row:row + 8, cols]
                        for j, i in pairs:
                            taps[j] = taps[j] + glu[i] * dhc8
            for j in range(CONV_WIDTH):
                dw_ref[8 * j:8 * (j + 1), cols] += taps[j]
            return carry

        lax.fori_loop(0, C_CONV // 128, colblock, 0)
        dglu = acc_ref[...]
        du_ref[:, :C_CONV] = (dglu * sg).astype(BF16)
        du_ref[:, C_CONV:] = (dglu * av * (sg * (1.0 - sg))).astype(BF16)

    cur = lambda cb: pl.BlockSpec((ts, C_CONV), lambda b, s: (b * nS + s, cb))
    nxt = pl.BlockSpec((HALO, C_CONV), lambda b, s: (jnp.minimum(b * (S // HALO) + (s + 1) * per32, last32), 0))
    vec = _full((1, C_CONV))
    return pl.pallas_call(
        body, name="conv_bwd", grid=(B, nS),
        in_specs=[cur(0), nxt, cur(0), nxt, cur(0), cur(1), _full((CONV_WIDTH, C_CONV)), vec, vec],
        out_specs=[pl.BlockSpec((ts, 2 * C_CONV), lambda b, s: (b * nS + s, 0)), _full((8 * CONV_WIDTH, C_CONV)), vec, vec, vec],
        out_shape=[jax.ShapeDtypeStruct((T, 2 * C_CONV), BF16), jax.ShapeDtypeStruct((8 * CONV_WIDTH, C_CONV), F32),
                   jax.ShapeDtypeStruct((1, C_CONV), F32), jax.ShapeDtypeStruct((1, C_CONV), F32),
                   jax.ShapeDtypeStruct((1, C_CONV), F32)],
        scratch_shapes=[pltpu.VMEM((ts + HALO, C_CONV), F32), pltpu.VMEM((ts, C_CONV), F32),
                        pltpu.VMEM((7, ts + HALO, C_CONV), F32), pltpu.VMEM((ts, C_CONV), F32)],
        compiler_params=_cp("arbitrary", "arbitrary"),
    )(dcp, dcp, hc, hc, u, u, conv_w, ln_g, ln_b)


def _attn_bwd_call(u, do, qg2, kg2, sinks, slopes, B, S):
    T = B * S
    nb = S // BLOCK
    half = Q_W // 2
    npair = GROUP // 2

    def body(sinks_ref, slopes_ref, q_ref, k_ref, v_ref, do_ref, qg_ref, kg_ref,
             dq_ref, dkv_ref, dqg_ref, dkg_ref, dsk_ref, kA, kB, vA, vB, dkn_acc, dv_acc, dkv_keep,
             btab, s_buf, dp_buf, p_buf, ds_buf, qn_buf, do_buf, rq_buf):
        b = pl.program_id(0)
        kvh = pl.program_id(1)
        lo = _lane_lo()
        lane = lax.broadcasted_iota(jnp.int32, (1, 128), 1)
        first = kvh == 0

        @pl.when((b == 0) & first)
        def _():
            dqg_ref[...] = jnp.zeros_like(dqg_ref)
            dkg_ref[...] = jnp.zeros_like(dkg_ref)
            dsk_ref[...] = jnp.zeros_like(dsk_ref)

        _stage_kv(kvh, k_ref, v_ref, kg_ref, lo, kA, kB, vA, vB)
        dkn_acc[...] = jnp.zeros_like(dkn_acc)
        dv_acc[...] = jnp.zeros_like(dv_acc)
        _fill_bias(btab, kvh, slopes_ref)
        qgs = qg_ref[...] * ATTN_SCALE

        def fold(acc):
            canon = jnp.where(lo, acc[0:2 * BLOCK] + pltpu.roll(acc[2 * BLOCK:], HEAD_DIM, 1), 0.0)
            return jnp.where(first, canon, pltpu.roll(canon, HEAD_DIM, 1))

        def block(n, slot):
            s_scr, dp_scr, p_scr, ds_scr = s_buf.at[slot], dp_buf.at[slot], p_buf.at[slot], ds_buf.at[slot]
            qn_scr, do_scr, rq_scr = qn_buf.at[slot], do_buf.at[slot], rq_buf.at[slot]
            r0 = pl.multiple_of(n * BLOCK, BLOCK)
            win = pl.ds(r0, 2 * BLOCK)
            rows = pl.ds(r0, BLOCK)
            k2t = jnp.concatenate([kA[win, :], kB[win, :]], axis=0)
            v2 = jnp.concatenate([vA[win, :], vB[win, :]], axis=0)
            var = jnp.minimum(n, 1)
            for p in range(npair):
                pc = slice(128 * p, 128 * (p + 1))
                qp = q_ref[rows, pc].astype(F32)
                rq = _head_rstd(qp, lo)
                rq_scr[pc, :] = rq
                qn_scr[pc, :] = (qp * rq * qgs).astype(BF16)
                do_scr[pc, :] = do_ref[rows, pc]
            s_scr[...] = _dot_nt(k2t, qn_scr[...])
            dp_scr[...] = _dot_nt(v2, do_scr[...])
            for p in range(npair):
                for hf in range(2):
                    g = 2 * p + hf
                    keys, qs = slice(2 * BLOCK * hf, 2 * BLOCK * (hf + 1)), slice(128 * p, 128 * (p + 1))
                    pn, esk, rz = _softmax_cols(s_scr, btab, p, var, g, hf, sinks_ref[kvh * GROUP + g])
                    dph = dp_scr[keys, qs]
                    delta = jnp.sum(pn * dph, axis=0, keepdims=True)
                    ds_scr[keys, qs] = (pn * (dph - delta)).astype(BF16)
                    p_scr[keys, qs] = pn.astype(BF16)
                    dsk_ref[...] += jnp.where(lane == kvh * GROUP + g, -jnp.sum(esk * rz * delta), 0.0)
            dqs_all = _dot_tn(k2t, ds_scr[...]).T
            for p in range(npair):
                pc = slice(128 * p, 128 * (p + 1))
                qp = q_ref[rows, pc].astype(F32)
                rq = rq_scr[pc, :]
                dqs = dqs_all[pc, :]
                t = dqs * qgs
                dq = rq * t - qp * (rq * rq * rq * (_half_sums(t * qp, lo) * (1.0 / HEAD_DIM)))
                dq_ref[rows, pc] = dq.astype(BF16)
                dqg_ref[...] += jnp.sum(dqs * qp * rq, axis=0, keepdims=True) * ATTN_SCALE
            dkn_acc[win, :] += fold(_dot(ds_scr[...], qn_scr[...]))
            dv_acc[win, :] += fold(_dot(p_scr[...], do_scr[...]))

        def four_blocks(i, carry):
            for slot in range(4):
                block(4 * i + slot, slot)
            return carry

        lax.fori_loop(0, nb // 4, four_blocks, 0)

        dkn = dkn_acc[BLOCK:, :]
        kv = k_ref[...].astype(F32)
        kg = kg_ref[...]
        rk = _head_rstd(kv, lo)
        t = dkn * kg
        dk = rk * t - kv * (rk * rk * rk * (_half_sums(t * kv, lo) * (1.0 / HEAD_DIM)))
        dkg_ref[...] += jnp.sum(dkn * kv * rk, axis=0, keepdims=True)
        dv = dv_acc[BLOCK:, :]

        @pl.when(first)
        def _():
            dkv_keep[:, :128] = dk
            dkv_keep[:, 128:] = dv

        @pl.when(kvh == N_KV_HEADS - 1)
        def _():
            dkv_ref[:, :128] = (dkv_keep[:, :128] + dk).astype(BF16)
            dkv_ref[:, 128:] = (dkv_keep[:, 128:] + dv).astype(BF16)

    smem = pl.BlockSpec(memory_space=pltpu.SMEM)
    kvbuf = pltpu.VMEM((S + BLOCK, 128), BF16)
    accbuf = pltpu.VMEM((S + BLOCK, 128), F32)
    vec = _full((1, 128))
    return pl.pallas_call(
        body, name="attn_bwd", grid=(B, N_KV_HEADS),
        in_specs=[smem, smem,
                  pl.BlockSpec((S, half), lambda b, h: (b, COL_Q // half + h)),
                  pl.BlockSpec((S, 128), lambda b, h: (b, COL_K // 128)),
                  pl.BlockSpec((S, 128), lambda b, h: (b, COL_V // 128)),
                  pl.BlockSpec((S, half), lambda b, h: (b, h)),
                  vec, vec],
        out_specs=[pl.BlockSpec((S, half), lambda b, h: (b, h)), pl.BlockSpec((S, 256), lambda b, h: (b, 0)), vec, vec, vec],
        out_shape=[jax.ShapeDtypeStruct((T, Q_W), BF16), jax.ShapeDtypeStruct((T, 2 * KV_W), BF16),
                   jax.ShapeDtypeStruct((1, 128), F32), jax.ShapeDtypeStruct((1, 128), F32), jax.ShapeDtypeStruct((1, 128), F32)],
        scratch_shapes=[kvbuf, kvbuf, kvbuf, kvbuf, accbuf, accbuf, pltpu.VMEM((S, 256), F32),
                        pltpu.VMEM((2, GROUP, 2 * BLOCK, BLOCK), F32),
                        pltpu.VMEM((4, npair * BLOCK, 4 * BLOCK), F32), pltpu.VMEM((4, npair * BLOCK, 4 * BLOCK), F32),
                        pltpu.VMEM((4, npair * BLOCK, 4 * BLOCK), BF16), pltpu.VMEM((4, npair * BLOCK, 4 * BLOCK), BF16),
                        pltpu.VMEM((4, npair * BLOCK, 128), BF16), pltpu.VMEM((4, npair * BLOCK, 128), BF16),
                        pltpu.VMEM((4, npair * BLOCK, 128), F32)],
        compiler_params=_cp("arbitrary", "arbitrary"),
    )(sinks, slopes, u, u, u, do, qg2, kg2)


def _inproj_bwd_call(du_conv, du_g, du_q, du_kv, w, x, g, dh, tm=512):
    T = x.shape[0]

    def body(dc_ref, dg_ref, dq_ref, dkv_ref, w_ref, x_ref, g_ref, dh_ref, gx_ref, dgm_ref):
        dxn = _dot(dc_ref[...], w_ref[COL_A:COL_GC, :])
        dxn += _dot(dg_ref[...], w_ref[COL_GC:COL_Q, :])
        dxn += _dot(dq_ref[...], w_ref[COL_Q:COL_K, :])
        dxn += _dot(dkv_ref[...], w_ref[COL_K:, :])
        dx, dgt = _rms_bwd(dxn, x_ref[...], g_ref[...])
        gx_ref[...] = dh_ref[...] + dx

        @pl.when(pl.program_id(0) == 0)
        def _():
            dgm_ref[...] = jnp.zeros_like(dgm_ref)

        dgm_ref[...] += jnp.sum(dgt, axis=0, keepdims=True)

    row = _rows(tm, D_MODEL)
    return pl.pallas_call(
        body, name="inproj_bwd", grid=(T // tm,),
        in_specs=[_rows(tm, 2 * C_CONV), _rows(tm, 2 * D_MODEL), _rows(tm, Q_W), _rows(tm, 2 * KV_W), _full(w.shape),
                  row, _full((1, D_MODEL)), row],
        out_specs=[row, _full((1, D_MODEL))],
        out_shape=[jax.ShapeDtypeStruct((T, D_MODEL), F32), jax.ShapeDtypeStruct((1, D_MODEL), F32)],
        compiler_params=_cp("arbitrary"),
    )(du_conv, du_g, du_q, du_kv, w, x, g, dh)


def _adamw_math(w, g, m, v):
    m = ADAM_B1 * m + (1.0 - ADAM_B1) * g
    v = ADAM_B2 * v + (1.0 - ADAM_B2) * (g * g)
    m_hat = m / (1.0 - ADAM_B1 ** ADAM_STEP)
    v_hat = v / (1.0 - ADAM_B2 ** ADAM_STEP)
    delta = -ADAM_LR * (m_hat / (jnp.sqrt(v_hat) + ADAM_EPS) + ADAM_WD * w)
    return delta, m, v


def _adamw_sum_call(name, parts, own, w, m, v, tr):
    R, C = w.shape

    def body(p_ref, own_ref, w_ref, m_ref, v_ref, g_ref, d_ref, nm_ref, nv_ref):
        x, y, c = _position()
        me = 4 * x + 2 * y + c
        g = jnp.zeros((tr, C), F32)
        for s in range(N_DEV):
            g = g + jnp.where(me == s, own_ref[...], p_ref[s]).astype(F32)
        g_ref[...] = g
        d_ref[...], nm_ref[...], nv_ref[...] = _adamw_math(w_ref[...], g, m_ref[...], v_ref[...])

    blk = pl.BlockSpec((tr, C), lambda i: (i, 0))
    out = jax.ShapeDtypeStruct((R, C), F32)
    return pl.pallas_call(
        body, name=name, grid=(R // tr,),
        in_specs=[pl.BlockSpec((N_DEV, tr, C), lambda i: (0, i, 0)), blk, blk, blk, blk],
        out_specs=[blk] * 4, out_shape=[out] * 4,
        compiler_params=_cp("parallel"),
    )(parts, own, w, m, v)


def _adamw_small_call(name, g, w, m, v):
    def body(g_ref, w_ref, m_ref, v_ref, d_ref, nm_ref, nv_ref):
        d_ref[...], nm_ref[...], nv_ref[...] = _adamw_math(w_ref[...], g_ref[...], m_ref[...], v_ref[...])

    out = jax.ShapeDtypeStruct(w.shape, F32)
    return pl.pallas_call(body, name=name, out_shape=[out] * 3)(g, w, m, v)


def _position():
    return lax.axis_index("x"), lax.axis_index("y"), lax.axis_index("c")


def _peer(x, y, c, rel):
    fx, fy, fc = (rel >> 2) & 1, (rel >> 1) & 1, rel & 1
    px = 1 - x if fx else x
    py = 1 - y if fy else y
    pc = 1 - c if fc else c
    return (px, py, pc), 4 * px + 2 * py + pc


def _gather_weights_call(shards):
    n = len(shards)

    def body(*refs):
        ins, outs = refs[:n], refs[n:2 * n]
        send_sems, recv_sems, local_sems = refs[2 * n:]
        x, y, c = _position()
        me = 4 * x + 2 * y + c
        sib, sib_id = _peer(x, y, c, 1)
        chips = [_peer(x, y, c, rel) for rel in (4, 2, 6)]

        def copy(i, k, src, slot, to):
            return pltpu.make_async_remote_copy(
                src_ref=src, dst_ref=outs[i].at[slot], send_sem=send_sems.at[i, k], recv_sem=recv_sems.at[i, k],
                device_id=to, device_id_type=MESH)

        mine = [pltpu.make_async_copy(ins[i], outs[i].at[me], local_sems.at[i]) for i in range(n)]
        for cp in mine:
            cp.start()
        first = []
        for i in range(n):
            first.append(copy(i, 0, ins[i], me, sib))
            first += [copy(i, 1 + j, ins[i], me, dev) for j, (dev, _) in enumerate(chips)]
        for cp in first:
            cp.start()
        passed = []
        for i in range(n):
            for j, (dev, slot) in enumerate(chips):
                copy(i, 1 + j, ins[i], slot, dev).wait_recv()
                fwd = copy(i, 4 + j, outs[i].at[slot], slot, sib)
                fwd.start()
                passed.append(fwd)
        for i in range(n):
            copy(i, 0, ins[i], sib_id, sib).wait_recv()
            for j, (_, slot) in enumerate(chips):
                copy(i, 4 + j, ins[i], slot ^ 1, sib).wait_recv()
        for cp in first + passed:
            cp.wait_send()
        for cp in mine:
            cp.wait()

    hbm = pl.BlockSpec(memory_space=pl.ANY)
    return pl.pallas_call(
        body, name="gather_weights",
        in_specs=[hbm] * n, out_specs=[hbm] * n,
        out_shape=[jax.ShapeDtypeStruct((N_DEV,) + s.shape, s.dtype) for s in shards],
        scratch_shapes=[pltpu.SemaphoreType.DMA((n, 7)), pltpu.SemaphoreType.DMA((n, 7)), pltpu.SemaphoreType.DMA((n,))],
    )(*shards)


_HBM = pl.BlockSpec(memory_space=pltpu.HBM)
_SEM = pl.BlockSpec(memory_space=pltpu.SEMAPHORE)
_EFFECT = pltpu.SideEffectType.DATAFLOW_SIDE_EFFECTING


def _exchange_copies(scatter, incoming, src_refs, land_refs, send_sems, recv_sems):
    x, y, c = _position()
    me = 4 * x + 2 * y + c
    out = []
    for i, (src, land) in enumerate(zip(src_refs, land_refs)):
        for k in range(N_DEV - 1):
            dev, dev_id = _peer(x, y, c, k + 1)
            j = i * (N_DEV - 1) + k
            out.append(pltpu.make_async_remote_copy(
                src_ref=src.at[dev_id] if scatter else src, dst_ref=land.at[dev_id if incoming else me],
                send_sem=send_sems.at[j], recv_sem=recv_sems.at[j], device_id=dev, device_id_type=MESH))
    return out


def _exchange_start_call(name, arrays, scatter):
    n = len(arrays)
    lands = [lax.empty((N_DEV,) + (a.shape[1:] if scatter else a.shape), a.dtype) for a in arrays]

    def body(*refs):
        src_refs, land_refs = refs[:n], refs[n:2 * n]
        send_sems, recv_sems, token = refs[2 * n], refs[2 * n + 1], refs[-1]
        for copy in _exchange_copies(scatter, False, src_refs, land_refs, send_sems, recv_sems):
            copy.start()
        token[...] = jnp.zeros_like(token)

    sems = pltpu.SemaphoreType.DMA((n * (N_DEV - 1),))
    res = pl.pallas_call(
        body, name=name,
        out_shape=[sems, sems] + [pltpu.HBM(a.shape, a.dtype) for a in arrays] + [pltpu.HBM(l.shape, l.dtype) for l in lands]
        + [jax.ShapeDtypeStruct((8, 128), F32)],
        in_specs=[_HBM] * (2 * n), out_specs=[_SEM, _SEM] + [_HBM] * (2 * n) + [pl.BlockSpec(memory_space=pltpu.VMEM)],
        input_output_aliases={i: 2 + i for i in range(2 * n)},
        compiler_params=pltpu.CompilerParams(has_side_effects=_EFFECT),
    )(*[pltpu.with_memory_space_constraint(a, pltpu.HBM) for a in list(arrays) + lands])
    return res[0], res[1], res[2:2 + n], res[2 + n:2 + 2 * n], res[-1]


def _exchange_wait_call(name, started, scatter, after):
    send_sems, recv_sems, srcs, lands, _ = started
    n = len(srcs)

    def body(*refs):
        src_refs, land_refs = refs[:n], refs[n:2 * n]
        send_s, recv_s = refs[2 * n], refs[2 * n + 1]
        for copy in _exchange_copies(scatter, False, src_refs, land_refs, send_s, recv_s):
            copy.wait_send()
        for copy in _exchange_copies(scatter, True, src_refs, land_refs, send_s, recv_s):
            copy.wait_recv()

    res = pl.pallas_call(
        body, name=name,
        out_shape=[pltpu.HBM(a.shape, a.dtype) for a in list(srcs) + list(lands)],
        in_specs=[_HBM] * (2 * n) + [_SEM, _SEM, pl.BlockSpec(memory_space=pl.ANY)], out_specs=[_HBM] * (2 * n),
        input_output_aliases={i: i for i in range(2 * n)},
        compiler_params=pltpu.CompilerParams(has_side_effects=_EFFECT),
    )(*srcs, *lands, send_sems, recv_sems, after)
    return res[n:]


def _allsum_small_call(part, blocks):
    P, Q = part.shape[0], blocks.shape[1]

    def body(p_ref, b_ref, o_ref, ob_ref, all_ref, allb_ref, send_sems, recv_sems):
        x, y, c = _position()
        me = 4 * x + 2 * y + c
        peers = [_peer(x, y, c, rel) for rel in range(1, N_DEV)]
        all_ref[me] = p_ref[...]
        allb_ref[me] = b_ref[me]

        def copies(k, slot, dev, dev_id):
            sems = lambda i: dict(send_sem=send_sems.at[i], recv_sem=recv_sems.at[i], device_id=dev, device_id_type=MESH)
            return (pltpu.make_async_remote_copy(src_ref=p_ref, dst_ref=all_ref.at[slot], **sems(k)),
                    pltpu.make_async_remote_copy(src_ref=b_ref.at[dev_id], dst_ref=allb_ref.at[slot], **sems(7 + k)))

        sends = [cp for k, (dev, dev_id) in enumerate(peers) for cp in copies(k, me, dev, dev_id)]
        for cp in sends:
            cp.start()
        for k, (dev, dev_id) in enumerate(peers):
            for cp in copies(k, dev_id, dev, dev_id):
                cp.wait_recv()
        for cp in sends:
            cp.wait_send()
        tot, totb = all_ref[0], allb_ref[0]
        for s in range(1, N_DEV):
            tot, totb = tot + all_ref[s], totb + allb_ref[s]
        o_ref[...] = tot
        ob_ref[...] = totb

    vm = pl.BlockSpec(memory_space=pltpu.VMEM)
    return pl.pallas_call(
        body, name="allsum_small", in_specs=[vm, vm], out_specs=[vm, vm],
        out_shape=[jax.ShapeDtypeStruct((P, 128), F32), jax.ShapeDtypeStruct((Q, 128), F32)],
        scratch_shapes=[pltpu.VMEM((N_DEV, P, 128), F32), pltpu.VMEM((N_DEV, Q, 128), F32),
                        pltpu.SemaphoreType.DMA((14,)), pltpu.SemaphoreType.DMA((14,))],
    )(part, blocks)


def _pad_rows(v, rows):
    flat = v.reshape(-1)
    return jnp.pad(flat, (0, rows * 128 - flat.shape[0])).reshape(rows, 128)


def _local_step(x, tgt, w_in_i, conv_w, conv_b, ln_g, ln_b, qg, kg, sinks, g_mix, g_ffn, token0, rest_weights, grads_ready, B, S):
    row = lambda v: v.reshape(1, -1)
    qg2 = jnp.tile(qg, 2).reshape(1, 128)
    kg2 = jnp.tile(kg, 2).reshape(1, 128)
    heads = jnp.arange(1, N_Q_HEADS + 1, dtype=F32)
    slopes = jnp.exp2(-8.0 * heads / N_Q_HEADS)

    u, xn = _inproj_call(x, row(g_mix) + token0, w_in_i)
    hc, c = _conv_fwd_call(u, conv_w, row(conv_b), row(ln_g), row(ln_b), B, S)
    o = _attn_fwd_call(u, qg2, kg2, sinks, slopes, B, S)
    wc, wa, wm, w_ffn_in, wd = rest_weights(o)
    yc, ya, z, h, hn = _merge_call(c, o, u, x, wc, wa, wm, row(g_ffn))
    gu, act = _ffn_in_call(hn, w_ffn_in)
    sse, dout, dout_b = _ffn_out_call(act, wd, h, tgt)

    dgu = _ffn_down_bwd_call(dout_b, wd, gu)
    g_wd = _dw_call("dw_ffn_down", act, dout_b, tr=D_FF // 2)
    g_wffn_t = _dw_call("dw_ffn_in", dgu, hn, tr=2 * D_FF // 4)
    tok = grads_ready(dict(w_ffn_down=g_wd, w_ffn_in=g_wffn_t))
    dh, dh_b, g_gffn = _ffn_in_bwd_call(dgu, w_ffn_in, h, row(g_ffn) + tok, dout)
    dyc, dya, du_g, dcp, do = _merge_bwd_call(dh_b, wm, wc, wa, u, yc, ya)
    g_wm = _dw_call("dw_merge", z, dh_b, tr=D_MODEL)
    g_wc = _dw_call("dw_conv_out", c, dyc, tr=D_MODEL)
    g_wa = _dw_call("dw_attn_out", o, dya, tr=D_MODEL)
    tok = grads_ready(dict(w_merge_out=g_wm, w_conv_out=g_wc, w_attn_out=g_wa))
    du_conv, g_cw8, g_cb, g_lg, g_lb = _conv_bwd_call(dcp, hc, u, conv_w, row(ln_g) + tok, row(ln_b), B, S)
    du_q, du_kv, g_qg2, g_kg2, g_sk = _attn_bwd_call(u, do, qg2, kg2, sinks, slopes, B, S)
    g_win_t = _concat_rows_call("dw_in_rows", [
        _dw_call("dw_in_conv", du_conv, xn, tr=D_MODEL),
        _dw_call("dw_in_q", du_q, xn, tr=D_MODEL),
        _dw_call("dw_in_kv", du_kv, xn, tr=2 * KV_W),
        _dw_call("dw_in_gates", du_g, xn, tr=D_MODEL)])
    tok = grads_ready(dict(w_in=g_win_t))
    grad_x, g_gmix = _inproj_bwd_call(du_conv, du_g, du_q, du_kv, w_in_i, x, row(g_mix) + tok, dh)

    g_cw = g_cw8.reshape(CONV_WIDTH, 8, C_CONV).sum(axis=1)
    g_qg = g_qg2[0, :HEAD_DIM] + g_qg2[0, HEAD_DIM:]
    g_kg = g_kg2[0, :HEAD_DIM] + g_kg2[0, HEAD_DIM:]
    small = dict(norm_mix_g=g_gmix[0], conv_dw_b=g_cb[0], conv_ln_g=g_lg[0], conv_ln_b=g_lb[0], q_norm_g=g_qg,
                 k_norm_g=g_kg, sinks=g_sk[0, :N_Q_HEADS], norm_ffn_g=g_gffn[0], conv_dw_w=g_cw)
    return sse[0, 0], grad_x, small


def _to_internal_rows(wt):
    o1 = 2 * C_CONV
    o4 = o1 + Q_W + 2 * KV_W
    return jnp.concatenate([wt[:o1], wt[o4:], wt[o1:o4]], axis=0)


_COL_SHARDED = ("w_in", "w_ffn_in")

_SMALL = (("norm_mix_g", 8), ("conv_dw_b", 8), ("conv_ln_g", 8), ("conv_ln_b", 8), ("norm_ffn_g", 8),
          ("q_norm_g", 1), ("k_norm_g", 1), ("sinks", 1))
_SMALL_ROWS = 48


def _pack_small(d):
    rows = [_pad_rows(d[name], r) for name, r in _SMALL]
    used = sum(r for _, r in _SMALL)
    return jnp.concatenate(rows + [jnp.zeros((_SMALL_ROWS - used, 128), F32)], axis=0)


def _unpack_small(packed, like):
    out, r0 = {}, 0
    for name, r in _SMALL:
        n = like[name].size
        out[name] = packed[r0:r0 + r].reshape(-1)[:n].reshape(like[name].shape)
        r0 += r
    return out


def kernel(x, norm_mix_g, w_in, conv_dw_w, conv_dw_b, conv_ln_g, conv_ln_b, w_conv_out, q_norm_g, k_norm_g, sinks, w_attn_out, w_merge_out, norm_ffn_g, w_ffn_in, w_ffn_down, loss_target, m_norm_mix_g, m_w_in, m_conv_dw_w, m_conv_dw_b, m_conv_ln_g, m_conv_ln_b, m_w_conv_out, m_q_norm_g, m_k_norm_g, m_sinks, m_w_attn_out, m_w_merge_out, m_norm_ffn_g, m_w_ffn_in, m_w_ffn_down, v_norm_mix_g, v_w_in, v_conv_dw_w, v_conv_dw_b, v_conv_ln_g, v_conv_ln_b, v_w_conv_out, v_q_norm_g, v_k_norm_g, v_sinks, v_w_attn_out, v_w_merge_out, v_norm_ffn_g, v_w_ffn_in, v_w_ffn_down):
    names = ["norm_mix_g", "w_in", "conv_dw_w", "conv_dw_b", "conv_ln_g", "conv_ln_b", "w_conv_out", "q_norm_g",
             "k_norm_g", "sinks", "w_attn_out", "w_merge_out", "norm_ffn_g", "w_ffn_in", "w_ffn_down"]
    w = dict(zip(names, (norm_mix_g, w_in, conv_dw_w, conv_dw_b, conv_ln_g, conv_ln_b, w_conv_out, q_norm_g, k_norm_g,
                         sinks, w_attn_out, w_merge_out, norm_ffn_g, w_ffn_in, w_ffn_down)))
    m = dict(zip(names, (m_norm_mix_g, m_w_in, m_conv_dw_w, m_conv_dw_b, m_conv_ln_g, m_conv_ln_b, m_w_conv_out,
                         m_q_norm_g, m_k_norm_g, m_sinks, m_w_attn_out, m_w_merge_out, m_norm_ffn_g, m_w_ffn_in,
                         m_w_ffn_down)))
    v = dict(zip(names, (v_norm_mix_g, v_w_in, v_conv_dw_w, v_conv_dw_b, v_conv_ln_g, v_conv_ln_b, v_w_conv_out,
                         v_q_norm_g, v_k_norm_g, v_sinks, v_w_attn_out, v_w_merge_out, v_norm_ffn_g, v_w_ffn_in,
                         v_w_ffn_down)))
    B, S, _ = x.shape
    T = B * S
    me = 4 * lax.axis_index("x") + 2 * lax.axis_index("y") + lax.axis_index("c")

    as_rows = lambda k, t: t.T if k in _COL_SHARDED else t
    cw_shard = _pad_rows(conv_dw_w.reshape(CONV_WIDTH, 128), 32)
    g_win, g_cwt = _gather_weights_call([w_in.astype(BF16), cw_shard])
    conv_w_full = g_cwt.reshape(N_DEV, 32 * 128)[:, :CONV_WIDTH * 128].reshape(N_DEV, CONV_WIDTH, 128)
    conv_w_full = conv_w_full.transpose(1, 0, 2).reshape(CONV_WIDTH, C_CONV)
    w_in_i = _to_internal_rows(g_win.transpose(0, 2, 1).reshape(IN_COLS, D_MODEL))
    rest_names = ["w_conv_out", "w_attn_out", "w_merge_out", "w_ffn_in", "w_ffn_down"]
    rest_shards = [as_rows(k, w[k]).astype(BF16) for k in rest_names]
    rest_started = _exchange_start_call("gather_rest_start", rest_shards, scatter=False)

    def rest_weights(after):
        lands = _exchange_wait_call("gather_rest_wait", rest_started, False, after)
        mine = lax.broadcasted_iota(jnp.int32, (N_DEV, 1, 1), 0) == me
        full = [jnp.where(mine, s[None], l) for l, s in zip(lands, rest_shards)]
        return tuple(f.reshape(N_DEV * f.shape[1], D_MODEL) for f in full)

    exchanges = []

    def grads_ready(group):
        ks = list(group)
        blocks = [group[k].reshape(N_DEV, group[k].shape[0] // N_DEV, D_MODEL) for k in ks]
        started = _exchange_start_call("scatter_" + ks[0] + "_start", blocks, scatter=True)
        exchanges.append((ks, started, [lax.dynamic_index_in_dim(b, me, 0, keepdims=False) for b in blocks]))
        return started[-1][0:1, 0:1]

    sse, grad_x, small = _local_step(
        x.reshape(T, D_MODEL), loss_target.reshape(T, D_MODEL), w_in_i, conv_w_full, conv_dw_b, conv_ln_g, conv_ln_b,
        q_norm_g, k_norm_g, sinks, norm_mix_g, norm_ffn_g, rest_started[-1][0:1, 0:1], rest_weights, grads_ready, B, S)

    grad, delta, new_m, new_v = {}, {}, {}, {}
    packed = jnp.concatenate([_pack_small(small), jnp.full((8, 128), sse, F32)], axis=0)
    cw_blocks = small["conv_dw_w"].reshape(CONV_WIDTH, N_DEV, 128).transpose(1, 0, 2)
    tot, cw_tot = _allsum_small_call(packed, jnp.pad(cw_blocks, ((0, 0), (0, 32 - CONV_WIDTH), (0, 0))))
    loss = 0.5 / D_MODEL * tot[_SMALL_ROWS, 0]
    g_small = _unpack_small(tot[:_SMALL_ROWS], w)
    d_s, m_s, v_s = _adamw_small_call("adamw_small", tot[:_SMALL_ROWS], _pack_small(w), _pack_small(m), _pack_small(v))
    d_small, m_small, v_small = _unpack_small(d_s, w), _unpack_small(m_s, w), _unpack_small(v_s, w)
    g_cw = cw_tot[:CONV_WIDTH]
    cw2 = lambda t: t.reshape(CONV_WIDTH, 128)
    d_cw, m_cw, v_cw = _adamw_small_call("adamw_conv_w", g_cw, cw2(conv_dw_w), cw2(m_conv_dw_w), cw2(v_conv_dw_w))
    cw3 = lambda t: t.reshape(CONV_WIDTH, 1, 128)
    for k, _ in _SMALL:
        grad[k], delta[k], new_m[k], new_v[k] = g_small[k], d_small[k], m_small[k], v_small[k]
    grad["conv_dw_w"], delta["conv_dw_w"], new_m["conv_dw_w"], new_v["conv_dw_w"] = cw3(g_cw), cw3(d_cw), cw3(m_cw), cw3(v_cw)

    after = d_cw
    for ks, started, owns in exchanges:
        lands = _exchange_wait_call("scatter_" + ks[0] + "_wait", started, True, after)
        for k, land, own in zip(ks, lands, owns):
            tr = {672: 336, 704: 352, 352: 176, 128: 128}[own.shape[0]]
            outs = _adamw_sum_call("adamw_" + k, land, own, as_rows(k, w[k]), as_rows(k, m[k]), as_rows(k, v[k]), tr)
            grad[k], delta[k], new_m[k], new_v[k] = (as_rows(k, t) for t in outs)
            after = outs[3]

    return (loss, grad_x.reshape(B, S, D_MODEL), *[grad[k] for k in names], *[delta[k] for k in names],
            *[new_m[k] for k in names], *[new_v[k] for k in names])
```

```python
import functools
import math

import jax
import jax.numpy as jnp
from jax import lax
from jax.experimental import pallas as pl
from jax.experimental.pallas import tpu as pltpu

F32 = jnp.float32
BF16 = jnp.bfloat16
MESH = pl.DeviceIdType.MESH

N_DEV = 8
D_MODEL = 1024
C_CONV = 1024
CONV_WIDTH = 31
HEAD_DIM = 64
N_Q_HEADS = 16
N_KV_HEADS = 2
GROUP = N_Q_HEADS // N_KV_HEADS
WINDOW = 128
BLOCK = 128
D_FF = 2816
EPS = 1e-6
NEG = -1e30
ATTN_SCALE = 1.0 / math.sqrt(HEAD_DIM)
Q_W = N_Q_HEADS * HEAD_DIM
KV_W = N_KV_HEADS * HEAD_DIM
IN_COLS = 2 * C_CONV + Q_W + 2 * KV_W + 2 * D_MODEL

ADAM_LR = 0.001
ADAM_B1 = 0.9
ADAM_B2 = 0.999
ADAM_EPS = 1e-08
ADAM_WD = 0.01
ADAM_STEP = 10

COL_A, COL_GATE, COL_GC, COL_GA, COL_Q, COL_K, COL_V = 0, 1024, 2048, 3072, 4096, 5120, 5248

HALO = 32
VMEM_LIMIT = 56 * 1024 * 1024


def _cp(*sem):
    return pltpu.CompilerParams(dimension_semantics=sem, vmem_limit_bytes=VMEM_LIMIT)


def _sigmoid(x):
    return jax.nn.sigmoid(x)


def _dot(a, b):
    return jnp.dot(a, b, preferred_element_type=F32)


def _dot_nt(a, b):
    return lax.dot_general(a, b, (((1,), (1,)), ((), ())), preferred_element_type=F32)


def _dot_tn(a, b):
    return lax.dot_general(a, b, (((0,), (0,)), ((), ())), preferred_element_type=F32)


def _full(shape):
    return pl.BlockSpec(shape, lambda *_: (0,) * len(shape))


def _rows(tm, ncols, colblk=0):
    return pl.BlockSpec((tm, ncols), lambda i: (i, colblk))


def _inproj_call(x, g, w, tm=512):
    T = x.shape[0]
    N = w.shape[0]

    def body(x_ref, g_ref, w_ref, u_ref, xn_ref):
        xv = x_ref[...]
        r = lax.rsqrt(jnp.mean(xv * xv, axis=-1, keepdims=True) + EPS)
        xn = (xv * r * g_ref[...]).astype(BF16)
        xn_ref[...] = xn
        u_ref[...] = _dot_nt(xn, w_ref[...]).astype(BF16)

    return pl.pallas_call(
        body, name="inproj", grid=(T // tm,),
        in_specs=[_rows(tm, D_MODEL), _full((1, D_MODEL)), _full(w.shape)],
        out_specs=[_rows(tm, N), _rows(tm, D_MODEL)],
        out_shape=[jax.ShapeDtypeStruct((T, N), BF16), jax.ShapeDtypeStruct((T, D_MODEL), BF16)],
        compiler_params=_cp("parallel"),
    )(x, g, w)


def _fill_shifted(src_ref, sh_ref):
    n = src_ref.shape[0] - 8
    for s in range(1, 8):
        sh_ref[s - 1, 0:n, :] = src_ref[s:s + n, :]


def _window(src_ref, sh_ref, off, cols):
    s = off % 8
    if s == 0:
        return src_ref[off:off + 128, cols]
    return sh_ref[s - 1, off - s:off - s + 128, cols]


def _conv_taps(w_ref, src_ref, sh_ref, base, rows, cb, reverse):
    cols = pl.ds(pl.multiple_of(cb * 128, 128), 128)
    outs = []
    for rb in range(rows // 128):
        acc = jnp.zeros((128, 128), F32)
        for j in range(CONV_WIDTH):
            off = base + (CONV_WIDTH - 1 - j if reverse else j) + rb * 128
            acc = acc + w_ref[j:j + 1, cols] * _window(src_ref, sh_ref, off, cols)
        outs.append(acc)
    return outs


def _conv_fwd_call(u, conv_w, conv_b, ln_g, ln_b, B, S, ts=512):
    T = B * S
    nS = S // ts
    per32 = ts // HALO

    def body(a_ref, gate_ref, ap_ref, gp_ref, w_ref, b_ref, g_ref, bb_ref, hc_ref, c_ref, hext, hsh, acc_ref):
        s = pl.program_id(1)
        prev = ap_ref[...].astype(F32) * _sigmoid(gp_ref[...].astype(F32))
        hext[0:HALO, :] = jnp.where(s > 0, prev, 0.0)
        hext[HALO:, :] = a_ref[...].astype(F32) * _sigmoid(gate_ref[...].astype(F32))
        _fill_shifted(hext, hsh)

        def colblock(cb, carry):
            cols = pl.ds(pl.multiple_of(cb * 128, 128), 128)
            outs = _conv_taps(w_ref, hext, hsh, HALO - (CONV_WIDTH - 1), ts, cb, reverse=False)
            for rb, acc in enumerate(outs):
                acc_ref[rb * 128:(rb + 1) * 128, cols] = acc
            return carry

        lax.fori_loop(0, C_CONV // 128, colblock, 0)
        hc = acc_ref[...] + b_ref[...]
        hc_ref[...] = hc
        mu = jnp.mean(hc, axis=-1, keepdims=True)
        xc = hc - mu
        var = jnp.mean(xc * xc, axis=-1, keepdims=True)
        y = xc * lax.rsqrt(var + EPS) * g_ref[...] + bb_ref[...]
        c_ref[...] = (y * _sigmoid(y)).astype(BF16)

    cur = lambda cb: pl.BlockSpec((ts, C_CONV), lambda b, s: (b * nS + s, cb))
    prv = lambda cb: pl.BlockSpec((HALO, C_CONV), lambda b, s: (jnp.maximum(b * (S // HALO) + s * per32 - 1, 0), cb))
    vec = _full((1, C_CONV))
    return pl.pallas_call(
        body, name="conv_fwd", grid=(B, nS),
        in_specs=[cur(0), cur(1), prv(0), prv(1), _full((CONV_WIDTH, C_CONV)), vec, vec, vec],
        out_specs=[pl.BlockSpec((ts, C_CONV), lambda b, s: (b * nS + s, 0))] * 2,
        out_shape=[jax.ShapeDtypeStruct((T, C_CONV), F32), jax.ShapeDtypeStruct((T, C_CONV), BF16)],
        scratch_shapes=[pltpu.VMEM((HALO + ts, C_CONV), F32), pltpu.VMEM((7, HALO + ts, C_CONV), F32),
                        pltpu.VMEM((ts, C_CONV), F32)],
        compiler_params=_cp("parallel", "arbitrary"),
    )(u, u, u, u, conv_w, conv_b, ln_g, ln_b)


def _lane_lo():
    return lax.broadcasted_iota(jnp.int32, (1, 128), 1) < HEAD_DIM


def _half_sums(t, lo):
    s_lo = jnp.sum(jnp.where(lo, t, 0.0), axis=-1, keepdims=True)
    s_hi = jnp.sum(jnp.where(lo, 0.0, t), axis=-1, keepdims=True)
    return jnp.where(lo, s_lo, s_hi)


def _head_rstd(t, lo):
    return lax.rsqrt(_half_sums(t * t, lo) * (1.0 / HEAD_DIM) + EPS)


def _fill_bias(btab, kvh, slopes_ref):
    sj = lax.broadcasted_iota(jnp.int32, (2 * BLOCK, BLOCK), 0)
    qi = lax.broadcasted_iota(jnp.int32, (2 * BLOCK, BLOCK), 1)
    dist = qi + BLOCK - sj
    valid = (dist >= 0) & (dist < WINDOW)
    distf = dist.astype(F32)
    for g in range(GROUP):
        bias = jnp.where(valid, -slopes_ref[kvh * GROUP + g] * distf, NEG)
        btab[1, g] = bias
        btab[0, g] = jnp.where(sj >= BLOCK, bias, NEG)


def _softmax_cols(st_ref, btab, p, var, g, hf, sk):
    sh = st_ref[2 * BLOCK * hf:2 * BLOCK * (hf + 1), 128 * p:128 * (p + 1)] + btab[var, g]
    m = jnp.maximum(jnp.max(sh, axis=0, keepdims=True), sk)
    e = jnp.exp(sh - m)
    esk = jnp.exp(sk - m)
    rz = 1.0 / (jnp.sum(e, axis=0, keepdims=True) + esk)
    return e * rz, esk, rz


def _stage_kv(kvh, k_ref, v_ref, kg_ref, lo, kA, kB, vA, vB):
    kv = k_ref[...].astype(F32)
    rk = _head_rstd(kv, lo)
    kn = kv * rk * kg_ref[...]
    mine = lax.broadcasted_iota(jnp.int32, (1, 128), 1) // HEAD_DIM == kvh
    ksel = jnp.where(mine, kn, 0.0)
    vsel = jnp.where(mine, v_ref[...].astype(F32), 0.0)
    krol = pltpu.roll(ksel, HEAD_DIM, 1)
    vrol = pltpu.roll(vsel, HEAD_DIM, 1)
    first = kvh == 0
    zeros = jnp.zeros((BLOCK, 128), BF16)
    for ref, val in ((kA, jnp.where(first, ksel, krol)), (kB, jnp.where(first, krol, ksel)),
                     (vA, jnp.where(first, vsel, vrol)), (vB, jnp.where(first, vrol, vsel))):
        ref[0:BLOCK, :] = zeros
        ref[BLOCK:, :] = val.astype(BF16)


def _attn_fwd_call(u, qg2, kg2, sinks, slopes, B, S):
    T = B * S
    nb = S // BLOCK
    half = Q_W // 2
    npair = GROUP // 2
    per_trip = 8 if nb % 8 == 0 else 4

    def body(sinks_ref, slopes_ref, q_ref, k_ref, v_ref, qg_ref, kg_ref, o_ref, kA, kB, vA, vB, btab, qn_buf, s_buf, p_buf):
        kvh = pl.program_id(1)
        lo = _lane_lo()
        _stage_kv(kvh, k_ref, v_ref, kg_ref, lo, kA, kB, vA, vB)
        _fill_bias(btab, kvh, slopes_ref)
        qgs = qg_ref[...] * ATTN_SCALE

        def block(n, slot):
            qn_scr, s_scr, p_scr = qn_buf.at[slot], s_buf.at[slot], p_buf.at[slot]
            r0 = pl.multiple_of(n * BLOCK, BLOCK)
            win = pl.ds(r0, 2 * BLOCK)
            k2t = jnp.concatenate([kA[win, :], kB[win, :]], axis=0)
            v2 = jnp.concatenate([vA[win, :], vB[win, :]], axis=0)
            var = jnp.minimum(n, 1)
            for p in range(npair):
                pc = slice(128 * p, 128 * (p + 1))
                qp = q_ref[pl.ds(r0, BLOCK), pc].astype(F32)
                qn_scr[pc, :] = (qp * _head_rstd(qp, lo) * qgs).astype(BF16)
            s_scr[...] = _dot_nt(k2t, qn_scr[...])
            for p in range(npair):
                for hf in range(2):
                    g = 2 * p + hf
                    pn, _, _ = _softmax_cols(s_scr, btab, p, var, g, hf, sinks_ref[kvh * GROUP + g])
                    p_scr[2 * BLOCK * hf:2 * BLOCK * (hf + 1), 128 * p:128 * (p + 1)] = pn.astype(BF16)
            o_all = _dot_tn(v2, p_scr[...]).T
            for p in range(npair):
                pc = slice(128 * p, 128 * (p + 1))
                o_ref[pl.ds(r0, BLOCK), pc] = o_all[pc, :].astype(BF16)

        def trip(i, carry):
            for slot in range(per_trip):
                block(per_trip * i + slot, slot)
            return carry

        lax.fori_loop(0, nb // per_trip, trip, 0)

    smem = pl.BlockSpec(memory_space=pltpu.SMEM)
    kvbuf = pltpu.VMEM((S + BLOCK, 128), BF16)
    pair_scores = (per_trip, npair * BLOCK, 4 * BLOCK)
    return pl.pallas_call(
        body, name="attn_fwd", grid=(B, N_KV_HEADS),
        in_specs=[smem, smem,
                  pl.BlockSpec((S, half), lambda b, h: (b, COL_Q // half + h)),
                  pl.BlockSpec((S, 128), lambda b, h: (b, COL_K // 128)),
                  pl.BlockSpec((S, 128), lambda b, h: (b, COL_V // 128)),
                  _full((1, 128)), _full((1, 128))],
        out_specs=pl.BlockSpec((S, half), lambda b, h: (b, h)),
        out_shape=jax.ShapeDtypeStruct((T, Q_W), BF16),
        scratch_shapes=[kvbuf, kvbuf, kvbuf, kvbuf, pltpu.VMEM((2, GROUP, 2 * BLOCK, BLOCK), F32),
                        pltpu.VMEM((per_trip, npair * BLOCK, 128), BF16), pltpu.VMEM(pair_scores, F32),
                        pltpu.VMEM(pair_scores, BF16)],
        compiler_params=_cp("parallel", "arbitrary"),
    )(sinks, slopes, u, u, u, qg2, kg2)


def _merge_call(c, o, u, x, wc, wa, wm, g_ffn, tm=512):
    T = x.shape[0]

    def body(c_ref, o_ref, gc_ref, ga_ref, x_ref, wc_ref, wa_ref, wm_ref, g_ref, yc_ref, ya_ref, z_ref, h_ref, hn_ref):
        yc = _dot(c_ref[...], wc_ref[...])
        ya = _dot(o_ref[...], wa_ref[...])
        yc_ref[...] = yc.astype(BF16)
        ya_ref[...] = ya.astype(BF16)
        z = (_sigmoid(gc_ref[...].astype(F32)) * yc + _sigmoid(ga_ref[...].astype(F32)) * ya).astype(BF16)
        z_ref[...] = z
        h = x_ref[...] + _dot(z, wm_ref[...])
        h_ref[...] = h
        r = lax.rsqrt(jnp.mean(h * h, axis=-1, keepdims=True) + EPS)
        hn_ref[...] = (h * r * g_ref[...]).astype(BF16)

    sq = _full((D_MODEL, D_MODEL))
    row = _rows(tm, D_MODEL)
    f32 = jax.ShapeDtypeStruct((T, D_MODEL), F32)
    b16 = jax.ShapeDtypeStruct((T, D_MODEL), BF16)
    return pl.pallas_call(
        body, name="merge_fwd", grid=(T // tm,),
        in_specs=[row, row, _rows(tm, D_MODEL, COL_GC // D_MODEL), _rows(tm, D_MODEL, COL_GA // D_MODEL), row,
                  sq, sq, sq, _full((1, D_MODEL))],
        out_specs=[row] * 5,
        out_shape=[b16, b16, b16, f32, b16],
        compiler_params=_cp("parallel"),
    )(c, o, u, u, x, wc, wa, wm, g_ffn)


def _ffn_in_call(hn, w, tm=512):
    T = hn.shape[0]

    def body(hn_ref, w_ref, fac_ref, act_ref):
        gu = _dot_nt(hn_ref[...], w_ref[...])
        g = gu[:, :D_FF]
        up = gu[:, D_FF:]
        sg = _sigmoid(g)
        silu = g * sg
        act_ref[...] = (silu * up).astype(BF16)
        fac_ref[:, :D_FF] = (up * (sg * (1.0 + g * (1.0 - sg)))).astype(BF16)
        fac_ref[:, D_FF:] = silu.astype(BF16)

    return pl.pallas_call(
        body, name="ffn_in", grid=(T // tm,),
        in_specs=[_rows(tm, D_MODEL), _full(w.shape)],
        out_specs=[_rows(tm, 2 * D_FF), _rows(tm, D_FF)],
        out_shape=[jax.ShapeDtypeStruct((T, 2 * D_FF), BF16), jax.ShapeDtypeStruct((T, D_FF), BF16)],
        compiler_params=_cp("parallel"),
    )(hn, w)


def _ffn_out_call(act, wd, h, tgt, tm=512):
    T = h.shape[0]

    def body(act_ref, wd_ref, h_ref, t_ref, sse_ref, d_ref, db_ref):
        e = h_ref[...] + _dot(act_ref[...], wd_ref[...]) - t_ref[...]

        @pl.when(pl.program_id(0) == 0)
        def _():
            sse_ref[...] = jnp.zeros_like(sse_ref)

        sse_ref[...] += jnp.sum(e * e)
        d = e * (1.0 / D_MODEL)
        d_ref[...] = d
        db_ref[...] = d.astype(BF16)

    row = _rows(tm, D_MODEL)
    return pl.pallas_call(
        body, name="ffn_out", grid=(T // tm,),
        in_specs=[_rows(tm, D_FF), _full(wd.shape), row, row],
        out_specs=[_full((8, 128)), row, row],
        out_shape=[jax.ShapeDtypeStruct((8, 128), F32), jax.ShapeDtypeStruct((T, D_MODEL), F32),
                   jax.ShapeDtypeStruct((T, D_MODEL), BF16)],
        compiler_params=_cp("arbitrary"),
    )(act, wd, h, tgt)


def _dw_call(name, a, b, tr, tk=2048):
    T, K = a.shape
    N = b.shape[1]
    tk = min(tk, T)
    nk = T // tk

    def body(a_ref, b_ref, o_ref, acc_ref):
        k = pl.program_id(1)

        @pl.when(k == 0)
        def _():
            acc_ref[...] = jnp.zeros_like(acc_ref)

        acc_ref[...] += _dot_tn(a_ref[...], b_ref[...])

        @pl.when(k == nk - 1)
        def _():
            o_ref[...] = acc_ref[...].astype(BF16)

    return pl.pallas_call(
        body, name=name, grid=(K // tr, nk),
        in_specs=[pl.BlockSpec((tk, tr), lambda j, k: (k, j)), pl.BlockSpec((tk, N), lambda j, k: (k, 0))],
        out_specs=pl.BlockSpec((tr, N), lambda j, k: (j, 0)),
        out_shape=jax.ShapeDtypeStruct((K, N), BF16),
        scratch_shapes=[pltpu.VMEM((tr, N), F32)],
        compiler_params=_cp("parallel", "arbitrary"),
    )(a, b)


def _concat_rows_call(name, pieces, tr=256):
    N = pieces[0].shape[1]
    starts = [0]
    for p in pieces:
        starts.append(starts[-1] + p.shape[0] // tr)

    def body(*refs):
        o_ref = refs[-1]
        i = pl.program_id(0)
        for p, ref in enumerate(refs[:-1]):
            @pl.when((i >= starts[p]) & (i < starts[p + 1]))
            def _(ref=ref):
                o_ref[...] = ref[...]

    def spec(p):
        lo, n = starts[p], pieces[p].shape[0] // tr
        return pl.BlockSpec((tr, N), lambda i: (jnp.clip(i - lo, 0, n - 1), 0))

    return pl.pallas_call(
        body, name=name, grid=(starts[-1],),
        in_specs=[spec(p) for p in range(len(pieces))], out_specs=pl.BlockSpec((tr, N), lambda i: (i, 0)),
        out_shape=jax.ShapeDtypeStruct((starts[-1] * tr, N), pieces[0].dtype),
        compiler_params=_cp("arbitrary"),
    )(*pieces)


def _rms_bwd(dy, xv, g):
    r = lax.rsqrt(jnp.mean(xv * xv, axis=-1, keepdims=True) + EPS)
    t = dy * g
    dx = r * t - xv * (r * r * r * jnp.mean(t * xv, axis=-1, keepdims=True))
    return dx, dy * xv * r


def _ffn_down_bwd_call(dout_b, wd, gu, tm=512):
    T = dout_b.shape[0]

    def body(d_ref, wd_ref, fg_ref, fu_ref, o_ref):
        dact = _dot_nt(d_ref[...], wd_ref[...])
        o_ref[:, :D_FF] = (dact * fg_ref[...].astype(F32)).astype(BF16)
        o_ref[:, D_FF:] = (dact * fu_ref[...].astype(F32)).astype(BF16)

    return pl.pallas_call(
        body, name="ffn_down_bwd", grid=(T // tm,),
        in_specs=[_rows(tm, D_MODEL), _full(wd.shape), _rows(tm, D_FF, 0), _rows(tm, D_FF, 1)],
        out_specs=_rows(tm, 2 * D_FF),
        out_shape=jax.ShapeDtypeStruct((T, 2 * D_FF), BF16),
        compiler_params=_cp("parallel"),
    )(dout_b, wd, gu, gu)


def _ffn_in_bwd_call(dgu, w, h, g_ffn, dout, tm=512):
    T = h.shape[0]

    def body(dgu_ref, w_ref, h_ref, g_ref, d_ref, dh_ref, dhb_ref, dg_ref):
        dhn = _dot(dgu_ref[...], w_ref[...])
        dx, dgt = _rms_bwd(dhn, h_ref[...], g_ref[...])
        dh = d_ref[...] + dx
        dh_ref[...] = dh
        dhb_ref[...] = dh.astype(BF16)

        @pl.when(pl.program_id(0) == 0)
        def _():
            dg_ref[...] = jnp.zeros_like(dg_ref)

        dg_ref[...] += jnp.sum(dgt, axis=0, keepdims=True)

    row = _rows(tm, D_MODEL)
    return pl.pallas_call(
        body, name="ffn_in_bwd", grid=(T // tm,),
        in_specs=[_rows(tm, 2 * D_FF), _full(w.shape), row, _full((1, D_MODEL)), row],
        out_specs=[row, row, _full((1, D_MODEL))],
        out_shape=[jax.ShapeDtypeStruct((T, D_MODEL), F32), jax.ShapeDtypeStruct((T, D_MODEL), BF16),
                   jax.ShapeDtypeStruct((1, D_MODEL), F32)],
        compiler_params=_cp("arbitrary"),
    )(dgu, w, h, g_ffn, dout)


def _merge_bwd_call(dh_b, wm, wc, wa, u, yc, ya, tm=512):
    T = dh_b.shape[0]

    def body(dh_ref, wm_ref, wc_ref, wa_ref, gc_ref, ga_ref, yc_ref, ya_ref, dyc_ref, dya_ref, dug_ref, dcp_ref, do_ref):
        dz = _dot_nt(dh_ref[...], wm_ref[...])
        sgc = _sigmoid(gc_ref[...].astype(F32))
        sga = _sigmoid(ga_ref[...].astype(F32))
        dyc = (dz * sgc).astype(BF16)
        dya = (dz * sga).astype(BF16)
        dyc_ref[...] = dyc
        dya_ref[...] = dya
        dug_ref[:, :D_MODEL] = (dz * yc_ref[...].astype(F32) * (sgc * (1.0 - sgc))).astype(BF16)
        dug_ref[:, D_MODEL:] = (dz * ya_ref[...].astype(F32) * (sga * (1.0 - sga))).astype(BF16)
        dcp_ref[...] = _dot_nt(dyc, wc_ref[...])
        do_ref[...] = _dot_nt(dya, wa_ref[...]).astype(BF16)

    sq = _full((D_MODEL, D_MODEL))
    row = _rows(tm, D_MODEL)
    b16 = jax.ShapeDtypeStruct((T, D_MODEL), BF16)
    return pl.pallas_call(
        body, name="merge_bwd", grid=(T // tm,),
        in_specs=[row, sq, sq, sq, _rows(tm, D_MODEL, COL_GC // D_MODEL), _rows(tm, D_MODEL, COL_GA // D_MODEL), row, row],
        out_specs=[row, row, _rows(tm, 2 * D_MODEL), row, row],
        out_shape=[b16, b16, jax.ShapeDtypeStruct((T, 2 * D_MODEL), BF16), jax.ShapeDtypeStruct((T, D_MODEL), F32), b16],
        compiler_params=_cp("parallel"),
    )(dh_b, wm, wc, wa, u, u, yc, ya)


def _conv_bwd_call(dcp, hc, u, conv_w, ln_g, ln_b, B, S, ts=256):
    T = B * S
    nS = S // ts
    per32 = ts // HALO
    last32 = T // HALO - 1

    def ln_bwd(dcp_v, hc_v, g, bb):
        mu = jnp.mean(hc_v, axis=-1, keepdims=True)
        xc = hc_v - mu
        rstd = lax.rsqrt(jnp.mean(xc * xc, axis=-1, keepdims=True) + EPS)
        xhat = xc * rstd
        y = xhat * g + bb
        sy = _sigmoid(y)
        dy = dcp_v * (sy * (1.0 + y * (1.0 - sy)))
        dxh = dy * g
        dhc = rstd * (dxh - jnp.mean(dxh, axis=-1, keepdims=True) - xhat * jnp.mean(dxh * xhat, axis=-1, keepdims=True))
        return dhc, dy * xhat, dy

    def body(dcp_ref, dcpn_ref, hc_ref, hcn_ref, a_ref, gate_ref, w_ref, g_ref, bb_ref,
             du_ref, dw_ref, db_ref, dlg_ref, dlb_ref, dhext, glu_scr, dhsh, acc_ref):
        b = pl.program_id(0)
        s = pl.program_id(1)

        @pl.when((b == 0) & (s == 0))
        def _():
            dw_ref[...] = jnp.zeros_like(dw_ref)
            db_ref[...] = jnp.zeros_like(db_ref)
            dlg_ref[...] = jnp.zeros_like(dlg_ref)
            dlb_ref[...] = jnp.zeros_like(dlb_ref)

        dhc, dgt, dbt = ln_bwd(dcp_ref[...], hc_ref[...], g_ref[...], bb_ref[...])
        dhn, _, _ = ln_bwd(dcpn_ref[...], hcn_ref[...], g_ref[...], bb_ref[...])
        dhext[0:ts, :] = dhc
        dhext[ts:, :] = jnp.where(s < nS - 1, dhn, 0.0)
        db_ref[...] += jnp.sum(dhc, axis=0, keepdims=True)
        dlg_ref[...] += jnp.sum(dgt, axis=0, keepdims=True)
        dlb_ref[...] += jnp.sum(dbt, axis=0, keepdims=True)

        sg = _sigmoid(gate_ref[...].astype(F32))
        av = a_ref[...].astype(F32)
        glu_scr[...] = av * sg
        _fill_shifted(dhext, dhsh)

        def colblock(cb, carry):
            cols = pl.ds(pl.multiple_of(cb * 128, 128), 128)
            outs = _conv_taps(w_ref, dhext, dhsh, 0, ts, cb, reverse=True)
            for rb, acc in enumerate(outs):
                acc_ref[rb * 128:(rb + 1) * 128, cols] = acc
            taps = [jnp.zeros((8, 128), F32) for _ in range(CONV_WIDTH)]
            for r0 in range(0, ts, 128):
                glu = [glu_scr[r0 + 8 * i:r0 + 8 * (i + 1), cols] for i in range(16)]
                for s8 in range(8):
                    users = {}
                    for j in range(CONV_WIDTH):
                        if (CONV_WIDTH - 1 - j) % 8 == s8:
                            for i in range(16):
                                users.setdefault(CONV_WIDTH - 1 - j + r0 + 8 * i - s8, []).append((j, i))
                    for row, pairs in sorted(users.items()):
                        dhc8 = dhext[row:row + 8, cols] if s8 == 0 else dhsh[s8 - 1, row:row + 8, cols]
                        for j, i in pairs:
                            taps[j] = taps[j] + glu[i] * dhc8
            for j in range(CONV_WIDTH):
                dw_ref[8 * j:8 * (j + 1), cols] += taps[j]
            return carry

        lax.fori_loop(0, C_CONV // 128, colblock, 0)
        dglu = acc_ref[...]
        du_ref[:, :C_CONV] = (dglu * sg).astype(BF16)
        du_ref[:, C_CONV:] = (dglu * av * (sg * (1.0 - sg))).astype(BF16)

    cur = lambda cb: pl.BlockSpec((ts, C_CONV), lambda b, s: (b * nS + s, cb))
    nxt = pl.BlockSpec((HALO, C_CONV), lambda b, s: (jnp.minimum(b * (S // HALO) + (s + 1) * per32, last32), 0))
    vec = _full((1, C_CONV))
    return pl.pallas_call(
        body, name="conv_bwd", grid=(B, nS),
        in_specs=[cur(0), nxt, cur(0), nxt, cur(0), cur(1), _full((CONV_WIDTH, C_CONV)), vec, vec],
        out_specs=[pl.BlockSpec((ts, 2 * C_CONV), lambda b, s: (b * nS + s, 0)), _full((8 * CONV_WIDTH, C_CONV)), vec, vec, vec],
        out_shape=[jax.ShapeDtypeStruct((T, 2 * C_CONV), BF16), jax.ShapeDtypeStruct((8 * CONV_WIDTH, C_CONV), F32),
                   jax.ShapeDtypeStruct((1, C_CONV), F32), jax.ShapeDtypeStruct((1, C_CONV), F32),
                   jax.ShapeDtypeStruct((1, C_CONV), F32)],
        scratch_shapes=[pltpu.VMEM((ts + HALO, C_CONV), F32), pltpu.VMEM((ts, C_CONV), F32),
                        pltpu.VMEM((7, ts + HALO, C_CONV), F32), pltpu.VMEM((ts, C_CONV), F32)],
        compiler_params=_cp("arbitrary", "arbitrary"),
    )(dcp, dcp, hc, hc, u, u, conv_w, ln_g, ln_b)


def _attn_bwd_call(u, do, qg2, kg2, sinks, slopes, B, S):
    T = B * S
    nb = S // BLOCK
    half = Q_W // 2
    npair = GROUP // 2
    per_trip = 8 if nb % 8 == 0 else 4

    def body(sinks_ref, slopes_ref, q_ref, k_ref, v_ref, do_ref, qg_ref, kg_ref,
             dq_ref, dkv_ref, dqg_ref, dkg_ref, dsk_ref, kA, kB, vA, vB, dkn_acc, dv_acc, dkv_keep,
             btab, s_buf, dp_buf, p_buf, ds_buf, qn_buf, do_buf, rq_buf):
        b = pl.program_id(0)
        kvh = pl.program_id(1)
        lo = _lane_lo()
        lane = lax.broadcasted_iota(jnp.int32, (1, 128), 1)
        first = kvh == 0

        @pl.when((b == 0) & first)
        def _():
            dqg_ref[...] = jnp.zeros_like(dqg_ref)
            dkg_ref[...] = jnp.zeros_like(dkg_ref)
            dsk_ref[...] = jnp.zeros_like(dsk_ref)

        _stage_kv(kvh, k_ref, v_ref, kg_ref, lo, kA, kB, vA, vB)
        dkn_acc[...] = jnp.zeros_like(dkn_acc)
        dv_acc[...] = jnp.zeros_like(dv_acc)
        _fill_bias(btab, kvh, slopes_ref)
        qgs = qg_ref[...] * ATTN_SCALE

        def fold(acc):
            canon = jnp.where(lo, acc[0:2 * BLOCK] + pltpu.roll(acc[2 * BLOCK:], HEAD_DIM, 1), 0.0)
            return jnp.where(first, canon, pltpu.roll(canon, HEAD_DIM, 1))

        def block(n, slot):
            s_scr, dp_scr, p_scr, ds_scr = s_buf.at[slot], dp_buf.at[slot], p_buf.at[slot], ds_buf.at[slot]
            qn_scr, do_scr, rq_scr = qn_buf.at[slot], do_buf.at[slot], rq_buf.at[slot]
            r0 = pl.multiple_of(n * BLOCK, BLOCK)
            win = pl.ds(r0, 2 * BLOCK)
            rows = pl.ds(r0, BLOCK)
            k2t = jnp.concatenate([kA[win, :], kB[win, :]], axis=0)
            v2 = jnp.concatenate([vA[win, :], vB[win, :]], axis=0)
            var = jnp.minimum(n, 1)
            for p in range(npair):
                pc = slice(128 * p, 128 * (p + 1))
                qp = q_ref[rows, pc].astype(F32)
                rq = _head_rstd(qp, lo)
                rq_scr[pc, :] = rq
                qn_scr[pc, :] = (qp * rq * qgs).astype(BF16)
                do_scr[pc, :] = do_ref[rows, pc]
            s_scr[...] = _dot_nt(k2t, qn_scr[...])
            dp_scr[...] = _dot_nt(v2, do_scr[...])
            for p in range(npair):
                for hf in range(2):
                    g = 2 * p + hf
                    keys, qs = slice(2 * BLOCK * hf, 2 * BLOCK * (hf + 1)), slice(128 * p, 128 * (p + 1))
                    pn, esk, rz = _softmax_cols(s_scr, btab, p, var, g, hf, sinks_ref[kvh * GROUP + g])
                    dph = dp_scr[keys, qs]
                    delta = jnp.sum(pn * dph, axis=0, keepdims=True)
                    ds_scr[keys, qs] = (pn * (dph - delta)).astype(BF16)
                    p_scr[keys, qs] = pn.astype(BF16)
                    dsk_ref[...] += jnp.where(lane == kvh * GROUP + g, -jnp.sum(esk * rz * delta), 0.0)
            dqs_all = _dot_tn(k2t, ds_scr[...]).T
            for p in range(npair):
                pc = slice(128 * p, 128 * (p + 1))
                qp = q_ref[rows, pc].astype(F32)
                rq = rq_scr[pc, :]
                dqs = dqs_all[pc, :]
                t = dqs * qgs
                dq = rq * t - qp * (rq * rq * rq * (_half_sums(t * qp, lo) * (1.0 / HEAD_DIM)))
                dq_ref[rows, pc] = dq.astype(BF16)
                dqg_ref[...] += jnp.sum(dqs * qp * rq, axis=0, keepdims=True) * ATTN_SCALE
            dkn_acc[win, :] += fold(_dot(ds_scr[...], qn_scr[...]))
            dv_acc[win, :] += fold(_dot(p_scr[...], do_scr[...]))

        def trip(i, carry):
            for slot in range(per_trip):
                block(per_trip * i + slot, slot)
            return carry

        lax.fori_loop(0, nb // per_trip, trip, 0)

        dkn = dkn_acc[BLOCK:, :]
        kv = k_ref[...].astype(F32)
        kg = kg_ref[...]
        rk = _head_rstd(kv, lo)
        t = dkn * kg
        dk = rk * t - kv * (rk * rk * rk * (_half_sums(t * kv, lo) * (1.0 / HEAD_DIM)))
        dkg_ref[...] += jnp.sum(dkn * kv * rk, axis=0, keepdims=True)
        dv = dv_acc[BLOCK:, :]

        @pl.when(first)
        def _():
            dkv_keep[:, :128] = dk
            dkv_keep[:, 128:] = dv

        @pl.when(kvh == N_KV_HEADS - 1)
        def _():
            dkv_ref[:, :128] = (dkv_keep[:, :128] + dk).astype(BF16)
            dkv_ref[:, 128:] = (dkv_keep[:, 128:] + dv).astype(BF16)

    smem = pl.BlockSpec(memory_space=pltpu.SMEM)
    kvbuf = pltpu.VMEM((S + BLOCK, 128), BF16)
    accbuf = pltpu.VMEM((S + BLOCK, 128), F32)
    vec = _full((1, 128))
    return pl.pallas_call(
        body, name="attn_bwd", grid=(B, N_KV_HEADS),
        in_specs=[smem, smem,
                  pl.BlockSpec((S, half), lambda b, h: (b, COL_Q // half + h)),
                  pl.BlockSpec((S, 128), lambda b, h: (b, COL_K // 128)),
                  pl.BlockSpec((S, 128), lambda b, h: (b, COL_V // 128)),
                  pl.BlockSpec((S, half), lambda b, h: (b, h)),
                  vec, vec],
        out_specs=[pl.BlockSpec((S, half), lambda b, h: (b, h)), pl.BlockSpec((S, 256), lambda b, h: (b, 0)), vec, vec, vec],
        out_shape=[jax.ShapeDtypeStruct((T, Q_W), BF16), jax.ShapeDtypeStruct((T, 2 * KV_W), BF16),
                   jax.ShapeDtypeStruct((1, 128), F32), jax.ShapeDtypeStruct((1, 128), F32), jax.ShapeDtypeStruct((1, 128), F32)],
        scratch_shapes=[kvbuf, kvbuf, kvbuf, kvbuf, accbuf, accbuf, pltpu.VMEM((S, 256), F32),
                        pltpu.VMEM((2, GROUP, 2 * BLOCK, BLOCK), F32),
                        pltpu.VMEM((per_trip, npair * BLOCK, 4 * BLOCK), F32), pltpu.VMEM((per_trip, npair * BLOCK, 4 * BLOCK), F32),
                        pltpu.VMEM((per_trip, npair * BLOCK, 4 * BLOCK), BF16), pltpu.VMEM((per_trip, npair * BLOCK, 4 * BLOCK), BF16),
                        pltpu.VMEM((per_trip, npair * BLOCK, 128), BF16), pltpu.VMEM((per_trip, npair * BLOCK, 128), BF16),
                        pltpu.VMEM((per_trip, npair * BLOCK, 128), F32)],
        compiler_params=_cp("arbitrary", "arbitrary"),
    )(sinks, slopes, u, u, u, do, qg2, kg2)


def _inproj_bwd_call(du_conv, du_g, du_q, du_kv, w, x, g, dh, tm=512):
    T = x.shape[0]

    def body(dc_ref, dg_ref, dq_ref, dkv_ref, w_ref, x_ref, g_ref, dh_ref, gx_ref, dgm_ref):
        dxn = _dot(dc_ref[...], w_ref[COL_A:COL_GC, :])
        dxn += _dot(dg_ref[...], w_ref[COL_GC:COL_Q, :])
        dxn += _dot(dq_ref[...], w_ref[COL_Q:COL_K, :])
        dxn += _dot(dkv_ref[...], w_ref[COL_K:, :])
        dx, dgt = _rms_bwd(dxn, x_ref[...], g_ref[...])
        gx_ref[...] = dh_ref[...] + dx

        @pl.when(pl.program_id(0) == 0)
        def _():
            dgm_ref[...] = jnp.zeros_like(dgm_ref)

        dgm_ref[...] += jnp.sum(dgt, axis=0, keepdims=True)

    row = _rows(tm, D_MODEL)
    return pl.pallas_call(
        body, name="inproj_bwd", grid=(T // tm,),
        in_specs=[_rows(tm, 2 * C_CONV), _rows(tm, 2 * D_MODEL), _rows(tm, Q_W), _rows(tm, 2 * KV_W), _full(w.shape),
                  row, _full((1, D_MODEL)), row],
        out_specs=[row, _full((1, D_MODEL))],
        out_shape=[jax.ShapeDtypeStruct((T, D_MODEL), F32), jax.ShapeDtypeStruct((1, D_MODEL), F32)],
        compiler_params=_cp("arbitrary"),
    )(du_conv, du_g, du_q, du_kv, w, x, g, dh)


def _adamw_math(w, g, m, v):
    m = ADAM_B1 * m + (1.0 - ADAM_B1) * g
    v = ADAM_B2 * v + (1.0 - ADAM_B2) * (g * g)
    m_hat = m / (1.0 - ADAM_B1 ** ADAM_STEP)
    v_hat = v / (1.0 - ADAM_B2 ** ADAM_STEP)
    delta = -ADAM_LR * (m_hat / (jnp.sqrt(v_hat) + ADAM_EPS) + ADAM_WD * w)
    return delta, m, v


def _adamw_sum_call(name, parts, own, w, m, v, tr):
    R, C = w.shape

    def body(p_ref, own_ref, w_ref, m_ref, v_ref, g_ref, d_ref, nm_ref, nv_ref):
        x, y, c = _position()
        me = 4 * x + 2 * y + c
        g = jnp.zeros((tr, C), F32)
        for s in range(N_DEV):
            g = g + jnp.where(me == s, own_ref[...], p_ref[s]).astype(F32)
        g_ref[...] = g
        d_ref[...], nm_ref[...], nv_ref[...] = _adamw_math(w_ref[...], g, m_ref[...], v_ref[...])

    blk = pl.BlockSpec((tr, C), lambda i: (i, 0))
    out = jax.ShapeDtypeStruct((R, C), F32)
    return pl.pallas_call(
        body, name=name, grid=(R // tr,),
        in_specs=[pl.BlockSpec((N_DEV, tr, C), lambda i: (0, i, 0)), blk, blk, blk, blk],
        out_specs=[blk] * 4, out_shape=[out] * 4,
        compiler_params=_cp("parallel"),
    )(parts, own, w, m, v)


def _adamw_small_call(name, g, w, m, v):
    def body(g_ref, w_ref, m_ref, v_ref, d_ref, nm_ref, nv_ref):
        d_ref[...], nm_ref[...], nv_ref[...] = _adamw_math(w_ref[...], g_ref[...], m_ref[...], v_ref[...])

    out = jax.ShapeDtypeStruct(w.shape, F32)
    return pl.pallas_call(body, name=name, out_shape=[out] * 3)(g, w, m, v)


def _position():
    return lax.axis_index("x"), lax.axis_index("y"), lax.axis_index("c")


def _peer(x, y, c, rel):
    fx, fy, fc = (rel >> 2) & 1, (rel >> 1) & 1, rel & 1
    px = 1 - x if fx else x
    py = 1 - y if fy else y
    pc = 1 - c if fc else c
    return (px, py, pc), 4 * px + 2 * py + pc


def _gather_weights_call(shards):
    n = len(shards)

    def body(*refs):
        ins, outs = refs[:n], refs[n:2 * n]
        send_sems, recv_sems, local_sems = refs[2 * n:]
        x, y, c = _position()
        me = 4 * x + 2 * y + c
        sib, sib_id = _peer(x, y, c, 1)
        chips = [_peer(x, y, c, rel) for rel in (4, 2, 6)]

        def copy(i, k, src, slot, to):
            return pltpu.make_async_remote_copy(
                src_ref=src, dst_ref=outs[i].at[slot], send_sem=send_sems.at[i, k], recv_sem=recv_sems.at[i, k],
                device_id=to, device_id_type=MESH)

        mine = [pltpu.make_async_copy(ins[i], outs[i].at[me], local_sems.at[i]) for i in range(n)]
        for cp in mine:
            cp.start()
        first = []
        for i in range(n):
            first.append(copy(i, 0, ins[i], me, sib))
            first += [copy(i, 1 + j, ins[i], me, dev) for j, (dev, _) in enumerate(chips)]
        for cp in first:
            cp.start()
        passed = []
        for i in range(n):
            for j, (dev, slot) in enumerate(chips):
                copy(i, 1 + j, ins[i], slot, dev).wait_recv()
                fwd = copy(i, 4 + j, outs[i].at[slot], slot, sib)
                fwd.start()
                passed.append(fwd)
        for i in range(n):
            copy(i, 0, ins[i], sib_id, sib).wait_recv()
            for j, (_, slot) in enumerate(chips):
                copy(i, 4 + j, ins[i], slot ^ 1, sib).wait_recv()
        for cp in first + passed:
            cp.wait_send()
        for cp in mine:
            cp.wait()

    hbm = pl.BlockSpec(memory_space=pl.ANY)
    return pl.pallas_call(
        body, name="gather_weights",
        in_specs=[hbm] * n, out_specs=[hbm] * n,
        out_shape=[jax.ShapeDtypeStruct((N_DEV,) + s.shape, s.dtype) for s in shards],
        scratch_shapes=[pltpu.SemaphoreType.DMA((n, 7)), pltpu.SemaphoreType.DMA((n, 7)), pltpu.SemaphoreType.DMA((n,))],
    )(*shards)


_HBM = pl.BlockSpec(memory_space=pltpu.HBM)
_SEM = pl.BlockSpec(memory_space=pltpu.SEMAPHORE)
_EFFECT = pltpu.SideEffectType.DATAFLOW_SIDE_EFFECTING


def _exchange_copies(scatter, incoming, src_refs, land_refs, send_sems, recv_sems):
    x, y, c = _position()
    me = 4 * x + 2 * y + c
    out = []
    for i, (src, land) in enumerate(zip(src_refs, land_refs)):
        for k in range(N_DEV - 1):
            dev, dev_id = _peer(x, y, c, k + 1)
            j = i * (N_DEV - 1) + k
            out.append(pltpu.make_async_remote_copy(
                src_ref=src.at[dev_id] if scatter else src, dst_ref=land.at[dev_id if incoming else me],
                send_sem=send_sems.at[j], recv_sem=recv_sems.at[j], device_id=dev, device_id_type=MESH))
    return out


def _exchange_start_call(name, arrays, scatter):
    n = len(arrays)
    lands = [lax.empty((N_DEV,) + (a.shape[1:] if scatter else a.shape), a.dtype) for a in arrays]

    def body(*refs):
        src_refs, land_refs = refs[:n], refs[n:2 * n]
        send_sems, recv_sems, token = refs[2 * n], refs[2 * n + 1], refs[-1]
        for copy in _exchange_copies(scatter, False, src_refs, land_refs, send_sems, recv_sems):
            copy.start()
        token[...] = jnp.zeros_like(token)

    sems = pltpu.SemaphoreType.DMA((n * (N_DEV - 1),))
    res = pl.pallas_call(
        body, name=name,
        out_shape=[sems, sems] + [pltpu.HBM(a.shape, a.dtype) for a in arrays] + [pltpu.HBM(l.shape, l.dtype) for l in lands]
        + [jax.ShapeDtypeStruct((8, 128), F32)],
        in_specs=[_HBM] * (2 * n), out_specs=[_SEM, _SEM] + [_HBM] * (2 * n) + [pl.BlockSpec(memory_space=pltpu.VMEM)],
        input_output_aliases={i: 2 + i for i in range(2 * n)},
        compiler_params=pltpu.CompilerParams(has_side_effects=_EFFECT),
    )(*[pltpu.with_memory_space_constraint(a, pltpu.HBM) for a in list(arrays) + lands])
    return res[0], res[1], res[2:2 + n], res[2 + n:2 + 2 * n], res[-1]


def _exchange_wait_call(name, started, scatter, after):
    send_sems, recv_sems, srcs, lands, _ = started
    n = len(srcs)

    def body(*refs):
        src_refs, land_refs = refs[:n], refs[n:2 * n]
        send_s, recv_s = refs[2 * n], refs[2 * n + 1]
        for copy in _exchange_copies(scatter, False, src_refs, land_refs, send_s, recv_s):
            copy.wait_send()
        for copy in _exchange_copies(scatter, True, src_refs, land_refs, send_s, recv_s):
            copy.wait_recv()

    res = pl.pallas_call(
        body, name=name,
        out_shape=[pltpu.HBM(a.shape, a.dtype) for a in list(srcs) + list(lands)],
        in_specs=[_HBM] * (2 * n) + [_SEM, _SEM, pl.BlockSpec(memory_space=pl.ANY)], out_specs=[_HBM] * (2 * n),
        input_output_aliases={i: i for i in range(2 * n)},
        compiler_params=pltpu.CompilerParams(has_side_effects=_EFFECT),
    )(*srcs, *lands, send_sems, recv_sems, after)
    return res[n:]


def _allsum_small_call(part, blocks):
    P, Q = part.shape[0], blocks.shape[1]

    def body(p_ref, b_ref, o_ref, ob_ref, all_ref, allb_ref, send_sems, recv_sems):
        x, y, c = _position()
        me = 4 * x + 2 * y + c
        peers = [_peer(x, y, c, rel) for rel in range(1, N_DEV)]
        all_ref[me] = p_ref[...]
        allb_ref[me] = b_ref[me]

        def copies(k, slot, dev, dev_id):
            sems = lambda i: dict(send_sem=send_sems.at[i], recv_sem=recv_sems.at[i], device_id=dev, device_id_type=MESH)
            return (pltpu.make_async_remote_copy(src_ref=p_ref, dst_ref=all_ref.at[slot], **sems(k)),
                    pltpu.make_async_remote_copy(src_ref=b_ref.at[dev_id], dst_ref=allb_ref.at[slot], **sems(7 + k)))

        sends = [cp for k, (dev, dev_id) in enumerate(peers) for cp in copies(k, me, dev, dev_id)]
        for cp in sends:
            cp.start()
        for k, (dev, dev_id) in enumerate(peers):
            for cp in copies(k, dev_id, dev, dev_id):
                cp.wait_recv()
        for cp in sends:
            cp.wait_send()
        tot, totb = all_ref[0], allb_ref[0]
        for s in range(1, N_DEV):
            tot, totb = tot + all_ref[s], totb + allb_ref[s]
        o_ref[...] = tot
        ob_ref[...] = totb

    vm = pl.BlockSpec(memory_space=pltpu.VMEM)
    return pl.pallas_call(
        body, name="allsum_small", in_specs=[vm, vm], out_specs=[vm, vm],
        out_shape=[jax.ShapeDtypeStruct((P, 128), F32), jax.ShapeDtypeStruct((Q, 128), F32)],
        scratch_shapes=[pltpu.VMEM((N_DEV, P, 128), F32), pltpu.VMEM((N_DEV, Q, 128), F32),
                        pltpu.SemaphoreType.DMA((14,)), pltpu.SemaphoreType.DMA((14,))],
    )(part, blocks)


def _pad_rows(v, rows):
    flat = v.reshape(-1)
    return jnp.pad(flat, (0, rows * 128 - flat.shape[0])).reshape(rows, 128)


def _local_step(x, tgt, w_in_i, conv_w, conv_b, ln_g, ln_b, qg, kg, sinks, g_mix, g_ffn, token0, rest_weights, grads_ready, B, S):
    row = lambda v: v.reshape(1, -1)
    qg2 = jnp.tile(qg, 2).reshape(1, 128)
    kg2 = jnp.tile(kg, 2).reshape(1, 128)
    heads = jnp.arange(1, N_Q_HEADS + 1, dtype=F32)
    slopes = jnp.exp2(-8.0 * heads / N_Q_HEADS)

    u, xn = _inproj_call(x, row(g_mix) + token0, w_in_i)
    hc, c = _conv_fwd_call(u, conv_w, row(conv_b), row(ln_g), row(ln_b), B, S)
    o = _attn_fwd_call(u, qg2, kg2, sinks, slopes, B, S)
    wc, wa, wm, w_ffn_in, wd = rest_weights(o)
    yc, ya, z, h, hn = _merge_call(c, o, u, x, wc, wa, wm, row(g_ffn))
    gu, act = _ffn_in_call(hn, w_ffn_in)
    sse, dout, dout_b = _ffn_out_call(act, wd, h, tgt)

    dgu = _ffn_down_bwd_call(dout_b, wd, gu)
    g_wd = _dw_call("dw_ffn_down", act, dout_b, tr=D_FF // 2)
    g_wffn_t = _dw_call("dw_ffn_in", dgu, hn, tr=2 * D_FF // 4)
    tok = grads_ready(dict(w_ffn_down=g_wd, w_ffn_in=g_wffn_t))
    dh, dh_b, g_gffn = _ffn_in_bwd_call(dgu, w_ffn_in, h, row(g_ffn) + tok, dout)
    dyc, dya, du_g, dcp, do = _merge_bwd_call(dh_b, wm, wc, wa, u, yc, ya)
    g_wm = _dw_call("dw_merge", z, dh_b, tr=D_MODEL)
    g_wc = _dw_call("dw_conv_out", c, dyc, tr=D_MODEL)
    g_wa = _dw_call("dw_attn_out", o, dya, tr=D_MODEL)
    tok = grads_ready(dict(w_merge_out=g_wm, w_conv_out=g_wc, w_attn_out=g_wa))
    du_conv, g_cw8, g_cb, g_lg, g_lb = _conv_bwd_call(dcp, hc, u, conv_w, row(ln_g) + tok, row(ln_b), B, S)
    du_q, du_kv, g_qg2, g_kg2, g_sk = _attn_bwd_call(u, do, qg2, kg2, sinks, slopes, B, S)
    g_win_t = _concat_rows_call("dw_in_rows", [
        _dw_call("dw_in_conv", du_conv, xn, tr=D_MODEL),
        _dw_call("dw_in_q", du_q, xn, tr=D_MODEL),
        _dw_call("dw_in_kv", du_kv, xn, tr=2 * KV_W),
        _dw_call("dw_in_gates", du_g, xn, tr=D_MODEL)])
    tok = grads_ready(dict(w_in=g_win_t))
    grad_x, g_gmix = _inproj_bwd_call(du_conv, du_g, du_q, du_kv, w_in_i, x, row(g_mix) + tok, dh)

    g_cw = g_cw8.reshape(CONV_WIDTH, 8, C_CONV).sum(axis=1)
    g_qg = g_qg2[0, :HEAD_DIM] + g_qg2[0, HEAD_DIM:]
    g_kg = g_kg2[0, :HEAD_DIM] + g_kg2[0, HEAD_DIM:]
    small = dict(norm_mix_g=g_gmix[0], conv_dw_b=g_cb[0], conv_ln_g=g_lg[0], conv_ln_b=g_lb[0], q_norm_g=g_qg,
                 k_norm_g=g_kg, sinks=g_sk[0, :N_Q_HEADS], norm_ffn_g=g_gffn[0], conv_dw_w=g_cw)
    return sse[0, 0], grad_x, small


def _to_internal_rows(wt):
    o1 = 2 * C_CONV
    o4 = o1 + Q_W + 2 * KV_W
    return jnp.concatenate([wt[:o1], wt[o4:], wt[o1:o4]], axis=0)


_COL_SHARDED = ("w_in", "w_ffn_in")

_SMALL = (("norm_mix_g", 8), ("conv_dw_b", 8), ("conv_ln_g", 8), ("conv_ln_b", 8), ("norm_ffn_g", 8),
          ("q_norm_g", 1), ("k_norm_g", 1), ("sinks", 1))
_SMALL_ROWS = 48


def _pack_small(d):
    rows = [_pad_rows(d[name], r) for name, r in _SMALL]
    used = sum(r for _, r in _SMALL)
    return jnp.concatenate(rows + [jnp.zeros((_SMALL_ROWS - used, 128), F32)], axis=0)


def _unpack_small(packed, like):
    out, r0 = {}, 0
    for name, r in _SMALL:
        n = like[name].size
        out[name] = packed[r0:r0 + r].reshape(-1)[:n].reshape(like[name].shape)
        r0 += r
    return out


def kernel(x, norm_mix_g, w_in, conv_dw_w, conv_dw_b, conv_ln_g, conv_ln_b, w_conv_out, q_norm_g, k_norm_g, sinks, w_attn_out, w_merge_out, norm_ffn_g, w_ffn_in, w_ffn_down, loss_target, m_norm_mix_g, m_w_in, m_conv_dw_w, m_conv_dw_b, m_conv_ln_g, m_conv_ln_b, m_w_conv_out, m_q_norm_g, m_k_norm_g, m_sinks, m_w_attn_out, m_w_merge_out, m_norm_ffn_g, m_w_ffn_in, m_w_ffn_down, v_norm_mix_g, v_w_in, v_conv_dw_w, v_conv_dw_b, v_conv_ln_g, v_conv_ln_b, v_w_conv_out, v_q_norm_g, v_k_norm_g, v_sinks, v_w_attn_out, v_w_merge_out, v_norm_ffn_g, v_w_ffn_in, v_w_ffn_down):
    names = ["norm_mix_g", "w_in", "conv_dw_w", "conv_dw_b", "conv_ln_g", "conv_ln_b", "w_conv_out", "q_norm_g",
             "k_norm_g", "sinks", "w_attn_out", "w_merge_out", "norm_ffn_g", "w_ffn_in", "w_ffn_down"]
    w = dict(zip(names, (norm_mix_g, w_in, conv_dw_w, conv_dw_b, conv_ln_g, conv_ln_b, w_conv_out, q_norm_g, k_norm_g,
                         sinks, w_attn_out, w_merge_out, norm_ffn_g, w_ffn_in, w_ffn_down)))
    m = dict(zip(names, (m_norm_mix_g, m_w_in, m_conv_dw_w, m_conv_dw_b, m_conv_ln_g, m_conv_ln_b, m_w_conv_out,
                         m_q_norm_g, m_k_norm_g, m_sinks, m_w_attn_out, m_w_merge_out, m_norm_ffn_g, m_w_ffn_in,
                         m_w_ffn_down)))
    v = dict(zip(names, (v_norm_mix_g, v_w_in, v_conv_dw_w, v_conv_dw_b, v_conv_ln_g, v_conv_ln_b, v_w_conv_out,
                         v_q_norm_g, v_k_norm_g, v_sinks, v_w_attn_out, v_w_merge_out, v_norm_ffn_g, v_w_ffn_in,
                         v_w_ffn_down)))
    B, S, _ = x.shape
    T = B * S
    me = 4 * lax.axis_index("x") + 2 * lax.axis_index("y") + lax.axis_index("c")

    as_rows = lambda k, t: t.T if k in _COL_SHARDED else t
    cw_shard = _pad_rows(conv_dw_w.reshape(CONV_WIDTH, 128), 32)
    g_win, g_cwt = _gather_weights_call([w_in.astype(BF16), cw_shard])
    conv_w_full = g_cwt.reshape(N_DEV, 32 * 128)[:, :CONV_WIDTH * 128].reshape(N_DEV, CONV_WIDTH, 128)
    conv_w_full = conv_w_full.transpose(1, 0, 2).reshape(CONV_WIDTH, C_CONV)
    w_in_i = _to_internal_rows(g_win.transpose(0, 2, 1).reshape(IN_COLS, D_MODEL))
    rest_names = ["w_conv_out", "w_attn_out", "w_merge_out", "w_ffn_in", "w_ffn_down"]
    rest_shards = [as_rows(k, w[k]).astype(BF16) for k in rest_names]
    rest_started = _exchange_start_call("gather_rest_start", rest_shards, scatter=False)

    def rest_weights(after):
        lands = _exchange_wait_call("gather_rest_wait", rest_started, False, after)
        mine = lax.broadcasted_iota(jnp.int32, (N_DEV, 1, 1), 0) == me
        full = [jnp.where(mine, s[None], l) for l, s in zip(lands, rest_shards)]
        return tuple(f.reshape(N_DEV * f.shape[1], D_MODEL) for f in full)

    exchanges = []

    def grads_ready(group):
        ks = list(group)
        blocks = [group[k].reshape(N_DEV, group[k].shape[0] // N_DEV, D_MODEL) for k in ks]
        started = _exchange_start_call("scatter_" + ks[0] + "_start", blocks, scatter=True)
        exchanges.append((ks, started, [lax.dynamic_index_in_dim(b, me, 0, keepdims=False) for b in blocks]))
        return started[-1][0:1, 0:1]

    sse, grad_x, small = _local_step(
        x.reshape(T, D_MODEL), loss_target.reshape(T, D_MODEL), w_in_i, conv_w_full, conv_dw_b, conv_ln_g, conv_ln_b,
        q_norm_g, k_norm_g, sinks, norm_mix_g, norm_ffn_g, rest_started[-1][0:1, 0:1], rest_weights, grads_ready, B, S)

    grad, delta, new_m, new_v = {}, {}, {}, {}
    packed = jnp.concatenate([_pack_small(small), jnp.full((8, 128), sse, F32)], axis=0)
    cw_blocks = small["conv_dw_w"].reshape(CONV_WIDTH, N_DEV, 128).transpose(1, 0, 2)
    tot, cw_tot = _allsum_small_call(packed, jnp.pad(cw_blocks, ((0, 0), (0, 32 - CONV_WIDTH), (0, 0))))
    loss = 0.5 / D_MODEL * tot[_SMALL_ROWS, 0]
    g_small = _unpack_small(tot[:_SMALL_ROWS], w)
    d_s, m_s, v_s = _adamw_small_call("adamw_small", tot[:_SMALL_ROWS], _pack_small(w), _pack_small(m), _pack_small(v))
    d_small, m_small, v_small = _unpack_small(d_s, w), _unpack_small(m_s, w), _unpack_small(v_s, w)
    g_cw = cw_tot[:CONV_WIDTH]
    cw2 = lambda t: t.reshape(CONV_WIDTH, 128)
    d_cw, m_cw, v_cw = _adamw_small_call("adamw_conv_w", g_cw, cw2(conv_dw_w), cw2(m_conv_dw_w), cw2(v_conv_dw_w))
    cw3 = lambda t: t.reshape(CONV_WIDTH, 1, 128)
    for k, _ in _SMALL:
        grad[k], delta[k], new_m[k], new_v[k] = g_small[k], d_small[k], m_small[k], v_small[k]
    grad["conv_dw_w"], delta["conv_dw_w"], new_m["conv_dw_w"], new_v["conv_dw_w"] = cw3(g_cw), cw3(d_cw), cw3(m_cw), cw3(v_cw)

    after = d_cw
    for ks, started, owns in exchanges:
        lands = _exchange_wait_call("scatter_" + ks[0] + "_wait", started, True, after)
        for k, land, own in zip(ks, lands, owns):
            tr = {672: 336, 704: 352, 352: 176, 128: 128}[own.shape[0]]
            outs = _adamw_sum_call("adamw_" + k, land, own, as_rows(k, w[k]), as_rows(k, m[k]), as_rows(k, v[k]), tr)
            grad[k], delta[k], new_m[k], new_v[k] = (as_rows(k, t) for t in outs)
            after = outs[3]

    return (loss, grad_x.reshape(B, S, D_MODEL), *[grad[k] for k in names], *[delta[k] for k in names],
            *[new_m[k] for k in names], *[new_v[k] for k in names])
```

```python
import functools
import math

import jax
import jax.numpy as jnp
from jax import lax
from jax.experimental import pallas as pl
from jax.experimental.pallas import tpu as pltpu

F32 = jnp.float32
BF16 = jnp.bfloat16
MESH = pl.DeviceIdType.MESH

N_DEV = 8
D_MODEL = 1024
C_CONV = 1024
CONV_WIDTH = 31
HEAD_DIM = 64
N_Q_HEADS = 16
N_KV_HEADS = 2
GROUP = N_Q_HEADS // N_KV_HEADS
WINDOW = 128
BLOCK = 128
D_FF = 2816
EPS = 1e-6
NEG = -1e30
ATTN_SCALE = 1.0 / math.sqrt(HEAD_DIM)
Q_W = N_Q_HEADS * HEAD_DIM
KV_W = N_KV_HEADS * HEAD_DIM
IN_COLS = 2 * C_CONV + Q_W + 2 * KV_W + 2 * D_MODEL

ADAM_LR = 0.001
ADAM_B1 = 0.9
ADAM_B2 = 0.999
ADAM_EPS = 1e-08
ADAM_WD = 0.01
ADAM_STEP = 10

COL_A, COL_GATE, COL_GC, COL_GA, COL_Q, COL_K, COL_V = 0, 1024, 2048, 3072, 4096, 5120, 5248
REF_CONV = slice(0, 2 * C_CONV)
REF_QKV = slice(2 * C_CONV, 2 * C_CONV + Q_W + 2 * KV_W)
REF_GATES = slice(2 * C_CONV + Q_W + 2 * KV_W, IN_COLS)

HALO = 32
VMEM_LIMIT = 56 * 1024 * 1024


def _cp(*sem):
    return pltpu.CompilerParams(dimension_semantics=sem, vmem_limit_bytes=VMEM_LIMIT)


def _sigmoid(x):
    return jax.nn.sigmoid(x)


def _dot(a, b):
    return jnp.dot(a, b, preferred_element_type=F32)


def _dot_nt(a, b):
    return lax.dot_general(a, b, (((1,), (1,)), ((), ())), preferred_element_type=F32)


def _dot_tn(a, b):
    return lax.dot_general(a, b, (((0,), (0,)), ((), ())), preferred_element_type=F32)


def _full(shape):
    return pl.BlockSpec(shape, lambda *_: (0,) * len(shape))


def _rows(tm, ncols, colblk=0):
    return pl.BlockSpec((tm, ncols), lambda i: (i, colblk))


def _inproj_call(x, g, w, tm=512):
    T = x.shape[0]
    N = w.shape[0]

    def body(x_ref, g_ref, w_ref, u_ref, xn_ref):
        xv = x_ref[...]
        r = lax.rsqrt(jnp.mean(xv * xv, axis=-1, keepdims=True) + EPS)
        xn = (xv * r * g_ref[...]).astype(BF16)
        xn_ref[...] = xn
        u_ref[:, COL_A:COL_GC] = _dot_nt(xn, w_ref[REF_CONV, :]).astype(BF16)
        u_ref[:, COL_GC:COL_Q] = _dot_nt(xn, w_ref[REF_GATES, :]).astype(BF16)
        u_ref[:, COL_Q:] = _dot_nt(xn, w_ref[REF_QKV, :]).astype(BF16)

    return pl.pallas_call(
        body, name="inproj", grid=(T // tm,),
        in_specs=[_rows(tm, D_MODEL), _full((1, D_MODEL)), _full(w.shape)],
        out_specs=[_rows(tm, N), _rows(tm, D_MODEL)],
        out_shape=[jax.ShapeDtypeStruct((T, N), BF16), jax.ShapeDtypeStruct((T, D_MODEL), BF16)],
        compiler_params=_cp("parallel"),
    )(x, g, w)


def _fill_shifted(src_ref, sh_ref):
    n = src_ref.shape[0] - 8
    for s in range(1, 8):
        sh_ref[s - 1, 0:n, :] = src_ref[s:s + n, :]


def _window(src_ref, sh_ref, off, cols):
    s = off % 8
    if s == 0:
        return src_ref[off:off + 128, cols]
    return sh_ref[s - 1, off - s:off - s + 128, cols]


def _conv_taps(w_ref, src_ref, sh_ref, base, rows, cb, reverse):
    cols = pl.ds(pl.multiple_of(cb * 128, 128), 128)
    outs = []
    for rb in range(rows // 128):
        acc = jnp.zeros((128, 128), F32)
        for j in range(CONV_WIDTH):
            off = base + (CONV_WIDTH - 1 - j if reverse else j) + rb * 128
            acc = acc + w_ref[j:j + 1, cols] * _window(src_ref, sh_ref, off, cols)
        outs.append(acc)
    return outs


def _conv_fwd_call(u, conv_w, conv_b, ln_g, ln_b, B, S, ts=512):
    T = B * S
    nS = S // ts
    per32 = ts // HALO

    def body(a_ref, gate_ref, ap_ref, gp_ref, w_ref, b_ref, g_ref, bb_ref, hc_ref, c_ref, hext, hsh, acc_ref):
        s = pl.program_id(1)
        prev = ap_ref[...].astype(F32) * _sigmoid(gp_ref[...].astype(F32))
        hext[0:HALO, :] = jnp.where(s > 0, prev, 0.0)
        hext[HALO:, :] = a_ref[...].astype(F32) * _sigmoid(gate_ref[...].astype(F32))
        _fill_shifted(hext, hsh)

        def colblock(cb, carry):
            cols = pl.ds(pl.multiple_of(cb * 128, 128), 128)
            outs = _conv_taps(w_ref, hext, hsh, HALO - (CONV_WIDTH - 1), ts, cb, reverse=False)
            for rb, acc in enumerate(outs):
                acc_ref[rb * 128:(rb + 1) * 128, cols] = acc
            return carry

        lax.fori_loop(0, C_CONV // 128, colblock, 0)
        hc = acc_ref[...] + b_ref[...]
        hc_ref[...] = hc
        mu = jnp.mean(hc, axis=-1, keepdims=True)
        xc = hc - mu
        var = jnp.mean(xc * xc, axis=-1, keepdims=True)
        y = xc * lax.rsqrt(var + EPS) * g_ref[...] + bb_ref[...]
        c_ref[...] = (y * _sigmoid(y)).astype(BF16)

    cur = lambda cb: pl.BlockSpec((ts, C_CONV), lambda b, s: (b * nS + s, cb))
    prv = lambda cb: pl.BlockSpec((HALO, C_CONV), lambda b, s: (jnp.maximum(b * (S // HALO) + s * per32 - 1, 0), cb))
    vec = _full((1, C_CONV))
    return pl.pallas_call(
        body, name="conv_fwd", grid=(B, nS),
        in_specs=[cur(0), cur(1), prv(0), prv(1), _full((CONV_WIDTH, C_CONV)), vec, vec, vec],
        out_specs=[pl.BlockSpec((ts, C_CONV), lambda b, s: (b * nS + s, 0))] * 2,
        out_shape=[jax.ShapeDtypeStruct((T, C_CONV), F32), jax.ShapeDtypeStruct((T, C_CONV), BF16)],
        scratch_shapes=[pltpu.VMEM((HALO + ts, C_CONV), F32), pltpu.VMEM((7, HALO + ts, C_CONV), F32),
                        pltpu.VMEM((ts, C_CONV), F32)],
        compiler_params=_cp("parallel", "arbitrary"),
    )(u, u, u, u, conv_w, conv_b, ln_g, ln_b)


def _lane_lo():
    return lax.broadcasted_iota(jnp.int32, (1, 128), 1) < HEAD_DIM


def _half_sums(t, lo):
    s_lo = jnp.sum(jnp.where(lo, t, 0.0), axis=-1, keepdims=True)
    s_hi = jnp.sum(jnp.where(lo, 0.0, t), axis=-1, keepdims=True)
    return jnp.where(lo, s_lo, s_hi)


def _head_rstd(t, lo):
    return lax.rsqrt(_half_sums(t * t, lo) * (1.0 / HEAD_DIM) + EPS)


def _fill_bias(btab, kvh, slopes_ref):
    sj = lax.broadcasted_iota(jnp.int32, (2 * BLOCK, BLOCK), 0)
    qi = lax.broadcasted_iota(jnp.int32, (2 * BLOCK, BLOCK), 1)
    dist = qi + BLOCK - sj
    valid = (dist >= 0) & (dist < WINDOW)
    distf = dist.astype(F32)
    for g in range(GROUP):
        bias = jnp.where(valid, -slopes_ref[kvh * GROUP + g] * distf, NEG)
        btab[1, g] = bias
        btab[0, g] = jnp.where(sj >= BLOCK, bias, NEG)


def _softmax_cols(st_ref, btab, p, var, g, hf, sk):
    sh = st_ref[2 * BLOCK * hf:2 * BLOCK * (hf + 1), 128 * p:128 * (p + 1)] + btab[var, g]
    m = jnp.maximum(jnp.max(sh, axis=0, keepdims=True), sk)
    e = jnp.exp(sh - m)
    esk = jnp.exp(sk - m)
    rz = 1.0 / (jnp.sum(e, axis=0, keepdims=True) + esk)
    return e * rz, esk, rz


def _stage_kv(kvh, k_ref, v_ref, kg_ref, lo, kA, kB, vA, vB):
    kv = k_ref[...].astype(F32)
    rk = _head_rstd(kv, lo)
    kn = kv * rk * kg_ref[...]
    mine = lax.broadcasted_iota(jnp.int32, (1, 128), 1) // HEAD_DIM == kvh
    ksel = jnp.where(mine, kn, 0.0)
    vsel = jnp.where(mine, v_ref[...].astype(F32), 0.0)
    krol = pltpu.roll(ksel, HEAD_DIM, 1)
    vrol = pltpu.roll(vsel, HEAD_DIM, 1)
    first = kvh == 0
    zeros = jnp.zeros((BLOCK, 128), BF16)
    for ref, val in ((kA, jnp.where(first, ksel, krol)), (kB, jnp.where(first, krol, ksel)),
                     (vA, jnp.where(first, vsel, vrol)), (vB, jnp.where(first, vrol, vsel))):
        ref[0:BLOCK, :] = zeros
        ref[BLOCK:, :] = val.astype(BF16)


def _attn_fwd_call(u, qg2, kg2, sinks, slopes, B, S):
    T = B * S
    nb = S // BLOCK
    half = Q_W // 2
    npair = GROUP // 2
    per_trip = 8 if nb % 8 == 0 else 4

    def body(sinks_ref, slopes_ref, q_ref, k_ref, v_ref, qg_ref, kg_ref, o_ref, kA, kB, vA, vB, btab, qn_buf, s_buf, p_buf):
        kvh = pl.program_id(1)
        lo = _lane_lo()
        _stage_kv(kvh, k_ref, v_ref, kg_ref, lo, kA, kB, vA, vB)
        _fill_bias(btab, kvh, slopes_ref)
        qgs = qg_ref[...] * ATTN_SCALE

        def block(n, slot):
            qn_scr, s_scr, p_scr = qn_buf.at[slot], s_buf.at[slot], p_buf.at[slot]
            r0 = pl.multiple_of(n * BLOCK, BLOCK)
            win = pl.ds(r0, 2 * BLOCK)
            k2t = jnp.concatenate([kA[win, :], kB[win, :]], axis=0)
            v2 = jnp.concatenate([vA[win, :], vB[win, :]], axis=0)
            var = jnp.minimum(n, 1)
            for p in range(npair):
                pc = slice(128 * p, 128 * (p + 1))
                qp = q_ref[pl.ds(r0, BLOCK), pc].astype(F32)
                qn_scr[pc, :] = (qp * _head_rstd(qp, lo) * qgs).astype(BF16)
            s_scr[...] = _dot_nt(k2t, qn_scr[...])
            for p in range(npair):
                for hf in range(2):
                    g = 2 * p + hf
                    pn, _, _ = _softmax_cols(s_scr, btab, p, var, g, hf, sinks_ref[kvh * GROUP + g])
                    p_scr[2 * BLOCK * hf:2 * BLOCK * (hf + 1), 128 * p:128 * (p + 1)] = pn.astype(BF16)
            o_all = _dot_tn(v2, p_scr[...]).T
            for p in range(npair):
                pc = slice(128 * p, 128 * (p + 1))
                o_ref[pl.ds(r0, BLOCK), pc] = o_all[pc, :].astype(BF16)

        def trip(i, carry):
            for slot in range(per_trip):
                block(per_trip * i + slot, slot)
            return carry

        lax.fori_loop(0, nb // per_trip, trip, 0)

    smem = pl.BlockSpec(memory_space=pltpu.SMEM)
    kvbuf = pltpu.VMEM((S + BLOCK, 128), BF16)
    pair_scores = (per_trip, npair * BLOCK, 4 * BLOCK)
    return pl.pallas_call(
        body, name="attn_fwd", grid=(B, N_KV_HEADS),
        in_specs=[smem, smem,
                  pl.BlockSpec((S, half), lambda b, h: (b, COL_Q // half + h)),
                  pl.BlockSpec((S, 128), lambda b, h: (b, COL_K // 128)),
                  pl.BlockSpec((S, 128), lambda b, h: (b, COL_V // 128)),
                  _full((1, 128)), _full((1, 128))],
        out_specs=pl.BlockSpec((S, half), lambda b, h: (b, h)),
        out_shape=jax.ShapeDtypeStruct((T, Q_W), BF16),
        scratch_shapes=[kvbuf, kvbuf, kvbuf, kvbuf, pltpu.VMEM((2, GROUP, 2 * BLOCK, BLOCK), F32),
                        pltpu.VMEM((per_trip, npair * BLOCK, 128), BF16), pltpu.VMEM(pair_scores, F32),
                        pltpu.VMEM(pair_scores, BF16)],
        compiler_params=_cp("parallel", "arbitrary"),
    )(sinks, slopes, u, u, u, qg2, kg2)


def _merge_call(c, o, u, x, wc, wa, wm, g_ffn, tm=512):
    T = x.shape[0]

    def body(c_ref, o_ref, gc_ref, ga_ref, x_ref, wc_ref, wa_ref, wm_ref, g_ref, yc_ref, ya_ref, z_ref, h_ref, hn_ref):
        yc = _dot(c_ref[...], wc_ref[...])
        ya = _dot(o_ref[...], wa_ref[...])
        yc_ref[...] = yc.astype(BF16)
        ya_ref[...] = ya.astype(BF16)
        z = (_sigmoid(gc_ref[...].astype(F32)) * yc + _sigmoid(ga_ref[...].astype(F32)) * ya).astype(BF16)
        z_ref[...] = z
        h = x_ref[...] + _dot(z, wm_ref[...])
        h_ref[...] = h
        r = lax.rsqrt(jnp.mean(h * h, axis=-1, keepdims=True) + EPS)
        hn_ref[...] = (h * r * g_ref[...]).astype(BF16)

    sq = _full((D_MODEL, D_MODEL))
    row = _rows(tm, D_MODEL)
    f32 = jax.ShapeDtypeStruct((T, D_MODEL), F32)
    b16 = jax.ShapeDtypeStruct((T, D_MODEL), BF16)
    return pl.pallas_call(
        body, name="merge_fwd", grid=(T // tm,),
        in_specs=[row, row, _rows(tm, D_MODEL, COL_GC // D_MODEL), _rows(tm, D_MODEL, COL_GA // D_MODEL), row,
                  sq, sq, sq, _full((1, D_MODEL))],
        out_specs=[row] * 5,
        out_shape=[b16, b16, b16, f32, b16],
        compiler_params=_cp("parallel"),
    )(c, o, u, u, x, wc, wa, wm, g_ffn)


def _ffn_in_call(hn, w, tm=512):
    T = hn.shape[0]

    def body(hn_ref, w_ref, fac_ref, act_ref):
        gu = _dot_nt(hn_ref[...], w_ref[...])
        g = gu[:, :D_FF]
        up = gu[:, D_FF:]
        sg = _sigmoid(g)
        silu = g * sg
        act_ref[...] = (silu * up).astype(BF16)
        fac_ref[:, :D_FF] = (up * (sg * (1.0 + g * (1.0 - sg)))).astype(BF16)
        fac_ref[:, D_FF:] = silu.astype(BF16)

    return pl.pallas_call(
        body, name="ffn_in", grid=(T // tm,),
        in_specs=[_rows(tm, D_MODEL), _full(w.shape)],
        out_specs=[_rows(tm, 2 * D_FF), _rows(tm, D_FF)],
        out_shape=[jax.ShapeDtypeStruct((T, 2 * D_FF), BF16), jax.ShapeDtypeStruct((T, D_FF), BF16)],
        compiler_params=_cp("parallel"),
    )(hn, w)


def _ffn_out_call(act, wd, h, tgt, tm=512):
    T = h.shape[0]

    def body(act_ref, wd_ref, h_ref, t_ref, sse_ref, d_ref, db_ref):
        e = h_ref[...] + _dot(act_ref[...], wd_ref[...]) - t_ref[...]

        @pl.when(pl.program_id(0) == 0)
        def _():
            sse_ref[...] = jnp.zeros_like(sse_ref)

        sse_ref[...] += jnp.sum(e * e)
        d = e * (1.0 / D_MODEL)
        d_ref[...] = d
        db_ref[...] = d.astype(BF16)

    row = _rows(tm, D_MODEL)
    return pl.pallas_call(
        body, name="ffn_out", grid=(T // tm,),
        in_specs=[_rows(tm, D_FF), _full(wd.shape), row, row],
        out_specs=[_full((8, 128)), row, row],
        out_shape=[jax.ShapeDtypeStruct((8, 128), F32), jax.ShapeDtypeStruct((T, D_MODEL), F32),
                   jax.ShapeDtypeStruct((T, D_MODEL), BF16)],
        compiler_params=_cp("arbitrary"),
    )(act, wd, h, tgt)


def _dw_call(name, a, b, tr, tk=2048):
    T, K = a.shape
    N = b.shape[1]
    tk = min(tk, T)
    nk = T // tk

    def body(a_ref, b_ref, o_ref, acc_ref):
        k = pl.program_id(1)

        @pl.when(k == 0)
        def _():
            acc_ref[...] = jnp.zeros_like(acc_ref)

        acc_ref[...] += _dot_tn(a_ref[...], b_ref[...])

        @pl.when(k == nk - 1)
        def _():
            o_ref[...] = acc_ref[...].astype(BF16)

    return pl.pallas_call(
        body, name=name, grid=(K // tr, nk),
        in_specs=[pl.BlockSpec((tk, tr), lambda j, k: (k, j)), pl.BlockSpec((tk, N), lambda j, k: (k, 0))],
        out_specs=pl.BlockSpec((tr, N), lambda j, k: (j, 0)),
        out_shape=jax.ShapeDtypeStruct((K, N), BF16),
        scratch_shapes=[pltpu.VMEM((tr, N), F32)],
        compiler_params=_cp("parallel", "arbitrary"),
    )(a, b)


def _concat_rows_call(name, pieces, tr=256):
    N = pieces[0].shape[1]
    starts = [0]
    for p in pieces:
        starts.append(starts[-1] + p.shape[0] // tr)

    def body(*refs):
        o_ref = refs[-1]
        i = pl.program_id(0)
        for p, ref in enumerate(refs[:-1]):
            @pl.when((i >= starts[p]) & (i < starts[p + 1]))
            def _(ref=ref):
                o_ref[...] = ref[...]

    def spec(p):
        lo, n = starts[p], pieces[p].shape[0] // tr
        return pl.BlockSpec((tr, N), lambda i: (jnp.clip(i - lo, 0, n - 1), 0))

    return pl.pallas_call(
        body, name=name, grid=(starts[-1],),
        in_specs=[spec(p) for p in range(len(pieces))], out_specs=pl.BlockSpec((tr, N), lambda i: (i, 0)),
        out_shape=jax.ShapeDtypeStruct((starts[-1] * tr, N), pieces[0].dtype),
        compiler_params=_cp("arbitrary"),
    )(*pieces)


def _rms_bwd(dy, xv, g):
    r = lax.rsqrt(jnp.mean(xv * xv, axis=-1, keepdims=True) + EPS)
    t = dy * g
    dx = r * t - xv * (r * r * r * jnp.mean(t * xv, axis=-1, keepdims=True))
    return dx, dy * xv * r


def _ffn_down_bwd_call(dout_b, wd, gu, tm=512):
    T = dout_b.shape[0]

    def body(d_ref, wd_ref, fg_ref, fu_ref, o_ref):
        dact = _dot_nt(d_ref[...], wd_ref[...])
        o_ref[:, :D_FF] = (dact * fg_ref[...].astype(F32)).astype(BF16)
        o_ref[:, D_FF:] = (dact * fu_ref[...].astype(F32)).astype(BF16)

    return pl.pallas_call(
        body, name="ffn_down_bwd", grid=(T // tm,),
        in_specs=[_rows(tm, D_MODEL), _full(wd.shape), _rows(tm, D_FF, 0), _rows(tm, D_FF, 1)],
        out_specs=_rows(tm, 2 * D_FF),
        out_shape=jax.ShapeDtypeStruct((T, 2 * D_FF), BF16),
        compiler_params=_cp("parallel"),
    )(dout_b, wd, gu, gu)


def _ffn_in_bwd_call(dgu, w, h, g_ffn, dout, tm=512):
    T = h.shape[0]

    def body(dgu_ref, w_ref, h_ref, g_ref, d_ref, dh_ref, dhb_ref, dg_ref):
        dhn = _dot(dgu_ref[...], w_ref[...])
        dx, dgt = _rms_bwd(dhn, h_ref[...], g_ref[...])
        dh = d_ref[...] + dx
        dh_ref[...] = dh
        dhb_ref[...] = dh.astype(BF16)

        @pl.when(pl.program_id(0) == 0)
        def _():
            dg_ref[...] = jnp.zeros_like(dg_ref)

        dg_ref[...] += jnp.sum(dgt, axis=0, keepdims=True)

    row = _rows(tm, D_MODEL)
    return pl.pallas_call(
        body, name="ffn_in_bwd", grid=(T // tm,),
        in_specs=[_rows(tm, 2 * D_FF), _full(w.shape), row, _full((1, D_MODEL)), row],
        out_specs=[row, row, _full((1, D_MODEL))],
        out_shape=[jax.ShapeDtypeStruct((T, D_MODEL), F32), jax.ShapeDtypeStruct((T, D_MODEL), BF16),
                   jax.ShapeDtypeStruct((1, D_MODEL), F32)],
        compiler_params=_cp("arbitrary"),
    )(dgu, w, h, g_ffn, dout)


def _merge_bwd_call(dh_b, wm, wc, wa, u, yc, ya, tm=512):
    T = dh_b.shape[0]

    def body(dh_ref, wm_ref, wc_ref, wa_ref, gc_ref, ga_ref, yc_ref, ya_ref, dyc_ref, dya_ref, dug_ref, dcp_ref, do_ref):
        dz = _dot_nt(dh_ref[...], wm_ref[...])
        sgc = _sigmoid(gc_ref[...].astype(F32))
        sga = _sigmoid(ga_ref[...].astype(F32))
        dyc = (dz * sgc).astype(BF16)
        dya = (dz * sga).astype(BF16)
        dyc_ref[...] = dyc
        dya_ref[...] = dya
        dug_ref[:, :D_MODEL] = (dz * yc_ref[...].astype(F32) * (sgc * (1.0 - sgc))).astype(BF16)
        dug_ref[:, D_MODEL:] = (dz * ya_ref[...].astype(F32) * (sga * (1.0 - sga))).astype(BF16)
        dcp_ref[...] = _dot_nt(dyc, wc_ref[...])
        do_ref[...] = _dot_nt(dya, wa_ref[...]).astype(BF16)

    sq = _full((D_MODEL, D_MODEL))
    row = _rows(tm, D_MODEL)
    b16 = jax.ShapeDtypeStruct((T, D_MODEL), BF16)
    return pl.pallas_call(
        body, name="merge_bwd", grid=(T // tm,),
        in_specs=[row, sq, sq, sq, _rows(tm, D_MODEL, COL_GC // D_MODEL), _rows(tm, D_MODEL, COL_GA // D_MODEL), row, row],
        out_specs=[row, row, _rows(tm, 2 * D_MODEL), row, row],
        out_shape=[b16, b16, jax.ShapeDtypeStruct((T, 2 * D_MODEL), BF16), jax.ShapeDtypeStruct((T, D_MODEL), F32), b16],
        compiler_params=_cp("parallel"),
    )(dh_b, wm, wc, wa, u, u, yc, ya)


def _conv_bwd_call(dcp, hc, u, conv_w, ln_g, ln_b, B, S, ts=256):
    T = B * S
    nS = S // ts
    per32 = ts // HALO
    last32 = T // HALO - 1

    def ln_bwd(dcp_v, hc_v, g, bb):
        mu = jnp.mean(hc_v, axis=-1, keepdims=True)
        xc = hc_v - mu
        rstd = lax.rsqrt(jnp.mean(xc * xc, axis=-1, keepdims=True) + EPS)
        xhat = xc * rstd
        y = xhat * g + bb
        sy = _sigmoid(y)
        dy = dcp_v * (sy * (1.0 + y * (1.0 - sy)))
        dxh = dy * g
        dhc = rstd * (dxh - jnp.mean(dxh, axis=-1, keepdims=True) - xhat * jnp.mean(dxh * xhat, axis=-1, keepdims=True))
        return dhc, dy * xhat, dy

    def body(dcp_ref, dcpn_ref, hc_ref, hcn_ref, a_ref, gate_ref, w_ref, g_ref, bb_ref,
             du_ref, dw_ref, db_ref, dlg_ref, dlb_ref, dhext, glu_scr, dhsh, acc_ref):
        b = pl.program_id(0)
        s = pl.program_id(1)

        @pl.when((b == 0) & (s == 0))
        def _():
            dw_ref[...] = jnp.zeros_like(dw_ref)
            db_ref[...] = jnp.zeros_like(db_ref)
            dlg_ref[...] = jnp.zeros_like(dlg_ref)
            dlb_ref[...] = jnp.zeros_like(dlb_ref)

        dhc, dgt, dbt = ln_bwd(dcp_ref[...], hc_ref[...], g_ref[...], bb_ref[...])
        dhn, _, _ = ln_bwd(dcpn_ref[...], hcn_ref[...], g_ref[...], bb_ref[...])
        dhext[0:ts, :] = dhc
        dhext[ts:, :] = jnp.where(s < nS - 1, dhn, 0.0)
        db_ref[...] += jnp.sum(dhc, axis=0, keepdims=True)
        dlg_ref[...] += jnp.sum(dgt, axis=0, keepdims=True)
        dlb_ref[...] += jnp.sum(dbt, axis=0, keepdims=True)

        sg = _sigmoid(gate_ref[...].astype(F32))
        av = a_ref[...].astype(F32)
        glu_scr[...] = av * sg
        _fill_shifted(dhext, dhsh)

        def colblock(cb, carry):
            cols = pl.ds(pl.multiple_of(cb * 128, 128), 128)
            outs = _conv_taps(w_ref, dhext, dhsh, 0, ts, cb, reverse=True)
            for rb, acc in enumerate(outs):
                acc_ref[rb * 128:(rb + 1) * 128, cols] = acc
            taps = [jnp.zeros((8, 128), F32) for _ in range(CONV_WIDTH)]
            for r0 in range(0, ts, 128):
                glu = [glu_scr[r0 + 8 * i:r0 + 8 * (i + 1), cols] for i in range(16)]
                for s8 in range(8):
                    users = {}
                    for j in range(CONV_WIDTH):
                        if (CONV_WIDTH - 1 - j) % 8 == s8:
                            for i in range(16):
                                users.setdefault(CONV_WIDTH - 1 - j + r0 + 8 * i - s8, []).append((j, i))
                    for row, pairs in sorted(users.items()):
                        dhc8 = dhext[row:row + 8, cols] if s8 == 0 else dhsh[s8 - 1, row:row + 8, cols]
                        for j, i in pairs:
                            taps[j] = taps[j] + glu[i] * dhc8
            for j in range(CONV_WIDTH):
                dw_ref[8 * j:8 * (j + 1), cols] += taps[j]
            return carry

        lax.fori_loop(0, C_CONV // 128, colblock, 0)
        dglu = acc_ref[...]
        du_ref[:, :C_CONV] = (dglu * sg).astype(BF16)
        du_ref[:, C_CONV:] = (dglu * av * (sg * (1.0 - sg))).astype(BF16)

    cur = lambda cb: pl.BlockSpec((ts, C_CONV), lambda b, s: (b * nS + s, cb))
    nxt = pl.BlockSpec((HALO, C_CONV), lambda b, s: (jnp.minimum(b * (S // HALO) + (s + 1) * per32, last32), 0))
    vec = _full((1, C_CONV))
    return pl.pallas_call(
        body, name="conv_bwd", grid=(B, nS),
        in_specs=[cur(0), nxt, cur(0), nxt, cur(0), cur(1), _full((CONV_WIDTH, C_CONV)), vec, vec],
        out_specs=[pl.BlockSpec((ts, 2 * C_CONV), lambda b, s: (b * nS + s, 0)), _full((8 * CONV_WIDTH, C_CONV)), vec, vec, vec],
        out_shape=[jax.ShapeDtypeStruct((T, 2 * C_CONV), BF16), jax.ShapeDtypeStruct((8 * CONV_WIDTH, C_CONV), F32),
                   jax.ShapeDtypeStruct((1, C_CONV), F32), jax.ShapeDtypeStruct((1, C_CONV), F32),
                   jax.ShapeDtypeStruct((1, C_CONV), F32)],
        scratch_shapes=[pltpu.VMEM((ts + HALO, C_CONV), F32), pltpu.VMEM((ts, C_CONV), F32),
                        pltpu.VMEM((7, ts + HALO, C_CONV), F32), pltpu.VMEM((ts, C_CONV), F32)],
        compiler_params=_cp("arbitrary", "arbitrary"),
    )(dcp, dcp, hc, hc, u, u, conv_w, ln_g, ln_b)


def _attn_bwd_call(u, do, qg2, kg2, sinks, slopes, B, S):
    T = B * S
    nb = S // BLOCK
    half = Q_W // 2
    npair = GROUP // 2

    def body(sinks_ref, slopes_ref, q_ref, k_ref, v_ref, do_ref, qg_ref, kg_ref,
             dq_ref, dkv_ref, dqg_ref, dkg_ref, dsk_ref, kA, kB, vA, vB, dkn_acc, dv_acc, dkv_keep,
             btab, s_buf, dp_buf, p_buf, ds_buf, qn_buf, do_buf, rq_buf):
        b = pl.program_id(0)
        kvh = pl.program_id(1)
        lo = _lane_lo()
        lane = lax.broadcasted_iota(jnp.int32, (1, 128), 1)
        first = kvh == 0

        @pl.when((b == 0) & first)
        def _():
            dqg_ref[...] = jnp.zeros_like(dqg_ref)
            dkg_ref[...] = jnp.zeros_like(dkg_ref)
            dsk_ref[...] = jnp.zeros_like(dsk_ref)

        _stage_kv(kvh, k_ref, v_ref, kg_ref, lo, kA, kB, vA, vB)
        dkn_acc[...] = jnp.zeros_like(dkn_acc)
        dv_acc[...] = jnp.zeros_like(dv_acc)
        _fill_bias(btab, kvh, slopes_ref)
        qgs = qg_ref[...] * ATTN_SCALE

        def fold(acc):
            canon = jnp.where(lo, acc[0:2 * BLOCK] + pltpu.roll(acc[2 * BLOCK:], HEAD_DIM, 1), 0.0)
            return jnp.where(first, canon, pltpu.roll(canon, HEAD_DIM, 1))

        def block(n, slot):
            s_scr, dp_scr, p_scr, ds_scr = s_buf.at[slot], dp_buf.at[slot], p_buf.at[slot], ds_buf.at[slot]
            qn_scr, do_scr, rq_scr = qn_buf.at[slot], do_buf.at[slot], rq_buf.at[slot]
            r0 = pl.multiple_of(n * BLOCK, BLOCK)
            win = pl.ds(r0, 2 * BLOCK)
            rows = pl.ds(r0, BLOCK)
            k2t = jnp.concatenate([kA[win, :], kB[win, :]], axis=0)
            v2 = jnp.concatenate([vA[win, :], vB[win, :]], axis=0)
            var = jnp.minimum(n, 1)
            for p in range(npair):
                pc = slice(128 * p, 128 * (p + 1))
                qp = q_ref[rows, pc].astype(F32)
                rq = _head_rstd(qp, lo)
                rq_scr[pc, :] = rq
                qn_scr[pc, :] = (qp * rq * qgs).astype(BF16)
                do_scr[pc, :] = do_ref[rows, pc]
            s_scr[...] = _dot_nt(k2t, qn_scr[...])
            dp_scr[...] = _dot_nt(v2, do_scr[...])
            for p in range(npair):
                for hf in range(2):
                    g = 2 * p + hf
                    keys, qs = slice(2 * BLOCK * hf, 2 * BLOCK * (hf + 1)), slice(128 * p, 128 * (p + 1))
                    pn, esk, rz = _softmax_cols(s_scr, btab, p, var, g, hf, sinks_ref[kvh * GROUP + g])
                    dph = dp_scr[keys, qs]
                    delta = jnp.sum(pn * dph, axis=0, keepdims=True)
                    ds_scr[keys, qs] = (pn * (dph - delta)).astype(BF16)
                    p_scr[keys, qs] = pn.astype(BF16)
                    dsk_ref[...] += jnp.where(lane == kvh * GROUP + g, -jnp.sum(esk * rz * delta), 0.0)
            dqs_all = _dot_tn(k2t, ds_scr[...]).T
            for p in range(npair):
                pc = slice(128 * p, 128 * (p + 1))
                qp = q_ref[rows, pc].astype(F32)
                rq = rq_scr[pc, :]
                dqs = dqs_all[pc, :]
                t = dqs * qgs
                dq = rq * t - qp * (rq * rq * rq * (_half_sums(t * qp, lo) * (1.0 / HEAD_DIM)))
                dq_ref[rows, pc] = dq.astype(BF16)
                dqg_ref[...] += jnp.sum(dqs * qp * rq, axis=0, keepdims=True) * ATTN_SCALE
            dkn_acc[win, :] += fold(_dot(ds_scr[...], qn_scr[...]))
            dv_acc[win, :] += fold(_dot(p_scr[...], do_scr[...]))

        def four_blocks(i, carry):
            for slot in range(4):
                block(4 * i + slot, slot)
            return carry

        lax.fori_loop(0, nb // 4, four_blocks, 0)

        dkn = dkn_acc[BLOCK:, :]
        kv = k_ref[...].astype(F32)
        kg = kg_ref[...]
        rk = _head_rstd(kv, lo)
        t = dkn * kg
        dk = rk * t - kv * (rk * rk * rk * (_half_sums(t * kv, lo) * (1.0 / HEAD_DIM)))
        dkg_ref[...] += jnp.sum(dkn * kv * rk, axis=0, keepdims=True)
        dv = dv_acc[BLOCK:, :]

        @pl.when(first)
        def _():
            dkv_keep[:, :128] = dk
            dkv_keep[:, 128:] = dv

        @pl.when(kvh == N_KV_HEADS - 1)
        def _():
            dkv_ref[:, :128] = (dkv_keep[:, :128] + dk).astype(BF16)
            dkv_ref[:, 128:] = (dkv_keep[:, 128:] + dv).astype(BF16)

    smem = pl.BlockSpec(memory_space=pltpu.SMEM)
    kvbuf = pltpu.VMEM((S + BLOCK, 128), BF16)
    accbuf = pltpu.VMEM((S + BLOCK, 128), F32)
    vec = _full((1, 128))
    return pl.pallas_call(
        body, name="attn_bwd", grid=(B, N_KV_HEADS),
        in_specs=[smem, smem,
                  pl.BlockSpec((S, half), lambda b, h: (b, COL_Q // half + h)),
                  pl.BlockSpec((S, 128), lambda b, h: (b, COL_K // 128)),
                  pl.BlockSpec((S, 128), lambda b, h: (b, COL_V // 128)),
                  pl.BlockSpec((S, half), lambda b, h: (b, h)),
                  vec, vec],
        out_specs=[pl.BlockSpec((S, half), lambda b, h: (b, h)), pl.BlockSpec((S, 256), lambda b, h: (b, 0)), vec, vec, vec],
        out_shape=[jax.ShapeDtypeStruct((T, Q_W), BF16), jax.ShapeDtypeStruct((T, 2 * KV_W), BF16),
                   jax.ShapeDtypeStruct((1, 128), F32), jax.ShapeDtypeStruct((1, 128), F32), jax.ShapeDtypeStruct((1, 128), F32)],
        scratch_shapes=[kvbuf, kvbuf, kvbuf, kvbuf, accbuf, accbuf, pltpu.VMEM((S, 256), F32),
                        pltpu.VMEM((2, GROUP, 2 * BLOCK, BLOCK), F32),
                        pltpu.VMEM((4, npair * BLOCK, 4 * BLOCK), F32), pltpu.VMEM((4, npair * BLOCK, 4 * BLOCK), F32),
                        pltpu.VMEM((4, npair * BLOCK, 4 * BLOCK), BF16), pltpu.VMEM((4, npair * BLOCK, 4 * BLOCK), BF16),
                        pltpu.VMEM((4, npair * BLOCK, 128), BF16), pltpu.VMEM((4, npair * BLOCK, 128), BF16),
                        pltpu.VMEM((4, npair * BLOCK, 128), F32)],
        compiler_params=_cp("arbitrary", "arbitrary"),
    )(sinks, slopes, u, u, u, do, qg2, kg2)


def _inproj_bwd_call(du_conv, du_g, du_q, du_kv, w, x, g, dh, tm=512):
    T = x.shape[0]

    def body(dc_ref, dg_ref, dq_ref, dkv_ref, w_ref, x_ref, g_ref, dh_ref, gx_ref, dgm_ref):
        dxn = _dot(dc_ref[...], w_ref[REF_CONV, :])
        dxn += _dot(dg_ref[...], w_ref[REF_GATES, :])
        dxn += _dot(dq_ref[...], w_ref[REF_QKV.start:REF_QKV.start + Q_W, :])
        dxn += _dot(dkv_ref[...], w_ref[REF_QKV.start + Q_W:REF_QKV.stop, :])
        dx, dgt = _rms_bwd(dxn, x_ref[...], g_ref[...])
        gx_ref[...] = dh_ref[...] + dx

        @pl.when(pl.program_id(0) == 0)
        def _():
            dgm_ref[...] = jnp.zeros_like(dgm_ref)

        dgm_ref[...] += jnp.sum(dgt, axis=0, keepdims=True)

    row = _rows(tm, D_MODEL)
    return pl.pallas_call(
        body, name="inproj_bwd", grid=(T // tm,),
        in_specs=[_rows(tm, 2 * C_CONV), _rows(tm, 2 * D_MODEL), _rows(tm, Q_W), _rows(tm, 2 * KV_W), _full(w.shape),
                  row, _full((1, D_MODEL)), row],
        out_specs=[row, _full((1, D_MODEL))],
        out_shape=[jax.ShapeDtypeStruct((T, D_MODEL), F32), jax.ShapeDtypeStruct((1, D_MODEL), F32)],
        compiler_params=_cp("arbitrary"),
    )(du_conv, du_g, du_q, du_kv, w, x, g, dh)


def _adamw_math(w, g, m, v):
    m = ADAM_B1 * m + (1.0 - ADAM_B1) * g
    v = ADAM_B2 * v + (1.0 - ADAM_B2) * (g * g)
    m_hat = m / (1.0 - ADAM_B1 ** ADAM_STEP)
    v_hat = v / (1.0 - ADAM_B2 ** ADAM_STEP)
    delta = -ADAM_LR * (m_hat / (jnp.sqrt(v_hat) + ADAM_EPS) + ADAM_WD * w)
    return delta, m, v


def _adamw_sum_call(name, parts, own, w, m, v, tr):
    R, C = w.shape

    def body(p_ref, own_ref, w_ref, m_ref, v_ref, g_ref, d_ref, nm_ref, nv_ref):
        x, y, c = _position()
        me = 4 * x + 2 * y + c
        g = jnp.zeros((tr, C), F32)
        for s in range(N_DEV):
            g = g + jnp.where(me == s, own_ref[...], p_ref[s]).astype(F32)
        g_ref[...] = g
        d_ref[...], nm_ref[...], nv_ref[...] = _adamw_math(w_ref[...], g, m_ref[...], v_ref[...])

    blk = pl.BlockSpec((tr, C), lambda i: (i, 0))
    out = jax.ShapeDtypeStruct((R, C), F32)
    return pl.pallas_call(
        body, name=name, grid=(R // tr,),
        in_specs=[pl.BlockSpec((N_DEV, tr, C), lambda i: (0, i, 0)), blk, blk, blk, blk],
        out_specs=[blk] * 4, out_shape=[out] * 4,
        compiler_params=_cp("parallel"),
    )(parts, own, w, m, v)


def _adamw_small_call(name, g, w, m, v):
    def body(g_ref, w_ref, m_ref, v_ref, d_ref, nm_ref, nv_ref):
        d_ref[...], nm_ref[...], nv_ref[...] = _adamw_math(w_ref[...], g_ref[...], m_ref[...], v_ref[...])

    out = jax.ShapeDtypeStruct(w.shape, F32)
    return pl.pallas_call(body, name=name, out_shape=[out] * 3)(g, w, m, v)


def _position():
    return lax.axis_index("x"), lax.axis_index("y"), lax.axis_index("c")


def _peer(x, y, c, rel):
    fx, fy, fc = (rel >> 2) & 1, (rel >> 1) & 1, rel & 1
    px = 1 - x if fx else x
    py = 1 - y if fy else y
    pc = 1 - c if fc else c
    return (px, py, pc), 4 * px + 2 * py + pc


def _gather_weights_call(shards):
    n = len(shards)

    def body(*refs):
        ins, outs = refs[:n], refs[n:2 * n]
        send_sems, recv_sems, local_sems = refs[2 * n:]
        x, y, c = _position()
        me = 4 * x + 2 * y + c
        sib, sib_id = _peer(x, y, c, 1)
        chips = [_peer(x, y, c, rel) for rel in (4, 2, 6)]

        def copy(i, k, src, slot, to):
            return pltpu.make_async_remote_copy(
                src_ref=src, dst_ref=outs[i].at[slot], send_sem=send_sems.at[i, k], recv_sem=recv_sems.at[i, k],
                device_id=to, device_id_type=MESH)

        mine = [pltpu.make_async_copy(ins[i], outs[i].at[me], local_sems.at[i]) for i in range(n)]
        for cp in mine:
            cp.start()
        first = []
        for i in range(n):
            first.append(copy(i, 0, ins[i], me, sib))
            first += [copy(i, 1 + j, ins[i], me, dev) for j, (dev, _) in enumerate(chips)]
        for cp in first:
            cp.start()
        passed = []
        for i in range(n):
            for j, (dev, slot) in enumerate(chips):
                copy(i, 1 + j, ins[i], slot, dev).wait_recv()
                fwd = copy(i, 4 + j, outs[i].at[slot], slot, sib)
                fwd.start()
                passed.append(fwd)
        for i in range(n):
            copy(i, 0, ins[i], sib_id, sib).wait_recv()
            for j, (_, slot) in enumerate(chips):
                copy(i, 4 + j, ins[i], slot ^ 1, sib).wait_recv()
        for cp in first + passed:
            cp.wait_send()
        for cp in mine:
            cp.wait()

    hbm = pl.BlockSpec(memory_space=pl.ANY)
    return pl.pallas_call(
        body, name="gather_weights",
        in_specs=[hbm] * n, out_specs=[hbm] * n,
        out_shape=[jax.ShapeDtypeStruct((N_DEV,) + s.shape, s.dtype) for s in shards],
        scratch_shapes=[pltpu.SemaphoreType.DMA((n, 7)), pltpu.SemaphoreType.DMA((n, 7)), pltpu.SemaphoreType.DMA((n,))],
    )(*shards)


_HBM = pl.BlockSpec(memory_space=pltpu.HBM)
_SEM = pl.BlockSpec(memory_space=pltpu.SEMAPHORE)
_EFFECT = pltpu.SideEffectType.DATAFLOW_SIDE_EFFECTING


def _exchange_copies(scatter, incoming, src_refs, land_refs, send_sems, recv_sems):
    x, y, c = _position()
    me = 4 * x + 2 * y + c
    out = []
    for i, (src, land) in enumerate(zip(src_refs, land_refs)):
        for k in range(N_DEV - 1):
            dev, dev_id = _peer(x, y, c, k + 1)
            j = i * (N_DEV - 1) + k
            out.append(pltpu.make_async_remote_copy(
                src_ref=src.at[dev_id] if scatter else src, dst_ref=land.at[dev_id if incoming else me],
                send_sem=send_sems.at[j], recv_sem=recv_sems.at[j], device_id=dev, device_id_type=MESH))
    return out


def _exchange_start_call(name, arrays, scatter):
    n = len(arrays)
    lands = [lax.empty((N_DEV,) + (a.shape[1:] if scatter else a.shape), a.dtype) for a in arrays]

    def body(*refs):
        src_refs, land_refs = refs[:n], refs[n:2 * n]
        send_sems, recv_sems, token = refs[2 * n], refs[2 * n + 1], refs[-1]
        for copy in _exchange_copies(scatter, False, src_refs, land_refs, send_sems, recv_sems):
            copy.start()
        token[...] = jnp.zeros_like(token)

    sems = pltpu.SemaphoreType.DMA((n * (N_DEV - 1),))
    res = pl.pallas_call(
        body, name=name,
        out_shape=[sems, sems] + [pltpu.HBM(a.shape, a.dtype) for a in arrays] + [pltpu.HBM(l.shape, l.dtype) for l in lands]
        + [jax.ShapeDtypeStruct((8, 128), F32)],
        in_specs=[_HBM] * (2 * n), out_specs=[_SEM, _SEM] + [_HBM] * (2 * n) + [pl.BlockSpec(memory_space=pltpu.VMEM)],
        input_output_aliases={i: 2 + i for i in range(2 * n)},
        compiler_params=pltpu.CompilerParams(has_side_effects=_EFFECT),
    )(*[pltpu.with_memory_space_constraint(a, pltpu.HBM) for a in list(arrays) + lands])
    return res[0], res[1], res[2:2 + n], res[2 + n:2 + 2 * n], res[-1]


def _exchange_wait_call(name, started, scatter, after):
    send_sems, recv_sems, srcs, lands, _ = started
    n = len(srcs)

    def body(*refs):
        src_refs, land_refs = refs[:n], refs[n:2 * n]
        send_s, recv_s = refs[2 * n], refs[2 * n + 1]
        for copy in _exchange_copies(scatter, False, src_refs, land_refs, send_s, recv_s):
            copy.wait_send()
        for copy in _exchange_copies(scatter, True, src_refs, land_refs, send_s, recv_s):
            copy.wait_recv()

    res = pl.pallas_call(
        body, name=name,
        out_shape=[pltpu.HBM(a.shape, a.dtype) for a in list(srcs) + list(lands)],
        in_specs=[_HBM] * (2 * n) + [_SEM, _SEM, pl.BlockSpec(memory_space=pl.ANY)], out_specs=[_HBM] * (2 * n),
        input_output_aliases={i: i for i in range(2 * n)},
        compiler_params=pltpu.CompilerParams(has_side_effects=_EFFECT),
    )(*srcs, *lands, send_sems, recv_sems, after)
    return res[n:]


def _allsum_small_call(part, blocks):
    P, Q = part.shape[0], blocks.shape[1]

    def body(p_ref, b_ref, o_ref, ob_ref, all_ref, allb_ref, send_sems, recv_sems):
        x, y, c = _position()
        me = 4 * x + 2 * y + c
        peers = [_peer(x, y, c, rel) for rel in range(1, N_DEV)]
        all_ref[me] = p_ref[...]
        allb_ref[me] = b_ref[me]

        def copies(k, slot, dev, dev_id):
            sems = lambda i: dict(send_sem=send_sems.at[i], recv_sem=recv_sems.at[i], device_id=dev, device_id_type=MESH)
            return (pltpu.make_async_remote_copy(src_ref=p_ref, dst_ref=all_ref.at[slot], **sems(k)),
                    pltpu.make_async_remote_copy(src_ref=b_ref.at[dev_id], dst_ref=allb_ref.at[slot], **sems(7 + k)))

        sends = [cp for k, (dev, dev_id) in enumerate(peers) for cp in copies(k, me, dev, dev_id)]
        for cp in sends:
            cp.start()
        for k, (dev, dev_id) in enumerate(peers):
            for cp in copies(k, dev_id, dev, dev_id):
                cp.wait_recv()
        for cp in sends:
            cp.wait_send()
        tot, totb = all_ref[0], allb_ref[0]
        for s in range(1, N_DEV):
            tot, totb = tot + all_ref[s], totb + allb_ref[s]
        o_ref[...] = tot
        ob_ref[...] = totb

    vm = pl.BlockSpec(memory_space=pltpu.VMEM)
    return pl.pallas_call(
        body, name="allsum_small", in_specs=[vm, vm], out_specs=[vm, vm],
        out_shape=[jax.ShapeDtypeStruct((P, 128), F32), jax.ShapeDtypeStruct((Q, 128), F32)],
        scratch_shapes=[pltpu.VMEM((N_DEV, P, 128), F32), pltpu.VMEM((N_DEV, Q, 128), F32),
                        pltpu.SemaphoreType.DMA((14,)), pltpu.SemaphoreType.DMA((14,))],
    )(part, blocks)


def _pad_rows(v, rows):
    flat = v.reshape(-1)
    return jnp.pad(flat, (0, rows * 128 - flat.shape[0])).reshape(rows, 128)


def _local_step(x, tgt, w_in_t, conv_w, conv_b, ln_g, ln_b, qg, kg, sinks, g_mix, g_ffn, token0, rest_weights, grads_ready, B, S):
    row = lambda v: v.reshape(1, -1)
    qg2 = jnp.tile(qg, 2).reshape(1, 128)
    kg2 = jnp.tile(kg, 2).reshape(1, 128)
    heads = jnp.arange(1, N_Q_HEADS + 1, dtype=F32)
    slopes = jnp.exp2(-8.0 * heads / N_Q_HEADS)

    u, xn = _inproj_call(x, row(g_mix) + token0, w_in_t)
    hc, c = _conv_fwd_call(u, conv_w, row(conv_b), row(ln_g), row(ln_b), B, S)
    o = _attn_fwd_call(u, qg2, kg2, sinks, slopes, B, S)
    wc, wa, wm, w_ffn_in, wd = rest_weights(o)
    yc, ya, z, h, hn = _merge_call(c, o, u, x, wc, wa, wm, row(g_ffn))
    gu, act = _ffn_in_call(hn, w_ffn_in)
    sse, dout, dout_b = _ffn_out_call(act, wd, h, tgt)

    dgu = _ffn_down_bwd_call(dout_b, wd, gu)
    g_wd = _dw_call("dw_ffn_down", act, dout_b, tr=D_FF // 2)
    g_wffn_t = _dw_call("dw_ffn_in", dgu, hn, tr=2 * D_FF // 4)
    tok = grads_ready(dict(w_ffn_down=g_wd, w_ffn_in=g_wffn_t))
    dh, dh_b, g_gffn = _ffn_in_bwd_call(dgu, w_ffn_in, h, row(g_ffn) + tok, dout)
    dyc, dya, du_g, dcp, do = _merge_bwd_call(dh_b, wm, wc, wa, u, yc, ya)
    g_wm = _dw_call("dw_merge", z, dh_b, tr=D_MODEL)
    g_wc = _dw_call("dw_conv_out", c, dyc, tr=D_MODEL)
    g_wa = _dw_call("dw_attn_out", o, dya, tr=D_MODEL)
    tok = grads_ready(dict(w_merge_out=g_wm, w_conv_out=g_wc, w_attn_out=g_wa))
    du_conv, g_cw8, g_cb, g_lg, g_lb = _conv_bwd_call(dcp, hc, u, conv_w, row(ln_g) + tok, row(ln_b), B, S)
    du_q, du_kv, g_qg2, g_kg2, g_sk = _attn_bwd_call(u, do, qg2, kg2, sinks, slopes, B, S)
    g_win_t = _concat_rows_call("dw_in_rows", [
        _dw_call("dw_in_conv", du_conv, xn, tr=D_MODEL),
        _dw_call("dw_in_q", du_q, xn, tr=D_MODEL),
        _dw_call("dw_in_kv", du_kv, xn, tr=2 * KV_W),
        _dw_call("dw_in_gates", du_g, xn, tr=D_MODEL)])
    tok = grads_ready(dict(w_in=g_win_t))
    grad_x, g_gmix = _inproj_bwd_call(du_conv, du_g, du_q, du_kv, w_in_t, x, row(g_mix) + tok, dh)

    g_cw = g_cw8.reshape(CONV_WIDTH, 8, C_CONV).sum(axis=1)
    g_qg = g_qg2[0, :HEAD_DIM] + g_qg2[0, HEAD_DIM:]
    g_kg = g_kg2[0, :HEAD_DIM] + g_kg2[0, HEAD_DIM:]
    small = dict(norm_mix_g=g_gmix[0], conv_dw_b=g_cb[0], conv_ln_g=g_lg[0], conv_ln_b=g_lb[0], q_norm_g=g_qg,
                 k_norm_g=g_kg, sinks=g_sk[0, :N_Q_HEADS], norm_ffn_g=g_gffn[0], conv_dw_w=g_cw)
    return sse[0, 0], grad_x, small


_COL_SHARDED = ("w_in", "w_ffn_in")

_SMALL = (("norm_mix_g", 8), ("conv_dw_b", 8), ("conv_ln_g", 8), ("conv_ln_b", 8), ("norm_ffn_g", 8),
          ("q_norm_g", 1), ("k_norm_g", 1), ("sinks", 1))
_SMALL_ROWS = 48


def _pack_small(d):
    rows = [_pad_rows(d[name], r) for name, r in _SMALL]
    used = sum(r for _, r in _SMALL)
    return jnp.concatenate(rows + [jnp.zeros((_SMALL_ROWS - used, 128), F32)], axis=0)


def _unpack_small(packed, like):
    out, r0 = {}, 0
    for name, r in _SMALL:
        n = like[name].size
        out[name] = packed[r0:r0 + r].reshape(-1)[:n].reshape(like[name].shape)
        r0 += r
    return out


def kernel(x, norm_mix_g, w_in, conv_dw_w, conv_dw_b, conv_ln_g, conv_ln_b, w_conv_out, q_norm_g, k_norm_g, sinks, w_attn_out, w_merge_out, norm_ffn_g, w_ffn_in, w_ffn_down, loss_target, m_norm_mix_g, m_w_in, m_conv_dw_w, m_conv_dw_b, m_conv_ln_g, m_conv_ln_b, m_w_conv_out, m_q_norm_g, m_k_norm_g, m_sinks, m_w_attn_out, m_w_merge_out, m_norm_ffn_g, m_w_ffn_in, m_w_ffn_down, v_norm_mix_g, v_w_in, v_conv_dw_w, v_conv_dw_b, v_conv_ln_g, v_conv_ln_b, v_w_conv_out, v_q_norm_g, v_k_norm_g, v_sinks, v_w_attn_out, v_w_merge_out, v_norm_ffn_g, v_w_ffn_in, v_w_ffn_down):
    names = ["norm_mix_g", "w_in", "conv_dw_w", "conv_dw_b", "conv_ln_g", "conv_ln_b", "w_conv_out", "q_norm_g",
             "k_norm_g", "sinks", "w_attn_out", "w_merge_out", "norm_ffn_g", "w_ffn_in", "w_ffn_down"]
    w = dict(zip(names, (norm_mix_g, w_in, conv_dw_w, conv_dw_b, conv_ln_g, conv_ln_b, w_conv_out, q_norm_g, k_norm_g,
                         sinks, w_attn_out, w_merge_out, norm_ffn_g, w_ffn_in, w_ffn_down)))
    m = dict(zip(names, (m_norm_mix_g, m_w_in, m_conv_dw_w, m_conv_dw_b, m_conv_ln_g, m_conv_ln_b, m_w_conv_out,
                         m_q_norm_g, m_k_norm_g, m_sinks, m_w_attn_out, m_w_merge_out, m_norm_ffn_g, m_w_ffn_in,
                         m_w_ffn_down)))
    v = dict(zip(names, (v_norm_mix_g, v_w_in, v_conv_dw_w, v_conv_dw_b, v_conv_ln_g, v_conv_ln_b, v_w_conv_out,
                         v_q_norm_g, v_k_norm_g, v_sinks, v_w_attn_out, v_w_merge_out, v_norm_ffn_g, v_w_ffn_in,
                         v_w_ffn_down)))
    B, S, _ = x.shape
    T = B * S
    me = 4 * lax.axis_index("x") + 2 * lax.axis_index("y") + lax.axis_index("c")

    as_rows = lambda k, t: t.T if k in _COL_SHARDED else t
    cw_shard = _pad_rows(conv_dw_w.reshape(CONV_WIDTH, 128), 32)
    g_win, g_cwt = _gather_weights_call([w_in.astype(BF16), cw_shard])
    conv_w_full = g_cwt.reshape(N_DEV, 32 * 128)[:, :CONV_WIDTH * 128].reshape(N_DEV, CONV_WIDTH, 128)
    conv_w_full = conv_w_full.transpose(1, 0, 2).reshape(CONV_WIDTH, C_CONV)
    w_in_t = g_win.transpose(0, 2, 1).reshape(IN_COLS, D_MODEL)
    rest_names = ["w_conv_out", "w_attn_out", "w_merge_out", "w_ffn_in", "w_ffn_down"]
    rest_shards = [as_rows(k, w[k]).astype(BF16) for k in rest_names]
    rest_started = _exchange_start_call("gather_rest_start", rest_shards, scatter=False)

    def rest_weights(after):
        lands = _exchange_wait_call("gather_rest_wait", rest_started, False, after)
        mine = lax.broadcasted_iota(jnp.int32, (N_DEV, 1, 1), 0) == me
        full = [jnp.where(mine, s[None], l) for l, s in zip(lands, rest_shards)]
        return tuple(f.reshape(N_DEV * f.shape[1], D_MODEL) for f in full)

    exchanges = []

    def grads_ready(group):
        ks = list(group)
        blocks = [group[k].reshape(N_DEV, group[k].shape[0] // N_DEV, D_MODEL) for k in ks]
        started = _exchange_start_call("scatter_" + ks[0] + "_start", blocks, scatter=True)
        exchanges.append((ks, started, [lax.dynamic_index_in_dim(b, me, 0, keepdims=False) for b in blocks]))
        return started[-1][0:1, 0:1]

    sse, grad_x, small = _local_step(
        x.reshape(T, D_MODEL), loss_target.reshape(T, D_MODEL), w_in_t, conv_w_full, conv_dw_b, conv_ln_g, conv_ln_b,
        q_norm_g, k_norm_g, sinks, norm_mix_g, norm_ffn_g, rest_started[-1][0:1, 0:1], rest_weights, grads_ready, B, S)

    grad, delta, new_m, new_v = {}, {}, {}, {}
    packed = jnp.concatenate([_pack_small(small), jnp.full((8, 128), sse, F32)], axis=0)
    cw_blocks = small["conv_dw_w"].reshape(CONV_WIDTH, N_DEV, 128).transpose(1, 0, 2)
    tot, cw_tot = _allsum_small_call(packed, jnp.pad(cw_blocks, ((0, 0), (0, 32 - CONV_WIDTH), (0, 0))))
    loss = 0.5 / D_MODEL * tot[_SMALL_ROWS, 0]
    g_small = _unpack_small(tot[:_SMALL_ROWS], w)
    d_s, m_s, v_s = _adamw_small_call("adamw_small", tot[:_SMALL_ROWS], _pack_small(w), _pack_small(m), _pack_small(v))
    d_small, m_small, v_small = _unpack_small(d_s, w), _unpack_small(m_s, w), _unpack_small(v_s, w)
    g_cw = cw_tot[:CONV_WIDTH]
    cw2 = lambda t: t.reshape(CONV_WIDTH, 128)
    d_cw, m_cw, v_cw = _adamw_small_call("adamw_conv_w", g_cw, cw2(conv_dw_w), cw2(m_conv_dw_w), cw2(v_conv_dw_w))
    cw3 = lambda t: t.reshape(CONV_WIDTH, 1, 128)
    for k, _ in _SMALL:
        grad[k], delta[k], new_m[k], new_v[k] = g_small[k], d_small[k], m_small[k], v_small[k]
    grad["conv_dw_w"], delta["conv_dw_w"], new_m["conv_dw_w"], new_v["conv_dw_w"] = cw3(g_cw), cw3(d_cw), cw3(m_cw), cw3(v_cw)

    after = d_cw
    for ks, started, owns in exchanges:
        lands = _exchange_wait_call("scatter_" + ks[0] + "_wait", started, True, after)
        for k, land, own in zip(ks, lands, owns):
            tr = {672: 336, 704: 352, 352: 176, 128: 128}[own.shape[0]]
            outs = _adamw_sum_call("adamw_" + k, land, own, as_rows(k, w[k]), as_rows(k, m[k]), as_rows(k, v[k]), tr)
            grad[k], delta[k], new_m[k], new_v[k] = (as_rows(k, t) for t in outs)
            after = outs[3]

    return (loss, grad_x.reshape(B, S, D_MODEL), *[grad[k] for k in names], *[delta[k] for k in names],
            *[new_m[k] for k in names], *[new_v[k] for k in names])
```

```python
import functools
import math

import jax
import jax.numpy as jnp
from jax import lax
from jax.experimental import pallas as pl
from jax.experimental.pallas import tpu as pltpu

F32 = jnp.float32
BF16 = jnp.bfloat16
MESH = pl.DeviceIdType.MESH

N_DEV = 8
D_MODEL = 1024
C_CONV = 1024
CONV_WIDTH = 31
HEAD_DIM = 64
N_Q_HEADS = 16
N_KV_HEADS = 2
GROUP = N_Q_HEADS // N_KV_HEADS
WINDOW = 128
BLOCK = 128
D_FF = 2816
EPS = 1e-6
NEG = -1e30
ATTN_SCALE = 1.0 / math.sqrt(HEAD_DIM)
Q_W = N_Q_HEADS * HEAD_DIM
KV_W = N_KV_HEADS * HEAD_DIM
IN_COLS = 2 * C_CONV + Q_W + 2 * KV_W + 2 * D_MODEL

ADAM_LR = 0.001
ADAM_B1 = 0.9
ADAM_B2 = 0.999
ADAM_EPS = 1e-08
ADAM_WD = 0.01
ADAM_STEP = 10

COL_A, COL_GATE, COL_GC, COL_GA, COL_Q, COL_K, COL_V = 0, 1024, 2048, 3072, 4096, 5120, 5248
REF_CONV = slice(0, 2 * C_CONV)
REF_QKV = slice(2 * C_CONV, 2 * C_CONV + Q_W + 2 * KV_W)
REF_GATES = slice(2 * C_CONV + Q_W + 2 * KV_W, IN_COLS)

HALO = 32
VMEM_LIMIT = 56 * 1024 * 1024


def _cp(*sem):
    return pltpu.CompilerParams(dimension_semantics=sem, vmem_limit_bytes=VMEM_LIMIT)


def _sigmoid(x):
    return jax.nn.sigmoid(x)


def _dot(a, b):
    return jnp.dot(a, b, preferred_element_type=F32)


def _dot_nt(a, b):
    return lax.dot_general(a, b, (((1,), (1,)), ((), ())), preferred_element_type=F32)


def _dot_tn(a, b):
    return lax.dot_general(a, b, (((0,), (0,)), ((), ())), preferred_element_type=F32)


def _full(shape):
    return pl.BlockSpec(shape, lambda *_: (0,) * len(shape))


def _rows(tm, ncols, colblk=0):
    return pl.BlockSpec((tm, ncols), lambda i: (i, colblk))


def _inproj_call(x, g, w, tm=512):
    T = x.shape[0]
    N = w.shape[0]

    def body(x_ref, g_ref, w_ref, u_ref, xn_ref):
        xv = x_ref[...]
        r = lax.rsqrt(jnp.mean(xv * xv, axis=-1, keepdims=True) + EPS)
        xn = (xv * r * g_ref[...]).astype(BF16)
        xn_ref[...] = xn
        u_ref[:, COL_A:COL_GC] = _dot_nt(xn, w_ref[REF_CONV, :]).astype(BF16)
        u_ref[:, COL_GC:COL_Q] = _dot_nt(xn, w_ref[REF_GATES, :]).astype(BF16)
        u_ref[:, COL_Q:] = _dot_nt(xn, w_ref[REF_QKV, :]).astype(BF16)

    return pl.pallas_call(
        body, name="inproj", grid=(T // tm,),
        in_specs=[_rows(tm, D_MODEL), _full((1, D_MODEL)), _full(w.shape)],
        out_specs=[_rows(tm, N), _rows(tm, D_MODEL)],
        out_shape=[jax.ShapeDtypeStruct((T, N), BF16), jax.ShapeDtypeStruct((T, D_MODEL), BF16)],
        compiler_params=_cp("parallel"),
    )(x, g, w)


def _fill_shifted(src_ref, sh_ref):
    n = src_ref.shape[0] - 8
    for s in range(1, 8):
        sh_ref[s - 1, 0:n, :] = src_ref[s:s + n, :]


def _window(src_ref, sh_ref, off, cols):
    s = off % 8
    if s == 0:
        return src_ref[off:off + 128, cols]
    return sh_ref[s - 1, off - s:off - s + 128, cols]


def _conv_taps(w_ref, src_ref, sh_ref, base, rows, cb, reverse):
    cols = pl.ds(pl.multiple_of(cb * 128, 128), 128)
    outs = []
    for rb in range(rows // 128):
        acc = jnp.zeros((128, 128), F32)
        for j in range(CONV_WIDTH):
            off = base + (CONV_WIDTH - 1 - j if reverse else j) + rb * 128
            acc = acc + w_ref[j:j + 1, cols] * _window(src_ref, sh_ref, off, cols)
        outs.append(acc)
    return outs


def _conv_fwd_call(u, conv_w, conv_b, ln_g, ln_b, B, S, ts=512):
    T = B * S
    nS = S // ts
    per32 = ts // HALO

    def body(a_ref, gate_ref, ap_ref, gp_ref, w_ref, b_ref, g_ref, bb_ref, hc_ref, c_ref, hext, hsh, acc_ref):
        s = pl.program_id(1)
        prev = ap_ref[...].astype(F32) * _sigmoid(gp_ref[...].astype(F32))
        hext[0:HALO, :] = jnp.where(s > 0, prev, 0.0)
        hext[HALO:, :] = a_ref[...].astype(F32) * _sigmoid(gate_ref[...].astype(F32))
        _fill_shifted(hext, hsh)

        def colblock(cb, carry):
            cols = pl.ds(pl.multiple_of(cb * 128, 128), 128)
            outs = _conv_taps(w_ref, hext, hsh, HALO - (CONV_WIDTH - 1), ts, cb, reverse=False)
            for rb, acc in enumerate(outs):
                acc_ref[rb * 128:(rb + 1) * 128, cols] = acc
            return carry

        lax.fori_loop(0, C_CONV // 128, colblock, 0)
        hc = acc_ref[...] + b_ref[...]
        hc_ref[...] = hc
        mu = jnp.mean(hc, axis=-1, keepdims=True)
        xc = hc - mu
        var = jnp.mean(xc * xc, axis=-1, keepdims=True)
        y = xc * lax.rsqrt(var + EPS) * g_ref[...] + bb_ref[...]
        c_ref[...] = (y * _sigmoid(y)).astype(BF16)

    cur = lambda cb: pl.BlockSpec((ts, C_CONV), lambda b, s: (b * nS + s, cb))
    prv = lambda cb: pl.BlockSpec((HALO, C_CONV), lambda b, s: (jnp.maximum(b * (S // HALO) + s * per32 - 1, 0), cb))
    vec = _full((1, C_CONV))
    return pl.pallas_call(
        body, name="conv_fwd", grid=(B, nS),
        in_specs=[cur(0), cur(1), prv(0), prv(1), _full((CONV_WIDTH, C_CONV)), vec, vec, vec],
        out_specs=[pl.BlockSpec((ts, C_CONV), lambda b, s: (b * nS + s, 0))] * 2,
        out_shape=[jax.ShapeDtypeStruct((T, C_CONV), F32), jax.ShapeDtypeStruct((T, C_CONV), BF16)],
        scratch_shapes=[pltpu.VMEM((HALO + ts, C_CONV), F32), pltpu.VMEM((7, HALO + ts, C_CONV), F32),
                        pltpu.VMEM((ts, C_CONV), F32)],
        compiler_params=_cp("parallel", "arbitrary"),
    )(u, u, u, u, conv_w, conv_b, ln_g, ln_b)


def _lane_lo():
    return lax.broadcasted_iota(jnp.int32, (1, 128), 1) < HEAD_DIM


def _half_sums(t, lo):
    s_lo = jnp.sum(jnp.where(lo, t, 0.0), axis=-1, keepdims=True)
    s_hi = jnp.sum(jnp.where(lo, 0.0, t), axis=-1, keepdims=True)
    return jnp.where(lo, s_lo, s_hi)


def _head_rstd(t, lo):
    return lax.rsqrt(_half_sums(t * t, lo) * (1.0 / HEAD_DIM) + EPS)


def _fill_bias(btab, kvh, slopes_ref):
    sj = lax.broadcasted_iota(jnp.int32, (2 * BLOCK, BLOCK), 0)
    qi = lax.broadcasted_iota(jnp.int32, (2 * BLOCK, BLOCK), 1)
    dist = qi + BLOCK - sj
    valid = (dist >= 0) & (dist < WINDOW)
    distf = dist.astype(F32)
    for g in range(GROUP):
        bias = jnp.where(valid, -slopes_ref[kvh * GROUP + g] * distf, NEG)
        btab[1, g] = bias
        btab[0, g] = jnp.where(sj >= BLOCK, bias, NEG)


def _softmax_cols(st_ref, btab, p, var, g, hf, sk):
    sh = st_ref[2 * BLOCK * hf:2 * BLOCK * (hf + 1), 128 * p:128 * (p + 1)] + btab[var, g]
    m = jnp.maximum(jnp.max(sh, axis=0, keepdims=True), sk)
    e = jnp.exp(sh - m)
    esk = jnp.exp(sk - m)
    rz = 1.0 / (jnp.sum(e, axis=0, keepdims=True) + esk)
    return e * rz, esk, rz


def _stage_kv(kvh, k_ref, v_ref, kg_ref, lo, kA, kB, vA, vB):
    kv = k_ref[...].astype(F32)
    rk = _head_rstd(kv, lo)
    kn = kv * rk * kg_ref[...]
    mine = lax.broadcasted_iota(jnp.int32, (1, 128), 1) // HEAD_DIM == kvh
    ksel = jnp.where(mine, kn, 0.0)
    vsel = jnp.where(mine, v_ref[...].astype(F32), 0.0)
    krol = pltpu.roll(ksel, HEAD_DIM, 1)
    vrol = pltpu.roll(vsel, HEAD_DIM, 1)
    first = kvh == 0
    zeros = jnp.zeros((BLOCK, 128), BF16)
    for ref, val in ((kA, jnp.where(first, ksel, krol)), (kB, jnp.where(first, krol, ksel)),
                     (vA, jnp.where(first, vsel, vrol)), (vB, jnp.where(first, vrol, vsel))):
        ref[0:BLOCK, :] = zeros
        ref[BLOCK:, :] = val.astype(BF16)


def _attn_fwd_call(u, qg2, kg2, sinks, slopes, B, S):
    T = B * S
    nb = S // BLOCK
    half = Q_W // 2
    npair = GROUP // 2
    per_trip = 8 if nb % 8 == 0 else 4

    def body(sinks_ref, slopes_ref, q_ref, k_ref, v_ref, qg_ref, kg_ref, o_ref, kA, kB, vA, vB, btab, qn_buf, s_buf, p_buf):
        kvh = pl.program_id(1)
        lo = _lane_lo()
        _stage_kv(kvh, k_ref, v_ref, kg_ref, lo, kA, kB, vA, vB)
        _fill_bias(btab, kvh, slopes_ref)
        qgs = qg_ref[...] * ATTN_SCALE

        def block(n, slot):
            qn_scr, s_scr, p_scr = qn_buf.at[slot], s_buf.at[slot], p_buf.at[slot]
            r0 = pl.multiple_of(n * BLOCK, BLOCK)
            win = pl.ds(r0, 2 * BLOCK)
            k2t = jnp.concatenate([kA[win, :], kB[win, :]], axis=0)
            v2 = jnp.concatenate([vA[win, :], vB[win, :]], axis=0)
            var = jnp.minimum(n, 1)
            for p in range(npair):
                pc = slice(128 * p, 128 * (p + 1))
                qp = q_ref[pl.ds(r0, BLOCK), pc].astype(F32)
                qn_scr[pc, :] = (qp * _head_rstd(qp, lo) * qgs).astype(BF16)
            s_scr[...] = _dot_nt(k2t, qn_scr[...])
            for p in range(npair):
                for hf in range(2):
                    g = 2 * p + hf
                    pn, _, _ = _softmax_cols(s_scr, btab, p, var, g, hf, sinks_ref[kvh * GROUP + g])
                    p_scr[2 * BLOCK * hf:2 * BLOCK * (hf + 1), 128 * p:128 * (p + 1)] = pn.astype(BF16)
            o_all = _dot_tn(v2, p_scr[...]).T
            for p in range(npair):
                pc = slice(128 * p, 128 * (p + 1))
                o_ref[pl.ds(r0, BLOCK), pc] = o_all[pc, :].astype(BF16)

        def trip(i, carry):
            for slot in range(per_trip):
                block(per_trip * i + slot, slot)
            return carry

        lax.fori_loop(0, nb // per_trip, trip, 0)

    smem = pl.BlockSpec(memory_space=pltpu.SMEM)
    kvbuf = pltpu.VMEM((S + BLOCK, 128), BF16)
    pair_scores = (per_trip, npair * BLOCK, 4 * BLOCK)
    return pl.pallas_call(
        body, name="attn_fwd", grid=(B, N_KV_HEADS),
        in_specs=[smem, smem,
                  pl.BlockSpec((S, half), lambda b, h: (b, COL_Q // half + h)),
                  pl.BlockSpec((S, 128), lambda b, h: (b, COL_K // 128)),
                  pl.BlockSpec((S, 128), lambda b, h: (b, COL_V // 128)),
                  _full((1, 128)), _full((1, 128))],
        out_specs=pl.BlockSpec((S, half), lambda b, h: (b, h)),
        out_shape=jax.ShapeDtypeStruct((T, Q_W), BF16),
        scratch_shapes=[kvbuf, kvbuf, kvbuf, kvbuf, pltpu.VMEM((2, GROUP, 2 * BLOCK, BLOCK), F32),
                        pltpu.VMEM((per_trip, npair * BLOCK, 128), BF16), pltpu.VMEM(pair_scores, F32),
                        pltpu.VMEM(pair_scores, BF16)],
        compiler_params=_cp("parallel", "arbitrary"),
    )(sinks, slopes, u, u, u, qg2, kg2)


def _merge_call(c, o, u, x, wc, wa, wm, g_ffn, tm=512):
    T = x.shape[0]

    def body(c_ref, o_ref, gc_ref, ga_ref, x_ref, wc_ref, wa_ref, wm_ref, g_ref, yc_ref, ya_ref, z_ref, h_ref, hn_ref):
        yc = _dot(c_ref[...], wc_ref[...])
        ya = _dot(o_ref[...], wa_ref[...])
        yc_ref[...] = yc.astype(BF16)
        ya_ref[...] = ya.astype(BF16)
        z = (_sigmoid(gc_ref[...].astype(F32)) * yc + _sigmoid(ga_ref[...].astype(F32)) * ya).astype(BF16)
        z_ref[...] = z
        h = x_ref[...] + _dot(z, wm_ref[...])
        h_ref[...] = h
        r = lax.rsqrt(jnp.mean(h * h, axis=-1, keepdims=True) + EPS)
        hn_ref[...] = (h * r * g_ref[...]).astype(BF16)

    sq = _full((D_MODEL, D_MODEL))
    row = _rows(tm, D_MODEL)
    f32 = jax.ShapeDtypeStruct((T, D_MODEL), F32)
    b16 = jax.ShapeDtypeStruct((T, D_MODEL), BF16)
    return pl.pallas_call(
        body, name="merge_fwd", grid=(T // tm,),
        in_specs=[row, row, _rows(tm, D_MODEL, COL_GC // D_MODEL), _rows(tm, D_MODEL, COL_GA // D_MODEL), row,
                  sq, sq, sq, _full((1, D_MODEL))],
        out_specs=[row] * 5,
        out_shape=[b16, b16, b16, f32, b16],
        compiler_params=_cp("parallel"),
    )(c, o, u, u, x, wc, wa, wm, g_ffn)


def _ffn_in_call(hn, w, tm=512):
    T = hn.shape[0]

    def body(hn_ref, w_ref, fac_ref, act_ref):
        gu = _dot_nt(hn_ref[...], w_ref[...])
        g = gu[:, :D_FF]
        up = gu[:, D_FF:]
        sg = _sigmoid(g)
        silu = g * sg
        act_ref[...] = (silu * up).astype(BF16)
        fac_ref[:, :D_FF] = (up * (sg * (1.0 + g * (1.0 - sg)))).astype(BF16)
        fac_ref[:, D_FF:] = silu.astype(BF16)

    return pl.pallas_call(
        body, name="ffn_in", grid=(T // tm,),
        in_specs=[_rows(tm, D_MODEL), _full(w.shape)],
        out_specs=[_rows(tm, 2 * D_FF), _rows(tm, D_FF)],
        out_shape=[jax.ShapeDtypeStruct((T, 2 * D_FF), BF16), jax.ShapeDtypeStruct((T, D_FF), BF16)],
        compiler_params=_cp("parallel"),
    )(hn, w)


def _ffn_out_call(act, wd, h, tgt, tm=512):
    T = h.shape[0]

    def body(act_ref, wd_ref, h_ref, t_ref, sse_ref, d_ref, db_ref):
        e = h_ref[...] + _dot(act_ref[...], wd_ref[...]) - t_ref[...]

        @pl.when(pl.program_id(0) == 0)
        def _():
            sse_ref[...] = jnp.zeros_like(sse_ref)

        sse_ref[...] += jnp.sum(e * e)
        d = e * (1.0 / D_MODEL)
        d_ref[...] = d
        db_ref[...] = d.astype(BF16)

    row = _rows(tm, D_MODEL)
    return pl.pallas_call(
        body, name="ffn_out", grid=(T // tm,),
        in_specs=[_rows(tm, D_FF), _full(wd.shape), row, row],
        out_specs=[_full((8, 128)), row, row],
        out_shape=[jax.ShapeDtypeStruct((8, 128), F32), jax.ShapeDtypeStruct((T, D_MODEL), F32),
                   jax.ShapeDtypeStruct((T, D_MODEL), BF16)],
        compiler_params=_cp("arbitrary"),
    )(act, wd, h, tgt)


def _dw_call(name, a, b, tr, tk=2048):
    T, K = a.shape
    N = b.shape[1]
    tk = min(tk, T)
    nk = T // tk

    def body(a_ref, b_ref, o_ref, acc_ref):
        k = pl.program_id(1)

        @pl.when(k == 0)
        def _():
            acc_ref[...] = jnp.zeros_like(acc_ref)

        acc_ref[...] += _dot_tn(a_ref[...], b_ref[...])

        @pl.when(k == nk - 1)
        def _():
            o_ref[...] = acc_ref[...].astype(BF16)

    return pl.pallas_call(
        body, name=name, grid=(K // tr, nk),
        in_specs=[pl.BlockSpec((tk, tr), lambda j, k: (k, j)), pl.BlockSpec((tk, N), lambda j, k: (k, 0))],
        out_specs=pl.BlockSpec((tr, N), lambda j, k: (j, 0)),
        out_shape=jax.ShapeDtypeStruct((K, N), BF16),
        scratch_shapes=[pltpu.VMEM((tr, N), F32)],
        compiler_params=_cp("parallel", "arbitrary"),
    )(a, b)


def _concat_rows_call(name, pieces, tr=256):
    N = pieces[0].shape[1]
    starts = [0]
    for p in pieces:
        starts.append(starts[-1] + p.shape[0] // tr)

    def body(*refs):
        o_ref = refs[-1]
        i = pl.program_id(0)
        for p, ref in enumerate(refs[:-1]):
            @pl.when((i >= starts[p]) & (i < starts[p + 1]))
            def _(ref=ref):
                o_ref[...] = ref[...]

    def spec(p):
        lo, n = starts[p], pieces[p].shape[0] // tr
        return pl.BlockSpec((tr, N), lambda i: (jnp.clip(i - lo, 0, n - 1), 0))

    return pl.pallas_call(
        body, name=name, grid=(starts[-1],),
        in_specs=[spec(p) for p in range(len(pieces))], out_specs=pl.BlockSpec((tr, N), lambda i: (i, 0)),
        out_shape=jax.ShapeDtypeStruct((starts[-1] * tr, N), pieces[0].dtype),
        compiler_params=_cp("arbitrary"),
    )(*pieces)


def _rms_bwd(dy, xv, g):
    r = lax.rsqrt(jnp.mean(xv * xv, axis=-1, keepdims=True) + EPS)
    t = dy * g
    dx = r * t - xv * (r * r * r * jnp.mean(t * xv, axis=-1, keepdims=True))
    return dx, dy * xv * r


def _ffn_down_bwd_call(dout_b, wd, gu, tm=512):
    T = dout_b.shape[0]

    def body(d_ref, wd_ref, fg_ref, fu_ref, o_ref):
        dact = _dot_nt(d_ref[...], wd_ref[...])
        o_ref[:, :D_FF] = (dact * fg_ref[...].astype(F32)).astype(BF16)
        o_ref[:, D_FF:] = (dact * fu_ref[...].astype(F32)).astype(BF16)

    return pl.pallas_call(
        body, name="ffn_down_bwd", grid=(T // tm,),
        in_specs=[_rows(tm, D_MODEL), _full(wd.shape), _rows(tm, D_FF, 0), _rows(tm, D_FF, 1)],
        out_specs=_rows(tm, 2 * D_FF),
        out_shape=jax.ShapeDtypeStruct((T, 2 * D_FF), BF16),
        compiler_params=_cp("parallel"),
    )(dout_b, wd, gu, gu)


def _ffn_in_bwd_call(dgu, w, h, g_ffn, dout, tm=512):
    T = h.shape[0]

    def body(dgu_ref, w_ref, h_ref, g_ref, d_ref, dh_ref, dhb_ref, dg_ref):
        dhn = _dot(dgu_ref[...], w_ref[...])
        dx, dgt = _rms_bwd(dhn, h_ref[...], g_ref[...])
        dh = d_ref[...] + dx
        dh_ref[...] = dh
        dhb_ref[...] = dh.astype(BF16)

        @pl.when(pl.program_id(0) == 0)
        def _():
            dg_ref[...] = jnp.zeros_like(dg_ref)

        dg_ref[...] += jnp.sum(dgt, axis=0, keepdims=True)

    row = _rows(tm, D_MODEL)
    return pl.pallas_call(
        body, name="ffn_in_bwd", grid=(T // tm,),
        in_specs=[_rows(tm, 2 * D_FF), _full(w.shape), row, _full((1, D_MODEL)), row],
        out_specs=[row, row, _full((1, D_MODEL))],
        out_shape=[jax.ShapeDtypeStruct((T, D_MODEL), F32), jax.ShapeDtypeStruct((T, D_MODEL), BF16),
                   jax.ShapeDtypeStruct((1, D_MODEL), F32)],
        compiler_params=_cp("arbitrary"),
    )(dgu, w, h, g_ffn, dout)


def _merge_bwd_call(dh_b, wm, wc, wa, u, yc, ya, tm=512):
    T = dh_b.shape[0]

    def body(dh_ref, wm_ref, wc_ref, wa_ref, gc_ref, ga_ref, yc_ref, ya_ref, dyc_ref, dya_ref, dug_ref, dcp_ref, do_ref):
        dz = _dot_nt(dh_ref[...], wm_ref[...])
        sgc = _sigmoid(gc_ref[...].astype(F32))
        sga = _sigmoid(ga_ref[...].astype(F32))
        dyc = (dz * sgc).astype(BF16)
        dya = (dz * sga).astype(BF16)
        dyc_ref[...] = dyc
        dya_ref[...] = dya
        dug_ref[:, :D_MODEL] = (dz * yc_ref[...].astype(F32) * (sgc * (1.0 - sgc))).astype(BF16)
        dug_ref[:, D_MODEL:] = (dz * ya_ref[...].astype(F32) * (sga * (1.0 - sga))).astype(BF16)
        dcp_ref[...] = _dot_nt(dyc, wc_ref[...]).astype(BF16)
        do_ref[...] = _dot_nt(dya, wa_ref[...]).astype(BF16)

    sq = _full((D_MODEL, D_MODEL))
    row = _rows(tm, D_MODEL)
    b16 = jax.ShapeDtypeStruct((T, D_MODEL), BF16)
    return pl.pallas_call(
        body, name="merge_bwd", grid=(T // tm,),
        in_specs=[row, sq, sq, sq, _rows(tm, D_MODEL, COL_GC // D_MODEL), _rows(tm, D_MODEL, COL_GA // D_MODEL), row, row],
        out_specs=[row, row, _rows(tm, 2 * D_MODEL), row, row],
        out_shape=[b16, b16, jax.ShapeDtypeStruct((T, 2 * D_MODEL), BF16), b16, b16],
        compiler_params=_cp("parallel"),
    )(dh_b, wm, wc, wa, u, u, yc, ya)


def _conv_bwd_call(dcp, hc, u, conv_w, ln_g, ln_b, B, S, ts=256):
    T = B * S
    nS = S // ts
    per32 = ts // HALO
    last32 = T // HALO - 1

    def ln_bwd(dcp_v, hc_v, g, bb):
        mu = jnp.mean(hc_v, axis=-1, keepdims=True)
        xc = hc_v - mu
        rstd = lax.rsqrt(jnp.mean(xc * xc, axis=-1, keepdims=True) + EPS)
        xhat = xc * rstd
        y = xhat * g + bb
        sy = _sigmoid(y)
        dy = dcp_v * (sy * (1.0 + y * (1.0 - sy)))
        dxh = dy * g
        dhc = rstd * (dxh - jnp.mean(dxh, axis=-1, keepdims=True) - xhat * jnp.mean(dxh * xhat, axis=-1, keepdims=True))
        return dhc, dy * xhat, dy

    def body(dcp_ref, dcpn_ref, hc_ref, hcn_ref, a_ref, gate_ref, w_ref, g_ref, bb_ref,
             du_ref, dw_ref, db_ref, dlg_ref, dlb_ref, dhext, glu_scr, dhsh, acc_ref):
        b = pl.program_id(0)
        s = pl.program_id(1)

        @pl.when((b == 0) & (s == 0))
        def _():
            dw_ref[...] = jnp.zeros_like(dw_ref)
            db_ref[...] = jnp.zeros_like(db_ref)
            dlg_ref[...] = jnp.zeros_like(dlg_ref)
            dlb_ref[...] = jnp.zeros_like(dlb_ref)

        dhc, dgt, dbt = ln_bwd(dcp_ref[...].astype(F32), hc_ref[...], g_ref[...], bb_ref[...])
        dhn, _, _ = ln_bwd(dcpn_ref[...].astype(F32), hcn_ref[...], g_ref[...], bb_ref[...])
        dhext[0:ts, :] = dhc
        dhext[ts:, :] = jnp.where(s < nS - 1, dhn, 0.0)
        db_ref[...] += jnp.sum(dhc, axis=0, keepdims=True)
        dlg_ref[...] += jnp.sum(dgt, axis=0, keepdims=True)
        dlb_ref[...] += jnp.sum(dbt, axis=0, keepdims=True)

        sg = _sigmoid(gate_ref[...].astype(F32))
        av = a_ref[...].astype(F32)
        glu_scr[...] = av * sg
        _fill_shifted(dhext, dhsh)

        def colblock(cb, carry):
            cols = pl.ds(pl.multiple_of(cb * 128, 128), 128)
            outs = _conv_taps(w_ref, dhext, dhsh, 0, ts, cb, reverse=True)
            for rb, acc in enumerate(outs):
                acc_ref[rb * 128:(rb + 1) * 128, cols] = acc
            taps = [jnp.zeros((8, 128), F32) for _ in range(CONV_WIDTH)]
            for r0 in range(0, ts, 128):
                glu = [glu_scr[r0 + 8 * i:r0 + 8 * (i + 1), cols] for i in range(16)]
                for s8 in range(8):
                    users = {}
                    for j in range(CONV_WIDTH):
                        if (CONV_WIDTH - 1 - j) % 8 == s8:
                            for i in range(16):
                                users.setdefault(CONV_WIDTH - 1 - j + r0 + 8 * i - s8, []).append((j, i))
                    for row, pairs in sorted(users.items()):
                        dhc8 = dhext[row:row + 8, cols] if s8 == 0 else dhsh[s8 - 1, row:row + 8, cols]
                        for j, i in pairs:
                            taps[j] = taps[j] + glu[i] * dhc8
            for j in range(CONV_WIDTH):
                dw_ref[8 * j:8 * (j + 1), cols] += taps[j]
            return carry

        lax.fori_loop(0, C_CONV // 128, colblock, 0)
        dglu = acc_ref[...]
        du_ref[:, :C_CONV] = (dglu * sg).astype(BF16)
        du_ref[:, C_CONV:] = (dglu * av * (sg * (1.0 - sg))).astype(BF16)

    cur = lambda cb: pl.BlockSpec((ts, C_CONV), lambda b, s: (b * nS + s, cb))
    nxt = pl.BlockSpec((HALO, C_CONV), lambda b, s: (jnp.minimum(b * (S // HALO) + (s + 1) * per32, last32), 0))
    vec = _full((1, C_CONV))
    return pl.pallas_call(
        body, name="conv_bwd", grid=(B, nS),
        in_specs=[cur(0), nxt, cur(0), nxt, cur(0), cur(1), _full((CONV_WIDTH, C_CONV)), vec, vec],
        out_specs=[pl.BlockSpec((ts, 2 * C_CONV), lambda b, s: (b * nS + s, 0)), _full((8 * CONV_WIDTH, C_CONV)), vec, vec, vec],
        out_shape=[jax.ShapeDtypeStruct((T, 2 * C_CONV), BF16), jax.ShapeDtypeStruct((8 * CONV_WIDTH, C_CONV), F32),
                   jax.ShapeDtypeStruct((1, C_CONV), F32), jax.ShapeDtypeStruct((1, C_CONV), F32),
                   jax.ShapeDtypeStruct((1, C_CONV), F32)],
        scratch_shapes=[pltpu.VMEM((ts + HALO, C_CONV), F32), pltpu.VMEM((ts, C_CONV), F32),
                        pltpu.VMEM((7, ts + HALO, C_CONV), F32), pltpu.VMEM((ts, C_CONV), F32)],
        compiler_params=_cp("arbitrary", "arbitrary"),
    )(dcp, dcp, hc, hc, u, u, conv_w, ln_g, ln_b)


def _attn_bwd_call(u, do, qg2, kg2, sinks, slopes, B, S):
    T = B * S
    nb = S // BLOCK
    half = Q_W // 2
    npair = GROUP // 2

    def body(sinks_ref, slopes_ref, q_ref, k_ref, v_ref, do_ref, qg_ref, kg_ref,
             dq_ref, dkv_ref, dqg_ref, dkg_ref, dsk_ref, kA, kB, vA, vB, dkn_acc, dv_acc, dkv_keep,
             btab, s_buf, dp_buf, p_buf, ds_buf, qn_buf, do_buf, rq_buf):
        b = pl.program_id(0)
        kvh = pl.program_id(1)
        lo = _lane_lo()
        lane = lax.broadcasted_iota(jnp.int32, (1, 128), 1)
        first = kvh == 0

        @pl.when((b == 0) & first)
        def _():
            dqg_ref[...] = jnp.zeros_like(dqg_ref)
            dkg_ref[...] = jnp.zeros_like(dkg_ref)
            dsk_ref[...] = jnp.zeros_like(dsk_ref)

        _stage_kv(kvh, k_ref, v_ref, kg_ref, lo, kA, kB, vA, vB)
        dkn_acc[...] = jnp.zeros_like(dkn_acc)
        dv_acc[...] = jnp.zeros_like(dv_acc)
        _fill_bias(btab, kvh, slopes_ref)
        qgs = qg_ref[...] * ATTN_SCALE

        def fold(acc):
            canon = jnp.where(lo, acc[0:2 * BLOCK] + pltpu.roll(acc[2 * BLOCK:], HEAD_DIM, 1), 0.0)
            return jnp.where(first, canon, pltpu.roll(canon, HEAD_DIM, 1))

        def block(n, slot):
            s_scr, dp_scr, p_scr, ds_scr = s_buf.at[slot], dp_buf.at[slot], p_buf.at[slot], ds_buf.at[slot]
            qn_scr, do_scr, rq_scr = qn_buf.at[slot], do_buf.at[slot], rq_buf.at[slot]
            r0 = pl.multiple_of(n * BLOCK, BLOCK)
            win = pl.ds(r0, 2 * BLOCK)
            rows = pl.ds(r0, BLOCK)
            k2t = jnp.concatenate([kA[win, :], kB[win, :]], axis=0)
            v2 = jnp.concatenate([vA[win, :], vB[win, :]], axis=0)
            var = jnp.minimum(n, 1)
            for p in range(npair):
                pc = slice(128 * p, 128 * (p + 1))
                qp = q_ref[rows, pc].astype(F32)
                rq = _head_rstd(qp, lo)
                rq_scr[pc, :] = rq
                qn_scr[pc, :] = (qp * rq * qgs).astype(BF16)
                do_scr[pc, :] = do_ref[rows, pc]
            s_scr[...] = _dot_nt(k2t, qn_scr[...])
            dp_scr[...] = _dot_nt(v2, do_scr[...])
            for p in range(npair):
                for hf in range(2):
                    g = 2 * p + hf
                    keys, qs = slice(2 * BLOCK * hf, 2 * BLOCK * (hf + 1)), slice(128 * p, 128 * (p + 1))
                    pn, esk, rz = _softmax_cols(s_scr, btab, p, var, g, hf, sinks_ref[kvh * GROUP + g])
                    dph = dp_scr[keys, qs]
                    delta = jnp.sum(pn * dph, axis=0, keepdims=True)
                    ds_scr[keys, qs] = (pn * (dph - delta)).astype(BF16)
                    p_scr[keys, qs] = pn.astype(BF16)
                    dsk_ref[...] += jnp.where(lane == kvh * GROUP + g, -jnp.sum(esk * rz * delta), 0.0)
            dqs_all = _dot_tn(k2t, ds_scr[...]).T
            for p in range(npair):
                pc = slice(128 * p, 128 * (p + 1))
                qp = q_ref[rows, pc].astype(F32)
                rq = rq_scr[pc, :]
                dqs = dqs_all[pc, :]
                t = dqs * qgs
                dq = rq * t - qp * (rq * rq * rq * (_half_sums(t * qp, lo) * (1.0 / HEAD_DIM)))
                dq_ref[rows, pc] = dq.astype(BF16)
                dqg_ref[...] += jnp.sum(dqs * qp * rq, axis=0, keepdims=True) * ATTN_SCALE
            dkn_acc[win, :] += fold(_dot(ds_scr[...], qn_scr[...]))
            dv_acc[win, :] += fold(_dot(p_scr[...], do_scr[...]))

        def four_blocks(i, carry):
            for slot in range(4):
                block(4 * i + slot, slot)
            return carry

        lax.fori_loop(0, nb // 4, four_blocks, 0)

        dkn = dkn_acc[BLOCK:, :]
        kv = k_ref[...].astype(F32)
        kg = kg_ref[...]
        rk = _head_rstd(kv, lo)
        t = dkn * kg
        dk = rk * t - kv * (rk * rk * rk * (_half_sums(t * kv, lo) * (1.0 / HEAD_DIM)))
        dkg_ref[...] += jnp.sum(dkn * kv * rk, axis=0, keepdims=True)
        dv = dv_acc[BLOCK:, :]

        @pl.when(first)
        def _():
            dkv_keep[:, :128] = dk
            dkv_keep[:, 128:] = dv

        @pl.when(kvh == N_KV_HEADS - 1)
        def _():
            dkv_ref[:, :128] = (dkv_keep[:, :128] + dk).astype(BF16)
            dkv_ref[:, 128:] = (dkv_keep[:, 128:] + dv).astype(BF16)

    smem = pl.BlockSpec(memory_space=pltpu.SMEM)
    kvbuf = pltpu.VMEM((S + BLOCK, 128), BF16)
    accbuf = pltpu.VMEM((S + BLOCK, 128), F32)
    vec = _full((1, 128))
    return pl.pallas_call(
        body, name="attn_bwd", grid=(B, N_KV_HEADS),
        in_specs=[smem, smem,
                  pl.BlockSpec((S, half), lambda b, h: (b, COL_Q // half + h)),
                  pl.BlockSpec((S, 128), lambda b, h: (b, COL_K // 128)),
                  pl.BlockSpec((S, 128), lambda b, h: (b, COL_V // 128)),
                  pl.BlockSpec((S, half), lambda b, h: (b, h)),
                  vec, vec],
        out_specs=[pl.BlockSpec((S, half), lambda b, h: (b, h)), pl.BlockSpec((S, 256), lambda b, h: (b, 0)), vec, vec, vec],
        out_shape=[jax.ShapeDtypeStruct((T, Q_W), BF16), jax.ShapeDtypeStruct((T, 2 * KV_W), BF16),
                   jax.ShapeDtypeStruct((1, 128), F32), jax.ShapeDtypeStruct((1, 128), F32), jax.ShapeDtypeStruct((1, 128), F32)],
        scratch_shapes=[kvbuf, kvbuf, kvbuf, kvbuf, accbuf, accbuf, pltpu.VMEM((S, 256), F32),
                        pltpu.VMEM((2, GROUP, 2 * BLOCK, BLOCK), F32),
                        pltpu.VMEM((4, npair * BLOCK, 4 * BLOCK), F32), pltpu.VMEM((4, npair * BLOCK, 4 * BLOCK), F32),
                        pltpu.VMEM((4, npair * BLOCK, 4 * BLOCK), BF16), pltpu.VMEM((4, npair * BLOCK, 4 * BLOCK), BF16),
                        pltpu.VMEM((4, npair * BLOCK, 128), BF16), pltpu.VMEM((4, npair * BLOCK, 128), BF16),
                        pltpu.VMEM((4, npair * BLOCK, 128), F32)],
        compiler_params=_cp("arbitrary", "arbitrary"),
    )(sinks, slopes, u, u, u, do, qg2, kg2)


def _inproj_bwd_call(du_conv, du_g, du_q, du_kv, w, x, g, dh, tm=512):
    T = x.shape[0]

    def body(dc_ref, dg_ref, dq_ref, dkv_ref, w_ref, x_ref, g_ref, dh_ref, gx_ref, dgm_ref):
        dxn = _dot(dc_ref[...], w_ref[REF_CONV, :])
        dxn += _dot(dg_ref[...], w_ref[REF_GATES, :])
        dxn += _dot(dq_ref[...], w_ref[REF_QKV.start:REF_QKV.start + Q_W, :])
        dxn += _dot(dkv_ref[...], w_ref[REF_QKV.start + Q_W:REF_QKV.stop, :])
        dx, dgt = _rms_bwd(dxn, x_ref[...], g_ref[...])
        gx_ref[...] = dh_ref[...] + dx

        @pl.when(pl.program_id(0) == 0)
        def _():
            dgm_ref[...] = jnp.zeros_like(dgm_ref)

        dgm_ref[...] += jnp.sum(dgt, axis=0, keepdims=True)

    row = _rows(tm, D_MODEL)
    return pl.pallas_call(
        body, name="inproj_bwd", grid=(T // tm,),
        in_specs=[_rows(tm, 2 * C_CONV), _rows(tm, 2 * D_MODEL), _rows(tm, Q_W), _rows(tm, 2 * KV_W), _full(w.shape),
                  row, _full((1, D_MODEL)), row],
        out_specs=[row, _full((1, D_MODEL))],
        out_shape=[jax.ShapeDtypeStruct((T, D_MODEL), F32), jax.ShapeDtypeStruct((1, D_MODEL), F32)],
        compiler_params=_cp("arbitrary"),
    )(du_conv, du_g, du_q, du_kv, w, x, g, dh)


def _adamw_math(w, g, m, v):
    m = ADAM_B1 * m + (1.0 - ADAM_B1) * g
    v = ADAM_B2 * v + (1.0 - ADAM_B2) * (g * g)
    m_hat = m / (1.0 - ADAM_B1 ** ADAM_STEP)
    v_hat = v / (1.0 - ADAM_B2 ** ADAM_STEP)
    delta = -ADAM_LR * (m_hat / (jnp.sqrt(v_hat) + ADAM_EPS) + ADAM_WD * w)
    return delta, m, v


def _adamw_sum_call(name, parts, own, w, m, v, tr):
    R, C = w.shape

    def body(p_ref, own_ref, w_ref, m_ref, v_ref, g_ref, d_ref, nm_ref, nv_ref):
        x, y, c = _position()
        me = 4 * x + 2 * y + c
        g = jnp.zeros((tr, C), F32)
        for s in range(N_DEV):
            g = g + jnp.where(me == s, own_ref[...], p_ref[s]).astype(F32)
        g_ref[...] = g
        d_ref[...], nm_ref[...], nv_ref[...] = _adamw_math(w_ref[...], g, m_ref[...], v_ref[...])

    blk = pl.BlockSpec((tr, C), lambda i: (i, 0))
    out = jax.ShapeDtypeStruct((R, C), F32)
    return pl.pallas_call(
        body, name=name, grid=(R // tr,),
        in_specs=[pl.BlockSpec((N_DEV, tr, C), lambda i: (0, i, 0)), blk, blk, blk, blk],
        out_specs=[blk] * 4, out_shape=[out] * 4,
        compiler_params=_cp("parallel"),
    )(parts, own, w, m, v)


def _adamw_small_call(name, g, w, m, v):
    def body(g_ref, w_ref, m_ref, v_ref, d_ref, nm_ref, nv_ref):
        d_ref[...], nm_ref[...], nv_ref[...] = _adamw_math(w_ref[...], g_ref[...], m_ref[...], v_ref[...])

    out = jax.ShapeDtypeStruct(w.shape, F32)
    return pl.pallas_call(body, name=name, out_shape=[out] * 3)(g, w, m, v)


def _position():
    return lax.axis_index("x"), lax.axis_index("y"), lax.axis_index("c")


def _peer(x, y, c, rel):
    fx, fy, fc = (rel >> 2) & 1, (rel >> 1) & 1, rel & 1
    px = 1 - x if fx else x
    py = 1 - y if fy else y
    pc = 1 - c if fc else c
    return (px, py, pc), 4 * px + 2 * py + pc


def _gather_weights_call(shards):
    n = len(shards)

    def body(*refs):
        ins, outs = refs[:n], refs[n:2 * n]
        send_sems, recv_sems, local_sems = refs[2 * n:]
        x, y, c = _position()
        me = 4 * x + 2 * y + c
        sib, sib_id = _peer(x, y, c, 1)
        chips = [_peer(x, y, c, rel) for rel in (4, 2, 6)]

        def copy(i, k, src, slot, to):
            return pltpu.make_async_remote_copy(
                src_ref=src, dst_ref=outs[i].at[slot], send_sem=send_sems.at[i, k], recv_sem=recv_sems.at[i, k],
                device_id=to, device_id_type=MESH)

        mine = [pltpu.make_async_copy(ins[i], outs[i].at[me], local_sems.at[i]) for i in range(n)]
        for cp in mine:
            cp.start()
        first = []
        for i in range(n):
            first.append(copy(i, 0, ins[i], me, sib))
            first += [copy(i, 1 + j, ins[i], me, dev) for j, (dev, _) in enumerate(chips)]
        for cp in first:
            cp.start()
        passed = []
        for i in range(n):
            for j, (dev, slot) in enumerate(chips):
                copy(i, 1 + j, ins[i], slot, dev).wait_recv()
                fwd = copy(i, 4 + j, outs[i].at[slot], slot, sib)
                fwd.start()
                passed.append(fwd)
        for i in range(n):
            copy(i, 0, ins[i], sib_id, sib).wait_recv()
            for j, (_, slot) in enumerate(chips):
                copy(i, 4 + j, ins[i], slot ^ 1, sib).wait_recv()
        for cp in first + passed:
            cp.wait_send()
        for cp in mine:
            cp.wait()

    hbm = pl.BlockSpec(memory_space=pl.ANY)
    return pl.pallas_call(
        body, name="gather_weights",
        in_specs=[hbm] * n, out_specs=[hbm] * n,
        out_shape=[jax.ShapeDtypeStruct((N_DEV,) + s.shape, s.dtype) for s in shards],
        scratch_shapes=[pltpu.SemaphoreType.DMA((n, 7)), pltpu.SemaphoreType.DMA((n, 7)), pltpu.SemaphoreType.DMA((n,))],
    )(*shards)


_HBM = pl.BlockSpec(memory_space=pltpu.HBM)
_SEM = pl.BlockSpec(memory_space=pltpu.SEMAPHORE)
_EFFECT = pltpu.SideEffectType.DATAFLOW_SIDE_EFFECTING


def _exchange_copies(scatter, incoming, src_refs, land_refs, send_sems, recv_sems):
    x, y, c = _position()
    me = 4 * x + 2 * y + c
    out = []
    for i, (src, land) in enumerate(zip(src_refs, land_refs)):
        for k in range(N_DEV - 1):
            dev, dev_id = _peer(x, y, c, k + 1)
            j = i * (N_DEV - 1) + k
            out.append(pltpu.make_async_remote_copy(
                src_ref=src.at[dev_id] if scatter else src, dst_ref=land.at[dev_id if incoming else me],
                send_sem=send_sems.at[j], recv_sem=recv_sems.at[j], device_id=dev, device_id_type=MESH))
    return out


def _exchange_start_call(name, arrays, scatter):
    n = len(arrays)
    lands = [lax.empty((N_DEV,) + (a.shape[1:] if scatter else a.shape), a.dtype) for a in arrays]

    def body(*refs):
        src_refs, land_refs = refs[:n], refs[n:2 * n]
        send_sems, recv_sems, token = refs[2 * n], refs[2 * n + 1], refs[-1]
        for copy in _exchange_copies(scatter, False, src_refs, land_refs, send_sems, recv_sems):
            copy.start()
        token[...] = jnp.zeros_like(token)

    sems = pltpu.SemaphoreType.DMA((n * (N_DEV - 1),))
    res = pl.pallas_call(
        body, name=name,
        out_shape=[sems, sems] + [pltpu.HBM(a.shape, a.dtype) for a in arrays] + [pltpu.HBM(l.shape, l.dtype) for l in lands]
        + [jax.ShapeDtypeStruct((8, 128), F32)],
        in_specs=[_HBM] * (2 * n), out_specs=[_SEM, _SEM] + [_HBM] * (2 * n) + [pl.BlockSpec(memory_space=pltpu.VMEM)],
        input_output_aliases={i: 2 + i for i in range(2 * n)},
        compiler_params=pltpu.CompilerParams(has_side_effects=_EFFECT),
    )(*[pltpu.with_memory_space_constraint(a, pltpu.HBM) for a in list(arrays) + lands])
    return res[0], res[1], res[2:2 + n], res[2 + n:2 + 2 * n], res[-1]


def _exchange_wait_call(name, started, scatter, after):
    send_sems, recv_sems, srcs, lands, _ = started
    n = len(srcs)

    def body(*refs):
        src_refs, land_refs = refs[:n], refs[n:2 * n]
        send_s, recv_s = refs[2 * n], refs[2 * n + 1]
        for copy in _exchange_copies(scatter, False, src_refs, land_refs, send_s, recv_s):
            copy.wait_send()
        for copy in _exchange_copies(scatter, True, src_refs, land_refs, send_s, recv_s):
            copy.wait_recv()

    res = pl.pallas_call(
        body, name=name,
        out_shape=[pltpu.HBM(a.shape, a.dtype) for a in list(srcs) + list(lands)],
        in_specs=[_HBM] * (2 * n) + [_SEM, _SEM, pl.BlockSpec(memory_space=pl.ANY)], out_specs=[_HBM] * (2 * n),
        input_output_aliases={i: i for i in range(2 * n)},
        compiler_params=pltpu.CompilerParams(has_side_effects=_EFFECT),
    )(*srcs, *lands, send_sems, recv_sems, after)
    return res[n:]


def _allsum_small_call(part, blocks):
    P, Q = part.shape[0], blocks.shape[1]

    def body(p_ref, b_ref, o_ref, ob_ref, all_ref, allb_ref, send_sems, recv_sems):
        x, y, c = _position()
        me = 4 * x + 2 * y + c
        peers = [_peer(x, y, c, rel) for rel in range(1, N_DEV)]
        all_ref[me] = p_ref[...]
        allb_ref[me] = b_ref[me]

        def copies(k, slot, dev, dev_id):
            sems = lambda i: dict(send_sem=send_sems.at[i], recv_sem=recv_sems.at[i], device_id=dev, device_id_type=MESH)
            return (pltpu.make_async_remote_copy(src_ref=p_ref, dst_ref=all_ref.at[slot], **sems(k)),
                    pltpu.make_async_remote_copy(src_ref=b_ref.at[dev_id], dst_ref=allb_ref.at[slot], **sems(7 + k)))

        sends = [cp for k, (dev, dev_id) in enumerate(peers) for cp in copies(k, me, dev, dev_id)]
        for cp in sends:
            cp.start()
        for k, (dev, dev_id) in enumerate(peers):
            for cp in copies(k, dev_id, dev, dev_id):
                cp.wait_recv()
        for cp in sends:
            cp.wait_send()
        tot, totb = all_ref[0], allb_ref[0]
        for s in range(1, N_DEV):
            tot, totb = tot + all_ref[s], totb + allb_ref[s]
        o_ref[...] = tot
        ob_ref[...] = totb

    vm = pl.BlockSpec(memory_space=pltpu.VMEM)
    return pl.pallas_call(
        body, name="allsum_small", in_specs=[vm, vm], out_specs=[vm, vm],
        out_shape=[jax.ShapeDtypeStruct((P, 128), F32), jax.ShapeDtypeStruct((Q, 128), F32)],
        scratch_shapes=[pltpu.VMEM((N_DEV, P, 128), F32), pltpu.VMEM((N_DEV, Q, 128), F32),
                        pltpu.SemaphoreType.DMA((14,)), pltpu.SemaphoreType.DMA((14,))],
    )(part, blocks)


def _pad_rows(v, rows):
    flat = v.reshape(-1)
    return jnp.pad(flat, (0, rows * 128 - flat.shape[0])).reshape(rows, 128)


def _local_step(x, tgt, w_in_t, conv_w, conv_b, ln_g, ln_b, qg, kg, sinks, g_mix, g_ffn, token0, rest_weights, grads_ready, B, S):
    row = lambda v: v.reshape(1, -1)
    qg2 = jnp.tile(qg, 2).reshape(1, 128)
    kg2 = jnp.tile(kg, 2).reshape(1, 128)
    heads = jnp.arange(1, N_Q_HEADS + 1, dtype=F32)
    slopes = jnp.exp2(-8.0 * heads / N_Q_HEADS)

    u, xn = _inproj_call(x, row(g_mix) + token0, w_in_t)
    hc, c = _conv_fwd_call(u, conv_w, row(conv_b), row(ln_g), row(ln_b), B, S)
    o = _attn_fwd_call(u, qg2, kg2, sinks, slopes, B, S)
    wc, wa, wm, w_ffn_in, wd = rest_weights(o)
    yc, ya, z, h, hn = _merge_call(c, o, u, x, wc, wa, wm, row(g_ffn))
    gu, act = _ffn_in_call(hn, w_ffn_in)
    sse, dout, dout_b = _ffn_out_call(act, wd, h, tgt)

    dgu = _ffn_down_bwd_call(dout_b, wd, gu)
    g_wd = _dw_call("dw_ffn_down", act, dout_b, tr=D_FF // 2)
    g_wffn_t = _dw_call("dw_ffn_in", dgu, hn, tr=2 * D_FF // 4)
    tok = grads_ready(dict(w_ffn_down=g_wd, w_ffn_in=g_wffn_t))
    dh, dh_b, g_gffn = _ffn_in_bwd_call(dgu, w_ffn_in, h, row(g_ffn) + tok, dout)
    dyc, dya, du_g, dcp, do = _merge_bwd_call(dh_b, wm, wc, wa, u, yc, ya)
    g_wm = _dw_call("dw_merge", z, dh_b, tr=D_MODEL)
    g_wc = _dw_call("dw_conv_out", c, dyc, tr=D_MODEL)
    g_wa = _dw_call("dw_attn_out", o, dya, tr=D_MODEL)
    tok = grads_ready(dict(w_merge_out=g_wm, w_conv_out=g_wc, w_attn_out=g_wa))
    du_conv, g_cw8, g_cb, g_lg, g_lb = _conv_bwd_call(dcp, hc, u, conv_w, row(ln_g) + tok, row(ln_b), B, S)
    du_q, du_kv, g_qg2, g_kg2, g_sk = _attn_bwd_call(u, do, qg2, kg2, sinks, slopes, B, S)
    g_win_t = _concat_rows_call("dw_in_rows", [
        _dw_call("dw_in_conv", du_conv, xn, tr=D_MODEL),
        _dw_call("dw_in_q", du_q, xn, tr=D_MODEL),
        _dw_call("dw_in_kv", du_kv, xn, tr=2 * KV_W),
        _dw_call("dw_in_gates", du_g, xn, tr=D_MODEL)])
    tok = grads_ready(dict(w_in=g_win_t))
    grad_x, g_gmix = _inproj_bwd_call(du_conv, du_g, du_q, du_kv, w_in_t, x, row(g_mix) + tok, dh)

    g_cw = g_cw8.reshape(CONV_WIDTH, 8, C_CONV).sum(axis=1)
    g_qg = g_qg2[0, :HEAD_DIM] + g_qg2[0, HEAD_DIM:]
    g_kg = g_kg2[0, :HEAD_DIM] + g_kg2[0, HEAD_DIM:]
    small = dict(norm_mix_g=g_gmix[0], conv_dw_b=g_cb[0], conv_ln_g=g_lg[0], conv_ln_b=g_lb[0], q_norm_g=g_qg,
                 k_norm_g=g_kg, sinks=g_sk[0, :N_Q_HEADS], norm_ffn_g=g_gffn[0], conv_dw_w=g_cw)
    return sse[0, 0], grad_x, small


_COL_SHARDED = ("w_in", "w_ffn_in")

_SMALL = (("norm_mix_g", 8), ("conv_dw_b", 8), ("conv_ln_g", 8), ("conv_ln_b", 8), ("norm_ffn_g", 8),
          ("q_norm_g", 1), ("k_norm_g", 1), ("sinks", 1))
_SMALL_ROWS = 48


def _pack_small(d):
    rows = [_pad_rows(d[name], r) for name, r in _SMALL]
    used = sum(r for _, r in _SMALL)
    return jnp.concatenate(rows + [jnp.zeros((_SMALL_ROWS - used, 128), F32)], axis=0)


def _unpack_small(packed, like):
    out, r0 = {}, 0
    for name, r in _SMALL:
        n = like[name].size
        out[name] = packed[r0:r0 + r].reshape(-1)[:n].reshape(like[name].shape)
        r0 += r
    return out


def kernel(x, norm_mix_g, w_in, conv_dw_w, conv_dw_b, conv_ln_g, conv_ln_b, w_conv_out, q_norm_g, k_norm_g, sinks, w_attn_out, w_merge_out, norm_ffn_g, w_ffn_in, w_ffn_down, loss_target, m_norm_mix_g, m_w_in, m_conv_dw_w, m_conv_dw_b, m_conv_ln_g, m_conv_ln_b, m_w_conv_out, m_q_norm_g, m_k_norm_g, m_sinks, m_w_attn_out, m_w_merge_out, m_norm_ffn_g, m_w_ffn_in, m_w_ffn_down, v_norm_mix_g, v_w_in, v_conv_dw_w, v_conv_dw_b, v_conv_ln_g, v_conv_ln_b, v_w_conv_out, v_q_norm_g, v_k_norm_g, v_sinks, v_w_attn_out, v_w_merge_out, v_norm_ffn_g, v_w_ffn_in, v_w_ffn_down):
    names = ["norm_mix_g", "w_in", "conv_dw_w", "conv_dw_b", "conv_ln_g", "conv_ln_b", "w_conv_out", "q_norm_g",
             "k_norm_g", "sinks", "w_attn_out", "w_merge_out", "norm_ffn_g", "w_ffn_in", "w_ffn_down"]
    w = dict(zip(names, (norm_mix_g, w_in, conv_dw_w, conv_dw_b, conv_ln_g, conv_ln_b, w_conv_out, q_norm_g, k_norm_g,
                         sinks, w_attn_out, w_merge_out, norm_ffn_g, w_ffn_in, w_ffn_down)))
    m = dict(zip(names, (m_norm_mix_g, m_w_in, m_conv_dw_w, m_conv_dw_b, m_conv_ln_g, m_conv_ln_b, m_w_conv_out,
                         m_q_norm_g, m_k_norm_g, m_sinks, m_w_attn_out, m_w_merge_out, m_norm_ffn_g, m_w_ffn_in,
                         m_w_ffn_down)))
    v = dict(zip(names, (v_norm_mix_g, v_w_in, v_conv_dw_w, v_conv_dw_b, v_conv_ln_g, v_conv_ln_b, v_w_conv_out,
                         v_q_norm_g, v_k_norm_g, v_sinks, v_w_attn_out, v_w_merge_out, v_norm_ffn_g, v_w_ffn_in,
                         v_w_ffn_down)))
    B, S, _ = x.shape
    T = B * S
    me = 4 * lax.axis_index("x") + 2 * lax.axis_index("y") + lax.axis_index("c")

    as_rows = lambda k, t: t.T if k in _COL_SHARDED else t
    cw_shard = _pad_rows(conv_dw_w.reshape(CONV_WIDTH, 128), 32)
    g_win, g_cwt = _gather_weights_call([w_in.astype(BF16), cw_shard])
    conv_w_full = g_cwt.reshape(N_DEV, 32 * 128)[:, :CONV_WIDTH * 128].reshape(N_DEV, CONV_WIDTH, 128)
    conv_w_full = conv_w_full.transpose(1, 0, 2).reshape(CONV_WIDTH, C_CONV)
    w_in_t = g_win.transpose(0, 2, 1).reshape(IN_COLS, D_MODEL)
    rest_names = ["w_conv_out", "w_attn_out", "w_merge_out", "w_ffn_in", "w_ffn_down"]
    rest_shards = [as_rows(k, w[k]).astype(BF16) for k in rest_names]
    rest_started = _exchange_start_call("gather_rest_start", rest_shards, scatter=False)

    def rest_weights(after):
        lands = _exchange_wait_call("gather_rest_wait", rest_started, False, after)
        mine = lax.broadcasted_iota(jnp.int32, (N_DEV, 1, 1), 0) == me
        full = [jnp.where(mine, s[None], l) for l, s in zip(lands, rest_shards)]
        return tuple(f.reshape(N_DEV * f.shape[1], D_MODEL) for f in full)

    exchanges = []

    def grads_ready(group):
        ks = list(group)
        blocks = [group[k].reshape(N_DEV, group[k].shape[0] // N_DEV, D_MODEL) for k in ks]
        started = _exchange_start_call("scatter_" + ks[0] + "_start", blocks, scatter=True)
        exchanges.append((ks, started, [lax.dynamic_index_in_dim(b, me, 0, keepdims=False) for b in blocks]))
        return started[-1][0:1, 0:1]

    sse, grad_x, small = _local_step(
        x.reshape(T, D_MODEL), loss_target.reshape(T, D_MODEL), w_in_t, conv_w_full, conv_dw_b, conv_ln_g, conv_ln_b,
        q_norm_g, k_norm_g, sinks, norm_mix_g, norm_ffn_g, rest_started[-1][0:1, 0:1], rest_weights, grads_ready, B, S)

    grad, delta, new_m, new_v = {}, {}, {}, {}
    packed = jnp.concatenate([_pack_small(small), jnp.full((8, 128), sse, F32)], axis=0)
    cw_blocks = small["conv_dw_w"].reshape(CONV_WIDTH, N_DEV, 128).transpose(1, 0, 2)
    tot, cw_tot = _allsum_small_call(packed, jnp.pad(cw_blocks, ((0, 0), (0, 32 - CONV_WIDTH), (0, 0))))
    loss = 0.5 / D_MODEL * tot[_SMALL_ROWS, 0]
    g_small = _unpack_small(tot[:_SMALL_ROWS], w)
    d_s, m_s, v_s = _adamw_small_call("adamw_small", tot[:_SMALL_ROWS], _pack_small(w), _pack_small(m), _pack_small(v))
    d_small, m_small, v_small = _unpack_small(d_s, w), _unpack_small(m_s, w), _unpack_small(v_s, w)
    g_cw = cw_tot[:CONV_WIDTH]
    cw2 = lambda t: t.reshape(CONV_WIDTH, 128)
    d_cw, m_cw, v_cw = _adamw_small_call("adamw_conv_w", g_cw, cw2(conv_dw_w), cw2(m_conv_dw_w), cw2(v_conv_dw_w))
    cw3 = lambda t: t.reshape(CONV_WIDTH, 1, 128)
    for k, _ in _SMALL:
        grad[k], delta[k], new_m[k], new_v[k] = g_small[k], d_small[k], m_small[k], v_small[k]
    grad["conv_dw_w"], delta["conv_dw_w"], new_m["conv_dw_w"], new_v["conv_dw_w"] = cw3(g_cw), cw3(d_cw), cw3(m_cw), cw3(v_cw)

    after = d_cw
    for ks, started, owns in exchanges:
        lands = _exchange_wait_call("scatter_" + ks[0] + "_wait", started, True, after)
        for k, land, own in zip(ks, lands, owns):
            tr = {672: 336, 704: 352, 352: 176, 128: 128}[own.shape[0]]
            outs = _adamw_sum_call("adamw_" + k, land, own, as_rows(k, w[k]), as_rows(k, m[k]), as_rows(k, v[k]), tr)
            grad[k], delta[k], new_m[k], new_v[k] = (as_rows(k, t) for t in outs)
            after = outs[3]

    return (loss, grad_x.reshape(B, S, D_MODEL), *[grad[k] for k in names], *[delta[k] for k in names],
            *[new_m[k] for k in names], *[new_v[k] for k in names])
```
